```python
import math
import jax, jax.numpy as jnp
from jax import lax
import numpy as np

D_MODEL = 1024
BATCH = 8
SEQ = 4096
DEPTH = 4

N_MIXERS = 2
D_FF = 4 * D_MODEL
RMS_EPS = 1e-6
NEG_INF = -1e30

GDN_HEADS = D_MODEL // 128
GDN_DK = 128
GDN_DV = 128
GDN_CONV = 5
GDN_CHUNK = 64
GDN_QKV = GDN_HEADS * (2 * GDN_DK + GDN_DV)
GDN_IN = GDN_QKV + GDN_HEADS * GDN_DV + 4 * GDN_HEADS

DSWA_CONFIGS = ((128, 1), (512, 4), (2048, 16))
DSWA_HEADS_PER_GROUP = 6
DSWA_HEAD_DIM = 64
DSWA_HEADS = len(DSWA_CONFIGS) * DSWA_HEADS_PER_GROUP
DSWA_WIDTH = DSWA_HEADS * DSWA_HEAD_DIM

REL_BUCKETS = 32
REL_MAX_DIST = 1024

N_LAYERS_A = (DEPTH + 1) // 2
N_LAYERS_B = DEPTH // 2

kernel_name = 'hybrid_gdn_dilated_swa_encoder'


def rmsnorm(x, g):
    xf = x.astype(jnp.float32)
    y = xf * lax.rsqrt(jnp.mean(xf * xf, axis=-1, keepdims=True) + RMS_EPS)
    return (y * g.astype(jnp.float32)).astype(x.dtype)


def l2norm(t):
    return t * lax.rsqrt(jnp.sum(t * t, axis=-1, keepdims=True) + 1e-6)


def rel_bucket(rel):
    nb = REL_BUCKETS // 2
    max_exact = nb // 2
    ret = jnp.where(rel > 0, nb, 0)
    n = jnp.abs(rel)
    nf = jnp.maximum(n, 1).astype(jnp.float32)
    large = max_exact + (jnp.log(nf / max_exact) / math.log(REL_MAX_DIST / max_exact)
                         * (nb - max_exact)).astype(jnp.int32)
    large = jnp.minimum(large, nb - 1)
    return ret + jnp.where(n < max_exact, n, large)


def gated_delta_chunked(q, k, v, g, beta):
    B, H, S, DK = q.shape
    DV = v.shape[-1]
    C = GDN_CHUNK
    nc = S // C
    q = q.reshape(B, H, nc, C, DK)
    k = k.reshape(B, H, nc, C, DK)
    v = v.reshape(B, H, nc, C, DV)
    g = g.reshape(B, H, nc, C)
    beta = beta.reshape(B, H, nc, C)
    G = jnp.cumsum(g, axis=-1)
    incl = jnp.tril(jnp.ones((C, C), bool))
    strict = jnp.tril(jnp.ones((C, C), bool), -1)
    diff = jnp.where(incl, G[..., :, None] - G[..., None, :], 0.0)
    decay = jnp.where(incl, jnp.exp(diff), 0.0)
    kb = k * beta[..., None]
    a = jnp.where(strict, jnp.einsum('bhnid,bhnjd->bhnij', kb, k) * decay, 0.0)
    rhs = jnp.concatenate([v * beta[..., None], kb * jnp.exp(G)[..., None]], axis=-1)
    sol = lax.linalg.triangular_solve(a, rhs, left_side=True, lower=True, unit_diagonal=True)
    u, w = sol[..., :DV], sol[..., DV:]
    intra = jnp.einsum('bhnid,bhnjd->bhnij', q, k) * decay
    g_last = G[..., -1]
    q_dec = q * jnp.exp(G)[..., None]
    k_dec = k * jnp.exp(g_last[..., None] - G)[..., None]
    xs = tuple(jnp.moveaxis(t, 2, 0) for t in (u, w, intra, q_dec, k_dec, g_last))

    def step(state, inp):
        u_c, w_c, intra_c, qd_c, kd_c, gl_c = inp
        v_new = u_c - jnp.einsum('bhck,bhkv->bhcv', w_c, state)
        o_c = (jnp.einsum('bhck,bhkv->bhcv', qd_c, state)
               + jnp.einsum('bhij,bhjv->bhiv', intra_c, v_new))
        state = (state * jnp.exp(gl_c)[..., None, None]
                 + jnp.einsum('bhck,bhcv->bhkv', kd_c, v_new))
        return state, o_c

    s0 = jnp.zeros((B, H, DK, DV), jnp.float32)
    _, o = lax.scan(step, s0, xs)
    return jnp.moveaxis(o, 0, 2).reshape(B, H, S, DV)


def gated_deltanet_bidir(h, w_in, conv_w, a_log, dt_bias, norm_w, w_out):
    B, S, _ = h.shape
    H, DK, DV = GDN_HEADS, GDN_DK, GDN_DV
    f32 = jnp.float32
    proj = h @ w_in
    qkv = proj[..., :GDN_QKV]
    z = proj[..., GDN_QKV:GDN_QKV + H * DV].reshape(B, S, H, DV)
    ab = proj[..., GDN_QKV + H * DV:]
    pad = GDN_CONV // 2
    qkv = jax.nn.silu(lax.conv_general_dilated(
        qkv, conv_w, window_strides=(1,), padding=[(pad, pad)],
        dimension_numbers=('NWC', 'WIO', 'NWC'), feature_group_count=GDN_QKV))
    qkv = qkv.astype(f32)
    q = qkv[..., :H * DK].reshape(B, S, H, DK)
    k = qkv[..., H * DK:2 * H * DK].reshape(B, S, H, DK)
    v = qkv[..., 2 * H * DK:].reshape(B, S, H, DV)
    q = l2norm(q) * (DK ** -0.5)
    k = l2norm(k)
    a = ab[..., :2 * H].reshape(B, S, 2, H).astype(f32)
    b = ab[..., 2 * H:].reshape(B, S, 2, H).astype(f32)
    g = -jnp.exp(a_log.astype(f32)) * jax.nn.softplus(a + dt_bias.astype(f32))
    beta = jax.nn.sigmoid(b)
    qh, kh, vh = (jnp.transpose(t, (0, 2, 1, 3)) for t in (q, k, v))
    gh = jnp.transpose(g, (2, 0, 3, 1))
    bh = jnp.transpose(beta, (2, 0, 3, 1))
    o_fwd = gated_delta_chunked(qh, kh, vh, gh[0], bh[0])
    flip = lambda t: jnp.flip(t, axis=2)
    o_bwd = flip(gated_delta_chunked(flip(qh), flip(kh), flip(vh), flip(gh[1]), flip(bh[1])))
    o = jnp.transpose(o_fwd + o_bwd, (0, 2, 1, 3))
    o = o * lax.rsqrt(jnp.mean(o * o, axis=-1, keepdims=True) + RMS_EPS)
    o = o * norm_w.astype(f32) * jax.nn.silu(z.astype(f32))
    return o.reshape(B, S, H * DV).astype(h.dtype) @ w_out


def banded_attention(q, k, v, bias, half):
    N, L, H, E = q.shape
    blk = half
    nb = -(-L // blk)
    lp = nb * blk
    qb = jnp.pad(q, ((0, 0), (0, lp - L), (0, 0), (0, 0))).reshape(N, nb, blk, H, E)

    def windows(t):
        tb = jnp.pad(t, ((0, 0), (blk, lp - L + blk), (0, 0), (0, 0))).reshape(N, nb + 2, blk, H, E)
        return jnp.concatenate([tb[:, :-2], tb[:, 1:-1], tb[:, 2:]], axis=2)

    kw, vw = windows(k), windows(v)
    s = jnp.einsum('nbqhe,nbkhe->nbhqk', qb, kw).astype(jnp.float32) * (E ** -0.5)
    s = s + bias[None, None]
    q_idx = jnp.arange(lp).reshape(nb, blk)
    k_idx = jnp.arange(nb)[:, None] * blk - blk + jnp.arange(3 * blk)[None, :]
    off = k_idx[:, None, :] - q_idx[:, :, None]
    valid = (jnp.abs(off) <= half) & (k_idx[:, None, :] >= 0) & (k_idx[:, None, :] < L)
    s = jnp.where(valid[None, :, None], s, NEG_INF)
    m = jnp.max(s, axis=-1, keepdims=True)
    lse = m + jnp.log(jnp.sum(jnp.exp(s - m), axis=-1, keepdims=True))
    p = jnp.exp(s - lse)
    o = jnp.einsum('nbhqk,nbkhe->nbqhe', p.astype(v.dtype), vw).reshape(N, lp, H, E)[:, :L]
    lse = jnp.transpose(lse[..., 0], (0, 1, 3, 2)).reshape(N, lp, H)[:, :L]
    return o, lse


def to_sub(t, d):
    B, S, Hh, E = t.shape
    return jnp.swapaxes(t.reshape(B, S // d, d, Hh, E), 1, 2).reshape(B * d, S // d, Hh, E)


def from_sub(t, d, B):
    L = t.shape[1]
    rest = t.shape[2:]
    return jnp.swapaxes(t.reshape(B, d, L, *rest), 1, 2).reshape(B, L * d, *rest)


def dilated_window_attention(h, w_in, w_out, rel_table):
    B, S, _ = h.shape
    Hg, E = DSWA_HEADS_PER_GROUP, DSWA_HEAD_DIM
    qkv = (h @ w_in).reshape(B, S, 3, DSWA_HEADS, E)
    q, k, v = qkv[:, :, 0], qkv[:, :, 1], qkv[:, :, 2]
    outs, lses = [], []
    for gi, (window, dil) in enumerate(DSWA_CONFIGS):
        half = window // (2 * dil)
        hs = slice(gi * Hg, (gi + 1) * Hg)
        rel = (jnp.arange(3 * half)[None, :] - half - jnp.arange(half)[:, None]) * dil
        bias = jnp.take(rel_table, rel_bucket(rel), axis=0)[..., hs]
        bias = jnp.transpose(bias, (2, 0, 1)).astype(jnp.float32)
        o, lse = banded_attention(to_sub(q[:, :, hs], dil), to_sub(k[:, :, hs], dil),
                                  to_sub(v[:, :, hs], dil), bias, half)
        outs.append(from_sub(o, dil, B))
        lses.append(from_sub(lse, dil, B))
    o = jnp.stack(outs, axis=2)
    alpha = jax.nn.softmax(jnp.stack(lses, axis=2), axis=2)
    o = (o * alpha[..., None].astype(o.dtype)).reshape(B, S, DSWA_WIDTH)
    return o @ w_out


def squared_relu_mlp(h, w1, w2):
    return jnp.square(jax.nn.relu(h @ w1)) @ w2


def _fwd_setup_inputs(seed: int = 0) -> dict:
    key = jax.random.key(seed)
    ks = jax.random.split(key, 16)
    f32 = jnp.float32
    nrm = lambda kk, shape, scale: jax.random.normal(kk, shape, f32) * scale
    x = nrm(ks[0], (BATCH, SEQ, D_MODEL), 1.0)
    norm_mix = 1.0 + nrm(ks[1], (DEPTH, D_MODEL), 0.02)
    norm_mlp = 1.0 + nrm(ks[2], (DEPTH, D_MODEL), 0.02)
    norm_final = 1.0 + nrm(ks[3], (D_MODEL,), 0.02)
    rel_bias = nrm(ks[4], (REL_BUCKETS, DSWA_HEADS), 0.2)
    gdn_w_in = nrm(ks[5], (N_LAYERS_A, D_MODEL, GDN_IN), D_MODEL ** -0.5)
    gdn_conv_w = nrm(ks[6], (N_LAYERS_A, GDN_CONV, 1, GDN_QKV), GDN_CONV ** -0.5)
    gdn_a_log = jnp.log(jax.random.uniform(ks[7], (N_LAYERS_A, 2, GDN_HEADS), f32, 1.0, 16.0))
    dt = jnp.exp(jax.random.uniform(ks[8], (N_LAYERS_A, 2, GDN_HEADS), f32,
                                    math.log(1e-3), math.log(1e-1)))
    gdn_dt_bias = dt + jnp.log(-jnp.expm1(-dt))
    gdn_norm_w = 1.0 + nrm(ks[9], (N_LAYERS_A, GDN_DV), 0.02)
    gdn_w_out = nrm(ks[10], (N_LAYERS_A, GDN_HEADS * GDN_DV, D_MODEL), (GDN_HEADS * GDN_DV) ** -0.5)
    dswa_w_in = nrm(ks[11], (N_LAYERS_B, D_MODEL, 3 * DSWA_WIDTH), D_MODEL ** -0.5)
    dswa_w_out = nrm(ks[12], (N_LAYERS_B, DSWA_WIDTH, D_MODEL), DSWA_WIDTH ** -0.5)
    mlp_w1 = nrm(ks[13], (DEPTH, D_MODEL, D_FF), D_MODEL ** -0.5)
    mlp_w2 = nrm(ks[14], (DEPTH, D_FF, D_MODEL), D_FF ** -0.5)
    return {'x': x, 'norm_mix': norm_mix, 'norm_mlp': norm_mlp, 'norm_final': norm_final,
            'rel_bias': rel_bias, 'gdn_w_in': gdn_w_in, 'gdn_conv_w': gdn_conv_w,
            'gdn_a_log': gdn_a_log, 'gdn_dt_bias': gdn_dt_bias, 'gdn_norm_w': gdn_norm_w,
            'gdn_w_out': gdn_w_out, 'dswa_w_in': dswa_w_in, 'dswa_w_out': dswa_w_out,
            'mlp_w1': mlp_w1, 'mlp_w2': mlp_w2}


def _fwd_reference(x, norm_mix, norm_mlp, norm_final, rel_bias, gdn_w_in, gdn_conv_w,
              gdn_a_log, gdn_dt_bias, gdn_norm_w, gdn_w_out, dswa_w_in, dswa_w_out,
              mlp_w1, mlp_w2):
    for i in range(DEPTH):
        h = rmsnorm(x, norm_mix[i])
        j = i // N_MIXERS
        if i % N_MIXERS == 0:
            y = gated_deltanet_bidir(h, gdn_w_in[j], gdn_conv_w[j], gdn_a_log[j],
                                     gdn_dt_bias[j], gdn_norm_w[j], gdn_w_out[j])
        else:
            y = dilated_window_attention(h, dswa_w_in[j], dswa_w_out[j], rel_bias)
        x = x + y
        h = rmsnorm(x, norm_mlp[i])
        x = x + squared_relu_mlp(h, mlp_w1[i], mlp_w2[i])
    return rmsnorm(x, norm_final)


import jax as _jax
import jax.numpy as _jnp

TWIN_FORMAT = 'train_step'
FWD_PARAMS = ['x', 'norm_mix', 'norm_mlp', 'norm_final', 'rel_bias', 'gdn_w_in', 'gdn_conv_w', 'gdn_a_log', 'gdn_dt_bias', 'gdn_norm_w', 'gdn_w_out', 'dswa_w_in', 'dswa_w_out', 'mlp_w1', 'mlp_w2']
TWIN_WEIGHTS = ['norm_mix', 'norm_mlp', 'norm_final', 'rel_bias', 'gdn_w_in', 'gdn_conv_w', 'gdn_a_log', 'gdn_dt_bias', 'gdn_norm_w', 'gdn_w_out', 'dswa_w_in', 'dswa_w_out', 'mlp_w1', 'mlp_w2']
TWIN_DIFF_INPUT = 'x'
TWIN_INPUTS = ['x', 'norm_mix', 'norm_mlp', 'norm_final', 'rel_bias', 'gdn_w_in', 'gdn_conv_w', 'gdn_a_log', 'gdn_dt_bias', 'gdn_norm_w', 'gdn_w_out', 'dswa_w_in', 'dswa_w_out', 'mlp_w1', 'mlp_w2', 'loss_target', 'm_norm_mix', 'm_norm_mlp', 'm_norm_final', 'm_rel_bias', 'm_gdn_w_in', 'm_gdn_conv_w', 'm_gdn_a_log', 'm_gdn_dt_bias', 'm_gdn_norm_w', 'm_gdn_w_out', 'm_dswa_w_in', 'm_dswa_w_out', 'm_mlp_w1', 'm_mlp_w2', 'v_norm_mix', 'v_norm_mlp', 'v_norm_final', 'v_rel_bias', 'v_gdn_w_in', 'v_gdn_conv_w', 'v_gdn_a_log', 'v_gdn_dt_bias', 'v_gdn_norm_w', 'v_gdn_w_out', 'v_dswa_w_in', 'v_dswa_w_out', 'v_mlp_w1', 'v_mlp_w2']
TWIN_OUTPUTS = ['loss', 'grad_x', 'grad_norm_mix', 'grad_norm_mlp', 'grad_norm_final', 'grad_rel_bias', 'grad_gdn_w_in', 'grad_gdn_conv_w', 'grad_gdn_a_log', 'grad_gdn_dt_bias', 'grad_gdn_norm_w', 'grad_gdn_w_out', 'grad_dswa_w_in', 'grad_dswa_w_out', 'grad_mlp_w1', 'grad_mlp_w2', 'delta_norm_mix', 'delta_norm_mlp', 'delta_norm_final', 'delta_rel_bias', 'delta_gdn_w_in', 'delta_gdn_conv_w', 'delta_gdn_a_log', 'delta_gdn_dt_bias', 'delta_gdn_norm_w', 'delta_gdn_w_out', 'delta_dswa_w_in', 'delta_dswa_w_out', 'delta_mlp_w1', 'delta_mlp_w2', 'new_m_norm_mix', 'new_m_norm_mlp', 'new_m_norm_final', 'new_m_rel_bias', 'new_m_gdn_w_in', 'new_m_gdn_conv_w', 'new_m_gdn_a_log', 'new_m_gdn_dt_bias', 'new_m_gdn_norm_w', 'new_m_gdn_w_out', 'new_m_dswa_w_in', 'new_m_dswa_w_out', 'new_m_mlp_w1', 'new_m_mlp_w2', 'new_v_norm_mix', 'new_v_norm_mlp', 'new_v_norm_final', 'new_v_rel_bias', 'new_v_gdn_w_in', 'new_v_gdn_conv_w', 'new_v_gdn_a_log', 'new_v_gdn_dt_bias', 'new_v_gdn_norm_w', 'new_v_gdn_w_out', 'new_v_dswa_w_in', 'new_v_dswa_w_out', 'new_v_mlp_w1', 'new_v_mlp_w2']
TWIN_LEAF_KINDS = {'loss': 'loss', 'grad_x': 'grad_x', 'grad_norm_mix': 'grad_w', 'grad_norm_mlp': 'grad_w', 'grad_norm_final': 'grad_w', 'grad_rel_bias': 'grad_w', 'grad_gdn_w_in': 'grad_w', 'grad_gdn_conv_w': 'grad_w', 'grad_gdn_a_log': 'grad_w', 'grad_gdn_dt_bias': 'grad_w', 'grad_gdn_norm_w': 'grad_w', 'grad_gdn_w_out': 'grad_w', 'grad_dswa_w_in': 'grad_w', 'grad_dswa_w_out': 'grad_w', 'grad_mlp_w1': 'grad_w', 'grad_mlp_w2': 'grad_w', 'delta_norm_mix': 'delta_w', 'delta_norm_mlp': 'delta_w', 'delta_norm_final': 'delta_w', 'delta_rel_bias': 'delta_w', 'delta_gdn_w_in': 'delta_w', 'delta_gdn_conv_w': 'delta_w', 'delta_gdn_a_log': 'delta_w', 'delta_gdn_dt_bias': 'delta_w', 'delta_gdn_norm_w': 'delta_w', 'delta_gdn_w_out': 'delta_w', 'delta_dswa_w_in': 'delta_w', 'delta_dswa_w_out': 'delta_w', 'delta_mlp_w1': 'delta_w', 'delta_mlp_w2': 'delta_w', 'new_m_norm_mix': 'new_m', 'new_m_norm_mlp': 'new_m', 'new_m_norm_final': 'new_m', 'new_m_rel_bias': 'new_m', 'new_m_gdn_w_in': 'new_m', 'new_m_gdn_conv_w': 'new_m', 'new_m_gdn_a_log': 'new_m', 'new_m_gdn_dt_bias': 'new_m', 'new_m_gdn_norm_w': 'new_m', 'new_m_gdn_w_out': 'new_m', 'new_m_dswa_w_in': 'new_m', 'new_m_dswa_w_out': 'new_m', 'new_m_mlp_w1': 'new_m', 'new_m_mlp_w2': 'new_m', 'new_v_norm_mix': 'new_v', 'new_v_norm_mlp': 'new_v', 'new_v_norm_final': 'new_v', 'new_v_rel_bias': 'new_v', 'new_v_gdn_w_in': 'new_v', 'new_v_gdn_conv_w': 'new_v', 'new_v_gdn_a_log': 'new_v', 'new_v_gdn_dt_bias': 'new_v', 'new_v_gdn_norm_w': 'new_v', 'new_v_gdn_w_out': 'new_v', 'new_v_dswa_w_in': 'new_v', 'new_v_dswa_w_out': 'new_v', 'new_v_mlp_w1': 'new_v', 'new_v_mlp_w2': 'new_v'}


def _forward(args):
    return _fwd_reference(*[args[k] for k in FWD_PARAMS])


def _output_shape():
    out = _jax.eval_shape(lambda: _forward(_fwd_setup_inputs(0)))
    return out.shape, out.dtype

N_MICROBATCH = 1
ADAM_LR = 0.001
ADAM_B1 = 0.9
ADAM_B2 = 0.999
ADAM_EPS = 1e-08
ADAM_WD = 0.01
ADAM_STEP = 10
PER_EXAMPLE_BATCH_AXIS = {'x': 0, 'loss_target': 0}
SHARED_INPUTS = []
_WEIGHT_DTYPES = {'norm_mix': _jnp.float32, 'norm_mlp': _jnp.float32, 'norm_final': _jnp.float32, 'rel_bias': _jnp.float32, 'gdn_w_in': _jnp.float32, 'gdn_conv_w': _jnp.float32, 'gdn_a_log': _jnp.float32, 'gdn_dt_bias': _jnp.float32, 'gdn_norm_w': _jnp.float32, 'gdn_w_out': _jnp.float32, 'dswa_w_in': _jnp.float32, 'dswa_w_out': _jnp.float32, 'mlp_w1': _jnp.float32, 'mlp_w2': _jnp.float32}
MOMENT_SCALE = {'norm_mix': 1.037448e-01, 'norm_mlp': 1.368608e-01, 'norm_final': 3.297229e+01, 'rel_bias': 1.280323e-02, 'gdn_w_in': 7.097488e-02, 'gdn_conv_w': 6.357729e-02, 'gdn_a_log': 2.315371e-01, 'gdn_dt_bias': 2.287872e-01, 'gdn_norm_w': 2.582919e-01, 'gdn_w_out': 9.415103e-02, 'dswa_w_in': 1.058573e-02, 'dswa_w_out': 1.737591e-02, 'mlp_w1': 6.743461e-02, 'mlp_w2': 1.297542e-01}


def _to_microbatches(a, axis):
    t = _jnp.moveaxis(a, axis, 0)
    t = t.reshape((N_MICROBATCH, t.shape[0] // N_MICROBATCH) + t.shape[1:])
    return _jnp.moveaxis(t, 1, axis + 1)


def setup_inputs(seed: int = 0) -> dict:
    inp = _fwd_setup_inputs(seed)
    key = _jax.random.fold_in(_jax.random.key(seed), 7919)
    shape, _ = _output_shape()
    out = dict(inp)
    out["loss_target"] = _jax.random.normal(_jax.random.fold_in(key, 0), shape, _jnp.float32)
    for i, name in enumerate(TWIN_WEIGHTS):
        w = inp[name].astype(_jnp.float32)
        if MOMENT_SCALE is None:
            s = _jnp.sqrt(_jnp.mean(_jnp.square(w)) + 1e-30)
        else:
            s = MOMENT_SCALE[name]
        km, kv = _jax.random.split(_jax.random.fold_in(key, i + 1))
        out[name] = w
        out["m_" + name] = s * _jax.random.normal(km, w.shape, _jnp.float32)
        out["v_" + name] = (s * s) * _jax.random.uniform(kv, w.shape, _jnp.float32, 0.5, 1.5)
    if N_MICROBATCH > 1:
        for name, axis in PER_EXAMPLE_BATCH_AXIS.items():
            out[name] = _to_microbatches(out[name], axis)
    return {'x': out['x'], 'norm_mix': out['norm_mix'], 'norm_mlp': out['norm_mlp'], 'norm_final': out['norm_final'], 'rel_bias': out['rel_bias'], 'gdn_w_in': out['gdn_w_in'], 'gdn_conv_w': out['gdn_conv_w'], 'gdn_a_log': out['gdn_a_log'], 'gdn_dt_bias': out['gdn_dt_bias'], 'gdn_norm_w': out['gdn_norm_w'], 'gdn_w_out': out['gdn_w_out'], 'dswa_w_in': out['dswa_w_in'], 'dswa_w_out': out['dswa_w_out'], 'mlp_w1': out['mlp_w1'], 'mlp_w2': out['mlp_w2'], 'loss_target': out['loss_target'], 'm_norm_mix': out['m_norm_mix'], 'm_norm_mlp': out['m_norm_mlp'], 'm_norm_final': out['m_norm_final'], 'm_rel_bias': out['m_rel_bias'], 'm_gdn_w_in': out['m_gdn_w_in'], 'm_gdn_conv_w': out['m_gdn_conv_w'], 'm_gdn_a_log': out['m_gdn_a_log'], 'm_gdn_dt_bias': out['m_gdn_dt_bias'], 'm_gdn_norm_w': out['m_gdn_norm_w'], 'm_gdn_w_out': out['m_gdn_w_out'], 'm_dswa_w_in': out['m_dswa_w_in'], 'm_dswa_w_out': out['m_dswa_w_out'], 'm_mlp_w1': out['m_mlp_w1'], 'm_mlp_w2': out['m_mlp_w2'], 'v_norm_mix': out['v_norm_mix'], 'v_norm_mlp': out['v_norm_mlp'], 'v_norm_final': out['v_norm_final'], 'v_rel_bias': out['v_rel_bias'], 'v_gdn_w_in': out['v_gdn_w_in'], 'v_gdn_conv_w': out['v_gdn_conv_w'], 'v_gdn_a_log': out['v_gdn_a_log'], 'v_gdn_dt_bias': out['v_gdn_dt_bias'], 'v_gdn_norm_w': out['v_gdn_norm_w'], 'v_gdn_w_out': out['v_gdn_w_out'], 'v_dswa_w_in': out['v_dswa_w_in'], 'v_dswa_w_out': out['v_dswa_w_out'], 'v_mlp_w1': out['v_mlp_w1'], 'v_mlp_w2': out['v_mlp_w2']}


def _loss(weights, diff, rest, loss_target):
    with _jax.named_scope("forward"):
        args = {**rest, TWIN_DIFF_INPUT: diff, **{k: w.astype(_WEIGHT_DTYPES[k]) for k, w in weights.items()}}
        y = _forward(args)
    with _jax.named_scope("loss_head"):
        err = _jnp.square(y.astype(_jnp.float32) - loss_target)
        return 0.5 * _jnp.sum(_jnp.mean(err, axis=-1)) if err.ndim else 0.5 * err


def _adamw(w, g, m, v):
    m = ADAM_B1 * m + (1.0 - ADAM_B1) * g
    v = ADAM_B2 * v + (1.0 - ADAM_B2) * _jnp.square(g)
    m_hat = m / (1.0 - ADAM_B1 ** ADAM_STEP)
    v_hat = v / (1.0 - ADAM_B2 ** ADAM_STEP)
    delta = -ADAM_LR * (m_hat / (_jnp.sqrt(v_hat) + ADAM_EPS) + ADAM_WD * w)
    return delta, m, v


def reference(x, norm_mix, norm_mlp, norm_final, rel_bias, gdn_w_in, gdn_conv_w, gdn_a_log, gdn_dt_bias, gdn_norm_w, gdn_w_out, dswa_w_in, dswa_w_out, mlp_w1, mlp_w2, loss_target, m_norm_mix, m_norm_mlp, m_norm_final, m_rel_bias, m_gdn_w_in, m_gdn_conv_w, m_gdn_a_log, m_gdn_dt_bias, m_gdn_norm_w, m_gdn_w_out, m_dswa_w_in, m_dswa_w_out, m_mlp_w1, m_mlp_w2, v_norm_mix, v_norm_mlp, v_norm_final, v_rel_bias, v_gdn_w_in, v_gdn_conv_w, v_gdn_a_log, v_gdn_dt_bias, v_gdn_norm_w, v_gdn_w_out, v_dswa_w_in, v_dswa_w_out, v_mlp_w1, v_mlp_w2):
    given = dict(x=x, norm_mix=norm_mix, norm_mlp=norm_mlp, norm_final=norm_final, rel_bias=rel_bias, gdn_w_in=gdn_w_in, gdn_conv_w=gdn_conv_w, gdn_a_log=gdn_a_log, gdn_dt_bias=gdn_dt_bias, gdn_norm_w=gdn_norm_w, gdn_w_out=gdn_w_out, dswa_w_in=dswa_w_in, dswa_w_out=dswa_w_out, mlp_w1=mlp_w1, mlp_w2=mlp_w2, loss_target=loss_target, m_norm_mix=m_norm_mix, m_norm_mlp=m_norm_mlp, m_norm_final=m_norm_final, m_rel_bias=m_rel_bias, m_gdn_w_in=m_gdn_w_in, m_gdn_conv_w=m_gdn_conv_w, m_gdn_a_log=m_gdn_a_log, m_gdn_dt_bias=m_gdn_dt_bias, m_gdn_norm_w=m_gdn_norm_w, m_gdn_w_out=m_gdn_w_out, m_dswa_w_in=m_dswa_w_in, m_dswa_w_out=m_dswa_w_out, m_mlp_w1=m_mlp_w1, m_mlp_w2=m_mlp_w2, v_norm_mix=v_norm_mix, v_norm_mlp=v_norm_mlp, v_norm_final=v_norm_final, v_rel_bias=v_rel_bias, v_gdn_w_in=v_gdn_w_in, v_gdn_conv_w=v_gdn_conv_w, v_gdn_a_log=v_gdn_a_log, v_gdn_dt_bias=v_gdn_dt_bias, v_gdn_norm_w=v_gdn_norm_w, v_gdn_w_out=v_gdn_w_out, v_dswa_w_in=v_dswa_w_in, v_dswa_w_out=v_dswa_w_out, v_mlp_w1=v_mlp_w1, v_mlp_w2=v_mlp_w2)
    weights = {n: given[n] for n in TWIN_WEIGHTS}
    shared = {n: given[n] for n in SHARED_INPUTS}
    per_example = {n: given[n] for n in ['x']}
    grad_fn = _jax.value_and_grad(_loss, argnums=(0, 1))

    def one_microbatch(ex, loss_target):
        ex = dict(ex)
        diff = ex.pop(TWIN_DIFF_INPUT)
        return grad_fn(weights, diff, {**shared, **ex}, loss_target)

    if N_MICROBATCH == 1:
        loss, (grad_w, grad_x) = one_microbatch(per_example, given["loss_target"])
    else:
        def body(carry, xs):
            loss_sum, grad_sum = carry
            l_k, (gw_k, gx_k) = one_microbatch(xs[0], xs[1])
            with _jax.named_scope("update"):
                return (loss_sum + l_k, _jax.tree.map(_jnp.add, grad_sum, gw_k)), gx_k

        init = (_jnp.zeros((), _jnp.float32), _jax.tree.map(_jnp.zeros_like, weights))
        (loss, grad_w), grad_x = _jax.lax.scan(body, init, (per_example, given["loss_target"]))
    with _jax.named_scope("update"):
        delta_w, new_m, new_v = {}, {}, {}
        for n in TWIN_WEIGHTS:
            delta_w[n], new_m[n], new_v[n] = _adamw(weights[n], grad_w[n], given["m_" + n], given["v_" + n])
    return (loss, grad_x, *[grad_w[n] for n in TWIN_WEIGHTS], *[delta_w[n] for n in TWIN_WEIGHTS],
            *[new_m[n] for n in TWIN_WEIGHTS], *[new_v[n] for n in TWIN_WEIGHTS])
```

```python
import functools
import math

import jax
import jax.numpy as jnp
import numpy as np
from jax import lax
from jax.experimental import pallas as pl
from jax.experimental.pallas import tpu as pltpu

F32 = jnp.float32
BF16 = jnp.bfloat16
HP = lax.Precision.HIGHEST

N_DEV = 8
D_MODEL = 1024
DEPTH = 4
RMS_EPS = 1e-6
NEG_INF = -1e30

GDN_HEADS = 8
GDN_DK = 128
GDN_CONV = 5
GDN_CHUNK = 64
GDN_QKV = 3 * GDN_HEADS * GDN_DK
GDN_MAIN = GDN_QKV + GDN_HEADS * GDN_DK
GDN_AB = 4 * GDN_HEADS

DSWA_DILS = (1, 4, 16)
DSWA_HG = 6
DSWA_E = 64
DSWA_HEADS = 18
DSWA_WIDTH = DSWA_HEADS * DSWA_E
DSWA_HALF = 64
REL_BUCKETS = 32
REL_MAX_DIST = 1024

ADAM_LR = 0.001
ADAM_B1 = 0.9
ADAM_B2 = 0.999
ADAM_EPS = 1e-08
ADAM_WD = 0.01
ADAM_STEP = 10

VMEM_LIMIT = 56 * 1024 * 1024


def _cparams(sem=None, **kw):
    return pltpu.CompilerParams(dimension_semantics=sem, vmem_limit_bytes=VMEM_LIMIT, **kw)


def _pick(dim, cands):
    for c in cands:
        if dim % c == 0:
            return c
    return dim


def _bdot(a, b):
    return jnp.dot(a.astype(BF16), b.astype(BF16), preferred_element_type=F32)


def _bdot_nt(a, b):
    return lax.dot_general(a.astype(BF16), b.astype(BF16), (((1,), (1,)), ((), ())),
                           preferred_element_type=F32)


def _bdot_tn(a, b):
    return lax.dot_general(a.astype(BF16), b.astype(BF16), (((0,), (0,)), ((), ())),
                           preferred_element_type=F32)


def _hdot(a, b):
    return jnp.dot(a, b, precision=HP, preferred_element_type=F32)


def _hdot_tn(a, b):
    return lax.dot_general(a, b, (((0,), (0,)), ((), ())), precision=HP, preferred_element_type=F32)


def _hdot_nt(a, b):
    return lax.dot_general(a, b, (((1,), (1,)), ((), ())), precision=HP, preferred_element_type=F32)


def _sigmoid(x):
    return 1.0 / (1.0 + jnp.exp(-x))


def _mm(a, b, *, name, ta=False, tb=False, out_dtypes=(F32,), epilogue=None, extras=(),
        tm=None, tn=None, tk=None):
    if ta:
        kdim, m = a.shape
    else:
        m, kdim = a.shape
    n = b.shape[0] if tb else b.shape[1]
    tm = tm or _pick(m, (512, 256, 128))
    tn = tn or _pick(n, (512, 384, 256, 128))
    tk = tk or _pick(kdim, (1024, 512, 384, 256, 128))
    nk = kdim // tk
    n_out = len(out_dtypes)
    n_ex = len(extras)

    def body(*refs):
        a_ref, b_ref = refs[0], refs[1]
        ex_refs = refs[2:2 + n_ex]
        out_refs = refs[2 + n_ex:2 + n_ex + n_out]
        acc_ref = refs[-1]
        k = pl.program_id(2)

        @pl.when(k == 0)
        def _():
            acc_ref[...] = jnp.zeros_like(acc_ref)

        av = a_ref[...].astype(BF16)
        bv = b_ref[...].astype(BF16)
        dims = (((0 if ta else 1,), (1 if tb else 0,)), ((), ()))
        acc_ref[...] += lax.dot_general(av, bv, dims, preferred_element_type=F32)

        @pl.when(k == nk - 1)
        def _():
            acc = acc_ref[...]
            outs = (acc,) if epilogue is None else epilogue(acc, *[r[...] for r in ex_refs])
            for r, o in zip(out_refs, outs):
                r[...] = o.astype(r.dtype)

    a_spec = pl.BlockSpec((tk, tm), lambda i, j, k: (k, i)) if ta else pl.BlockSpec((tm, tk), lambda i, j, k: (i, k))
    b_spec = pl.BlockSpec((tn, tk), lambda i, j, k: (j, k)) if tb else pl.BlockSpec((tk, tn), lambda i, j, k: (k, j))
    o_spec = pl.BlockSpec((tm, tn), lambda i, j, k: (i, j))
    outs = pl.pallas_call(
        body, name=name,
        grid=(m // tm, n // tn, nk),
        in_specs=[a_spec, b_spec] + [o_spec] * n_ex,
        out_specs=[o_spec] * n_out,
        out_shape=[jax.ShapeDtypeStruct((m, n), dt) for dt in out_dtypes],
        scratch_shapes=[pltpu.VMEM((tm, tn), F32)],
        compiler_params=_cparams(("parallel", "parallel", "arbitrary")),
    )(a, b, *extras)
    return outs[0] if n_out == 1 else outs


def _rms_fwd(x, g, *, name):
    s, d = x.shape
    tr = _pick(s, (512, 256, 128))

    def body(x_ref, g_ref, h_ref):
        xv = x_ref[...]
        r = lax.rsqrt(jnp.mean(xv * xv, axis=-1, keepdims=True) + RMS_EPS)
        h_ref[...] = (xv * r * g_ref[...]).astype(h_ref.dtype)

    return pl.pallas_call(
        body, name=name, grid=(s // tr,),
        in_specs=[pl.BlockSpec((tr, d), lambda i: (i, 0)), pl.BlockSpec((1, d), lambda i: (0, 0))],
        out_specs=pl.BlockSpec((tr, d), lambda i: (i, 0)),
        out_shape=jax.ShapeDtypeStruct((s, d), BF16),
        compiler_params=_cparams(("parallel",)),
    )(x, g.reshape(1, d))


def _rms_bwd(x, g, dh, dres, *, name):
    s, d = x.shape
    tr = _pick(s, (512, 256, 128))

    def body(x_ref, g_ref, dh_ref, dres_ref, dx_ref, dg_ref):
        i = pl.program_id(0)
        xv = x_ref[...]
        r = lax.rsqrt(jnp.mean(xv * xv, axis=-1, keepdims=True) + RMS_EPS)
        xn = xv * r
        dhv = dh_ref[...]
        dn = dhv * g_ref[...]
        dx_ref[...] = dres_ref[...] + r * (dn - xn * jnp.mean(dn * xn, axis=-1, keepdims=True))
        part = jnp.sum(dhv * xn, axis=0, keepdims=True)

        @pl.when(i == 0)
        def _():
            dg_ref[...] = part

        @pl.when(i > 0)
        def _():
            dg_ref[...] += part

    row = pl.BlockSpec((tr, d), lambda i: (i, 0))
    vec = pl.BlockSpec((1, d), lambda i: (0, 0))
    return pl.pallas_call(
        body, name=name, grid=(s // tr,),
        in_specs=[row, vec, row, row], out_specs=[row, vec],
        out_shape=[jax.ShapeDtypeStruct((s, d), F32), jax.ShapeDtypeStruct((1, d), F32)],
        compiler_params=_cparams(("arbitrary",)),
    )(x, g.reshape(1, d), dh, dres)


def _loss_head(x, g, target, *, name):
    s, d = x.shape
    tr = _pick(s, (512, 256, 128))

    def body(x_ref, g_ref, t_ref, loss_ref, dx_ref, dg_ref):
        i = pl.program_id(0)
        xv = x_ref[...]
        gv = g_ref[...]
        r = lax.rsqrt(jnp.mean(xv * xv, axis=-1, keepdims=True) + RMS_EPS)
        xn = xv * r
        err = xn * gv - t_ref[...]
        lpart = 0.5 * jnp.sum(jnp.mean(err * err, axis=-1, keepdims=True), axis=0, keepdims=True)
        dy = err * (1.0 / d)
        dn = dy * gv
        dx_ref[...] = r * (dn - xn * jnp.mean(dn * xn, axis=-1, keepdims=True))
        gpart = jnp.sum(dy * xn, axis=0, keepdims=True)

        @pl.when(i == 0)
        def _():
            dg_ref[...] = gpart
            loss_ref[...] = lpart

        @pl.when(i > 0)
        def _():
            dg_ref[...] += gpart
            loss_ref[...] += lpart

    row = pl.BlockSpec((tr, d), lambda i: (i, 0))
    vec = pl.BlockSpec((1, d), lambda i: (0, 0))
    one = pl.BlockSpec((1, 1), lambda i: (0, 0))
    return pl.pallas_call(
        body, name=name, grid=(s // tr,),
        in_specs=[row, vec, row], out_specs=[one, row, vec],
        out_shape=[jax.ShapeDtypeStruct((1, 1), F32), jax.ShapeDtypeStruct((s, d), F32),
                   jax.ShapeDtypeStruct((1, d), F32)],
        compiler_params=_cparams(("arbitrary",)),
    )(x, g.reshape(1, d), target)


def _shift_rows(x, sft, rows):
    s = x.shape[0]
    if sft == 0:
        return x
    y = pltpu.roll(x, (-sft) % s, 0)
    ok = (rows + sft >= 0) & (rows + sft < s)
    return jnp.where(ok, y, 0.0)


def _gdn_pre_fwd(proj, conv_w, *, name):
    s = proj.shape[0]
    nblk = GDN_QKV // 128
    pad = GDN_CONV // 2

    def body(x_ref, w_ref, o_ref):
        j = pl.program_id(0)
        x = x_ref[...]
        rows = lax.broadcasted_iota(jnp.int32, x.shape, 0)
        c = jnp.zeros_like(x)
        for t in range(GDN_CONV):
            c = c + w_ref[pl.ds(t, 1), :] * _shift_rows(x, t - pad, rows)
        a = c * _sigmoid(c)
        rinv = lax.rsqrt(jnp.sum(a * a, axis=-1, keepdims=True) + 1e-6)
        scale = jnp.where(j < GDN_HEADS, GDN_DK ** -0.5, 1.0)
        o_ref[...] = jnp.where(j >= 2 * GDN_HEADS, a, a * (rinv * scale))

    return pl.pallas_call(
        body, name=name, grid=(nblk,),
        in_specs=[pl.BlockSpec((s, 128), lambda j: (0, j)), pl.BlockSpec((GDN_CONV, 128), lambda j: (0, j))],
        out_specs=pl.BlockSpec((s, 128), lambda j: (0, j)),
        out_shape=jax.ShapeDtypeStruct((s, GDN_QKV), F32),
        compiler_params=_cparams(("parallel",)),
    )(proj, conv_w)


def _gdn_pre_bwd(proj, conv_w, dqkv, *, name):
    s = proj.shape[0]
    nblk = GDN_QKV // 128
    pad = GDN_CONV // 2

    def body(x_ref, w_ref, d_ref, dx_ref, dw_ref):
        j = pl.program_id(0)
        x = x_ref[...]
        rows = lax.broadcasted_iota(jnp.int32, x.shape, 0)
        xs = [_shift_rows(x, t - pad, rows) for t in range(GDN_CONV)]
        c = jnp.zeros_like(x)
        for t in range(GDN_CONV):
            c = c + w_ref[pl.ds(t, 1), :] * xs[t]
        sg = _sigmoid(c)
        a = c * sg
        rinv = lax.rsqrt(jnp.sum(a * a, axis=-1, keepdims=True) + 1e-6)
        scale = jnp.where(j < GDN_HEADS, GDN_DK ** -0.5, 1.0)
        dy = d_ref[...]
        nh = a * rinv
        da_n = (rinv * scale) * (dy - nh * jnp.sum(dy * nh, axis=-1, keepdims=True))
        da = jnp.where(j >= 2 * GDN_HEADS, dy, da_n)
        dc = da * (sg * (1.0 + c * (1.0 - sg)))
        dx = jnp.zeros_like(x)
        for t in range(GDN_CONV):
            dx = dx + w_ref[pl.ds(t, 1), :] * _shift_rows(dc, pad - t, rows)
            dw_ref[pl.ds(t, 1), :] = jnp.sum(dc * xs[t], axis=0, keepdims=True)
        dx_ref[...] = dx.astype(dx_ref.dtype)

    col = pl.BlockSpec((s, 128), lambda j: (0, j))
    wsp = pl.BlockSpec((GDN_CONV, 128), lambda j: (0, j))
    return pl.pallas_call(
        body, name=name, grid=(nblk,),
        in_specs=[col, wsp, col], out_specs=[col, wsp],
        out_shape=[jax.ShapeDtypeStruct((s, GDN_QKV), BF16), jax.ShapeDtypeStruct((GDN_CONV, GDN_QKV), F32)],
        compiler_params=_cparams(("parallel",)),
    )(proj, conv_w, dqkv)


def _softplus(x):
    return jnp.maximum(x, 0.0) + jnp.log(1.0 + jnp.exp(-jnp.abs(x)))


def _gdn_gate_fwd(a, b, a_log, dt_bias, *, name):
    s = a.shape[0]
    nh = 2 * GDN_HEADS

    def body(a_ref, b_ref, al_ref, dt_ref, g_ref, be_ref):
        g_ref[...] = -jnp.exp(al_ref[...]) * _softplus(a_ref[...] + dt_ref[...])
        be_ref[...] = _sigmoid(b_ref[...])

    return pl.pallas_call(
        body, name=name,
        out_shape=[jax.ShapeDtypeStruct((s, nh), F32), jax.ShapeDtypeStruct((s, nh), F32)],
        compiler_params=_cparams(),
    )(a, b, a_log.reshape(1, nh), dt_bias.reshape(1, nh))


def _gdn_gate_bwd(a, b, a_log, dt_bias, dg, dbeta, *, name):
    s = a.shape[0]
    nh = 2 * GDN_HEADS

    def body(a_ref, b_ref, al_ref, dt_ref, dg_ref, db_ref, da_ref, dbb_ref, dal_ref, ddt_ref):
        ea = jnp.exp(al_ref[...])
        z = a_ref[...] + dt_ref[...]
        dgv = dg_ref[...]
        dz = dgv * (-ea) * _sigmoid(z)
        dal_ref[...] = jnp.sum(dgv * (-ea) * _softplus(z), axis=0, keepdims=True)
        ddt_ref[...] = jnp.sum(dz, axis=0, keepdims=True)
        sb = _sigmoid(b_ref[...])
        da_ref[...] = dz
        dbb_ref[...] = db_ref[...] * sb * (1.0 - sb)

    return pl.pallas_call(
        body, name=name,
        out_shape=[jax.ShapeDtypeStruct((s, nh), F32), jax.ShapeDtypeStruct((s, nh), F32),
                   jax.ShapeDtypeStruct((1, nh), F32), jax.ShapeDtypeStruct((1, nh), F32)],
        compiler_params=_cparams(),
    )(a, b, a_log.reshape(1, nh), dt_bias.reshape(1, nh), dg, dbeta)


def _chunk_masks(d):
    c = GDN_CHUNK
    ii = lax.broadcasted_iota(jnp.int32, (c, c), 0)
    jj = lax.broadcasted_iota(jnp.int32, (c, c), 1)
    dif = (ii - jj) * jnp.where(d == 0, 1, -1)
    mi = dif >= 0
    mit = dif <= 0
    ms = dif > 0
    eye = ii == jj
    bds = [(ii >> sh) == (jj >> sh) for sh in (3, 4, 5)]
    return dict(mi=mi, mit=mit, ms=ms, eye=eye, bds=bds,
                mif=mi.astype(F32), mitf=mit.astype(F32), eyef=eye.astype(F32))


def _tri_inv(a, mk):
    eyef = mk["eyef"]
    bd8, bd16, bd32 = mk["bds"]
    a8 = jnp.where(bd8, a, 0.0)
    a2 = _hdot(a8, a8)
    a4 = _hdot(a2, a2)
    t = _hdot(_hdot(eyef - a8, eyef + a2), eyef + a4)
    for inner, outer in ((bd8, bd16), (bd16, bd32), (bd32, None)):
        off = jnp.logical_not(inner) if outer is None else (outer & jnp.logical_not(inner))
        low = jnp.where(off, a, 0.0)
        t = t - _hdot(_hdot(t, low), t)
    return t


def _chunk_prep(q, k, v, g_row, b_row, mk):
    dv = v.shape[1]
    g_col = jnp.sum(mk["eyef"] * g_row, axis=1, keepdims=True)
    b_col = jnp.sum(mk["eyef"] * b_row, axis=1, keepdims=True)
    gc_col = jnp.sum(mk["mif"] * g_row, axis=1, keepdims=True)
    gc_row = jnp.sum(mk["mitf"] * g_col, axis=0, keepdims=True)
    gl = jnp.sum(g_row, axis=1, keepdims=True)
    decay = jnp.where(mk["mi"], jnp.exp(jnp.where(mk["mi"], gc_col - gc_row, 0.0)), 0.0)
    eg = jnp.exp(gc_col)
    e2 = jnp.exp(gl - gc_col)
    egl = jnp.exp(gl)
    kb = k * b_col
    pm = _bdot_nt(kb, k)
    a = jnp.where(mk["ms"], pm * decay, 0.0)
    t = _tri_inv(a, mk)
    sol = _hdot(t, jnp.concatenate([v * b_col, kb * eg], axis=1))
    u, w = sol[:, :dv], sol[:, dv:]
    qm = _bdot_nt(q, k)
    return dict(b_col=b_col, decay=decay, eg=eg, e2=e2, egl=egl, kb=kb, pm=pm, t=t, u=u, w=w,
                qm=qm, intra=qm * decay, qd=q * eg, kd=k * e2)


def _chunk_fwd_step(p, state):
    v_new = p["u"] - _bdot(p["w"], state)
    o = _bdot(p["qd"], state) + _bdot(p["intra"], v_new)
    new_state = state * p["egl"] + _bdot_tn(p["kd"], v_new)
    return o, new_state


def _chunk_bwd_step(q, k, v, p, mk, state, dso, do):
    dv_dim = v.shape[1]
    v_new = p["u"] - _bdot(p["w"], state)
    dvn = _bdot_tn(p["intra"], do) + _bdot(p["kd"], dso)
    dintra = _bdot_nt(do, v_new)
    dqd = _bdot_nt(do, state)
    ds = p["egl"] * dso + _bdot_tn(p["qd"], do) - _bdot_tn(p["w"], dvn)
    dkd = _bdot_nt(v_new, dso)
    dgl = jnp.sum(jnp.sum(dso * state, axis=1, keepdims=True), axis=0, keepdims=True) * p["egl"]
    dw = -_bdot_nt(dvn, state)
    drhs = _hdot_tn(p["t"], jnp.concatenate([dvn, dw], axis=1))
    dru, drw = drhs[:, :dv_dim], drhs[:, dv_dim:]
    da = -jnp.where(mk["ms"], _bdot_nt(drhs, jnp.concatenate([p["u"], p["w"]], axis=1)), 0.0)
    b_col = p["b_col"]
    dv = dru * b_col
    dbeta = jnp.sum(dru * v, axis=1, keepdims=True)
    dkb = drw * p["eg"]
    deg = jnp.sum(drw * p["kb"], axis=1, keepdims=True)
    dp = da * p["decay"]
    ddecay = da * p["pm"]
    dkb = dkb + _bdot(dp, k)
    dk = _bdot_tn(dp, p["kb"])
    dqm = dintra * p["decay"]
    ddecay = ddecay + dintra * p["qm"]
    dq = _bdot(dqm, k)
    dk = dk + _bdot_tn(dqm, q)
    dd = ddecay * p["decay"]
    dgc_col = jnp.sum(dd, axis=1, keepdims=True)
    dgc_row = -jnp.sum(dd, axis=0, keepdims=True)
    dq = dq + dqd * p["eg"]
    deg = deg + jnp.sum(dqd * q, axis=1, keepdims=True)
    dk = dk + dkd * p["e2"]
    de2 = jnp.sum(dkd * k, axis=1, keepdims=True) * p["e2"]
    dgl = dgl + jnp.sum(de2, axis=0, keepdims=True)
    dgc_col = dgc_col - de2 + deg * p["eg"]
    dk = dk + dkb * b_col
    dbeta = dbeta + jnp.sum(dkb * k, axis=1, keepdims=True)
    dgc_col = dgc_col + jnp.sum(mk["eyef"] * dgc_row, axis=1, keepdims=True)
    dg_row = jnp.sum(mk["mif"] * dgc_col, axis=0, keepdims=True) + dgl
    dbeta_row = jnp.sum(mk["eyef"] * dbeta, axis=0, keepdims=True)
    return dq, dk, dv, dg_row, dbeta_row, ds


def _gdn_chunk_fwd(qkvn, g_row, b_row, *, name):
    s = qkvn.shape[0]
    c = GDN_CHUNK
    nc = s // c
    h_, dk = GDN_HEADS, GDN_DK

    def body(q_ref, k_ref, v_ref, g_ref, b_ref, o_ref, st_ref):
        d = pl.program_id(1)
        mk = _chunk_masks(d)

        def step(n, state):
            ce = jnp.where(d == 0, n, nc - 1 - n)
            r0 = pl.multiple_of(ce * c, c)
            q = q_ref[pl.ds(r0, c), :]
            k = k_ref[pl.ds(r0, c), :]
            v = v_ref[pl.ds(r0, c), :]
            p = _chunk_prep(q, k, v, g_ref[0, pl.ds(ce, 1), :], b_ref[0, pl.ds(ce, 1), :], mk)
            st_ref[0, pl.ds(ce, 1)] = state[None]
            o, new_state = _chunk_fwd_step(p, state)

            @pl.when(d == 0)
            def _():
                o_ref[pl.ds(r0, c), :] = o

            @pl.when(d != 0)
            def _():
                o_ref[pl.ds(r0, c), :] += o

            return new_state

        lax.fori_loop(0, nc, step, jnp.zeros((dk, dk), F32))

    col = lambda off: pl.BlockSpec((s, dk), lambda h, d, off=off: (0, off + h))
    row = pl.BlockSpec((1, nc, c), lambda h, d: (d * h_ + h, 0, 0))
    return pl.pallas_call(
        body, name=name, grid=(h_, 2),
        in_specs=[col(0), col(h_), col(2 * h_), row, row],
        out_specs=[pl.BlockSpec((s, dk), lambda h, d: (0, h)),
                   pl.BlockSpec((1, nc, dk, dk), lambda h, d: (d * h_ + h, 0, 0, 0))],
        out_shape=[jax.ShapeDtypeStruct((s, h_ * dk), F32),
                   jax.ShapeDtypeStruct((2 * h_, nc, dk, dk), F32)],
        compiler_params=_cparams(("parallel", "arbitrary")),
    )(qkvn, qkvn, qkvn, g_row, b_row)


def _gdn_chunk_bwd(qkvn, g_row, b_row, states, do, *, name):
    s = qkvn.shape[0]
    c = GDN_CHUNK
    nc = s // c
    h_, dk = GDN_HEADS, GDN_DK

    def body(q_ref, k_ref, v_ref, g_ref, b_ref, st_ref, do_ref, dq_ref, dk_ref, dv_ref, dg_ref, db_ref):
        d = pl.program_id(1)
        mk = _chunk_masks(d)

        def step(i, dso):
            n = nc - 1 - i
            ce = jnp.where(d == 0, n, nc - 1 - n)
            r0 = pl.multiple_of(ce * c, c)
            q = q_ref[pl.ds(r0, c), :]
            k = k_ref[pl.ds(r0, c), :]
            v = v_ref[pl.ds(r0, c), :]
            p = _chunk_prep(q, k, v, g_ref[0, pl.ds(ce, 1), :], b_ref[0, pl.ds(ce, 1), :], mk)
            state = st_ref[0, pl.ds(ce, 1)][0]
            dq, dkk, dvv, dg_r, db_r, ds = _chunk_bwd_step(q, k, v, p, mk, state, dso, do_ref[pl.ds(r0, c), :])
            dg_ref[0, pl.ds(ce, 1), :] = dg_r
            db_ref[0, pl.ds(ce, 1), :] = db_r

            @pl.when(d == 0)
            def _():
                dq_ref[pl.ds(r0, c), :] = dq
                dk_ref[pl.ds(r0, c), :] = dkk
                dv_ref[pl.ds(r0, c), :] = dvv

            @pl.when(d != 0)
            def _():
                dq_ref[pl.ds(r0, c), :] += dq
                dk_ref[pl.ds(r0, c), :] += dkk
                dv_ref[pl.ds(r0, c), :] += dvv

            return ds

        lax.fori_loop(0, nc, step, jnp.zeros((dk, dk), F32))

    col = lambda off: pl.BlockSpec((s, dk), lambda h, d, off=off: (0, off + h))
    row = pl.BlockSpec((1, nc, c), lambda h, d: (d * h_ + h, 0, 0))
    st = pl.BlockSpec((1, nc, dk, dk), lambda h, d: (d * h_ + h, 0, 0, 0))
    return pl.pallas_call(
        body, name=name, grid=(h_, 2),
        in_specs=[col(0), col(h_), col(2 * h_), row, row, st, col(0)],
        out_specs=[col(0), col(0), col(0), row, row],
        out_shape=[jax.ShapeDtypeStruct((s, h_ * dk), F32)] * 3
        + [jax.ShapeDtypeStruct((2 * h_, nc, c), F32)] * 2,
        compiler_params=_cparams(("parallel", "arbitrary")),
    )(qkvn, qkvn, qkvn, g_row, b_row, states, do)


def _gdn_post_fwd(o, z, norm_w, *, name):
    s = o.shape[0]
    h_, dk = GDN_HEADS, GDN_DK

    def body(o_ref, z_ref, w_ref, a_ref):
        ov = o_ref[...]
        zv = z_ref[...]
        r = lax.rsqrt(jnp.mean(ov * ov, axis=-1, keepdims=True) + RMS_EPS)
        a_ref[...] = (ov * r * w_ref[...] * (zv * _sigmoid(zv))).astype(a_ref.dtype)

    return pl.pallas_call(
        body, name=name, grid=(h_,),
        in_specs=[pl.BlockSpec((s, dk), lambda h: (0, h)), pl.BlockSpec((s, dk), lambda h: (0, 3 * h_ + h)),
                  pl.BlockSpec((1, dk), lambda h: (0, 0))],
        out_specs=pl.BlockSpec((s, dk), lambda h: (0, h)),
        out_shape=jax.ShapeDtypeStruct((s, h_ * dk), BF16),
        compiler_params=_cparams(("parallel",)),
    )(o, z, norm_w.reshape(1, dk))


def _gdn_post_bwd(o, z, norm_w, dact, *, name):
    s = o.shape[0]
    h_, dk = GDN_HEADS, GDN_DK

    def body(o_ref, z_ref, w_ref, da_ref, do_ref, dz_ref, dw_ref):
        h = pl.program_id(0)
        ov = o_ref[...]
        zv = z_ref[...]
        wv = w_ref[...]
        dav = da_ref[...]
        r = lax.rsqrt(jnp.mean(ov * ov, axis=-1, keepdims=True) + RMS_EPS)
        nrm = ov * r
        sg = _sigmoid(zv)
        sz = zv * sg
        dn = dav * wv * sz
        do_ref[...] = r * (dn - nrm * jnp.mean(dn * nrm, axis=-1, keepdims=True))
        dz_ref[...] = (dav * nrm * wv * (sg * (1.0 + zv * (1.0 - sg)))).astype(dz_ref.dtype)
        part = jnp.sum(dav * nrm * sz, axis=0, keepdims=True)

        @pl.when(h == 0)
        def _():
            dw_ref[...] = part

        @pl.when(h > 0)
        def _():
            dw_ref[...] += part

    col = pl.BlockSpec((s, dk), lambda h: (0, h))
    vec = pl.BlockSpec((1, dk), lambda h: (0, 0))
    return pl.pallas_call(
        body, name=name, grid=(h_,),
        in_specs=[col, pl.BlockSpec((s, dk), lambda h: (0, 3 * h_ + h)), vec, col],
        out_specs=[col, col, vec],
        out_shape=[jax.ShapeDtypeStruct((s, h_ * dk), F32), jax.ShapeDtypeStruct((s, h_ * dk), BF16),
                   jax.ShapeDtypeStruct((1, dk), F32)],
        compiler_params=_cparams(("arbitrary",)),
    )(o, z, norm_w.reshape(1, dk), dact)


def _rel_bucket(rel):
    nb = REL_BUCKETS // 2
    max_exact = nb // 2
    ret = jnp.where(rel > 0, nb, 0)
    n = jnp.abs(rel)
    nf = jnp.maximum(n, 1).astype(F32)
    large = max_exact + (jnp.log(nf / max_exact) / math.log(REL_MAX_DIST / max_exact)
                         * (nb - max_exact)).astype(jnp.int32)
    large = jnp.minimum(large, nb - 1)
    return ret + jnp.where(n < max_exact, n, large)


def _bucket_onehot():
    half = DSWA_HALF
    outs = []
    for dil in DSWA_DILS:
        rel = (jnp.arange(3 * half)[None, :] - half - jnp.arange(half)[:, None]) * dil
        outs.append(jax.nn.one_hot(_rel_bucket(rel).reshape(-1), REL_BUCKETS, dtype=F32, axis=0))
    return jnp.stack(outs)


def _head_group_select(vals):
    rows = lax.broadcasted_iota(jnp.int32, vals[0].shape, 0)
    return jnp.where(rows < DSWA_HG, vals[0], jnp.where(rows < 2 * DSWA_HG, vals[1], vals[2]))


def _dswa_bias(table_t, onehot, *, name):
    p = onehot.shape[-1]

    def body(t_ref, oh_ref, b_ref):
        b_ref[...] = _head_group_select([_hdot(t_ref[...], oh_ref[g]) for g in range(3)])

    return pl.pallas_call(body, name=name, out_shape=jax.ShapeDtypeStruct((DSWA_HEADS, p), F32),
                          compiler_params=_cparams())(table_t, onehot)


def _dswa_dtable(dbias, onehot, *, name):
    def body(d_ref, oh_ref, t_ref):
        t_ref[...] = _head_group_select([_hdot_nt(d_ref[...], oh_ref[g]) for g in range(3)])

    return pl.pallas_call(body, name=name, out_shape=jax.ShapeDtypeStruct((DSWA_HEADS, REL_BUCKETS), F32),
                          compiler_params=_cparams())(dbias, onehot)


def _attn_valid(blk, h, s):
    half = DSWA_HALF
    nblocks = s // half
    nbs = jnp.where(h < DSWA_HG, nblocks // DSWA_DILS[0],
                    jnp.where(h < 2 * DSWA_HG, nblocks // DSWA_DILS[1], nblocks // DSWA_DILS[2]))
    b = blk & (nbs - 1)
    ii = lax.broadcasted_iota(jnp.int32, (half, 3 * half), 0)
    jj = lax.broadcasted_iota(jnp.int32, (half, 3 * half), 1)
    off = jj - half - ii
    return (jnp.abs(off) <= half) & ((jj >= half) | (b > 0)) & ((jj < 2 * half) | (b < nbs - 1))


def _dswa_attn_fwd(qkvp, bias, *, name):
    sp = qkvp.shape[2]
    half, e = DSWA_HALF, DSWA_E
    s = sp - 2 * half

    def body(qkv_ref, bias_ref, o_ref, lse_ref):
        h = pl.program_id(0)
        bias_v = bias_ref[0]

        def step(blk, carry):
            r0 = pl.multiple_of(blk * half, half)
            q = qkv_ref[0, 0, pl.ds(r0 + half, half), :]
            kw = qkv_ref[1, 0, pl.ds(r0, 3 * half), :]
            vw = qkv_ref[2, 0, pl.ds(r0, 3 * half), :]
            sc = _bdot_nt(q, kw) * (e ** -0.5) + bias_v
            sc = jnp.where(_attn_valid(blk, h, s), sc, NEG_INF)
            m = jnp.max(sc, axis=-1, keepdims=True)
            p = jnp.exp(sc - m)
            l = jnp.sum(p, axis=-1, keepdims=True)
            o_ref[0, pl.ds(r0, half), :] = _bdot(p / l, vw)
            lse_ref[0, pl.ds(r0, half), :] = m + jnp.log(l)
            return carry

        lax.fori_loop(0, s // half, step, 0)

    return pl.pallas_call(
        body, name=name, grid=(DSWA_HEADS,),
        in_specs=[pl.BlockSpec((3, 1, sp, e), lambda h: (0, h, 0, 0)),
                  pl.BlockSpec((1, half, 3 * half), lambda h: (h, 0, 0))],
        out_specs=[pl.BlockSpec((1, s, e), lambda h: (h, 0, 0)), pl.BlockSpec((1, s, 1), lambda h: (h, 0, 0))],
        out_shape=[jax.ShapeDtypeStruct((DSWA_HEADS, s, e), F32), jax.ShapeDtypeStruct((DSWA_HEADS, s, 1), F32)],
        compiler_params=_cparams(("parallel",)),
    )(qkvp, bias)


def _dswa_attn_bwd(qkvp, bias, o, lse, do, dlse, *, name):
    sp = qkvp.shape[2]
    half, e = DSWA_HALF, DSWA_E
    s = sp - 2 * half

    def body(qkv_ref, bias_ref, o_ref, lse_ref, do_ref, dlse_ref, dq_ref, dk_ref, dv_ref, db_ref):
        h = pl.program_id(0)
        bias_v = bias_ref[0]
        dk_ref[...] = jnp.zeros_like(dk_ref)
        dv_ref[...] = jnp.zeros_like(dv_ref)

        def step(blk, dbias):
            r0 = pl.multiple_of(blk * half, half)
            q = qkv_ref[0, 0, pl.ds(r0 + half, half), :]
            kw = qkv_ref[1, 0, pl.ds(r0, 3 * half), :]
            vw = qkv_ref[2, 0, pl.ds(r0, 3 * half), :]
            sc = _bdot_nt(q, kw) * (e ** -0.5) + bias_v
            valid = _attn_valid(blk, h, s)
            p = jnp.where(valid, jnp.exp(jnp.where(valid, sc, 0.0) - lse_ref[0, pl.ds(r0, half), :]), 0.0)
            dov = do_ref[0, pl.ds(r0, half), :]
            delta = jnp.sum(dov * o_ref[0, pl.ds(r0, half), :], axis=-1, keepdims=True)
            dp = _bdot_nt(dov, vw)
            dsc = p * (dp - delta + dlse_ref[0, pl.ds(r0, half), :])
            dq_ref[0, pl.ds(r0, half), :] = _bdot(dsc, kw) * (e ** -0.5)
            dk_ref[0, pl.ds(r0, 3 * half), :] += _bdot_tn(dsc, q) * (e ** -0.5)
            dv_ref[0, pl.ds(r0, 3 * half), :] += _bdot_tn(p, dov)
            return dbias + dsc

        db_ref[0] = lax.fori_loop(0, s // half, step, jnp.zeros((half, 3 * half), F32))

    hs = pl.BlockSpec((1, s, e), lambda h: (h, 0, 0))
    hp = pl.BlockSpec((1, sp, e), lambda h: (h, 0, 0))
    h1 = pl.BlockSpec((1, s, 1), lambda h: (h, 0, 0))
    bs = pl.BlockSpec((1, half, 3 * half), lambda h: (h, 0, 0))
    return pl.pallas_call(
        body, name=name, grid=(DSWA_HEADS,),
        in_specs=[pl.BlockSpec((3, 1, sp, e), lambda h: (0, h, 0, 0)), bs, hs, h1, hs, h1],
        out_specs=[hs, hp, hp, bs],
        out_shape=[jax.ShapeDtypeStruct((DSWA_HEADS, s, e), F32), jax.ShapeDtypeStruct((DSWA_HEADS, sp, e), F32),
                   jax.ShapeDtypeStruct((DSWA_HEADS, sp, e), F32),
                   jax.ShapeDtypeStruct((DSWA_HEADS, half, 3 * half), F32)],
        compiler_params=_cparams(("parallel",)),
    )(qkvp, bias, o, lse, do, dlse)


def _dswa_combine_fwd(o, lse, *, name):
    s = o.shape[1]
    tr = _pick(s, (512, 256, 128))

    def body(o_ref, l_ref, c_ref):
        for j in range(DSWA_HG):
            ls = [l_ref[g * DSWA_HG + j] for g in range(3)]
            m = jnp.maximum(jnp.maximum(ls[0], ls[1]), ls[2])
            es = [jnp.exp(x - m) for x in ls]
            den = es[0] + es[1] + es[2]
            for g in range(3):
                hh = g * DSWA_HG + j
                c_ref[hh] = (o_ref[hh] * (es[g] / den)).astype(c_ref.dtype)

    return pl.pallas_call(
        body, name=name, grid=(s // tr,),
        in_specs=[pl.BlockSpec((DSWA_HEADS, tr, DSWA_E), lambda i: (0, i, 0)),
                  pl.BlockSpec((DSWA_HEADS, tr, 1), lambda i: (0, i, 0))],
        out_specs=pl.BlockSpec((DSWA_HEADS, tr, DSWA_E), lambda i: (0, i, 0)),
        out_shape=jax.ShapeDtypeStruct((DSWA_HEADS, s, DSWA_E), BF16),
        compiler_params=_cparams(("parallel",)),
    )(o, lse)


def _dswa_combine_bwd(o, lse, dc, *, name):
    s = o.shape[1]
    tr = _pick(s, (512, 256, 128))

    def body(o_ref, l_ref, dc_ref, do_ref, dl_ref):
        for j in range(DSWA_HG):
            ls = [l_ref[g * DSWA_HG + j] for g in range(3)]
            m = jnp.maximum(jnp.maximum(ls[0], ls[1]), ls[2])
            es = [jnp.exp(x - m) for x in ls]
            den = es[0] + es[1] + es[2]
            al = [x / den for x in es]
            dal = []
            for g in range(3):
                hh = g * DSWA_HG + j
                dcv = dc_ref[hh]
                do_ref[hh] = dcv * al[g]
                dal.append(jnp.sum(dcv * o_ref[hh], axis=-1, keepdims=True))
            tot = al[0] * dal[0] + al[1] * dal[1] + al[2] * dal[2]
            for g in range(3):
                dl_ref[g * DSWA_HG + j] = al[g] * (dal[g] - tot)

    big = pl.BlockSpec((DSWA_HEADS, tr, DSWA_E), lambda i: (0, i, 0))
    one = pl.BlockSpec((DSWA_HEADS, tr, 1), lambda i: (0, i, 0))
    return pl.pallas_call(
        body, name=name, grid=(s // tr,),
        in_specs=[big, one, big], out_specs=[big, one],
        out_shape=[jax.ShapeDtypeStruct((DSWA_HEADS, s, DSWA_E), F32), jax.ShapeDtypeStruct((DSWA_HEADS, s, 1), F32)],
        compiler_params=_cparams(("parallel",)),
    )(o, lse, dc)


def _dswa_permute(t):
    s = t.shape[0]
    mid = t.shape[1:-2]
    x = t.shape[-1]
    parts = []
    for gi, dil in enumerate(DSWA_DILS):
        tg = t[..., gi * DSWA_HG:(gi + 1) * DSWA_HG, :].reshape((s // dil, dil) + mid + (DSWA_HG, x))
        nm = len(mid)
        perm = tuple(range(2, 2 + nm)) + (2 + nm, 1, 0, 3 + nm)
        parts.append(jnp.transpose(tg, perm).reshape(mid + (DSWA_HG, s, x)))
    return jnp.concatenate(parts, axis=len(mid))


def _dswa_unpermute(t):
    s, x = t.shape[1], t.shape[2]
    parts = []
    for gi, dil in enumerate(DSWA_DILS):
        tg = t[gi * DSWA_HG:(gi + 1) * DSWA_HG].reshape(DSWA_HG, dil, s // dil, x)
        parts.append(jnp.swapaxes(tg, 1, 2).reshape(DSWA_HG, s, x))
    return jnp.concatenate(parts, axis=0)


def _dswa_permute_heads(t):
    s, x = t.shape[1], t.shape[2]
    parts = []
    for gi, dil in enumerate(DSWA_DILS):
        tg = t[gi * DSWA_HG:(gi + 1) * DSWA_HG].reshape(DSWA_HG, s // dil, dil, x)
        parts.append(jnp.swapaxes(tg, 1, 2).reshape(DSWA_HG, s, x))
    return jnp.concatenate(parts, axis=0)


def _all_gather(x, *, name):
    def body(x_ref, out_ref, send_sems, recv_sems, local_sem):
        mx, my, mc = lax.axis_index("x"), lax.axis_index("y"), lax.axis_index("c")
        me, sibling = (mx, my, mc), (mx, my, 1 - mc)
        chips = [(1 - mx, my), (mx, 1 - my), (1 - mx, 1 - my)]

        def slot(px, py, pc):
            return out_ref.at[4 * px + 2 * py + pc]

        def copy(k, block, to, src=None):
            return pltpu.make_async_remote_copy(
                src_ref=slot(*block) if src is None else src, dst_ref=slot(*block),
                send_sem=send_sems.at[k], recv_sem=recv_sems.at[k],
                device_id=to, device_id_type=pl.DeviceIdType.MESH)

        mine = pltpu.make_async_copy(x_ref, slot(*me), local_sem)
        mine.start()
        first = [copy(0, me, sibling, src=x_ref)]
        first += [copy(1 + j, me, (*chip, mc), src=x_ref) for j, chip in enumerate(chips)]
        for cp in first:
            cp.start()
        passed = [copy(4 + j, (*chip, mc), sibling) for j, chip in enumerate(chips)]
        for j, chip in enumerate(chips):
            copy(1 + j, (*chip, mc), me).wait_recv()
            passed[j].start()
        copy(0, sibling, me).wait_recv()
        for j, chip in enumerate(chips):
            copy(4 + j, (*chip, 1 - mc), me).wait_recv()
        for cp in first + passed:
            cp.wait_send()
        mine.wait()

    return pl.pallas_call(
        body, name=name,
        in_specs=[pl.BlockSpec(memory_space=pl.ANY)], out_specs=pl.BlockSpec(memory_space=pl.ANY),
        out_shape=jax.ShapeDtypeStruct((N_DEV,) + x.shape, x.dtype),
        scratch_shapes=[pltpu.SemaphoreType.DMA((7,)), pltpu.SemaphoreType.DMA((7,)), pltpu.SemaphoreType.DMA(())],
    )(x)


def _exchange(send, *, name):
    def body(s_ref, r_ref, send_sems, recv_sems, local_sem):
        mx, my, mc = lax.axis_index("x"), lax.axis_index("y"), lax.axis_index("c")
        me = 4 * mx + 2 * my + mc
        mine = pltpu.make_async_copy(s_ref.at[me], r_ref.at[me], local_sem)
        mine.start()
        copies = []
        for k in range(1, N_DEV):
            fx, fy, fc = (k >> 2) & 1, (k >> 1) & 1, k & 1
            px = 1 - mx if fx else mx
            py = 1 - my if fy else my
            pc = 1 - mc if fc else mc
            peer = 4 * px + 2 * py + pc
            copies.append(pltpu.make_async_remote_copy(
                src_ref=s_ref.at[peer], dst_ref=r_ref.at[me],
                send_sem=send_sems.at[k - 1], recv_sem=recv_sems.at[k - 1],
                device_id=(px, py, pc), device_id_type=pl.DeviceIdType.MESH))
        for cp in copies:
            cp.start()
        for cp in copies:
            cp.wait_recv()
        for cp in copies:
            cp.wait_send()
        mine.wait()

    return pl.pallas_call(
        body, name=name,
        in_specs=[pl.BlockSpec(memory_space=pl.ANY)], out_specs=pl.BlockSpec(memory_space=pl.ANY),
        out_shape=jax.ShapeDtypeStruct(send.shape, send.dtype),
        scratch_shapes=[pltpu.SemaphoreType.DMA((7,)), pltpu.SemaphoreType.DMA((7,)), pltpu.SemaphoreType.DMA(())],
    )(send)


def _adamw_reduce(recv, w, m, v, *, name):
    r, c = w.shape
    tr = _pick(r, (128, 64, 8))

    def body(rv_ref, w_ref, m_ref, v_ref, g_ref, d_ref, nm_ref, nv_ref):
        g = rv_ref[0]
        for q in range(1, N_DEV):
            g = g + rv_ref[q]
        mn = ADAM_B1 * m_ref[...] + (1.0 - ADAM_B1) * g
        vn = ADAM_B2 * v_ref[...] + (1.0 - ADAM_B2) * (g * g)
        m_hat = mn / (1.0 - ADAM_B1 ** ADAM_STEP)
        v_hat = vn / (1.0 - ADAM_B2 ** ADAM_STEP)
        g_ref[...] = g
        d_ref[...] = -ADAM_LR * (m_hat / (jnp.sqrt(v_hat) + ADAM_EPS) + ADAM_WD * w_ref[...])
        nm_ref[...] = mn
        nv_ref[...] = vn

    row = pl.BlockSpec((tr, c), lambda i: (i, 0))
    return pl.pallas_call(
        body, name=name, grid=(r // tr,),
        in_specs=[pl.BlockSpec((N_DEV, tr, c), lambda i: (0, i, 0)), row, row, row],
        out_specs=[row] * 4,
        out_shape=[jax.ShapeDtypeStruct((r, c), F32)] * 4,
        compiler_params=_cparams(("parallel",)),
    )(recv, w, m, v)


_BIG = ("gdn_w_in", "gdn_w_out", "dswa_w_in", "dswa_w_out", "mlp_w1", "mlp_w2")
_SMALL = ("gdn_conv_w", "norm_mix", "norm_mlp", "norm_final", "rel_bias", "gdn_a_log", "gdn_dt_bias", "gdn_norm_w")
_ORDER = ("norm_mix", "norm_mlp", "norm_final", "rel_bias", "gdn_w_in", "gdn_conv_w", "gdn_a_log", "gdn_dt_bias",
          "gdn_norm_w", "gdn_w_out", "dswa_w_in", "dswa_w_out", "mlp_w1", "mlp_w2")
SLAB_ALIGN = 128


def _pack_rows(arrs, align):
    rows, counts = [], []
    for a in arrs:
        flat = a.reshape(-1)
        n = -(-flat.shape[0] // D_MODEL)
        flat = jnp.pad(flat, (0, n * D_MODEL - flat.shape[0]))
        rows.append(flat.reshape(n, D_MODEL))
        counts.append(n)
    out = jnp.concatenate(rows, axis=0)
    total = -(-out.shape[0] // align) * align
    return jnp.pad(out, ((0, total - out.shape[0]), (0, 0))), counts


def _unpack_rows(slab, shapes):
    outs, r = [], 0
    for shp in shapes:
        size = int(np.prod(shp))
        n = -(-size // D_MODEL)
        outs.append(slab[r:r + n].reshape(-1)[:size].reshape(shp))
        r += n
    return outs


def _col_shards(full, nshard):
    lead = full.shape[:-1]
    n = full.shape[-1] // nshard
    t = full.reshape(lead + (nshard, n))
    return jnp.moveaxis(t, -2, 0)


def _from_col_shards(g):
    t = jnp.moveaxis(g, 0, -2)
    return t.reshape(t.shape[:-2] + (t.shape[-2] * t.shape[-1],))


def kernel(x, norm_mix, norm_mlp, norm_final, rel_bias, gdn_w_in, gdn_conv_w, gdn_a_log, gdn_dt_bias, gdn_norm_w, gdn_w_out, dswa_w_in, dswa_w_out, mlp_w1, mlp_w2, loss_target, m_norm_mix, m_norm_mlp, m_norm_final, m_rel_bias, m_gdn_w_in, m_gdn_conv_w, m_gdn_a_log, m_gdn_dt_bias, m_gdn_norm_w, m_gdn_w_out, m_dswa_w_in, m_dswa_w_out, m_mlp_w1, m_mlp_w2, v_norm_mix, v_norm_mlp, v_norm_final, v_rel_bias, v_gdn_w_in, v_gdn_conv_w, v_gdn_a_log, v_gdn_dt_bias, v_gdn_norm_w, v_gdn_w_out, v_dswa_w_in, v_dswa_w_out, v_mlp_w1, v_mlp_w2):
    params = dict(norm_mix=norm_mix, norm_mlp=norm_mlp, norm_final=norm_final, rel_bias=rel_bias,
                  gdn_w_in=gdn_w_in, gdn_conv_w=gdn_conv_w, gdn_a_log=gdn_a_log, gdn_dt_bias=gdn_dt_bias,
                  gdn_norm_w=gdn_norm_w, gdn_w_out=gdn_w_out, dswa_w_in=dswa_w_in, dswa_w_out=dswa_w_out,
                  mlp_w1=mlp_w1, mlp_w2=mlp_w2)
    mom_m = dict(norm_mix=m_norm_mix, norm_mlp=m_norm_mlp, norm_final=m_norm_final, rel_bias=m_rel_bias,
                 gdn_w_in=m_gdn_w_in, gdn_conv_w=m_gdn_conv_w, gdn_a_log=m_gdn_a_log, gdn_dt_bias=m_gdn_dt_bias,
                 gdn_norm_w=m_gdn_norm_w, gdn_w_out=m_gdn_w_out, dswa_w_in=m_dswa_w_in, dswa_w_out=m_dswa_w_out,
                 mlp_w1=m_mlp_w1, mlp_w2=m_mlp_w2)
    mom_v = dict(norm_mix=v_norm_mix, norm_mlp=v_norm_mlp, norm_final=v_norm_final, rel_bias=v_rel_bias,
                 gdn_w_in=v_gdn_w_in, gdn_conv_w=v_gdn_conv_w, gdn_a_log=v_gdn_a_log, gdn_dt_bias=v_gdn_dt_bias,
                 gdn_norm_w=v_gdn_norm_w, gdn_w_out=v_gdn_w_out, dswa_w_in=v_dswa_w_in, dswa_w_out=v_dswa_w_out,
                 mlp_w1=v_mlp_w1, mlp_w2=v_mlp_w2)
    xs = x[0]
    target = loss_target[0]
    wslab, _ = _pack_rows([params[n].astype(BF16) for n in _BIG], 16)
    gathered = _all_gather(wslab, name="ag_weights")
    big_shapes = [params[n].shape for n in _BIG]
    parts = [[] for _ in _BIG]
    for dev in range(N_DEV):
        for i, t in enumerate(_unpack_rows(gathered[dev], big_shapes)):
            parts[i].append(t)
    full = {}
    for i, n in enumerate(_BIG):
        st = jnp.stack(parts[i])
        if n in ("gdn_w_in", "dswa_w_in", "mlp_w1"):
            full[n] = _from_col_shards(st)
        else:
            full[n] = jnp.moveaxis(st, 0, 1).reshape((st.shape[1], -1, st.shape[-1]))
    conv_full = None

    conv_tail, _ = _pack_rows([gdn_conv_w], 8)
    conv_g = _all_gather(conv_tail, name="ag_conv")
    conv_parts = [_unpack_rows(conv_g[dev], [gdn_conv_w.shape])[0] for dev in range(N_DEV)]
    conv_full = _from_col_shards(jnp.stack(conv_parts))[:, :, 0, :]

    loss_part, dcur, g_big, rep, g_conv = _local_step(
        xs, target, dict(norm_mix=norm_mix, norm_mlp=norm_mlp, norm_final=norm_final, rel_bias=rel_bias,
                         gdn_a_log=gdn_a_log, gdn_dt_bias=gdn_dt_bias, gdn_norm_w=gdn_norm_w), full, conv_full)
    loss = lax.psum(loss_part[0, 0], ("x", "y", "c"))
    grad_x = dcur[None]

    gfull = {n: jnp.stack(g_big[n]) for n in _BIG}
    per_dev = {}
    for n in _BIG:
        if n in ("gdn_w_in", "dswa_w_in", "mlp_w1"):
            per_dev[n] = _col_shards(gfull[n], N_DEV)
        else:
            gl = gfull[n]
            per_dev[n] = jnp.moveaxis(gl.reshape(gl.shape[0], N_DEV, gl.shape[1] // N_DEV, gl.shape[2]), 1, 0)
    conv_g_full = jnp.stack(g_conv)[:, :, None, :]
    per_dev["gdn_conv_w"] = _col_shards(conv_g_full, N_DEV)
    names = _BIG + _SMALL
    slabs = []
    for dev in range(N_DEV):
        slab, _ = _pack_rows([per_dev[n][dev] if n in per_dev else rep[n] for n in names], SLAB_ALIGN)
        slabs.append(slab)
    send = jnp.stack(slabs)
    recv = _exchange(send, name="grad_exchange")

    w_slab, _ = _pack_rows([params[n] for n in names], SLAB_ALIGN)
    m_slab, _ = _pack_rows([mom_m[n] for n in names], SLAB_ALIGN)
    v_slab, _ = _pack_rows([mom_v[n] for n in names], SLAB_ALIGN)
    g_s, d_s, nm_s, nv_s = _adamw_reduce(recv, w_slab, m_slab, v_slab, name="adamw")
    shapes = [params[n].shape for n in names]
    outs = {}
    for tag, slab in (("grad", g_s), ("delta", d_s), ("new_m", nm_s), ("new_v", nv_s)):
        for n, t in zip(names, _unpack_rows(slab, shapes)):
            outs[(tag, n)] = t
    result = [loss, grad_x]
    for tag in ("grad", "delta", "new_m", "new_v"):
        result += [outs[(tag, n)] for n in _ORDER]
    return tuple(result)


def _local_step(xs, target, sp, full, conv_full):
    s = xs.shape[0]
    norm_mix, norm_mlp, norm_final = sp["norm_mix"], sp["norm_mlp"], sp["norm_final"]
    gdn_a_log, gdn_dt_bias, gdn_norm_w = sp["gdn_a_log"], sp["gdn_dt_bias"], sp["gdn_norm_w"]
    onehot = _bucket_onehot()
    table_t = sp["rel_bias"].T
    bias = _dswa_bias(table_t, onehot, name="dswa_bias").reshape(DSWA_HEADS, DSWA_HALF, 3 * DSWA_HALF)

    saved = []
    cur = xs
    for i in range(DEPTH):
        j = i // 2
        sv = dict(x_in=cur)
        h = _rms_fwd(cur, norm_mix[i], name=f"rms_mix_fwd{i}")
        sv["h"] = h
        if i % 2 == 0:
            w_in = full["gdn_w_in"][j]
            proj = _mm(h, w_in[:, :GDN_MAIN], name=f"gdn_proj{i}")
            ab = _mm(h, w_in[:, GDN_MAIN:], name=f"gdn_proj_ab{i}")
            qkvn = _gdn_pre_fwd(proj, conv_full[j], name=f"gdn_pre_fwd{i}")
            g_all, beta_all = _gdn_gate_fwd(ab[:, :2 * GDN_HEADS], ab[:, 2 * GDN_HEADS:], gdn_a_log[j], gdn_dt_bias[j],
                                            name=f"gdn_gate_fwd{i}")
            g_row = g_all.T.reshape(2 * GDN_HEADS, s // GDN_CHUNK, GDN_CHUNK)
            b_row = beta_all.T.reshape(2 * GDN_HEADS, s // GDN_CHUNK, GDN_CHUNK)
            o, states = _gdn_chunk_fwd(qkvn, g_row, b_row, name=f"gdn_chunk_fwd{i}")
            act = _gdn_post_fwd(o, proj, gdn_norm_w[j], name=f"gdn_post_fwd{i}")
            sv.update(proj=proj, ab=ab, qkvn=qkvn, g_row=g_row, b_row=b_row, o=o, states=states, act=act)
            w_out = full["gdn_w_out"][j]
        else:
            w_in = full["dswa_w_in"][j]
            qkv = _mm(h, w_in, name=f"dswa_proj{i}", out_dtypes=(BF16,))
            qkvp = _dswa_permute(qkv.reshape(s, 3, DSWA_HEADS, DSWA_E))
            qkvp = jnp.pad(qkvp, ((0, 0), (0, 0), (DSWA_HALF, DSWA_HALF), (0, 0)))
            o_p, lse_p = _dswa_attn_fwd(qkvp, bias, name=f"dswa_attn_fwd{i}")
            o_n, lse_n = _dswa_unpermute(o_p), _dswa_unpermute(lse_p)
            comb = _dswa_combine_fwd(o_n, lse_n, name=f"dswa_comb_fwd{i}")
            act = jnp.transpose(comb, (1, 0, 2)).reshape(s, DSWA_WIDTH)
            sv.update(qkvp=qkvp, o_p=o_p, lse_p=lse_p, o_n=o_n, lse_n=lse_n, act=act)
            w_out = full["dswa_w_out"][j]
        cur = _mm(act, w_out, name=f"mix_out{i}", epilogue=lambda acc, r: (acc + r,), extras=(cur,))
        sv["x_mid"] = cur
        h2 = _rms_fwd(cur, norm_mlp[i], name=f"rms_mlp_fwd{i}")
        u, a = _mm(h2, full["mlp_w1"][i], name=f"mlp_up{i}", out_dtypes=(F32, BF16),
                   epilogue=lambda acc: (acc, jnp.square(jnp.maximum(acc, 0.0))))
        cur = _mm(a, full["mlp_w2"][i], name=f"mlp_down{i}", epilogue=lambda acc, r: (acc + r,), extras=(cur,))
        sv.update(h2=h2, u=u, a=a)
        saved.append(sv)

    loss_part, dcur, dg_final = _loss_head(cur, norm_final, target, name="loss_head")

    g_norm_mix, g_norm_mlp = [None] * DEPTH, [None] * DEPTH
    g_big = {n: [None] * full[n].shape[0] for n in _BIG}
    g_conv, g_alog, g_dt, g_nw = [None] * 2, [None] * 2, [None] * 2, [None] * 2
    d_table_t = jnp.zeros((DSWA_HEADS, REL_BUCKETS), F32)
    for i in reversed(range(DEPTH)):
        j = i // 2
        sv = saved[i]
        w1, w2 = full["mlp_w1"][i], full["mlp_w2"][i]
        du = _mm(dcur, w2, tb=True, name=f"mlp_down_bwd{i}", out_dtypes=(BF16,),
                 epilogue=lambda acc, uu: (acc * (2.0 * jnp.maximum(uu, 0.0)),), extras=(sv["u"],))
        g_big["mlp_w2"][i] = _mm(sv["a"], dcur, ta=True, name=f"mlp_w2_grad{i}")
        g_big["mlp_w1"][i] = _mm(sv["h2"], du, ta=True, name=f"mlp_w1_grad{i}")
        dh2 = _mm(du, w1, tb=True, name=f"mlp_up_bwd{i}")
        dmid, g_norm_mlp[i] = _rms_bwd(sv["x_mid"], norm_mlp[i], dh2, dcur, name=f"rms_mlp_bwd{i}")
        if i % 2 == 0:
            w_in, w_out = full["gdn_w_in"][j], full["gdn_w_out"][j]
            dact = _mm(dmid, w_out, tb=True, name=f"mix_out_bwd{i}")
            g_big["gdn_w_out"][j] = _mm(sv["act"], dmid, ta=True, name=f"mix_out_grad{i}")
            do, dz, g_nw[j] = _gdn_post_bwd(sv["o"], sv["proj"], gdn_norm_w[j], dact, name=f"gdn_post_bwd{i}")
            dq_a, dk_a, dv_a, dg_row, db_row = _gdn_chunk_bwd(sv["qkvn"], sv["g_row"], sv["b_row"], sv["states"], do,
                                                             name=f"gdn_chunk_bwd{i}")
            dqkvn = jnp.concatenate([dq_a, dk_a, dv_a], axis=1)
            dpre, g_conv[j] = _gdn_pre_bwd(sv["proj"], conv_full[j], dqkvn, name=f"gdn_pre_bwd{i}")
            nh2 = 2 * GDN_HEADS
            da_, db_, g_alog[j], g_dt[j] = _gdn_gate_bwd(sv["ab"][:, :nh2], sv["ab"][:, nh2:], gdn_a_log[j], gdn_dt_bias[j],
                                                         dg_row.reshape(nh2, s).T, db_row.reshape(nh2, s).T,
                                                         name=f"gdn_gate_bwd{i}")
            dab = jnp.concatenate([da_, db_], axis=1)
            dproj = jnp.concatenate([dpre, dz], axis=1)
            gw_main = _mm(sv["h"], dproj, ta=True, name=f"gdn_w_in_grad{i}")
            gw_ab = _mm(sv["h"], dab, ta=True, name=f"gdn_w_ab_grad{i}")
            g_big["gdn_w_in"][j] = jnp.concatenate([gw_main, gw_ab], axis=1)
            dh_ab = _mm(dab, w_in[:, GDN_MAIN:], tb=True, name=f"gdn_proj_ab_bwd{i}")
            dh = _mm(dproj, w_in[:, :GDN_MAIN], tb=True, name=f"gdn_proj_bwd{i}",
                     epilogue=lambda acc, r: (acc + r,), extras=(dh_ab,))
        else:
            w_in, w_out = full["dswa_w_in"][j], full["dswa_w_out"][j]
            dact = _mm(dmid, w_out, tb=True, name=f"mix_out_bwd{i}")
            g_big["dswa_w_out"][j] = _mm(sv["act"], dmid, ta=True, name=f"mix_out_grad{i}")
            dc = jnp.transpose(dact.reshape(s, DSWA_HEADS, DSWA_E), (1, 0, 2))
            do_n, dlse_n = _dswa_combine_bwd(sv["o_n"], sv["lse_n"], dc, name=f"dswa_comb_bwd{i}")
            do_p, dlse_p = _dswa_permute_heads(do_n), _dswa_permute_heads(dlse_n)
            dq_p, dk_p, dv_p, dbias = _dswa_attn_bwd(sv["qkvp"], bias, sv["o_p"], sv["lse_p"], do_p, dlse_p,
                                                     name=f"dswa_attn_bwd{i}")
            d_table_t = d_table_t + _dswa_dtable(dbias.reshape(DSWA_HEADS, -1), onehot, name=f"dswa_dtable{i}")
            hf = DSWA_HALF
            dqkv_p = jnp.stack([dq_p, dk_p[:, hf:-hf], dv_p[:, hf:-hf]])
            dqkv_n = jnp.stack([_dswa_unpermute(dqkv_p[t]) for t in range(3)])
            dqkv = jnp.transpose(dqkv_n, (2, 0, 1, 3)).reshape(s, 3 * DSWA_WIDTH).astype(BF16)
            g_big["dswa_w_in"][j] = _mm(sv["h"], dqkv, ta=True, name=f"dswa_w_in_grad{i}")
            dh = _mm(dqkv, w_in, tb=True, name=f"dswa_proj_bwd{i}")
        dcur, g_norm_mix[i] = _rms_bwd(sv["x_in"], norm_mix[i], dh, dmid, name=f"rms_mix_bwd{i}")

    rep = dict(norm_mix=jnp.concatenate(g_norm_mix, axis=0), norm_mlp=jnp.concatenate(g_norm_mlp, axis=0),
               norm_final=dg_final.reshape(-1), rel_bias=d_table_t.T,
               gdn_a_log=jnp.stack(g_alog).reshape(gdn_a_log.shape), gdn_dt_bias=jnp.stack(g_dt).reshape(gdn_dt_bias.shape),
               gdn_norm_w=jnp.stack(g_nw).reshape(gdn_norm_w.shape))
    return loss_part, dcur, g_big, rep, g_conv
```

```python
import functools
import math

import jax
import jax.numpy as jnp
import numpy as np
from jax import lax
from jax.experimental import pallas as pl
from jax.experimental.pallas import tpu as pltpu

F32 = jnp.float32
BF16 = jnp.bfloat16
HP = lax.Precision.HIGHEST

N_DEV = 8
D_MODEL = 1024
DEPTH = 4
RMS_EPS = 1e-6
NEG_INF = -1e30

GDN_HEADS = 8
GDN_DK = 128
GDN_CONV = 5
GDN_CHUNK = 64
GDN_QKV = 3 * GDN_HEADS * GDN_DK
GDN_MAIN = GDN_QKV + GDN_HEADS * GDN_DK
GDN_AB = 4 * GDN_HEADS

DSWA_DILS = (1, 4, 16)
DSWA_HG = 6
DSWA_E = 64
DSWA_HEADS = 18
DSWA_WIDTH = DSWA_HEADS * DSWA_E
DSWA_HALF = 64
REL_BUCKETS = 32
REL_MAX_DIST = 1024

ADAM_LR = 0.001
ADAM_B1 = 0.9
ADAM_B2 = 0.999
ADAM_EPS = 1e-08
ADAM_WD = 0.01
ADAM_STEP = 10

VMEM_LIMIT = 56 * 1024 * 1024


def _cparams(sem=None, **kw):
    return pltpu.CompilerParams(dimension_semantics=sem, vmem_limit_bytes=VMEM_LIMIT, **kw)


def _pick(dim, cands):
    for c in cands:
        if dim % c == 0:
            return c
    return dim


def _bdot(a, b):
    return jnp.dot(a.astype(BF16), b.astype(BF16), preferred_element_type=F32)


def _bdot_nt(a, b):
    return lax.dot_general(a.astype(BF16), b.astype(BF16), (((1,), (1,)), ((), ())),
                           preferred_element_type=F32)


def _bdot_tn(a, b):
    return lax.dot_general(a.astype(BF16), b.astype(BF16), (((0,), (0,)), ((), ())),
                           preferred_element_type=F32)


def _hdot(a, b):
    return jnp.dot(a, b, precision=HP, preferred_element_type=F32)


def _hdot_tn(a, b):
    return lax.dot_general(a, b, (((0,), (0,)), ((), ())), precision=HP, preferred_element_type=F32)


def _hdot_nt(a, b):
    return lax.dot_general(a, b, (((1,), (1,)), ((), ())), precision=HP, preferred_element_type=F32)


def _sigmoid(x):
    return 1.0 / (1.0 + jnp.exp(-x))


def _mm(a, b, *, name, ta=False, tb=False, out_dtypes=(F32,), epilogue=None, extras=(),
        tm=None, tn=None, tk=None):
    if ta:
        kdim, m = a.shape
    else:
        m, kdim = a.shape
    n = b.shape[0] if tb else b.shape[1]
    tm = tm or _pick(m, (512, 256, 128))
    tn = tn or _pick(n, (512, 384, 256, 128))
    tk = tk or _pick(kdim, (1024, 512, 384, 256, 128))
    nk = kdim // tk
    n_out = len(out_dtypes)
    n_ex = len(extras)

    def body(*refs):
        a_ref, b_ref = refs[0], refs[1]
        ex_refs = refs[2:2 + n_ex]
        out_refs = refs[2 + n_ex:2 + n_ex + n_out]
        acc_ref = refs[-1]
        k = pl.program_id(2)

        @pl.when(k == 0)
        def _():
            acc_ref[...] = jnp.zeros_like(acc_ref)

        av = a_ref[...].astype(BF16)
        bv = b_ref[...].astype(BF16)
        dims = (((0 if ta else 1,), (1 if tb else 0,)), ((), ()))
        acc_ref[...] += lax.dot_general(av, bv, dims, preferred_element_type=F32)

        @pl.when(k == nk - 1)
        def _():
            acc = acc_ref[...]
            outs = (acc,) if epilogue is None else epilogue(acc, *[r[...] for r in ex_refs])
            for r, o in zip(out_refs, outs):
                r[...] = o.astype(r.dtype)

    a_spec = pl.BlockSpec((tk, tm), lambda i, j, k: (k, i)) if ta else pl.BlockSpec((tm, tk), lambda i, j, k: (i, k))
    b_spec = pl.BlockSpec((tn, tk), lambda i, j, k: (j, k)) if tb else pl.BlockSpec((tk, tn), lambda i, j, k: (k, j))
    o_spec = pl.BlockSpec((tm, tn), lambda i, j, k: (i, j))
    outs = pl.pallas_call(
        body, name=name,
        grid=(m // tm, n // tn, nk),
        in_specs=[a_spec, b_spec] + [o_spec] * n_ex,
        out_specs=[o_spec] * n_out,
        out_shape=[jax.ShapeDtypeStruct((m, n), dt) for dt in out_dtypes],
        scratch_shapes=[pltpu.VMEM((tm, tn), F32)],
        compiler_params=_cparams(("parallel", "parallel", "arbitrary")),
    )(a, b, *extras)
    return outs[0] if n_out == 1 else outs


def _rms_fwd(x, g, *, name):
    s, d = x.shape
    tr = _pick(s, (512, 256, 128))

    def body(x_ref, g_ref, h_ref):
        xv = x_ref[...]
        r = lax.rsqrt(jnp.mean(xv * xv, axis=-1, keepdims=True) + RMS_EPS)
        h_ref[...] = (xv * r * g_ref[...]).astype(h_ref.dtype)

    return pl.pallas_call(
        body, name=name, grid=(s // tr,),
        in_specs=[pl.BlockSpec((tr, d), lambda i: (i, 0)), pl.BlockSpec((1, d), lambda i: (0, 0))],
        out_specs=pl.BlockSpec((tr, d), lambda i: (i, 0)),
        out_shape=jax.ShapeDtypeStruct((s, d), BF16),
        compiler_params=_cparams(("parallel",)),
    )(x, g.reshape(1, d))


def _rms_bwd(x, g, dh, dres, *, name):
    s, d = x.shape
    tr = _pick(s, (512, 256, 128))

    def body(x_ref, g_ref, dh_ref, dres_ref, dx_ref, dg_ref):
        i = pl.program_id(0)
        xv = x_ref[...]
        r = lax.rsqrt(jnp.mean(xv * xv, axis=-1, keepdims=True) + RMS_EPS)
        xn = xv * r
        dhv = dh_ref[...]
        dn = dhv * g_ref[...]
        dx_ref[...] = dres_ref[...] + r * (dn - xn * jnp.mean(dn * xn, axis=-1, keepdims=True))
        part = jnp.sum(dhv * xn, axis=0, keepdims=True)

        @pl.when(i == 0)
        def _():
            dg_ref[...] = part

        @pl.when(i > 0)
        def _():
            dg_ref[...] += part

    row = pl.BlockSpec((tr, d), lambda i: (i, 0))
    vec = pl.BlockSpec((1, d), lambda i: (0, 0))
    return pl.pallas_call(
        body, name=name, grid=(s // tr,),
        in_specs=[row, vec, row, row], out_specs=[row, vec],
        out_shape=[jax.ShapeDtypeStruct((s, d), F32), jax.ShapeDtypeStruct((1, d), F32)],
        compiler_params=_cparams(("arbitrary",)),
    )(x, g.reshape(1, d), dh, dres)


def _loss_head(x, g, target, *, name):
    s, d = x.shape
    tr = _pick(s, (512, 256, 128))

    def body(x_ref, g_ref, t_ref, loss_ref, dx_ref, dg_ref):
        i = pl.program_id(0)
        xv = x_ref[...]
        gv = g_ref[...]
        r = lax.rsqrt(jnp.mean(xv * xv, axis=-1, keepdims=True) + RMS_EPS)
        xn = xv * r
        err = xn * gv - t_ref[...]
        lpart = 0.5 * jnp.sum(jnp.mean(err * err, axis=-1, keepdims=True), axis=0, keepdims=True)
        dy = err * (1.0 / d)
        dn = dy * gv
        dx_ref[...] = r * (dn - xn * jnp.mean(dn * xn, axis=-1, keepdims=True))
        gpart = jnp.sum(dy * xn, axis=0, keepdims=True)

        @pl.when(i == 0)
        def _():
            dg_ref[...] = gpart
            loss_ref[...] = lpart

        @pl.when(i > 0)
        def _():
            dg_ref[...] += gpart
            loss_ref[...] += lpart

    row = pl.BlockSpec((tr, d), lambda i: (i, 0))
    vec = pl.BlockSpec((1, d), lambda i: (0, 0))
    one = pl.BlockSpec((1, 1), lambda i: (0, 0))
    return pl.pallas_call(
        body, name=name, grid=(s // tr,),
        in_specs=[row, vec, row], out_specs=[one, row, vec],
        out_shape=[jax.ShapeDtypeStruct((1, 1), F32), jax.ShapeDtypeStruct((s, d), F32),
                   jax.ShapeDtypeStruct((1, d), F32)],
        compiler_params=_cparams(("arbitrary",)),
    )(x, g.reshape(1, d), target)


def _shift_rows(x, sft, rows):
    s = x.shape[0]
    if sft == 0:
        return x
    y = pltpu.roll(x, (-sft) % s, 0)
    ok = (rows + sft >= 0) & (rows + sft < s)
    return jnp.where(ok, y, 0.0)


def _gdn_pre_fwd(proj, conv_w, *, name):
    s = proj.shape[0]
    nblk = GDN_QKV // 128
    pad = GDN_CONV // 2

    def body(x_ref, w_ref, o_ref):
        j = pl.program_id(0)
        x = x_ref[...]
        rows = lax.broadcasted_iota(jnp.int32, x.shape, 0)
        c = jnp.zeros_like(x)
        for t in range(GDN_CONV):
            c = c + w_ref[pl.ds(t, 1), :] * _shift_rows(x, t - pad, rows)
        a = c * _sigmoid(c)
        rinv = lax.rsqrt(jnp.sum(a * a, axis=-1, keepdims=True) + 1e-6)
        scale = jnp.where(j < GDN_HEADS, GDN_DK ** -0.5, 1.0)
        o_ref[...] = jnp.where(j >= 2 * GDN_HEADS, a, a * (rinv * scale))

    return pl.pallas_call(
        body, name=name, grid=(nblk,),
        in_specs=[pl.BlockSpec((s, 128), lambda j: (0, j)), pl.BlockSpec((GDN_CONV, 128), lambda j: (0, j))],
        out_specs=pl.BlockSpec((s, 128), lambda j: (0, j)),
        out_shape=jax.ShapeDtypeStruct((s, GDN_QKV), F32),
        compiler_params=_cparams(("parallel",)),
    )(proj, conv_w)


def _gdn_pre_bwd(proj, conv_w, dqkv, *, name):
    s = proj.shape[0]
    nblk = GDN_QKV // 128
    pad = GDN_CONV // 2

    def body(x_ref, w_ref, d_ref, dx_ref, dw_ref):
        j = pl.program_id(0)
        x = x_ref[...]
        rows = lax.broadcasted_iota(jnp.int32, x.shape, 0)
        xs = [_shift_rows(x, t - pad, rows) for t in range(GDN_CONV)]
        c = jnp.zeros_like(x)
        for t in range(GDN_CONV):
            c = c + w_ref[pl.ds(t, 1), :] * xs[t]
        sg = _sigmoid(c)
        a = c * sg
        rinv = lax.rsqrt(jnp.sum(a * a, axis=-1, keepdims=True) + 1e-6)
        scale = jnp.where(j < GDN_HEADS, GDN_DK ** -0.5, 1.0)
        dy = d_ref[0] + d_ref[1]
        nh = a * rinv
        da_n = (rinv * scale) * (dy - nh * jnp.sum(dy * nh, axis=-1, keepdims=True))
        da = jnp.where(j >= 2 * GDN_HEADS, dy, da_n)
        dc = da * (sg * (1.0 + c * (1.0 - sg)))
        dx = jnp.zeros_like(x)
        for t in range(GDN_CONV):
            dx = dx + w_ref[pl.ds(t, 1), :] * _shift_rows(dc, pad - t, rows)
            dw_ref[pl.ds(t, 1), :] = jnp.sum(dc * xs[t], axis=0, keepdims=True)
        dx_ref[...] = dx.astype(dx_ref.dtype)

    col = pl.BlockSpec((s, 128), lambda j: (0, j))
    wsp = pl.BlockSpec((GDN_CONV, 128), lambda j: (0, j))
    return pl.pallas_call(
        body, name=name, grid=(nblk,),
        in_specs=[col, wsp, pl.BlockSpec((2, s, 128), lambda j: (0, 0, j))], out_specs=[col, wsp],
        out_shape=[jax.ShapeDtypeStruct((s, GDN_QKV), BF16), jax.ShapeDtypeStruct((GDN_CONV, GDN_QKV), F32)],
        compiler_params=_cparams(("parallel",)),
    )(proj, conv_w, dqkv)


def _softplus(x):
    return jnp.maximum(x, 0.0) + jnp.log(1.0 + jnp.exp(-jnp.abs(x)))


def _gdn_gate_fwd(a, b, a_log, dt_bias, *, name):
    s = a.shape[0]
    nh = 2 * GDN_HEADS

    def body(a_ref, b_ref, al_ref, dt_ref, g_ref, be_ref):
        g_ref[...] = -jnp.exp(al_ref[...]) * _softplus(a_ref[...] + dt_ref[...])
        be_ref[...] = _sigmoid(b_ref[...])

    return pl.pallas_call(
        body, name=name,
        out_shape=[jax.ShapeDtypeStruct((s, nh), F32), jax.ShapeDtypeStruct((s, nh), F32)],
        compiler_params=_cparams(),
    )(a, b, a_log.reshape(1, nh), dt_bias.reshape(1, nh))


def _gdn_gate_bwd(a, b, a_log, dt_bias, dg, dbeta, *, name):
    s = a.shape[0]
    nh = 2 * GDN_HEADS

    def body(a_ref, b_ref, al_ref, dt_ref, dg_ref, db_ref, da_ref, dbb_ref, dal_ref, ddt_ref):
        ea = jnp.exp(al_ref[...])
        z = a_ref[...] + dt_ref[...]
        dgv = dg_ref[...]
        dz = dgv * (-ea) * _sigmoid(z)
        dal_ref[...] = jnp.sum(dgv * (-ea) * _softplus(z), axis=0, keepdims=True)
        ddt_ref[...] = jnp.sum(dz, axis=0, keepdims=True)
        sb = _sigmoid(b_ref[...])
        da_ref[...] = dz
        dbb_ref[...] = db_ref[...] * sb * (1.0 - sb)

    return pl.pallas_call(
        body, name=name,
        out_shape=[jax.ShapeDtypeStruct((s, nh), F32), jax.ShapeDtypeStruct((s, nh), F32),
                   jax.ShapeDtypeStruct((1, nh), F32), jax.ShapeDtypeStruct((1, nh), F32)],
        compiler_params=_cparams(),
    )(a, b, a_log.reshape(1, nh), dt_bias.reshape(1, nh), dg, dbeta)


def _chunk_masks(d):
    c = GDN_CHUNK
    ii = lax.broadcasted_iota(jnp.int32, (c, c), 0)
    jj = lax.broadcasted_iota(jnp.int32, (c, c), 1)
    dif = (ii - jj) * (1 - 2 * d)
    mi = dif >= 0
    mit = dif <= 0
    ms = dif > 0
    eye = ii == jj
    bds = [(ii >> sh) == (jj >> sh) for sh in (3, 4, 5)]
    return dict(mi=mi, mit=mit, ms=ms, eye=eye, bds=bds,
                mif=mi.astype(F32), mitf=mit.astype(F32), eyef=eye.astype(F32))


class _V:
    def __init__(self, xs):
        self.xs = tuple(xs)

    def __add__(self, o):
        return _lift(lambda a, b: a + b)(self, o)

    def __radd__(self, o):
        return _lift(lambda a, b: b + a)(self, o)

    def __sub__(self, o):
        return _lift(lambda a, b: a - b)(self, o)

    def __rsub__(self, o):
        return _lift(lambda a, b: b - a)(self, o)

    def __mul__(self, o):
        return _lift(lambda a, b: a * b)(self, o)

    def __rmul__(self, o):
        return _lift(lambda a, b: b * a)(self, o)

    def __and__(self, o):
        return _lift(lambda a, b: a & b)(self, o)

    def __neg__(self):
        return _lift(lambda a: -a)(self)


def _lift(f):
    def g(*args, **kw):
        n = next(len(a.xs) for a in args if isinstance(a, _V))
        return _V(f(*[a.xs[i] if isinstance(a, _V) else a for a in args], **kw) for i in range(n))
    return g


_vwhere, _vsum, _vexp, _vnot = _lift(jnp.where), _lift(jnp.sum), _lift(jnp.exp), _lift(jnp.logical_not)
_vhdot, _vhdot_tn = _lift(_bdot), _lift(_bdot_tn)
_vbdot, _vbdot_nt, _vbdot_tn = _lift(_bdot), _lift(_bdot_nt), _lift(_bdot_tn)
_vcat = _lift(lambda a, b: jnp.concatenate([a, b], axis=1))
_vlo = _lift(lambda a, n: a[:, :n])
_vhi = _lift(lambda a, n: a[:, n:])


def _stack_masks(d, n):
    m = _chunk_masks(d)
    mk = {key: _V([m[key]] * n) for key in m if key != "bds"}
    mk["bds"] = [_V([m["bds"][i]] * n) for i in range(3)]
    return mk


def _tri_inv(a, mk):
    eyef = mk["eyef"]
    bd8, bd16, bd32 = mk["bds"]
    a8 = _vwhere(bd8, a, 0.0)
    a2 = _vhdot(a8, a8)
    a4 = _vhdot(a2, a2)
    t = _vhdot(_vhdot(eyef - a8, eyef + a2), eyef + a4)
    for inner, outer in ((bd8, bd16), (bd16, bd32), (bd32, None)):
        off = _vnot(inner) if outer is None else (outer & _vnot(inner))
        low = _vwhere(off, a, 0.0)
        t = t - _vhdot(_vhdot(t, low), t)
    return t


def _chunk_prep(q, k, v, g_row, b_row, mk):
    dv = GDN_DK
    g_col = _vsum(mk["eyef"] * g_row, axis=1, keepdims=True)
    b_col = _vsum(mk["eyef"] * b_row, axis=1, keepdims=True)
    gc_col = _vsum(mk["mif"] * g_row, axis=1, keepdims=True)
    gc_row = _vsum(mk["mitf"] * g_col, axis=0, keepdims=True)
    gl = _vsum(g_row, axis=1, keepdims=True)
    decay = _vwhere(mk["mi"], _vexp(_vwhere(mk["mi"], gc_col - gc_row, 0.0)), 0.0)
    eg = _vexp(gc_col)
    e2 = _vexp(gl - gc_col)
    egl = _vexp(gl)
    kb = k * b_col
    pm = _vbdot_nt(kb, k)
    a = _vwhere(mk["ms"], pm * decay, 0.0)
    t = _tri_inv(a, mk)
    sol = _vhdot(t, _vcat(v * b_col, kb * eg))
    u, w = _vlo(sol, dv), _vhi(sol, dv)
    qm = _vbdot_nt(q, k)
    return dict(b_col=b_col, decay=decay, eg=eg, e2=e2, egl=egl, kb=kb, pm=pm, t=t, u=u, w=w,
                qm=qm, intra=qm * decay, qd=q * eg, kd=k * e2)


def _chunk_fwd_step(p, state):
    v_new = p["u"] - _vbdot(p["w"], state)
    o = _vbdot(p["qd"], state) + _vbdot(p["intra"], v_new)
    new_state = state * p["egl"] + _vbdot_tn(p["kd"], v_new)
    return o, new_state


def _chunk_bwd_step(q, k, v, p, mk, state, dso, do):
    dv_dim = GDN_DK
    v_new = p["u"] - _vbdot(p["w"], state)
    dvn = _vbdot_tn(p["intra"], do) + _vbdot(p["kd"], dso)
    dintra = _vbdot_nt(do, v_new)
    dqd = _vbdot_nt(do, state)
    ds = p["egl"] * dso + _vbdot_tn(p["qd"], do) - _vbdot_tn(p["w"], dvn)
    dkd = _vbdot_nt(v_new, dso)
    dgl = _vsum(_vsum(dso * state, axis=1, keepdims=True), axis=0, keepdims=True) * p["egl"]
    dw = -_vbdot_nt(dvn, state)
    drhs = _vhdot_tn(p["t"], _vcat(dvn, dw))
    dru, drw = _vlo(drhs, dv_dim), _vhi(drhs, dv_dim)
    da = -_vwhere(mk["ms"], _vbdot_nt(drhs, _vcat(p["u"], p["w"])), 0.0)
    b_col = p["b_col"]
    dv = dru * b_col
    dbeta = _vsum(dru * v, axis=1, keepdims=True)
    dkb = drw * p["eg"]
    deg = _vsum(drw * p["kb"], axis=1, keepdims=True)
    dp = da * p["decay"]
    ddecay = da * p["pm"]
    dkb = dkb + _vbdot(dp, k)
    dk = _vbdot_tn(dp, p["kb"])
    dqm = dintra * p["decay"]
    ddecay = ddecay + dintra * p["qm"]
    dq = _vbdot(dqm, k)
    dk = dk + _vbdot_tn(dqm, q)
    dd = ddecay * p["decay"]
    dgc_col = _vsum(dd, axis=1, keepdims=True)
    dgc_row = -_vsum(dd, axis=0, keepdims=True)
    dq = dq + dqd * p["eg"]
    deg = deg + _vsum(dqd * q, axis=1, keepdims=True)
    dk = dk + dkd * p["e2"]
    de2 = _vsum(dkd * k, axis=1, keepdims=True) * p["e2"]
    dgl = dgl + _vsum(de2, axis=0, keepdims=True)
    dgc_col = dgc_col - de2 + deg * p["eg"]
    dk = dk + dkb * b_col
    dbeta = dbeta + _vsum(dkb * k, axis=1, keepdims=True)
    dgc_col = dgc_col + _vsum(mk["eyef"] * dgc_row, axis=1, keepdims=True)
    dg_row = _vsum(mk["mif"] * dgc_col, axis=0, keepdims=True) + dgl
    dbeta_row = _vsum(mk["eyef"] * dbeta, axis=0, keepdims=True)
    return dq, dk, dv, dg_row, dbeta_row, ds


def _gdn_chunk_fwd(qkvn, g5, b5, *, name):
    s = qkvn.shape[0]
    c = GDN_CHUNK
    nc = s // c
    h_, dk = GDN_HEADS, GDN_DK

    def body(x_ref, g_ref, b_ref, o_ref, st_ref, st_scr):
        d, n = pl.program_id(0), pl.program_id(1)

        @pl.when(n == 0)
        def _():
            st_scr[...] = jnp.zeros_like(st_scr)

        mk = _stack_masks(d, h_)
        q, k, v = (_V(x_ref[:, (t * h_ + h) * dk:(t * h_ + h + 1) * dk] for h in range(h_)) for t in range(3))
        g, b = (_V(r[0, h, 0] for h in range(h_)) for r in (g_ref, b_ref))
        state = _V(st_scr[h] for h in range(h_))
        o, new_state = _chunk_fwd_step(_chunk_prep(q, k, v, g, b, mk), state)
        for h in range(h_):
            st_ref[0, h, 0] = state.xs[h]
            st_scr[h] = new_state.xs[h]
            o_ref[0, :, h * dk:(h + 1) * dk] = o.xs[h]

    ce = lambda d, n: n + d * (nc - 1 - 2 * n)
    gate = pl.BlockSpec((1, h_, 1, 1, c), lambda d, n: (d, 0, ce(d, n), 0, 0))
    return pl.pallas_call(
        body, name=name, grid=(2, nc),
        in_specs=[pl.BlockSpec((c, 3 * h_ * dk), lambda d, n: (ce(d, n), 0)), gate, gate],
        out_specs=[pl.BlockSpec((1, c, h_ * dk), lambda d, n: (d, ce(d, n), 0)),
                   pl.BlockSpec((1, h_, 1, dk, dk), lambda d, n: (d, 0, ce(d, n), 0, 0))],
        out_shape=[jax.ShapeDtypeStruct((2, s, h_ * dk), F32), jax.ShapeDtypeStruct((2, h_, nc, dk, dk), F32)],
        scratch_shapes=[pltpu.VMEM((h_, dk, dk), F32)],
        compiler_params=_cparams(("arbitrary", "arbitrary")),
    )(qkvn, g5, b5)


def _gdn_chunk_bwd(qkvn, g5, b5, states, do, *, name):
    s = qkvn.shape[0]
    c = GDN_CHUNK
    nc = s // c
    h_, dk = GDN_HEADS, GDN_DK

    def body(x_ref, g_ref, b_ref, st_ref, do_ref, dx_ref, dg_ref, db_ref, ds_scr):
        d, i = pl.program_id(0), pl.program_id(1)

        @pl.when(i == 0)
        def _():
            ds_scr[...] = jnp.zeros_like(ds_scr)

        mk = _stack_masks(d, h_)
        q, k, v = (_V(x_ref[:, (t * h_ + h) * dk:(t * h_ + h + 1) * dk] for h in range(h_)) for t in range(3))
        g, b = (_V(r[0, h, 0] for h in range(h_)) for r in (g_ref, b_ref))
        state = _V(st_ref[0, h, 0] for h in range(h_))
        dso = _V(ds_scr[h] for h in range(h_))
        dov = _V(do_ref[:, h * dk:(h + 1) * dk] for h in range(h_))
        res = _chunk_bwd_step(q, k, v, _chunk_prep(q, k, v, g, b, mk), mk, state, dso, dov)
        for h, (dq, dkk, dvv, dg_r, db_r, ds) in enumerate(zip(*[r.xs for r in res])):
            ds_scr[h] = ds
            dg_ref[0, h, 0] = dg_r
            db_ref[0, h, 0] = db_r
            for t, val in enumerate((dq, dkk, dvv)):
                dx_ref[0, :, (t * h_ + h) * dk:(t * h_ + h + 1) * dk] = val

    ce = lambda d, i: (nc - 1 - i) + d * (2 * i - nc + 1)
    gate = pl.BlockSpec((1, h_, 1, 1, c), lambda d, i: (d, 0, ce(d, i), 0, 0))
    return pl.pallas_call(
        body, name=name, grid=(2, nc),
        in_specs=[pl.BlockSpec((c, 3 * h_ * dk), lambda d, i: (ce(d, i), 0)), gate, gate,
                  pl.BlockSpec((1, h_, 1, dk, dk), lambda d, i: (d, 0, ce(d, i), 0, 0)),
                  pl.BlockSpec((c, h_ * dk), lambda d, i: (ce(d, i), 0))],
        out_specs=[pl.BlockSpec((1, c, 3 * h_ * dk), lambda d, i: (d, ce(d, i), 0)), gate, gate],
        out_shape=[jax.ShapeDtypeStruct((2, s, 3 * h_ * dk), F32)]
        + [jax.ShapeDtypeStruct((2, h_, nc, 1, c), F32)] * 2,
        scratch_shapes=[pltpu.VMEM((h_, dk, dk), F32)],
        compiler_params=_cparams(("arbitrary", "arbitrary")),
    )(qkvn, g5, b5, states, do)


def _gdn_post_fwd(o, z, norm_w, *, name):
    s = o.shape[1]
    h_, dk = GDN_HEADS, GDN_DK

    def body(o_ref, z_ref, w_ref, a_ref):
        ov = o_ref[0] + o_ref[1]
        zv = z_ref[...]
        r = lax.rsqrt(jnp.mean(ov * ov, axis=-1, keepdims=True) + RMS_EPS)
        a_ref[...] = (ov * r * w_ref[...] * (zv * _sigmoid(zv))).astype(a_ref.dtype)

    return pl.pallas_call(
        body, name=name, grid=(h_,),
        in_specs=[pl.BlockSpec((2, s, dk), lambda h: (0, 0, h)), pl.BlockSpec((s, dk), lambda h: (0, 3 * h_ + h)),
                  pl.BlockSpec((1, dk), lambda h: (0, 0))],
        out_specs=pl.BlockSpec((s, dk), lambda h: (0, h)),
        out_shape=jax.ShapeDtypeStruct((s, h_ * dk), BF16),
        compiler_params=_cparams(("parallel",)),
    )(o, z, norm_w.reshape(1, dk))


def _gdn_post_bwd(o, z, norm_w, dact, *, name):
    s = o.shape[1]
    h_, dk = GDN_HEADS, GDN_DK

    def body(o_ref, z_ref, w_ref, da_ref, do_ref, dz_ref, dw_ref):
        h = pl.program_id(0)
        ov = o_ref[0] + o_ref[1]
        zv = z_ref[...]
        wv = w_ref[...]
        dav = da_ref[...]
        r = lax.rsqrt(jnp.mean(ov * ov, axis=-1, keepdims=True) + RMS_EPS)
        nrm = ov * r
        sg = _sigmoid(zv)
        sz = zv * sg
        dn = dav * wv * sz
        do_ref[...] = r * (dn - nrm * jnp.mean(dn * nrm, axis=-1, keepdims=True))
        dz_ref[...] = (dav * nrm * wv * (sg * (1.0 + zv * (1.0 - sg)))).astype(dz_ref.dtype)
        part = jnp.sum(dav * nrm * sz, axis=0, keepdims=True)

        @pl.when(h == 0)
        def _():
            dw_ref[...] = part

        @pl.when(h > 0)
        def _():
            dw_ref[...] += part

    col = pl.BlockSpec((s, dk), lambda h: (0, h))
    vec = pl.BlockSpec((1, dk), lambda h: (0, 0))
    return pl.pallas_call(
        body, name=name, grid=(h_,),
        in_specs=[pl.BlockSpec((2, s, dk), lambda h: (0, 0, h)), pl.BlockSpec((s, dk), lambda h: (0, 3 * h_ + h)), vec, col],
        out_specs=[col, col, vec],
        out_shape=[jax.ShapeDtypeStruct((s, h_ * dk), F32), jax.ShapeDtypeStruct((s, h_ * dk), BF16),
                   jax.ShapeDtypeStruct((1, dk), F32)],
        compiler_params=_cparams(("arbitrary",)),
    )(o, z, norm_w.reshape(1, dk), dact)


def _rel_bucket(rel):
    nb = REL_BUCKETS // 2
    max_exact = nb // 2
    ret = jnp.where(rel > 0, nb, 0)
    n = jnp.abs(rel)
    nf = jnp.maximum(n, 1).astype(F32)
    large = max_exact + (jnp.log(nf / max_exact) / math.log(REL_MAX_DIST / max_exact)
                         * (nb - max_exact)).astype(jnp.int32)
    large = jnp.minimum(large, nb - 1)
    return ret + jnp.where(n < max_exact, n, large)


def _bucket_onehot():
    half = DSWA_HALF
    outs = []
    for dil in DSWA_DILS:
        rel = (jnp.arange(3 * half)[None, :] - half - jnp.arange(half)[:, None]) * dil
        outs.append(jax.nn.one_hot(_rel_bucket(rel).reshape(-1), REL_BUCKETS, dtype=F32, axis=0))
    return jnp.stack(outs)


def _head_group_select(vals):
    rows = lax.broadcasted_iota(jnp.int32, vals[0].shape, 0)
    return jnp.where(rows < DSWA_HG, vals[0], jnp.where(rows < 2 * DSWA_HG, vals[1], vals[2]))


def _dswa_bias(table_t, onehot, *, name):
    p = onehot.shape[-1]

    def body(t_ref, oh_ref, b_ref):
        b_ref[...] = _head_group_select([_hdot(t_ref[...], oh_ref[g]) for g in range(3)])

    return pl.pallas_call(body, name=name, out_shape=jax.ShapeDtypeStruct((DSWA_HEADS, p), F32),
                          compiler_params=_cparams())(table_t, onehot)


def _dswa_dtable(dbias, onehot, *, name):
    def body(d_ref, oh_ref, t_ref):
        t_ref[...] = _head_group_select([_hdot_nt(d_ref[...], oh_ref[g]) for g in range(3)])

    return pl.pallas_call(body, name=name, out_shape=jax.ShapeDtypeStruct((DSWA_HEADS, REL_BUCKETS), F32),
                          compiler_params=_cparams())(dbias, onehot)


def _attn_valid(blk, h, s):
    half = DSWA_HALF
    nblocks = s // half
    nbs = jnp.where(h < DSWA_HG, nblocks // DSWA_DILS[0],
                    jnp.where(h < 2 * DSWA_HG, nblocks // DSWA_DILS[1], nblocks // DSWA_DILS[2]))
    b = blk & (nbs - 1)
    ii = lax.broadcasted_iota(jnp.int32, (half, 3 * half), 0)
    jj = lax.broadcasted_iota(jnp.int32, (half, 3 * half), 1)
    off = jj - half - ii
    return (jnp.abs(off) <= half) & ((jj >= half) | (b > 0)) & ((jj < 2 * half) | (b < nbs - 1))


def _dswa_attn_fwd(qkvp, bias, *, name):
    sp = qkvp.shape[2]
    half, e = DSWA_HALF, DSWA_E
    s = sp - 2 * half

    def body(qkv_ref, bias_ref, o_ref, lse_ref):
        h = pl.program_id(0)
        bias_v = bias_ref[0]

        def step(blk, carry):
            r0 = pl.multiple_of(blk * half, half)
            q = qkv_ref[0, 0, pl.ds(r0 + half, half), :]
            kw = qkv_ref[1, 0, pl.ds(r0, 3 * half), :]
            vw = qkv_ref[2, 0, pl.ds(r0, 3 * half), :]
            sc = _bdot_nt(q, kw) * (e ** -0.5) + bias_v
            sc = jnp.where(_attn_valid(blk, h, s), sc, NEG_INF)
            m = jnp.max(sc, axis=-1, keepdims=True)
            p = jnp.exp(sc - m)
            l = jnp.sum(p, axis=-1, keepdims=True)
            o_ref[0, pl.ds(r0, half), :] = _bdot(p / l, vw)
            lse_ref[0, pl.ds(r0, half), :] = m + jnp.log(l)
            return carry

        lax.fori_loop(0, s // half, step, 0)

    return pl.pallas_call(
        body, name=name, grid=(DSWA_HEADS,),
        in_specs=[pl.BlockSpec((3, 1, sp, e), lambda h: (0, h, 0, 0)),
                  pl.BlockSpec((1, half, 3 * half), lambda h: (h, 0, 0))],
        out_specs=[pl.BlockSpec((1, s, e), lambda h: (h, 0, 0)), pl.BlockSpec((1, s, 1), lambda h: (h, 0, 0))],
        out_shape=[jax.ShapeDtypeStruct((DSWA_HEADS, s, e), F32), jax.ShapeDtypeStruct((DSWA_HEADS, s, 1), F32)],
        compiler_params=_cparams(("parallel",)),
    )(qkvp, bias)


def _dswa_attn_bwd(qkvp, bias, o, lse, do, dlse, *, name):
    sp = qkvp.shape[2]
    half, e = DSWA_HALF, DSWA_E
    s = sp - 2 * half

    def body(qkv_ref, bias_ref, o_ref, lse_ref, do_ref, dlse_ref, dq_ref, dk_ref, dv_ref, db_ref):
        h = pl.program_id(0)
        bias_v = bias_ref[0]
        dk_ref[...] = jnp.zeros_like(dk_ref)
        dv_ref[...] = jnp.zeros_like(dv_ref)

        def step(blk, dbias):
            r0 = pl.multiple_of(blk * half, half)
            q = qkv_ref[0, 0, pl.ds(r0 + half, half), :]
            kw = qkv_ref[1, 0, pl.ds(r0, 3 * half), :]
            vw = qkv_ref[2, 0, pl.ds(r0, 3 * half), :]
            sc = _bdot_nt(q, kw) * (e ** -0.5) + bias_v
            valid = _attn_valid(blk, h, s)
            p = jnp.where(valid, jnp.exp(jnp.where(valid, sc, 0.0) - lse_ref[0, pl.ds(r0, half), :]), 0.0)
            dov = do_ref[0, pl.ds(r0, half), :]
            delta = jnp.sum(dov * o_ref[0, pl.ds(r0, half), :], axis=-1, keepdims=True)
            dp = _bdot_nt(dov, vw)
            dsc = p * (dp - delta + dlse_ref[0, pl.ds(r0, half), :])
            dq_ref[0, pl.ds(r0, half), :] = _bdot(dsc, kw) * (e ** -0.5)
            dk_ref[0, pl.ds(r0, 3 * half), :] += _bdot_tn(dsc, q) * (e ** -0.5)
            dv_ref[0, pl.ds(r0, 3 * half), :] += _bdot_tn(p, dov)
            return dbias + dsc

        db_ref[0] = lax.fori_loop(0, s // half, step, jnp.zeros((half, 3 * half), F32))

    hs = pl.BlockSpec((1, s, e), lambda h: (h, 0, 0))
    hp = pl.BlockSpec((1, sp, e), lambda h: (h, 0, 0))
    h1 = pl.BlockSpec((1, s, 1), lambda h: (h, 0, 0))
    bs = pl.BlockSpec((1, half, 3 * half), lambda h: (h, 0, 0))
    return pl.pallas_call(
        body, name=name, grid=(DSWA_HEADS,),
        in_specs=[pl.BlockSpec((3, 1, sp, e), lambda h: (0, h, 0, 0)), bs, hs, h1, hs, h1],
        out_specs=[hs, hp, hp, bs],
        out_shape=[jax.ShapeDtypeStruct((DSWA_HEADS, s, e), F32), jax.ShapeDtypeStruct((DSWA_HEADS, sp, e), F32),
                   jax.ShapeDtypeStruct((DSWA_HEADS, sp, e), F32),
                   jax.ShapeDtypeStruct((DSWA_HEADS, half, 3 * half), F32)],
        compiler_params=_cparams(("parallel",)),
    )(qkvp, bias, o, lse, do, dlse)


def _dswa_combine_fwd(o, lse, *, name):
    s = o.shape[1]
    tr = _pick(s, (512, 256, 128))

    def body(o_ref, l_ref, c_ref):
        for j in range(DSWA_HG):
            ls = [l_ref[g * DSWA_HG + j] for g in range(3)]
            m = jnp.maximum(jnp.maximum(ls[0], ls[1]), ls[2])
            es = [jnp.exp(x - m) for x in ls]
            den = es[0] + es[1] + es[2]
            for g in range(3):
                hh = g * DSWA_HG + j
                c_ref[hh] = (o_ref[hh] * (es[g] / den)).astype(c_ref.dtype)

    return pl.pallas_call(
        body, name=name, grid=(s // tr,),
        in_specs=[pl.BlockSpec((DSWA_HEADS, tr, DSWA_E), lambda i: (0, i, 0)),
                  pl.BlockSpec((DSWA_HEADS, tr, 1), lambda i: (0, i, 0))],
        out_specs=pl.BlockSpec((DSWA_HEADS, tr, DSWA_E), lambda i: (0, i, 0)),
        out_shape=jax.ShapeDtypeStruct((DSWA_HEADS, s, DSWA_E), BF16),
        compiler_params=_cparams(("parallel",)),
    )(o, lse)


def _dswa_combine_bwd(o, lse, dc, *, name):
    s = o.shape[1]
    tr = _pick(s, (512, 256, 128))

    def body(o_ref, l_ref, dc_ref, do_ref, dl_ref):
        for j in range(DSWA_HG):
            ls = [l_ref[g * DSWA_HG + j] for g in range(3)]
            m = jnp.maximum(jnp.maximum(ls[0], ls[1]), ls[2])
            es = [jnp.exp(x - m) for x in ls]
            den = es[0] + es[1] + es[2]
            al = [x / den for x in es]
            dal = []
            for g in range(3):
                hh = g * DSWA_HG + j
                dcv = dc_ref[hh]
                do_ref[hh] = dcv * al[g]
                dal.append(jnp.sum(dcv * o_ref[hh], axis=-1, keepdims=True))
            tot = al[0] * dal[0] + al[1] * dal[1] + al[2] * dal[2]
            for g in range(3):
                dl_ref[g * DSWA_HG + j] = al[g] * (dal[g] - tot)

    big = pl.BlockSpec((DSWA_HEADS, tr, DSWA_E), lambda i: (0, i, 0))
    one = pl.BlockSpec((DSWA_HEADS, tr, 1), lambda i: (0, i, 0))
    return pl.pallas_call(
        body, name=name, grid=(s // tr,),
        in_specs=[big, one, big], out_specs=[big, one],
        out_shape=[jax.ShapeDtypeStruct((DSWA_HEADS, s, DSWA_E), F32), jax.ShapeDtypeStruct((DSWA_HEADS, s, 1), F32)],
        compiler_params=_cparams(("parallel",)),
    )(o, lse, dc)


def _dswa_permute(t):
    s = t.shape[0]
    mid = t.shape[1:-2]
    x = t.shape[-1]
    parts = []
    for gi, dil in enumerate(DSWA_DILS):
        tg = t[..., gi * DSWA_HG:(gi + 1) * DSWA_HG, :].reshape((s // dil, dil) + mid + (DSWA_HG, x))
        nm = len(mid)
        perm = tuple(range(2, 2 + nm)) + (2 + nm, 1, 0, 3 + nm)
        parts.append(jnp.transpose(tg, perm).reshape(mid + (DSWA_HG, s, x)))
    return jnp.concatenate(parts, axis=len(mid))


def _dswa_unpermute(t):
    s, x = t.shape[1], t.shape[2]
    parts = []
    for gi, dil in enumerate(DSWA_DILS):
        tg = t[gi * DSWA_HG:(gi + 1) * DSWA_HG].reshape(DSWA_HG, dil, s // dil, x)
        parts.append(jnp.swapaxes(tg, 1, 2).reshape(DSWA_HG, s, x))
    return jnp.concatenate(parts, axis=0)


def _dswa_permute_heads(t):
    s, x = t.shape[1], t.shape[2]
    parts = []
    for gi, dil in enumerate(DSWA_DILS):
        tg = t[gi * DSWA_HG:(gi + 1) * DSWA_HG].reshape(DSWA_HG, s // dil, dil, x)
        parts.append(jnp.swapaxes(tg, 1, 2).reshape(DSWA_HG, s, x))
    return jnp.concatenate(parts, axis=0)


def _all_gather(x, *, name):
    def body(x_ref, out_ref, send_sems, recv_sems, local_sem):
        mx, my, mc = lax.axis_index("x"), lax.axis_index("y"), lax.axis_index("c")
        me, sibling = (mx, my, mc), (mx, my, 1 - mc)
        chips = [(1 - mx, my), (mx, 1 - my), (1 - mx, 1 - my)]

        def slot(px, py, pc):
            return out_ref.at[4 * px + 2 * py + pc]

        def copy(k, block, to, src=None):
            return pltpu.make_async_remote_copy(
                src_ref=slot(*block) if src is None else src, dst_ref=slot(*block),
                send_sem=send_sems.at[k], recv_sem=recv_sems.at[k],
                device_id=to, device_id_type=pl.DeviceIdType.MESH)

        mine = pltpu.make_async_copy(x_ref, slot(*me), local_sem)
        mine.start()
        first = [copy(0, me, sibling, src=x_ref)]
        first += [copy(1 + j, me, (*chip, mc), src=x_ref) for j, chip in enumerate(chips)]
        for cp in first:
            cp.start()
        passed = [copy(4 + j, (*chip, mc), sibling) for j, chip in enumerate(chips)]
        for j, chip in enumerate(chips):
            copy(1 + j, (*chip, mc), me).wait_recv()
            passed[j].start()
        copy(0, sibling, me).wait_recv()
        for j, chip in enumerate(chips):
            copy(4 + j, (*chip, 1 - mc), me).wait_recv()
        for cp in first + passed:
            cp.wait_send()
        mine.wait()

    return pl.pallas_call(
        body, name=name,
        in_specs=[pl.BlockSpec(memory_space=pl.ANY)], out_specs=pl.BlockSpec(memory_space=pl.ANY),
        out_shape=jax.ShapeDtypeStruct((N_DEV,) + x.shape, x.dtype),
        scratch_shapes=[pltpu.SemaphoreType.DMA((7,)), pltpu.SemaphoreType.DMA((7,)), pltpu.SemaphoreType.DMA(())],
    )(x)


def _exchange(send, *, name):
    def body(s_ref, r_ref, send_sems, recv_sems, local_sem):
        mx, my, mc = lax.axis_index("x"), lax.axis_index("y"), lax.axis_index("c")
        me = 4 * mx + 2 * my + mc
        mine = pltpu.make_async_copy(s_ref.at[me], r_ref.at[me], local_sem)
        mine.start()
        copies = []
        for k in range(1, N_DEV):
            fx, fy, fc = (k >> 2) & 1, (k >> 1) & 1, k & 1
            px = 1 - mx if fx else mx
            py = 1 - my if fy else my
            pc = 1 - mc if fc else mc
            peer = 4 * px + 2 * py + pc
            copies.append(pltpu.make_async_remote_copy(
                src_ref=s_ref.at[peer], dst_ref=r_ref.at[me],
                send_sem=send_sems.at[k - 1], recv_sem=recv_sems.at[k - 1],
                device_id=(px, py, pc), device_id_type=pl.DeviceIdType.MESH))
        for cp in copies:
            cp.start()
        for cp in copies:
            cp.wait_recv()
        for cp in copies:
            cp.wait_send()
        mine.wait()

    return pl.pallas_call(
        body, name=name,
        in_specs=[pl.BlockSpec(memory_space=pl.ANY)], out_specs=pl.BlockSpec(memory_space=pl.ANY),
        out_shape=jax.ShapeDtypeStruct(send.shape, send.dtype),
        scratch_shapes=[pltpu.SemaphoreType.DMA((7,)), pltpu.SemaphoreType.DMA((7,)), pltpu.SemaphoreType.DMA(())],
    )(send)


def _adamw_reduce(recv, w, m, v, *, name):
    r, c = w.shape
    tr = _pick(r, (128, 64, 8))

    def body(rv_ref, w_ref, m_ref, v_ref, g_ref, d_ref, nm_ref, nv_ref):
        g = rv_ref[0]
        for q in range(1, N_DEV):
            g = g + rv_ref[q]
        mn = ADAM_B1 * m_ref[...] + (1.0 - ADAM_B1) * g
        vn = ADAM_B2 * v_ref[...] + (1.0 - ADAM_B2) * (g * g)
        m_hat = mn / (1.0 - ADAM_B1 ** ADAM_STEP)
        v_hat = vn / (1.0 - ADAM_B2 ** ADAM_STEP)
        g_ref[...] = g
        d_ref[...] = -ADAM_LR * (m_hat / (jnp.sqrt(v_hat) + ADAM_EPS) + ADAM_WD * w_ref[...])
        nm_ref[...] = mn
        nv_ref[...] = vn

    row = pl.BlockSpec((tr, c), lambda i: (i, 0))
    return pl.pallas_call(
        body, name=name, grid=(r // tr,),
        in_specs=[pl.BlockSpec((N_DEV, tr, c), lambda i: (0, i, 0)), row, row, row],
        out_specs=[row] * 4,
        out_shape=[jax.ShapeDtypeStruct((r, c), F32)] * 4,
        compiler_params=_cparams(("parallel",)),
    )(recv, w, m, v)


_BIG = ("gdn_w_in", "gdn_w_out", "dswa_w_in", "dswa_w_out", "mlp_w1", "mlp_w2")
_SMALL = ("gdn_conv_w", "norm_mix", "norm_mlp", "norm_final", "rel_bias", "gdn_a_log", "gdn_dt_bias", "gdn_norm_w")
_ORDER = ("norm_mix", "norm_mlp", "norm_final", "rel_bias", "gdn_w_in", "gdn_conv_w", "gdn_a_log", "gdn_dt_bias",
          "gdn_norm_w", "gdn_w_out", "dswa_w_in", "dswa_w_out", "mlp_w1", "mlp_w2")
SLAB_ALIGN = 128


def _pack_rows(arrs, align):
    rows, counts = [], []
    for a in arrs:
        flat = a.reshape(-1)
        n = -(-flat.shape[0] // D_MODEL)
        flat = jnp.pad(flat, (0, n * D_MODEL - flat.shape[0]))
        rows.append(flat.reshape(n, D_MODEL))
        counts.append(n)
    out = jnp.concatenate(rows, axis=0)
    total = -(-out.shape[0] // align) * align
    return jnp.pad(out, ((0, total - out.shape[0]), (0, 0))), counts


def _unpack_rows(slab, shapes):
    outs, r = [], 0
    for shp in shapes:
        size = int(np.prod(shp))
        n = -(-size // D_MODEL)
        outs.append(slab[r:r + n].reshape(-1)[:size].reshape(shp))
        r += n
    return outs


def _col_shards(full, nshard):
    lead = full.shape[:-1]
    n = full.shape[-1] // nshard
    t = full.reshape(lead + (nshard, n))
    return jnp.moveaxis(t, -2, 0)


def _from_col_shards(g):
    t = jnp.moveaxis(g, 0, -2)
    return t.reshape(t.shape[:-2] + (t.shape[-2] * t.shape[-1],))


def kernel(x, norm_mix, norm_mlp, norm_final, rel_bias, gdn_w_in, gdn_conv_w, gdn_a_log, gdn_dt_bias, gdn_norm_w, gdn_w_out, dswa_w_in, dswa_w_out, mlp_w1, mlp_w2, loss_target, m_norm_mix, m_norm_mlp, m_norm_final, m_rel_bias, m_gdn_w_in, m_gdn_conv_w, m_gdn_a_log, m_gdn_dt_bias, m_gdn_norm_w, m_gdn_w_out, m_dswa_w_in, m_dswa_w_out, m_mlp_w1, m_mlp_w2, v_norm_mix, v_norm_mlp, v_norm_final, v_rel_bias, v_gdn_w_in, v_gdn_conv_w, v_gdn_a_log, v_gdn_dt_bias, v_gdn_norm_w, v_gdn_w_out, v_dswa_w_in, v_dswa_w_out, v_mlp_w1, v_mlp_w2):
    params = dict(norm_mix=norm_mix, norm_mlp=norm_mlp, norm_final=norm_final, rel_bias=rel_bias,
                  gdn_w_in=gdn_w_in, gdn_conv_w=gdn_conv_w, gdn_a_log=gdn_a_log, gdn_dt_bias=gdn_dt_bias,
                  gdn_norm_w=gdn_norm_w, gdn_w_out=gdn_w_out, dswa_w_in=dswa_w_in, dswa_w_out=dswa_w_out,
                  mlp_w1=mlp_w1, mlp_w2=mlp_w2)
    mom_m = dict(norm_mix=m_norm_mix, norm_mlp=m_norm_mlp, norm_final=m_norm_final, rel_bias=m_rel_bias,
                 gdn_w_in=m_gdn_w_in, gdn_conv_w=m_gdn_conv_w, gdn_a_log=m_gdn_a_log, gdn_dt_bias=m_gdn_dt_bias,
                 gdn_norm_w=m_gdn_norm_w, gdn_w_out=m_gdn_w_out, dswa_w_in=m_dswa_w_in, dswa_w_out=m_dswa_w_out,
                 mlp_w1=m_mlp_w1, mlp_w2=m_mlp_w2)
    mom_v = dict(norm_mix=v_norm_mix, norm_mlp=v_norm_mlp, norm_final=v_norm_final, rel_bias=v_rel_bias,
                 gdn_w_in=v_gdn_w_in, gdn_conv_w=v_gdn_conv_w, gdn_a_log=v_gdn_a_log, gdn_dt_bias=v_gdn_dt_bias,
                 gdn_norm_w=v_gdn_norm_w, gdn_w_out=v_gdn_w_out, dswa_w_in=v_dswa_w_in, dswa_w_out=v_dswa_w_out,
                 mlp_w1=v_mlp_w1, mlp_w2=v_mlp_w2)
    xs = x[0]
    target = loss_target[0]
    wslab, _ = _pack_rows([params[n].astype(BF16) for n in _BIG], 16)
    gathered = _all_gather(wslab, name="ag_weights")
    big_shapes = [params[n].shape for n in _BIG]
    parts = [[] for _ in _BIG]
    for dev in range(N_DEV):
        for i, t in enumerate(_unpack_rows(gathered[dev], big_shapes)):
            parts[i].append(t)
    full = {}
    for i, n in enumerate(_BIG):
        st = jnp.stack(parts[i])
        if n in ("gdn_w_in", "dswa_w_in", "mlp_w1"):
            full[n] = _from_col_shards(st)
        else:
            full[n] = jnp.moveaxis(st, 0, 1).reshape((st.shape[1], -1, st.shape[-1]))
    conv_full = None

    conv_tail, _ = _pack_rows([gdn_conv_w], 8)
    conv_g = _all_gather(conv_tail, name="ag_conv")
    conv_parts = [_unpack_rows(conv_g[dev], [gdn_conv_w.shape])[0] for dev in range(N_DEV)]
    conv_full = _from_col_shards(jnp.stack(conv_parts))[:, :, 0, :]

    loss_part, dcur, g_big, rep, g_conv = _local_step(
        xs, target, dict(norm_mix=norm_mix, norm_mlp=norm_mlp, norm_final=norm_final, rel_bias=rel_bias,
                         gdn_a_log=gdn_a_log, gdn_dt_bias=gdn_dt_bias, gdn_norm_w=gdn_norm_w), full, conv_full)
    loss = lax.psum(loss_part[0, 0], ("x", "y", "c"))
    grad_x = dcur[None]

    gfull = {n: jnp.stack(g_big[n]) for n in _BIG}
    per_dev = {}
    for n in _BIG:
        if n in ("gdn_w_in", "dswa_w_in", "mlp_w1"):
            per_dev[n] = _col_shards(gfull[n], N_DEV)
        else:
            gl = gfull[n]
            per_dev[n] = jnp.moveaxis(gl.reshape(gl.shape[0], N_DEV, gl.shape[1] // N_DEV, gl.shape[2]), 1, 0)
    conv_g_full = jnp.stack(g_conv)[:, :, None, :]
    per_dev["gdn_conv_w"] = _col_shards(conv_g_full, N_DEV)
    names = _BIG + _SMALL
    slabs = []
    for dev in range(N_DEV):
        slab, _ = _pack_rows([per_dev[n][dev] if n in per_dev else rep[n] for n in names], SLAB_ALIGN)
        slabs.append(slab)
    send = jnp.stack(slabs)
    recv = _exchange(send, name="grad_exchange")

    w_slab, _ = _pack_rows([params[n] for n in names], SLAB_ALIGN)
    m_slab, _ = _pack_rows([mom_m[n] for n in names], SLAB_ALIGN)
    v_slab, _ = _pack_rows([mom_v[n] for n in names], SLAB_ALIGN)
    g_s, d_s, nm_s, nv_s = _adamw_reduce(recv, w_slab, m_slab, v_slab, name="adamw")
    shapes = [params[n].shape for n in names]
    outs = {}
    for tag, slab in (("grad", g_s), ("delta", d_s), ("new_m", nm_s), ("new_v", nv_s)):
        for n, t in zip(names, _unpack_rows(slab, shapes)):
            outs[(tag, n)] = t
    result = [loss, grad_x]
    for tag in ("grad", "delta", "new_m", "new_v"):
        result += [outs[(tag, n)] for n in _ORDER]
    return tuple(result)


def _local_step(xs, target, sp, full, conv_full):
    s = xs.shape[0]
    norm_mix, norm_mlp, norm_final = sp["norm_mix"], sp["norm_mlp"], sp["norm_final"]
    gdn_a_log, gdn_dt_bias, gdn_norm_w = sp["gdn_a_log"], sp["gdn_dt_bias"], sp["gdn_norm_w"]
    onehot = _bucket_onehot()
    table_t = sp["rel_bias"].T
    bias = _dswa_bias(table_t, onehot, name="dswa_bias").reshape(DSWA_HEADS, DSWA_HALF, 3 * DSWA_HALF)

    saved = []
    cur = xs
    for i in range(DEPTH):
        j = i // 2
        sv = dict(x_in=cur)
        h = _rms_fwd(cur, norm_mix[i], name=f"rms_mix_fwd{i}")
        sv["h"] = h
        if i % 2 == 0:
            w_in = full["gdn_w_in"][j]
            proj = _mm(h, w_in[:, :GDN_MAIN], name=f"gdn_proj{i}")
            ab = _mm(h, w_in[:, GDN_MAIN:], name=f"gdn_proj_ab{i}")
            qkvn = _gdn_pre_fwd(proj, conv_full[j], name=f"gdn_pre_fwd{i}")
            g_all, beta_all = _gdn_gate_fwd(ab[:, :2 * GDN_HEADS], ab[:, 2 * GDN_HEADS:], gdn_a_log[j], gdn_dt_bias[j],
                                            name=f"gdn_gate_fwd{i}")
            gshape = (2, GDN_HEADS, s // GDN_CHUNK, 1, GDN_CHUNK)
            g_row = g_all.T.reshape(gshape)
            b_row = beta_all.T.reshape(gshape)
            o, states = _gdn_chunk_fwd(qkvn, g_row, b_row, name=f"gdn_chunk_fwd{i}")
            act = _gdn_post_fwd(o, proj, gdn_norm_w[j], name=f"gdn_post_fwd{i}")
            sv.update(proj=proj, ab=ab, qkvn=qkvn, g_row=g_row, b_row=b_row, o=o, states=states, act=act)
            w_out = full["gdn_w_out"][j]
        else:
            w_in = full["dswa_w_in"][j]
            qkv = _mm(h, w_in, name=f"dswa_proj{i}", out_dtypes=(BF16,))
            qkvp = _dswa_permute(qkv.reshape(s, 3, DSWA_HEADS, DSWA_E))
            qkvp = jnp.pad(qkvp, ((0, 0), (0, 0), (DSWA_HALF, DSWA_HALF), (0, 0)))
            o_p, lse_p = _dswa_attn_fwd(qkvp, bias, name=f"dswa_attn_fwd{i}")
            o_n, lse_n = _dswa_unpermute(o_p), _dswa_unpermute(lse_p)
            comb = _dswa_combine_fwd(o_n, lse_n, name=f"dswa_comb_fwd{i}")
            act = jnp.transpose(comb, (1, 0, 2)).reshape(s, DSWA_WIDTH)
            sv.update(qkvp=qkvp, o_p=o_p, lse_p=lse_p, o_n=o_n, lse_n=lse_n, act=act)
            w_out = full["dswa_w_out"][j]
        cur = _mm(act, w_out, name=f"mix_out{i}", epilogue=lambda acc, r: (acc + r,), extras=(cur,))
        sv["x_mid"] = cur
        h2 = _rms_fwd(cur, norm_mlp[i], name=f"rms_mlp_fwd{i}")
        u, a = _mm(h2, full["mlp_w1"][i], name=f"mlp_up{i}", out_dtypes=(F32, BF16),
                   epilogue=lambda acc: (acc, jnp.square(jnp.maximum(acc, 0.0))))
        cur = _mm(a, full["mlp_w2"][i], name=f"mlp_down{i}", epilogue=lambda acc, r: (acc + r,), extras=(cur,))
        sv.update(h2=h2, u=u, a=a)
        saved.append(sv)

    loss_part, dcur, dg_final = _loss_head(cur, norm_final, target, name="loss_head")

    g_norm_mix, g_norm_mlp = [None] * DEPTH, [None] * DEPTH
    g_big = {n: [None] * full[n].shape[0] for n in _BIG}
    g_conv, g_alog, g_dt, g_nw = [None] * 2, [None] * 2, [None] * 2, [None] * 2
    d_table_t = jnp.zeros((DSWA_HEADS, REL_BUCKETS), F32)
    for i in reversed(range(DEPTH)):
        j = i // 2
        sv = saved[i]
        w1, w2 = full["mlp_w1"][i], full["mlp_w2"][i]
        du = _mm(dcur, w2, tb=True, name=f"mlp_down_bwd{i}", out_dtypes=(BF16,),
                 epilogue=lambda acc, uu: (acc * (2.0 * jnp.maximum(uu, 0.0)),), extras=(sv["u"],))
        g_big["mlp_w2"][i] = _mm(sv["a"], dcur, ta=True, name=f"mlp_w2_grad{i}")
        g_big["mlp_w1"][i] = _mm(sv["h2"], du, ta=True, name=f"mlp_w1_grad{i}")
        dh2 = _mm(du, w1, tb=True, name=f"mlp_up_bwd{i}")
        dmid, g_norm_mlp[i] = _rms_bwd(sv["x_mid"], norm_mlp[i], dh2, dcur, name=f"rms_mlp_bwd{i}")
        if i % 2 == 0:
            w_in, w_out = full["gdn_w_in"][j], full["gdn_w_out"][j]
            dact = _mm(dmid, w_out, tb=True, name=f"mix_out_bwd{i}")
            g_big["gdn_w_out"][j] = _mm(sv["act"], dmid, ta=True, name=f"mix_out_grad{i}")
            do, dz, g_nw[j] = _gdn_post_bwd(sv["o"], sv["proj"], gdn_norm_w[j], dact, name=f"gdn_post_bwd{i}")
            dqkvn, dg_row, db_row = _gdn_chunk_bwd(sv["qkvn"], sv["g_row"], sv["b_row"], sv["states"], do,
                                                   name=f"gdn_chunk_bwd{i}")
            dpre, g_conv[j] = _gdn_pre_bwd(sv["proj"], conv_full[j], dqkvn, name=f"gdn_pre_bwd{i}")
            nh2 = 2 * GDN_HEADS
            da_, db_, g_alog[j], g_dt[j] = _gdn_gate_bwd(sv["ab"][:, :nh2], sv["ab"][:, nh2:], gdn_a_log[j], gdn_dt_bias[j],
                                                         dg_row.reshape(nh2, s).T, db_row.reshape(nh2, s).T,
                                                         name=f"gdn_gate_bwd{i}")
            dab = jnp.concatenate([da_, db_], axis=1)
            dproj = jnp.concatenate([dpre, dz], axis=1)
            gw_main = _mm(sv["h"], dproj, ta=True, name=f"gdn_w_in_grad{i}")
            gw_ab = _mm(sv["h"], dab, ta=True, name=f"gdn_w_ab_grad{i}")
            g_big["gdn_w_in"][j] = jnp.concatenate([gw_main, gw_ab], axis=1)
            dh_ab = _mm(dab, w_in[:, GDN_MAIN:], tb=True, name=f"gdn_proj_ab_bwd{i}")
            dh = _mm(dproj, w_in[:, :GDN_MAIN], tb=True, name=f"gdn_proj_bwd{i}",
                     epilogue=lambda acc, r: (acc + r,), extras=(dh_ab,))
        else:
            w_in, w_out = full["dswa_w_in"][j], full["dswa_w_out"][j]
            dact = _mm(dmid, w_out, tb=True, name=f"mix_out_bwd{i}")
            g_big["dswa_w_out"][j] = _mm(sv["act"], dmid, ta=True, name=f"mix_out_grad{i}")
            dc = jnp.transpose(dact.reshape(s, DSWA_HEADS, DSWA_E), (1, 0, 2))
            do_n, dlse_n = _dswa_combine_bwd(sv["o_n"], sv["lse_n"], dc, name=f"dswa_comb_bwd{i}")
            do_p, dlse_p = _dswa_permute_heads(do_n), _dswa_permute_heads(dlse_n)
            dq_p, dk_p, dv_p, dbias = _dswa_attn_bwd(sv["qkvp"], bias, sv["o_p"], sv["lse_p"], do_p, dlse_p,
                                                     name=f"dswa_attn_bwd{i}")
            d_table_t = d_table_t + _dswa_dtable(dbias.reshape(DSWA_HEADS, -1), onehot, name=f"dswa_dtable{i}")
            hf = DSWA_HALF
            dqkv_p = jnp.stack([dq_p, dk_p[:, hf:-hf], dv_p[:, hf:-hf]])
            dqkv_n = jnp.stack([_dswa_unpermute(dqkv_p[t]) for t in range(3)])
            dqkv = jnp.transpose(dqkv_n, (2, 0, 1, 3)).reshape(s, 3 * DSWA_WIDTH).astype(BF16)
            g_big["dswa_w_in"][j] = _mm(sv["h"], dqkv, ta=True, name=f"dswa_w_in_grad{i}")
            dh = _mm(dqkv, w_in, tb=True, name=f"dswa_proj_bwd{i}")
        dcur, g_norm_mix[i] = _rms_bwd(sv["x_in"], norm_mix[i], dh, dmid, name=f"rms_mix_bwd{i}")

    rep = dict(norm_mix=jnp.concatenate(g_norm_mix, axis=0), norm_mlp=jnp.concatenate(g_norm_mlp, axis=0),
               norm_final=dg_final.reshape(-1), rel_bias=d_table_t.T,
               gdn_a_log=jnp.stack(g_alog).reshape(gdn_a_log.shape), gdn_dt_bias=jnp.stack(g_dt).reshape(gdn_dt_bias.shape),
               gdn_norm_w=jnp.stack(g_nw).reshape(gdn_norm_w.shape))
    return loss_part, dcur, g_big, rep, g_conv
```

```python
import functools
import math

import jax
import jax.numpy as jnp
import numpy as np
from jax import lax
from jax.experimental import pallas as pl
from jax.experimental.pallas import tpu as pltpu

F32 = jnp.float32
BF16 = jnp.bfloat16
HP = lax.Precision.HIGHEST

N_DEV = 8
D_MODEL = 1024
DEPTH = 4
RMS_EPS = 1e-6
NEG_INF = -1e30

GDN_HEADS = 8
GDN_DK = 128
GDN_CONV = 5
GDN_CHUNK = 64
GDN_QKV = 3 * GDN_HEADS * GDN_DK
GDN_MAIN = GDN_QKV + GDN_HEADS * GDN_DK
GDN_AB = 4 * GDN_HEADS

DSWA_DILS = (1, 4, 16)
DSWA_HG = 6
DSWA_E = 64
DSWA_HEADS = 18
DSWA_WIDTH = DSWA_HEADS * DSWA_E
DSWA_HALF = 64
REL_BUCKETS = 32
REL_MAX_DIST = 1024

ADAM_LR = 0.001
ADAM_B1 = 0.9
ADAM_B2 = 0.999
ADAM_EPS = 1e-08
ADAM_WD = 0.01
ADAM_STEP = 10

VMEM_LIMIT = 56 * 1024 * 1024


def _cparams(sem=None, **kw):
    return pltpu.CompilerParams(dimension_semantics=sem, vmem_limit_bytes=VMEM_LIMIT, **kw)


def _pick(dim, cands):
    for c in cands:
        if dim % c == 0:
            return c
    return dim


def _bdot(a, b):
    return jnp.dot(a.astype(BF16), b.astype(BF16), preferred_element_type=F32)


def _bdot_nt(a, b):
    return lax.dot_general(a.astype(BF16), b.astype(BF16), (((1,), (1,)), ((), ())),
                           preferred_element_type=F32)


def _bdot_tn(a, b):
    return lax.dot_general(a.astype(BF16), b.astype(BF16), (((0,), (0,)), ((), ())),
                           preferred_element_type=F32)


def _hdot(a, b):
    return jnp.dot(a, b, precision=HP, preferred_element_type=F32)


def _hdot_tn(a, b):
    return lax.dot_general(a, b, (((0,), (0,)), ((), ())), precision=HP, preferred_element_type=F32)


def _hdot_nt(a, b):
    return lax.dot_general(a, b, (((1,), (1,)), ((), ())), precision=HP, preferred_element_type=F32)


def _sigmoid(x):
    return 1.0 / (1.0 + jnp.exp(-x))


def _mm(a, b, *, name, ta=False, tb=False, out_dtypes=(F32,), epilogue=None, extras=(),
        tm=None, tn=None, tk=None, shard=None):
    if ta:
        kdim, m = a.shape
    else:
        m, kdim = a.shape
    n = b.shape[0] if tb else b.shape[1]
    if shard == "rows":
        tm = m // N_DEV if (m // N_DEV) % 128 == 0 else m
    if shard == "cols":
        tn = n // N_DEV
    tm = tm or _pick(m, (512, 256, 128))
    tn = tn or _pick(n, (512, 384, 256, 128))
    tk = tk or _pick(kdim, (1024, 512, 384, 256, 128))
    nk = kdim // tk
    n_out = len(out_dtypes)
    n_ex = len(extras)
    rows_all = shard == "rows" and tm == m

    def body(*refs):
        a_ref, b_ref = refs[0], refs[1]
        ex_refs = refs[2:2 + n_ex]
        out_refs = refs[2 + n_ex:2 + n_ex + n_out]
        acc_ref = refs[-1]
        k = pl.program_id(2)

        @pl.when(k == 0)
        def _():
            acc_ref[...] = jnp.zeros_like(acc_ref)

        av = a_ref[...].astype(BF16)
        bv = b_ref[...].astype(BF16)
        dims = (((0 if ta else 1,), (1 if tb else 0,)), ((), ()))
        acc_ref[...] += lax.dot_general(av, bv, dims, preferred_element_type=F32)

        @pl.when(k == nk - 1)
        def _():
            acc = acc_ref[...]
            outs = (acc,) if epilogue is None else epilogue(acc, *[r[...] for r in ex_refs])
            for r, o in zip(out_refs, outs):
                if rows_all:
                    for p in range(N_DEV):
                        r[p] = o[p * (m // N_DEV):(p + 1) * (m // N_DEV)].astype(r.dtype)
                else:
                    r[...] = o.astype(r.dtype)

    a_spec = pl.BlockSpec((tk, tm), lambda i, j, k: (k, i)) if ta else pl.BlockSpec((tm, tk), lambda i, j, k: (i, k))
    b_spec = pl.BlockSpec((tn, tk), lambda i, j, k: (j, k)) if tb else pl.BlockSpec((tk, tn), lambda i, j, k: (k, j))
    o_spec = pl.BlockSpec((tm, tn), lambda i, j, k: (i, j))
    out_specs = [o_spec] * n_out
    out_shape = [jax.ShapeDtypeStruct((m, n), dt) for dt in out_dtypes]
    if shard == "rows":
        out_shape = [jax.ShapeDtypeStruct((N_DEV, m // N_DEV, n), out_dtypes[0])]
        out_specs = [pl.BlockSpec((N_DEV, m // N_DEV, tn), lambda i, j, k: (0, 0, j)) if rows_all
                     else pl.BlockSpec((None, tm, tn), lambda i, j, k: (i, 0, j))]
    if shard == "cols":
        out_shape = [jax.ShapeDtypeStruct((N_DEV, m, tn), out_dtypes[0])]
        out_specs = [pl.BlockSpec((None, tm, tn), lambda i, j, k: (j, i, 0))]
    outs = pl.pallas_call(
        body, name=name,
        grid=(m // tm, n // tn, nk),
        in_specs=[a_spec, b_spec] + [o_spec] * n_ex,
        out_specs=out_specs,
        out_shape=out_shape,
        scratch_shapes=[pltpu.VMEM((tm, tn), F32)],
        compiler_params=_cparams(("parallel", "parallel", "arbitrary")),
    )(a, b, *extras)
    return outs[0] if n_out == 1 else outs


def _rms_fwd(x, g, *, name):
    s, d = x.shape
    tr = _pick(s, (512, 256, 128))

    def body(x_ref, g_ref, h_ref):
        xv = x_ref[...]
        r = lax.rsqrt(jnp.mean(xv * xv, axis=-1, keepdims=True) + RMS_EPS)
        h_ref[...] = (xv * r * g_ref[...]).astype(h_ref.dtype)

    return pl.pallas_call(
        body, name=name, grid=(s // tr,),
        in_specs=[pl.BlockSpec((tr, d), lambda i: (i, 0)), pl.BlockSpec((1, d), lambda i: (0, 0))],
        out_specs=pl.BlockSpec((tr, d), lambda i: (i, 0)),
        out_shape=jax.ShapeDtypeStruct((s, d), BF16),
        compiler_params=_cparams(("parallel",)),
    )(x, g.reshape(1, d))


def _rms_bwd(x, g, dh, dres, *, name):
    s, d = x.shape
    tr = _pick(s, (512, 256, 128))

    def body(x_ref, g_ref, dh_ref, dres_ref, dx_ref, dg_ref):
        i = pl.program_id(0)
        xv = x_ref[...]
        r = lax.rsqrt(jnp.mean(xv * xv, axis=-1, keepdims=True) + RMS_EPS)
        xn = xv * r
        dhv = dh_ref[...]
        dn = dhv * g_ref[...]
        dx_ref[...] = dres_ref[...] + r * (dn - xn * jnp.mean(dn * xn, axis=-1, keepdims=True))
        part = jnp.sum(dhv * xn, axis=0, keepdims=True)

        @pl.when(i == 0)
        def _():
            dg_ref[...] = part

        @pl.when(i > 0)
        def _():
            dg_ref[...] += part

    row = pl.BlockSpec((tr, d), lambda i: (i, 0))
    vec = pl.BlockSpec((1, d), lambda i: (0, 0))
    return pl.pallas_call(
        body, name=name, grid=(s // tr,),
        in_specs=[row, vec, row, row], out_specs=[row, vec],
        out_shape=[jax.ShapeDtypeStruct((s, d), F32), jax.ShapeDtypeStruct((1, d), F32)],
        compiler_params=_cparams(("arbitrary",)),
    )(x, g.reshape(1, d), dh, dres)


def _loss_head(x, g, target, *, name):
    s, d = x.shape
    tr = _pick(s, (512, 256, 128))

    def body(x_ref, g_ref, t_ref, loss_ref, dx_ref, dg_ref):
        i = pl.program_id(0)
        xv = x_ref[...]
        gv = g_ref[...]
        r = lax.rsqrt(jnp.mean(xv * xv, axis=-1, keepdims=True) + RMS_EPS)
        xn = xv * r
        err = xn * gv - t_ref[...]
        lpart = 0.5 * jnp.sum(jnp.mean(err * err, axis=-1, keepdims=True), axis=0, keepdims=True)
        dy = err * (1.0 / d)
        dn = dy * gv
        dx_ref[...] = r * (dn - xn * jnp.mean(dn * xn, axis=-1, keepdims=True))
        gpart = jnp.sum(dy * xn, axis=0, keepdims=True)

        @pl.when(i == 0)
        def _():
            dg_ref[...] = gpart
            loss_ref[...] = lpart

        @pl.when(i > 0)
        def _():
            dg_ref[...] += gpart
            loss_ref[...] += lpart

    row = pl.BlockSpec((tr, d), lambda i: (i, 0))
    vec = pl.BlockSpec((1, d), lambda i: (0, 0))
    one = pl.BlockSpec((1, 1), lambda i: (0, 0))
    return pl.pallas_call(
        body, name=name, grid=(s // tr,),
        in_specs=[row, vec, row], out_specs=[one, row, vec],
        out_shape=[jax.ShapeDtypeStruct((1, 1), F32), jax.ShapeDtypeStruct((s, d), F32),
                   jax.ShapeDtypeStruct((1, d), F32)],
        compiler_params=_cparams(("arbitrary",)),
    )(x, g.reshape(1, d), target)


def _shift_rows(x, sft, rows):
    s = x.shape[0]
    if sft == 0:
        return x
    y = pltpu.roll(x, (-sft) % s, 0)
    ok = (rows + sft >= 0) & (rows + sft < s)
    return jnp.where(ok, y, 0.0)


def _gdn_pre_fwd(proj, conv_w, *, name):
    s = proj.shape[0]
    nblk = GDN_QKV // 128
    pad = GDN_CONV // 2

    def body(x_ref, w_ref, o_ref):
        j = pl.program_id(0)
        x = x_ref[...]
        rows = lax.broadcasted_iota(jnp.int32, x.shape, 0)
        c = jnp.zeros_like(x)
        for t in range(GDN_CONV):
            c = c + w_ref[pl.ds(t, 1), :] * _shift_rows(x, t - pad, rows)
        a = c * _sigmoid(c)
        rinv = lax.rsqrt(jnp.sum(a * a, axis=-1, keepdims=True) + 1e-6)
        scale = jnp.where(j < GDN_HEADS, GDN_DK ** -0.5, 1.0)
        o_ref[...] = jnp.where(j >= 2 * GDN_HEADS, a, a * (rinv * scale))

    return pl.pallas_call(
        body, name=name, grid=(nblk,),
        in_specs=[pl.BlockSpec((s, 128), lambda j: (0, j)), pl.BlockSpec((GDN_CONV, 128), lambda j: (0, j))],
        out_specs=pl.BlockSpec((s, 128), lambda j: (0, j)),
        out_shape=jax.ShapeDtypeStruct((s, GDN_QKV), F32),
        compiler_params=_cparams(("parallel",)),
    )(proj, conv_w)


def _gdn_pre_bwd(proj, conv_w, dqkv, *, name):
    s = proj.shape[0]
    nblk = GDN_QKV // 128
    pad = GDN_CONV // 2

    def body(x_ref, w_ref, d_ref, dx_ref, dw_ref):
        j = pl.program_id(0)
        x = x_ref[...]
        rows = lax.broadcasted_iota(jnp.int32, x.shape, 0)
        xs = [_shift_rows(x, t - pad, rows) for t in range(GDN_CONV)]
        c = jnp.zeros_like(x)
        for t in range(GDN_CONV):
            c = c + w_ref[pl.ds(t, 1), :] * xs[t]
        sg = _sigmoid(c)
        a = c * sg
        rinv = lax.rsqrt(jnp.sum(a * a, axis=-1, keepdims=True) + 1e-6)
        scale = jnp.where(j < GDN_HEADS, GDN_DK ** -0.5, 1.0)
        dy = d_ref[0] + d_ref[1]
        nh = a * rinv
        da_n = (rinv * scale) * (dy - nh * jnp.sum(dy * nh, axis=-1, keepdims=True))
        da = jnp.where(j >= 2 * GDN_HEADS, dy, da_n)
        dc = da * (sg * (1.0 + c * (1.0 - sg)))
        dx = jnp.zeros_like(x)
        for t in range(GDN_CONV):
            dx = dx + w_ref[pl.ds(t, 1), :] * _shift_rows(dc, pad - t, rows)
            dw_ref[pl.ds(t, 1), :] = jnp.sum(dc * xs[t], axis=0, keepdims=True)
        dx_ref[...] = dx.astype(dx_ref.dtype)

    col = pl.BlockSpec((s, 128), lambda j: (0, j))
    wsp = pl.BlockSpec((GDN_CONV, 128), lambda j: (0, j))
    return pl.pallas_call(
        body, name=name, grid=(nblk,),
        in_specs=[col, wsp, pl.BlockSpec((2, s, 128), lambda j: (0, 0, j))], out_specs=[col, wsp],
        out_shape=[jax.ShapeDtypeStruct((s, GDN_QKV), BF16), jax.ShapeDtypeStruct((GDN_CONV, GDN_QKV), F32)],
        compiler_params=_cparams(("parallel",)),
    )(proj, conv_w, dqkv)


def _softplus(x):
    return jnp.maximum(x, 0.0) + jnp.log(1.0 + jnp.exp(-jnp.abs(x)))


def _gdn_gate_fwd(a, b, a_log, dt_bias, *, name):
    s = a.shape[0]
    nh = 2 * GDN_HEADS

    def body(a_ref, b_ref, al_ref, dt_ref, g_ref, be_ref):
        g_ref[...] = -jnp.exp(al_ref[...]) * _softplus(a_ref[...] + dt_ref[...])
        be_ref[...] = _sigmoid(b_ref[...])

    return pl.pallas_call(
        body, name=name,
        out_shape=[jax.ShapeDtypeStruct((s, nh), F32), jax.ShapeDtypeStruct((s, nh), F32)],
        compiler_params=_cparams(),
    )(a, b, a_log.reshape(1, nh), dt_bias.reshape(1, nh))


def _gdn_gate_bwd(a, b, a_log, dt_bias, dg, dbeta, *, name):
    s = a.shape[0]
    nh = 2 * GDN_HEADS

    def body(a_ref, b_ref, al_ref, dt_ref, dg_ref, db_ref, da_ref, dbb_ref, dal_ref, ddt_ref):
        ea = jnp.exp(al_ref[...])
        z = a_ref[...] + dt_ref[...]
        dgv = dg_ref[...]
        dz = dgv * (-ea) * _sigmoid(z)
        dal_ref[...] = jnp.sum(dgv * (-ea) * _softplus(z), axis=0, keepdims=True)
        ddt_ref[...] = jnp.sum(dz, axis=0, keepdims=True)
        sb = _sigmoid(b_ref[...])
        da_ref[...] = dz
        dbb_ref[...] = db_ref[...] * sb * (1.0 - sb)

    return pl.pallas_call(
        body, name=name,
        out_shape=[jax.ShapeDtypeStruct((s, nh), F32), jax.ShapeDtypeStruct((s, nh), F32),
                   jax.ShapeDtypeStruct((1, nh), F32), jax.ShapeDtypeStruct((1, nh), F32)],
        compiler_params=_cparams(),
    )(a, b, a_log.reshape(1, nh), dt_bias.reshape(1, nh), dg, dbeta)


def _chunk_masks(d):
    c = GDN_CHUNK
    ii = lax.broadcasted_iota(jnp.int32, (c, c), 0)
    jj = lax.broadcasted_iota(jnp.int32, (c, c), 1)
    dif = (ii - jj) * (1 - 2 * d)
    mi = dif >= 0
    mit = dif <= 0
    ms = dif > 0
    eye = ii == jj
    bds = [(ii >> sh) == (jj >> sh) for sh in (3, 4, 5)]
    return dict(mi=mi, mit=mit, ms=ms, eye=eye, bds=bds,
                mif=mi.astype(F32), mitf=mit.astype(F32), eyef=eye.astype(F32))


class _V:
    def __init__(self, xs):
        self.xs = tuple(xs)

    def __add__(self, o):
        return _lift(lambda a, b: a + b)(self, o)

    def __radd__(self, o):
        return _lift(lambda a, b: b + a)(self, o)

    def __sub__(self, o):
        return _lift(lambda a, b: a - b)(self, o)

    def __rsub__(self, o):
        return _lift(lambda a, b: b - a)(self, o)

    def __mul__(self, o):
        return _lift(lambda a, b: a * b)(self, o)

    def __rmul__(self, o):
        return _lift(lambda a, b: b * a)(self, o)

    def __and__(self, o):
        return _lift(lambda a, b: a & b)(self, o)

    def __neg__(self):
        return _lift(lambda a: -a)(self)


def _lift(f):
    def g(*args, **kw):
        n = next(len(a.xs) for a in args if isinstance(a, _V))
        return _V(f(*[a.xs[i] if isinstance(a, _V) else a for a in args], **kw) for i in range(n))
    return g


_vwhere, _vsum, _vexp, _vnot = _lift(jnp.where), _lift(jnp.sum), _lift(jnp.exp), _lift(jnp.logical_not)
_vhdot, _vhdot_tn = _lift(_bdot), _lift(_bdot_tn)
_vbdot, _vbdot_nt, _vbdot_tn = _lift(_bdot), _lift(_bdot_nt), _lift(_bdot_tn)
_vcat = _lift(lambda a, b: jnp.concatenate([a, b], axis=1))
_vlo = _lift(lambda a, n: a[:, :n])
_vhi = _lift(lambda a, n: a[:, n:])


def _stack_masks(d, n):
    m = _chunk_masks(d)
    mk = {key: _V([m[key]] * n) for key in m if key != "bds"}
    mk["bds"] = [_V([m["bds"][i]] * n) for i in range(3)]
    return mk


def _tri_inv(a, mk):
    eyef = mk["eyef"]
    bd8, bd16, bd32 = mk["bds"]
    a8 = _vwhere(bd8, a, 0.0)
    a2 = _vhdot(a8, a8)
    a4 = _vhdot(a2, a2)
    t = _vhdot(_vhdot(eyef - a8, eyef + a2), eyef + a4)
    for inner, outer in ((bd8, bd16), (bd16, bd32), (bd32, None)):
        off = _vnot(inner) if outer is None else (outer & _vnot(inner))
        low = _vwhere(off, a, 0.0)
        t = t - _vhdot(_vhdot(t, low), t)
    return t


def _chunk_prep(q, k, v, g_row, b_row, mk):
    dv = GDN_DK
    g_col = _vsum(mk["eyef"] * g_row, axis=1, keepdims=True)
    b_col = _vsum(mk["eyef"] * b_row, axis=1, keepdims=True)
    gc_col = _vsum(mk["mif"] * g_row, axis=1, keepdims=True)
    gc_row = _vsum(mk["mitf"] * g_col, axis=0, keepdims=True)
    gl = _vsum(g_row, axis=1, keepdims=True)
    decay = _vwhere(mk["mi"], _vexp(_vwhere(mk["mi"], gc_col - gc_row, 0.0)), 0.0)
    eg = _vexp(gc_col)
    e2 = _vexp(gl - gc_col)
    egl = _vexp(gl)
    kb = k * b_col
    pm = _vbdot_nt(kb, k)
    a = _vwhere(mk["ms"], pm * decay, 0.0)
    t = _tri_inv(a, mk)
    sol = _vhdot(t, _vcat(v * b_col, kb * eg))
    u, w = _vlo(sol, dv), _vhi(sol, dv)
    qm = _vbdot_nt(q, k)
    return dict(b_col=b_col, decay=decay, eg=eg, e2=e2, egl=egl, kb=kb, pm=pm, t=t, u=u, w=w,
                qm=qm, intra=qm * decay, qd=q * eg, kd=k * e2)


def _chunk_fwd_step(p, state):
    v_new = p["u"] - _vbdot(p["w"], state)
    o = _vbdot(p["qd"], state) + _vbdot(p["intra"], v_new)
    new_state = state * p["egl"] + _vbdot_tn(p["kd"], v_new)
    return o, new_state


def _chunk_bwd_step(q, k, v, p, mk, state, dso, do):
    dv_dim = GDN_DK
    v_new = p["u"] - _vbdot(p["w"], state)
    dvn = _vbdot_tn(p["intra"], do) + _vbdot(p["kd"], dso)
    dintra = _vbdot_nt(do, v_new)
    dqd = _vbdot_nt(do, state)
    ds = p["egl"] * dso + _vbdot_tn(p["qd"], do) - _vbdot_tn(p["w"], dvn)
    dkd = _vbdot_nt(v_new, dso)
    dgl = _vsum(_vsum(dso * state, axis=1, keepdims=True), axis=0, keepdims=True) * p["egl"]
    dw = -_vbdot_nt(dvn, state)
    drhs = _vhdot_tn(p["t"], _vcat(dvn, dw))
    dru, drw = _vlo(drhs, dv_dim), _vhi(drhs, dv_dim)
    da = -_vwhere(mk["ms"], _vbdot_nt(drhs, _vcat(p["u"], p["w"])), 0.0)
    b_col = p["b_col"]
    dv = dru * b_col
    dbeta = _vsum(dru * v, axis=1, keepdims=True)
    dkb = drw * p["eg"]
    deg = _vsum(drw * p["kb"], axis=1, keepdims=True)
    dp = da * p["decay"]
    ddecay = da * p["pm"]
    dkb = dkb + _vbdot(dp, k)
    dk = _vbdot_tn(dp, p["kb"])
    dqm = dintra * p["decay"]
    ddecay = ddecay + dintra * p["qm"]
    dq = _vbdot(dqm, k)
    dk = dk + _vbdot_tn(dqm, q)
    dd = ddecay * p["decay"]
    dgc_col = _vsum(dd, axis=1, keepdims=True)
    dgc_row = -_vsum(dd, axis=0, keepdims=True)
    dq = dq + dqd * p["eg"]
    deg = deg + _vsum(dqd * q, axis=1, keepdims=True)
    dk = dk + dkd * p["e2"]
    de2 = _vsum(dkd * k, axis=1, keepdims=True) * p["e2"]
    dgl = dgl + _vsum(de2, axis=0, keepdims=True)
    dgc_col = dgc_col - de2 + deg * p["eg"]
    dk = dk + dkb * b_col
    dbeta = dbeta + _vsum(dkb * k, axis=1, keepdims=True)
    dgc_col = dgc_col + _vsum(mk["eyef"] * dgc_row, axis=1, keepdims=True)
    dg_row = _vsum(mk["mif"] * dgc_col, axis=0, keepdims=True) + dgl
    dbeta_row = _vsum(mk["eyef"] * dbeta, axis=0, keepdims=True)
    return dq, dk, dv, dg_row, dbeta_row, ds


def _gdn_chunk_fwd(qkvn, g5, b5, *, name):
    s = qkvn.shape[0]
    c = GDN_CHUNK
    nc = s // c
    h_, dk = GDN_HEADS, GDN_DK

    def body(x_ref, g_ref, b_ref, o_ref, st_ref, st_scr):
        d, n = pl.program_id(0), pl.program_id(1)

        @pl.when(n == 0)
        def _():
            st_scr[...] = jnp.zeros_like(st_scr)

        mk = _stack_masks(d, h_)
        q, k, v = (_V(x_ref[:, (t * h_ + h) * dk:(t * h_ + h + 1) * dk] for h in range(h_)) for t in range(3))
        g, b = (_V(r[0, h, 0] for h in range(h_)) for r in (g_ref, b_ref))
        state = _V(st_scr[h] for h in range(h_))
        o, new_state = _chunk_fwd_step(_chunk_prep(q, k, v, g, b, mk), state)
        for h in range(h_):
            st_ref[0, h, 0] = state.xs[h]
            st_scr[h] = new_state.xs[h]
            o_ref[0, :, h * dk:(h + 1) * dk] = o.xs[h]

    ce = lambda d, n: n + d * (nc - 1 - 2 * n)
    gate = pl.BlockSpec((1, h_, 1, 1, c), lambda d, n: (d, 0, ce(d, n), 0, 0))
    return pl.pallas_call(
        body, name=name, grid=(2, nc),
        in_specs=[pl.BlockSpec((c, 3 * h_ * dk), lambda d, n: (ce(d, n), 0)), gate, gate],
        out_specs=[pl.BlockSpec((1, c, h_ * dk), lambda d, n: (d, ce(d, n), 0)),
                   pl.BlockSpec((1, h_, 1, dk, dk), lambda d, n: (d, 0, ce(d, n), 0, 0))],
        out_shape=[jax.ShapeDtypeStruct((2, s, h_ * dk), F32), jax.ShapeDtypeStruct((2, h_, nc, dk, dk), F32)],
        scratch_shapes=[pltpu.VMEM((h_, dk, dk), F32)],
        compiler_params=_cparams(("arbitrary", "arbitrary")),
    )(qkvn, g5, b5)


def _gdn_chunk_bwd(qkvn, g5, b5, states, do, *, name):
    s = qkvn.shape[0]
    c = GDN_CHUNK
    nc = s // c
    h_, dk = GDN_HEADS, GDN_DK

    def body(x_ref, g_ref, b_ref, st_ref, do_ref, dx_ref, dg_ref, db_ref, ds_scr):
        d, i = pl.program_id(0), pl.program_id(1)

        @pl.when(i == 0)
        def _():
            ds_scr[...] = jnp.zeros_like(ds_scr)

        mk = _stack_masks(d, h_)
        q, k, v = (_V(x_ref[:, (t * h_ + h) * dk:(t * h_ + h + 1) * dk] for h in range(h_)) for t in range(3))
        g, b = (_V(r[0, h, 0] for h in range(h_)) for r in (g_ref, b_ref))
        state = _V(st_ref[0, h, 0] for h in range(h_))
        dso = _V(ds_scr[h] for h in range(h_))
        dov = _V(do_ref[:, h * dk:(h + 1) * dk] for h in range(h_))
        res = _chunk_bwd_step(q, k, v, _chunk_prep(q, k, v, g, b, mk), mk, state, dso, dov)
        for h, (dq, dkk, dvv, dg_r, db_r, ds) in enumerate(zip(*[r.xs for r in res])):
            ds_scr[h] = ds
            dg_ref[0, h, 0] = dg_r
            db_ref[0, h, 0] = db_r
            for t, val in enumerate((dq, dkk, dvv)):
                dx_ref[0, :, (t * h_ + h) * dk:(t * h_ + h + 1) * dk] = val

    ce = lambda d, i: (nc - 1 - i) + d * (2 * i - nc + 1)
    gate = pl.BlockSpec((1, h_, 1, 1, c), lambda d, i: (d, 0, ce(d, i), 0, 0))
    return pl.pallas_call(
        body, name=name, grid=(2, nc),
        in_specs=[pl.BlockSpec((c, 3 * h_ * dk), lambda d, i: (ce(d, i), 0)), gate, gate,
                  pl.BlockSpec((1, h_, 1, dk, dk), lambda d, i: (d, 0, ce(d, i), 0, 0)),
                  pl.BlockSpec((c, h_ * dk), lambda d, i: (ce(d, i), 0))],
        out_specs=[pl.BlockSpec((1, c, 3 * h_ * dk), lambda d, i: (d, ce(d, i), 0)), gate, gate],
        out_shape=[jax.ShapeDtypeStruct((2, s, 3 * h_ * dk), F32)]
        + [jax.ShapeDtypeStruct((2, h_, nc, 1, c), F32)] * 2,
        scratch_shapes=[pltpu.VMEM((h_, dk, dk), F32)],
        compiler_params=_cparams(("arbitrary", "arbitrary")),
    )(qkvn, g5, b5, states, do)


def _gdn_post_fwd(o, z, norm_w, *, name):
    s = o.shape[1]
    h_, dk = GDN_HEADS, GDN_DK

    def body(o_ref, z_ref, w_ref, a_ref):
        ov = o_ref[0] + o_ref[1]
        zv = z_ref[...]
        r = lax.rsqrt(jnp.mean(ov * ov, axis=-1, keepdims=True) + RMS_EPS)
        a_ref[...] = (ov * r * w_ref[...] * (zv * _sigmoid(zv))).astype(a_ref.dtype)

    return pl.pallas_call(
        body, name=name, grid=(h_,),
        in_specs=[pl.BlockSpec((2, s, dk), lambda h: (0, 0, h)), pl.BlockSpec((s, dk), lambda h: (0, 3 * h_ + h)),
                  pl.BlockSpec((1, dk), lambda h: (0, 0))],
        out_specs=pl.BlockSpec((s, dk), lambda h: (0, h)),
        out_shape=jax.ShapeDtypeStruct((s, h_ * dk), BF16),
        compiler_params=_cparams(("parallel",)),
    )(o, z, norm_w.reshape(1, dk))


def _gdn_post_bwd(o, z, norm_w, dact, *, name):
    s = o.shape[1]
    h_, dk = GDN_HEADS, GDN_DK

    def body(o_ref, z_ref, w_ref, da_ref, do_ref, dz_ref, dw_ref):
        h = pl.program_id(0)
        ov = o_ref[0] + o_ref[1]
        zv = z_ref[...]
        wv = w_ref[...]
        dav = da_ref[...]
        r = lax.rsqrt(jnp.mean(ov * ov, axis=-1, keepdims=True) + RMS_EPS)
        nrm = ov * r
        sg = _sigmoid(zv)
        sz = zv * sg
        dn = dav * wv * sz
        do_ref[...] = r * (dn - nrm * jnp.mean(dn * nrm, axis=-1, keepdims=True))
        dz_ref[...] = (dav * nrm * wv * (sg * (1.0 + zv * (1.0 - sg)))).astype(dz_ref.dtype)
        part = jnp.sum(dav * nrm * sz, axis=0, keepdims=True)

        @pl.when(h == 0)
        def _():
            dw_ref[...] = part

        @pl.when(h > 0)
        def _():
            dw_ref[...] += part

    col = pl.BlockSpec((s, dk), lambda h: (0, h))
    vec = pl.BlockSpec((1, dk), lambda h: (0, 0))
    return pl.pallas_call(
        body, name=name, grid=(h_,),
        in_specs=[pl.BlockSpec((2, s, dk), lambda h: (0, 0, h)), pl.BlockSpec((s, dk), lambda h: (0, 3 * h_ + h)), vec, col],
        out_specs=[col, col, vec],
        out_shape=[jax.ShapeDtypeStruct((s, h_ * dk), F32), jax.ShapeDtypeStruct((s, h_ * dk), BF16),
                   jax.ShapeDtypeStruct((1, dk), F32)],
        compiler_params=_cparams(("arbitrary",)),
    )(o, z, norm_w.reshape(1, dk), dact)


def _rel_bucket(rel):
    nb = REL_BUCKETS // 2
    max_exact = nb // 2
    ret = jnp.where(rel > 0, nb, 0)
    n = jnp.abs(rel)
    nf = jnp.maximum(n, 1).astype(F32)
    large = max_exact + (jnp.log(nf / max_exact) / math.log(REL_MAX_DIST / max_exact)
                         * (nb - max_exact)).astype(jnp.int32)
    large = jnp.minimum(large, nb - 1)
    return ret + jnp.where(n < max_exact, n, large)


def _bucket_onehot():
    half = DSWA_HALF
    outs = []
    for dil in DSWA_DILS:
        rel = (jnp.arange(3 * half)[None, :] - half - jnp.arange(half)[:, None]) * dil
        outs.append(jax.nn.one_hot(_rel_bucket(rel).reshape(-1), REL_BUCKETS, dtype=F32, axis=0))
    return jnp.stack(outs)


def _head_group_select(vals):
    rows = lax.broadcasted_iota(jnp.int32, vals[0].shape, 0)
    return jnp.where(rows < DSWA_HG, vals[0], jnp.where(rows < 2 * DSWA_HG, vals[1], vals[2]))


def _dswa_bias(table_t, onehot, *, name):
    p = onehot.shape[-1]

    def body(t_ref, oh_ref, b_ref):
        b_ref[...] = _head_group_select([_hdot(t_ref[...], oh_ref[g]) for g in range(3)])

    return pl.pallas_call(body, name=name, out_shape=jax.ShapeDtypeStruct((DSWA_HEADS, p), F32),
                          compiler_params=_cparams())(table_t, onehot)


def _dswa_dtable(dbias, onehot, *, name):
    def body(d_ref, oh_ref, t_ref):
        t_ref[...] = _head_group_select([_hdot_nt(d_ref[...], oh_ref[g]) for g in range(3)])

    return pl.pallas_call(body, name=name, out_shape=jax.ShapeDtypeStruct((DSWA_HEADS, REL_BUCKETS), F32),
                          compiler_params=_cparams())(dbias, onehot)


def _attn_valid(blk, h, s):
    half = DSWA_HALF
    nblocks = s // half
    nbs = jnp.where(h < DSWA_HG, nblocks // DSWA_DILS[0],
                    jnp.where(h < 2 * DSWA_HG, nblocks // DSWA_DILS[1], nblocks // DSWA_DILS[2]))
    b = blk & (nbs - 1)
    ii = lax.broadcasted_iota(jnp.int32, (half, 3 * half), 0)
    jj = lax.broadcasted_iota(jnp.int32, (half, 3 * half), 1)
    off = jj - half - ii
    return (jnp.abs(off) <= half) & ((jj >= half) | (b > 0)) & ((jj < 2 * half) | (b < nbs - 1))


def _dswa_attn_fwd(qkvp, bias, *, name):
    sp = qkvp.shape[2]
    half, e = DSWA_HALF, DSWA_E
    s = sp - 2 * half

    def body(qkv_ref, bias_ref, o_ref, lse_ref):
        h = pl.program_id(0)
        bias_v = bias_ref[0]

        def step(blk, carry):
            r0 = pl.multiple_of(blk * half, half)
            q = qkv_ref[0, 0, pl.ds(r0 + half, half), :]
            kw = qkv_ref[1, 0, pl.ds(r0, 3 * half), :]
            vw = qkv_ref[2, 0, pl.ds(r0, 3 * half), :]
            sc = _bdot_nt(q, kw) * (e ** -0.5) + bias_v
            sc = jnp.where(_attn_valid(blk, h, s), sc, NEG_INF)
            m = jnp.max(sc, axis=-1, keepdims=True)
            p = jnp.exp(sc - m)
            l = jnp.sum(p, axis=-1, keepdims=True)
            o_ref[0, pl.ds(r0, half), :] = _bdot(p / l, vw)
            lse_ref[0, pl.ds(r0, half), :] = m + jnp.log(l)
            return carry

        lax.fori_loop(0, s // half, step, 0)

    return pl.pallas_call(
        body, name=name, grid=(DSWA_HEADS,),
        in_specs=[pl.BlockSpec((3, 1, sp, e), lambda h: (0, h, 0, 0)),
                  pl.BlockSpec((1, half, 3 * half), lambda h: (h, 0, 0))],
        out_specs=[pl.BlockSpec((1, s, e), lambda h: (h, 0, 0)), pl.BlockSpec((1, s, 1), lambda h: (h, 0, 0))],
        out_shape=[jax.ShapeDtypeStruct((DSWA_HEADS, s, e), F32), jax.ShapeDtypeStruct((DSWA_HEADS, s, 1), F32)],
        compiler_params=_cparams(("parallel",)),
    )(qkvp, bias)


def _dswa_attn_bwd(qkvp, bias, o, lse, do, dlse, *, name):
    sp = qkvp.shape[2]
    half, e = DSWA_HALF, DSWA_E
    s = sp - 2 * half

    def body(qkv_ref, bias_ref, o_ref, lse_ref, do_ref, dlse_ref, dq_ref, dk_ref, dv_ref, db_ref):
        h = pl.program_id(0)
        bias_v = bias_ref[0]
        dk_ref[...] = jnp.zeros_like(dk_ref)
        dv_ref[...] = jnp.zeros_like(dv_ref)

        def step(blk, dbias):
            r0 = pl.multiple_of(blk * half, half)
            q = qkv_ref[0, 0, pl.ds(r0 + half, half), :]
            kw = qkv_ref[1, 0, pl.ds(r0, 3 * half), :]
            vw = qkv_ref[2, 0, pl.ds(r0, 3 * half), :]
            sc = _bdot_nt(q, kw) * (e ** -0.5) + bias_v
            valid = _attn_valid(blk, h, s)
            p = jnp.where(valid, jnp.exp(jnp.where(valid, sc, 0.0) - lse_ref[0, pl.ds(r0, half), :]), 0.0)
            dov = do_ref[0, pl.ds(r0, half), :]
            delta = jnp.sum(dov * o_ref[0, pl.ds(r0, half), :], axis=-1, keepdims=True)
            dp = _bdot_nt(dov, vw)
            dsc = p * (dp - delta + dlse_ref[0, pl.ds(r0, half), :])
            dq_ref[0, pl.ds(r0, half), :] = _bdot(dsc, kw) * (e ** -0.5)
            dk_ref[0, pl.ds(r0, 3 * half), :] += _bdot_tn(dsc, q) * (e ** -0.5)
            dv_ref[0, pl.ds(r0, 3 * half), :] += _bdot_tn(p, dov)
            return dbias + dsc

        db_ref[0] = lax.fori_loop(0, s // half, step, jnp.zeros((half, 3 * half), F32))

    hs = pl.BlockSpec((1, s, e), lambda h: (h, 0, 0))
    hp = pl.BlockSpec((1, sp, e), lambda h: (h, 0, 0))
    h1 = pl.BlockSpec((1, s, 1), lambda h: (h, 0, 0))
    bs = pl.BlockSpec((1, half, 3 * half), lambda h: (h, 0, 0))
    return pl.pallas_call(
        body, name=name, grid=(DSWA_HEADS,),
        in_specs=[pl.BlockSpec((3, 1, sp, e), lambda h: (0, h, 0, 0)), bs, hs, h1, hs, h1],
        out_specs=[hs, hp, hp, bs],
        out_shape=[jax.ShapeDtypeStruct((DSWA_HEADS, s, e), F32), jax.ShapeDtypeStruct((DSWA_HEADS, sp, e), F32),
                   jax.ShapeDtypeStruct((DSWA_HEADS, sp, e), F32),
                   jax.ShapeDtypeStruct((DSWA_HEADS, half, 3 * half), F32)],
        compiler_params=_cparams(("parallel",)),
    )(qkvp, bias, o, lse, do, dlse)


def _dswa_combine_fwd(o, lse, *, name):
    s = o.shape[1]
    tr = _pick(s, (512, 256, 128))

    def body(o_ref, l_ref, c_ref):
        for j in range(DSWA_HG):
            ls = [l_ref[g * DSWA_HG + j] for g in range(3)]
            m = jnp.maximum(jnp.maximum(ls[0], ls[1]), ls[2])
            es = [jnp.exp(x - m) for x in ls]
            den = es[0] + es[1] + es[2]
            for g in range(3):
                hh = g * DSWA_HG + j
                c_ref[hh] = (o_ref[hh] * (es[g] / den)).astype(c_ref.dtype)

    return pl.pallas_call(
        body, name=name, grid=(s // tr,),
        in_specs=[pl.BlockSpec((DSWA_HEADS, tr, DSWA_E), lambda i: (0, i, 0)),
                  pl.BlockSpec((DSWA_HEADS, tr, 1), lambda i: (0, i, 0))],
        out_specs=pl.BlockSpec((DSWA_HEADS, tr, DSWA_E), lambda i: (0, i, 0)),
        out_shape=jax.ShapeDtypeStruct((DSWA_HEADS, s, DSWA_E), BF16),
        compiler_params=_cparams(("parallel",)),
    )(o, lse)


def _dswa_combine_bwd(o, lse, dc, *, name):
    s = o.shape[1]
    tr = _pick(s, (512, 256, 128))

    def body(o_ref, l_ref, dc_ref, do_ref, dl_ref):
        for j in range(DSWA_HG):
            ls = [l_ref[g * DSWA_HG + j] for g in range(3)]
            m = jnp.maximum(jnp.maximum(ls[0], ls[1]), ls[2])
            es = [jnp.exp(x - m) for x in ls]
            den = es[0] + es[1] + es[2]
            al = [x / den for x in es]
            dal = []
            for g in range(3):
                hh = g * DSWA_HG + j
                dcv = dc_ref[hh]
                do_ref[hh] = dcv * al[g]
                dal.append(jnp.sum(dcv * o_ref[hh], axis=-1, keepdims=True))
            tot = al[0] * dal[0] + al[1] * dal[1] + al[2] * dal[2]
            for g in range(3):
                dl_ref[g * DSWA_HG + j] = al[g] * (dal[g] - tot)

    big = pl.BlockSpec((DSWA_HEADS, tr, DSWA_E), lambda i: (0, i, 0))
    one = pl.BlockSpec((DSWA_HEADS, tr, 1), lambda i: (0, i, 0))
    return pl.pallas_call(
        body, name=name, grid=(s // tr,),
        in_specs=[big, one, big], out_specs=[big, one],
        out_shape=[jax.ShapeDtypeStruct((DSWA_HEADS, s, DSWA_E), F32), jax.ShapeDtypeStruct((DSWA_HEADS, s, 1), F32)],
        compiler_params=_cparams(("parallel",)),
    )(o, lse, dc)


def _dswa_permute(t):
    s = t.shape[0]
    mid = t.shape[1:-2]
    x = t.shape[-1]
    parts = []
    for gi, dil in enumerate(DSWA_DILS):
        tg = t[..., gi * DSWA_HG:(gi + 1) * DSWA_HG, :].reshape((s // dil, dil) + mid + (DSWA_HG, x))
        nm = len(mid)
        perm = tuple(range(2, 2 + nm)) + (2 + nm, 1, 0, 3 + nm)
        parts.append(jnp.transpose(tg, perm).reshape(mid + (DSWA_HG, s, x)))
    return jnp.concatenate(parts, axis=len(mid))


def _dswa_unpermute(t):
    s, x = t.shape[1], t.shape[2]
    parts = []
    for gi, dil in enumerate(DSWA_DILS):
        tg = t[gi * DSWA_HG:(gi + 1) * DSWA_HG].reshape(DSWA_HG, dil, s // dil, x)
        parts.append(jnp.swapaxes(tg, 1, 2).reshape(DSWA_HG, s, x))
    return jnp.concatenate(parts, axis=0)


def _dswa_permute_heads(t):
    s, x = t.shape[1], t.shape[2]
    parts = []
    for gi, dil in enumerate(DSWA_DILS):
        tg = t[gi * DSWA_HG:(gi + 1) * DSWA_HG].reshape(DSWA_HG, s // dil, dil, x)
        parts.append(jnp.swapaxes(tg, 1, 2).reshape(DSWA_HG, s, x))
    return jnp.concatenate(parts, axis=0)


def _all_gather(shards, kinds, *, name):
    nref = len(shards)

    def out_shape(x, kind):
        if kind == "stack":
            return (N_DEV,) + x.shape
        if kind == "rows":
            return (x.shape[0], N_DEV * x.shape[1], x.shape[2])
        return (x.shape[0], x.shape[1], N_DEV * x.shape[2])

    def body(*refs):
        x_refs, out_refs = refs[:nref], refs[nref:2 * nref]
        send_sems, recv_sems, local_sems = refs[2 * nref:]
        mx, my, mc = lax.axis_index("x"), lax.axis_index("y"), lax.axis_index("c")
        me, sibling = (mx, my, mc), (mx, my, 1 - mc)
        chips = [(1 - mx, my), (mx, 1 - my), (1 - mx, 1 - my)]

        def slot(i, px, py, pc):
            p = 4 * px + 2 * py + pc
            if kinds[i] == "stack":
                return out_refs[i].at[p]
            if kinds[i] == "rows":
                ks = x_refs[i].shape[1]
                return out_refs[i].at[:, pl.ds(p * ks, ks), :]
            ns = x_refs[i].shape[2]
            return out_refs[i].at[:, :, pl.ds(p * ns, ns)]

        def copy(i, k, block, to, src=None):
            return pltpu.make_async_remote_copy(
                src_ref=slot(i, *block) if src is None else src, dst_ref=slot(i, *block),
                send_sem=send_sems.at[7 * i + k], recv_sem=recv_sems.at[7 * i + k],
                device_id=to, device_id_type=pl.DeviceIdType.MESH)

        rng = range(nref)
        mine = [pltpu.make_async_copy(x_refs[i], slot(i, *me), local_sems.at[i]) for i in rng]
        first = [copy(i, 0, me, sibling, src=x_refs[i]) for i in rng]
        first += [copy(i, 1 + j, me, (*chip, mc), src=x_refs[i]) for j, chip in enumerate(chips) for i in rng]
        for cp in mine + first:
            cp.start()
        passed = []
        for j, chip in enumerate(chips):
            for i in rng:
                copy(i, 1 + j, (*chip, mc), me).wait_recv()
                passed.append(copy(i, 4 + j, (*chip, mc), sibling))
                passed[-1].start()
        for i in rng:
            copy(i, 0, sibling, me).wait_recv()
            for j, chip in enumerate(chips):
                copy(i, 4 + j, (*chip, 1 - mc), me).wait_recv()
        for cp in first + passed:
            cp.wait_send()
        for cp in mine:
            cp.wait()

    anyspec = pl.BlockSpec(memory_space=pl.ANY)
    return pl.pallas_call(
        body, name=name,
        in_specs=[anyspec] * nref, out_specs=[anyspec] * nref,
        out_shape=[jax.ShapeDtypeStruct(out_shape(x, kd), x.dtype) for x, kd in zip(shards, kinds)],
        scratch_shapes=[pltpu.SemaphoreType.DMA((7 * nref,)), pltpu.SemaphoreType.DMA((7 * nref,)),
                        pltpu.SemaphoreType.DMA((nref,))],
    )(*shards)


def _exchange(sends, *, name):
    nref = len(sends)

    def body(*refs):
        s_refs, r_refs = refs[:nref], refs[nref:2 * nref]
        send_sems, recv_sems, local_sems = refs[2 * nref:]
        mx, my, mc = lax.axis_index("x"), lax.axis_index("y"), lax.axis_index("c")
        me = 4 * mx + 2 * my + mc
        mine = [pltpu.make_async_copy(s_refs[i].at[me], r_refs[i].at[me], local_sems.at[i]) for i in range(nref)]
        copies = []
        for k in range(1, N_DEV):
            px = 1 - mx if (k >> 2) & 1 else mx
            py = 1 - my if (k >> 1) & 1 else my
            pc = 1 - mc if k & 1 else mc
            peer = 4 * px + 2 * py + pc
            for i in range(nref):
                copies.append(pltpu.make_async_remote_copy(
                    src_ref=s_refs[i].at[peer], dst_ref=r_refs[i].at[me],
                    send_sem=send_sems.at[7 * i + k - 1], recv_sem=recv_sems.at[7 * i + k - 1],
                    device_id=(px, py, pc), device_id_type=pl.DeviceIdType.MESH))
        for cp in mine + copies:
            cp.start()
        for cp in copies:
            cp.wait_recv()
        for cp in copies:
            cp.wait_send()
        for cp in mine:
            cp.wait()

    anyspec = pl.BlockSpec(memory_space=pl.ANY)
    return pl.pallas_call(
        body, name=name,
        in_specs=[anyspec] * nref, out_specs=[anyspec] * nref,
        out_shape=[jax.ShapeDtypeStruct(x.shape, x.dtype) for x in sends],
        scratch_shapes=[pltpu.SemaphoreType.DMA((7 * nref,)), pltpu.SemaphoreType.DMA((7 * nref,)),
                        pltpu.SemaphoreType.DMA((nref,))],
    )(*sends)


def _adamw_update(g, w, m, v):
    mn = ADAM_B1 * m + (1.0 - ADAM_B1) * g
    vn = ADAM_B2 * v + (1.0 - ADAM_B2) * (g * g)
    m_hat = mn / (1.0 - ADAM_B1 ** ADAM_STEP)
    v_hat = vn / (1.0 - ADAM_B2 ** ADAM_STEP)
    return -ADAM_LR * (m_hat / (jnp.sqrt(v_hat) + ADAM_EPS) + ADAM_WD * w), mn, vn


def _adamw_layers(recvs, w, m, v, *, name):
    nl, ks, ns = w.shape
    tr = _pick(ks, (64, 48))

    def body(*refs):
        rv_refs = refs[:nl]
        w_ref, m_ref, v_ref, g_ref, d_ref, nm_ref, nv_ref = refs[nl:]
        for l in range(nl):
            g = rv_refs[l][0].astype(F32)
            for q in range(1, N_DEV):
                g = g + rv_refs[l][q].astype(F32)
            delta, mn, vn = _adamw_update(g, w_ref[l], m_ref[l], v_ref[l])
            g_ref[l] = g
            d_ref[l] = delta
            nm_ref[l] = mn
            nv_ref[l] = vn

    row = pl.BlockSpec((nl, tr, ns), lambda i: (0, i, 0))
    return pl.pallas_call(
        body, name=name, grid=(ks // tr,),
        in_specs=[pl.BlockSpec((N_DEV, tr, ns), lambda i: (0, i, 0))] * nl + [row] * 3,
        out_specs=[row] * 4,
        out_shape=[jax.ShapeDtypeStruct((nl, ks, ns), F32)] * 4,
        compiler_params=_cparams(("parallel",)),
    )(*recvs, w, m, v)


def _adamw_reduce(recv, w, m, v, *, name):
    r, c = w.shape
    tr = _pick(r, (128, 64, 8))

    def body(rv_ref, w_ref, m_ref, v_ref, g_ref, d_ref, nm_ref, nv_ref):
        g = rv_ref[0]
        for q in range(1, N_DEV):
            g = g + rv_ref[q]
        delta, mn, vn = _adamw_update(g, w_ref[...], m_ref[...], v_ref[...])
        g_ref[...] = g
        d_ref[...] = delta
        nm_ref[...] = mn
        nv_ref[...] = vn

    row = pl.BlockSpec((tr, c), lambda i: (i, 0))
    return pl.pallas_call(
        body, name=name, grid=(r // tr,),
        in_specs=[pl.BlockSpec((N_DEV, tr, c), lambda i: (0, i, 0)), row, row, row],
        out_specs=[row] * 4,
        out_shape=[jax.ShapeDtypeStruct((r, c), F32)] * 4,
        compiler_params=_cparams(("parallel",)),
    )(recv, w, m, v)


_BIG = ("gdn_w_in", "gdn_w_out", "dswa_w_in", "dswa_w_out", "mlp_w1", "mlp_w2")
_SMALL = ("gdn_conv_w", "norm_mix", "norm_mlp", "norm_final", "rel_bias", "gdn_a_log", "gdn_dt_bias", "gdn_norm_w")
_ORDER = ("norm_mix", "norm_mlp", "norm_final", "rel_bias", "gdn_w_in", "gdn_conv_w", "gdn_a_log", "gdn_dt_bias",
          "gdn_norm_w", "gdn_w_out", "dswa_w_in", "dswa_w_out", "mlp_w1", "mlp_w2")
_KIND = dict(gdn_w_in="stack", gdn_w_out="rows", dswa_w_in="stack", dswa_w_out="rows", mlp_w1="cols", mlp_w2="rows")


def _pack_rows(arrs, align):
    rows, counts = [], []
    for a in arrs:
        flat = a.reshape(-1)
        n = -(-flat.shape[0] // D_MODEL)
        flat = jnp.pad(flat, (0, n * D_MODEL - flat.shape[0]))
        rows.append(flat.reshape(n, D_MODEL))
        counts.append(n)
    out = jnp.concatenate(rows, axis=0)
    total = -(-out.shape[0] // align) * align
    return jnp.pad(out, ((0, total - out.shape[0]), (0, 0))), counts


def _unpack_rows(slab, shapes):
    outs, r = [], 0
    for shp in shapes:
        size = int(np.prod(shp))
        n = -(-size // D_MODEL)
        outs.append(slab[r:r + n].reshape(-1)[:size].reshape(shp))
        r += n
    return outs


def _col_shards(full, nshard):
    lead = full.shape[:-1]
    n = full.shape[-1] // nshard
    t = full.reshape(lead + (nshard, n))
    return jnp.moveaxis(t, -2, 0)


def _from_col_shards(g):
    t = jnp.moveaxis(g, 0, -2)
    return t.reshape(t.shape[:-2] + (t.shape[-2] * t.shape[-1],))


def kernel(x, norm_mix, norm_mlp, norm_final, rel_bias, gdn_w_in, gdn_conv_w, gdn_a_log, gdn_dt_bias, gdn_norm_w, gdn_w_out, dswa_w_in, dswa_w_out, mlp_w1, mlp_w2, loss_target, m_norm_mix, m_norm_mlp, m_norm_final, m_rel_bias, m_gdn_w_in, m_gdn_conv_w, m_gdn_a_log, m_gdn_dt_bias, m_gdn_norm_w, m_gdn_w_out, m_dswa_w_in, m_dswa_w_out, m_mlp_w1, m_mlp_w2, v_norm_mix, v_norm_mlp, v_norm_final, v_rel_bias, v_gdn_w_in, v_gdn_conv_w, v_gdn_a_log, v_gdn_dt_bias, v_gdn_norm_w, v_gdn_w_out, v_dswa_w_in, v_dswa_w_out, v_mlp_w1, v_mlp_w2):
    params = dict(norm_mix=norm_mix, norm_mlp=norm_mlp, norm_final=norm_final, rel_bias=rel_bias,
                  gdn_w_in=gdn_w_in, gdn_conv_w=gdn_conv_w, gdn_a_log=gdn_a_log, gdn_dt_bias=gdn_dt_bias,
                  gdn_norm_w=gdn_norm_w, gdn_w_out=gdn_w_out, dswa_w_in=dswa_w_in, dswa_w_out=dswa_w_out,
                  mlp_w1=mlp_w1, mlp_w2=mlp_w2)
    mom_m = dict(norm_mix=m_norm_mix, norm_mlp=m_norm_mlp, norm_final=m_norm_final, rel_bias=m_rel_bias,
                 gdn_w_in=m_gdn_w_in, gdn_conv_w=m_gdn_conv_w, gdn_a_log=m_gdn_a_log, gdn_dt_bias=m_gdn_dt_bias,
                 gdn_norm_w=m_gdn_norm_w, gdn_w_out=m_gdn_w_out, dswa_w_in=m_dswa_w_in, dswa_w_out=m_dswa_w_out,
                 mlp_w1=m_mlp_w1, mlp_w2=m_mlp_w2)
    mom_v = dict(norm_mix=v_norm_mix, norm_mlp=v_norm_mlp, norm_final=v_norm_final, rel_bias=v_rel_bias,
                 gdn_w_in=v_gdn_w_in, gdn_conv_w=v_gdn_conv_w, gdn_a_log=v_gdn_a_log, gdn_dt_bias=v_gdn_dt_bias,
                 gdn_norm_w=v_gdn_norm_w, gdn_w_out=v_gdn_w_out, dswa_w_in=v_dswa_w_in, dswa_w_out=v_dswa_w_out,
                 mlp_w1=v_mlp_w1, mlp_w2=v_mlp_w2)
    xs = x[0]
    target = loss_target[0]
    conv_tail, _ = _pack_rows([gdn_conv_w], 8)
    gathered = _all_gather([params[n].astype(BF16) for n in _BIG] + [conv_tail],
                           [_KIND[n] for n in _BIG] + ["stack"], name="ag_weights")
    full = {n: (_from_col_shards(t) if _KIND[n] == "stack" else t) for n, t in zip(_BIG, gathered)}
    conv_parts = [_unpack_rows(gathered[-1][dev], [gdn_conv_w.shape])[0] for dev in range(N_DEV)]
    conv_full = _from_col_shards(jnp.stack(conv_parts))[:, :, 0, :]

    loss_part, dcur, g_big, rep, g_conv = _local_step(
        xs, target, dict(norm_mix=norm_mix, norm_mlp=norm_mlp, norm_final=norm_final, rel_bias=rel_bias,
                         gdn_a_log=gdn_a_log, gdn_dt_bias=gdn_dt_bias, gdn_norm_w=gdn_norm_w), full, conv_full)
    loss = lax.psum(loss_part[0, 0], ("x", "y", "c"))
    grad_x = dcur[None]

    sends = []
    for n in _BIG:
        for g in g_big[n]:
            sends.append(_col_shards(g, N_DEV) if _KIND[n] == "stack" else g)
    conv_dev = _col_shards(jnp.stack(g_conv)[:, :, None, :], N_DEV)
    sends.append(jnp.stack([_pack_rows([conv_dev[dev]] + [rep[n] for n in _SMALL[1:]], 8)[0] for dev in range(N_DEV)]))
    recvs = _exchange(sends, name="grad_exchange")

    outs = {}
    pos = 0
    for n in _BIG:
        nl = params[n].shape[0]
        res = _adamw_layers(recvs[pos:pos + nl], params[n], mom_m[n], mom_v[n], name=f"adamw_{n}")
        pos += nl
        for tag, t in zip(("grad", "delta", "new_m", "new_v"), res):
            outs[(tag, n)] = t
    w_slab, _ = _pack_rows([params[n] for n in _SMALL], 8)
    m_slab, _ = _pack_rows([mom_m[n] for n in _SMALL], 8)
    v_slab, _ = _pack_rows([mom_v[n] for n in _SMALL], 8)
    small = _adamw_reduce(recvs[-1], w_slab, m_slab, v_slab, name="adamw_small")
    shapes = [params[n].shape for n in _SMALL]
    for tag, slab in zip(("grad", "delta", "new_m", "new_v"), small):
        for n, t in zip(_SMALL, _unpack_rows(slab, shapes)):
            outs[(tag, n)] = t
    result = [loss, grad_x]
    for tag in ("grad", "delta", "new_m", "new_v"):
        result += [outs[(tag, n)] for n in _ORDER]
    return tuple(result)


def _local_step(xs, target, sp, full, conv_full):
    s = xs.shape[0]
    norm_mix, norm_mlp, norm_final = sp["norm_mix"], sp["norm_mlp"], sp["norm_final"]
    gdn_a_log, gdn_dt_bias, gdn_norm_w = sp["gdn_a_log"], sp["gdn_dt_bias"], sp["gdn_norm_w"]
    onehot = _bucket_onehot()
    table_t = sp["rel_bias"].T
    bias = _dswa_bias(table_t, onehot, name="dswa_bias").reshape(DSWA_HEADS, DSWA_HALF, 3 * DSWA_HALF)

    saved = []
    cur = xs
    for i in range(DEPTH):
        j = i // 2
        sv = dict(x_in=cur)
        h = _rms_fwd(cur, norm_mix[i], name=f"rms_mix_fwd{i}")
        sv["h"] = h
        if i % 2 == 0:
            w_in = full["gdn_w_in"][j]
            proj = _mm(h, w_in[:, :GDN_MAIN], name=f"gdn_proj{i}")
            ab = _mm(h, w_in[:, GDN_MAIN:], name=f"gdn_proj_ab{i}")
            qkvn = _gdn_pre_fwd(proj, conv_full[j], name=f"gdn_pre_fwd{i}")
            g_all, beta_all = _gdn_gate_fwd(ab[:, :2 * GDN_HEADS], ab[:, 2 * GDN_HEADS:], gdn_a_log[j], gdn_dt_bias[j],
                                            name=f"gdn_gate_fwd{i}")
            gshape = (2, GDN_HEADS, s // GDN_CHUNK, 1, GDN_CHUNK)
            g_row = g_all.T.reshape(gshape)
            b_row = beta_all.T.reshape(gshape)
            o, states = _gdn_chunk_fwd(qkvn, g_row, b_row, name=f"gdn_chunk_fwd{i}")
            act = _gdn_post_fwd(o, proj, gdn_norm_w[j], name=f"gdn_post_fwd{i}")
            sv.update(proj=proj, ab=ab, qkvn=qkvn, g_row=g_row, b_row=b_row, o=o, states=states, act=act)
            w_out = full["gdn_w_out"][j]
        else:
            w_in = full["dswa_w_in"][j]
            qkv = _mm(h, w_in, name=f"dswa_proj{i}", out_dtypes=(BF16,))
            qkvp = _dswa_permute(qkv.reshape(s, 3, DSWA_HEADS, DSWA_E))
            qkvp = jnp.pad(qkvp, ((0, 0), (0, 0), (DSWA_HALF, DSWA_HALF), (0, 0)))
            o_p, lse_p = _dswa_attn_fwd(qkvp, bias, name=f"dswa_attn_fwd{i}")
            o_n, lse_n = _dswa_unpermute(o_p), _dswa_unpermute(lse_p)
            comb = _dswa_combine_fwd(o_n, lse_n, name=f"dswa_comb_fwd{i}")
            act = jnp.transpose(comb, (1, 0, 2)).reshape(s, DSWA_WIDTH)
            sv.update(qkvp=qkvp, o_p=o_p, lse_p=lse_p, o_n=o_n, lse_n=lse_n, act=act)
            w_out = full["dswa_w_out"][j]
        cur = _mm(act, w_out, name=f"mix_out{i}", epilogue=lambda acc, r: (acc + r,), extras=(cur,))
        sv["x_mid"] = cur
        h2 = _rms_fwd(cur, norm_mlp[i], name=f"rms_mlp_fwd{i}")
        u, a = _mm(h2, full["mlp_w1"][i], name=f"mlp_up{i}", out_dtypes=(F32, BF16),
                   epilogue=lambda acc: (acc, jnp.square(jnp.maximum(acc, 0.0))))
        cur = _mm(a, full["mlp_w2"][i], name=f"mlp_down{i}", epilogue=lambda acc, r: (acc + r,), extras=(cur,))
        sv.update(h2=h2, u=u, a=a)
        saved.append(sv)

    loss_part, dcur, dg_final = _loss_head(cur, norm_final, target, name="loss_head")

    g_norm_mix, g_norm_mlp = [None] * DEPTH, [None] * DEPTH
    g_big = {n: [None] * full[n].shape[0] for n in _BIG}
    g_conv, g_alog, g_dt, g_nw = [None] * 2, [None] * 2, [None] * 2, [None] * 2
    d_table_t = jnp.zeros((DSWA_HEADS, REL_BUCKETS), F32)
    for i in reversed(range(DEPTH)):
        j = i // 2
        sv = saved[i]
        w1, w2 = full["mlp_w1"][i], full["mlp_w2"][i]
        du = _mm(dcur, w2, tb=True, name=f"mlp_down_bwd{i}", out_dtypes=(BF16,),
                 epilogue=lambda acc, uu: (acc * (2.0 * jnp.maximum(uu, 0.0)),), extras=(sv["u"],))
        g_big["mlp_w2"][i] = _mm(sv["a"], dcur, ta=True, name=f"mlp_w2_grad{i}", out_dtypes=(BF16,), shard="rows")
        g_big["mlp_w1"][i] = _mm(sv["h2"], du, ta=True, name=f"mlp_w1_grad{i}", out_dtypes=(BF16,), shard="cols")
        dh2 = _mm(du, w1, tb=True, name=f"mlp_up_bwd{i}")
        dmid, g_norm_mlp[i] = _rms_bwd(sv["x_mid"], norm_mlp[i], dh2, dcur, name=f"rms_mlp_bwd{i}")
        if i % 2 == 0:
            w_in, w_out = full["gdn_w_in"][j], full["gdn_w_out"][j]
            dact = _mm(dmid, w_out, tb=True, name=f"mix_out_bwd{i}")
            g_big["gdn_w_out"][j] = _mm(sv["act"], dmid, ta=True, name=f"mix_out_grad{i}", out_dtypes=(BF16,),
                                        shard="rows")
            do, dz, g_nw[j] = _gdn_post_bwd(sv["o"], sv["proj"], gdn_norm_w[j], dact, name=f"gdn_post_bwd{i}")
            dqkvn, dg_row, db_row = _gdn_chunk_bwd(sv["qkvn"], sv["g_row"], sv["b_row"], sv["states"], do,
                                                   name=f"gdn_chunk_bwd{i}")
            dpre, g_conv[j] = _gdn_pre_bwd(sv["proj"], conv_full[j], dqkvn, name=f"gdn_pre_bwd{i}")
            nh2 = 2 * GDN_HEADS
            da_, db_, g_alog[j], g_dt[j] = _gdn_gate_bwd(sv["ab"][:, :nh2], sv["ab"][:, nh2:], gdn_a_log[j], gdn_dt_bias[j],
                                                         dg_row.reshape(nh2, s).T, db_row.reshape(nh2, s).T,
                                                         name=f"gdn_gate_bwd{i}")
            dab = jnp.concatenate([da_, db_], axis=1)
            dproj = jnp.concatenate([dpre, dz], axis=1)
            gw_main = _mm(sv["h"], dproj, ta=True, name=f"gdn_w_in_grad{i}", out_dtypes=(BF16,))
            gw_ab = _mm(sv["h"], dab, ta=True, name=f"gdn_w_ab_grad{i}", out_dtypes=(BF16,))
            g_big["gdn_w_in"][j] = jnp.concatenate([gw_main, gw_ab], axis=1)
            dh_ab = _mm(dab, w_in[:, GDN_MAIN:], tb=True, name=f"gdn_proj_ab_bwd{i}")
            dh = _mm(dproj, w_in[:, :GDN_MAIN], tb=True, name=f"gdn_proj_bwd{i}",
                     epilogue=lambda acc, r: (acc + r,), extras=(dh_ab,))
        else:
            w_in, w_out = full["dswa_w_in"][j], full["dswa_w_out"][j]
            dact = _mm(dmid, w_out, tb=True, name=f"mix_out_bwd{i}")
            g_big["dswa_w_out"][j] = _mm(sv["act"], dmid, ta=True, name=f"mix_out_grad{i}", out_dtypes=(BF16,),
                                         shard="rows")
            dc = jnp.transpose(dact.reshape(s, DSWA_HEADS, DSWA_E), (1, 0, 2))
            do_n, dlse_n = _dswa_combine_bwd(sv["o_n"], sv["lse_n"], dc, name=f"dswa_comb_bwd{i}")
            do_p, dlse_p = _dswa_permute_heads(do_n), _dswa_permute_heads(dlse_n)
            dq_p, dk_p, dv_p, dbias = _dswa_attn_bwd(sv["qkvp"], bias, sv["o_p"], sv["lse_p"], do_p, dlse_p,
                                                     name=f"dswa_attn_bwd{i}")
            d_table_t = d_table_t + _dswa_dtable(dbias.reshape(DSWA_HEADS, -1), onehot, name=f"dswa_dtable{i}")
            hf = DSWA_HALF
            dqkv_p = jnp.stack([dq_p, dk_p[:, hf:-hf], dv_p[:, hf:-hf]])
            dqkv_n = jnp.stack([_dswa_unpermute(dqkv_p[t]) for t in range(3)])
            dqkv = jnp.transpose(dqkv_n, (2, 0, 1, 3)).reshape(s, 3 * DSWA_WIDTH).astype(BF16)
            g_big["dswa_w_in"][j] = _mm(sv["h"], dqkv, ta=True, name=f"dswa_w_in_grad{i}", out_dtypes=(BF16,))
            dh = _mm(dqkv, w_in, tb=True, name=f"dswa_proj_bwd{i}")
        dcur, g_norm_mix[i] = _rms_bwd(sv["x_in"], norm_mix[i], dh, dmid, name=f"rms_mix_bwd{i}")

    rep = dict(norm_mix=jnp.concatenate(g_norm_mix, axis=0), norm_mlp=jnp.concatenate(g_norm_mlp, axis=0),
               norm_final=dg_final.reshape(-1), rel_bias=d_table_t.T,
               gdn_a_log=jnp.stack(g_alog).reshape(gdn_a_log.shape), gdn_dt_bias=jnp.stack(g_dt).reshape(gdn_dt_bias.shape),
               gdn_norm_w=jnp.stack(g_nw).reshape(gdn_norm_w.shape))
    return loss_part, dcur, g_big, rep, g_conv
```

```python
import functools
import math

import jax
import jax.numpy as jnp
import numpy as np
from jax import lax
from jax.experimental import pallas as pl
from jax.experimental.pallas import tpu as pltpu

F32 = jnp.float32
BF16 = jnp.bfloat16
HP = lax.Precision.HIGHEST

N_DEV = 8
D_MODEL = 1024
DEPTH = 4
RMS_EPS = 1e-6
NEG_INF = -1e30

GDN_HEADS = 8
GDN_DK = 128
GDN_CONV = 5
GDN_CHUNK = 64
GDN_QKV = 3 * GDN_HEADS * GDN_DK
GDN_MAIN = GDN_QKV + GDN_HEADS * GDN_DK
GDN_AB = 4 * GDN_HEADS

DSWA_DILS = (1, 4, 16)
DSWA_HG = 6
DSWA_E = 64
DSWA_HEADS = 18
DSWA_WIDTH = DSWA_HEADS * DSWA_E
DSWA_HALF = 64
DSWA_PG = DSWA_HG // 2
DSWA_UNROLL = 4
REL_BUCKETS = 32
REL_MAX_DIST = 1024

ADAM_LR = 0.001
ADAM_B1 = 0.9
ADAM_B2 = 0.999
ADAM_EPS = 1e-08
ADAM_WD = 0.01
ADAM_STEP = 10

VMEM_LIMIT = 56 * 1024 * 1024


def _cparams(sem=None, **kw):
    return pltpu.CompilerParams(dimension_semantics=sem, vmem_limit_bytes=VMEM_LIMIT, **kw)


def _pick(dim, cands):
    for c in cands:
        if dim % c == 0:
            return c
    return dim


def _bdot(a, b):
    return jnp.dot(a.astype(BF16), b.astype(BF16), preferred_element_type=F32)


def _bdot_nt(a, b):
    return lax.dot_general(a.astype(BF16), b.astype(BF16), (((1,), (1,)), ((), ())),
                           preferred_element_type=F32)


def _bdot_tn(a, b):
    return lax.dot_general(a.astype(BF16), b.astype(BF16), (((0,), (0,)), ((), ())),
                           preferred_element_type=F32)


def _hdot(a, b):
    return jnp.dot(a, b, precision=HP, preferred_element_type=F32)


def _hdot_tn(a, b):
    return lax.dot_general(a, b, (((0,), (0,)), ((), ())), precision=HP, preferred_element_type=F32)


def _hdot_nt(a, b):
    return lax.dot_general(a, b, (((1,), (1,)), ((), ())), precision=HP, preferred_element_type=F32)


def _sigmoid(x):
    return 1.0 / (1.0 + jnp.exp(-x))


def _mm(a, b, *, name, ta=False, tb=False, out_dtypes=(F32,), epilogue=None, extras=(),
        tm=None, tn=None, tk=None, shard=None):
    if ta:
        kdim, m = a.shape
    else:
        m, kdim = a.shape
    n = b.shape[0] if tb else b.shape[1]
    if shard == "rows":
        tm = m // N_DEV if (m // N_DEV) % 128 == 0 else m
    if shard == "cols":
        tn = n // N_DEV
    tm = tm or _pick(m, (1024, 1152, 512, 384, 256, 128))
    tn = tn or _pick(n, (1024, 1152, 512, 384, 256, 128))
    tk = tk or _pick(kdim, (1024, 1152, 512, 384, 256, 128))
    nk = kdim // tk
    n_out = len(out_dtypes)
    n_ex = len(extras)
    rows_all = shard == "rows" and tm == m

    def body(*refs):
        a_ref, b_ref = refs[0], refs[1]
        ex_refs = refs[2:2 + n_ex]
        out_refs = refs[2 + n_ex:2 + n_ex + n_out]
        acc_ref = refs[-1]
        k = pl.program_id(2)

        @pl.when(k == 0)
        def _():
            acc_ref[...] = jnp.zeros_like(acc_ref)

        av = a_ref[...].astype(BF16)
        bv = b_ref[...].astype(BF16)
        dims = (((0 if ta else 1,), (1 if tb else 0,)), ((), ()))
        acc_ref[...] += lax.dot_general(av, bv, dims, preferred_element_type=F32)

        @pl.when(k == nk - 1)
        def _():
            acc = acc_ref[...]
            outs = (acc,) if epilogue is None else epilogue(acc, *[r[...] for r in ex_refs])
            for r, o in zip(out_refs, outs):
                if rows_all:
                    for p in range(N_DEV):
                        r[p] = o[p * (m // N_DEV):(p + 1) * (m // N_DEV)].astype(r.dtype)
                else:
                    r[...] = o.astype(r.dtype)

    a_spec = pl.BlockSpec((tk, tm), lambda i, j, k: (k, i)) if ta else pl.BlockSpec((tm, tk), lambda i, j, k: (i, k))
    b_spec = pl.BlockSpec((tn, tk), lambda i, j, k: (j, k)) if tb else pl.BlockSpec((tk, tn), lambda i, j, k: (k, j))
    o_spec = pl.BlockSpec((tm, tn), lambda i, j, k: (i, j))
    out_specs = [o_spec] * n_out
    out_shape = [jax.ShapeDtypeStruct((m, n), dt) for dt in out_dtypes]
    if shard == "rows":
        out_shape = [jax.ShapeDtypeStruct((N_DEV, m // N_DEV, n), out_dtypes[0])]
        out_specs = [pl.BlockSpec((N_DEV, m // N_DEV, tn), lambda i, j, k: (0, 0, j)) if rows_all
                     else pl.BlockSpec((None, tm, tn), lambda i, j, k: (i, 0, j))]
    if shard == "cols":
        out_shape = [jax.ShapeDtypeStruct((N_DEV, m, tn), out_dtypes[0])]
        out_specs = [pl.BlockSpec((None, tm, tn), lambda i, j, k: (j, i, 0))]
    outs = pl.pallas_call(
        body, name=name,
        grid=(m // tm, n // tn, nk),
        in_specs=[a_spec, b_spec] + [o_spec] * n_ex,
        out_specs=out_specs,
        out_shape=out_shape,
        scratch_shapes=[pltpu.VMEM((tm, tn), F32)],
        compiler_params=_cparams(("parallel", "parallel", "arbitrary")),
    )(a, b, *extras)
    return outs[0] if n_out == 1 else outs


def _rms_fwd(x, g, *, name):
    s, d = x.shape
    tr = _pick(s, (512, 256, 128))

    def body(x_ref, g_ref, h_ref):
        xv = x_ref[...]
        r = lax.rsqrt(jnp.mean(xv * xv, axis=-1, keepdims=True) + RMS_EPS)
        h_ref[...] = (xv * r * g_ref[...]).astype(h_ref.dtype)

    return pl.pallas_call(
        body, name=name, grid=(s // tr,),
        in_specs=[pl.BlockSpec((tr, d), lambda i: (i, 0)), pl.BlockSpec((1, d), lambda i: (0, 0))],
        out_specs=pl.BlockSpec((tr, d), lambda i: (i, 0)),
        out_shape=jax.ShapeDtypeStruct((s, d), BF16),
        compiler_params=_cparams(("parallel",)),
    )(x, g.reshape(1, d))


def _rms_bwd(x, g, dh, dres, *, name):
    s, d = x.shape
    tr = _pick(s, (512, 256, 128))

    def body(x_ref, g_ref, dh_ref, dres_ref, dx_ref, dxb_ref, dg_ref):
        i = pl.program_id(0)
        xv = x_ref[...]
        r = lax.rsqrt(jnp.mean(xv * xv, axis=-1, keepdims=True) + RMS_EPS)
        xn = xv * r
        dhv = dh_ref[...]
        dn = dhv * g_ref[...]
        dx = dres_ref[...] + r * (dn - xn * jnp.mean(dn * xn, axis=-1, keepdims=True))
        dx_ref[...] = dx
        dxb_ref[...] = dx.astype(dxb_ref.dtype)
        part = jnp.sum(dhv * xn, axis=0, keepdims=True)

        @pl.when(i == 0)
        def _():
            dg_ref[...] = part

        @pl.when(i > 0)
        def _():
            dg_ref[...] += part

    row = pl.BlockSpec((tr, d), lambda i: (i, 0))
    vec = pl.BlockSpec((1, d), lambda i: (0, 0))
    return pl.pallas_call(
        body, name=name, grid=(s // tr,),
        in_specs=[row, vec, row, row], out_specs=[row, row, vec],
        out_shape=[jax.ShapeDtypeStruct((s, d), F32), jax.ShapeDtypeStruct((s, d), BF16),
                   jax.ShapeDtypeStruct((1, d), F32)],
        compiler_params=_cparams(("arbitrary",)),
    )(x, g.reshape(1, d), dh, dres)


def _loss_head(x, g, target, *, name):
    s, d = x.shape
    tr = _pick(s, (512, 256, 128))

    def body(x_ref, g_ref, t_ref, loss_ref, dx_ref, dxb_ref, dg_ref):
        i = pl.program_id(0)
        xv = x_ref[...]
        gv = g_ref[...]
        r = lax.rsqrt(jnp.mean(xv * xv, axis=-1, keepdims=True) + RMS_EPS)
        xn = xv * r
        err = xn * gv - t_ref[...]
        lpart = 0.5 * jnp.sum(jnp.mean(err * err, axis=-1, keepdims=True), axis=0, keepdims=True)
        dy = err * (1.0 / d)
        dn = dy * gv
        dx = r * (dn - xn * jnp.mean(dn * xn, axis=-1, keepdims=True))
        dx_ref[...] = dx
        dxb_ref[...] = dx.astype(dxb_ref.dtype)
        gpart = jnp.sum(dy * xn, axis=0, keepdims=True)

        @pl.when(i == 0)
        def _():
            dg_ref[...] = gpart
            loss_ref[...] = lpart

        @pl.when(i > 0)
        def _():
            dg_ref[...] += gpart
            loss_ref[...] += lpart

    row = pl.BlockSpec((tr, d), lambda i: (i, 0))
    vec = pl.BlockSpec((1, d), lambda i: (0, 0))
    one = pl.BlockSpec((1, 1), lambda i: (0, 0))
    return pl.pallas_call(
        body, name=name, grid=(s // tr,),
        in_specs=[row, vec, row], out_specs=[one, row, row, vec],
        out_shape=[jax.ShapeDtypeStruct((1, 1), F32), jax.ShapeDtypeStruct((s, d), F32),
                   jax.ShapeDtypeStruct((s, d), BF16), jax.ShapeDtypeStruct((1, d), F32)],
        compiler_params=_cparams(("arbitrary",)),
    )(x, g.reshape(1, d), target)


def _shift_rows(x, sft, rows):
    s = x.shape[0]
    if sft == 0:
        return x
    y = pltpu.roll(x, (-sft) % s, 0)
    ok = (rows + sft >= 0) & (rows + sft < s)
    return jnp.where(ok, y, 0.0)


def _gdn_pre_fwd(proj, conv_w, *, name):
    s = proj.shape[0]
    nblk = GDN_QKV // 128
    pad = GDN_CONV // 2

    def body(x_ref, w_ref, o_ref):
        j = pl.program_id(0)
        x = x_ref[...]
        rows = lax.broadcasted_iota(jnp.int32, x.shape, 0)
        c = jnp.zeros_like(x)
        for t in range(GDN_CONV):
            c = c + w_ref[pl.ds(t, 1), :] * _shift_rows(x, t - pad, rows)
        a = c * _sigmoid(c)
        rinv = lax.rsqrt(jnp.sum(a * a, axis=-1, keepdims=True) + 1e-6)
        scale = jnp.where(j < GDN_HEADS, GDN_DK ** -0.5, 1.0)
        o_ref[...] = jnp.where(j >= 2 * GDN_HEADS, a, a * (rinv * scale))

    return pl.pallas_call(
        body, name=name, grid=(nblk,),
        in_specs=[pl.BlockSpec((s, 128), lambda j: (0, j)), pl.BlockSpec((GDN_CONV, 128), lambda j: (0, j))],
        out_specs=pl.BlockSpec((s, 128), lambda j: (0, j)),
        out_shape=jax.ShapeDtypeStruct((s, GDN_QKV), F32),
        compiler_params=_cparams(("parallel",)),
    )(proj, conv_w)


def _gdn_pre_bwd(proj, conv_w, dqkv, *, name):
    s = proj.shape[0]
    nblk = GDN_QKV // 128
    pad = GDN_CONV // 2

    def body(x_ref, w_ref, d_ref, dx_ref, dw_ref):
        j = pl.program_id(0)
        x = x_ref[...]
        rows = lax.broadcasted_iota(jnp.int32, x.shape, 0)
        xs = [_shift_rows(x, t - pad, rows) for t in range(GDN_CONV)]
        c = jnp.zeros_like(x)
        for t in range(GDN_CONV):
            c = c + w_ref[pl.ds(t, 1), :] * xs[t]
        sg = _sigmoid(c)
        a = c * sg
        rinv = lax.rsqrt(jnp.sum(a * a, axis=-1, keepdims=True) + 1e-6)
        scale = jnp.where(j < GDN_HEADS, GDN_DK ** -0.5, 1.0)
        dy = d_ref[0] + d_ref[1]
        nh = a * rinv
        da_n = (rinv * scale) * (dy - nh * jnp.sum(dy * nh, axis=-1, keepdims=True))
        da = jnp.where(j >= 2 * GDN_HEADS, dy, da_n)
        dc = da * (sg * (1.0 + c * (1.0 - sg)))
        dx = jnp.zeros_like(x)
        for t in range(GDN_CONV):
            dx = dx + w_ref[pl.ds(t, 1), :] * _shift_rows(dc, pad - t, rows)
            dw_ref[pl.ds(t, 1), :] = jnp.sum(dc * xs[t], axis=0, keepdims=True)
        dx_ref[...] = dx.astype(dx_ref.dtype)

    col = pl.BlockSpec((s, 128), lambda j: (0, j))
    wsp = pl.BlockSpec((GDN_CONV, 128), lambda j: (0, j))
    return pl.pallas_call(
        body, name=name, grid=(nblk,),
        in_specs=[col, wsp, pl.BlockSpec((2, s, 128), lambda j: (0, 0, j))], out_specs=[col, wsp],
        out_shape=[jax.ShapeDtypeStruct((s, GDN_QKV), BF16), jax.ShapeDtypeStruct((GDN_CONV, GDN_QKV), F32)],
        compiler_params=_cparams(("parallel",)),
    )(proj, conv_w, dqkv)


def _softplus(x):
    return jnp.maximum(x, 0.0) + jnp.log(1.0 + jnp.exp(-jnp.abs(x)))


def _gdn_gate_fwd(a, b, a_log, dt_bias, *, name):
    s = a.shape[0]
    nh = 2 * GDN_HEADS

    def body(a_ref, b_ref, al_ref, dt_ref, g_ref, be_ref):
        g_ref[...] = -jnp.exp(al_ref[...]) * _softplus(a_ref[...] + dt_ref[...])
        be_ref[...] = _sigmoid(b_ref[...])

    return pl.pallas_call(
        body, name=name,
        out_shape=[jax.ShapeDtypeStruct((s, nh), F32), jax.ShapeDtypeStruct((s, nh), F32)],
        compiler_params=_cparams(),
    )(a, b, a_log.reshape(1, nh), dt_bias.reshape(1, nh))


def _gdn_gate_bwd(a, b, a_log, dt_bias, dg, dbeta, *, name):
    s = a.shape[0]
    nh = 2 * GDN_HEADS

    def body(a_ref, b_ref, al_ref, dt_ref, dg_ref, db_ref, da_ref, dbb_ref, dal_ref, ddt_ref):
        ea = jnp.exp(al_ref[...])
        z = a_ref[...] + dt_ref[...]
        dgv = dg_ref[...]
        dz = dgv * (-ea) * _sigmoid(z)
        dal_ref[...] = jnp.sum(dgv * (-ea) * _softplus(z), axis=0, keepdims=True)
        ddt_ref[...] = jnp.sum(dz, axis=0, keepdims=True)
        sb = _sigmoid(b_ref[...])
        da_ref[...] = dz
        dbb_ref[...] = db_ref[...] * sb * (1.0 - sb)

    return pl.pallas_call(
        body, name=name,
        out_shape=[jax.ShapeDtypeStruct((s, nh), F32), jax.ShapeDtypeStruct((s, nh), F32),
                   jax.ShapeDtypeStruct((1, nh), F32), jax.ShapeDtypeStruct((1, nh), F32)],
        compiler_params=_cparams(),
    )(a, b, a_log.reshape(1, nh), dt_bias.reshape(1, nh), dg, dbeta)


def _chunk_masks(d):
    c = GDN_CHUNK
    ii = lax.broadcasted_iota(jnp.int32, (c, c), 0)
    jj = lax.broadcasted_iota(jnp.int32, (c, c), 1)
    dif = (ii - jj) * (1 - 2 * d)
    mi = dif >= 0
    mit = dif <= 0
    ms = dif > 0
    eye = ii == jj
    bds = [(ii >> sh) == (jj >> sh) for sh in (3, 4, 5)]
    return dict(mi=mi, mit=mit, ms=ms, eye=eye, bds=bds,
                mif=mi.astype(F32), mitf=mit.astype(F32), eyef=eye.astype(F32))


class _V:
    def __init__(self, xs):
        self.xs = tuple(xs)

    def __add__(self, o):
        return _lift(lambda a, b: a + b)(self, o)

    def __radd__(self, o):
        return _lift(lambda a, b: b + a)(self, o)

    def __sub__(self, o):
        return _lift(lambda a, b: a - b)(self, o)

    def __rsub__(self, o):
        return _lift(lambda a, b: b - a)(self, o)

    def __mul__(self, o):
        return _lift(lambda a, b: a * b)(self, o)

    def __rmul__(self, o):
        return _lift(lambda a, b: b * a)(self, o)

    def __and__(self, o):
        return _lift(lambda a, b: a & b)(self, o)

    def __neg__(self):
        return _lift(lambda a: -a)(self)

    def __rtruediv__(self, o):
        return _lift(lambda a, b: b / a)(self, o)


def _lift(f):
    def g(*args, **kw):
        n = next(len(a.xs) for a in args if isinstance(a, _V))
        return _V(f(*[a.xs[i] if isinstance(a, _V) else a for a in args], **kw) for i in range(n))
    return g


_vwhere, _vsum, _vexp, _vnot = _lift(jnp.where), _lift(jnp.sum), _lift(jnp.exp), _lift(jnp.logical_not)
_vhdot, _vhdot_tn = _lift(_bdot), _lift(_bdot_tn)
_vbdot, _vbdot_nt, _vbdot_tn = _lift(_bdot), _lift(_bdot_nt), _lift(_bdot_tn)
_vcat = _lift(lambda a, b: jnp.concatenate([a, b], axis=1))
_vlo = _lift(lambda a, n: a[:, :n])
_vhi = _lift(lambda a, n: a[:, n:])


def _stack_masks(d, n):
    m = _chunk_masks(d)
    mk = {key: _V([m[key]] * n) for key in m if key != "bds"}
    mk["bds"] = [_V([m["bds"][i]] * n) for i in range(3)]
    return mk


def _tri_inv(a, mk):
    eyef = mk["eyef"]
    bd8, bd16, bd32 = mk["bds"]
    a8 = _vwhere(bd8, a, 0.0)
    a2 = _vhdot(a8, a8)
    a4 = _vhdot(a2, a2)
    t = _vhdot(_vhdot(eyef - a8, eyef + a2), eyef + a4)
    for inner, outer in ((bd8, bd16), (bd16, bd32), (bd32, None)):
        off = _vnot(inner) if outer is None else (outer & _vnot(inner))
        low = _vwhere(off, a, 0.0)
        t = t - _vhdot(_vhdot(t, low), t)
    return t


def _chunk_prep(q, k, v, g_row, b_row, mk):
    dv = GDN_DK
    g_col = _vsum(mk["eyef"] * g_row, axis=1, keepdims=True)
    b_col = _vsum(mk["eyef"] * b_row, axis=1, keepdims=True)
    gc_col = _vsum(mk["mif"] * g_row, axis=1, keepdims=True)
    gc_row = _vsum(mk["mitf"] * g_col, axis=0, keepdims=True)
    gl = _vsum(g_row, axis=1, keepdims=True)
    decay = _vwhere(mk["mi"], _vexp(_vwhere(mk["mi"], gc_col - gc_row, 0.0)), 0.0)
    eg = _vexp(gc_col)
    e2 = _vexp(gl - gc_col)
    egl = _vexp(gl)
    kb = k * b_col
    pm = _vbdot_nt(kb, k)
    a = _vwhere(mk["ms"], pm * decay, 0.0)
    t = _tri_inv(a, mk)
    sol = _vhdot(t, _vcat(v * b_col, kb * eg))
    u, w = _vlo(sol, dv), _vhi(sol, dv)
    qm = _vbdot_nt(q, k)
    return dict(b_col=b_col, decay=decay, eg=eg, e2=e2, egl=egl, kb=kb, pm=pm, t=t, u=u, w=w,
                qm=qm, intra=qm * decay, qd=q * eg, kd=k * e2)


def _chunk_fwd_step(p, state):
    v_new = p["u"] - _vbdot(p["w"], state)
    o = _vbdot(p["qd"], state) + _vbdot(p["intra"], v_new)
    new_state = state * p["egl"] + _vbdot_tn(p["kd"], v_new)
    return o, new_state


def _chunk_bwd_step(q, k, v, p, mk, state, dso, do):
    dv_dim = GDN_DK
    v_new = p["u"] - _vbdot(p["w"], state)
    dvn = _vbdot_tn(p["intra"], do) + _vbdot(p["kd"], dso)
    dintra = _vbdot_nt(do, v_new)
    dqd = _vbdot_nt(do, state)
    ds = p["egl"] * dso + _vbdot_tn(p["qd"], do) - _vbdot_tn(p["w"], dvn)
    dkd = _vbdot_nt(v_new, dso)
    dgl = _vsum(_vsum(dso * state, axis=1, keepdims=True), axis=0, keepdims=True) * p["egl"]
    dw = -_vbdot_nt(dvn, state)
    drhs = _vhdot_tn(p["t"], _vcat(dvn, dw))
    dru, drw = _vlo(drhs, dv_dim), _vhi(drhs, dv_dim)
    da = -_vwhere(mk["ms"], _vbdot_nt(drhs, _vcat(p["u"], p["w"])), 0.0)
    b_col = p["b_col"]
    dv = dru * b_col
    dbeta = _vsum(dru * v, axis=1, keepdims=True)
    dkb = drw * p["eg"]
    deg = _vsum(drw * p["kb"], axis=1, keepdims=True)
    dp = da * p["decay"]
    ddecay = da * p["pm"]
    dkb = dkb + _vbdot(dp, k)
    dk = _vbdot_tn(dp, p["kb"])
    dqm = dintra * p["decay"]
    ddecay = ddecay + dintra * p["qm"]
    dq = _vbdot(dqm, k)
    dk = dk + _vbdot_tn(dqm, q)
    dd = ddecay * p["decay"]
    dgc_col = _vsum(dd, axis=1, keepdims=True)
    dgc_row = -_vsum(dd, axis=0, keepdims=True)
    dq = dq + dqd * p["eg"]
    deg = deg + _vsum(dqd * q, axis=1, keepdims=True)
    dk = dk + dkd * p["e2"]
    de2 = _vsum(dkd * k, axis=1, keepdims=True) * p["e2"]
    dgl = dgl + _vsum(de2, axis=0, keepdims=True)
    dgc_col = dgc_col - de2 + deg * p["eg"]
    dk = dk + dkb * b_col
    dbeta = dbeta + _vsum(dkb * k, axis=1, keepdims=True)
    dgc_col = dgc_col + _vsum(mk["eyef"] * dgc_row, axis=1, keepdims=True)
    dg_row = _vsum(mk["mif"] * dgc_col, axis=0, keepdims=True) + dgl
    dbeta_row = _vsum(mk["eyef"] * dbeta, axis=0, keepdims=True)
    return dq, dk, dv, dg_row, dbeta_row, ds


def _gdn_chunk_fwd(qkvn, g5, b5, *, name):
    s = qkvn.shape[0]
    c = GDN_CHUNK
    nc = s // c
    h_, dk = GDN_HEADS, GDN_DK

    def body(x_ref, g_ref, b_ref, o_ref, st_ref, st_scr):
        d, n = pl.program_id(0), pl.program_id(1)

        @pl.when(n == 0)
        def _():
            st_scr[...] = jnp.zeros_like(st_scr)

        mk = _stack_masks(d, h_)
        q, k, v = (_V(x_ref[:, (t * h_ + h) * dk:(t * h_ + h + 1) * dk] for h in range(h_)) for t in range(3))
        g, b = (_V(r[0, h, 0] for h in range(h_)) for r in (g_ref, b_ref))
        state = _V(st_scr[h] for h in range(h_))
        o, new_state = _chunk_fwd_step(_chunk_prep(q, k, v, g, b, mk), state)
        for h in range(h_):
            st_ref[0, h, 0] = state.xs[h]
            st_scr[h] = new_state.xs[h]
            o_ref[0, :, h * dk:(h + 1) * dk] = o.xs[h]

    ce = lambda d, n: n + d * (nc - 1 - 2 * n)
    gate = pl.BlockSpec((1, h_, 1, 1, c), lambda d, n: (d, 0, ce(d, n), 0, 0))
    return pl.pallas_call(
        body, name=name, grid=(2, nc),
        in_specs=[pl.BlockSpec((c, 3 * h_ * dk), lambda d, n: (ce(d, n), 0)), gate, gate],
        out_specs=[pl.BlockSpec((1, c, h_ * dk), lambda d, n: (d, ce(d, n), 0)),
                   pl.BlockSpec((1, h_, 1, dk, dk), lambda d, n: (d, 0, ce(d, n), 0, 0))],
        out_shape=[jax.ShapeDtypeStruct((2, s, h_ * dk), F32), jax.ShapeDtypeStruct((2, h_, nc, dk, dk), F32)],
        scratch_shapes=[pltpu.VMEM((h_, dk, dk), F32)],
        compiler_params=_cparams(("arbitrary", "arbitrary")),
    )(qkvn, g5, b5)


def _gdn_chunk_bwd(qkvn, g5, b5, states, do, *, name):
    s = qkvn.shape[0]
    c = GDN_CHUNK
    nc = s // c
    h_, dk = GDN_HEADS, GDN_DK

    def body(x_ref, g_ref, b_ref, st_ref, do_ref, dx_ref, dg_ref, db_ref, ds_scr):
        d, i = pl.program_id(0), pl.program_id(1)

        @pl.when(i == 0)
        def _():
            ds_scr[...] = jnp.zeros_like(ds_scr)

        mk = _stack_masks(d, h_)
        q, k, v = (_V(x_ref[:, (t * h_ + h) * dk:(t * h_ + h + 1) * dk] for h in range(h_)) for t in range(3))
        g, b = (_V(r[0, h, 0] for h in range(h_)) for r in (g_ref, b_ref))
        state = _V(st_ref[0, h, 0] for h in range(h_))
        dso = _V(ds_scr[h] for h in range(h_))
        dov = _V(do_ref[:, h * dk:(h + 1) * dk] for h in range(h_))
        res = _chunk_bwd_step(q, k, v, _chunk_prep(q, k, v, g, b, mk), mk, state, dso, dov)
        for h, (dq, dkk, dvv, dg_r, db_r, ds) in enumerate(zip(*[r.xs for r in res])):
            ds_scr[h] = ds
            dg_ref[0, h, 0] = dg_r
            db_ref[0, h, 0] = db_r
            for t, val in enumerate((dq, dkk, dvv)):
                dx_ref[0, :, (t * h_ + h) * dk:(t * h_ + h + 1) * dk] = val

    ce = lambda d, i: (nc - 1 - i) + d * (2 * i - nc + 1)
    gate = pl.BlockSpec((1, h_, 1, 1, c), lambda d, i: (d, 0, ce(d, i), 0, 0))
    return pl.pallas_call(
        body, name=name, grid=(2, nc),
        in_specs=[pl.BlockSpec((c, 3 * h_ * dk), lambda d, i: (ce(d, i), 0)), gate, gate,
                  pl.BlockSpec((1, h_, 1, dk, dk), lambda d, i: (d, 0, ce(d, i), 0, 0)),
                  pl.BlockSpec((c, h_ * dk), lambda d, i: (ce(d, i), 0))],
        out_specs=[pl.BlockSpec((1, c, 3 * h_ * dk), lambda d, i: (d, ce(d, i), 0)), gate, gate],
        out_shape=[jax.ShapeDtypeStruct((2, s, 3 * h_ * dk), F32)]
        + [jax.ShapeDtypeStruct((2, h_, nc, 1, c), F32)] * 2,
        scratch_shapes=[pltpu.VMEM((h_, dk, dk), F32)],
        compiler_params=_cparams(("arbitrary", "arbitrary")),
    )(qkvn, g5, b5, states, do)


def _gdn_post_fwd(o, z, norm_w, *, name):
    s = o.shape[1]
    h_, dk = GDN_HEADS, GDN_DK

    def body(o_ref, z_ref, w_ref, a_ref):
        ov = o_ref[0] + o_ref[1]
        zv = z_ref[...]
        r = lax.rsqrt(jnp.mean(ov * ov, axis=-1, keepdims=True) + RMS_EPS)
        a_ref[...] = (ov * r * w_ref[...] * (zv * _sigmoid(zv))).astype(a_ref.dtype)

    return pl.pallas_call(
        body, name=name, grid=(h_,),
        in_specs=[pl.BlockSpec((2, s, dk), lambda h: (0, 0, h)), pl.BlockSpec((s, dk), lambda h: (0, 3 * h_ + h)),
                  pl.BlockSpec((1, dk), lambda h: (0, 0))],
        out_specs=pl.BlockSpec((s, dk), lambda h: (0, h)),
        out_shape=jax.ShapeDtypeStruct((s, h_ * dk), BF16),
        compiler_params=_cparams(("parallel",)),
    )(o, z, norm_w.reshape(1, dk))


def _gdn_post_bwd(o, z, norm_w, dact, *, name):
    s = o.shape[1]
    h_, dk = GDN_HEADS, GDN_DK

    def body(o_ref, z_ref, w_ref, da_ref, do_ref, dz_ref, dw_ref):
        h = pl.program_id(0)
        ov = o_ref[0] + o_ref[1]
        zv = z_ref[...]
        wv = w_ref[...]
        dav = da_ref[...]
        r = lax.rsqrt(jnp.mean(ov * ov, axis=-1, keepdims=True) + RMS_EPS)
        nrm = ov * r
        sg = _sigmoid(zv)
        sz = zv * sg
        dn = dav * wv * sz
        do_ref[...] = r * (dn - nrm * jnp.mean(dn * nrm, axis=-1, keepdims=True))
        dz_ref[...] = (dav * nrm * wv * (sg * (1.0 + zv * (1.0 - sg)))).astype(dz_ref.dtype)
        part = jnp.sum(dav * nrm * sz, axis=0, keepdims=True)

        @pl.when(h == 0)
        def _():
            dw_ref[...] = part

        @pl.when(h > 0)
        def _():
            dw_ref[...] += part

    col = pl.BlockSpec((s, dk), lambda h: (0, h))
    vec = pl.BlockSpec((1, dk), lambda h: (0, 0))
    return pl.pallas_call(
        body, name=name, grid=(h_,),
        in_specs=[pl.BlockSpec((2, s, dk), lambda h: (0, 0, h)), pl.BlockSpec((s, dk), lambda h: (0, 3 * h_ + h)), vec, col],
        out_specs=[col, col, vec],
        out_shape=[jax.ShapeDtypeStruct((s, h_ * dk), F32), jax.ShapeDtypeStruct((s, h_ * dk), BF16),
                   jax.ShapeDtypeStruct((1, dk), F32)],
        compiler_params=_cparams(("arbitrary",)),
    )(o, z, norm_w.reshape(1, dk), dact)


def _rel_bucket(rel):
    nb = REL_BUCKETS // 2
    max_exact = nb // 2
    ret = jnp.where(rel > 0, nb, 0)
    n = jnp.abs(rel)
    nf = jnp.maximum(n, 1).astype(F32)
    large = max_exact + (jnp.log(nf / max_exact) / math.log(REL_MAX_DIST / max_exact)
                         * (nb - max_exact)).astype(jnp.int32)
    large = jnp.minimum(large, nb - 1)
    return ret + jnp.where(n < max_exact, n, large)


def _bucket_onehot():
    half = DSWA_HALF
    outs = []
    for dil in DSWA_DILS:
        rel = (jnp.arange(3 * half)[None, :] - half - jnp.arange(half)[:, None]) * dil
        outs.append(jax.nn.one_hot(_rel_bucket(rel).reshape(-1), REL_BUCKETS, dtype=F32, axis=0))
    return jnp.stack(outs)


def _head_group_select(vals):
    rows = lax.broadcasted_iota(jnp.int32, vals[0].shape, 0)
    return jnp.where(rows < DSWA_HG, vals[0], jnp.where(rows < 2 * DSWA_HG, vals[1], vals[2]))


def _dswa_bias(table_t, onehot, *, name):
    p = onehot.shape[-1]

    def body(t_ref, oh_ref, b_ref):
        b_ref[...] = _head_group_select([_hdot(t_ref[...], oh_ref[g]) for g in range(3)])

    return pl.pallas_call(body, name=name, out_shape=jax.ShapeDtypeStruct((DSWA_HEADS, p), F32),
                          compiler_params=_cparams())(table_t, onehot)


def _dswa_dtable(dbias, onehot, *, name):
    def body(d_ref, oh_ref, t_ref):
        t_ref[...] = _head_group_select([_hdot_nt(d_ref[...], oh_ref[g]) for g in range(3)])

    return pl.pallas_call(body, name=name, out_shape=jax.ShapeDtypeStruct((DSWA_HEADS, REL_BUCKETS), F32),
                          compiler_params=_cparams())(dbias, onehot)


def _attn_valid(blk, h, s):
    half = DSWA_HALF
    nblocks = s // half
    nbs = jnp.where(h < DSWA_HG, nblocks // DSWA_DILS[0],
                    jnp.where(h < 2 * DSWA_HG, nblocks // DSWA_DILS[1], nblocks // DSWA_DILS[2]))
    b = blk & (nbs - 1)
    ii = lax.broadcasted_iota(jnp.int32, (half, 3 * half), 0)
    jj = lax.broadcasted_iota(jnp.int32, (half, 3 * half), 1)
    off = jj - half - ii
    return (jnp.abs(off) <= half) & ((jj >= half) | (b > 0)) & ((jj < 2 * half) | (b < nbs - 1))


def _attn_chains(qkv_ref, pr, it, s):
    half = DSWA_HALF
    lane = lax.broadcasted_iota(jnp.int32, (half, 2 * DSWA_E), 1)
    r0s, qm, kw, vw, valid, hmask = [], [], [], [], [], []
    for u in range(DSWA_UNROLL):
        blk = it * DSWA_UNROLL + u
        r0 = pl.multiple_of(blk * half, half)
        r0s.append(r0)
        q = qkv_ref[0, 0, pl.ds(r0 + half, half), :]
        k = qkv_ref[1, 0, pl.ds(r0, 3 * half), :]
        v = qkv_ref[2, 0, pl.ds(r0, 3 * half), :]
        ok = _attn_valid(blk, 2 * pr, s)
        for hd in range(2):
            mine = (lane < DSWA_E) if hd == 0 else (lane >= DSWA_E)
            qm.append(jnp.where(mine, q, jnp.zeros_like(q)))
            kw.append(k)
            vw.append(v)
            valid.append(ok)
            hmask.append(mine)
    return r0s, _V(qm), _V(kw), _V(vw), _V(valid), _V(hmask)


_vmax, _vlog = _lift(jnp.max), _lift(jnp.log)


def _dswa_attn_fwd(qkvp, bias, *, name):
    sp = qkvp.shape[2]
    half, e = DSWA_HALF, DSWA_E
    s = sp - 2 * half
    npair = DSWA_HEADS // 2

    def body(qkv_ref, bias_ref, o_ref, lse_ref):
        pr = pl.program_id(0)
        bias_v = _V([bias_ref[0], bias_ref[1]] * DSWA_UNROLL)

        def step(it, carry):
            r0s, qm, kw, vw, valid, hmask = _attn_chains(qkv_ref, pr, it, s)
            sc = _vwhere(valid, _vbdot_nt(qm, kw) * (e ** -0.5) + bias_v, NEG_INF)
            m = _vmax(sc, axis=-1, keepdims=True)
            p = _vexp(sc - m)
            l = _vsum(p, axis=-1, keepdims=True)
            o = _vbdot(p * (1.0 / l), vw)
            lse = m + _vlog(l)
            for u, r0 in enumerate(r0s):
                is_a = hmask.xs[2 * u]
                o_ref[0, pl.ds(r0, half), :] = jnp.where(is_a, o.xs[2 * u], o.xs[2 * u + 1])
                lse_ref[0, pl.ds(r0, half), :] = jnp.where(is_a, lse.xs[2 * u], lse.xs[2 * u + 1])
            return carry

        lax.fori_loop(0, s // half // DSWA_UNROLL, step, 0)

    pair = pl.BlockSpec((1, s, 2 * e), lambda p: (p, 0, 0))
    return pl.pallas_call(
        body, name=name, grid=(npair,),
        in_specs=[pl.BlockSpec((3, 1, sp, 2 * e), lambda p: (0, p, 0, 0)),
                  pl.BlockSpec((2, half, 3 * half), lambda p: (p, 0, 0))],
        out_specs=[pair, pair],
        out_shape=[jax.ShapeDtypeStruct((npair, s, 2 * e), F32)] * 2,
        compiler_params=_cparams(("parallel",)),
    )(qkvp, bias)


def _dswa_attn_bwd(qkvp, bias, lse, do, corr, *, name):
    sp = qkvp.shape[2]
    half, e = DSWA_HALF, DSWA_E
    s = sp - 2 * half
    npair = DSWA_HEADS // 2

    def body(qkv_ref, bias_ref, lse_ref, do_ref, corr_ref, dq_ref, dk_ref, dv_ref, db_ref):
        pr = pl.program_id(0)
        bias_v = _V([bias_ref[0], bias_ref[1]] * DSWA_UNROLL)
        dk_ref[...] = jnp.zeros_like(dk_ref)
        dv_ref[...] = jnp.zeros_like(dv_ref)

        def step(it, dbias):
            r0s, qm, kw, vw, valid, hmask = _attn_chains(qkv_ref, pr, it, s)
            hd = [0, 1] * DSWA_UNROLL
            rows = [r0 for r0 in r0s for _ in range(2)]
            lse_c = _V(lse_ref[0, pl.ds(r0, half), :][:, h * e:h * e + 1] for r0, h in zip(rows, hd))
            corr_c = _V(corr_ref[0, pl.ds(r0, half), :][:, h * e:h * e + 1] for r0, h in zip(rows, hd))
            dov = _vwhere(hmask, _V(do_ref[0, pl.ds(r0, half), :] for r0 in rows), 0.0)
            sc = _vbdot_nt(qm, kw) * (e ** -0.5) + bias_v
            p = _vwhere(valid, _vexp(_vwhere(valid, sc, 0.0) - lse_c), 0.0)
            dsc = p * (_vbdot_nt(dov, vw) + corr_c)
            dq = _vbdot(dsc, kw) * (e ** -0.5)
            dkc = _vbdot_tn(dsc, qm) * (e ** -0.5)
            dvc = _vbdot_tn(p, dov)
            for u, r0 in enumerate(r0s):
                dq_ref[0, pl.ds(r0, half), :] = jnp.where(hmask.xs[2 * u], dq.xs[2 * u], dq.xs[2 * u + 1])
                dk_ref[0, pl.ds(r0, 3 * half), :] += dkc.xs[2 * u] + dkc.xs[2 * u + 1]
                dv_ref[0, pl.ds(r0, 3 * half), :] += dvc.xs[2 * u] + dvc.xs[2 * u + 1]
            da, db = dbias
            for u in range(DSWA_UNROLL):
                da, db = da + dsc.xs[2 * u], db + dsc.xs[2 * u + 1]
            return da, db

        zero = jnp.zeros((half, 3 * half), F32)
        da, db = lax.fori_loop(0, s // half // DSWA_UNROLL, step, (zero, zero))
        db_ref[0] = da
        db_ref[1] = db

    ps = pl.BlockSpec((1, s, 2 * e), lambda p: (p, 0, 0))
    pp = pl.BlockSpec((1, sp, 2 * e), lambda p: (p, 0, 0))
    bs = pl.BlockSpec((2, half, 3 * half), lambda p: (p, 0, 0))
    return pl.pallas_call(
        body, name=name, grid=(npair,),
        in_specs=[pl.BlockSpec((3, 1, sp, 2 * e), lambda p: (0, p, 0, 0)), bs, ps, ps, ps],
        out_specs=[ps, pp, pp, bs],
        out_shape=[jax.ShapeDtypeStruct((npair, s, 2 * e), F32), jax.ShapeDtypeStruct((npair, sp, 2 * e), F32),
                   jax.ShapeDtypeStruct((npair, sp, 2 * e), F32),
                   jax.ShapeDtypeStruct((DSWA_HEADS, half, 3 * half), F32)],
        compiler_params=_cparams(("parallel",)),
    )(qkvp, bias, lse, do, corr)


def _group_weights(l_ref, j):
    ls = [l_ref[g * DSWA_PG + j] for g in range(3)]
    m = jnp.maximum(jnp.maximum(ls[0], ls[1]), ls[2])
    es = [jnp.exp(x - m) for x in ls]
    inv = 1.0 / (es[0] + es[1] + es[2])
    return [x * inv for x in es]


def _dswa_combine_fwd(o, lse, *, name):
    s = o.shape[1]
    tr = _pick(s, (512, 256, 128))

    def body(o_ref, l_ref, c_ref):
        for j in range(DSWA_PG):
            al = _group_weights(l_ref, j)
            for g in range(3):
                c_ref[g * DSWA_PG + j] = (o_ref[g * DSWA_PG + j] * al[g]).astype(c_ref.dtype)

    big = pl.BlockSpec((DSWA_HEADS // 2, tr, 2 * DSWA_E), lambda i: (0, i, 0))
    return pl.pallas_call(
        body, name=name, grid=(s // tr,),
        in_specs=[big, big], out_specs=big,
        out_shape=jax.ShapeDtypeStruct(o.shape, BF16),
        compiler_params=_cparams(("parallel",)),
    )(o, lse)


def _dswa_combine_bwd(o, lse, dc, *, name):
    s = o.shape[1]
    tr = _pick(s, (512, 256, 128))

    def body(o_ref, l_ref, dc_ref, do_ref, corr_ref):
        lane = lax.broadcasted_iota(jnp.int32, (tr, 2 * DSWA_E), 1)
        is_a = lane < DSWA_E
        for j in range(DSWA_PG):
            al = _group_weights(l_ref, j)
            tot = jnp.zeros((tr, 2 * DSWA_E), F32)
            for g in range(3):
                pi = g * DSWA_PG + j
                dcv = dc_ref[pi]
                do_ref[pi] = dcv * al[g]
                prod = dcv * o_ref[pi]
                dal = jnp.where(is_a, jnp.sum(jnp.where(is_a, prod, 0.0), axis=-1, keepdims=True),
                                jnp.sum(jnp.where(is_a, 0.0, prod), axis=-1, keepdims=True))
                tot = tot + al[g] * dal
            for g in range(3):
                corr_ref[g * DSWA_PG + j] = -al[g] * tot

    big = pl.BlockSpec((DSWA_HEADS // 2, tr, 2 * DSWA_E), lambda i: (0, i, 0))
    return pl.pallas_call(
        body, name=name, grid=(s // tr,),
        in_specs=[big, big, big], out_specs=[big, big],
        out_shape=[jax.ShapeDtypeStruct(o.shape, F32)] * 2,
        compiler_params=_cparams(("parallel",)),
    )(o, lse, dc)


def _dswa_permute(t):
    s = t.shape[0]
    mid = t.shape[1:-2]
    x = t.shape[-1]
    nm = len(mid)
    parts = []
    for gi, dil in enumerate(DSWA_DILS):
        tg = t[..., gi * DSWA_PG:(gi + 1) * DSWA_PG, :].reshape((s // dil, dil) + mid + (DSWA_PG, x))
        perm = tuple(range(2, 2 + nm)) + (2 + nm, 1, 0, 3 + nm)
        parts.append(jnp.transpose(tg, perm).reshape(mid + (DSWA_PG, s, x)))
    return jnp.concatenate(parts, axis=nm)


def _dswa_unpermute(t):
    s, x = t.shape[1], t.shape[2]
    parts = []
    for gi, dil in enumerate(DSWA_DILS):
        tg = t[gi * DSWA_PG:(gi + 1) * DSWA_PG].reshape(DSWA_PG, dil, s // dil, x)
        parts.append(jnp.swapaxes(tg, 1, 2).reshape(DSWA_PG, s, x))
    return jnp.concatenate(parts, axis=0)


def _dswa_permute_heads(t):
    s, x = t.shape[1], t.shape[2]
    parts = []
    for gi, dil in enumerate(DSWA_DILS):
        tg = t[gi * DSWA_PG:(gi + 1) * DSWA_PG].reshape(DSWA_PG, s // dil, dil, x)
        parts.append(jnp.swapaxes(tg, 1, 2).reshape(DSWA_PG, s, x))
    return jnp.concatenate(parts, axis=0)


def _all_gather(shards, kinds, *, name):
    nref = len(shards)

    def out_shape(x, kind):
        if kind == "stack":
            return (N_DEV,) + x.shape
        if kind == "rows":
            return (x.shape[0], N_DEV * x.shape[1], x.shape[2])
        return (x.shape[0], x.shape[1], N_DEV * x.shape[2])

    def body(*refs):
        x_refs, out_refs = refs[:nref], refs[nref:2 * nref]
        send_sems, recv_sems, local_sems = refs[2 * nref:]
        mx, my, mc = lax.axis_index("x"), lax.axis_index("y"), lax.axis_index("c")
        me, sibling = (mx, my, mc), (mx, my, 1 - mc)
        chips = [(1 - mx, my), (mx, 1 - my), (1 - mx, 1 - my)]

        def slot(i, px, py, pc):
            p = 4 * px + 2 * py + pc
            if kinds[i] == "stack":
                return out_refs[i].at[p]
            if kinds[i] == "rows":
                ks = x_refs[i].shape[1]
                return out_refs[i].at[:, pl.ds(p * ks, ks), :]
            ns = x_refs[i].shape[2]
            return out_refs[i].at[:, :, pl.ds(p * ns, ns)]

        def copy(i, k, block, to, src=None):
            return pltpu.make_async_remote_copy(
                src_ref=slot(i, *block) if src is None else src, dst_ref=slot(i, *block),
                send_sem=send_sems.at[7 * i + k], recv_sem=recv_sems.at[7 * i + k],
                device_id=to, device_id_type=pl.DeviceIdType.MESH)

        rng = range(nref)
        mine = [pltpu.make_async_copy(x_refs[i], slot(i, *me), local_sems.at[i]) for i in rng]
        first = [copy(i, 0, me, sibling, src=x_refs[i]) for i in rng]
        first += [copy(i, 1 + j, me, (*chip, mc), src=x_refs[i]) for j, chip in enumerate(chips) for i in rng]
        for cp in mine + first:
            cp.start()
        passed = []
        for j, chip in enumerate(chips):
            for i in rng:
                copy(i, 1 + j, (*chip, mc), me).wait_recv()
                passed.append(copy(i, 4 + j, (*chip, mc), sibling))
                passed[-1].start()
        for i in rng:
            copy(i, 0, sibling, me).wait_recv()
            for j, chip in enumerate(chips):
                copy(i, 4 + j, (*chip, 1 - mc), me).wait_recv()
        for cp in first + passed:
            cp.wait_send()
        for cp in mine:
            cp.wait()

    anyspec = pl.BlockSpec(memory_space=pl.ANY)
    return pl.pallas_call(
        body, name=name,
        in_specs=[anyspec] * nref, out_specs=[anyspec] * nref,
        out_shape=[jax.ShapeDtypeStruct(out_shape(x, kd), x.dtype) for x, kd in zip(shards, kinds)],
        scratch_shapes=[pltpu.SemaphoreType.DMA((7 * nref,)), pltpu.SemaphoreType.DMA((7 * nref,)),
                        pltpu.SemaphoreType.DMA((nref,))],
    )(*shards)


def _exchange(sends, *, name):
    nref = len(sends)

    def body(*refs):
        s_refs, r_refs = refs[:nref], refs[nref:2 * nref]
        send_sems, recv_sems, local_sems = refs[2 * nref:]
        mx, my, mc = lax.axis_index("x"), lax.axis_index("y"), lax.axis_index("c")
        me = 4 * mx + 2 * my + mc
        mine = [pltpu.make_async_copy(s_refs[i].at[me], r_refs[i].at[me], local_sems.at[i]) for i in range(nref)]
        copies = []
        for k in range(1, N_DEV):
            px = 1 - mx if (k >> 2) & 1 else mx
            py = 1 - my if (k >> 1) & 1 else my
            pc = 1 - mc if k & 1 else mc
            peer = 4 * px + 2 * py + pc
            for i in range(nref):
                copies.append(pltpu.make_async_remote_copy(
                    src_ref=s_refs[i].at[peer], dst_ref=r_refs[i].at[me],
                    send_sem=send_sems.at[7 * i + k - 1], recv_sem=recv_sems.at[7 * i + k - 1],
                    device_id=(px, py, pc), device_id_type=pl.DeviceIdType.MESH))
        for cp in mine + copies:
            cp.start()
        for cp in copies:
            cp.wait_recv()
        for cp in copies:
            cp.wait_send()
        for cp in mine:
            cp.wait()

    anyspec = pl.BlockSpec(memory_space=pl.ANY)
    return pl.pallas_call(
        body, name=name,
        in_specs=[anyspec] * nref, out_specs=[anyspec] * nref,
        out_shape=[jax.ShapeDtypeStruct(x.shape, x.dtype) for x in sends],
        scratch_shapes=[pltpu.SemaphoreType.DMA((7 * nref,)), pltpu.SemaphoreType.DMA((7 * nref,)),
                        pltpu.SemaphoreType.DMA((nref,))],
    )(*sends)


def _adamw_update(g, w, m, v):
    mn = ADAM_B1 * m + (1.0 - ADAM_B1) * g
    vn = ADAM_B2 * v + (1.0 - ADAM_B2) * (g * g)
    m_hat = mn / (1.0 - ADAM_B1 ** ADAM_STEP)
    v_hat = vn / (1.0 - ADAM_B2 ** ADAM_STEP)
    return -ADAM_LR * (m_hat / (jnp.sqrt(v_hat) + ADAM_EPS) + ADAM_WD * w), mn, vn


def _adamw_layers(recvs, w, m, v, *, name):
    nl, ks, ns = w.shape
    tr = _pick(ks, (64, 48))

    def body(*refs):
        rv_refs = refs[:nl]
        w_ref, m_ref, v_ref, g_ref, d_ref, nm_ref, nv_ref = refs[nl:]
        for l in range(nl):
            g = rv_refs[l][0].astype(F32)
            for q in range(1, N_DEV):
                g = g + rv_refs[l][q].astype(F32)
            delta, mn, vn = _adamw_update(g, w_ref[l], m_ref[l], v_ref[l])
            g_ref[l] = g
            d_ref[l] = delta
            nm_ref[l] = mn
            nv_ref[l] = vn

    row = pl.BlockSpec((nl, tr, ns), lambda i: (0, i, 0))
    return pl.pallas_call(
        body, name=name, grid=(ks // tr,),
        in_specs=[pl.BlockSpec((N_DEV, tr, ns), lambda i: (0, i, 0))] * nl + [row] * 3,
        out_specs=[row] * 4,
        out_shape=[jax.ShapeDtypeStruct((nl, ks, ns), F32)] * 4,
        compiler_params=_cparams(("parallel",)),
    )(*recvs, w, m, v)


def _adamw_reduce(recv, w, m, v, *, name):
    r, c = w.shape
    tr = _pick(r, (128, 64, 8))

    def body(rv_ref, w_ref, m_ref, v_ref, g_ref, d_ref, nm_ref, nv_ref):
        g = rv_ref[0]
        for q in range(1, N_DEV):
            g = g + rv_ref[q]
        delta, mn, vn = _adamw_update(g, w_ref[...], m_ref[...], v_ref[...])
        g_ref[...] = g
        d_ref[...] = delta
        nm_ref[...] = mn
        nv_ref[...] = vn

    row = pl.BlockSpec((tr, c), lambda i: (i, 0))
    return pl.pallas_call(
        body, name=name, grid=(r // tr,),
        in_specs=[pl.BlockSpec((N_DEV, tr, c), lambda i: (0, i, 0)), row, row, row],
        out_specs=[row] * 4,
        out_shape=[jax.ShapeDtypeStruct((r, c), F32)] * 4,
        compiler_params=_cparams(("parallel",)),
    )(recv, w, m, v)


_BIG = ("gdn_w_in", "gdn_w_out", "dswa_w_in", "dswa_w_out", "mlp_w1", "mlp_w2")
_SMALL = ("gdn_conv_w", "norm_mix", "norm_mlp", "norm_final", "rel_bias", "gdn_a_log", "gdn_dt_bias", "gdn_norm_w")
_ORDER = ("norm_mix", "norm_mlp", "norm_final", "rel_bias", "gdn_w_in", "gdn_conv_w", "gdn_a_log", "gdn_dt_bias",
          "gdn_norm_w", "gdn_w_out", "dswa_w_in", "dswa_w_out", "mlp_w1", "mlp_w2")
_KIND = dict(gdn_w_in="stack", gdn_w_out="rows", dswa_w_in="stack", dswa_w_out="rows", mlp_w1="cols", mlp_w2="rows")


def _pack_rows(arrs, align):
    rows, counts = [], []
    for a in arrs:
        flat = a.reshape(-1)
        n = -(-flat.shape[0] // D_MODEL)
        flat = jnp.pad(flat, (0, n * D_MODEL - flat.shape[0]))
        rows.append(flat.reshape(n, D_MODEL))
        counts.append(n)
    out = jnp.concatenate(rows, axis=0)
    total = -(-out.shape[0] // align) * align
    return jnp.pad(out, ((0, total - out.shape[0]), (0, 0))), counts


def _unpack_rows(slab, shapes):
    outs, r = [], 0
    for shp in shapes:
        size = int(np.prod(shp))
        n = -(-size // D_MODEL)
        outs.append(slab[r:r + n].reshape(-1)[:size].reshape(shp))
        r += n
    return outs


def _col_shards(full, nshard):
    lead = full.shape[:-1]
    n = full.shape[-1] // nshard
    t = full.reshape(lead + (nshard, n))
    return jnp.moveaxis(t, -2, 0)


def _from_col_shards(g):
    t = jnp.moveaxis(g, 0, -2)
    return t.reshape(t.shape[:-2] + (t.shape[-2] * t.shape[-1],))


def kernel(x, norm_mix, norm_mlp, norm_final, rel_bias, gdn_w_in, gdn_conv_w, gdn_a_log, gdn_dt_bias, gdn_norm_w, gdn_w_out, dswa_w_in, dswa_w_out, mlp_w1, mlp_w2, loss_target, m_norm_mix, m_norm_mlp, m_norm_final, m_rel_bias, m_gdn_w_in, m_gdn_conv_w, m_gdn_a_log, m_gdn_dt_bias, m_gdn_norm_w, m_gdn_w_out, m_dswa_w_in, m_dswa_w_out, m_mlp_w1, m_mlp_w2, v_norm_mix, v_norm_mlp, v_norm_final, v_rel_bias, v_gdn_w_in, v_gdn_conv_w, v_gdn_a_log, v_gdn_dt_bias, v_gdn_norm_w, v_gdn_w_out, v_dswa_w_in, v_dswa_w_out, v_mlp_w1, v_mlp_w2):
    params = dict(norm_mix=norm_mix, norm_mlp=norm_mlp, norm_final=norm_final, rel_bias=rel_bias,
                  gdn_w_in=gdn_w_in, gdn_conv_w=gdn_conv_w, gdn_a_log=gdn_a_log, gdn_dt_bias=gdn_dt_bias,
                  gdn_norm_w=gdn_norm_w, gdn_w_out=gdn_w_out, dswa_w_in=dswa_w_in, dswa_w_out=dswa_w_out,
                  mlp_w1=mlp_w1, mlp_w2=mlp_w2)
    mom_m = dict(norm_mix=m_norm_mix, norm_mlp=m_norm_mlp, norm_final=m_norm_final, rel_bias=m_rel_bias,
                 gdn_w_in=m_gdn_w_in, gdn_conv_w=m_gdn_conv_w, gdn_a_log=m_gdn_a_log, gdn_dt_bias=m_gdn_dt_bias,
                 gdn_norm_w=m_gdn_norm_w, gdn_w_out=m_gdn_w_out, dswa_w_in=m_dswa_w_in, dswa_w_out=m_dswa_w_out,
                 mlp_w1=m_mlp_w1, mlp_w2=m_mlp_w2)
    mom_v = dict(norm_mix=v_norm_mix, norm_mlp=v_norm_mlp, norm_final=v_norm_final, rel_bias=v_rel_bias,
                 gdn_w_in=v_gdn_w_in, gdn_conv_w=v_gdn_conv_w, gdn_a_log=v_gdn_a_log, gdn_dt_bias=v_gdn_dt_bias,
                 gdn_norm_w=v_gdn_norm_w, gdn_w_out=v_gdn_w_out, dswa_w_in=v_dswa_w_in, dswa_w_out=v_dswa_w_out,
                 mlp_w1=v_mlp_w1, mlp_w2=v_mlp_w2)
    xs = x[0]
    target = loss_target[0]
    conv_tail, _ = _pack_rows([gdn_conv_w], 8)
    gathered = _all_gather([params[n].astype(BF16) for n in _BIG] + [conv_tail],
                           [_KIND[n] for n in _BIG] + ["stack"], name="ag_weights")
    full = {n: (_from_col_shards(t) if _KIND[n] == "stack" else t) for n, t in zip(_BIG, gathered)}
    conv_parts = [_unpack_rows(gathered[-1][dev], [gdn_conv_w.shape])[0] for dev in range(N_DEV)]
    conv_full = _from_col_shards(jnp.stack(conv_parts))[:, :, 0, :]

    loss_part, dcur, g_big, rep, g_conv = _local_step(
        xs, target, dict(norm_mix=norm_mix, norm_mlp=norm_mlp, norm_final=norm_final, rel_bias=rel_bias,
                         gdn_a_log=gdn_a_log, gdn_dt_bias=gdn_dt_bias, gdn_norm_w=gdn_norm_w), full, conv_full)
    loss = lax.psum(loss_part[0, 0], ("x", "y", "c"))
    grad_x = dcur[None]

    sends = []
    for n in _BIG:
        for g in g_big[n]:
            sends.append(_col_shards(g, N_DEV) if _KIND[n] == "stack" else g)
    conv_dev = _col_shards(jnp.stack(g_conv)[:, :, None, :], N_DEV)
    sends.append(jnp.stack([_pack_rows([conv_dev[dev]] + [rep[n] for n in _SMALL[1:]], 8)[0] for dev in range(N_DEV)]))
    recvs = _exchange(sends, name="grad_exchange")

    outs = {}
    pos = 0
    for n in _BIG:
        nl = params[n].shape[0]
        res = _adamw_layers(recvs[pos:pos + nl], params[n], mom_m[n], mom_v[n], name=f"adamw_{n}")
        pos += nl
        for tag, t in zip(("grad", "delta", "new_m", "new_v"), res):
            outs[(tag, n)] = t
    w_slab, _ = _pack_rows([params[n] for n in _SMALL], 8)
    m_slab, _ = _pack_rows([mom_m[n] for n in _SMALL], 8)
    v_slab, _ = _pack_rows([mom_v[n] for n in _SMALL], 8)
    small = _adamw_reduce(recvs[-1], w_slab, m_slab, v_slab, name="adamw_small")
    shapes = [params[n].shape for n in _SMALL]
    for tag, slab in zip(("grad", "delta", "new_m", "new_v"), small):
        for n, t in zip(_SMALL, _unpack_rows(slab, shapes)):
            outs[(tag, n)] = t
    result = [loss, grad_x]
    for tag in ("grad", "delta", "new_m", "new_v"):
        result += [outs[(tag, n)] for n in _ORDER]
    return tuple(result)


def _local_step(xs, target, sp, full, conv_full):
    s = xs.shape[0]
    norm_mix, norm_mlp, norm_final = sp["norm_mix"], sp["norm_mlp"], sp["norm_final"]
    gdn_a_log, gdn_dt_bias, gdn_norm_w = sp["gdn_a_log"], sp["gdn_dt_bias"], sp["gdn_norm_w"]
    onehot = _bucket_onehot()
    table_t = sp["rel_bias"].T
    bias = _dswa_bias(table_t, onehot, name="dswa_bias").reshape(DSWA_HEADS, DSWA_HALF, 3 * DSWA_HALF)

    saved = []
    cur = xs
    for i in range(DEPTH):
        j = i // 2
        sv = dict(x_in=cur)
        h = _rms_fwd(cur, norm_mix[i], name=f"rms_mix_fwd{i}")
        sv["h"] = h
        if i % 2 == 0:
            w_in = full["gdn_w_in"][j]
            proj = _mm(h, w_in[:, :GDN_MAIN], name=f"gdn_proj{i}")
            ab = _mm(h, w_in[:, GDN_MAIN:], name=f"gdn_proj_ab{i}")
            qkvn = _gdn_pre_fwd(proj, conv_full[j], name=f"gdn_pre_fwd{i}")
            g_all, beta_all = _gdn_gate_fwd(ab[:, :2 * GDN_HEADS], ab[:, 2 * GDN_HEADS:], gdn_a_log[j], gdn_dt_bias[j],
                                            name=f"gdn_gate_fwd{i}")
            gshape = (2, GDN_HEADS, s // GDN_CHUNK, 1, GDN_CHUNK)
            g_row = g_all.T.reshape(gshape)
            b_row = beta_all.T.reshape(gshape)
            o, states = _gdn_chunk_fwd(qkvn, g_row, b_row, name=f"gdn_chunk_fwd{i}")
            act = _gdn_post_fwd(o, proj, gdn_norm_w[j], name=f"gdn_post_fwd{i}")
            sv.update(proj=proj, ab=ab, qkvn=qkvn, g_row=g_row, b_row=b_row, o=o, states=states, act=act)
            w_out = full["gdn_w_out"][j]
        else:
            w_in = full["dswa_w_in"][j]
            qkv = _mm(h, w_in, name=f"dswa_proj{i}", out_dtypes=(BF16,))
            qkvp = _dswa_permute(qkv.reshape(s, 3, DSWA_HEADS // 2, 2 * DSWA_E))
            qkvp = jnp.pad(qkvp, ((0, 0), (0, 0), (DSWA_HALF, DSWA_HALF), (0, 0)))
            o_p, lse_p = _dswa_attn_fwd(qkvp, bias, name=f"dswa_attn_fwd{i}")
            o_n, lse_n = _dswa_unpermute(o_p), _dswa_unpermute(lse_p)
            comb = _dswa_combine_fwd(o_n, lse_n, name=f"dswa_comb_fwd{i}")
            act = jnp.transpose(comb, (1, 0, 2)).reshape(s, DSWA_WIDTH)
            sv.update(qkvp=qkvp, lse_p=lse_p, o_n=o_n, lse_n=lse_n, act=act)
            w_out = full["dswa_w_out"][j]
        cur = _mm(act, w_out, name=f"mix_out{i}", epilogue=lambda acc, r: (acc + r,), extras=(cur,))
        sv["x_mid"] = cur
        h2 = _rms_fwd(cur, norm_mlp[i], name=f"rms_mlp_fwd{i}")
        u, a = _mm(h2, full["mlp_w1"][i], name=f"mlp_up{i}", out_dtypes=(F32, BF16),
                   epilogue=lambda acc: (acc, jnp.square(jnp.maximum(acc, 0.0))))
        cur = _mm(a, full["mlp_w2"][i], name=f"mlp_down{i}", epilogue=lambda acc, r: (acc + r,), extras=(cur,))
        sv.update(h2=h2, u=u, a=a)
        saved.append(sv)

    loss_part, dcur, dcur_b, dg_final = _loss_head(cur, norm_final, target, name="loss_head")

    g_norm_mix, g_norm_mlp = [None] * DEPTH, [None] * DEPTH
    g_big = {n: [None] * full[n].shape[0] for n in _BIG}
    g_conv, g_alog, g_dt, g_nw = [None] * 2, [None] * 2, [None] * 2, [None] * 2
    d_table_t = jnp.zeros((DSWA_HEADS, REL_BUCKETS), F32)
    for i in reversed(range(DEPTH)):
        j = i // 2
        sv = saved[i]
        w1, w2 = full["mlp_w1"][i], full["mlp_w2"][i]
        du = _mm(dcur_b, w2, tb=True, name=f"mlp_down_bwd{i}", out_dtypes=(BF16,),
                 epilogue=lambda acc, uu: (acc * (2.0 * jnp.maximum(uu, 0.0)),), extras=(sv["u"],))
        g_big["mlp_w2"][i] = _mm(sv["a"], dcur_b, ta=True, name=f"mlp_w2_grad{i}", out_dtypes=(BF16,), shard="rows")
        g_big["mlp_w1"][i] = _mm(sv["h2"], du, ta=True, name=f"mlp_w1_grad{i}", out_dtypes=(BF16,), shard="cols")
        dh2 = _mm(du, w1, tb=True, name=f"mlp_up_bwd{i}")
        dmid, dmid_b, g_norm_mlp[i] = _rms_bwd(sv["x_mid"], norm_mlp[i], dh2, dcur, name=f"rms_mlp_bwd{i}")
        if i % 2 == 0:
            w_in, w_out = full["gdn_w_in"][j], full["gdn_w_out"][j]
            dact = _mm(dmid_b, w_out, tb=True, name=f"mix_out_bwd{i}")
            g_big["gdn_w_out"][j] = _mm(sv["act"], dmid_b, ta=True, name=f"mix_out_grad{i}", out_dtypes=(BF16,),
                                        shard="rows")
            do, dz, g_nw[j] = _gdn_post_bwd(sv["o"], sv["proj"], gdn_norm_w[j], dact, name=f"gdn_post_bwd{i}")
            dqkvn, dg_row, db_row = _gdn_chunk_bwd(sv["qkvn"], sv["g_row"], sv["b_row"], sv["states"], do,
                                                   name=f"gdn_chunk_bwd{i}")
            dpre, g_conv[j] = _gdn_pre_bwd(sv["proj"], conv_full[j], dqkvn, name=f"gdn_pre_bwd{i}")
            nh2 = 2 * GDN_HEADS
            da_, db_, g_alog[j], g_dt[j] = _gdn_gate_bwd(sv["ab"][:, :nh2], sv["ab"][:, nh2:], gdn_a_log[j], gdn_dt_bias[j],
                                                         dg_row.reshape(nh2, s).T, db_row.reshape(nh2, s).T,
                                                         name=f"gdn_gate_bwd{i}")
            dab = jnp.concatenate([da_, db_], axis=1)
            dproj = jnp.concatenate([dpre, dz], axis=1)
            gw_main = _mm(sv["h"], dproj, ta=True, name=f"gdn_w_in_grad{i}", out_dtypes=(BF16,))
            gw_ab = _mm(sv["h"], dab, ta=True, name=f"gdn_w_ab_grad{i}", out_dtypes=(BF16,))
            g_big["gdn_w_in"][j] = jnp.concatenate([gw_main, gw_ab], axis=1)
            dh_ab = _mm(dab, w_in[:, GDN_MAIN:], tb=True, name=f"gdn_proj_ab_bwd{i}")
            dh = _mm(dproj, w_in[:, :GDN_MAIN], tb=True, name=f"gdn_proj_bwd{i}",
                     epilogue=lambda acc, r: (acc + r,), extras=(dh_ab,))
        else:
            w_in, w_out = full["dswa_w_in"][j], full["dswa_w_out"][j]
            dact = _mm(dmid_b, w_out, tb=True, name=f"mix_out_bwd{i}")
            g_big["dswa_w_out"][j] = _mm(sv["act"], dmid_b, ta=True, name=f"mix_out_grad{i}", out_dtypes=(BF16,),
                                         shard="rows")
            dc = jnp.transpose(dact.reshape(s, DSWA_HEADS // 2, 2 * DSWA_E), (1, 0, 2))
            do_n, corr_n = _dswa_combine_bwd(sv["o_n"], sv["lse_n"], dc, name=f"dswa_comb_bwd{i}")
            do_p, corr_p = _dswa_permute_heads(do_n), _dswa_permute_heads(corr_n)
            dq_p, dk_p, dv_p, dbias = _dswa_attn_bwd(sv["qkvp"], bias, sv["lse_p"], do_p, corr_p,
                                                     name=f"dswa_attn_bwd{i}")
            d_table_t = d_table_t + _dswa_dtable(dbias.reshape(DSWA_HEADS, -1), onehot, name=f"dswa_dtable{i}")
            hf = DSWA_HALF
            dqkv_p = jnp.stack([dq_p, dk_p[:, hf:-hf], dv_p[:, hf:-hf]])
            dqkv_n = jnp.stack([_dswa_unpermute(dqkv_p[t]) for t in range(3)])
            dqkv = jnp.transpose(dqkv_n, (2, 0, 1, 3)).reshape(s, 3 * DSWA_WIDTH).astype(BF16)
            g_big["dswa_w_in"][j] = _mm(sv["h"], dqkv, ta=True, name=f"dswa_w_in_grad{i}", out_dtypes=(BF16,))
            dh = _mm(dqkv, w_in, tb=True, name=f"dswa_proj_bwd{i}")
        dcur, dcur_b, g_norm_mix[i] = _rms_bwd(sv["x_in"], norm_mix[i], dh, dmid, name=f"rms_mix_bwd{i}")

    rep = dict(norm_mix=jnp.concatenate(g_norm_mix, axis=0), norm_mlp=jnp.concatenate(g_norm_mlp, axis=0),
               norm_final=dg_final.reshape(-1), rel_bias=d_table_t.T,
               gdn_a_log=jnp.stack(g_alog).reshape(gdn_a_log.shape), gdn_dt_bias=jnp.stack(g_dt).reshape(gdn_dt_bias.shape),
               gdn_norm_w=jnp.stack(g_nw).reshape(gdn_norm_w.shape))
    return loss_part, dcur, g_big, rep, g_conv
```

```python
import functools
import math

import jax
import jax.numpy as jnp
import numpy as np
from jax import lax
from jax.experimental import pallas as pl
from jax.experimental.pallas import tpu as pltpu

F32 = jnp.float32
BF16 = jnp.bfloat16
HP = lax.Precision.HIGHEST

N_DEV = 8
D_MODEL = 1024
DEPTH = 4
RMS_EPS = 1e-6
NEG_INF = -1e30

GDN_HEADS = 8
GDN_DK = 128
GDN_CONV = 5
GDN_CHUNK = 64
GDN_QKV = 3 * GDN_HEADS * GDN_DK
GDN_MAIN = GDN_QKV + GDN_HEADS * GDN_DK
GDN_AB = 4 * GDN_HEADS

DSWA_DILS = (1, 4, 16)
DSWA_HG = 6
DSWA_E = 64
DSWA_HEADS = 18
DSWA_WIDTH = DSWA_HEADS * DSWA_E
DSWA_HALF = 64
DSWA_PG = DSWA_HG // 2
DSWA_UNROLL = 4
REL_BUCKETS = 32
REL_MAX_DIST = 1024

ADAM_LR = 0.001
ADAM_B1 = 0.9
ADAM_B2 = 0.999
ADAM_EPS = 1e-08
ADAM_WD = 0.01
ADAM_STEP = 10

VMEM_LIMIT = 56 * 1024 * 1024


def _cparams(sem=None, **kw):
    return pltpu.CompilerParams(dimension_semantics=sem, vmem_limit_bytes=VMEM_LIMIT, **kw)


def _pick(dim, cands):
    for c in cands:
        if dim % c == 0:
            return c
    return dim


def _bdot(a, b):
    return jnp.dot(a.astype(BF16), b.astype(BF16), preferred_element_type=F32)


def _bdot_nt(a, b):
    return lax.dot_general(a.astype(BF16), b.astype(BF16), (((1,), (1,)), ((), ())),
                           preferred_element_type=F32)


def _bdot_tn(a, b):
    return lax.dot_general(a.astype(BF16), b.astype(BF16), (((0,), (0,)), ((), ())),
                           preferred_element_type=F32)


def _hdot(a, b):
    return jnp.dot(a, b, precision=HP, preferred_element_type=F32)


def _hdot_tn(a, b):
    return lax.dot_general(a, b, (((0,), (0,)), ((), ())), precision=HP, preferred_element_type=F32)


def _hdot_nt(a, b):
    return lax.dot_general(a, b, (((1,), (1,)), ((), ())), precision=HP, preferred_element_type=F32)


def _sigmoid(x):
    return 1.0 / (1.0 + jnp.exp(-x))


def _mm(a, b, *, name, ta=False, tb=False, out_dtypes=(F32,), epilogue=None, extras=(),
        tm=None, tn=None, tk=None, shard=None):
    if ta:
        kdim, m = a.shape
    else:
        m, kdim = a.shape
    n = b.shape[0] if tb else b.shape[1]
    if shard == "rows":
        tm = m // N_DEV if (m // N_DEV) % 128 == 0 else m
    if shard == "cols":
        tn = n // N_DEV
    tm = tm or _pick(m, (1024, 1152, 512, 384, 256, 128))
    tn = tn or _pick(n, (1024, 1152, 512, 384, 256, 128))
    tk = tk or _pick(kdim, (1024, 1152, 512, 384, 256, 128))
    nk = kdim // tk
    n_out = len(out_dtypes)
    n_ex = len(extras)
    rows_all = shard == "rows" and tm == m

    def body(*refs):
        a_ref, b_ref = refs[0], refs[1]
        ex_refs = refs[2:2 + n_ex]
        out_refs = refs[2 + n_ex:2 + n_ex + n_out]
        acc_ref = refs[-1]
        k = pl.program_id(2)

        @pl.when(k == 0)
        def _():
            acc_ref[...] = jnp.zeros_like(acc_ref)

        av = a_ref[...].astype(BF16)
        bv = b_ref[...].astype(BF16)
        dims = (((0 if ta else 1,), (1 if tb else 0,)), ((), ()))
        acc_ref[...] += lax.dot_general(av, bv, dims, preferred_element_type=F32)

        @pl.when(k == nk - 1)
        def _():
            acc = acc_ref[...]
            outs = (acc,) if epilogue is None else epilogue(acc, *[r[...] for r in ex_refs])
            for r, o in zip(out_refs, outs):
                if rows_all:
                    for p in range(N_DEV):
                        r[p] = o[p * (m // N_DEV):(p + 1) * (m // N_DEV)].astype(r.dtype)
                else:
                    r[...] = o.astype(r.dtype)

    a_spec = pl.BlockSpec((tk, tm), lambda i, j, k: (k, i)) if ta else pl.BlockSpec((tm, tk), lambda i, j, k: (i, k))
    b_spec = pl.BlockSpec((tn, tk), lambda i, j, k: (j, k)) if tb else pl.BlockSpec((tk, tn), lambda i, j, k: (k, j))
    o_spec = pl.BlockSpec((tm, tn), lambda i, j, k: (i, j))
    out_specs = [o_spec] * n_out
    out_shape = [jax.ShapeDtypeStruct((m, n), dt) for dt in out_dtypes]
    if shard == "rows":
        out_shape = [jax.ShapeDtypeStruct((N_DEV, m // N_DEV, n), out_dtypes[0])]
        out_specs = [pl.BlockSpec((N_DEV, m // N_DEV, tn), lambda i, j, k: (0, 0, j)) if rows_all
                     else pl.BlockSpec((None, tm, tn), lambda i, j, k: (i, 0, j))]
    if shard == "cols":
        out_shape = [jax.ShapeDtypeStruct((N_DEV, m, tn), out_dtypes[0])]
        out_specs = [pl.BlockSpec((None, tm, tn), lambda i, j, k: (j, i, 0))]
    outs = pl.pallas_call(
        body, name=name,
        grid=(m // tm, n // tn, nk),
        in_specs=[a_spec, b_spec] + [o_spec] * n_ex,
        out_specs=out_specs,
        out_shape=out_shape,
        scratch_shapes=[pltpu.VMEM((tm, tn), F32)],
        compiler_params=_cparams(("parallel", "parallel", "arbitrary")),
    )(a, b, *extras)
    return outs[0] if n_out == 1 else outs


def _rms_fwd(x, g, *, name):
    s, d = x.shape
    tr = _pick(s, (512, 256, 128))

    def body(x_ref, g_ref, h_ref):
        xv = x_ref[...]
        r = lax.rsqrt(jnp.mean(xv * xv, axis=-1, keepdims=True) + RMS_EPS)
        h_ref[...] = (xv * r * g_ref[...]).astype(h_ref.dtype)

    return pl.pallas_call(
        body, name=name, grid=(s // tr,),
        in_specs=[pl.BlockSpec((tr, d), lambda i: (i, 0)), pl.BlockSpec((1, d), lambda i: (0, 0))],
        out_specs=pl.BlockSpec((tr, d), lambda i: (i, 0)),
        out_shape=jax.ShapeDtypeStruct((s, d), BF16),
        compiler_params=_cparams(("parallel",)),
    )(x, g.reshape(1, d))


def _rms_bwd(x, g, dh, dres, *, name):
    s, d = x.shape
    tr = _pick(s, (512, 256, 128))

    def body(x_ref, g_ref, dh_ref, dres_ref, dx_ref, dxb_ref, dg_ref):
        i = pl.program_id(0)
        xv = x_ref[...]
        r = lax.rsqrt(jnp.mean(xv * xv, axis=-1, keepdims=True) + RMS_EPS)
        xn = xv * r
        dhv = dh_ref[...]
        dn = dhv * g_ref[...]
        dx = dres_ref[...] + r * (dn - xn * jnp.mean(dn * xn, axis=-1, keepdims=True))
        dx_ref[...] = dx
        dxb_ref[...] = dx.astype(dxb_ref.dtype)
        part = jnp.sum(dhv * xn, axis=0, keepdims=True)

        @pl.when(i == 0)
        def _():
            dg_ref[...] = part

        @pl.when(i > 0)
        def _():
            dg_ref[...] += part

    row = pl.BlockSpec((tr, d), lambda i: (i, 0))
    vec = pl.BlockSpec((1, d), lambda i: (0, 0))
    return pl.pallas_call(
        body, name=name, grid=(s // tr,),
        in_specs=[row, vec, row, row], out_specs=[row, row, vec],
        out_shape=[jax.ShapeDtypeStruct((s, d), F32), jax.ShapeDtypeStruct((s, d), BF16),
                   jax.ShapeDtypeStruct((1, d), F32)],
        compiler_params=_cparams(("arbitrary",)),
    )(x, g.reshape(1, d), dh, dres)


def _loss_head(x, g, target, *, name):
    s, d = x.shape
    tr = _pick(s, (512, 256, 128))

    def body(x_ref, g_ref, t_ref, loss_ref, dx_ref, dxb_ref, dg_ref):
        i = pl.program_id(0)
        xv = x_ref[...]
        gv = g_ref[...]
        r = lax.rsqrt(jnp.mean(xv * xv, axis=-1, keepdims=True) + RMS_EPS)
        xn = xv * r
        err = xn * gv - t_ref[...]
        lpart = 0.5 * jnp.sum(jnp.mean(err * err, axis=-1, keepdims=True), axis=0, keepdims=True)
        dy = err * (1.0 / d)
        dn = dy * gv
        dx = r * (dn - xn * jnp.mean(dn * xn, axis=-1, keepdims=True))
        dx_ref[...] = dx
        dxb_ref[...] = dx.astype(dxb_ref.dtype)
        gpart = jnp.sum(dy * xn, axis=0, keepdims=True)

        @pl.when(i == 0)
        def _():
            dg_ref[...] = gpart
            loss_ref[...] = lpart

        @pl.when(i > 0)
        def _():
            dg_ref[...] += gpart
            loss_ref[...] += lpart

    row = pl.BlockSpec((tr, d), lambda i: (i, 0))
    vec = pl.BlockSpec((1, d), lambda i: (0, 0))
    one = pl.BlockSpec((1, 1), lambda i: (0, 0))
    return pl.pallas_call(
        body, name=name, grid=(s // tr,),
        in_specs=[row, vec, row], out_specs=[one, row, row, vec],
        out_shape=[jax.ShapeDtypeStruct((1, 1), F32), jax.ShapeDtypeStruct((s, d), F32),
                   jax.ShapeDtypeStruct((s, d), BF16), jax.ShapeDtypeStruct((1, d), F32)],
        compiler_params=_cparams(("arbitrary",)),
    )(x, g.reshape(1, d), target)


def _shift_rows(x, sft, rows):
    s = x.shape[0]
    if sft == 0:
        return x
    y = pltpu.roll(x, (-sft) % s, 0)
    ok = (rows + sft >= 0) & (rows + sft < s)
    return jnp.where(ok, y, 0.0)


def _gdn_pre_fwd(proj, conv_w, *, name):
    s = proj.shape[0]
    nblk = GDN_QKV // 128
    pad = GDN_CONV // 2

    def body(x_ref, w_ref, o_ref):
        j = pl.program_id(0)
        x = x_ref[...]
        rows = lax.broadcasted_iota(jnp.int32, x.shape, 0)
        c = jnp.zeros_like(x)
        for t in range(GDN_CONV):
            c = c + w_ref[pl.ds(t, 1), :] * _shift_rows(x, t - pad, rows)
        a = c * _sigmoid(c)
        rinv = lax.rsqrt(jnp.sum(a * a, axis=-1, keepdims=True) + 1e-6)
        scale = jnp.where(j < GDN_HEADS, GDN_DK ** -0.5, 1.0)
        o_ref[...] = jnp.where(j >= 2 * GDN_HEADS, a, a * (rinv * scale))

    return pl.pallas_call(
        body, name=name, grid=(nblk,),
        in_specs=[pl.BlockSpec((s, 128), lambda j: (0, j)), pl.BlockSpec((GDN_CONV, 128), lambda j: (0, j))],
        out_specs=pl.BlockSpec((s, 128), lambda j: (0, j)),
        out_shape=jax.ShapeDtypeStruct((s, GDN_QKV), F32),
        compiler_params=_cparams(("parallel",)),
    )(proj, conv_w)


def _gdn_pre_bwd(proj, conv_w, dqkv, *, name):
    s = proj.shape[0]
    nblk = GDN_QKV // 128
    pad = GDN_CONV // 2

    def body(x_ref, w_ref, d_ref, dx_ref, dw_ref):
        j = pl.program_id(0)
        x = x_ref[...]
        rows = lax.broadcasted_iota(jnp.int32, x.shape, 0)
        xs = [_shift_rows(x, t - pad, rows) for t in range(GDN_CONV)]
        c = jnp.zeros_like(x)
        for t in range(GDN_CONV):
            c = c + w_ref[pl.ds(t, 1), :] * xs[t]
        sg = _sigmoid(c)
        a = c * sg
        rinv = lax.rsqrt(jnp.sum(a * a, axis=-1, keepdims=True) + 1e-6)
        scale = jnp.where(j < GDN_HEADS, GDN_DK ** -0.5, 1.0)
        dy = d_ref[0] + d_ref[1]
        nh = a * rinv
        da_n = (rinv * scale) * (dy - nh * jnp.sum(dy * nh, axis=-1, keepdims=True))
        da = jnp.where(j >= 2 * GDN_HEADS, dy, da_n)
        dc = da * (sg * (1.0 + c * (1.0 - sg)))
        dx = jnp.zeros_like(x)
        for t in range(GDN_CONV):
            dx = dx + w_ref[pl.ds(t, 1), :] * _shift_rows(dc, pad - t, rows)
            dw_ref[pl.ds(t, 1), :] = jnp.sum(dc * xs[t], axis=0, keepdims=True)
        dx_ref[...] = dx.astype(dx_ref.dtype)

    col = pl.BlockSpec((s, 128), lambda j: (0, j))
    wsp = pl.BlockSpec((GDN_CONV, 128), lambda j: (0, j))
    return pl.pallas_call(
        body, name=name, grid=(nblk,),
        in_specs=[col, wsp, pl.BlockSpec((2, s, 128), lambda j: (0, 0, j))], out_specs=[col, wsp],
        out_shape=[jax.ShapeDtypeStruct((s, GDN_QKV), BF16), jax.ShapeDtypeStruct((GDN_CONV, GDN_QKV), F32)],
        compiler_params=_cparams(("parallel",)),
    )(proj, conv_w, dqkv)


def _softplus(x):
    return jnp.maximum(x, 0.0) + jnp.log(1.0 + jnp.exp(-jnp.abs(x)))


def _gdn_gate_fwd(a, b, a_log, dt_bias, *, name):
    s = a.shape[0]
    nh = 2 * GDN_HEADS

    def body(a_ref, b_ref, al_ref, dt_ref, g_ref, be_ref):
        g_ref[...] = -jnp.exp(al_ref[...]) * _softplus(a_ref[...] + dt_ref[...])
        be_ref[...] = _sigmoid(b_ref[...])

    return pl.pallas_call(
        body, name=name,
        out_shape=[jax.ShapeDtypeStruct((s, nh), F32), jax.ShapeDtypeStruct((s, nh), F32)],
        compiler_params=_cparams(),
    )(a, b, a_log.reshape(1, nh), dt_bias.reshape(1, nh))


def _gdn_gate_bwd(a, b, a_log, dt_bias, dg, dbeta, *, name):
    s = a.shape[0]
    nh = 2 * GDN_HEADS

    def body(a_ref, b_ref, al_ref, dt_ref, dg_ref, db_ref, da_ref, dbb_ref, dal_ref, ddt_ref):
        ea = jnp.exp(al_ref[...])
        z = a_ref[...] + dt_ref[...]
        dgv = dg_ref[...]
        dz = dgv * (-ea) * _sigmoid(z)
        dal_ref[...] = jnp.sum(dgv * (-ea) * _softplus(z), axis=0, keepdims=True)
        ddt_ref[...] = jnp.sum(dz, axis=0, keepdims=True)
        sb = _sigmoid(b_ref[...])
        da_ref[...] = dz
        dbb_ref[...] = db_ref[...] * sb * (1.0 - sb)

    return pl.pallas_call(
        body, name=name,
        out_shape=[jax.ShapeDtypeStruct((s, nh), F32), jax.ShapeDtypeStruct((s, nh), F32),
                   jax.ShapeDtypeStruct((1, nh), F32), jax.ShapeDtypeStruct((1, nh), F32)],
        compiler_params=_cparams(),
    )(a, b, a_log.reshape(1, nh), dt_bias.reshape(1, nh), dg, dbeta)


def _chunk_masks(d):
    c = GDN_CHUNK
    ii = lax.broadcasted_iota(jnp.int32, (c, c), 0)
    jj = lax.broadcasted_iota(jnp.int32, (c, c), 1)
    dif = (ii - jj) * (1 - 2 * d)
    mi = dif >= 0
    mit = dif <= 0
    ms = dif > 0
    eye = ii == jj
    bds = [(ii >> sh) == (jj >> sh) for sh in (3, 4, 5)]
    return dict(mi=mi, mit=mit, ms=ms, eye=eye, bds=bds,
                mif=mi.astype(F32), mitf=mit.astype(F32), eyef=eye.astype(F32))


class _V:
    def __init__(self, xs):
        self.xs = tuple(xs)

    def __add__(self, o):
        return _lift(lambda a, b: a + b)(self, o)

    def __radd__(self, o):
        return _lift(lambda a, b: b + a)(self, o)

    def __sub__(self, o):
        return _lift(lambda a, b: a - b)(self, o)

    def __rsub__(self, o):
        return _lift(lambda a, b: b - a)(self, o)

    def __mul__(self, o):
        return _lift(lambda a, b: a * b)(self, o)

    def __rmul__(self, o):
        return _lift(lambda a, b: b * a)(self, o)

    def __and__(self, o):
        return _lift(lambda a, b: a & b)(self, o)

    def __neg__(self):
        return _lift(lambda a: -a)(self)

    def __rtruediv__(self, o):
        return _lift(lambda a, b: b / a)(self, o)


def _lift(f):
    def g(*args, **kw):
        n = next(len(a.xs) for a in args if isinstance(a, _V))
        return _V(f(*[a.xs[i] if isinstance(a, _V) else a for a in args], **kw) for i in range(n))
    return g


_vwhere, _vsum, _vexp, _vnot = _lift(jnp.where), _lift(jnp.sum), _lift(jnp.exp), _lift(jnp.logical_not)
_vhdot, _vhdot_tn = _lift(_bdot), _lift(_bdot_tn)
_vbdot, _vbdot_nt, _vbdot_tn = _lift(_bdot), _lift(_bdot_nt), _lift(_bdot_tn)
_vcat = _lift(lambda a, b: jnp.concatenate([a, b], axis=1))
_vlo = _lift(lambda a, n: a[:, :n])
_vhi = _lift(lambda a, n: a[:, n:])


def _stack_masks(d, n):
    m = _chunk_masks(d)
    mk = {key: _V([m[key]] * n) for key in m if key != "bds"}
    mk["bds"] = [_V([m["bds"][i]] * n) for i in range(3)]
    return mk


def _tri_inv(a, mk):
    eyef = mk["eyef"]
    bd8, bd16, bd32 = mk["bds"]
    a8 = _vwhere(bd8, a, 0.0)
    a2 = _vhdot(a8, a8)
    a4 = _vhdot(a2, a2)
    t = _vhdot(_vhdot(eyef - a8, eyef + a2), eyef + a4)
    for inner, outer in ((bd8, bd16), (bd16, bd32), (bd32, None)):
        off = _vnot(inner) if outer is None else (outer & _vnot(inner))
        low = _vwhere(off, a, 0.0)
        t = t - _vhdot(_vhdot(t, low), t)
    return t


def _chunk_prep(q, k, v, g_row, b_row, mk):
    dv = GDN_DK
    g_col = _vsum(mk["eyef"] * g_row, axis=1, keepdims=True)
    b_col = _vsum(mk["eyef"] * b_row, axis=1, keepdims=True)
    gc_col = _vsum(mk["mif"] * g_row, axis=1, keepdims=True)
    gc_row = _vsum(mk["mitf"] * g_col, axis=0, keepdims=True)
    gl = _vsum(g_row, axis=1, keepdims=True)
    decay = _vwhere(mk["mi"], _vexp(_vwhere(mk["mi"], gc_col - gc_row, 0.0)), 0.0)
    eg = _vexp(gc_col)
    e2 = _vexp(gl - gc_col)
    egl = _vexp(gl)
    kb = k * b_col
    pm = _vbdot_nt(kb, k)
    a = _vwhere(mk["ms"], pm * decay, 0.0)
    t = _tri_inv(a, mk)
    sol = _vhdot(t, _vcat(v * b_col, kb * eg))
    u, w = _vlo(sol, dv), _vhi(sol, dv)
    qm = _vbdot_nt(q, k)
    return dict(b_col=b_col, decay=decay, eg=eg, e2=e2, egl=egl, kb=kb, pm=pm, t=t, u=u, w=w,
                qm=qm, intra=qm * decay, qd=q * eg, kd=k * e2)


def _chunk_fwd_step(p, state):
    v_new = p["u"] - _vbdot(p["w"], state)
    o = _vbdot(p["qd"], state) + _vbdot(p["intra"], v_new)
    new_state = state * p["egl"] + _vbdot_tn(p["kd"], v_new)
    return o, new_state


def _chunk_bwd_step(q, k, v, p, mk, state, dso, do):
    dv_dim = GDN_DK
    v_new = p["u"] - _vbdot(p["w"], state)
    dvn = _vbdot_tn(p["intra"], do) + _vbdot(p["kd"], dso)
    dintra = _vbdot_nt(do, v_new)
    dqd = _vbdot_nt(do, state)
    ds = p["egl"] * dso + _vbdot_tn(p["qd"], do) - _vbdot_tn(p["w"], dvn)
    dkd = _vbdot_nt(v_new, dso)
    dgl = _vsum(_vsum(dso * state, axis=1, keepdims=True), axis=0, keepdims=True) * p["egl"]
    dw = -_vbdot_nt(dvn, state)
    drhs = _vhdot_tn(p["t"], _vcat(dvn, dw))
    dru, drw = _vlo(drhs, dv_dim), _vhi(drhs, dv_dim)
    da = -_vwhere(mk["ms"], _vbdot_nt(drhs, _vcat(p["u"], p["w"])), 0.0)
    b_col = p["b_col"]
    dv = dru * b_col
    dbeta = _vsum(dru * v, axis=1, keepdims=True)
    dkb = drw * p["eg"]
    deg = _vsum(drw * p["kb"], axis=1, keepdims=True)
    dp = da * p["decay"]
    ddecay = da * p["pm"]
    dkb = dkb + _vbdot(dp, k)
    dk = _vbdot_tn(dp, p["kb"])
    dqm = dintra * p["decay"]
    ddecay = ddecay + dintra * p["qm"]
    dq = _vbdot(dqm, k)
    dk = dk + _vbdot_tn(dqm, q)
    dd = ddecay * p["decay"]
    dgc_col = _vsum(dd, axis=1, keepdims=True)
    dgc_row = -_vsum(dd, axis=0, keepdims=True)
    dq = dq + dqd * p["eg"]
    deg = deg + _vsum(dqd * q, axis=1, keepdims=True)
    dk = dk + dkd * p["e2"]
    de2 = _vsum(dkd * k, axis=1, keepdims=True) * p["e2"]
    dgl = dgl + _vsum(de2, axis=0, keepdims=True)
    dgc_col = dgc_col - de2 + deg * p["eg"]
    dk = dk + dkb * b_col
    dbeta = dbeta + _vsum(dkb * k, axis=1, keepdims=True)
    dgc_col = dgc_col + _vsum(mk["eyef"] * dgc_row, axis=1, keepdims=True)
    dg_row = _vsum(mk["mif"] * dgc_col, axis=0, keepdims=True) + dgl
    dbeta_row = _vsum(mk["eyef"] * dbeta, axis=0, keepdims=True)
    return dq, dk, dv, dg_row, dbeta_row, ds


def _gdn_chunk_fwd(qkvn, g5, b5, *, name, comm=None):
    s = qkvn.shape[0]
    c = GDN_CHUNK
    nc = s // c
    h_, dk = GDN_HEADS, GDN_DK

    def body(*refs):
        d, n = pl.program_id(0), pl.program_id(1)
        (x_ref, g_ref, b_ref), (o_ref, st_ref), (st_scr,) = _comm_hooks(
            comm, refs, 3, 2, 1, (d == 0) & (n == 0), (d == 1) & (n == 0), (d == 1) & (n == nc - 1))

        @pl.when(n == 0)
        def _():
            st_scr[...] = jnp.zeros_like(st_scr)

        mk = _stack_masks(d, h_)
        q, k, v = (_V(x_ref[:, (t * h_ + h) * dk:(t * h_ + h + 1) * dk] for h in range(h_)) for t in range(3))
        g, b = (_V(r[0, h, 0] for h in range(h_)) for r in (g_ref, b_ref))
        state = _V(st_scr[h] for h in range(h_))
        o, new_state = _chunk_fwd_step(_chunk_prep(q, k, v, g, b, mk), state)
        for h in range(h_):
            st_ref[0, h, 0] = state.xs[h]
            st_scr[h] = new_state.xs[h]
            o_ref[0, :, h * dk:(h + 1) * dk] = o.xs[h]

    ce = lambda d, n: n + d * (nc - 1 - 2 * n)
    gate = pl.BlockSpec((1, h_, 1, 1, c), lambda d, n: (d, 0, ce(d, n), 0, 0))
    c_in, c_out, c_shape, c_scr = _comm_specs(comm)
    res = pl.pallas_call(
        body, name=name, grid=(2, nc),
        in_specs=[pl.BlockSpec((c, 3 * h_ * dk), lambda d, n: (ce(d, n), 0)), gate, gate] + c_in,
        out_specs=[pl.BlockSpec((1, c, h_ * dk), lambda d, n: (d, ce(d, n), 0)),
                   pl.BlockSpec((1, h_, 1, dk, dk), lambda d, n: (d, 0, ce(d, n), 0, 0))] + c_out,
        out_shape=[jax.ShapeDtypeStruct((2, s, h_ * dk), F32), jax.ShapeDtypeStruct((2, h_, nc, dk, dk), F32)] + c_shape,
        scratch_shapes=[pltpu.VMEM((h_, dk, dk), F32)] + c_scr,
        compiler_params=_cparams(("arbitrary", "arbitrary")),
    )(qkvn, g5, b5, *(comm.arrays if comm else []))
    return res[0], res[1], res[2:]


def _gdn_chunk_bwd(qkvn, g5, b5, states, do, *, name, comm=None):
    s = qkvn.shape[0]
    c = GDN_CHUNK
    nc = s // c
    h_, dk = GDN_HEADS, GDN_DK

    def body(*refs):
        d, i = pl.program_id(0), pl.program_id(1)
        (x_ref, g_ref, b_ref, st_ref, do_ref), (dx_ref, dg_ref, db_ref), (ds_scr,) = _comm_hooks(
            comm, refs, 5, 3, 1, (d == 0) & (i == 0), (d == 1) & (i == 0), (d == 1) & (i == nc - 1))

        @pl.when(i == 0)
        def _():
            ds_scr[...] = jnp.zeros_like(ds_scr)

        mk = _stack_masks(d, h_)
        q, k, v = (_V(x_ref[:, (t * h_ + h) * dk:(t * h_ + h + 1) * dk] for h in range(h_)) for t in range(3))
        g, b = (_V(r[0, h, 0] for h in range(h_)) for r in (g_ref, b_ref))
        state = _V(st_ref[0, h, 0] for h in range(h_))
        dso = _V(ds_scr[h] for h in range(h_))
        dov = _V(do_ref[:, h * dk:(h + 1) * dk] for h in range(h_))
        res = _chunk_bwd_step(q, k, v, _chunk_prep(q, k, v, g, b, mk), mk, state, dso, dov)
        for h, (dq, dkk, dvv, dg_r, db_r, ds) in enumerate(zip(*[r.xs for r in res])):
            ds_scr[h] = ds
            dg_ref[0, h, 0] = dg_r
            db_ref[0, h, 0] = db_r
            for t, val in enumerate((dq, dkk, dvv)):
                dx_ref[0, :, (t * h_ + h) * dk:(t * h_ + h + 1) * dk] = val

    ce = lambda d, i: (nc - 1 - i) + d * (2 * i - nc + 1)
    gate = pl.BlockSpec((1, h_, 1, 1, c), lambda d, i: (d, 0, ce(d, i), 0, 0))
    c_in, c_out, c_shape, c_scr = _comm_specs(comm)
    res = pl.pallas_call(
        body, name=name, grid=(2, nc),
        in_specs=[pl.BlockSpec((c, 3 * h_ * dk), lambda d, i: (ce(d, i), 0)), gate, gate,
                  pl.BlockSpec((1, h_, 1, dk, dk), lambda d, i: (d, 0, ce(d, i), 0, 0)),
                  pl.BlockSpec((c, h_ * dk), lambda d, i: (ce(d, i), 0))] + c_in,
        out_specs=[pl.BlockSpec((1, c, 3 * h_ * dk), lambda d, i: (d, ce(d, i), 0)), gate, gate] + c_out,
        out_shape=[jax.ShapeDtypeStruct((2, s, 3 * h_ * dk), F32)]
        + [jax.ShapeDtypeStruct((2, h_, nc, 1, c), F32)] * 2 + c_shape,
        scratch_shapes=[pltpu.VMEM((h_, dk, dk), F32)] + c_scr,
        compiler_params=_cparams(("arbitrary", "arbitrary")),
    )(qkvn, g5, b5, states, do, *(comm.arrays if comm else []))
    return res[0], res[1], res[2], res[3:]


def _gdn_post_fwd(o, z, norm_w, *, name):
    s = o.shape[1]
    h_, dk = GDN_HEADS, GDN_DK

    def body(o_ref, z_ref, w_ref, a_ref):
        ov = o_ref[0] + o_ref[1]
        zv = z_ref[...]
        r = lax.rsqrt(jnp.mean(ov * ov, axis=-1, keepdims=True) + RMS_EPS)
        a_ref[...] = (ov * r * w_ref[...] * (zv * _sigmoid(zv))).astype(a_ref.dtype)

    return pl.pallas_call(
        body, name=name, grid=(h_,),
        in_specs=[pl.BlockSpec((2, s, dk), lambda h: (0, 0, h)), pl.BlockSpec((s, dk), lambda h: (0, 3 * h_ + h)),
                  pl.BlockSpec((1, dk), lambda h: (0, 0))],
        out_specs=pl.BlockSpec((s, dk), lambda h: (0, h)),
        out_shape=jax.ShapeDtypeStruct((s, h_ * dk), BF16),
        compiler_params=_cparams(("parallel",)),
    )(o, z, norm_w.reshape(1, dk))


def _gdn_post_bwd(o, z, norm_w, dact, *, name):
    s = o.shape[1]
    h_, dk = GDN_HEADS, GDN_DK

    def body(o_ref, z_ref, w_ref, da_ref, do_ref, dz_ref, dw_ref):
        h = pl.program_id(0)
        ov = o_ref[0] + o_ref[1]
        zv = z_ref[...]
        wv = w_ref[...]
        dav = da_ref[...]
        r = lax.rsqrt(jnp.mean(ov * ov, axis=-1, keepdims=True) + RMS_EPS)
        nrm = ov * r
        sg = _sigmoid(zv)
        sz = zv * sg
        dn = dav * wv * sz
        do_ref[...] = r * (dn - nrm * jnp.mean(dn * nrm, axis=-1, keepdims=True))
        dz_ref[...] = (dav * nrm * wv * (sg * (1.0 + zv * (1.0 - sg)))).astype(dz_ref.dtype)
        part = jnp.sum(dav * nrm * sz, axis=0, keepdims=True)

        @pl.when(h == 0)
        def _():
            dw_ref[...] = part

        @pl.when(h > 0)
        def _():
            dw_ref[...] += part

    col = pl.BlockSpec((s, dk), lambda h: (0, h))
    vec = pl.BlockSpec((1, dk), lambda h: (0, 0))
    return pl.pallas_call(
        body, name=name, grid=(h_,),
        in_specs=[pl.BlockSpec((2, s, dk), lambda h: (0, 0, h)), pl.BlockSpec((s, dk), lambda h: (0, 3 * h_ + h)), vec, col],
        out_specs=[col, col, vec],
        out_shape=[jax.ShapeDtypeStruct((s, h_ * dk), F32), jax.ShapeDtypeStruct((s, h_ * dk), BF16),
                   jax.ShapeDtypeStruct((1, dk), F32)],
        compiler_params=_cparams(("arbitrary",)),
    )(o, z, norm_w.reshape(1, dk), dact)


def _rel_bucket(rel):
    nb = REL_BUCKETS // 2
    max_exact = nb // 2
    ret = jnp.where(rel > 0, nb, 0)
    n = jnp.abs(rel)
    nf = jnp.maximum(n, 1).astype(F32)
    large = max_exact + (jnp.log(nf / max_exact) / math.log(REL_MAX_DIST / max_exact)
                         * (nb - max_exact)).astype(jnp.int32)
    large = jnp.minimum(large, nb - 1)
    return ret + jnp.where(n < max_exact, n, large)


def _bucket_onehot():
    half = DSWA_HALF
    outs = []
    for dil in DSWA_DILS:
        rel = (jnp.arange(3 * half)[None, :] - half - jnp.arange(half)[:, None]) * dil
        outs.append(jax.nn.one_hot(_rel_bucket(rel).reshape(-1), REL_BUCKETS, dtype=F32, axis=0))
    return jnp.stack(outs)


def _head_group_select(vals):
    rows = lax.broadcasted_iota(jnp.int32, vals[0].shape, 0)
    return jnp.where(rows < DSWA_HG, vals[0], jnp.where(rows < 2 * DSWA_HG, vals[1], vals[2]))


def _dswa_bias(table_t, onehot, *, name):
    p = onehot.shape[-1]

    def body(t_ref, oh_ref, b_ref):
        b_ref[...] = _head_group_select([_hdot(t_ref[...], oh_ref[g]) for g in range(3)])

    return pl.pallas_call(body, name=name, out_shape=jax.ShapeDtypeStruct((DSWA_HEADS, p), F32),
                          compiler_params=_cparams())(table_t, onehot)


def _dswa_dtable(dbias, onehot, *, name):
    def body(d_ref, oh_ref, t_ref):
        t_ref[...] = _head_group_select([_hdot_nt(d_ref[...], oh_ref[g]) for g in range(3)])

    return pl.pallas_call(body, name=name, out_shape=jax.ShapeDtypeStruct((DSWA_HEADS, REL_BUCKETS), F32),
                          compiler_params=_cparams())(dbias, onehot)


def _attn_valid(blk, h, s):
    half = DSWA_HALF
    nblocks = s // half
    nbs = jnp.where(h < DSWA_HG, nblocks // DSWA_DILS[0],
                    jnp.where(h < 2 * DSWA_HG, nblocks // DSWA_DILS[1], nblocks // DSWA_DILS[2]))
    b = blk & (nbs - 1)
    ii = lax.broadcasted_iota(jnp.int32, (half, 3 * half), 0)
    jj = lax.broadcasted_iota(jnp.int32, (half, 3 * half), 1)
    off = jj - half - ii
    return (jnp.abs(off) <= half) & ((jj >= half) | (b > 0)) & ((jj < 2 * half) | (b < nbs - 1))


def _attn_chains(qkv_ref, pr, it, s):
    half = DSWA_HALF
    lane = lax.broadcasted_iota(jnp.int32, (half, 2 * DSWA_E), 1)
    r0s, qm, kw, vw, valid, hmask = [], [], [], [], [], []
    for u in range(DSWA_UNROLL):
        blk = it * DSWA_UNROLL + u
        r0 = pl.multiple_of(blk * half, half)
        r0s.append(r0)
        q = qkv_ref[0, 0, pl.ds(r0 + half, half), :]
        k = qkv_ref[1, 0, pl.ds(r0, 3 * half), :]
        v = qkv_ref[2, 0, pl.ds(r0, 3 * half), :]
        ok = _attn_valid(blk, 2 * pr, s)
        for hd in range(2):
            mine = (lane < DSWA_E) if hd == 0 else (lane >= DSWA_E)
            qm.append(jnp.where(mine, q, jnp.zeros_like(q)))
            kw.append(k)
            vw.append(v)
            valid.append(ok)
            hmask.append(mine)
    return r0s, _V(qm), _V(kw), _V(vw), _V(valid), _V(hmask)


_vmax, _vlog = _lift(jnp.max), _lift(jnp.log)


def _dswa_attn_fwd(qkvp, bias, *, name, comm=None):
    sp = qkvp.shape[2]
    half, e = DSWA_HALF, DSWA_E
    s = sp - 2 * half
    npair = DSWA_HEADS // 2

    def body(*refs):
        pr = pl.program_id(0)
        (qkv_ref, bias_ref), (o_ref, lse_ref), _ = _comm_hooks(
            comm, refs, 2, 2, 0, pr == 0, pr == npair // 2, pr == npair - 1)
        bias_v = _V([bias_ref[0], bias_ref[1]] * DSWA_UNROLL)

        def step(it, carry):
            r0s, qm, kw, vw, valid, hmask = _attn_chains(qkv_ref, pr, it, s)
            sc = _vwhere(valid, _vbdot_nt(qm, kw) * (e ** -0.5) + bias_v, NEG_INF)
            m = _vmax(sc, axis=-1, keepdims=True)
            p = _vexp(sc - m)
            l = _vsum(p, axis=-1, keepdims=True)
            o = _vbdot(p * (1.0 / l), vw)
            lse = m + _vlog(l)
            for u, r0 in enumerate(r0s):
                is_a = hmask.xs[2 * u]
                o_ref[0, pl.ds(r0, half), :] = jnp.where(is_a, o.xs[2 * u], o.xs[2 * u + 1])
                lse_ref[0, pl.ds(r0, half), :] = jnp.where(is_a, lse.xs[2 * u], lse.xs[2 * u + 1])
            return carry

        lax.fori_loop(0, s // half // DSWA_UNROLL, step, 0)

    pair = pl.BlockSpec((1, s, 2 * e), lambda p: (p, 0, 0))
    c_in, c_out, c_shape, c_scr = _comm_specs(comm)
    res = pl.pallas_call(
        body, name=name, grid=(npair,),
        in_specs=[pl.BlockSpec((3, 1, sp, 2 * e), lambda p: (0, p, 0, 0)),
                  pl.BlockSpec((2, half, 3 * half), lambda p: (p, 0, 0))] + c_in,
        out_specs=[pair, pair] + c_out,
        out_shape=[jax.ShapeDtypeStruct((npair, s, 2 * e), F32)] * 2 + c_shape,
        scratch_shapes=c_scr,
        compiler_params=_cparams(("arbitrary",)),
    )(qkvp, bias, *(comm.arrays if comm else []))
    return res[0], res[1], res[2:]


def _dswa_attn_bwd(qkvp, bias, lse, do, corr, *, name, comm=None):
    sp = qkvp.shape[2]
    half, e = DSWA_HALF, DSWA_E
    s = sp - 2 * half
    npair = DSWA_HEADS // 2

    def body(*refs):
        pr = pl.program_id(0)
        (qkv_ref, bias_ref, lse_ref, do_ref, corr_ref), (dq_ref, dk_ref, dv_ref, db_ref), _ = _comm_hooks(
            comm, refs, 5, 4, 0, pr == 0, pr == npair // 2, pr == npair - 1)
        bias_v = _V([bias_ref[0], bias_ref[1]] * DSWA_UNROLL)
        dk_ref[...] = jnp.zeros_like(dk_ref)
        dv_ref[...] = jnp.zeros_like(dv_ref)

        def step(it, dbias):
            r0s, qm, kw, vw, valid, hmask = _attn_chains(qkv_ref, pr, it, s)
            hd = [0, 1] * DSWA_UNROLL
            rows = [r0 for r0 in r0s for _ in range(2)]
            lse_c = _V(lse_ref[0, pl.ds(r0, half), :][:, h * e:h * e + 1] for r0, h in zip(rows, hd))
            corr_c = _V(corr_ref[0, pl.ds(r0, half), :][:, h * e:h * e + 1] for r0, h in zip(rows, hd))
            dov = _vwhere(hmask, _V(do_ref[0, pl.ds(r0, half), :] for r0 in rows), 0.0)
            sc = _vbdot_nt(qm, kw) * (e ** -0.5) + bias_v
            p = _vwhere(valid, _vexp(_vwhere(valid, sc, 0.0) - lse_c), 0.0)
            dsc = p * (_vbdot_nt(dov, vw) + corr_c)
            dq = _vbdot(dsc, kw) * (e ** -0.5)
            dkc = _vbdot_tn(dsc, qm) * (e ** -0.5)
            dvc = _vbdot_tn(p, dov)
            for u, r0 in enumerate(r0s):
                dq_ref[0, pl.ds(r0, half), :] = jnp.where(hmask.xs[2 * u], dq.xs[2 * u], dq.xs[2 * u + 1])
                dk_ref[0, pl.ds(r0, 3 * half), :] += dkc.xs[2 * u] + dkc.xs[2 * u + 1]
                dv_ref[0, pl.ds(r0, 3 * half), :] += dvc.xs[2 * u] + dvc.xs[2 * u + 1]
            da, db = dbias
            for u in range(DSWA_UNROLL):
                da, db = da + dsc.xs[2 * u], db + dsc.xs[2 * u + 1]
            return da, db

        zero = jnp.zeros((half, 3 * half), F32)
        da, db = lax.fori_loop(0, s // half // DSWA_UNROLL, step, (zero, zero))
        db_ref[0] = da
        db_ref[1] = db

    ps = pl.BlockSpec((1, s, 2 * e), lambda p: (p, 0, 0))
    pp = pl.BlockSpec((1, sp, 2 * e), lambda p: (p, 0, 0))
    bs = pl.BlockSpec((2, half, 3 * half), lambda p: (p, 0, 0))
    c_in, c_out, c_shape, c_scr = _comm_specs(comm)
    res = pl.pallas_call(
        body, name=name, grid=(npair,),
        in_specs=[pl.BlockSpec((3, 1, sp, 2 * e), lambda p: (0, p, 0, 0)), bs, ps, ps, ps] + c_in,
        out_specs=[ps, pp, pp, bs] + c_out,
        out_shape=[jax.ShapeDtypeStruct((npair, s, 2 * e), F32), jax.ShapeDtypeStruct((npair, sp, 2 * e), F32),
                   jax.ShapeDtypeStruct((npair, sp, 2 * e), F32),
                   jax.ShapeDtypeStruct((DSWA_HEADS, half, 3 * half), F32)] + c_shape,
        scratch_shapes=c_scr,
        compiler_params=_cparams(("arbitrary",)),
    )(qkvp, bias, lse, do, corr, *(comm.arrays if comm else []))
    return res[0], res[1], res[2], res[3], res[4:]


def _group_weights(l_ref, j):
    ls = [l_ref[g * DSWA_PG + j] for g in range(3)]
    m = jnp.maximum(jnp.maximum(ls[0], ls[1]), ls[2])
    es = [jnp.exp(x - m) for x in ls]
    inv = 1.0 / (es[0] + es[1] + es[2])
    return [x * inv for x in es]


def _dswa_combine_fwd(o, lse, *, name):
    s = o.shape[1]
    tr = _pick(s, (512, 256, 128))

    def body(o_ref, l_ref, c_ref):
        for j in range(DSWA_PG):
            al = _group_weights(l_ref, j)
            for g in range(3):
                c_ref[g * DSWA_PG + j] = (o_ref[g * DSWA_PG + j] * al[g]).astype(c_ref.dtype)

    big = pl.BlockSpec((DSWA_HEADS // 2, tr, 2 * DSWA_E), lambda i: (0, i, 0))
    return pl.pallas_call(
        body, name=name, grid=(s // tr,),
        in_specs=[big, big], out_specs=big,
        out_shape=jax.ShapeDtypeStruct(o.shape, BF16),
        compiler_params=_cparams(("parallel",)),
    )(o, lse)


def _dswa_combine_bwd(o, lse, dc, *, name):
    s = o.shape[1]
    tr = _pick(s, (512, 256, 128))

    def body(o_ref, l_ref, dc_ref, do_ref, corr_ref):
        lane = lax.broadcasted_iota(jnp.int32, (tr, 2 * DSWA_E), 1)
        is_a = lane < DSWA_E
        for j in range(DSWA_PG):
            al = _group_weights(l_ref, j)
            tot = jnp.zeros((tr, 2 * DSWA_E), F32)
            for g in range(3):
                pi = g * DSWA_PG + j
                dcv = dc_ref[pi]
                do_ref[pi] = dcv * al[g]
                prod = dcv * o_ref[pi]
                dal = jnp.where(is_a, jnp.sum(jnp.where(is_a, prod, 0.0), axis=-1, keepdims=True),
                                jnp.sum(jnp.where(is_a, 0.0, prod), axis=-1, keepdims=True))
                tot = tot + al[g] * dal
            for g in range(3):
                corr_ref[g * DSWA_PG + j] = -al[g] * tot

    big = pl.BlockSpec((DSWA_HEADS // 2, tr, 2 * DSWA_E), lambda i: (0, i, 0))
    return pl.pallas_call(
        body, name=name, grid=(s // tr,),
        in_specs=[big, big, big], out_specs=[big, big],
        out_shape=[jax.ShapeDtypeStruct(o.shape, F32)] * 2,
        compiler_params=_cparams(("parallel",)),
    )(o, lse, dc)


def _dswa_permute(t):
    s = t.shape[0]
    mid = t.shape[1:-2]
    x = t.shape[-1]
    nm = len(mid)
    parts = []
    for gi, dil in enumerate(DSWA_DILS):
        tg = t[..., gi * DSWA_PG:(gi + 1) * DSWA_PG, :].reshape((s // dil, dil) + mid + (DSWA_PG, x))
        perm = tuple(range(2, 2 + nm)) + (2 + nm, 1, 0, 3 + nm)
        parts.append(jnp.transpose(tg, perm).reshape(mid + (DSWA_PG, s, x)))
    return jnp.concatenate(parts, axis=nm)


def _dswa_unpermute(t):
    s, x = t.shape[1], t.shape[2]
    parts = []
    for gi, dil in enumerate(DSWA_DILS):
        tg = t[gi * DSWA_PG:(gi + 1) * DSWA_PG].reshape(DSWA_PG, dil, s // dil, x)
        parts.append(jnp.swapaxes(tg, 1, 2).reshape(DSWA_PG, s, x))
    return jnp.concatenate(parts, axis=0)


def _dswa_permute_heads(t):
    s, x = t.shape[1], t.shape[2]
    parts = []
    for gi, dil in enumerate(DSWA_DILS):
        tg = t[gi * DSWA_PG:(gi + 1) * DSWA_PG].reshape(DSWA_PG, s // dil, dil, x)
        parts.append(jnp.swapaxes(tg, 1, 2).reshape(DSWA_PG, s, x))
    return jnp.concatenate(parts, axis=0)


class _Comm:
    def __init__(self, mode, arrays, kinds=None):
        self.mode, self.arrays, self.kinds = mode, list(arrays), kinds
        self.n = len(self.arrays)

    def out_shapes(self):
        if self.mode == "exchange":
            return [jax.ShapeDtypeStruct(x.shape, x.dtype) for x in self.arrays]
        shapes = []
        for x, kd in zip(self.arrays, self.kinds):
            shp = list(x.shape)
            if kd == "stack":
                shp = [N_DEV] + shp
            else:
                shp[-2 if kd == "rows" else -1] *= N_DEV
            shapes.append(jax.ShapeDtypeStruct(tuple(shp), x.dtype))
        return shapes

    def scratch(self):
        return [pltpu.SemaphoreType.DMA((7 * self.n,)), pltpu.SemaphoreType.DMA((7 * self.n,)),
                pltpu.SemaphoreType.DMA((self.n,))]

    def bind(self, in_refs, out_refs, sems):
        self.x, self.o = in_refs, out_refs
        self.send_sems, self.recv_sems, self.local_sems = sems
        self.pos = (lax.axis_index("x"), lax.axis_index("y"), lax.axis_index("c"))

    def _slot(self, i, px, py, pc):
        p = 4 * px + 2 * py + pc
        kd = self.kinds[i]
        if kd == "stack":
            return self.o[i].at[p]
        nd = len(self.x[i].shape)
        ax = nd - 2 if kd == "rows" else nd - 1
        size = self.x[i].shape[ax]
        idx = tuple(pl.ds(p * size, size) if a == ax else slice(None) for a in range(nd))
        return self.o[i].at[idx]

    def _gcopy(self, i, k, block, to, src=None):
        return pltpu.make_async_remote_copy(
            src_ref=self._slot(i, *block) if src is None else src, dst_ref=self._slot(i, *block),
            send_sem=self.send_sems.at[7 * i + k], recv_sem=self.recv_sems.at[7 * i + k],
            device_id=to, device_id_type=pl.DeviceIdType.MESH)

    def _chips(self):
        mx, my, _ = self.pos
        return [(1 - mx, my), (mx, 1 - my), (1 - mx, 1 - my)]

    def _xcopies(self):
        mx, my, mc = self.pos
        me = 4 * mx + 2 * my + mc
        copies = []
        for k in range(1, N_DEV):
            px = 1 - mx if (k >> 2) & 1 else mx
            py = 1 - my if (k >> 1) & 1 else my
            pc = 1 - mc if k & 1 else mc
            for i in range(self.n):
                copies.append(pltpu.make_async_remote_copy(
                    src_ref=self.x[i].at[4 * px + 2 * py + pc], dst_ref=self.o[i].at[me],
                    send_sem=self.send_sems.at[7 * i + k - 1], recv_sem=self.recv_sems.at[7 * i + k - 1],
                    device_id=(px, py, pc), device_id_type=pl.DeviceIdType.MESH))
        return copies

    def _local(self):
        mx, my, mc = self.pos
        if self.mode == "exchange":
            me = 4 * mx + 2 * my + mc
            return [pltpu.make_async_copy(self.x[i].at[me], self.o[i].at[me], self.local_sems.at[i]) for i in range(self.n)]
        return [pltpu.make_async_copy(self.x[i], self._slot(i, mx, my, mc), self.local_sems.at[i]) for i in range(self.n)]

    def _first(self):
        mx, my, mc = self.pos
        me, sibling = (mx, my, mc), (mx, my, 1 - mc)
        first = [self._gcopy(i, 0, me, sibling, src=self.x[i]) for i in range(self.n)]
        first += [self._gcopy(i, 1 + j, me, (*chip, mc), src=self.x[i]) for j, chip in enumerate(self._chips())
                  for i in range(self.n)]
        return first

    def _passed(self):
        mx, my, mc = self.pos
        return [self._gcopy(i, 4 + j, (*chip, mc), (mx, my, 1 - mc)) for j, chip in enumerate(self._chips())
                for i in range(self.n)]

    def start(self):
        for cp in self._local() + (self._xcopies() if self.mode == "exchange" else self._first()):
            cp.start()

    def mid(self):
        if self.mode == "exchange":
            return
        mx, my, mc = self.pos
        passed = self._passed()
        for j, chip in enumerate(self._chips()):
            for i in range(self.n):
                self._gcopy(i, 1 + j, (*chip, mc), (mx, my, mc)).wait_recv()
                passed[j * self.n + i].start()

    def end(self):
        mx, my, mc = self.pos
        if self.mode == "exchange":
            copies = self._xcopies()
            for cp in copies:
                cp.wait_recv()
            for cp in copies:
                cp.wait_send()
        else:
            for i in range(self.n):
                self._gcopy(i, 0, (mx, my, 1 - mc), (mx, my, mc)).wait_recv()
                for j, chip in enumerate(self._chips()):
                    self._gcopy(i, 4 + j, (*chip, 1 - mc), (mx, my, mc)).wait_recv()
            for cp in self._first() + self._passed():
                cp.wait_send()
        for cp in self._local():
            cp.wait()

    def run(self, *, name):
        n = self.n

        def body(*refs):
            self.bind(refs[:n], refs[n:2 * n], refs[2 * n:])
            self.start()
            self.mid()
            self.end()

        anyspec = pl.BlockSpec(memory_space=pl.ANY)
        return pl.pallas_call(body, name=name, in_specs=[anyspec] * n, out_specs=[anyspec] * n,
                              out_shape=self.out_shapes(), scratch_shapes=self.scratch())(*self.arrays)


def _comm_specs(comm):
    if comm is None:
        return [], [], [], []
    anyspec = pl.BlockSpec(memory_space=pl.ANY)
    return [anyspec] * comm.n, [anyspec] * comm.n, comm.out_shapes(), comm.scratch()


def _comm_hooks(comm, refs, n_in, n_out, n_scr, first, mid, last):
    if comm is None:
        return refs[:n_in], refs[n_in:n_in + n_out], refs[n_in + n_out:]
    c = comm.n
    ins, cin = refs[:n_in], refs[n_in:n_in + c]
    outs, cout = refs[n_in + c:n_in + c + n_out], refs[n_in + c + n_out:n_in + 2 * c + n_out]
    scr, sems = refs[n_in + 2 * c + n_out:n_in + 2 * c + n_out + n_scr], refs[n_in + 2 * c + n_out + n_scr:]
    comm.bind(cin, cout, sems)
    pl.when(first)(comm.start)
    pl.when(mid)(comm.mid)
    pl.when(last)(comm.end)
    return ins, outs, scr


def _adamw_update(g, w, m, v):
    mn = ADAM_B1 * m + (1.0 - ADAM_B1) * g
    vn = ADAM_B2 * v + (1.0 - ADAM_B2) * (g * g)
    m_hat = mn / (1.0 - ADAM_B1 ** ADAM_STEP)
    v_hat = vn / (1.0 - ADAM_B2 ** ADAM_STEP)
    return -ADAM_LR * (m_hat / (jnp.sqrt(v_hat) + ADAM_EPS) + ADAM_WD * w), mn, vn


def _adamw_layers(recvs, w, m, v, *, name):
    nl, ks, ns = w.shape
    tr = _pick(ks, (64, 48))

    def body(*refs):
        rv_refs = refs[:nl]
        w_ref, m_ref, v_ref, g_ref, d_ref, nm_ref, nv_ref = refs[nl:]
        for l in range(nl):
            g = rv_refs[l][0].astype(F32)
            for q in range(1, N_DEV):
                g = g + rv_refs[l][q].astype(F32)
            delta, mn, vn = _adamw_update(g, w_ref[l], m_ref[l], v_ref[l])
            g_ref[l] = g
            d_ref[l] = delta
            nm_ref[l] = mn
            nv_ref[l] = vn

    row = pl.BlockSpec((nl, tr, ns), lambda i: (0, i, 0))
    return pl.pallas_call(
        body, name=name, grid=(ks // tr,),
        in_specs=[pl.BlockSpec((N_DEV, tr, ns), lambda i: (0, i, 0))] * nl + [row] * 3,
        out_specs=[row] * 4,
        out_shape=[jax.ShapeDtypeStruct((nl, ks, ns), F32)] * 4,
        compiler_params=_cparams(("parallel",)),
    )(*recvs, w, m, v)


def _adamw_reduce(recv, w, m, v, *, name):
    r, c = w.shape
    tr = _pick(r, (128, 64, 8))

    def body(rv_ref, w_ref, m_ref, v_ref, g_ref, d_ref, nm_ref, nv_ref):
        g = rv_ref[0]
        for q in range(1, N_DEV):
            g = g + rv_ref[q]
        delta, mn, vn = _adamw_update(g, w_ref[...], m_ref[...], v_ref[...])
        g_ref[...] = g
        d_ref[...] = delta
        nm_ref[...] = mn
        nv_ref[...] = vn

    row = pl.BlockSpec((tr, c), lambda i: (i, 0))
    return pl.pallas_call(
        body, name=name, grid=(r // tr,),
        in_specs=[pl.BlockSpec((N_DEV, tr, c), lambda i: (0, i, 0)), row, row, row],
        out_specs=[row] * 4,
        out_shape=[jax.ShapeDtypeStruct((r, c), F32)] * 4,
        compiler_params=_cparams(("parallel",)),
    )(recv, w, m, v)


_BIG = ("gdn_w_in", "gdn_w_out", "dswa_w_in", "dswa_w_out", "mlp_w1", "mlp_w2")
_SMALL = ("gdn_conv_w", "norm_mix", "norm_mlp", "norm_final", "rel_bias", "gdn_a_log", "gdn_dt_bias", "gdn_norm_w")
_ORDER = ("norm_mix", "norm_mlp", "norm_final", "rel_bias", "gdn_w_in", "gdn_conv_w", "gdn_a_log", "gdn_dt_bias",
          "gdn_norm_w", "gdn_w_out", "dswa_w_in", "dswa_w_out", "mlp_w1", "mlp_w2")
_KIND = dict(gdn_w_in="stack", gdn_w_out="rows", dswa_w_in="stack", dswa_w_out="rows", mlp_w1="cols", mlp_w2="rows")


def _pack_rows(arrs, align):
    rows, counts = [], []
    for a in arrs:
        flat = a.reshape(-1)
        n = -(-flat.shape[0] // D_MODEL)
        flat = jnp.pad(flat, (0, n * D_MODEL - flat.shape[0]))
        rows.append(flat.reshape(n, D_MODEL))
        counts.append(n)
    out = jnp.concatenate(rows, axis=0)
    total = -(-out.shape[0] // align) * align
    return jnp.pad(out, ((0, total - out.shape[0]), (0, 0))), counts


def _unpack_rows(slab, shapes):
    outs, r = [], 0
    for shp in shapes:
        size = int(np.prod(shp))
        n = -(-size // D_MODEL)
        outs.append(slab[r:r + n].reshape(-1)[:size].reshape(shp))
        r += n
    return outs


def _col_shards(full, nshard):
    lead = full.shape[:-1]
    n = full.shape[-1] // nshard
    t = full.reshape(lead + (nshard, n))
    return jnp.moveaxis(t, -2, 0)


def _from_col_shards(g):
    t = jnp.moveaxis(g, 0, -2)
    return t.reshape(t.shape[:-2] + (t.shape[-2] * t.shape[-1],))


def kernel(x, norm_mix, norm_mlp, norm_final, rel_bias, gdn_w_in, gdn_conv_w, gdn_a_log, gdn_dt_bias, gdn_norm_w, gdn_w_out, dswa_w_in, dswa_w_out, mlp_w1, mlp_w2, loss_target, m_norm_mix, m_norm_mlp, m_norm_final, m_rel_bias, m_gdn_w_in, m_gdn_conv_w, m_gdn_a_log, m_gdn_dt_bias, m_gdn_norm_w, m_gdn_w_out, m_dswa_w_in, m_dswa_w_out, m_mlp_w1, m_mlp_w2, v_norm_mix, v_norm_mlp, v_norm_final, v_rel_bias, v_gdn_w_in, v_gdn_conv_w, v_gdn_a_log, v_gdn_dt_bias, v_gdn_norm_w, v_gdn_w_out, v_dswa_w_in, v_dswa_w_out, v_mlp_w1, v_mlp_w2):
    params = dict(norm_mix=norm_mix, norm_mlp=norm_mlp, norm_final=norm_final, rel_bias=rel_bias,
                  gdn_w_in=gdn_w_in, gdn_conv_w=gdn_conv_w, gdn_a_log=gdn_a_log, gdn_dt_bias=gdn_dt_bias,
                  gdn_norm_w=gdn_norm_w, gdn_w_out=gdn_w_out, dswa_w_in=dswa_w_in, dswa_w_out=dswa_w_out,
                  mlp_w1=mlp_w1, mlp_w2=mlp_w2)
    mom_m = dict(norm_mix=m_norm_mix, norm_mlp=m_norm_mlp, norm_final=m_norm_final, rel_bias=m_rel_bias,
                 gdn_w_in=m_gdn_w_in, gdn_conv_w=m_gdn_conv_w, gdn_a_log=m_gdn_a_log, gdn_dt_bias=m_gdn_dt_bias,
                 gdn_norm_w=m_gdn_norm_w, gdn_w_out=m_gdn_w_out, dswa_w_in=m_dswa_w_in, dswa_w_out=m_dswa_w_out,
                 mlp_w1=m_mlp_w1, mlp_w2=m_mlp_w2)
    mom_v = dict(norm_mix=v_norm_mix, norm_mlp=v_norm_mlp, norm_final=v_norm_final, rel_bias=v_rel_bias,
                 gdn_w_in=v_gdn_w_in, gdn_conv_w=v_gdn_conv_w, gdn_a_log=v_gdn_a_log, gdn_dt_bias=v_gdn_dt_bias,
                 gdn_norm_w=v_gdn_norm_w, gdn_w_out=v_gdn_w_out, dswa_w_in=v_dswa_w_in, dswa_w_out=v_dswa_w_out,
                 mlp_w1=v_mlp_w1, mlp_w2=v_mlp_w2)
    xs = x[0]
    target = loss_target[0]
    dist = _Dist(params)
    conv_tail, _ = _pack_rows([gdn_conv_w], 8)
    (conv_g,) = dist.put("start", dist.gather_comm("start", extra=[(conv_tail, "stack")]).run(name="ag_start"))
    conv_parts = [_unpack_rows(conv_g[dev], [gdn_conv_w.shape])[0] for dev in range(N_DEV)]
    conv_full = _from_col_shards(jnp.stack(conv_parts))[:, :, 0, :]

    loss_part, dcur, g_big, rep, g_conv = _local_step(
        xs, target, dict(norm_mix=norm_mix, norm_mlp=norm_mlp, norm_final=norm_final, rel_bias=rel_bias,
                         gdn_a_log=gdn_a_log, gdn_dt_bias=gdn_dt_bias, gdn_norm_w=gdn_norm_w), dist.full, conv_full, dist)
    loss = lax.psum(loss_part[0, 0], ("x", "y", "c"))
    grad_x = dcur[None]

    conv_dev = _col_shards(jnp.stack(g_conv)[:, :, None, :], N_DEV)
    small_send = jnp.stack([_pack_rows([conv_dev[dev]] + [rep[n] for n in _SMALL[1:]], 8)[0] for dev in range(N_DEV)])
    (small_recv,) = dist.got("end", dist.send_comm("end", g_big, extra=[small_send]).run(name="grad_exchange"))

    outs = {}
    for n in _BIG:
        recvs = [dist.recv[(n, l)] for l in range(params[n].shape[0])]
        res = _adamw_layers(recvs, params[n], mom_m[n], mom_v[n], name=f"adamw_{n}")
        for tag, t in zip(("grad", "delta", "new_m", "new_v"), res):
            outs[(tag, n)] = t
    w_slab, _ = _pack_rows([params[n] for n in _SMALL], 8)
    m_slab, _ = _pack_rows([mom_m[n] for n in _SMALL], 8)
    v_slab, _ = _pack_rows([mom_v[n] for n in _SMALL], 8)
    small = _adamw_reduce(small_recv, w_slab, m_slab, v_slab, name="adamw_small")
    shapes = [params[n].shape for n in _SMALL]
    for tag, slab in zip(("grad", "delta", "new_m", "new_v"), small):
        for n, t in zip(_SMALL, _unpack_rows(slab, shapes)):
            outs[(tag, n)] = t
    result = [loss, grad_x]
    for tag in ("grad", "delta", "new_m", "new_v"):
        result += [outs[(tag, n)] for n in _ORDER]
    return tuple(result)


_GATHER = {
    "start": (("gdn_w_in", 0), ("gdn_w_out", 0), ("mlp_w1", 0), ("mlp_w2", 0)),
    "chunk_fwd0": (("dswa_w_in", 0), ("dswa_w_out", 0), ("mlp_w1", 1), ("mlp_w2", 1), ("gdn_w_in", 1), ("gdn_w_out", 1)),
    "attn_fwd1": (("mlp_w1", 2), ("mlp_w2", 2)),
    "chunk_fwd2": (("dswa_w_in", 1), ("dswa_w_out", 1), ("mlp_w1", 3), ("mlp_w2", 3)),
}
_SEND = {
    "attn_bwd3": (("mlp_w1", 3), ("mlp_w2", 3)),
    "chunk_bwd2": (("dswa_w_in", 1), ("dswa_w_out", 1), ("mlp_w1", 2), ("mlp_w2", 2)),
    "attn_bwd1": (("gdn_w_in", 1), ("gdn_w_out", 1)),
    "chunk_bwd0": (("mlp_w1", 1), ("mlp_w2", 1), ("dswa_w_in", 0), ("dswa_w_out", 0), ("mlp_w1", 0), ("mlp_w2", 0)),
    "end": (("gdn_w_in", 0), ("gdn_w_out", 0)),
}


class _Dist:
    def __init__(self, params):
        self.shards = {n: params[n].astype(BF16) for n in _BIG}
        self.full = {n: [None] * params[n].shape[0] for n in _BIG}
        self.recv = {}

    def gather_comm(self, tag, extra=()):
        if tag not in _GATHER:
            return None
        arrays = [self.shards[n][l] for n, l in _GATHER[tag]] + [a for a, _ in extra]
        return _Comm("gather", arrays, [_KIND[n] for n, _ in _GATHER[tag]] + [k for _, k in extra])

    def put(self, tag, outs):
        for (n, l), t in zip(_GATHER.get(tag, ()), outs):
            self.full[n][l] = _from_col_shards(t) if _KIND[n] == "stack" else t
        return outs[len(_GATHER.get(tag, ())):]

    def send_comm(self, tag, g_big, extra=()):
        if tag not in _SEND:
            return None
        arrays = [_col_shards(g_big[n][l], N_DEV) if _KIND[n] == "stack" else g_big[n][l] for n, l in _SEND[tag]]
        return _Comm("exchange", arrays + list(extra))

    def got(self, tag, outs):
        for item, t in zip(_SEND.get(tag, ()), outs):
            self.recv[item] = t
        return outs[len(_SEND.get(tag, ())):]


def _local_step(xs, target, sp, full, conv_full, dist=None):
    s = xs.shape[0]
    norm_mix, norm_mlp, norm_final = sp["norm_mix"], sp["norm_mlp"], sp["norm_final"]
    gdn_a_log, gdn_dt_bias, gdn_norm_w = sp["gdn_a_log"], sp["gdn_dt_bias"], sp["gdn_norm_w"]
    onehot = _bucket_onehot()
    table_t = sp["rel_bias"].T
    bias = _dswa_bias(table_t, onehot, name="dswa_bias").reshape(DSWA_HEADS, DSWA_HALF, 3 * DSWA_HALF)

    saved = []
    cur = xs
    for i in range(DEPTH):
        j = i // 2
        sv = dict(x_in=cur)
        h = _rms_fwd(cur, norm_mix[i], name=f"rms_mix_fwd{i}")
        sv["h"] = h
        if i % 2 == 0:
            w_in = full["gdn_w_in"][j]
            proj = _mm(h, w_in[:, :GDN_MAIN], name=f"gdn_proj{i}")
            ab = _mm(h, w_in[:, GDN_MAIN:], name=f"gdn_proj_ab{i}")
            qkvn = _gdn_pre_fwd(proj, conv_full[j], name=f"gdn_pre_fwd{i}")
            g_all, beta_all = _gdn_gate_fwd(ab[:, :2 * GDN_HEADS], ab[:, 2 * GDN_HEADS:], gdn_a_log[j], gdn_dt_bias[j],
                                            name=f"gdn_gate_fwd{i}")
            gshape = (2, GDN_HEADS, s // GDN_CHUNK, 1, GDN_CHUNK)
            g_row = g_all.T.reshape(gshape)
            b_row = beta_all.T.reshape(gshape)
            o, states, got = _gdn_chunk_fwd(qkvn, g_row, b_row, name=f"gdn_chunk_fwd{i}",
                                            comm=dist and dist.gather_comm(f"chunk_fwd{i}"))
            if dist:
                dist.put(f"chunk_fwd{i}", got)
            act = _gdn_post_fwd(o, proj, gdn_norm_w[j], name=f"gdn_post_fwd{i}")
            sv.update(proj=proj, ab=ab, qkvn=qkvn, g_row=g_row, b_row=b_row, o=o, states=states, act=act)
            w_out = full["gdn_w_out"][j]
        else:
            w_in = full["dswa_w_in"][j]
            qkv = _mm(h, w_in, name=f"dswa_proj{i}", out_dtypes=(BF16,))
            qkvp = _dswa_permute(qkv.reshape(s, 3, DSWA_HEADS // 2, 2 * DSWA_E))
            qkvp = jnp.pad(qkvp, ((0, 0), (0, 0), (DSWA_HALF, DSWA_HALF), (0, 0)))
            o_p, lse_p, got = _dswa_attn_fwd(qkvp, bias, name=f"dswa_attn_fwd{i}",
                                             comm=dist and dist.gather_comm(f"attn_fwd{i}"))
            if dist:
                dist.put(f"attn_fwd{i}", got)
            o_n, lse_n = _dswa_unpermute(o_p), _dswa_unpermute(lse_p)
            comb = _dswa_combine_fwd(o_n, lse_n, name=f"dswa_comb_fwd{i}")
            act = jnp.transpose(comb, (1, 0, 2)).reshape(s, DSWA_WIDTH)
            sv.update(qkvp=qkvp, lse_p=lse_p, o_n=o_n, lse_n=lse_n, act=act)
            w_out = full["dswa_w_out"][j]
        cur = _mm(act, w_out, name=f"mix_out{i}", epilogue=lambda acc, r: (acc + r,), extras=(cur,))
        sv["x_mid"] = cur
        h2 = _rms_fwd(cur, norm_mlp[i], name=f"rms_mlp_fwd{i}")
        u, a = _mm(h2, full["mlp_w1"][i], name=f"mlp_up{i}", out_dtypes=(F32, BF16),
                   epilogue=lambda acc: (acc, jnp.square(jnp.maximum(acc, 0.0))))
        cur = _mm(a, full["mlp_w2"][i], name=f"mlp_down{i}", epilogue=lambda acc, r: (acc + r,), extras=(cur,))
        sv.update(h2=h2, u=u, a=a)
        saved.append(sv)

    loss_part, dcur, dcur_b, dg_final = _loss_head(cur, norm_final, target, name="loss_head")

    g_norm_mix, g_norm_mlp = [None] * DEPTH, [None] * DEPTH
    g_big = {n: [None] * len(full[n]) for n in _BIG}
    g_conv, g_alog, g_dt, g_nw = [None] * 2, [None] * 2, [None] * 2, [None] * 2
    d_table_t = jnp.zeros((DSWA_HEADS, REL_BUCKETS), F32)
    for i in reversed(range(DEPTH)):
        j = i // 2
        sv = saved[i]
        w1, w2 = full["mlp_w1"][i], full["mlp_w2"][i]
        du = _mm(dcur_b, w2, tb=True, name=f"mlp_down_bwd{i}", out_dtypes=(BF16,),
                 epilogue=lambda acc, uu: (acc * (2.0 * jnp.maximum(uu, 0.0)),), extras=(sv["u"],))
        g_big["mlp_w2"][i] = _mm(sv["a"], dcur_b, ta=True, name=f"mlp_w2_grad{i}", out_dtypes=(BF16,), shard="rows")
        g_big["mlp_w1"][i] = _mm(sv["h2"], du, ta=True, name=f"mlp_w1_grad{i}", out_dtypes=(BF16,), shard="cols")
        dh2 = _mm(du, w1, tb=True, name=f"mlp_up_bwd{i}")
        dmid, dmid_b, g_norm_mlp[i] = _rms_bwd(sv["x_mid"], norm_mlp[i], dh2, dcur, name=f"rms_mlp_bwd{i}")
        if i % 2 == 0:
            w_in, w_out = full["gdn_w_in"][j], full["gdn_w_out"][j]
            dact = _mm(dmid_b, w_out, tb=True, name=f"mix_out_bwd{i}")
            g_big["gdn_w_out"][j] = _mm(sv["act"], dmid_b, ta=True, name=f"mix_out_grad{i}", out_dtypes=(BF16,),
                                        shard="rows")
            do, dz, g_nw[j] = _gdn_post_bwd(sv["o"], sv["proj"], gdn_norm_w[j], dact, name=f"gdn_post_bwd{i}")
            dqkvn, dg_row, db_row, got = _gdn_chunk_bwd(sv["qkvn"], sv["g_row"], sv["b_row"], sv["states"], do,
                                                        name=f"gdn_chunk_bwd{i}",
                                                        comm=dist and dist.send_comm(f"chunk_bwd{i}", g_big))
            if dist:
                dist.got(f"chunk_bwd{i}", got)
            dpre, g_conv[j] = _gdn_pre_bwd(sv["proj"], conv_full[j], dqkvn, name=f"gdn_pre_bwd{i}")
            nh2 = 2 * GDN_HEADS
            da_, db_, g_alog[j], g_dt[j] = _gdn_gate_bwd(sv["ab"][:, :nh2], sv["ab"][:, nh2:], gdn_a_log[j], gdn_dt_bias[j],
                                                         dg_row.reshape(nh2, s).T, db_row.reshape(nh2, s).T,
                                                         name=f"gdn_gate_bwd{i}")
            dab = jnp.concatenate([da_, db_], axis=1)
            dproj = jnp.concatenate([dpre, dz], axis=1)
            gw_main = _mm(sv["h"], dproj, ta=True, name=f"gdn_w_in_grad{i}", out_dtypes=(BF16,))
            gw_ab = _mm(sv["h"], dab, ta=True, name=f"gdn_w_ab_grad{i}", out_dtypes=(BF16,))
            g_big["gdn_w_in"][j] = jnp.concatenate([gw_main, gw_ab], axis=1)
            dh_ab = _mm(dab, w_in[:, GDN_MAIN:], tb=True, name=f"gdn_proj_ab_bwd{i}")
            dh = _mm(dproj, w_in[:, :GDN_MAIN], tb=True, name=f"gdn_proj_bwd{i}",
                     epilogue=lambda acc, r: (acc + r,), extras=(dh_ab,))
        else:
            w_in, w_out = full["dswa_w_in"][j], full["dswa_w_out"][j]
            dact = _mm(dmid_b, w_out, tb=True, name=f"mix_out_bwd{i}")
            g_big["dswa_w_out"][j] = _mm(sv["act"], dmid_b, ta=True, name=f"mix_out_grad{i}", out_dtypes=(BF16,),
                                         shard="rows")
            dc = jnp.transpose(dact.reshape(s, DSWA_HEADS // 2, 2 * DSWA_E), (1, 0, 2))
            do_n, corr_n = _dswa_combine_bwd(sv["o_n"], sv["lse_n"], dc, name=f"dswa_comb_bwd{i}")
            do_p, corr_p = _dswa_permute_heads(do_n), _dswa_permute_heads(corr_n)
            dq_p, dk_p, dv_p, dbias, got = _dswa_attn_bwd(sv["qkvp"], bias, sv["lse_p"], do_p, corr_p,
                                                          name=f"dswa_attn_bwd{i}",
                                                          comm=dist and dist.send_comm(f"attn_bwd{i}", g_big))
            if dist:
                dist.got(f"attn_bwd{i}", got)
            d_table_t = d_table_t + _dswa_dtable(dbias.reshape(DSWA_HEADS, -1), onehot, name=f"dswa_dtable{i}")
            hf = DSWA_HALF
            dqkv_p = jnp.stack([dq_p, dk_p[:, hf:-hf], dv_p[:, hf:-hf]])
            dqkv_n = jnp.stack([_dswa_unpermute(dqkv_p[t]) for t in range(3)])
            dqkv = jnp.transpose(dqkv_n, (2, 0, 1, 3)).reshape(s, 3 * DSWA_WIDTH).astype(BF16)
            g_big["dswa_w_in"][j] = _mm(sv["h"], dqkv, ta=True, name=f"dswa_w_in_grad{i}", out_dtypes=(BF16,))
            dh = _mm(dqkv, w_in, tb=True, name=f"dswa_proj_bwd{i}")
        dcur, dcur_b, g_norm_mix[i] = _rms_bwd(sv["x_in"], norm_mix[i], dh, dmid, name=f"rms_mix_bwd{i}")

    rep = dict(norm_mix=jnp.concatenate(g_norm_mix, axis=0), norm_mlp=jnp.concatenate(g_norm_mlp, axis=0),
               norm_final=dg_final.reshape(-1), rel_bias=d_table_t.T,
               gdn_a_log=jnp.stack(g_alog).reshape(gdn_a_log.shape), gdn_dt_bias=jnp.stack(g_dt).reshape(gdn_dt_bias.shape),
               gdn_norm_w=jnp.stack(g_nw).reshape(gdn_norm_w.shape))
    return loss_part, dcur, g_big, rep, g_conv
```

```python
import functools
import math

import jax
import jax.numpy as jnp
import numpy as np
from jax import lax
from jax.experimental import pallas as pl
from jax.experimental.pallas import tpu as pltpu

F32 = jnp.float32
BF16 = jnp.bfloat16
HP = lax.Precision.HIGHEST

N_DEV = 8
D_MODEL = 1024
DEPTH = 4
RMS_EPS = 1e-6
NEG_INF = -1e30

GDN_HEADS = 8
GDN_DK = 128
GDN_CONV = 5
GDN_CHUNK = 64
GDN_QKV = 3 * GDN_HEADS * GDN_DK
GDN_MAIN = GDN_QKV + GDN_HEADS * GDN_DK
GDN_AB = 4 * GDN_HEADS

DSWA_DILS = (1, 4, 16)
DSWA_HG = 6
DSWA_E = 64
DSWA_HEADS = 18
DSWA_WIDTH = DSWA_HEADS * DSWA_E
DSWA_HALF = 64
DSWA_PG = DSWA_HG // 2
DSWA_UNROLL = 4
REL_BUCKETS = 32
REL_MAX_DIST = 1024

ADAM_LR = 0.001
ADAM_B1 = 0.9
ADAM_B2 = 0.999
ADAM_EPS = 1e-08
ADAM_WD = 0.01
ADAM_STEP = 10

VMEM_LIMIT = 56 * 1024 * 1024


def _cparams(sem=None, **kw):
    return pltpu.CompilerParams(dimension_semantics=sem, vmem_limit_bytes=VMEM_LIMIT, **kw)


def _pick(dim, cands):
    for c in cands:
        if dim % c == 0:
            return c
    return dim


def _bdot(a, b):
    return jnp.dot(a.astype(BF16), b.astype(BF16), preferred_element_type=F32)


def _bdot_nt(a, b):
    return lax.dot_general(a.astype(BF16), b.astype(BF16), (((1,), (1,)), ((), ())),
                           preferred_element_type=F32)


def _bdot_tn(a, b):
    return lax.dot_general(a.astype(BF16), b.astype(BF16), (((0,), (0,)), ((), ())),
                           preferred_element_type=F32)


def _hdot(a, b):
    return jnp.dot(a, b, precision=HP, preferred_element_type=F32)


def _hdot_tn(a, b):
    return lax.dot_general(a, b, (((0,), (0,)), ((), ())), precision=HP, preferred_element_type=F32)


def _hdot_nt(a, b):
    return lax.dot_general(a, b, (((1,), (1,)), ((), ())), precision=HP, preferred_element_type=F32)


def _sigmoid(x):
    return 1.0 / (1.0 + jnp.exp(-x))


def _mm(a, b, *, name, ta=False, tb=False, out_dtypes=(F32,), epilogue=None, extras=(),
        tm=None, tn=None, tk=None, shard=None):
    if ta:
        kdim, m = a.shape
    else:
        m, kdim = a.shape
    n = b.shape[0] if tb else b.shape[1]
    if shard == "rows":
        tm = m // N_DEV if (m // N_DEV) % 128 == 0 else m
    if shard == "cols":
        tn = n // N_DEV
    tm = tm or _pick(m, (1024, 1152, 512, 384, 256, 128))
    tn = tn or _pick(n, (1024, 1152, 512, 384, 256, 128))
    tk = tk or _pick(kdim, (1024, 1152, 512, 384, 256, 128))
    nk = kdim // tk
    n_out = len(out_dtypes)
    n_ex = len(extras)
    rows_all = shard == "rows" and tm == m

    def body(*refs):
        a_ref, b_ref = refs[0], refs[1]
        ex_refs = refs[2:2 + n_ex]
        out_refs = refs[2 + n_ex:2 + n_ex + n_out]
        acc_ref = refs[-1]
        k = pl.program_id(2)

        @pl.when(k == 0)
        def _():
            acc_ref[...] = jnp.zeros_like(acc_ref)

        av = a_ref[...].astype(BF16)
        bv = b_ref[...].astype(BF16)
        dims = (((0 if ta else 1,), (1 if tb else 0,)), ((), ()))
        acc_ref[...] += lax.dot_general(av, bv, dims, preferred_element_type=F32)

        @pl.when(k == nk - 1)
        def _():
            acc = acc_ref[...]
            outs = (acc,) if epilogue is None else epilogue(acc, *[r[...] for r in ex_refs])
            for r, o in zip(out_refs, outs):
                if rows_all:
                    for p in range(N_DEV):
                        r[p] = o[p * (m // N_DEV):(p + 1) * (m // N_DEV)].astype(r.dtype)
                else:
                    r[...] = o.astype(r.dtype)

    a_spec = pl.BlockSpec((tk, tm), lambda i, j, k: (k, i)) if ta else pl.BlockSpec((tm, tk), lambda i, j, k: (i, k))
    b_spec = pl.BlockSpec((tn, tk), lambda i, j, k: (j, k)) if tb else pl.BlockSpec((tk, tn), lambda i, j, k: (k, j))
    o_spec = pl.BlockSpec((tm, tn), lambda i, j, k: (i, j))
    out_specs = [o_spec] * n_out
    out_shape = [jax.ShapeDtypeStruct((m, n), dt) for dt in out_dtypes]
    if shard == "rows":
        out_shape = [jax.ShapeDtypeStruct((N_DEV, m // N_DEV, n), out_dtypes[0])]
        out_specs = [pl.BlockSpec((N_DEV, m // N_DEV, tn), lambda i, j, k: (0, 0, j)) if rows_all
                     else pl.BlockSpec((None, tm, tn), lambda i, j, k: (i, 0, j))]
    if shard == "cols":
        out_shape = [jax.ShapeDtypeStruct((N_DEV, m, tn), out_dtypes[0])]
        out_specs = [pl.BlockSpec((None, tm, tn), lambda i, j, k: (j, i, 0))]
    outs = pl.pallas_call(
        body, name=name,
        grid=(m // tm, n // tn, nk),
        in_specs=[a_spec, b_spec] + [o_spec] * n_ex,
        out_specs=out_specs,
        out_shape=out_shape,
        scratch_shapes=[pltpu.VMEM((tm, tn), F32)],
        compiler_params=_cparams(("parallel", "parallel", "arbitrary")),
    )(a, b, *extras)
    return outs[0] if n_out == 1 else outs


def _rms_fwd(x, g, *, name):
    s, d = x.shape
    tr = _pick(s, (512, 256, 128))

    def body(x_ref, g_ref, h_ref):
        xv = x_ref[...]
        r = lax.rsqrt(jnp.mean(xv * xv, axis=-1, keepdims=True) + RMS_EPS)
        h_ref[...] = (xv * r * g_ref[...]).astype(h_ref.dtype)

    return pl.pallas_call(
        body, name=name, grid=(s // tr,),
        in_specs=[pl.BlockSpec((tr, d), lambda i: (i, 0)), pl.BlockSpec((1, d), lambda i: (0, 0))],
        out_specs=pl.BlockSpec((tr, d), lambda i: (i, 0)),
        out_shape=jax.ShapeDtypeStruct((s, d), BF16),
        compiler_params=_cparams(("parallel",)),
    )(x, g.reshape(1, d))


def _rms_bwd(x, g, dh, dres, *, name):
    s, d = x.shape
    tr = _pick(s, (512, 256, 128))

    def body(x_ref, g_ref, dh_ref, dres_ref, dx_ref, dxb_ref, dg_ref):
        i = pl.program_id(0)
        xv = x_ref[...]
        r = lax.rsqrt(jnp.mean(xv * xv, axis=-1, keepdims=True) + RMS_EPS)
        xn = xv * r
        dhv = dh_ref[...]
        dn = dhv * g_ref[...]
        dx = dres_ref[...] + r * (dn - xn * jnp.mean(dn * xn, axis=-1, keepdims=True))
        dx_ref[...] = dx
        dxb_ref[...] = dx.astype(dxb_ref.dtype)
        part = jnp.sum(dhv * xn, axis=0, keepdims=True)

        @pl.when(i == 0)
        def _():
            dg_ref[...] = part

        @pl.when(i > 0)
        def _():
            dg_ref[...] += part

    row = pl.BlockSpec((tr, d), lambda i: (i, 0))
    vec = pl.BlockSpec((1, d), lambda i: (0, 0))
    return pl.pallas_call(
        body, name=name, grid=(s // tr,),
        in_specs=[row, vec, row, row], out_specs=[row, row, vec],
        out_shape=[jax.ShapeDtypeStruct((s, d), F32), jax.ShapeDtypeStruct((s, d), BF16),
                   jax.ShapeDtypeStruct((1, d), F32)],
        compiler_params=_cparams(("arbitrary",)),
    )(x, g.reshape(1, d), dh, dres)


def _loss_head(x, g, target, *, name):
    s, d = x.shape
    tr = _pick(s, (512, 256, 128))

    def body(x_ref, g_ref, t_ref, loss_ref, dx_ref, dxb_ref, dg_ref):
        i = pl.program_id(0)
        xv = x_ref[...]
        gv = g_ref[...]
        r = lax.rsqrt(jnp.mean(xv * xv, axis=-1, keepdims=True) + RMS_EPS)
        xn = xv * r
        err = xn * gv - t_ref[...]
        lpart = 0.5 * jnp.sum(jnp.mean(err * err, axis=-1, keepdims=True), axis=0, keepdims=True)
        dy = err * (1.0 / d)
        dn = dy * gv
        dx = r * (dn - xn * jnp.mean(dn * xn, axis=-1, keepdims=True))
        dx_ref[...] = dx
        dxb_ref[...] = dx.astype(dxb_ref.dtype)
        gpart = jnp.sum(dy * xn, axis=0, keepdims=True)

        @pl.when(i == 0)
        def _():
            dg_ref[...] = gpart
            loss_ref[...] = lpart

        @pl.when(i > 0)
        def _():
            dg_ref[...] += gpart
            loss_ref[...] += lpart

    row = pl.BlockSpec((tr, d), lambda i: (i, 0))
    vec = pl.BlockSpec((1, d), lambda i: (0, 0))
    one = pl.BlockSpec((1, 1), lambda i: (0, 0))
    return pl.pallas_call(
        body, name=name, grid=(s // tr,),
        in_specs=[row, vec, row], out_specs=[one, row, row, vec],
        out_shape=[jax.ShapeDtypeStruct((1, 1), F32), jax.ShapeDtypeStruct((s, d), F32),
                   jax.ShapeDtypeStruct((s, d), BF16), jax.ShapeDtypeStruct((1, d), F32)],
        compiler_params=_cparams(("arbitrary",)),
    )(x, g.reshape(1, d), target)


def _shift_rows(x, sft, rows):
    s = x.shape[0]
    if sft == 0:
        return x
    y = pltpu.roll(x, (-sft) % s, 0)
    ok = (rows + sft >= 0) & (rows + sft < s)
    return jnp.where(ok, y, 0.0)


def _gdn_pre_fwd(proj, conv_w, *, name):
    s = proj.shape[0]
    nblk = GDN_QKV // 128
    pad = GDN_CONV // 2

    def body(x_ref, w_ref, o_ref):
        j = pl.program_id(0)
        x = x_ref[...]
        rows = lax.broadcasted_iota(jnp.int32, x.shape, 0)
        c = jnp.zeros_like(x)
        for t in range(GDN_CONV):
            c = c + w_ref[pl.ds(t, 1), :] * _shift_rows(x, t - pad, rows)
        a = c * _sigmoid(c)
        rinv = lax.rsqrt(jnp.sum(a * a, axis=-1, keepdims=True) + 1e-6)
        scale = jnp.where(j < GDN_HEADS, GDN_DK ** -0.5, 1.0)
        o_ref[...] = jnp.where(j >= 2 * GDN_HEADS, a, a * (rinv * scale))

    return pl.pallas_call(
        body, name=name, grid=(nblk,),
        in_specs=[pl.BlockSpec((s, 128), lambda j: (0, j)), pl.BlockSpec((GDN_CONV, 128), lambda j: (0, j))],
        out_specs=pl.BlockSpec((s, 128), lambda j: (0, j)),
        out_shape=jax.ShapeDtypeStruct((s, GDN_QKV), F32),
        compiler_params=_cparams(("parallel",)),
    )(proj, conv_w)


def _gdn_pre_bwd(proj, conv_w, dqkv, *, name):
    s = proj.shape[0]
    nblk = GDN_QKV // 128
    pad = GDN_CONV // 2

    def body(x_ref, w_ref, df_ref, dbk_ref, dx_ref, dw_ref):
        j = pl.program_id(0)
        x = x_ref[...]
        rows = lax.broadcasted_iota(jnp.int32, x.shape, 0)
        xs = [_shift_rows(x, t - pad, rows) for t in range(GDN_CONV)]
        c = jnp.zeros_like(x)
        for t in range(GDN_CONV):
            c = c + w_ref[pl.ds(t, 1), :] * xs[t]
        sg = _sigmoid(c)
        a = c * sg
        rinv = lax.rsqrt(jnp.sum(a * a, axis=-1, keepdims=True) + 1e-6)
        scale = jnp.where(j < GDN_HEADS, GDN_DK ** -0.5, 1.0)
        dy = df_ref[...] + dbk_ref[...]
        nh = a * rinv
        da_n = (rinv * scale) * (dy - nh * jnp.sum(dy * nh, axis=-1, keepdims=True))
        da = jnp.where(j >= 2 * GDN_HEADS, dy, da_n)
        dc = da * (sg * (1.0 + c * (1.0 - sg)))
        dx = jnp.zeros_like(x)
        for t in range(GDN_CONV):
            dx = dx + w_ref[pl.ds(t, 1), :] * _shift_rows(dc, pad - t, rows)
            dw_ref[pl.ds(t, 1), :] = jnp.sum(dc * xs[t], axis=0, keepdims=True)
        dx_ref[...] = dx.astype(dx_ref.dtype)

    col = pl.BlockSpec((s, 128), lambda j: (0, j))
    wsp = pl.BlockSpec((GDN_CONV, 128), lambda j: (0, j))
    return pl.pallas_call(
        body, name=name, grid=(nblk,),
        in_specs=[col, wsp, col, col], out_specs=[col, wsp],
        out_shape=[jax.ShapeDtypeStruct((s, GDN_QKV), BF16), jax.ShapeDtypeStruct((GDN_CONV, GDN_QKV), F32)],
        compiler_params=_cparams(("parallel",)),
    )(proj, conv_w, dqkv[0], dqkv[1])


def _softplus(x):
    return jnp.maximum(x, 0.0) + jnp.log(1.0 + jnp.exp(-jnp.abs(x)))


def _gdn_gate_fwd(a, b, a_log, dt_bias, *, name):
    s = a.shape[0]
    nh = 2 * GDN_HEADS

    def body(a_ref, b_ref, al_ref, dt_ref, g_ref, be_ref):
        g_ref[...] = -jnp.exp(al_ref[...]) * _softplus(a_ref[...] + dt_ref[...])
        be_ref[...] = _sigmoid(b_ref[...])

    return pl.pallas_call(
        body, name=name,
        out_shape=[jax.ShapeDtypeStruct((s, nh), F32), jax.ShapeDtypeStruct((s, nh), F32)],
        compiler_params=_cparams(),
    )(a, b, a_log.reshape(1, nh), dt_bias.reshape(1, nh))


def _gdn_gate_bwd(a, b, a_log, dt_bias, dg, dbeta, *, name):
    s = a.shape[0]
    nh = 2 * GDN_HEADS

    def body(a_ref, b_ref, al_ref, dt_ref, dg_ref, db_ref, da_ref, dbb_ref, dal_ref, ddt_ref):
        ea = jnp.exp(al_ref[...])
        z = a_ref[...] + dt_ref[...]
        dgv = dg_ref[...]
        dz = dgv * (-ea) * _sigmoid(z)
        dal_ref[...] = jnp.sum(dgv * (-ea) * _softplus(z), axis=0, keepdims=True)
        ddt_ref[...] = jnp.sum(dz, axis=0, keepdims=True)
        sb = _sigmoid(b_ref[...])
        da_ref[...] = dz
        dbb_ref[...] = db_ref[...] * sb * (1.0 - sb)

    return pl.pallas_call(
        body, name=name,
        out_shape=[jax.ShapeDtypeStruct((s, nh), F32), jax.ShapeDtypeStruct((s, nh), F32),
                   jax.ShapeDtypeStruct((1, nh), F32), jax.ShapeDtypeStruct((1, nh), F32)],
        compiler_params=_cparams(),
    )(a, b, a_log.reshape(1, nh), dt_bias.reshape(1, nh), dg, dbeta)


def _chunk_masks(d):
    c = GDN_CHUNK
    ii = lax.broadcasted_iota(jnp.int32, (c, c), 0)
    jj = lax.broadcasted_iota(jnp.int32, (c, c), 1)
    dif = (ii - jj) * (1 - 2 * d)
    mi = dif >= 0
    mit = dif <= 0
    ms = dif > 0
    eye = ii == jj
    bds = [(ii >> sh) == (jj >> sh) for sh in (3, 4, 5)]
    return dict(mi=mi, mit=mit, ms=ms, eye=eye, bds=bds,
                mif=mi.astype(F32), mitf=mit.astype(F32), eyef=eye.astype(F32))


class _V:
    def __init__(self, xs):
        self.xs = tuple(xs)

    def __add__(self, o):
        return _lift(lambda a, b: a + b)(self, o)

    def __radd__(self, o):
        return _lift(lambda a, b: b + a)(self, o)

    def __sub__(self, o):
        return _lift(lambda a, b: a - b)(self, o)

    def __rsub__(self, o):
        return _lift(lambda a, b: b - a)(self, o)

    def __mul__(self, o):
        return _lift(lambda a, b: a * b)(self, o)

    def __rmul__(self, o):
        return _lift(lambda a, b: b * a)(self, o)

    def __and__(self, o):
        return _lift(lambda a, b: a & b)(self, o)

    def __neg__(self):
        return _lift(lambda a: -a)(self)

    def __rtruediv__(self, o):
        return _lift(lambda a, b: b / a)(self, o)


def _lift(f):
    def g(*args, **kw):
        n = next(len(a.xs) for a in args if isinstance(a, _V))
        return _V(f(*[a.xs[i] if isinstance(a, _V) else a for a in args], **kw) for i in range(n))
    return g


_vwhere, _vsum, _vexp, _vnot = _lift(jnp.where), _lift(jnp.sum), _lift(jnp.exp), _lift(jnp.logical_not)
_vhdot, _vhdot_tn = _lift(_bdot), _lift(_bdot_tn)
_vbdot, _vbdot_nt, _vbdot_tn = _lift(_bdot), _lift(_bdot_nt), _lift(_bdot_tn)
_vcat = _lift(lambda a, b: jnp.concatenate([a, b], axis=1))
_vlo = _lift(lambda a, n: a[:, :n])
_vhi = _lift(lambda a, n: a[:, n:])


def _both_masks(n):
    m = [_chunk_masks(d) for d in range(2)]
    mk = {key: _V([m[0][key]] * n + [m[1][key]] * n) for key in m[0] if key != "bds"}
    mk["bds"] = [_V([m[0]["bds"][i]] * n + [m[1]["bds"][i]] * n) for i in range(3)]
    return mk


def _tri_inv(a, mk):
    eyef = mk["eyef"]
    bd8, bd16, bd32 = mk["bds"]
    a8 = _vwhere(bd8, a, 0.0)
    a2 = _vhdot(a8, a8)
    a4 = _vhdot(a2, a2)
    t = _vhdot(_vhdot(eyef - a8, eyef + a2), eyef + a4)
    for inner, outer in ((bd8, bd16), (bd16, bd32), (bd32, None)):
        off = _vnot(inner) if outer is None else (outer & _vnot(inner))
        low = _vwhere(off, a, 0.0)
        t = t - _vhdot(_vhdot(t, low), t)
    return t


def _chunk_prep(q, k, v, g_row, b_row, mk):
    dv = GDN_DK
    g_col = _vsum(mk["eyef"] * g_row, axis=1, keepdims=True)
    b_col = _vsum(mk["eyef"] * b_row, axis=1, keepdims=True)
    gc_col = _vsum(mk["mif"] * g_row, axis=1, keepdims=True)
    gc_row = _vsum(mk["mitf"] * g_col, axis=0, keepdims=True)
    gl = _vsum(g_row, axis=1, keepdims=True)
    decay = _vwhere(mk["mi"], _vexp(_vwhere(mk["mi"], gc_col - gc_row, 0.0)), 0.0)
    eg = _vexp(gc_col)
    e2 = _vexp(gl - gc_col)
    egl = _vexp(gl)
    kb = k * b_col
    pm = _vbdot_nt(kb, k)
    a = _vwhere(mk["ms"], pm * decay, 0.0)
    t = _tri_inv(a, mk)
    sol = _vhdot(t, _vcat(v * b_col, kb * eg))
    u, w = _vlo(sol, dv), _vhi(sol, dv)
    qm = _vbdot_nt(q, k)
    return dict(b_col=b_col, decay=decay, eg=eg, e2=e2, egl=egl, kb=kb, pm=pm, t=t, u=u, w=w,
                qm=qm, intra=qm * decay, qd=q * eg, kd=k * e2)


def _chunk_fwd_step(p, state):
    v_new = p["u"] - _vbdot(p["w"], state)
    o = _vbdot(p["qd"], state) + _vbdot(p["intra"], v_new)
    new_state = state * p["egl"] + _vbdot_tn(p["kd"], v_new)
    return o, new_state


def _chunk_bwd_step(q, k, v, p, mk, state, dso, do):
    dv_dim = GDN_DK
    v_new = p["u"] - _vbdot(p["w"], state)
    dvn = _vbdot_tn(p["intra"], do) + _vbdot(p["kd"], dso)
    dintra = _vbdot_nt(do, v_new)
    dqd = _vbdot_nt(do, state)
    ds = p["egl"] * dso + _vbdot_tn(p["qd"], do) - _vbdot_tn(p["w"], dvn)
    dkd = _vbdot_nt(v_new, dso)
    dgl = _vsum(_vsum(dso * state, axis=1, keepdims=True), axis=0, keepdims=True) * p["egl"]
    dw = -_vbdot_nt(dvn, state)
    drhs = _vhdot_tn(p["t"], _vcat(dvn, dw))
    dru, drw = _vlo(drhs, dv_dim), _vhi(drhs, dv_dim)
    da = -_vwhere(mk["ms"], _vbdot_nt(drhs, _vcat(p["u"], p["w"])), 0.0)
    b_col = p["b_col"]
    dv = dru * b_col
    dbeta = _vsum(dru * v, axis=1, keepdims=True)
    dkb = drw * p["eg"]
    deg = _vsum(drw * p["kb"], axis=1, keepdims=True)
    dp = da * p["decay"]
    ddecay = da * p["pm"]
    dkb = dkb + _vbdot(dp, k)
    dk = _vbdot_tn(dp, p["kb"])
    dqm = dintra * p["decay"]
    ddecay = ddecay + dintra * p["qm"]
    dq = _vbdot(dqm, k)
    dk = dk + _vbdot_tn(dqm, q)
    dd = ddecay * p["decay"]
    dgc_col = _vsum(dd, axis=1, keepdims=True)
    dgc_row = -_vsum(dd, axis=0, keepdims=True)
    dq = dq + dqd * p["eg"]
    deg = deg + _vsum(dqd * q, axis=1, keepdims=True)
    dk = dk + dkd * p["e2"]
    de2 = _vsum(dkd * k, axis=1, keepdims=True) * p["e2"]
    dgl = dgl + _vsum(de2, axis=0, keepdims=True)
    dgc_col = dgc_col - de2 + deg * p["eg"]
    dk = dk + dkb * b_col
    dbeta = dbeta + _vsum(dkb * k, axis=1, keepdims=True)
    dgc_col = dgc_col + _vsum(mk["eyef"] * dgc_row, axis=1, keepdims=True)
    dg_row = _vsum(mk["mif"] * dgc_col, axis=0, keepdims=True) + dgl
    dbeta_row = _vsum(mk["eyef"] * dbeta, axis=0, keepdims=True)
    return dq, dk, dv, dg_row, dbeta_row, ds


def _gdn_chunk_fwd(qkvn, g5, b5, *, name, comm=None):
    s = qkvn.shape[0]
    c = GDN_CHUNK
    nc = s // c
    h_, dk = GDN_HEADS, GDN_DK

    def body(*refs):
        n = pl.program_id(0)
        ins, outs, (st_scr,) = _comm_hooks(comm, refs, 6, 4, 1, n == 0, n == nc // 2, n == nc - 1)
        x_refs, g_refs, b_refs = ins[0:2], ins[2:4], ins[4:6]
        o_refs, st_refs = outs[0:2], outs[2:4]

        @pl.when(n == 0)
        def _():
            st_scr[...] = jnp.zeros_like(st_scr)

        ch = [(d, h) for d in range(2) for h in range(h_)]
        mk = _both_masks(h_)
        q, k, v = (_V(x_refs[d][:, (t * h_ + h) * dk:(t * h_ + h + 1) * dk] for d, h in ch) for t in range(3))
        g, b = (_V(r[d][0, h, 0] for d, h in ch) for r in (g_refs, b_refs))
        state = _V(st_scr[d * h_ + h] for d, h in ch)
        o, new_state = _chunk_fwd_step(_chunk_prep(q, k, v, g, b, mk), state)
        for i, (d, h) in enumerate(ch):
            st_refs[d][h, 0] = state.xs[i]
            st_scr[d * h_ + h] = new_state.xs[i]
            o_refs[d][:, h * dk:(h + 1) * dk] = o.xs[i]

    ce = (lambda n: n, lambda n: nc - 1 - n)
    xs = [pl.BlockSpec((c, 3 * h_ * dk), lambda n, d=d: (ce[d](n), 0)) for d in range(2)]
    gates = [pl.BlockSpec((1, h_, 1, 1, c), lambda n, d=d: (d, 0, ce[d](n), 0, 0)) for d in range(2)]
    os_ = [pl.BlockSpec((c, h_ * dk), lambda n, d=d: (ce[d](n), 0)) for d in range(2)]
    sts = [pl.BlockSpec((h_, 1, dk, dk), lambda n, d=d: (0, ce[d](n), 0, 0)) for d in range(2)]
    c_in, c_out, c_shape, c_scr = _comm_specs(comm)
    res = pl.pallas_call(
        body, name=name, grid=(nc,),
        in_specs=xs + gates + gates + c_in,
        out_specs=os_ + sts + c_out,
        out_shape=[jax.ShapeDtypeStruct((s, h_ * dk), F32)] * 2 + [jax.ShapeDtypeStruct((h_, nc, dk, dk), F32)] * 2 + c_shape,
        scratch_shapes=[pltpu.VMEM((2 * h_, dk, dk), F32)] + c_scr,
        compiler_params=_cparams(("arbitrary",)),
    )(qkvn, qkvn, g5, g5, b5, b5, *(comm.arrays if comm else []))
    return res[0:2], res[2:4], res[4:]


def _gdn_chunk_bwd(qkvn, g5, b5, states, do, *, name, comm=None):
    s = qkvn.shape[0]
    c = GDN_CHUNK
    nc = s // c
    h_, dk = GDN_HEADS, GDN_DK

    def body(*refs):
        i = pl.program_id(0)
        ins, outs, (ds_scr,) = _comm_hooks(comm, refs, 10, 6, 1, i == 0, i == nc // 2, i == nc - 1)
        x_refs, g_refs, b_refs, st_refs, do_refs = ins[0:2], ins[2:4], ins[4:6], ins[6:8], ins[8:10]
        dx_refs, dg_refs, db_refs = outs[0:2], outs[2:4], outs[4:6]

        @pl.when(i == 0)
        def _():
            ds_scr[...] = jnp.zeros_like(ds_scr)

        ch = [(d, h) for d in range(2) for h in range(h_)]
        mk = _both_masks(h_)
        q, k, v = (_V(x_refs[d][:, (t * h_ + h) * dk:(t * h_ + h + 1) * dk] for d, h in ch) for t in range(3))
        g, b = (_V(r[d][0, h, 0] for d, h in ch) for r in (g_refs, b_refs))
        state = _V(st_refs[d][h, 0] for d, h in ch)
        dso = _V(ds_scr[d * h_ + h] for d, h in ch)
        dov = _V(do_refs[d][:, h * dk:(h + 1) * dk] for d, h in ch)
        res = _chunk_bwd_step(q, k, v, _chunk_prep(q, k, v, g, b, mk), mk, state, dso, dov)
        for (d, h), (dq, dkk, dvv, dg_r, db_r, ds) in zip(ch, zip(*[r.xs for r in res])):
            ds_scr[d * h_ + h] = ds
            dg_refs[d][h, 0] = dg_r
            db_refs[d][h, 0] = db_r
            for t, val in enumerate((dq, dkk, dvv)):
                dx_refs[d][:, (t * h_ + h) * dk:(t * h_ + h + 1) * dk] = val

    ce = (lambda i: nc - 1 - i, lambda i: i)
    both = lambda mk_spec: [mk_spec(d) for d in range(2)]
    xs = both(lambda d: pl.BlockSpec((c, 3 * h_ * dk), lambda i: (ce[d](i), 0)))
    gates = both(lambda d: pl.BlockSpec((1, h_, 1, 1, c), lambda i: (d, 0, ce[d](i), 0, 0)))
    sts = both(lambda d: pl.BlockSpec((h_, 1, dk, dk), lambda i: (0, ce[d](i), 0, 0)))
    dos = both(lambda d: pl.BlockSpec((c, h_ * dk), lambda i: (ce[d](i), 0)))
    gouts = both(lambda d: pl.BlockSpec((h_, 1, 1, c), lambda i: (0, ce[d](i), 0, 0)))
    c_in, c_out, c_shape, c_scr = _comm_specs(comm)
    res = pl.pallas_call(
        body, name=name, grid=(nc,),
        in_specs=xs + gates + gates + sts + dos + c_in,
        out_specs=xs + gouts + gouts + c_out,
        out_shape=[jax.ShapeDtypeStruct((s, 3 * h_ * dk), F32)] * 2
        + [jax.ShapeDtypeStruct((h_, nc, 1, c), F32)] * 4 + c_shape,
        scratch_shapes=[pltpu.VMEM((2 * h_, dk, dk), F32)] + c_scr,
        compiler_params=_cparams(("arbitrary",)),
    )(qkvn, qkvn, g5, g5, b5, b5, states[0], states[1], do, do, *(comm.arrays if comm else []))
    return res[0:2], jnp.stack(res[2:4]), jnp.stack(res[4:6]), res[6:]


def _gdn_post_fwd(o, z, norm_w, *, name):
    s = o[0].shape[0]
    h_, dk = GDN_HEADS, GDN_DK

    def body(of_ref, ob_ref, z_ref, w_ref, a_ref):
        ov = of_ref[...] + ob_ref[...]
        zv = z_ref[...]
        r = lax.rsqrt(jnp.mean(ov * ov, axis=-1, keepdims=True) + RMS_EPS)
        a_ref[...] = (ov * r * w_ref[...] * (zv * _sigmoid(zv))).astype(a_ref.dtype)

    col = pl.BlockSpec((s, dk), lambda h: (0, h))
    return pl.pallas_call(
        body, name=name, grid=(h_,),
        in_specs=[col, col, pl.BlockSpec((s, dk), lambda h: (0, 3 * h_ + h)), pl.BlockSpec((1, dk), lambda h: (0, 0))],
        out_specs=col,
        out_shape=jax.ShapeDtypeStruct((s, h_ * dk), BF16),
        compiler_params=_cparams(("parallel",)),
    )(o[0], o[1], z, norm_w.reshape(1, dk))


def _gdn_post_bwd(o, z, norm_w, dact, *, name):
    s = o[0].shape[0]
    h_, dk = GDN_HEADS, GDN_DK

    def body(of_ref, ob_ref, z_ref, w_ref, da_ref, do_ref, dz_ref, dw_ref):
        h = pl.program_id(0)
        ov = of_ref[...] + ob_ref[...]
        zv = z_ref[...]
        wv = w_ref[...]
        dav = da_ref[...]
        r = lax.rsqrt(jnp.mean(ov * ov, axis=-1, keepdims=True) + RMS_EPS)
        nrm = ov * r
        sg = _sigmoid(zv)
        sz = zv * sg
        dn = dav * wv * sz
        do_ref[...] = r * (dn - nrm * jnp.mean(dn * nrm, axis=-1, keepdims=True))
        dz_ref[...] = (dav * nrm * wv * (sg * (1.0 + zv * (1.0 - sg)))).astype(dz_ref.dtype)
        part = jnp.sum(dav * nrm * sz, axis=0, keepdims=True)

        @pl.when(h == 0)
        def _():
            dw_ref[...] = part

        @pl.when(h > 0)
        def _():
            dw_ref[...] += part

    col = pl.BlockSpec((s, dk), lambda h: (0, h))
    vec = pl.BlockSpec((1, dk), lambda h: (0, 0))
    return pl.pallas_call(
        body, name=name, grid=(h_,),
        in_specs=[col, col, pl.BlockSpec((s, dk), lambda h: (0, 3 * h_ + h)), vec, col],
        out_specs=[col, col, vec],
        out_shape=[jax.ShapeDtypeStruct((s, h_ * dk), F32), jax.ShapeDtypeStruct((s, h_ * dk), BF16),
                   jax.ShapeDtypeStruct((1, dk), F32)],
        compiler_params=_cparams(("arbitrary",)),
    )(o[0], o[1], z, norm_w.reshape(1, dk), dact)


def _rel_bucket(rel):
    nb = REL_BUCKETS // 2
    max_exact = nb // 2
    ret = jnp.where(rel > 0, nb, 0)
    n = jnp.abs(rel)
    nf = jnp.maximum(n, 1).astype(F32)
    large = max_exact + (jnp.log(nf / max_exact) / math.log(REL_MAX_DIST / max_exact)
                         * (nb - max_exact)).astype(jnp.int32)
    large = jnp.minimum(large, nb - 1)
    return ret + jnp.where(n < max_exact, n, large)


def _bucket_onehot():
    half = DSWA_HALF
    outs = []
    for dil in DSWA_DILS:
        rel = (jnp.arange(3 * half)[None, :] - half - jnp.arange(half)[:, None]) * dil
        outs.append(jax.nn.one_hot(_rel_bucket(rel).reshape(-1), REL_BUCKETS, dtype=F32, axis=0))
    return jnp.stack(outs)


def _head_group_select(vals):
    rows = lax.broadcasted_iota(jnp.int32, vals[0].shape, 0)
    return jnp.where(rows < DSWA_HG, vals[0], jnp.where(rows < 2 * DSWA_HG, vals[1], vals[2]))


def _dswa_bias(table_t, onehot, *, name):
    p = onehot.shape[-1]

    def body(t_ref, oh_ref, b_ref):
        b_ref[...] = _head_group_select([_hdot(t_ref[...], oh_ref[g]) for g in range(3)])

    return pl.pallas_call(body, name=name, out_shape=jax.ShapeDtypeStruct((DSWA_HEADS, p), F32),
                          compiler_params=_cparams())(table_t, onehot)


def _dswa_dtable(dbias, onehot, *, name):
    def body(d_ref, oh_ref, t_ref):
        t_ref[...] = _head_group_select([_hdot_nt(d_ref[...], oh_ref[g]) for g in range(3)])

    return pl.pallas_call(body, name=name, out_shape=jax.ShapeDtypeStruct((DSWA_HEADS, REL_BUCKETS), F32),
                          compiler_params=_cparams())(dbias, onehot)


def _rows(start, dil):
    if dil == 1:
        return pl.ds(pl.multiple_of(start, DSWA_HALF), DSWA_HALF)
    return pl.ds(start, DSWA_HALF, stride=dil)


def _attn_blocks(it, s, dil):
    half = DSWA_HALF
    nbs = s // half // dil
    ii = lax.broadcasted_iota(jnp.int32, (half, 3 * half), 0)
    jj = lax.broadcasted_iota(jnp.int32, (half, 3 * half), 1)
    band = jnp.abs(jj - half - ii) <= half
    out = []
    for u in range(DSWA_UNROLL):
        blk = it * DSWA_UNROLL + u
        r, b = blk // nbs, blk % nbs
        own = r + dil * half * b
        prev = own - jnp.where(b > 0, dil * half, 0)
        nxt = own + jnp.where(b < nbs - 1, dil * half, 0)
        ok = band & ((jj >= half) | (b > 0)) & ((jj < 2 * half) | (b < nbs - 1))
        out.append(((prev, own, nxt), ok))
    return out


def _attn_chains(q_ref, k_ref, v_ref, blocks, dil):
    lane = lax.broadcasted_iota(jnp.int32, (DSWA_HALF, 2 * DSWA_E), 1)
    qm, kw, vw, valid, hmask = [], [], [], [], []
    for (prev, own, nxt), ok in blocks:
        q = q_ref[_rows(own, dil), :].astype(BF16)
        k = jnp.concatenate([k_ref[_rows(st, dil), :] for st in (prev, own, nxt)], axis=0).astype(BF16)
        v = jnp.concatenate([v_ref[_rows(st, dil), :] for st in (prev, own, nxt)], axis=0).astype(BF16)
        for hd in range(2):
            mine = (lane < DSWA_E) if hd == 0 else (lane >= DSWA_E)
            qm.append(jnp.where(mine, q, jnp.zeros_like(q)))
            kw.append(k)
            vw.append(v)
            valid.append(ok)
            hmask.append(mine)
    return _V(qm), _V(kw), _V(vw), _V(valid), _V(hmask)


def _per_group(pr, fn):
    for gi, dil in enumerate(DSWA_DILS):
        pl.when(pr // DSWA_PG == gi)(functools.partial(fn, dil))


_vmax, _vlog = _lift(jnp.max), _lift(jnp.log)


def _dswa_attn_fwd(qkv, bias, *, name, comm=None):
    s = qkv.shape[0]
    half, e = DSWA_HALF, DSWA_E
    npair = DSWA_HEADS // 2

    def body(*refs):
        pr = pl.program_id(0)
        (q_ref, k_ref, v_ref, bias_ref), (o_ref, lse_ref), _ = _comm_hooks(
            comm, refs, 4, 2, 0, pr == 0, pr == npair // 2, pr == npair - 1)
        bias_v = _V([bias_ref[0], bias_ref[1]] * DSWA_UNROLL)

        def run(dil):
            def step(it, carry):
                blocks = _attn_blocks(it, s, dil)
                qm, kw, vw, valid, hmask = _attn_chains(q_ref, k_ref, v_ref, blocks, dil)
                sc = _vwhere(valid, _vbdot_nt(qm, kw) * (e ** -0.5) + bias_v, NEG_INF)
                m = _vmax(sc, axis=-1, keepdims=True)
                p = _vexp(sc - m)
                l = _vsum(p, axis=-1, keepdims=True)
                o = _vbdot(p * (1.0 / l), vw)
                lse = m + _vlog(l)
                for u, ((_, own, _), _) in enumerate(blocks):
                    is_a = hmask.xs[2 * u]
                    o_ref[_rows(own, dil), :] = jnp.where(is_a, o.xs[2 * u], o.xs[2 * u + 1])
                    lse_ref[_rows(own, dil), :] = jnp.where(is_a, lse.xs[2 * u], lse.xs[2 * u + 1])
                return carry

            lax.fori_loop(0, s // half // DSWA_UNROLL, step, 0)

        _per_group(pr, run)

    col = lambda t: pl.BlockSpec((s, 2 * e), lambda p: (0, t * npair + p))
    pair = pl.BlockSpec((s, 2 * e), lambda p: (0, p))
    c_in, c_out, c_shape, c_scr = _comm_specs(comm)
    res = pl.pallas_call(
        body, name=name, grid=(npair,),
        in_specs=[col(0), col(1), col(2), pl.BlockSpec((2, half, 3 * half), lambda p: (p, 0, 0))] + c_in,
        out_specs=[pair, pair] + c_out,
        out_shape=[jax.ShapeDtypeStruct((s, npair * 2 * e), F32)] * 2 + c_shape,
        scratch_shapes=c_scr,
        compiler_params=_cparams(("arbitrary",)),
    )(qkv, qkv, qkv, bias, *(comm.arrays if comm else []))
    return res[0], res[1], res[2:]


def _dswa_attn_bwd(qkv, bias, lse, do, corr, *, name, comm=None):
    s = qkv.shape[0]
    half, e = DSWA_HALF, DSWA_E
    npair = DSWA_HEADS // 2
    w = 2 * e

    def body(*refs):
        pr = pl.program_id(0)
        (q_ref, k_ref, v_ref, bias_ref, lse_ref, do_ref, corr_ref), (dq_ref, dk_ref, dv_ref, db_ref), _ = _comm_hooks(
            comm, refs, 7, 4, 0, pr == 0, pr == npair // 2, pr == npair - 1)
        bias_v = _V([bias_ref[0], bias_ref[1]] * DSWA_UNROLL)
        dk_ref[...] = jnp.zeros_like(dk_ref)
        dv_ref[...] = jnp.zeros_like(dv_ref)

        def run(dil):
            def step(it, dbias):
                blocks = _attn_blocks(it, s, dil)
                qm, kw, vw, valid, hmask = _attn_chains(q_ref, k_ref, v_ref, blocks, dil)
                hd = [0, 1] * DSWA_UNROLL
                rows = [_rows(own, dil) for (_, own, _), _ in blocks for _ in range(2)]
                lse_c = _V(lse_ref[rw, :][:, h * e:h * e + 1] for rw, h in zip(rows, hd))
                corr_c = _V(corr_ref[rw, :][:, h * e:h * e + 1] for rw, h in zip(rows, hd))
                dov = _vwhere(hmask, _V(do_ref[rw, :] for rw in rows), 0.0)
                sc = _vbdot_nt(qm, kw) * (e ** -0.5) + bias_v
                p = _vwhere(valid, _vexp(_vwhere(valid, sc, 0.0) - lse_c), 0.0)
                dsc = p * (_vbdot_nt(dov, vw) + corr_c)
                dq = _vbdot(dsc, kw) * (e ** -0.5)
                dkc = _vbdot_tn(dsc, qm) * (e ** -0.5)
                dvc = _vbdot_tn(p, dov)
                for u, (starts, _) in enumerate(blocks):
                    dq_ref[_rows(starts[1], dil), :] = jnp.where(hmask.xs[2 * u], dq.xs[2 * u], dq.xs[2 * u + 1])
                    dk_u = dkc.xs[2 * u] + dkc.xs[2 * u + 1]
                    dv_u = dvc.xs[2 * u] + dvc.xs[2 * u + 1]
                    for t, st in enumerate(starts):
                        dk_ref[_rows(st, dil), :] += dk_u[t * half:(t + 1) * half]
                        dv_ref[_rows(st, dil), :] += dv_u[t * half:(t + 1) * half]
                da, db = dbias
                for u in range(DSWA_UNROLL):
                    da, db = da + dsc.xs[2 * u], db + dsc.xs[2 * u + 1]
                return da, db

            zero = jnp.zeros((half, 3 * half), F32)
            da, db = lax.fori_loop(0, s // half // DSWA_UNROLL, step, (zero, zero))
            db_ref[0] = da
            db_ref[1] = db

        _per_group(pr, run)

    col = lambda t: pl.BlockSpec((s, w), lambda p: (0, t * npair + p))
    ps = pl.BlockSpec((s, w), lambda p: (0, p))
    bs = pl.BlockSpec((2, half, 3 * half), lambda p: (p, 0, 0))
    c_in, c_out, c_shape, c_scr = _comm_specs(comm)
    res = pl.pallas_call(
        body, name=name, grid=(npair,),
        in_specs=[col(0), col(1), col(2), bs, ps, ps, ps] + c_in,
        out_specs=[ps, ps, ps, bs] + c_out,
        out_shape=[jax.ShapeDtypeStruct((s, npair * w), F32)] * 3
        + [jax.ShapeDtypeStruct((DSWA_HEADS, half, 3 * half), F32)] + c_shape,
        scratch_shapes=c_scr,
        compiler_params=_cparams(("arbitrary",)),
    )(qkv, qkv, qkv, bias, lse, do, corr, *(comm.arrays if comm else []))
    return res[0], res[1], res[2], res[3], res[4:]


def _pair_cols(g, j):
    w = 2 * DSWA_E
    return slice((g * DSWA_PG + j) * w, (g * DSWA_PG + j + 1) * w)


def _group_weights(l_ref, j):
    ls = [l_ref[:, _pair_cols(g, j)] for g in range(3)]
    m = jnp.maximum(jnp.maximum(ls[0], ls[1]), ls[2])
    es = [jnp.exp(x - m) for x in ls]
    inv = 1.0 / (es[0] + es[1] + es[2])
    return [x * inv for x in es]


def _dswa_combine_fwd(o, lse, *, name):
    s, wd = o.shape
    tr = _pick(s, (512, 256, 128))

    def body(o_ref, l_ref, c_ref):
        for j in range(DSWA_PG):
            al = _group_weights(l_ref, j)
            for g in range(3):
                c_ref[:, _pair_cols(g, j)] = (o_ref[:, _pair_cols(g, j)] * al[g]).astype(c_ref.dtype)

    row = pl.BlockSpec((tr, wd), lambda i: (i, 0))
    return pl.pallas_call(
        body, name=name, grid=(s // tr,),
        in_specs=[row, row], out_specs=row,
        out_shape=jax.ShapeDtypeStruct(o.shape, BF16),
        compiler_params=_cparams(("parallel",)),
    )(o, lse)


def _dswa_combine_bwd(o, lse, dc, *, name):
    s, wd = o.shape
    tr = _pick(s, (512, 256, 128))

    def body(o_ref, l_ref, dc_ref, do_ref, corr_ref):
        lane = lax.broadcasted_iota(jnp.int32, (tr, 2 * DSWA_E), 1)
        is_a = lane < DSWA_E
        for j in range(DSWA_PG):
            al = _group_weights(l_ref, j)
            tot = jnp.zeros((tr, 2 * DSWA_E), F32)
            for g in range(3):
                cols = _pair_cols(g, j)
                dcv = dc_ref[:, cols]
                do_ref[:, cols] = dcv * al[g]
                prod = dcv * o_ref[:, cols]
                dal = jnp.where(is_a, jnp.sum(jnp.where(is_a, prod, 0.0), axis=-1, keepdims=True),
                                jnp.sum(jnp.where(is_a, 0.0, prod), axis=-1, keepdims=True))
                tot = tot + al[g] * dal
            for g in range(3):
                corr_ref[:, _pair_cols(g, j)] = -al[g] * tot

    row = pl.BlockSpec((tr, wd), lambda i: (i, 0))
    return pl.pallas_call(
        body, name=name, grid=(s // tr,),
        in_specs=[row, row, row], out_specs=[row, row],
        out_shape=[jax.ShapeDtypeStruct(o.shape, F32)] * 2,
        compiler_params=_cparams(("parallel",)),
    )(o, lse, dc)


class _Comm:
    def __init__(self, mode, arrays, kinds=None):
        self.mode, self.arrays, self.kinds = mode, list(arrays), kinds
        self.n = len(self.arrays)

    def out_shapes(self):
        if self.mode == "exchange":
            return [jax.ShapeDtypeStruct(x.shape, x.dtype) for x in self.arrays]
        shapes = []
        for x, kd in zip(self.arrays, self.kinds):
            shp = list(x.shape)
            if kd == "stack":
                shp = [N_DEV] + shp
            else:
                shp[-2 if kd == "rows" else -1] *= N_DEV
            shapes.append(jax.ShapeDtypeStruct(tuple(shp), x.dtype))
        return shapes

    def scratch(self):
        return [pltpu.SemaphoreType.DMA((7 * self.n,)), pltpu.SemaphoreType.DMA((7 * self.n,)),
                pltpu.SemaphoreType.DMA((self.n,))]

    def bind(self, in_refs, out_refs, sems):
        self.x, self.o = in_refs, out_refs
        self.send_sems, self.recv_sems, self.local_sems = sems
        self.pos = (lax.axis_index("x"), lax.axis_index("y"), lax.axis_index("c"))

    def _slot(self, i, px, py, pc):
        p = 4 * px + 2 * py + pc
        kd = self.kinds[i]
        if kd == "stack":
            return self.o[i].at[p]
        nd = len(self.x[i].shape)
        ax = nd - 2 if kd == "rows" else nd - 1
        size = self.x[i].shape[ax]
        idx = tuple(pl.ds(p * size, size) if a == ax else slice(None) for a in range(nd))
        return self.o[i].at[idx]

    def _gcopy(self, i, k, block, to, src=None):
        return pltpu.make_async_remote_copy(
            src_ref=self._slot(i, *block) if src is None else src, dst_ref=self._slot(i, *block),
            send_sem=self.send_sems.at[7 * i + k], recv_sem=self.recv_sems.at[7 * i + k],
            device_id=to, device_id_type=pl.DeviceIdType.MESH)

    def _chips(self):
        mx, my, _ = self.pos
        return [(1 - mx, my), (mx, 1 - my), (1 - mx, 1 - my)]

    def _xcopies(self):
        mx, my, mc = self.pos
        me = 4 * mx + 2 * my + mc
        copies = []
        for k in range(1, N_DEV):
            px = 1 - mx if (k >> 2) & 1 else mx
            py = 1 - my if (k >> 1) & 1 else my
            pc = 1 - mc if k & 1 else mc
            for i in range(self.n):
                copies.append(pltpu.make_async_remote_copy(
                    src_ref=self.x[i].at[4 * px + 2 * py + pc], dst_ref=self.o[i].at[me],
                    send_sem=self.send_sems.at[7 * i + k - 1], recv_sem=self.recv_sems.at[7 * i + k - 1],
                    device_id=(px, py, pc), device_id_type=pl.DeviceIdType.MESH))
        return copies

    def _local(self):
        mx, my, mc = self.pos
        if self.mode == "exchange":
            me = 4 * mx + 2 * my + mc
            return [pltpu.make_async_copy(self.x[i].at[me], self.o[i].at[me], self.local_sems.at[i]) for i in range(self.n)]
        return [pltpu.make_async_copy(self.x[i], self._slot(i, mx, my, mc), self.local_sems.at[i]) for i in range(self.n)]

    def _first(self):
        mx, my, mc = self.pos
        me, sibling = (mx, my, mc), (mx, my, 1 - mc)
        first = [self._gcopy(i, 0, me, sibling, src=self.x[i]) for i in range(self.n)]
        first += [self._gcopy(i, 1 + j, me, (*chip, mc), src=self.x[i]) for j, chip in enumerate(self._chips())
                  for i in range(self.n)]
        return first

    def _passed(self):
        mx, my, mc = self.pos
        return [self._gcopy(i, 4 + j, (*chip, mc), (mx, my, 1 - mc)) for j, chip in enumerate(self._chips())
                for i in range(self.n)]

    def start(self):
        for cp in self._local() + (self._xcopies() if self.mode == "exchange" else self._first()):
            cp.start()

    def mid(self):
        if self.mode == "exchange":
            return
        mx, my, mc = self.pos
        passed = self._passed()
        for j, chip in enumerate(self._chips()):
            for i in range(self.n):
                self._gcopy(i, 1 + j, (*chip, mc), (mx, my, mc)).wait_recv()
                passed[j * self.n + i].start()

    def end(self):
        mx, my, mc = self.pos
        if self.mode == "exchange":
            copies = self._xcopies()
            for cp in copies:
                cp.wait_recv()
            for cp in copies:
                cp.wait_send()
        else:
            for i in range(self.n):
                self._gcopy(i, 0, (mx, my, 1 - mc), (mx, my, mc)).wait_recv()
                for j, chip in enumerate(self._chips()):
                    self._gcopy(i, 4 + j, (*chip, 1 - mc), (mx, my, mc)).wait_recv()
            for cp in self._first() + self._passed():
                cp.wait_send()
        for cp in self._local():
            cp.wait()

    def run(self, *, name):
        n = self.n

        def body(*refs):
            self.bind(refs[:n], refs[n:2 * n], refs[2 * n:])
            self.start()
            self.mid()
            self.end()

        anyspec = pl.BlockSpec(memory_space=pl.ANY)
        return pl.pallas_call(body, name=name, in_specs=[anyspec] * n, out_specs=[anyspec] * n,
                              out_shape=self.out_shapes(), scratch_shapes=self.scratch())(*self.arrays)


def _comm_specs(comm):
    if comm is None:
        return [], [], [], []
    anyspec = pl.BlockSpec(memory_space=pl.ANY)
    return [anyspec] * comm.n, [anyspec] * comm.n, comm.out_shapes(), comm.scratch()


def _comm_hooks(comm, refs, n_in, n_out, n_scr, first, mid, last):
    if comm is None:
        return refs[:n_in], refs[n_in:n_in + n_out], refs[n_in + n_out:]
    c = comm.n
    ins, cin = refs[:n_in], refs[n_in:n_in + c]
    outs, cout = refs[n_in + c:n_in + c + n_out], refs[n_in + c + n_out:n_in + 2 * c + n_out]
    scr, sems = refs[n_in + 2 * c + n_out:n_in + 2 * c + n_out + n_scr], refs[n_in + 2 * c + n_out + n_scr:]
    comm.bind(cin, cout, sems)
    pl.when(first)(comm.start)
    pl.when(mid)(comm.mid)
    pl.when(last)(comm.end)
    return ins, outs, scr


def _adamw_update(g, w, m, v):
    mn = ADAM_B1 * m + (1.0 - ADAM_B1) * g
    vn = ADAM_B2 * v + (1.0 - ADAM_B2) * (g * g)
    m_hat = mn / (1.0 - ADAM_B1 ** ADAM_STEP)
    v_hat = vn / (1.0 - ADAM_B2 ** ADAM_STEP)
    return -ADAM_LR * (m_hat / (jnp.sqrt(v_hat) + ADAM_EPS) + ADAM_WD * w), mn, vn


def _adamw_layers(recvs, w, m, v, *, name):
    nl, ks, ns = w.shape
    tr = _pick(ks, (64, 48))

    def body(*refs):
        rv_refs = refs[:nl]
        w_ref, m_ref, v_ref, g_ref, d_ref, nm_ref, nv_ref = refs[nl:]
        for l in range(nl):
            g = rv_refs[l][0].astype(F32)
            for q in range(1, N_DEV):
                g = g + rv_refs[l][q].astype(F32)
            delta, mn, vn = _adamw_update(g, w_ref[l], m_ref[l], v_ref[l])
            g_ref[l] = g
            d_ref[l] = delta
            nm_ref[l] = mn
            nv_ref[l] = vn

    row = pl.BlockSpec((nl, tr, ns), lambda i: (0, i, 0))
    return pl.pallas_call(
        body, name=name, grid=(ks // tr,),
        in_specs=[pl.BlockSpec((N_DEV, tr, ns), lambda i: (0, i, 0))] * nl + [row] * 3,
        out_specs=[row] * 4,
        out_shape=[jax.ShapeDtypeStruct((nl, ks, ns), F32)] * 4,
        compiler_params=_cparams(("parallel",)),
    )(*recvs, w, m, v)


def _adamw_reduce(recv, w, m, v, *, name):
    r, c = w.shape
    tr = _pick(r, (128, 64, 8))

    def body(rv_ref, w_ref, m_ref, v_ref, g_ref, d_ref, nm_ref, nv_ref):
        g = rv_ref[0]
        for q in range(1, N_DEV):
            g = g + rv_ref[q]
        delta, mn, vn = _adamw_update(g, w_ref[...], m_ref[...], v_ref[...])
        g_ref[...] = g
        d_ref[...] = delta
        nm_ref[...] = mn
        nv_ref[...] = vn

    row = pl.BlockSpec((tr, c), lambda i: (i, 0))
    return pl.pallas_call(
        body, name=name, grid=(r // tr,),
        in_specs=[pl.BlockSpec((N_DEV, tr, c), lambda i: (0, i, 0)), row, row, row],
        out_specs=[row] * 4,
        out_shape=[jax.ShapeDtypeStruct((r, c), F32)] * 4,
        compiler_params=_cparams(("parallel",)),
    )(recv, w, m, v)


_BIG = ("gdn_w_in", "gdn_w_out", "dswa_w_in", "dswa_w_out", "mlp_w1", "mlp_w2")
_SMALL = ("gdn_conv_w", "norm_mix", "norm_mlp", "norm_final", "rel_bias", "gdn_a_log", "gdn_dt_bias", "gdn_norm_w")
_ORDER = ("norm_mix", "norm_mlp", "norm_final", "rel_bias", "gdn_w_in", "gdn_conv_w", "gdn_a_log", "gdn_dt_bias",
          "gdn_norm_w", "gdn_w_out", "dswa_w_in", "dswa_w_out", "mlp_w1", "mlp_w2")
_KIND = dict(gdn_w_in="stack", gdn_w_out="rows", dswa_w_in="stack", dswa_w_out="rows", mlp_w1="cols", mlp_w2="rows")


def _pack_rows(arrs, align):
    rows, counts = [], []
    for a in arrs:
        flat = a.reshape(-1)
        n = -(-flat.shape[0] // D_MODEL)
        flat = jnp.pad(flat, (0, n * D_MODEL - flat.shape[0]))
        rows.append(flat.reshape(n, D_MODEL))
        counts.append(n)
    out = jnp.concatenate(rows, axis=0)
    total = -(-out.shape[0] // align) * align
    return jnp.pad(out, ((0, total - out.shape[0]), (0, 0))), counts


def _unpack_rows(slab, shapes):
    outs, r = [], 0
    for shp in shapes:
        size = int(np.prod(shp))
        n = -(-size // D_MODEL)
        outs.append(slab[r:r + n].reshape(-1)[:size].reshape(shp))
        r += n
    return outs


def _col_shards(full, nshard):
    lead = full.shape[:-1]
    n = full.shape[-1] // nshard
    t = full.reshape(lead + (nshard, n))
    return jnp.moveaxis(t, -2, 0)


def _from_col_shards(g):
    t = jnp.moveaxis(g, 0, -2)
    return t.reshape(t.shape[:-2] + (t.shape[-2] * t.shape[-1],))


def kernel(x, norm_mix, norm_mlp, norm_final, rel_bias, gdn_w_in, gdn_conv_w, gdn_a_log, gdn_dt_bias, gdn_norm_w, gdn_w_out, dswa_w_in, dswa_w_out, mlp_w1, mlp_w2, loss_target, m_norm_mix, m_norm_mlp, m_norm_final, m_rel_bias, m_gdn_w_in, m_gdn_conv_w, m_gdn_a_log, m_gdn_dt_bias, m_gdn_norm_w, m_gdn_w_out, m_dswa_w_in, m_dswa_w_out, m_mlp_w1, m_mlp_w2, v_norm_mix, v_norm_mlp, v_norm_final, v_rel_bias, v_gdn_w_in, v_gdn_conv_w, v_gdn_a_log, v_gdn_dt_bias, v_gdn_norm_w, v_gdn_w_out, v_dswa_w_in, v_dswa_w_out, v_mlp_w1, v_mlp_w2):
    params = dict(norm_mix=norm_mix, norm_mlp=norm_mlp, norm_final=norm_final, rel_bias=rel_bias,
                  gdn_w_in=gdn_w_in, gdn_conv_w=gdn_conv_w, gdn_a_log=gdn_a_log, gdn_dt_bias=gdn_dt_bias,
                  gdn_norm_w=gdn_norm_w, gdn_w_out=gdn_w_out, dswa_w_in=dswa_w_in, dswa_w_out=dswa_w_out,
                  mlp_w1=mlp_w1, mlp_w2=mlp_w2)
    mom_m = dict(norm_mix=m_norm_mix, norm_mlp=m_norm_mlp, norm_final=m_norm_final, rel_bias=m_rel_bias,
                 gdn_w_in=m_gdn_w_in, gdn_conv_w=m_gdn_conv_w, gdn_a_log=m_gdn_a_log, gdn_dt_bias=m_gdn_dt_bias,
                 gdn_norm_w=m_gdn_norm_w, gdn_w_out=m_gdn_w_out, dswa_w_in=m_dswa_w_in, dswa_w_out=m_dswa_w_out,
                 mlp_w1=m_mlp_w1, mlp_w2=m_mlp_w2)
    mom_v = dict(norm_mix=v_norm_mix, norm_mlp=v_norm_mlp, norm_final=v_norm_final, rel_bias=v_rel_bias,
                 gdn_w_in=v_gdn_w_in, gdn_conv_w=v_gdn_conv_w, gdn_a_log=v_gdn_a_log, gdn_dt_bias=v_gdn_dt_bias,
                 gdn_norm_w=v_gdn_norm_w, gdn_w_out=v_gdn_w_out, dswa_w_in=v_dswa_w_in, dswa_w_out=v_dswa_w_out,
                 mlp_w1=v_mlp_w1, mlp_w2=v_mlp_w2)
    xs = x[0]
    target = loss_target[0]
    dist = _Dist(params)
    conv_tail, _ = _pack_rows([gdn_conv_w], 8)
    (conv_g,) = dist.put("start", dist.gather_comm("start", extra=[(conv_tail, "stack")]).run(name="ag_start"))
    conv_parts = [_unpack_rows(conv_g[dev], [gdn_conv_w.shape])[0] for dev in range(N_DEV)]
    conv_full = _from_col_shards(jnp.stack(conv_parts))[:, :, 0, :]

    loss_part, dcur, g_big, rep, g_conv = _local_step(
        xs, target, dict(norm_mix=norm_mix, norm_mlp=norm_mlp, norm_final=norm_final, rel_bias=rel_bias,
                         gdn_a_log=gdn_a_log, gdn_dt_bias=gdn_dt_bias, gdn_norm_w=gdn_norm_w), dist.full, conv_full, dist)
    loss = lax.psum(loss_part[0, 0], ("x", "y", "c"))
    grad_x = dcur[None]

    conv_dev = _col_shards(jnp.stack(g_conv)[:, :, None, :], N_DEV)
    small_send = jnp.stack([_pack_rows([conv_dev[dev]] + [rep[n] for n in _SMALL[1:]], 8)[0] for dev in range(N_DEV)])
    (small_recv,) = dist.got("end", dist.send_comm("end", g_big, extra=[small_send]).run(name="grad_exchange"))

    outs = {}
    for n in _BIG:
        recvs = [dist.recv[(n, l)] for l in range(params[n].shape[0])]
        res = _adamw_layers(recvs, params[n], mom_m[n], mom_v[n], name=f"adamw_{n}")
        for tag, t in zip(("grad", "delta", "new_m", "new_v"), res):
            outs[(tag, n)] = t
    w_slab, _ = _pack_rows([params[n] for n in _SMALL], 8)
    m_slab, _ = _pack_rows([mom_m[n] for n in _SMALL], 8)
    v_slab, _ = _pack_rows([mom_v[n] for n in _SMALL], 8)
    small = _adamw_reduce(small_recv, w_slab, m_slab, v_slab, name="adamw_small")
    shapes = [params[n].shape for n in _SMALL]
    for tag, slab in zip(("grad", "delta", "new_m", "new_v"), small):
        for n, t in zip(_SMALL, _unpack_rows(slab, shapes)):
            outs[(tag, n)] = t
    result = [loss, grad_x]
    for tag in ("grad", "delta", "new_m", "new_v"):
        result += [outs[(tag, n)] for n in _ORDER]
    return tuple(result)


_GATHER = {
    "start": (("gdn_w_in", 0), ("gdn_w_out", 0), ("mlp_w1", 0), ("mlp_w2", 0)),
    "chunk_fwd0": (("dswa_w_in", 0), ("dswa_w_out", 0), ("mlp_w1", 1), ("mlp_w2", 1), ("gdn_w_in", 1), ("gdn_w_out", 1)),
    "attn_fwd1": (("mlp_w1", 2), ("mlp_w2", 2)),
    "chunk_fwd2": (("dswa_w_in", 1), ("dswa_w_out", 1), ("mlp_w1", 3), ("mlp_w2", 3)),
}
_SEND = {
    "attn_bwd3": (("mlp_w1", 3), ("mlp_w2", 3)),
    "chunk_bwd2": (("dswa_w_in", 1), ("dswa_w_out", 1), ("mlp_w1", 2), ("mlp_w2", 2)),
    "attn_bwd1": (("gdn_w_in", 1), ("gdn_w_out", 1)),
    "chunk_bwd0": (("mlp_w1", 1), ("mlp_w2", 1), ("dswa_w_in", 0), ("dswa_w_out", 0), ("mlp_w1", 0), ("mlp_w2", 0)),
    "end": (("gdn_w_in", 0), ("gdn_w_out", 0)),
}


class _Dist:
    def __init__(self, params):
        self.shards = {n: params[n].astype(BF16) for n in _BIG}
        self.full = {n: [None] * params[n].shape[0] for n in _BIG}
        self.recv = {}

    def gather_comm(self, tag, extra=()):
        if tag not in _GATHER:
            return None
        arrays = [self.shards[n][l] for n, l in _GATHER[tag]] + [a for a, _ in extra]
        return _Comm("gather", arrays, [_KIND[n] for n, _ in _GATHER[tag]] + [k for _, k in extra])

    def put(self, tag, outs):
        for (n, l), t in zip(_GATHER.get(tag, ()), outs):
            self.full[n][l] = _from_col_shards(t) if _KIND[n] == "stack" else t
        return outs[len(_GATHER.get(tag, ())):]

    def send_comm(self, tag, g_big, extra=()):
        if tag not in _SEND:
            return None
        arrays = [_col_shards(g_big[n][l], N_DEV) if _KIND[n] == "stack" else g_big[n][l] for n, l in _SEND[tag]]
        return _Comm("exchange", arrays + list(extra))

    def got(self, tag, outs):
        for item, t in zip(_SEND.get(tag, ()), outs):
            self.recv[item] = t
        return outs[len(_SEND.get(tag, ())):]


def _local_step(xs, target, sp, full, conv_full, dist=None):
    s = xs.shape[0]
    norm_mix, norm_mlp, norm_final = sp["norm_mix"], sp["norm_mlp"], sp["norm_final"]
    gdn_a_log, gdn_dt_bias, gdn_norm_w = sp["gdn_a_log"], sp["gdn_dt_bias"], sp["gdn_norm_w"]
    onehot = _bucket_onehot()
    table_t = sp["rel_bias"].T
    bias = _dswa_bias(table_t, onehot, name="dswa_bias").reshape(DSWA_HEADS, DSWA_HALF, 3 * DSWA_HALF)

    saved = []
    cur = xs
    for i in range(DEPTH):
        j = i // 2
        sv = dict(x_in=cur)
        h = _rms_fwd(cur, norm_mix[i], name=f"rms_mix_fwd{i}")
        sv["h"] = h
        if i % 2 == 0:
            w_in = full["gdn_w_in"][j]
            proj = _mm(h, w_in[:, :GDN_MAIN], name=f"gdn_proj{i}")
            ab = _mm(h, w_in[:, GDN_MAIN:], name=f"gdn_proj_ab{i}")
            qkvn = _gdn_pre_fwd(proj, conv_full[j], name=f"gdn_pre_fwd{i}")
            g_all, beta_all = _gdn_gate_fwd(ab[:, :2 * GDN_HEADS], ab[:, 2 * GDN_HEADS:], gdn_a_log[j], gdn_dt_bias[j],
                                            name=f"gdn_gate_fwd{i}")
            gshape = (2, GDN_HEADS, s // GDN_CHUNK, 1, GDN_CHUNK)
            g_row = g_all.T.reshape(gshape)
            b_row = beta_all.T.reshape(gshape)
            o, states, got = _gdn_chunk_fwd(qkvn, g_row, b_row, name=f"gdn_chunk_fwd{i}",
                                            comm=dist and dist.gather_comm(f"chunk_fwd{i}"))
            if dist:
                dist.put(f"chunk_fwd{i}", got)
            act = _gdn_post_fwd(o, proj, gdn_norm_w[j], name=f"gdn_post_fwd{i}")
            sv.update(proj=proj, ab=ab, qkvn=qkvn, g_row=g_row, b_row=b_row, o=o, states=states, act=act)
            w_out = full["gdn_w_out"][j]
        else:
            w_in = full["dswa_w_in"][j]
            qkv = _mm(h, w_in, name=f"dswa_proj{i}")
            o_n, lse_n, got = _dswa_attn_fwd(qkv, bias, name=f"dswa_attn_fwd{i}",
                                             comm=dist and dist.gather_comm(f"attn_fwd{i}"))
            if dist:
                dist.put(f"attn_fwd{i}", got)
            act = _dswa_combine_fwd(o_n, lse_n, name=f"dswa_comb_fwd{i}")
            sv.update(qkv=qkv, o_n=o_n, lse_n=lse_n, act=act)
            w_out = full["dswa_w_out"][j]
        cur = _mm(act, w_out, name=f"mix_out{i}", epilogue=lambda acc, r: (acc + r,), extras=(cur,))
        sv["x_mid"] = cur
        h2 = _rms_fwd(cur, norm_mlp[i], name=f"rms_mlp_fwd{i}")
        u, a = _mm(h2, full["mlp_w1"][i], name=f"mlp_up{i}", out_dtypes=(F32, BF16),
                   epilogue=lambda acc: (acc, jnp.square(jnp.maximum(acc, 0.0))))
        cur = _mm(a, full["mlp_w2"][i], name=f"mlp_down{i}", epilogue=lambda acc, r: (acc + r,), extras=(cur,))
        sv.update(h2=h2, u=u, a=a)
        saved.append(sv)

    loss_part, dcur, dcur_b, dg_final = _loss_head(cur, norm_final, target, name="loss_head")

    g_norm_mix, g_norm_mlp = [None] * DEPTH, [None] * DEPTH
    g_big = {n: [None] * len(full[n]) for n in _BIG}
    g_conv, g_alog, g_dt, g_nw = [None] * 2, [None] * 2, [None] * 2, [None] * 2
    d_table_t = jnp.zeros((DSWA_HEADS, REL_BUCKETS), F32)
    for i in reversed(range(DEPTH)):
        j = i // 2
        sv = saved[i]
        w1, w2 = full["mlp_w1"][i], full["mlp_w2"][i]
        du = _mm(dcur_b, w2, tb=True, name=f"mlp_down_bwd{i}", out_dtypes=(BF16,),
                 epilogue=lambda acc, uu: (acc * (2.0 * jnp.maximum(uu, 0.0)),), extras=(sv["u"],))
        g_big["mlp_w2"][i] = _mm(sv["a"], dcur_b, ta=True, name=f"mlp_w2_grad{i}", out_dtypes=(BF16,), shard="rows")
        g_big["mlp_w1"][i] = _mm(sv["h2"], du, ta=True, name=f"mlp_w1_grad{i}", out_dtypes=(BF16,), shard="cols")
        dh2 = _mm(du, w1, tb=True, name=f"mlp_up_bwd{i}")
        dmid, dmid_b, g_norm_mlp[i] = _rms_bwd(sv["x_mid"], norm_mlp[i], dh2, dcur, name=f"rms_mlp_bwd{i}")
        if i % 2 == 0:
            w_in, w_out = full["gdn_w_in"][j], full["gdn_w_out"][j]
            dact = _mm(dmid_b, w_out, tb=True, name=f"mix_out_bwd{i}")
            g_big["gdn_w_out"][j] = _mm(sv["act"], dmid_b, ta=True, name=f"mix_out_grad{i}", out_dtypes=(BF16,),
                                        shard="rows")
            do, dz, g_nw[j] = _gdn_post_bwd(sv["o"], sv["proj"], gdn_norm_w[j], dact, name=f"gdn_post_bwd{i}")
            dqkvn, dg_row, db_row, got = _gdn_chunk_bwd(sv["qkvn"], sv["g_row"], sv["b_row"], sv["states"], do,
                                                        name=f"gdn_chunk_bwd{i}",
                                                        comm=dist and dist.send_comm(f"chunk_bwd{i}", g_big))
            if dist:
                dist.got(f"chunk_bwd{i}", got)
            dpre, g_conv[j] = _gdn_pre_bwd(sv["proj"], conv_full[j], dqkvn, name=f"gdn_pre_bwd{i}")
            nh2 = 2 * GDN_HEADS
            da_, db_, g_alog[j], g_dt[j] = _gdn_gate_bwd(sv["ab"][:, :nh2], sv["ab"][:, nh2:], gdn_a_log[j], gdn_dt_bias[j],
                                                         dg_row.reshape(nh2, s).T, db_row.reshape(nh2, s).T,
                                                         name=f"gdn_gate_bwd{i}")
            dab = jnp.concatenate([da_, db_], axis=1)
            dproj = jnp.concatenate([dpre, dz], axis=1)
            gw_main = _mm(sv["h"], dproj, ta=True, name=f"gdn_w_in_grad{i}", out_dtypes=(BF16,))
            gw_ab = _mm(sv["h"], dab, ta=True, name=f"gdn_w_ab_grad{i}", out_dtypes=(BF16,))
            g_big["gdn_w_in"][j] = jnp.concatenate([gw_main, gw_ab], axis=1)
            dh_ab = _mm(dab, w_in[:, GDN_MAIN:], tb=True, name=f"gdn_proj_ab_bwd{i}")
            dh = _mm(dproj, w_in[:, :GDN_MAIN], tb=True, name=f"gdn_proj_bwd{i}",
                     epilogue=lambda acc, r: (acc + r,), extras=(dh_ab,))
        else:
            w_in, w_out = full["dswa_w_in"][j], full["dswa_w_out"][j]
            dact = _mm(dmid_b, w_out, tb=True, name=f"mix_out_bwd{i}")
            g_big["dswa_w_out"][j] = _mm(sv["act"], dmid_b, ta=True, name=f"mix_out_grad{i}", out_dtypes=(BF16,),
                                         shard="rows")
            do_n, corr_n = _dswa_combine_bwd(sv["o_n"], sv["lse_n"], dact, name=f"dswa_comb_bwd{i}")
            *dqkv, dbias, got = _dswa_attn_bwd(sv["qkv"], bias, sv["lse_n"], do_n, corr_n, name=f"dswa_attn_bwd{i}",
                                               comm=dist and dist.send_comm(f"attn_bwd{i}", g_big))
            if dist:
                dist.got(f"attn_bwd{i}", got)
            d_table_t = d_table_t + _dswa_dtable(dbias.reshape(DSWA_HEADS, -1), onehot, name=f"dswa_dtable{i}")
            g_big["dswa_w_in"][j] = jnp.concatenate(
                [_mm(sv["h"], dt, ta=True, name=f"dswa_w_in_grad{i}_{t}", out_dtypes=(BF16,)) for t, dt in enumerate(dqkv)],
                axis=1)
            dh = None
            for t, dt in enumerate(dqkv):
                w_t = w_in[:, t * DSWA_WIDTH:(t + 1) * DSWA_WIDTH]
                if dh is None:
                    dh = _mm(dt, w_t, tb=True, name=f"dswa_proj_bwd{i}_{t}")
                else:
                    dh = _mm(dt, w_t, tb=True, name=f"dswa_proj_bwd{i}_{t}", epilogue=lambda acc, r: (acc + r,), extras=(dh,))
        dcur, dcur_b, g_norm_mix[i] = _rms_bwd(sv["x_in"], norm_mix[i], dh, dmid, name=f"rms_mix_bwd{i}")

    rep = dict(norm_mix=jnp.concatenate(g_norm_mix, axis=0), norm_mlp=jnp.concatenate(g_norm_mlp, axis=0),
               norm_final=dg_final.reshape(-1), rel_bias=d_table_t.T,
               gdn_a_log=jnp.stack(g_alog).reshape(gdn_a_log.shape), gdn_dt_bias=jnp.stack(g_dt).reshape(gdn_dt_bias.shape),
               gdn_norm_w=jnp.stack(g_nw).reshape(gdn_norm_w.shape))
    return loss_part, dcur, g_big, rep, g_conv
```

```python
import functools
import math

import jax
import jax.numpy as jnp
import numpy as np
from jax import lax
from jax.experimental import pallas as pl
from jax.experimental.pallas import tpu as pltpu

F32 = jnp.float32
BF16 = jnp.bfloat16
HP = lax.Precision.HIGHEST

N_DEV = 8
D_MODEL = 1024
DEPTH = 4
RMS_EPS = 1e-6
NEG_INF = -1e30

GDN_HEADS = 8
GDN_DK = 128
GDN_CONV = 5
GDN_CHUNK = 128
GDN_QKV = 3 * GDN_HEADS * GDN_DK
GDN_MAIN = GDN_QKV + GDN_HEADS * GDN_DK
GDN_AB = 4 * GDN_HEADS

DSWA_DILS = (1, 4, 16)
DSWA_HG = 6
DSWA_E = 64
DSWA_HEADS = 18
DSWA_WIDTH = DSWA_HEADS * DSWA_E
DSWA_HALF = 64
DSWA_PG = DSWA_HG // 2
DSWA_UNROLL = 4
REL_BUCKETS = 32
REL_MAX_DIST = 1024

ADAM_LR = 0.001
ADAM_B1 = 0.9
ADAM_B2 = 0.999
ADAM_EPS = 1e-08
ADAM_WD = 0.01
ADAM_STEP = 10

VMEM_LIMIT = 56 * 1024 * 1024


def _cparams(sem=None, **kw):
    return pltpu.CompilerParams(dimension_semantics=sem, vmem_limit_bytes=VMEM_LIMIT, **kw)


def _pick(dim, cands):
    for c in cands:
        if dim % c == 0:
            return c
    return dim


def _bdot(a, b):
    return jnp.dot(a.astype(BF16), b.astype(BF16), preferred_element_type=F32)


def _bdot_nt(a, b):
    return lax.dot_general(a.astype(BF16), b.astype(BF16), (((1,), (1,)), ((), ())),
                           preferred_element_type=F32)


def _bdot_tn(a, b):
    return lax.dot_general(a.astype(BF16), b.astype(BF16), (((0,), (0,)), ((), ())),
                           preferred_element_type=F32)


def _hdot(a, b):
    return jnp.dot(a, b, precision=HP, preferred_element_type=F32)


def _hdot_tn(a, b):
    return lax.dot_general(a, b, (((0,), (0,)), ((), ())), precision=HP, preferred_element_type=F32)


def _hdot_nt(a, b):
    return lax.dot_general(a, b, (((1,), (1,)), ((), ())), precision=HP, preferred_element_type=F32)


def _sigmoid(x):
    return 1.0 / (1.0 + jnp.exp(-x))


def _mm(a, b, *, name, ta=False, tb=False, out_dtypes=(F32,), epilogue=None, extras=(),
        tm=None, tn=None, tk=None, shard=None):
    if ta:
        kdim, m = a.shape
    else:
        m, kdim = a.shape
    n = b.shape[0] if tb else b.shape[1]
    if shard == "rows":
        tm = m // N_DEV if (m // N_DEV) % 128 == 0 else m
    if shard == "cols":
        tn = n // N_DEV
    tm = tm or _pick(m, (1024, 1152, 512, 384, 256, 128))
    tn = tn or _pick(n, (1024, 1152, 512, 384, 256, 128))
    tk = tk or _pick(kdim, (1024, 1152, 512, 384, 256, 128))
    nk = kdim // tk
    n_out = len(out_dtypes)
    n_ex = len(extras)
    rows_all = shard == "rows" and tm == m

    def body(*refs):
        a_ref, b_ref = refs[0], refs[1]
        ex_refs = refs[2:2 + n_ex]
        out_refs = refs[2 + n_ex:2 + n_ex + n_out]
        acc_ref = refs[-1]
        k = pl.program_id(2)

        @pl.when(k == 0)
        def _():
            acc_ref[...] = jnp.zeros_like(acc_ref)

        av = a_ref[...].astype(BF16)
        bv = b_ref[...].astype(BF16)
        dims = (((0 if ta else 1,), (1 if tb else 0,)), ((), ()))
        acc_ref[...] += lax.dot_general(av, bv, dims, preferred_element_type=F32)

        @pl.when(k == nk - 1)
        def _():
            acc = acc_ref[...]
            outs = (acc,) if epilogue is None else epilogue(acc, *[r[...] for r in ex_refs])
            for r, o in zip(out_refs, outs):
                if rows_all:
                    for p in range(N_DEV):
                        r[p] = o[p * (m // N_DEV):(p + 1) * (m // N_DEV)].astype(r.dtype)
                else:
                    r[...] = o.astype(r.dtype)

    a_spec = pl.BlockSpec((tk, tm), lambda i, j, k: (k, i)) if ta else pl.BlockSpec((tm, tk), lambda i, j, k: (i, k))
    b_spec = pl.BlockSpec((tn, tk), lambda i, j, k: (j, k)) if tb else pl.BlockSpec((tk, tn), lambda i, j, k: (k, j))
    o_spec = pl.BlockSpec((tm, tn), lambda i, j, k: (i, j))
    out_specs = [o_spec] * n_out
    out_shape = [jax.ShapeDtypeStruct((m, n), dt) for dt in out_dtypes]
    if shard == "rows":
        out_shape = [jax.ShapeDtypeStruct((N_DEV, m // N_DEV, n), out_dtypes[0])]
        out_specs = [pl.BlockSpec((N_DEV, m // N_DEV, tn), lambda i, j, k: (0, 0, j)) if rows_all
                     else pl.BlockSpec((None, tm, tn), lambda i, j, k: (i, 0, j))]
    if shard == "cols":
        out_shape = [jax.ShapeDtypeStruct((N_DEV, m, tn), out_dtypes[0])]
        out_specs = [pl.BlockSpec((None, tm, tn), lambda i, j, k: (j, i, 0))]
    outs = pl.pallas_call(
        body, name=name,
        grid=(m // tm, n // tn, nk),
        in_specs=[a_spec, b_spec] + [o_spec] * n_ex,
        out_specs=out_specs,
        out_shape=out_shape,
        scratch_shapes=[pltpu.VMEM((tm, tn), F32)],
        compiler_params=_cparams(("parallel", "parallel", "arbitrary")),
    )(a, b, *extras)
    return outs[0] if n_out == 1 else outs


def _rms_fwd(x, g, *, name):
    s, d = x.shape
    tr = _pick(s, (512, 256, 128))

    def body(x_ref, g_ref, h_ref):
        xv = x_ref[...]
        r = lax.rsqrt(jnp.mean(xv * xv, axis=-1, keepdims=True) + RMS_EPS)
        h_ref[...] = (xv * r * g_ref[...]).astype(h_ref.dtype)

    return pl.pallas_call(
        body, name=name, grid=(s // tr,),
        in_specs=[pl.BlockSpec((tr, d), lambda i: (i, 0)), pl.BlockSpec((1, d), lambda i: (0, 0))],
        out_specs=pl.BlockSpec((tr, d), lambda i: (i, 0)),
        out_shape=jax.ShapeDtypeStruct((s, d), BF16),
        compiler_params=_cparams(("parallel",)),
    )(x, g.reshape(1, d))


def _rms_bwd(x, g, dh, dres, *, name):
    s, d = x.shape
    tr = _pick(s, (512, 256, 128))

    def body(x_ref, g_ref, dh_ref, dres_ref, dx_ref, dxb_ref, dg_ref):
        i = pl.program_id(0)
        xv = x_ref[...]
        r = lax.rsqrt(jnp.mean(xv * xv, axis=-1, keepdims=True) + RMS_EPS)
        xn = xv * r
        dhv = dh_ref[...]
        dn = dhv * g_ref[...]
        dx = dres_ref[...] + r * (dn - xn * jnp.mean(dn * xn, axis=-1, keepdims=True))
        dx_ref[...] = dx
        dxb_ref[...] = dx.astype(dxb_ref.dtype)
        part = jnp.sum(dhv * xn, axis=0, keepdims=True)

        @pl.when(i == 0)
        def _():
            dg_ref[...] = part

        @pl.when(i > 0)
        def _():
            dg_ref[...] += part

    row = pl.BlockSpec((tr, d), lambda i: (i, 0))
    vec = pl.BlockSpec((1, d), lambda i: (0, 0))
    return pl.pallas_call(
        body, name=name, grid=(s // tr,),
        in_specs=[row, vec, row, row], out_specs=[row, row, vec],
        out_shape=[jax.ShapeDtypeStruct((s, d), F32), jax.ShapeDtypeStruct((s, d), BF16),
                   jax.ShapeDtypeStruct((1, d), F32)],
        compiler_params=_cparams(("arbitrary",)),
    )(x, g.reshape(1, d), dh, dres)


def _loss_head(x, g, target, *, name):
    s, d = x.shape
    tr = _pick(s, (512, 256, 128))

    def body(x_ref, g_ref, t_ref, loss_ref, dx_ref, dxb_ref, dg_ref):
        i = pl.program_id(0)
        xv = x_ref[...]
        gv = g_ref[...]
        r = lax.rsqrt(jnp.mean(xv * xv, axis=-1, keepdims=True) + RMS_EPS)
        xn = xv * r
        err = xn * gv - t_ref[...]
        lpart = 0.5 * jnp.sum(jnp.mean(err * err, axis=-1, keepdims=True), axis=0, keepdims=True)
        dy = err * (1.0 / d)
        dn = dy * gv
        dx = r * (dn - xn * jnp.mean(dn * xn, axis=-1, keepdims=True))
        dx_ref[...] = dx
        dxb_ref[...] = dx.astype(dxb_ref.dtype)
        gpart = jnp.sum(dy * xn, axis=0, keepdims=True)

        @pl.when(i == 0)
        def _():
            dg_ref[...] = gpart
            loss_ref[...] = lpart

        @pl.when(i > 0)
        def _():
            dg_ref[...] += gpart
            loss_ref[...] += lpart

    row = pl.BlockSpec((tr, d), lambda i: (i, 0))
    vec = pl.BlockSpec((1, d), lambda i: (0, 0))
    one = pl.BlockSpec((1, 1), lambda i: (0, 0))
    return pl.pallas_call(
        body, name=name, grid=(s // tr,),
        in_specs=[row, vec, row], out_specs=[one, row, row, vec],
        out_shape=[jax.ShapeDtypeStruct((1, 1), F32), jax.ShapeDtypeStruct((s, d), F32),
                   jax.ShapeDtypeStruct((s, d), BF16), jax.ShapeDtypeStruct((1, d), F32)],
        compiler_params=_cparams(("arbitrary",)),
    )(x, g.reshape(1, d), target)


def _shift_rows(x, sft, rows):
    s = x.shape[0]
    if sft == 0:
        return x
    y = pltpu.roll(x, (-sft) % s, 0)
    ok = (rows + sft >= 0) & (rows + sft < s)
    return jnp.where(ok, y, 0.0)


def _gdn_pre_fwd(proj, conv_w, *, name):
    s = proj.shape[0]
    nblk = GDN_QKV // 128
    pad = GDN_CONV // 2

    def body(x_ref, w_ref, o_ref):
        j = pl.program_id(0)
        x = x_ref[...]
        rows = lax.broadcasted_iota(jnp.int32, x.shape, 0)
        c = jnp.zeros_like(x)
        for t in range(GDN_CONV):
            c = c + w_ref[pl.ds(t, 1), :] * _shift_rows(x, t - pad, rows)
        a = c * _sigmoid(c)
        rinv = lax.rsqrt(jnp.sum(a * a, axis=-1, keepdims=True) + 1e-6)
        scale = jnp.where(j < GDN_HEADS, GDN_DK ** -0.5, 1.0)
        o_ref[...] = jnp.where(j >= 2 * GDN_HEADS, a, a * (rinv * scale))

    return pl.pallas_call(
        body, name=name, grid=(nblk,),
        in_specs=[pl.BlockSpec((s, 128), lambda j: (0, j)), pl.BlockSpec((GDN_CONV, 128), lambda j: (0, j))],
        out_specs=pl.BlockSpec((s, 128), lambda j: (0, j)),
        out_shape=jax.ShapeDtypeStruct((s, GDN_QKV), F32),
        compiler_params=_cparams(("parallel",)),
    )(proj, conv_w)


def _gdn_pre_bwd(proj, conv_w, dqkv, dproj, *, name):
    s = proj.shape[0]
    nblk = GDN_QKV // 128
    pad = GDN_CONV // 2

    def body(x_ref, w_ref, df_ref, dbk_ref, _, dx_ref, dw_ref):
        j = pl.program_id(0)
        x = x_ref[...]
        rows = lax.broadcasted_iota(jnp.int32, x.shape, 0)
        xs = [_shift_rows(x, t - pad, rows) for t in range(GDN_CONV)]
        c = jnp.zeros_like(x)
        for t in range(GDN_CONV):
            c = c + w_ref[pl.ds(t, 1), :] * xs[t]
        sg = _sigmoid(c)
        a = c * sg
        rinv = lax.rsqrt(jnp.sum(a * a, axis=-1, keepdims=True) + 1e-6)
        scale = jnp.where(j < GDN_HEADS, GDN_DK ** -0.5, 1.0)
        dy = df_ref[...] + dbk_ref[...]
        nh = a * rinv
        da_n = (rinv * scale) * (dy - nh * jnp.sum(dy * nh, axis=-1, keepdims=True))
        da = jnp.where(j >= 2 * GDN_HEADS, dy, da_n)
        dc = da * (sg * (1.0 + c * (1.0 - sg)))
        dx = jnp.zeros_like(x)
        for t in range(GDN_CONV):
            dx = dx + w_ref[pl.ds(t, 1), :] * _shift_rows(dc, pad - t, rows)
            dw_ref[pl.ds(t, 1), :] = jnp.sum(dc * xs[t], axis=0, keepdims=True)
        dx_ref[...] = dx.astype(dx_ref.dtype)

    col = pl.BlockSpec((s, 128), lambda j: (0, j))
    wsp = pl.BlockSpec((GDN_CONV, 128), lambda j: (0, j))
    return pl.pallas_call(
        body, name=name, grid=(nblk,),
        in_specs=[col, wsp, col, col, pl.BlockSpec(memory_space=pl.ANY)], out_specs=[col, wsp],
        out_shape=[jax.ShapeDtypeStruct(dproj.shape, BF16), jax.ShapeDtypeStruct((GDN_CONV, GDN_QKV), F32)],
        input_output_aliases={4: 0},
        compiler_params=_cparams(("parallel",)),
    )(proj, conv_w, dqkv[0], dqkv[1], dproj)


def _softplus(x):
    return jnp.maximum(x, 0.0) + jnp.log(1.0 + jnp.exp(-jnp.abs(x)))


def _gdn_gate_fwd(a, b, a_log, dt_bias, *, name):
    s = a.shape[0]
    nh = 2 * GDN_HEADS

    def body(a_ref, b_ref, al_ref, dt_ref, g_ref, be_ref):
        g_ref[...] = -jnp.exp(al_ref[...]) * _softplus(a_ref[...] + dt_ref[...])
        be_ref[...] = _sigmoid(b_ref[...])

    return pl.pallas_call(
        body, name=name,
        out_shape=[jax.ShapeDtypeStruct((s, nh), F32), jax.ShapeDtypeStruct((s, nh), F32)],
        compiler_params=_cparams(),
    )(a, b, a_log.reshape(1, nh), dt_bias.reshape(1, nh))


def _gdn_gate_bwd(a, b, a_log, dt_bias, dg, dbeta, *, name):
    s = a.shape[0]
    nh = 2 * GDN_HEADS

    def body(a_ref, b_ref, al_ref, dt_ref, dg_ref, db_ref, da_ref, dbb_ref, dal_ref, ddt_ref):
        ea = jnp.exp(al_ref[...])
        z = a_ref[...] + dt_ref[...]
        dgv = dg_ref[...]
        dz = dgv * (-ea) * _sigmoid(z)
        dal_ref[...] = jnp.sum(dgv * (-ea) * _softplus(z), axis=0, keepdims=True)
        ddt_ref[...] = jnp.sum(dz, axis=0, keepdims=True)
        sb = _sigmoid(b_ref[...])
        da_ref[...] = dz
        dbb_ref[...] = db_ref[...] * sb * (1.0 - sb)

    return pl.pallas_call(
        body, name=name,
        out_shape=[jax.ShapeDtypeStruct((s, nh), F32), jax.ShapeDtypeStruct((s, nh), F32),
                   jax.ShapeDtypeStruct((1, nh), F32), jax.ShapeDtypeStruct((1, nh), F32)],
        compiler_params=_cparams(),
    )(a, b, a_log.reshape(1, nh), dt_bias.reshape(1, nh), dg, dbeta)


def _chunk_masks(d):
    c = GDN_CHUNK
    ii = lax.broadcasted_iota(jnp.int32, (c, c), 0)
    jj = lax.broadcasted_iota(jnp.int32, (c, c), 1)
    dif = (ii - jj) * (1 - 2 * d)
    mi = dif >= 0
    mit = dif <= 0
    ms = dif > 0
    eye = ii == jj
    bds = [(ii >> sh) == (jj >> sh) for sh in range(3, c.bit_length() - 1)]
    return dict(mi=mi, mit=mit, ms=ms, eye=eye, bds=bds,
                mif=mi.astype(F32), mitf=mit.astype(F32), eyef=eye.astype(F32))


class _V:
    def __init__(self, xs):
        self.xs = tuple(xs)

    def __add__(self, o):
        return _lift(lambda a, b: a + b)(self, o)

    def __radd__(self, o):
        return _lift(lambda a, b: b + a)(self, o)

    def __sub__(self, o):
        return _lift(lambda a, b: a - b)(self, o)

    def __rsub__(self, o):
        return _lift(lambda a, b: b - a)(self, o)

    def __mul__(self, o):
        return _lift(lambda a, b: a * b)(self, o)

    def __rmul__(self, o):
        return _lift(lambda a, b: b * a)(self, o)

    def __and__(self, o):
        return _lift(lambda a, b: a & b)(self, o)

    def __neg__(self):
        return _lift(lambda a: -a)(self)

    def __rtruediv__(self, o):
        return _lift(lambda a, b: b / a)(self, o)


def _lift(f):
    def g(*args, **kw):
        n = next(len(a.xs) for a in args if isinstance(a, _V))
        return _V(f(*[a.xs[i] if isinstance(a, _V) else a for a in args], **kw) for i in range(n))
    return g


_vwhere, _vsum, _vexp, _vnot = _lift(jnp.where), _lift(jnp.sum), _lift(jnp.exp), _lift(jnp.logical_not)
_vhdot, _vhdot_tn = _lift(_bdot), _lift(_bdot_tn)
_vbdot, _vbdot_nt, _vbdot_tn = _lift(_bdot), _lift(_bdot_nt), _lift(_bdot_tn)
_vcat = _lift(lambda a, b: jnp.concatenate([a, b], axis=1))
_vlo = _lift(lambda a, n: a[:, :n])
_vhi = _lift(lambda a, n: a[:, n:])


def _both_masks(n):
    m = [_chunk_masks(d) for d in range(2)]
    mk = {key: _V([m[0][key]] * n + [m[1][key]] * n) for key in m[0] if key != "bds"}
    mk["bds"] = [_V([m[0]["bds"][i]] * n + [m[1]["bds"][i]] * n) for i in range(len(m[0]["bds"]))]
    return mk


def _tri_inv(a, mk):
    eyef = mk["eyef"]
    bds = mk["bds"]
    a8 = _vwhere(bds[0], a, 0.0)
    a2 = _vhdot(a8, a8)
    a4 = _vhdot(a2, a2)
    t = _vhdot(_vhdot(eyef - a8, eyef + a2), eyef + a4)
    for inner, outer in zip(bds, bds[1:] + [None]):
        off = _vnot(inner) if outer is None else (outer & _vnot(inner))
        low = _vwhere(off, a, 0.0)
        t = t - _vhdot(_vhdot(t, low), t)
    return t


def _chunk_prep(q, k, v, g_row, b_row, mk):
    dv = GDN_DK
    g_col = _vsum(mk["eyef"] * g_row, axis=1, keepdims=True)
    b_col = _vsum(mk["eyef"] * b_row, axis=1, keepdims=True)
    gc_col = _vsum(mk["mif"] * g_row, axis=1, keepdims=True)
    gc_row = _vsum(mk["mitf"] * g_col, axis=0, keepdims=True)
    gl = _vsum(g_row, axis=1, keepdims=True)
    decay = _vwhere(mk["mi"], _vexp(_vwhere(mk["mi"], gc_col - gc_row, 0.0)), 0.0)
    eg = _vexp(gc_col)
    e2 = _vexp(gl - gc_col)
    egl = _vexp(gl)
    kb = k * b_col
    pm = _vbdot_nt(kb, k)
    a = _vwhere(mk["ms"], pm * decay, 0.0)
    t = _tri_inv(a, mk)
    sol = _vhdot(t, _vcat(v * b_col, kb * eg))
    u, w = _vlo(sol, dv), _vhi(sol, dv)
    qm = _vbdot_nt(q, k)
    return dict(b_col=b_col, decay=decay, eg=eg, e2=e2, egl=egl, kb=kb, pm=pm, t=t, u=u, w=w,
                qm=qm, intra=qm * decay, qd=q * eg, kd=k * e2)


def _chunk_fwd_step(p, state):
    v_new = p["u"] - _vbdot(p["w"], state)
    o = _vbdot(p["qd"], state) + _vbdot(p["intra"], v_new)
    new_state = state * p["egl"] + _vbdot_tn(p["kd"], v_new)
    return o, new_state


def _chunk_bwd_step(q, k, v, p, mk, state, dso, do):
    dv_dim = GDN_DK
    v_new = p["u"] - _vbdot(p["w"], state)
    dvn = _vbdot_tn(p["intra"], do) + _vbdot(p["kd"], dso)
    dintra = _vbdot_nt(do, v_new)
    dqd = _vbdot_nt(do, state)
    ds = p["egl"] * dso + _vbdot_tn(p["qd"], do) - _vbdot_tn(p["w"], dvn)
    dkd = _vbdot_nt(v_new, dso)
    dgl = _vsum(_vsum(dso * state, axis=1, keepdims=True), axis=0, keepdims=True) * p["egl"]
    dw = -_vbdot_nt(dvn, state)
    drhs = _vhdot_tn(p["t"], _vcat(dvn, dw))
    dru, drw = _vlo(drhs, dv_dim), _vhi(drhs, dv_dim)
    da = -_vwhere(mk["ms"], _vbdot_nt(drhs, _vcat(p["u"], p["w"])), 0.0)
    b_col = p["b_col"]
    dv = dru * b_col
    dbeta = _vsum(dru * v, axis=1, keepdims=True)
    dkb = drw * p["eg"]
    deg = _vsum(drw * p["kb"], axis=1, keepdims=True)
    dp = da * p["decay"]
    ddecay = da * p["pm"]
    dkb = dkb + _vbdot(dp, k)
    dk = _vbdot_tn(dp, p["kb"])
    dqm = dintra * p["decay"]
    ddecay = ddecay + dintra * p["qm"]
    dq = _vbdot(dqm, k)
    dk = dk + _vbdot_tn(dqm, q)
    dd = ddecay * p["decay"]
    dgc_col = _vsum(dd, axis=1, keepdims=True)
    dgc_row = -_vsum(dd, axis=0, keepdims=True)
    dq = dq + dqd * p["eg"]
    deg = deg + _vsum(dqd * q, axis=1, keepdims=True)
    dk = dk + dkd * p["e2"]
    de2 = _vsum(dkd * k, axis=1, keepdims=True) * p["e2"]
    dgl = dgl + _vsum(de2, axis=0, keepdims=True)
    dgc_col = dgc_col - de2 + deg * p["eg"]
    dk = dk + dkb * b_col
    dbeta = dbeta + _vsum(dkb * k, axis=1, keepdims=True)
    dgc_col = dgc_col + _vsum(mk["eyef"] * dgc_row, axis=1, keepdims=True)
    dg_row = _vsum(mk["mif"] * dgc_col, axis=0, keepdims=True) + dgl
    dbeta_row = _vsum(mk["eyef"] * dbeta, axis=0, keepdims=True)
    return dq, dk, dv, dg_row, dbeta_row, ds


def _gdn_chunk_fwd(qkvn, g5, b5, *, name, comm=None):
    s = qkvn.shape[0]
    c = GDN_CHUNK
    nc = s // c
    h_, dk = GDN_HEADS, GDN_DK

    def body(*refs):
        n = pl.program_id(0)
        ins, outs, (st_scr,) = _comm_hooks(comm, refs, 6, 4, 1, n == 0, n == nc // 2, n == nc - 1)
        x_refs, g_refs, b_refs = ins[0:2], ins[2:4], ins[4:6]
        o_refs, st_refs = outs[0:2], outs[2:4]

        @pl.when(n == 0)
        def _():
            st_scr[...] = jnp.zeros_like(st_scr)

        ch = [(d, h) for d in range(2) for h in range(h_)]
        mk = _both_masks(h_)
        q, k, v = (_V(x_refs[d][:, (t * h_ + h) * dk:(t * h_ + h + 1) * dk] for d, h in ch) for t in range(3))
        g, b = (_V(r[d][0, h, 0] for d, h in ch) for r in (g_refs, b_refs))
        state = _V(st_scr[d * h_ + h] for d, h in ch)
        o, new_state = _chunk_fwd_step(_chunk_prep(q, k, v, g, b, mk), state)
        for i, (d, h) in enumerate(ch):
            st_refs[d][h, 0] = state.xs[i]
            st_scr[d * h_ + h] = new_state.xs[i]
            o_refs[d][:, h * dk:(h + 1) * dk] = o.xs[i]

    ce = (lambda n: n, lambda n: nc - 1 - n)
    xs = [pl.BlockSpec((c, 3 * h_ * dk), lambda n, d=d: (ce[d](n), 0)) for d in range(2)]
    gates = [pl.BlockSpec((1, h_, 1, 1, c), lambda n, d=d: (d, 0, ce[d](n), 0, 0)) for d in range(2)]
    os_ = [pl.BlockSpec((c, h_ * dk), lambda n, d=d: (ce[d](n), 0)) for d in range(2)]
    sts = [pl.BlockSpec((h_, 1, dk, dk), lambda n, d=d: (0, ce[d](n), 0, 0)) for d in range(2)]
    c_in, c_out, c_shape, c_scr = _comm_specs(comm)
    res = pl.pallas_call(
        body, name=name, grid=(nc,),
        in_specs=xs + gates + gates + c_in,
        out_specs=os_ + sts + c_out,
        out_shape=[jax.ShapeDtypeStruct((s, h_ * dk), F32)] * 2 + [jax.ShapeDtypeStruct((h_, nc, dk, dk), F32)] * 2 + c_shape,
        scratch_shapes=[pltpu.VMEM((2 * h_, dk, dk), F32)] + c_scr,
        compiler_params=_cparams(("arbitrary",)),
    )(qkvn, qkvn, g5, g5, b5, b5, *(comm.arrays if comm else []))
    return res[0:2], res[2:4], res[4:]


def _gdn_chunk_bwd(qkvn, g5, b5, states, do, *, name, comm=None):
    s = qkvn.shape[0]
    c = GDN_CHUNK
    nc = s // c
    h_, dk = GDN_HEADS, GDN_DK

    def body(*refs):
        i = pl.program_id(0)
        ins, outs, (ds_scr,) = _comm_hooks(comm, refs, 10, 6, 1, i == 0, i == nc // 2, i == nc - 1)
        x_refs, g_refs, b_refs, st_refs, do_refs = ins[0:2], ins[2:4], ins[4:6], ins[6:8], ins[8:10]
        dx_refs, dg_refs, db_refs = outs[0:2], outs[2:4], outs[4:6]

        @pl.when(i == 0)
        def _():
            ds_scr[...] = jnp.zeros_like(ds_scr)

        ch = [(d, h) for d in range(2) for h in range(h_)]
        mk = _both_masks(h_)
        q, k, v = (_V(x_refs[d][:, (t * h_ + h) * dk:(t * h_ + h + 1) * dk] for d, h in ch) for t in range(3))
        g, b = (_V(r[d][0, h, 0] for d, h in ch) for r in (g_refs, b_refs))
        state = _V(st_refs[d][h, 0] for d, h in ch)
        dso = _V(ds_scr[d * h_ + h] for d, h in ch)
        dov = _V(do_refs[d][:, h * dk:(h + 1) * dk] for d, h in ch)
        res = _chunk_bwd_step(q, k, v, _chunk_prep(q, k, v, g, b, mk), mk, state, dso, dov)
        for (d, h), (dq, dkk, dvv, dg_r, db_r, ds) in zip(ch, zip(*[r.xs for r in res])):
            ds_scr[d * h_ + h] = ds
            dg_refs[d][h, 0] = dg_r
            db_refs[d][h, 0] = db_r
            for t, val in enumerate((dq, dkk, dvv)):
                dx_refs[d][:, (t * h_ + h) * dk:(t * h_ + h + 1) * dk] = val

    ce = (lambda i: nc - 1 - i, lambda i: i)
    both = lambda mk_spec: [mk_spec(d) for d in range(2)]
    xs = both(lambda d: pl.BlockSpec((c, 3 * h_ * dk), lambda i: (ce[d](i), 0)))
    gates = both(lambda d: pl.BlockSpec((1, h_, 1, 1, c), lambda i: (d, 0, ce[d](i), 0, 0)))
    sts = both(lambda d: pl.BlockSpec((h_, 1, dk, dk), lambda i: (0, ce[d](i), 0, 0)))
    dos = both(lambda d: pl.BlockSpec((c, h_ * dk), lambda i: (ce[d](i), 0)))
    gouts = both(lambda d: pl.BlockSpec((h_, 1, 1, c), lambda i: (0, ce[d](i), 0, 0)))
    c_in, c_out, c_shape, c_scr = _comm_specs(comm)
    res = pl.pallas_call(
        body, name=name, grid=(nc,),
        in_specs=xs + gates + gates + sts + dos + c_in,
        out_specs=xs + gouts + gouts + c_out,
        out_shape=[jax.ShapeDtypeStruct((s, 3 * h_ * dk), F32)] * 2
        + [jax.ShapeDtypeStruct((h_, nc, 1, c), F32)] * 4 + c_shape,
        scratch_shapes=[pltpu.VMEM((2 * h_, dk, dk), F32)] + c_scr,
        compiler_params=_cparams(("arbitrary",)),
    )(qkvn, qkvn, g5, g5, b5, b5, states[0], states[1], do, do, *(comm.arrays if comm else []))
    return res[0:2], jnp.stack(res[2:4]), jnp.stack(res[4:6]), res[6:]


def _gdn_post_fwd(o, z, norm_w, *, name):
    s = o[0].shape[0]
    h_, dk = GDN_HEADS, GDN_DK

    def body(of_ref, ob_ref, z_ref, w_ref, a_ref):
        ov = of_ref[...] + ob_ref[...]
        zv = z_ref[...]
        r = lax.rsqrt(jnp.mean(ov * ov, axis=-1, keepdims=True) + RMS_EPS)
        a_ref[...] = (ov * r * w_ref[...] * (zv * _sigmoid(zv))).astype(a_ref.dtype)

    col = pl.BlockSpec((s, dk), lambda h: (0, h))
    return pl.pallas_call(
        body, name=name, grid=(h_,),
        in_specs=[col, col, pl.BlockSpec((s, dk), lambda h: (0, 3 * h_ + h)), pl.BlockSpec((1, dk), lambda h: (0, 0))],
        out_specs=col,
        out_shape=jax.ShapeDtypeStruct((s, h_ * dk), BF16),
        compiler_params=_cparams(("parallel",)),
    )(o[0], o[1], z, norm_w.reshape(1, dk))


def _gdn_post_bwd(o, z, norm_w, dact, *, name):
    s = o[0].shape[0]
    h_, dk = GDN_HEADS, GDN_DK

    def body(of_ref, ob_ref, z_ref, w_ref, da_ref, do_ref, dz_ref, dw_ref):
        h = pl.program_id(0)
        ov = of_ref[...] + ob_ref[...]
        zv = z_ref[...]
        wv = w_ref[...]
        dav = da_ref[...]
        r = lax.rsqrt(jnp.mean(ov * ov, axis=-1, keepdims=True) + RMS_EPS)
        nrm = ov * r
        sg = _sigmoid(zv)
        sz = zv * sg
        dn = dav * wv * sz
        do_ref[...] = r * (dn - nrm * jnp.mean(dn * nrm, axis=-1, keepdims=True))
        dz_ref[...] = (dav * nrm * wv * (sg * (1.0 + zv * (1.0 - sg)))).astype(dz_ref.dtype)
        part = jnp.sum(dav * nrm * sz, axis=0, keepdims=True)

        @pl.when(h == 0)
        def _():
            dw_ref[...] = part

        @pl.when(h > 0)
        def _():
            dw_ref[...] += part

    col = pl.BlockSpec((s, dk), lambda h: (0, h))
    vec = pl.BlockSpec((1, dk), lambda h: (0, 0))
    return pl.pallas_call(
        body, name=name, grid=(h_,),
        in_specs=[col, col, pl.BlockSpec((s, dk), lambda h: (0, 3 * h_ + h)), vec, col],
        out_specs=[col, pl.BlockSpec((s, dk), lambda h: (0, 3 * h_ + h)), vec],
        out_shape=[jax.ShapeDtypeStruct((s, h_ * dk), F32), jax.ShapeDtypeStruct((s, GDN_MAIN), BF16),
                   jax.ShapeDtypeStruct((1, dk), F32)],
        compiler_params=_cparams(("arbitrary",)),
    )(o[0], o[1], z, norm_w.reshape(1, dk), dact)


def _rel_bucket(rel):
    nb = REL_BUCKETS // 2
    max_exact = nb // 2
    ret = jnp.where(rel > 0, nb, 0)
    n = jnp.abs(rel)
    nf = jnp.maximum(n, 1).astype(F32)
    large = max_exact + (jnp.log(nf / max_exact) / math.log(REL_MAX_DIST / max_exact)
                         * (nb - max_exact)).astype(jnp.int32)
    large = jnp.minimum(large, nb - 1)
    return ret + jnp.where(n < max_exact, n, large)


def _bucket_onehot():
    half = DSWA_HALF
    outs = []
    for dil in DSWA_DILS:
        rel = (jnp.arange(3 * half)[None, :] - half - jnp.arange(half)[:, None]) * dil
        outs.append(jax.nn.one_hot(_rel_bucket(rel).reshape(-1), REL_BUCKETS, dtype=F32, axis=0))
    return jnp.stack(outs)


def _head_group_select(vals):
    rows = lax.broadcasted_iota(jnp.int32, vals[0].shape, 0)
    return jnp.where(rows < DSWA_HG, vals[0], jnp.where(rows < 2 * DSWA_HG, vals[1], vals[2]))


def _dswa_bias(table_t, onehot, *, name):
    p = onehot.shape[-1]

    def body(t_ref, oh_ref, b_ref):
        b_ref[...] = _head_group_select([_hdot(t_ref[...], oh_ref[g]) for g in range(3)])

    return pl.pallas_call(body, name=name, out_shape=jax.ShapeDtypeStruct((DSWA_HEADS, p), F32),
                          compiler_params=_cparams())(table_t, onehot)


def _dswa_dtable(dbias, onehot, *, name):
    def body(d_ref, oh_ref, t_ref):
        t_ref[...] = _head_group_select([_hdot_nt(d_ref[...], oh_ref[g]) for g in range(3)])

    return pl.pallas_call(body, name=name, out_shape=jax.ShapeDtypeStruct((DSWA_HEADS, REL_BUCKETS), F32),
                          compiler_params=_cparams())(dbias, onehot)


def _rows(start, dil):
    if dil == 1:
        return pl.ds(pl.multiple_of(start, DSWA_HALF), DSWA_HALF)
    return pl.ds(start, DSWA_HALF, stride=dil)


def _attn_blocks(it, s, dil):
    half = DSWA_HALF
    nbs = s // half // dil
    ii = lax.broadcasted_iota(jnp.int32, (half, 3 * half), 0)
    jj = lax.broadcasted_iota(jnp.int32, (half, 3 * half), 1)
    band = jnp.abs(jj - half - ii) <= half
    out = []
    for u in range(DSWA_UNROLL):
        blk = it * DSWA_UNROLL + u
        r, b = blk // nbs, blk % nbs
        own = r + dil * half * b
        prev = own - jnp.where(b > 0, dil * half, 0)
        nxt = own + jnp.where(b < nbs - 1, dil * half, 0)
        ok = band & ((jj >= half) | (b > 0)) & ((jj < 2 * half) | (b < nbs - 1))
        out.append(((prev, own, nxt), ok))
    return out


def _attn_chains(q_ref, k_ref, v_ref, blocks, dil):
    lane = lax.broadcasted_iota(jnp.int32, (DSWA_HALF, 2 * DSWA_E), 1)
    qm, kw, vw, valid, hmask = [], [], [], [], []
    for (prev, own, nxt), ok in blocks:
        q = q_ref[_rows(own, dil), :].astype(BF16)
        k = jnp.concatenate([k_ref[_rows(st, dil), :] for st in (prev, own, nxt)], axis=0).astype(BF16)
        v = jnp.concatenate([v_ref[_rows(st, dil), :] for st in (prev, own, nxt)], axis=0).astype(BF16)
        for hd in range(2):
            mine = (lane < DSWA_E) if hd == 0 else (lane >= DSWA_E)
            qm.append(jnp.where(mine, q, jnp.zeros_like(q)))
            kw.append(k)
            vw.append(v)
            valid.append(ok)
            hmask.append(mine)
    return _V(qm), _V(kw), _V(vw), _V(valid), _V(hmask)


def _per_group(pr, fn):
    for gi, dil in enumerate(DSWA_DILS):
        pl.when(pr // DSWA_PG == gi)(functools.partial(fn, dil))


_vmax, _vlog = _lift(jnp.max), _lift(jnp.log)


def _dswa_attn_fwd(qkv, bias, *, name, comm=None):
    s = qkv.shape[0]
    half, e = DSWA_HALF, DSWA_E
    npair = DSWA_HEADS // 2

    def body(*refs):
        pr = pl.program_id(0)
        (q_ref, k_ref, v_ref, bias_ref), (o_ref, lse_ref), _ = _comm_hooks(
            comm, refs, 4, 2, 0, pr == 0, pr == npair // 2, pr == npair - 1)
        bias_v = _V([bias_ref[0], bias_ref[1]] * DSWA_UNROLL)

        def run(dil):
            def step(it, carry):
                blocks = _attn_blocks(it, s, dil)
                qm, kw, vw, valid, hmask = _attn_chains(q_ref, k_ref, v_ref, blocks, dil)
                sc = _vwhere(valid, _vbdot_nt(qm, kw) * (e ** -0.5) + bias_v, NEG_INF)
                m = _vmax(sc, axis=-1, keepdims=True)
                p = _vexp(sc - m)
                l = _vsum(p, axis=-1, keepdims=True)
                o = _vbdot(p * (1.0 / l), vw)
                lse = m + _vlog(l)
                for u, ((_, own, _), _) in enumerate(blocks):
                    is_a = hmask.xs[2 * u]
                    o_ref[_rows(own, dil), :] = jnp.where(is_a, o.xs[2 * u], o.xs[2 * u + 1])
                    lse_ref[_rows(own, dil), :] = jnp.where(is_a, lse.xs[2 * u], lse.xs[2 * u + 1])
                return carry

            lax.fori_loop(0, s // half // DSWA_UNROLL, step, 0)

        _per_group(pr, run)

    col = lambda t: pl.BlockSpec((s, 2 * e), lambda p: (0, t * npair + p))
    pair = pl.BlockSpec((s, 2 * e), lambda p: (0, p))
    c_in, c_out, c_shape, c_scr = _comm_specs(comm)
    res = pl.pallas_call(
        body, name=name, grid=(npair,),
        in_specs=[col(0), col(1), col(2), pl.BlockSpec((2, half, 3 * half), lambda p: (p, 0, 0))] + c_in,
        out_specs=[pair, pair] + c_out,
        out_shape=[jax.ShapeDtypeStruct((s, npair * 2 * e), F32)] * 2 + c_shape,
        scratch_shapes=c_scr,
        compiler_params=_cparams(("arbitrary",)),
    )(qkv, qkv, qkv, bias, *(comm.arrays if comm else []))
    return res[0], res[1], res[2:]


def _dswa_attn_bwd(qkv, bias, lse, do, corr, *, name, comm=None):
    s = qkv.shape[0]
    half, e = DSWA_HALF, DSWA_E
    npair = DSWA_HEADS // 2
    w = 2 * e

    def body(*refs):
        pr = pl.program_id(0)
        (q_ref, k_ref, v_ref, bias_ref, lse_ref, do_ref, corr_ref), (dq_ref, dk_ref, dv_ref, db_ref), _ = _comm_hooks(
            comm, refs, 7, 4, 0, pr == 0, pr == npair // 2, pr == npair - 1)
        bias_v = _V([bias_ref[0], bias_ref[1]] * DSWA_UNROLL)
        dk_ref[...] = jnp.zeros_like(dk_ref)
        dv_ref[...] = jnp.zeros_like(dv_ref)

        def run(dil):
            def step(it, dbias):
                blocks = _attn_blocks(it, s, dil)
                qm, kw, vw, valid, hmask = _attn_chains(q_ref, k_ref, v_ref, blocks, dil)
                hd = [0, 1] * DSWA_UNROLL
                rows = [_rows(own, dil) for (_, own, _), _ in blocks for _ in range(2)]
                lse_c = _V(lse_ref[rw, :][:, h * e:h * e + 1] for rw, h in zip(rows, hd))
                corr_c = _V(corr_ref[rw, :][:, h * e:h * e + 1] for rw, h in zip(rows, hd))
                dov = _vwhere(hmask, _V(do_ref[rw, :] for rw in rows), 0.0)
                sc = _vbdot_nt(qm, kw) * (e ** -0.5) + bias_v
                p = _vwhere(valid, _vexp(_vwhere(valid, sc, 0.0) - lse_c), 0.0)
                dsc = p * (_vbdot_nt(dov, vw) + corr_c)
                dq = _vbdot(dsc, kw) * (e ** -0.5)
                dkc = _vbdot_tn(dsc, qm) * (e ** -0.5)
                dvc = _vbdot_tn(p, dov)
                for u, (starts, _) in enumerate(blocks):
                    dq_ref[_rows(starts[1], dil), :] = jnp.where(hmask.xs[2 * u], dq.xs[2 * u], dq.xs[2 * u + 1])
                    dk_u = dkc.xs[2 * u] + dkc.xs[2 * u + 1]
                    dv_u = dvc.xs[2 * u] + dvc.xs[2 * u + 1]
                    for t, st in enumerate(starts):
                        dk_ref[_rows(st, dil), :] += dk_u[t * half:(t + 1) * half]
                        dv_ref[_rows(st, dil), :] += dv_u[t * half:(t + 1) * half]
                da, db = dbias
                for u in range(DSWA_UNROLL):
                    da, db = da + dsc.xs[2 * u], db + dsc.xs[2 * u + 1]
                return da, db

            zero = jnp.zeros((half, 3 * half), F32)
            da, db = lax.fori_loop(0, s // half // DSWA_UNROLL, step, (zero, zero))
            db_ref[0] = da
            db_ref[1] = db

        _per_group(pr, run)

    col = lambda t: pl.BlockSpec((s, w), lambda p: (0, t * npair + p))
    ps = pl.BlockSpec((s, w), lambda p: (0, p))
    bs = pl.BlockSpec((2, half, 3 * half), lambda p: (p, 0, 0))
    c_in, c_out, c_shape, c_scr = _comm_specs(comm)
    res = pl.pallas_call(
        body, name=name, grid=(npair,),
        in_specs=[col(0), col(1), col(2), bs, ps, ps, ps] + c_in,
        out_specs=[ps, ps, ps, bs] + c_out,
        out_shape=[jax.ShapeDtypeStruct((s, npair * w), F32)] * 3
        + [jax.ShapeDtypeStruct((DSWA_HEADS, half, 3 * half), F32)] + c_shape,
        scratch_shapes=c_scr,
        compiler_params=_cparams(("arbitrary",)),
    )(qkv, qkv, qkv, bias, lse, do, corr, *(comm.arrays if comm else []))
    return res[0], res[1], res[2], res[3], res[4:]


def _pair_cols(g, j):
    w = 2 * DSWA_E
    return slice((g * DSWA_PG + j) * w, (g * DSWA_PG + j + 1) * w)


def _group_weights(l_ref, j):
    ls = [l_ref[:, _pair_cols(g, j)] for g in range(3)]
    m = jnp.maximum(jnp.maximum(ls[0], ls[1]), ls[2])
    es = [jnp.exp(x - m) for x in ls]
    inv = 1.0 / (es[0] + es[1] + es[2])
    return [x * inv for x in es]


def _dswa_combine_fwd(o, lse, *, name):
    s, wd = o.shape
    tr = _pick(s, (512, 256, 128))

    def body(o_ref, l_ref, c_ref):
        for j in range(DSWA_PG):
            al = _group_weights(l_ref, j)
            for g in range(3):
                c_ref[:, _pair_cols(g, j)] = (o_ref[:, _pair_cols(g, j)] * al[g]).astype(c_ref.dtype)

    row = pl.BlockSpec((tr, wd), lambda i: (i, 0))
    return pl.pallas_call(
        body, name=name, grid=(s // tr,),
        in_specs=[row, row], out_specs=row,
        out_shape=jax.ShapeDtypeStruct(o.shape, BF16),
        compiler_params=_cparams(("parallel",)),
    )(o, lse)


def _dswa_combine_bwd(o, lse, dc, *, name):
    s, wd = o.shape
    tr = _pick(s, (512, 256, 128))

    def body(o_ref, l_ref, dc_ref, do_ref, corr_ref):
        lane = lax.broadcasted_iota(jnp.int32, (tr, 2 * DSWA_E), 1)
        is_a = lane < DSWA_E
        for j in range(DSWA_PG):
            al = _group_weights(l_ref, j)
            tot = jnp.zeros((tr, 2 * DSWA_E), F32)
            for g in range(3):
                cols = _pair_cols(g, j)
                dcv = dc_ref[:, cols]
                do_ref[:, cols] = dcv * al[g]
                prod = dcv * o_ref[:, cols]
                dal = jnp.where(is_a, jnp.sum(jnp.where(is_a, prod, 0.0), axis=-1, keepdims=True),
                                jnp.sum(jnp.where(is_a, 0.0, prod), axis=-1, keepdims=True))
                tot = tot + al[g] * dal
            for g in range(3):
                corr_ref[:, _pair_cols(g, j)] = -al[g] * tot

    row = pl.BlockSpec((tr, wd), lambda i: (i, 0))
    return pl.pallas_call(
        body, name=name, grid=(s // tr,),
        in_specs=[row, row, row], out_specs=[row, row],
        out_shape=[jax.ShapeDtypeStruct(o.shape, F32)] * 2,
        compiler_params=_cparams(("parallel",)),
    )(o, lse, dc)


class _Comm:
    def __init__(self, mode, arrays, kinds=None):
        self.mode, self.arrays, self.kinds = mode, list(arrays), kinds
        self.n = len(self.arrays)

    def out_shapes(self):
        if self.mode == "exchange":
            return [jax.ShapeDtypeStruct(x.shape, x.dtype) for x in self.arrays]
        shapes = []
        for x, kd in zip(self.arrays, self.kinds):
            shp = list(x.shape)
            if kd == "stack":
                shp = [N_DEV] + shp
            else:
                shp[-2 if kd == "rows" else -1] *= N_DEV
            shapes.append(jax.ShapeDtypeStruct(tuple(shp), x.dtype))
        return shapes

    def scratch(self):
        return [pltpu.SemaphoreType.DMA((7 * self.n,)), pltpu.SemaphoreType.DMA((7 * self.n,)),
                pltpu.SemaphoreType.DMA((self.n,))]

    def bind(self, in_refs, out_refs, sems):
        self.x, self.o = in_refs, out_refs
        self.send_sems, self.recv_sems, self.local_sems = sems
        self.pos = (lax.axis_index("x"), lax.axis_index("y"), lax.axis_index("c"))

    def _slot(self, i, px, py, pc):
        p = 4 * px + 2 * py + pc
        kd = self.kinds[i]
        if kd == "stack":
            return self.o[i].at[p]
        nd = len(self.x[i].shape)
        ax = nd - 2 if kd == "rows" else nd - 1
        size = self.x[i].shape[ax]
        idx = tuple(pl.ds(p * size, size) if a == ax else slice(None) for a in range(nd))
        return self.o[i].at[idx]

    def _gcopy(self, i, k, block, to, src=None):
        return pltpu.make_async_remote_copy(
            src_ref=self._slot(i, *block) if src is None else src, dst_ref=self._slot(i, *block),
            send_sem=self.send_sems.at[7 * i + k], recv_sem=self.recv_sems.at[7 * i + k],
            device_id=to, device_id_type=pl.DeviceIdType.MESH)

    def _chips(self):
        mx, my, _ = self.pos
        return [(1 - mx, my), (mx, 1 - my), (1 - mx, 1 - my)]

    def _xcopies(self):
        mx, my, mc = self.pos
        me = 4 * mx + 2 * my + mc
        copies = []
        for k in range(1, N_DEV):
            px = 1 - mx if (k >> 2) & 1 else mx
            py = 1 - my if (k >> 1) & 1 else my
            pc = 1 - mc if k & 1 else mc
            for i in range(self.n):
                copies.append(pltpu.make_async_remote_copy(
                    src_ref=self.x[i].at[4 * px + 2 * py + pc], dst_ref=self.o[i].at[me],
                    send_sem=self.send_sems.at[7 * i + k - 1], recv_sem=self.recv_sems.at[7 * i + k - 1],
                    device_id=(px, py, pc), device_id_type=pl.DeviceIdType.MESH))
        return copies

    def _local(self):
        mx, my, mc = self.pos
        if self.mode == "exchange":
            me = 4 * mx + 2 * my + mc
            return [pltpu.make_async_copy(self.x[i].at[me], self.o[i].at[me], self.local_sems.at[i]) for i in range(self.n)]
        return [pltpu.make_async_copy(self.x[i], self._slot(i, mx, my, mc), self.local_sems.at[i]) for i in range(self.n)]

    def _first(self):
        mx, my, mc = self.pos
        me, sibling = (mx, my, mc), (mx, my, 1 - mc)
        first = [self._gcopy(i, 0, me, sibling, src=self.x[i]) for i in range(self.n)]
        first += [self._gcopy(i, 1 + j, me, (*chip, mc), src=self.x[i]) for j, chip in enumerate(self._chips())
                  for i in range(self.n)]
        return first

    def _passed(self):
        mx, my, mc = self.pos
        return [self._gcopy(i, 4 + j, (*chip, mc), (mx, my, 1 - mc)) for j, chip in enumerate(self._chips())
                for i in range(self.n)]

    def start(self):
        for cp in self._local() + (self._xcopies() if self.mode == "exchange" else self._first()):
            cp.start()

    def mid(self):
        if self.mode == "exchange":
            return
        mx, my, mc = self.pos
        passed = self._passed()
        for j, chip in enumerate(self._chips()):
            for i in range(self.n):
                self._gcopy(i, 1 + j, (*chip, mc), (mx, my, mc)).wait_recv()
                passed[j * self.n + i].start()

    def end(self):
        mx, my, mc = self.pos
        if self.mode == "exchange":
            copies = self._xcopies()
            for cp in copies:
                cp.wait_recv()
            for cp in copies:
                cp.wait_send()
        else:
            for i in range(self.n):
                self._gcopy(i, 0, (mx, my, 1 - mc), (mx, my, mc)).wait_recv()
                for j, chip in enumerate(self._chips()):
                    self._gcopy(i, 4 + j, (*chip, 1 - mc), (mx, my, mc)).wait_recv()
            for cp in self._first() + self._passed():
                cp.wait_send()
        for cp in self._local():
            cp.wait()

    def run(self, *, name):
        n = self.n

        def body(*refs):
            self.bind(refs[:n], refs[n:2 * n], refs[2 * n:])
            self.start()
            self.mid()
            self.end()

        anyspec = pl.BlockSpec(memory_space=pl.ANY)
        return pl.pallas_call(body, name=name, in_specs=[anyspec] * n, out_specs=[anyspec] * n,
                              out_shape=self.out_shapes(), scratch_shapes=self.scratch())(*self.arrays)


def _comm_specs(comm):
    if comm is None:
        return [], [], [], []
    anyspec = pl.BlockSpec(memory_space=pl.ANY)
    return [anyspec] * comm.n, [anyspec] * comm.n, comm.out_shapes(), comm.scratch()


def _comm_hooks(comm, refs, n_in, n_out, n_scr, first, mid, last):
    if comm is None:
        return refs[:n_in], refs[n_in:n_in + n_out], refs[n_in + n_out:]
    c = comm.n
    ins, cin = refs[:n_in], refs[n_in:n_in + c]
    outs, cout = refs[n_in + c:n_in + c + n_out], refs[n_in + c + n_out:n_in + 2 * c + n_out]
    scr, sems = refs[n_in + 2 * c + n_out:n_in + 2 * c + n_out + n_scr], refs[n_in + 2 * c + n_out + n_scr:]
    comm.bind(cin, cout, sems)
    pl.when(first)(comm.start)
    pl.when(mid)(comm.mid)
    pl.when(last)(comm.end)
    return ins, outs, scr


def _adamw_update(g, w, m, v):
    mn = ADAM_B1 * m + (1.0 - ADAM_B1) * g
    vn = ADAM_B2 * v + (1.0 - ADAM_B2) * (g * g)
    m_hat = mn / (1.0 - ADAM_B1 ** ADAM_STEP)
    v_hat = vn / (1.0 - ADAM_B2 ** ADAM_STEP)
    return -ADAM_LR * (m_hat / (jnp.sqrt(v_hat) + ADAM_EPS) + ADAM_WD * w), mn, vn


def _adamw_layers(recvs, w, m, v, *, name):
    nl, ks, ns = w.shape
    tr = _pick(ks, (64, 48))

    def body(*refs):
        rv_refs = refs[:nl]
        w_ref, m_ref, v_ref, g_ref, d_ref, nm_ref, nv_ref = refs[nl:]
        for l in range(nl):
            g = rv_refs[l][0].astype(F32)
            for q in range(1, N_DEV):
                g = g + rv_refs[l][q].astype(F32)
            delta, mn, vn = _adamw_update(g, w_ref[l], m_ref[l], v_ref[l])
            g_ref[l] = g
            d_ref[l] = delta
            nm_ref[l] = mn
            nv_ref[l] = vn

    row = pl.BlockSpec((nl, tr, ns), lambda i: (0, i, 0))
    return pl.pallas_call(
        body, name=name, grid=(ks // tr,),
        in_specs=[pl.BlockSpec((N_DEV, tr, ns), lambda i: (0, i, 0))] * nl + [row] * 3,
        out_specs=[row] * 4,
        out_shape=[jax.ShapeDtypeStruct((nl, ks, ns), F32)] * 4,
        compiler_params=_cparams(("parallel",)),
    )(*recvs, w, m, v)


def _adamw_reduce(recv, w, m, v, *, name):
    r, c = w.shape
    tr = _pick(r, (128, 64, 8))

    def body(rv_ref, w_ref, m_ref, v_ref, g_ref, d_ref, nm_ref, nv_ref):
        g = rv_ref[0]
        for q in range(1, N_DEV):
            g = g + rv_ref[q]
        delta, mn, vn = _adamw_update(g, w_ref[...], m_ref[...], v_ref[...])
        g_ref[...] = g
        d_ref[...] = delta
        nm_ref[...] = mn
        nv_ref[...] = vn

    row = pl.BlockSpec((tr, c), lambda i: (i, 0))
    return pl.pallas_call(
        body, name=name, grid=(r // tr,),
        in_specs=[pl.BlockSpec((N_DEV, tr, c), lambda i: (0, i, 0)), row, row, row],
        out_specs=[row] * 4,
        out_shape=[jax.ShapeDtypeStruct((r, c), F32)] * 4,
        compiler_params=_cparams(("parallel",)),
    )(recv, w, m, v)


_BIG = ("gdn_w_in", "gdn_w_out", "dswa_w_in", "dswa_w_out", "mlp_w1", "mlp_w2")
_SMALL = ("gdn_conv_w", "norm_mix", "norm_mlp", "norm_final", "rel_bias", "gdn_a_log", "gdn_dt_bias", "gdn_norm_w")
_ORDER = ("norm_mix", "norm_mlp", "norm_final", "rel_bias", "gdn_w_in", "gdn_conv_w", "gdn_a_log", "gdn_dt_bias",
          "gdn_norm_w", "gdn_w_out", "dswa_w_in", "dswa_w_out", "mlp_w1", "mlp_w2")
_KIND = dict(gdn_w_in="stack", gdn_w_out="rows", dswa_w_in="stack", dswa_w_out="rows", mlp_w1="cols", mlp_w2="rows")


def _pack_rows(arrs, align):
    rows, counts = [], []
    for a in arrs:
        flat = a.reshape(-1)
        n = -(-flat.shape[0] // D_MODEL)
        flat = jnp.pad(flat, (0, n * D_MODEL - flat.shape[0]))
        rows.append(flat.reshape(n, D_MODEL))
        counts.append(n)
    out = jnp.concatenate(rows, axis=0)
    total = -(-out.shape[0] // align) * align
    return jnp.pad(out, ((0, total - out.shape[0]), (0, 0))), counts


def _unpack_rows(slab, shapes):
    outs, r = [], 0
    for shp in shapes:
        size = int(np.prod(shp))
        n = -(-size // D_MODEL)
        outs.append(slab[r:r + n].reshape(-1)[:size].reshape(shp))
        r += n
    return outs


def _col_shards(full, nshard):
    lead = full.shape[:-1]
    n = full.shape[-1] // nshard
    t = full.reshape(lead + (nshard, n))
    return jnp.moveaxis(t, -2, 0)


def _from_col_shards(g):
    t = jnp.moveaxis(g, 0, -2)
    return t.reshape(t.shape[:-2] + (t.shape[-2] * t.shape[-1],))


def kernel(x, norm_mix, norm_mlp, norm_final, rel_bias, gdn_w_in, gdn_conv_w, gdn_a_log, gdn_dt_bias, gdn_norm_w, gdn_w_out, dswa_w_in, dswa_w_out, mlp_w1, mlp_w2, loss_target, m_norm_mix, m_norm_mlp, m_norm_final, m_rel_bias, m_gdn_w_in, m_gdn_conv_w, m_gdn_a_log, m_gdn_dt_bias, m_gdn_norm_w, m_gdn_w_out, m_dswa_w_in, m_dswa_w_out, m_mlp_w1, m_mlp_w2, v_norm_mix, v_norm_mlp, v_norm_final, v_rel_bias, v_gdn_w_in, v_gdn_conv_w, v_gdn_a_log, v_gdn_dt_bias, v_gdn_norm_w, v_gdn_w_out, v_dswa_w_in, v_dswa_w_out, v_mlp_w1, v_mlp_w2):
    params = dict(norm_mix=norm_mix, norm_mlp=norm_mlp, norm_final=norm_final, rel_bias=rel_bias,
                  gdn_w_in=gdn_w_in, gdn_conv_w=gdn_conv_w, gdn_a_log=gdn_a_log, gdn_dt_bias=gdn_dt_bias,
                  gdn_norm_w=gdn_norm_w, gdn_w_out=gdn_w_out, dswa_w_in=dswa_w_in, dswa_w_out=dswa_w_out,
                  mlp_w1=mlp_w1, mlp_w2=mlp_w2)
    mom_m = dict(norm_mix=m_norm_mix, norm_mlp=m_norm_mlp, norm_final=m_norm_final, rel_bias=m_rel_bias,
                 gdn_w_in=m_gdn_w_in, gdn_conv_w=m_gdn_conv_w, gdn_a_log=m_gdn_a_log, gdn_dt_bias=m_gdn_dt_bias,
                 gdn_norm_w=m_gdn_norm_w, gdn_w_out=m_gdn_w_out, dswa_w_in=m_dswa_w_in, dswa_w_out=m_dswa_w_out,
                 mlp_w1=m_mlp_w1, mlp_w2=m_mlp_w2)
    mom_v = dict(norm_mix=v_norm_mix, norm_mlp=v_norm_mlp, norm_final=v_norm_final, rel_bias=v_rel_bias,
                 gdn_w_in=v_gdn_w_in, gdn_conv_w=v_gdn_conv_w, gdn_a_log=v_gdn_a_log, gdn_dt_bias=v_gdn_dt_bias,
                 gdn_norm_w=v_gdn_norm_w, gdn_w_out=v_gdn_w_out, dswa_w_in=v_dswa_w_in, dswa_w_out=v_dswa_w_out,
                 mlp_w1=v_mlp_w1, mlp_w2=v_mlp_w2)
    xs = x[0]
    target = loss_target[0]
    dist = _Dist(params)
    conv_tail, _ = _pack_rows([gdn_conv_w], 8)
    (conv_g,) = dist.put("start", dist.gather_comm("start", extra=[(conv_tail, "stack")]).run(name="ag_start"))
    conv_parts = [_unpack_rows(conv_g[dev], [gdn_conv_w.shape])[0] for dev in range(N_DEV)]
    conv_full = _from_col_shards(jnp.stack(conv_parts))[:, :, 0, :]

    loss_part, dcur, g_big, rep, g_conv = _local_step(
        xs, target, dict(norm_mix=norm_mix, norm_mlp=norm_mlp, norm_final=norm_final, rel_bias=rel_bias,
                         gdn_a_log=gdn_a_log, gdn_dt_bias=gdn_dt_bias, gdn_norm_w=gdn_norm_w), dist.full, conv_full, dist)
    loss = lax.psum(loss_part[0, 0], ("x", "y", "c"))
    grad_x = dcur[None]

    conv_dev = _col_shards(jnp.stack(g_conv)[:, :, None, :], N_DEV)
    small_send = jnp.stack([_pack_rows([conv_dev[dev]] + [rep[n] for n in _SMALL[1:]], 8)[0] for dev in range(N_DEV)])
    (small_recv,) = dist.got("end", dist.send_comm("end", g_big, extra=[small_send]).run(name="grad_exchange"))

    outs = {}
    for n in _BIG:
        recvs = [dist.recv[(n, l)] for l in range(params[n].shape[0])]
        res = _adamw_layers(recvs, params[n], mom_m[n], mom_v[n], name=f"adamw_{n}")
        for tag, t in zip(("grad", "delta", "new_m", "new_v"), res):
            outs[(tag, n)] = t
    w_slab, _ = _pack_rows([params[n] for n in _SMALL], 8)
    m_slab, _ = _pack_rows([mom_m[n] for n in _SMALL], 8)
    v_slab, _ = _pack_rows([mom_v[n] for n in _SMALL], 8)
    small = _adamw_reduce(small_recv, w_slab, m_slab, v_slab, name="adamw_small")
    shapes = [params[n].shape for n in _SMALL]
    for tag, slab in zip(("grad", "delta", "new_m", "new_v"), small):
        for n, t in zip(_SMALL, _unpack_rows(slab, shapes)):
            outs[(tag, n)] = t
    result = [loss, grad_x]
    for tag in ("grad", "delta", "new_m", "new_v"):
        result += [outs[(tag, n)] for n in _ORDER]
    return tuple(result)


_GATHER = {
    "start": (("gdn_w_in", 0), ("gdn_w_out", 0), ("mlp_w1", 0), ("mlp_w2", 0)),
    "chunk_fwd0": (("dswa_w_in", 0), ("dswa_w_out", 0), ("mlp_w1", 1), ("mlp_w2", 1), ("gdn_w_in", 1), ("gdn_w_out", 1)),
    "attn_fwd1": (("mlp_w1", 2), ("mlp_w2", 2)),
    "chunk_fwd2": (("dswa_w_in", 1), ("dswa_w_out", 1), ("mlp_w1", 3), ("mlp_w2", 3)),
}
_SEND = {
    "attn_bwd3": (("mlp_w1", 3), ("mlp_w2", 3)),
    "chunk_bwd2": (("dswa_w_in", 1), ("dswa_w_out", 1), ("mlp_w1", 2), ("mlp_w2", 2)),
    "attn_bwd1": (("gdn_w_in", 1), ("gdn_w_out", 1)),
    "chunk_bwd0": (("mlp_w1", 1), ("mlp_w2", 1), ("dswa_w_in", 0), ("dswa_w_out", 0), ("mlp_w1", 0), ("mlp_w2", 0)),
    "end": (("gdn_w_in", 0), ("gdn_w_out", 0)),
}


class _Dist:
    def __init__(self, params):
        self.shards = {n: params[n].astype(BF16) for n in _BIG}
        self.full = {n: [None] * params[n].shape[0] for n in _BIG}
        self.recv = {}

    def gather_comm(self, tag, extra=()):
        if tag not in _GATHER:
            return None
        arrays = [self.shards[n][l] for n, l in _GATHER[tag]] + [a for a, _ in extra]
        return _Comm("gather", arrays, [_KIND[n] for n, _ in _GATHER[tag]] + [k for _, k in extra])

    def put(self, tag, outs):
        for (n, l), t in zip(_GATHER.get(tag, ()), outs):
            self.full[n][l] = _from_col_shards(t) if _KIND[n] == "stack" else t
        return outs[len(_GATHER.get(tag, ())):]

    def send_comm(self, tag, g_big, extra=()):
        if tag not in _SEND:
            return None
        arrays = [_col_shards(g_big[n][l], N_DEV) if _KIND[n] == "stack" else g_big[n][l] for n, l in _SEND[tag]]
        return _Comm("exchange", arrays + list(extra))

    def got(self, tag, outs):
        for item, t in zip(_SEND.get(tag, ()), outs):
            self.recv[item] = t
        return outs[len(_SEND.get(tag, ())):]


def _local_step(xs, target, sp, full, conv_full, dist=None):
    s = xs.shape[0]
    norm_mix, norm_mlp, norm_final = sp["norm_mix"], sp["norm_mlp"], sp["norm_final"]
    gdn_a_log, gdn_dt_bias, gdn_norm_w = sp["gdn_a_log"], sp["gdn_dt_bias"], sp["gdn_norm_w"]
    onehot = _bucket_onehot()
    table_t = sp["rel_bias"].T
    bias = _dswa_bias(table_t, onehot, name="dswa_bias").reshape(DSWA_HEADS, DSWA_HALF, 3 * DSWA_HALF)

    saved = []
    cur = xs
    for i in range(DEPTH):
        j = i // 2
        sv = dict(x_in=cur)
        h = _rms_fwd(cur, norm_mix[i], name=f"rms_mix_fwd{i}")
        sv["h"] = h
        if i % 2 == 0:
            w_in = full["gdn_w_in"][j]
            proj = _mm(h, w_in[:, :GDN_MAIN], name=f"gdn_proj{i}")
            ab = _mm(h, w_in[:, GDN_MAIN:], name=f"gdn_proj_ab{i}")
            qkvn = _gdn_pre_fwd(proj, conv_full[j], name=f"gdn_pre_fwd{i}")
            g_all, beta_all = _gdn_gate_fwd(ab[:, :2 * GDN_HEADS], ab[:, 2 * GDN_HEADS:], gdn_a_log[j], gdn_dt_bias[j],
                                            name=f"gdn_gate_fwd{i}")
            gshape = (2, GDN_HEADS, s // GDN_CHUNK, 1, GDN_CHUNK)
            g_row = g_all.T.reshape(gshape)
            b_row = beta_all.T.reshape(gshape)
            o, states, got = _gdn_chunk_fwd(qkvn, g_row, b_row, name=f"gdn_chunk_fwd{i}",
                                            comm=dist and dist.gather_comm(f"chunk_fwd{i}"))
            if dist:
                dist.put(f"chunk_fwd{i}", got)
            act = _gdn_post_fwd(o, proj, gdn_norm_w[j], name=f"gdn_post_fwd{i}")
            sv.update(proj=proj, ab=ab, qkvn=qkvn, g_row=g_row, b_row=b_row, o=o, states=states, act=act)
            w_out = full["gdn_w_out"][j]
        else:
            w_in = full["dswa_w_in"][j]
            qkv = _mm(h, w_in, name=f"dswa_proj{i}")
            o_n, lse_n, got = _dswa_attn_fwd(qkv, bias, name=f"dswa_attn_fwd{i}",
                                             comm=dist and dist.gather_comm(f"attn_fwd{i}"))
            if dist:
                dist.put(f"attn_fwd{i}", got)
            act = _dswa_combine_fwd(o_n, lse_n, name=f"dswa_comb_fwd{i}")
            sv.update(qkv=qkv, o_n=o_n, lse_n=lse_n, act=act)
            w_out = full["dswa_w_out"][j]
        cur = _mm(act, w_out, name=f"mix_out{i}", epilogue=lambda acc, r: (acc + r,), extras=(cur,))
        sv["x_mid"] = cur
        h2 = _rms_fwd(cur, norm_mlp[i], name=f"rms_mlp_fwd{i}")
        u, a = _mm(h2, full["mlp_w1"][i], name=f"mlp_up{i}", out_dtypes=(BF16, BF16),
                   epilogue=lambda acc: (acc, jnp.square(jnp.maximum(acc, 0.0))))
        cur = _mm(a, full["mlp_w2"][i], name=f"mlp_down{i}", epilogue=lambda acc, r: (acc + r,), extras=(cur,))
        sv.update(h2=h2, u=u, a=a)
        saved.append(sv)

    loss_part, dcur, dcur_b, dg_final = _loss_head(cur, norm_final, target, name="loss_head")

    g_norm_mix, g_norm_mlp = [None] * DEPTH, [None] * DEPTH
    g_big = {n: [None] * len(full[n]) for n in _BIG}
    g_conv, g_alog, g_dt, g_nw = [None] * 2, [None] * 2, [None] * 2, [None] * 2
    d_table_t = jnp.zeros((DSWA_HEADS, REL_BUCKETS), F32)
    for i in reversed(range(DEPTH)):
        j = i // 2
        sv = saved[i]
        w1, w2 = full["mlp_w1"][i], full["mlp_w2"][i]
        du = _mm(dcur_b, w2, tb=True, name=f"mlp_down_bwd{i}", out_dtypes=(BF16,),
                 epilogue=lambda acc, uu: (acc * (2.0 * jnp.maximum(uu.astype(F32), 0.0)),), extras=(sv["u"],))
        g_big["mlp_w2"][i] = _mm(sv["a"], dcur_b, ta=True, name=f"mlp_w2_grad{i}", out_dtypes=(BF16,), shard="rows")
        g_big["mlp_w1"][i] = _mm(sv["h2"], du, ta=True, name=f"mlp_w1_grad{i}", out_dtypes=(BF16,), shard="cols")
        dh2 = _mm(du, w1, tb=True, name=f"mlp_up_bwd{i}")
        dmid, dmid_b, g_norm_mlp[i] = _rms_bwd(sv["x_mid"], norm_mlp[i], dh2, dcur, name=f"rms_mlp_bwd{i}")
        if i % 2 == 0:
            w_in, w_out = full["gdn_w_in"][j], full["gdn_w_out"][j]
            dact = _mm(dmid_b, w_out, tb=True, name=f"mix_out_bwd{i}")
            g_big["gdn_w_out"][j] = _mm(sv["act"], dmid_b, ta=True, name=f"mix_out_grad{i}", out_dtypes=(BF16,),
                                        shard="rows")
            do, dz, g_nw[j] = _gdn_post_bwd(sv["o"], sv["proj"], gdn_norm_w[j], dact, name=f"gdn_post_bwd{i}")
            dqkvn, dg_row, db_row, got = _gdn_chunk_bwd(sv["qkvn"], sv["g_row"], sv["b_row"], sv["states"], do,
                                                        name=f"gdn_chunk_bwd{i}",
                                                        comm=dist and dist.send_comm(f"chunk_bwd{i}", g_big))
            if dist:
                dist.got(f"chunk_bwd{i}", got)
            dproj, g_conv[j] = _gdn_pre_bwd(sv["proj"], conv_full[j], dqkvn, dz, name=f"gdn_pre_bwd{i}")
            nh2 = 2 * GDN_HEADS
            da_, db_, g_alog[j], g_dt[j] = _gdn_gate_bwd(sv["ab"][:, :nh2], sv["ab"][:, nh2:], gdn_a_log[j], gdn_dt_bias[j],
                                                         dg_row.reshape(nh2, s).T, db_row.reshape(nh2, s).T,
                                                         name=f"gdn_gate_bwd{i}")
            dab = jnp.concatenate([da_, db_], axis=1)
            gw_main = _mm(sv["h"], dproj, ta=True, name=f"gdn_w_in_grad{i}", out_dtypes=(BF16,))
            gw_ab = _mm(sv["h"], dab, ta=True, name=f"gdn_w_ab_grad{i}", out_dtypes=(BF16,))
            g_big["gdn_w_in"][j] = jnp.concatenate([gw_main, gw_ab], axis=1)
            dh_ab = _mm(dab, w_in[:, GDN_MAIN:], tb=True, name=f"gdn_proj_ab_bwd{i}")
            dh = _mm(dproj, w_in[:, :GDN_MAIN], tb=True, name=f"gdn_proj_bwd{i}",
                     epilogue=lambda acc, r: (acc + r,), extras=(dh_ab,))
        else:
            w_in, w_out = full["dswa_w_in"][j], full["dswa_w_out"][j]
            dact = _mm(dmid_b, w_out, tb=True, name=f"mix_out_bwd{i}")
            g_big["dswa_w_out"][j] = _mm(sv["act"], dmid_b, ta=True, name=f"mix_out_grad{i}", out_dtypes=(BF16,),
                                         shard="rows")
            do_n, corr_n = _dswa_combine_bwd(sv["o_n"], sv["lse_n"], dact, name=f"dswa_comb_bwd{i}")
            *dqkv, dbias, got = _dswa_attn_bwd(sv["qkv"], bias, sv["lse_n"], do_n, corr_n, name=f"dswa_attn_bwd{i}",
                                               comm=dist and dist.send_comm(f"attn_bwd{i}", g_big))
            if dist:
                dist.got(f"attn_bwd{i}", got)
            d_table_t = d_table_t + _dswa_dtable(dbias.reshape(DSWA_HEADS, -1), onehot, name=f"dswa_dtable{i}")
            g_big["dswa_w_in"][j] = jnp.concatenate(
                [_mm(sv["h"], dt, ta=True, name=f"dswa_w_in_grad{i}_{t}", out_dtypes=(BF16,)) for t, dt in enumerate(dqkv)],
                axis=1)
            dh = None
            for t, dt in enumerate(dqkv):
                w_t = w_in[:, t * DSWA_WIDTH:(t + 1) * DSWA_WIDTH]
                if dh is None:
                    dh = _mm(dt, w_t, tb=True, name=f"dswa_proj_bwd{i}_{t}")
                else:
                    dh = _mm(dt, w_t, tb=True, name=f"dswa_proj_bwd{i}_{t}", epilogue=lambda acc, r: (acc + r,), extras=(dh,))
        dcur, dcur_b, g_norm_mix[i] = _rms_bwd(sv["x_in"], norm_mix[i], dh, dmid, name=f"rms_mix_bwd{i}")

    rep = dict(norm_mix=jnp.concatenate(g_norm_mix, axis=0), norm_mlp=jnp.concatenate(g_norm_mlp, axis=0),
               norm_final=dg_final.reshape(-1), rel_bias=d_table_t.T,
               gdn_a_log=jnp.stack(g_alog).reshape(gdn_a_log.shape), gdn_dt_bias=jnp.stack(g_dt).reshape(gdn_dt_bias.shape),
               gdn_norm_w=jnp.stack(g_nw).reshape(gdn_norm_w.shape))
    return loss_part, dcur, g_big, rep, g_conv
```

```python
import functools
import math

import jax
import jax.numpy as jnp
import numpy as np
from jax import lax
from jax.experimental import pallas as pl
from jax.experimental.pallas import tpu as pltpu

F32 = jnp.float32
BF16 = jnp.bfloat16
HP = lax.Precision.HIGHEST

N_DEV = 8
D_MODEL = 1024
DEPTH = 4
RMS_EPS = 1e-6
NEG_INF = -1e30

GDN_HEADS = 8
GDN_DK = 128
GDN_CONV = 5
GDN_CHUNK = 128
GDN_QKV = 3 * GDN_HEADS * GDN_DK
GDN_MAIN = GDN_QKV + GDN_HEADS * GDN_DK
GDN_AB = 4 * GDN_HEADS

DSWA_DILS = (1, 4, 16)
DSWA_HG = 6
DSWA_E = 64
DSWA_HEADS = 18
DSWA_WIDTH = DSWA_HEADS * DSWA_E
DSWA_HALF = 64
DSWA_PG = DSWA_HG // 2
DSWA_UNROLL = 4
REL_BUCKETS = 32
REL_MAX_DIST = 1024

ADAM_LR = 0.001
ADAM_B1 = 0.9
ADAM_B2 = 0.999
ADAM_EPS = 1e-08
ADAM_WD = 0.01
ADAM_STEP = 10

VMEM_LIMIT = 56 * 1024 * 1024


def _cparams(sem=None, **kw):
    return pltpu.CompilerParams(dimension_semantics=sem, vmem_limit_bytes=VMEM_LIMIT, **kw)


def _pick(dim, cands):
    for c in cands:
        if dim % c == 0:
            return c
    return dim


def _bdot(a, b):
    return jnp.dot(a.astype(BF16), b.astype(BF16), preferred_element_type=F32)


def _bdot_nt(a, b):
    return lax.dot_general(a.astype(BF16), b.astype(BF16), (((1,), (1,)), ((), ())),
                           preferred_element_type=F32)


def _bdot_tn(a, b):
    return lax.dot_general(a.astype(BF16), b.astype(BF16), (((0,), (0,)), ((), ())),
                           preferred_element_type=F32)


def _hdot(a, b):
    return jnp.dot(a, b, precision=HP, preferred_element_type=F32)


def _hdot_tn(a, b):
    return lax.dot_general(a, b, (((0,), (0,)), ((), ())), precision=HP, preferred_element_type=F32)


def _hdot_nt(a, b):
    return lax.dot_general(a, b, (((1,), (1,)), ((), ())), precision=HP, preferred_element_type=F32)


def _sigmoid(x):
    return 1.0 / (1.0 + jnp.exp(-x))


def _mm(a, b, *, name, ta=False, tb=False, out_dtypes=(F32,), epilogue=None, extras=(),
        tm=None, tn=None, tk=None, shard=None, comm=None):
    if ta:
        kdim, m = a.shape
    else:
        m, kdim = a.shape
    n = b.shape[0] if tb else b.shape[1]
    if shard == "rows":
        tm = m // N_DEV if (m // N_DEV) % 128 == 0 else m
    if shard == "cols":
        tn = n // N_DEV
    tm = tm or _pick(m, (1024, 1152, 512, 384, 256, 128))
    tn = tn or _pick(n, (1024, 1152, 512, 384, 256, 128))
    tk = tk or _pick(kdim, (1024, 1152, 512, 384, 256, 128))
    nk = kdim // tk
    n_out = len(out_dtypes)
    n_ex = len(extras)
    rows_all = shard == "rows" and tm == m

    gi, gj = m // tm, n // tn

    def body(*refs):
        i, j, k = pl.program_id(0), pl.program_id(1), pl.program_id(2)
        inner = (j == 0) & (k == 0)
        ins, out_refs, (acc_ref,) = _comm_hooks(
            comm, refs, 2 + n_ex, n_out, 1, (i == 0) & inner, (i == gi // 2) & inner,
            (i == gi - 1) & (j == gj - 1) & (k == nk - 1))
        a_ref, b_ref, ex_refs = ins[0], ins[1], ins[2:]

        @pl.when(k == 0)
        def _():
            acc_ref[...] = jnp.zeros_like(acc_ref)

        av = a_ref[...].astype(BF16)
        bv = b_ref[...].astype(BF16)
        dims = (((0 if ta else 1,), (1 if tb else 0,)), ((), ()))
        acc_ref[...] += lax.dot_general(av, bv, dims, preferred_element_type=F32)

        @pl.when(k == nk - 1)
        def _():
            acc = acc_ref[...]
            outs = (acc,) if epilogue is None else epilogue(acc, *[r[...] for r in ex_refs])
            for r, o in zip(out_refs, outs):
                if rows_all:
                    for p in range(N_DEV):
                        r[p] = o[p * (m // N_DEV):(p + 1) * (m // N_DEV)].astype(r.dtype)
                else:
                    r[...] = o.astype(r.dtype)

    a_spec = pl.BlockSpec((tk, tm), lambda i, j, k: (k, i)) if ta else pl.BlockSpec((tm, tk), lambda i, j, k: (i, k))
    b_spec = pl.BlockSpec((tn, tk), lambda i, j, k: (j, k)) if tb else pl.BlockSpec((tk, tn), lambda i, j, k: (k, j))
    o_spec = pl.BlockSpec((tm, tn), lambda i, j, k: (i, j))
    out_specs = [o_spec] * n_out
    out_shape = [jax.ShapeDtypeStruct((m, n), dt) for dt in out_dtypes]
    if shard == "rows":
        out_shape = [jax.ShapeDtypeStruct((N_DEV, m // N_DEV, n), out_dtypes[0])]
        out_specs = [pl.BlockSpec((N_DEV, m // N_DEV, tn), lambda i, j, k: (0, 0, j)) if rows_all
                     else pl.BlockSpec((None, tm, tn), lambda i, j, k: (i, 0, j))]
    if shard == "cols":
        out_shape = [jax.ShapeDtypeStruct((N_DEV, m, tn), out_dtypes[0])]
        out_specs = [pl.BlockSpec((None, tm, tn), lambda i, j, k: (j, i, 0))]
    c_in, c_out, c_shape, c_scr = _comm_specs(comm)
    outs = pl.pallas_call(
        body, name=name,
        grid=(gi, gj, nk),
        in_specs=[a_spec, b_spec] + [o_spec] * n_ex + c_in,
        out_specs=out_specs + c_out,
        out_shape=out_shape + c_shape,
        scratch_shapes=[pltpu.VMEM((tm, tn), F32)] + c_scr,
        compiler_params=_cparams(("arbitrary",) * 3 if comm else ("parallel", "parallel", "arbitrary")),
    )(a, b, *extras, *(comm.arrays if comm else []))
    res = outs[0] if n_out == 1 else tuple(outs[:n_out])
    return (res, outs[n_out:]) if comm else res


def _rms_fwd(x, g, *, name):
    s, d = x.shape
    tr = _pick(s, (512, 256, 128))

    def body(x_ref, g_ref, h_ref):
        xv = x_ref[...]
        r = lax.rsqrt(jnp.mean(xv * xv, axis=-1, keepdims=True) + RMS_EPS)
        h_ref[...] = (xv * r * g_ref[...]).astype(h_ref.dtype)

    return pl.pallas_call(
        body, name=name, grid=(s // tr,),
        in_specs=[pl.BlockSpec((tr, d), lambda i: (i, 0)), pl.BlockSpec((1, d), lambda i: (0, 0))],
        out_specs=pl.BlockSpec((tr, d), lambda i: (i, 0)),
        out_shape=jax.ShapeDtypeStruct((s, d), BF16),
        compiler_params=_cparams(("parallel",)),
    )(x, g.reshape(1, d))


def _rms_bwd(x, g, dh, dres, *, name):
    s, d = x.shape
    tr = _pick(s, (512, 256, 128))

    def body(x_ref, g_ref, dh_ref, dres_ref, dx_ref, dxb_ref, dg_ref):
        i = pl.program_id(0)
        xv = x_ref[...]
        r = lax.rsqrt(jnp.mean(xv * xv, axis=-1, keepdims=True) + RMS_EPS)
        xn = xv * r
        dhv = dh_ref[...]
        dn = dhv * g_ref[...]
        dx = dres_ref[...] + r * (dn - xn * jnp.mean(dn * xn, axis=-1, keepdims=True))
        dx_ref[...] = dx
        dxb_ref[...] = dx.astype(dxb_ref.dtype)
        part = jnp.sum(dhv * xn, axis=0, keepdims=True)

        @pl.when(i == 0)
        def _():
            dg_ref[...] = part

        @pl.when(i > 0)
        def _():
            dg_ref[...] += part

    row = pl.BlockSpec((tr, d), lambda i: (i, 0))
    vec = pl.BlockSpec((1, d), lambda i: (0, 0))
    return pl.pallas_call(
        body, name=name, grid=(s // tr,),
        in_specs=[row, vec, row, row], out_specs=[row, row, vec],
        out_shape=[jax.ShapeDtypeStruct((s, d), F32), jax.ShapeDtypeStruct((s, d), BF16),
                   jax.ShapeDtypeStruct((1, d), F32)],
        compiler_params=_cparams(("arbitrary",)),
    )(x, g.reshape(1, d), dh, dres)


def _loss_head(x, g, target, *, name):
    s, d = x.shape
    tr = _pick(s, (512, 256, 128))

    def body(x_ref, g_ref, t_ref, loss_ref, dx_ref, dxb_ref, dg_ref):
        i = pl.program_id(0)
        xv = x_ref[...]
        gv = g_ref[...]
        r = lax.rsqrt(jnp.mean(xv * xv, axis=-1, keepdims=True) + RMS_EPS)
        xn = xv * r
        err = xn * gv - t_ref[...]
        lpart = 0.5 * jnp.sum(jnp.mean(err * err, axis=-1, keepdims=True), axis=0, keepdims=True)
        dy = err * (1.0 / d)
        dn = dy * gv
        dx = r * (dn - xn * jnp.mean(dn * xn, axis=-1, keepdims=True))
        dx_ref[...] = dx
        dxb_ref[...] = dx.astype(dxb_ref.dtype)
        gpart = jnp.sum(dy * xn, axis=0, keepdims=True)

        @pl.when(i == 0)
        def _():
            dg_ref[...] = gpart
            loss_ref[...] = lpart

        @pl.when(i > 0)
        def _():
            dg_ref[...] += gpart
            loss_ref[...] += lpart

    row = pl.BlockSpec((tr, d), lambda i: (i, 0))
    vec = pl.BlockSpec((1, d), lambda i: (0, 0))
    one = pl.BlockSpec((1, 1), lambda i: (0, 0))
    return pl.pallas_call(
        body, name=name, grid=(s // tr,),
        in_specs=[row, vec, row], out_specs=[one, row, row, vec],
        out_shape=[jax.ShapeDtypeStruct((1, 1), F32), jax.ShapeDtypeStruct((s, d), F32),
                   jax.ShapeDtypeStruct((s, d), BF16), jax.ShapeDtypeStruct((1, d), F32)],
        compiler_params=_cparams(("arbitrary",)),
    )(x, g.reshape(1, d), target)


def _shift_rows(x, sft, rows):
    s = x.shape[0]
    if sft == 0:
        return x
    y = pltpu.roll(x, (-sft) % s, 0)
    ok = (rows + sft >= 0) & (rows + sft < s)
    return jnp.where(ok, y, 0.0)


def _gdn_pre_fwd(proj, conv_w, *, name):
    s = proj.shape[0]
    nblk = GDN_QKV // 128
    pad = GDN_CONV // 2

    def body(x_ref, w_ref, o_ref):
        j = pl.program_id(0)
        x = x_ref[...]
        rows = lax.broadcasted_iota(jnp.int32, x.shape, 0)
        c = jnp.zeros_like(x)
        for t in range(GDN_CONV):
            c = c + w_ref[pl.ds(t, 1), :] * _shift_rows(x, t - pad, rows)
        a = c * _sigmoid(c)
        rinv = lax.rsqrt(jnp.sum(a * a, axis=-1, keepdims=True) + 1e-6)
        scale = jnp.where(j < GDN_HEADS, GDN_DK ** -0.5, 1.0)
        o_ref[...] = jnp.where(j >= 2 * GDN_HEADS, a, a * (rinv * scale))

    return pl.pallas_call(
        body, name=name, grid=(nblk,),
        in_specs=[pl.BlockSpec((s, 128), lambda j: (0, j)), pl.BlockSpec((GDN_CONV, 128), lambda j: (0, j))],
        out_specs=pl.BlockSpec((s, 128), lambda j: (0, j)),
        out_shape=jax.ShapeDtypeStruct((s, GDN_QKV), F32),
        compiler_params=_cparams(("parallel",)),
    )(proj, conv_w)


def _gdn_pre_bwd(proj, conv_w, dqkv, dproj, *, name, comm=None):
    s = proj.shape[0]
    nblk = GDN_QKV // 128
    pad = GDN_CONV // 2

    def body(*refs):
        j = pl.program_id(0)
        (x_ref, w_ref, df_ref, dbk_ref, _), (dx_ref, dw_ref), _ = _comm_hooks(
            comm, refs, 5, 2, 0, j == 0, j == nblk // 2, j == nblk - 1)
        x = x_ref[...]
        rows = lax.broadcasted_iota(jnp.int32, x.shape, 0)
        xs = [_shift_rows(x, t - pad, rows) for t in range(GDN_CONV)]
        c = jnp.zeros_like(x)
        for t in range(GDN_CONV):
            c = c + w_ref[pl.ds(t, 1), :] * xs[t]
        sg = _sigmoid(c)
        a = c * sg
        rinv = lax.rsqrt(jnp.sum(a * a, axis=-1, keepdims=True) + 1e-6)
        scale = jnp.where(j < GDN_HEADS, GDN_DK ** -0.5, 1.0)
        dy = df_ref[...] + dbk_ref[...]
        nh = a * rinv
        da_n = (rinv * scale) * (dy - nh * jnp.sum(dy * nh, axis=-1, keepdims=True))
        da = jnp.where(j >= 2 * GDN_HEADS, dy, da_n)
        dc = da * (sg * (1.0 + c * (1.0 - sg)))
        dx = jnp.zeros_like(x)
        for t in range(GDN_CONV):
            dx = dx + w_ref[pl.ds(t, 1), :] * _shift_rows(dc, pad - t, rows)
            dw_ref[pl.ds(t, 1), :] = jnp.sum(dc * xs[t], axis=0, keepdims=True)
        dx_ref[...] = dx.astype(dx_ref.dtype)

    col = pl.BlockSpec((s, 128), lambda j: (0, j))
    wsp = pl.BlockSpec((GDN_CONV, 128), lambda j: (0, j))
    c_in, c_out, c_shape, c_scr = _comm_specs(comm)
    res = pl.pallas_call(
        body, name=name, grid=(nblk,),
        in_specs=[col, wsp, col, col, pl.BlockSpec(memory_space=pl.ANY)] + c_in, out_specs=[col, wsp] + c_out,
        out_shape=[jax.ShapeDtypeStruct(dproj.shape, BF16), jax.ShapeDtypeStruct((GDN_CONV, GDN_QKV), F32)] + c_shape,
        input_output_aliases={4: 0},
        scratch_shapes=c_scr,
        compiler_params=_cparams(("arbitrary",) if comm else ("parallel",)),
    )(proj, conv_w, dqkv[0], dqkv[1], dproj, *(comm.arrays if comm else []))
    return res[0], res[1], res[2:]


def _softplus(x):
    return jnp.maximum(x, 0.0) + jnp.log(1.0 + jnp.exp(-jnp.abs(x)))


def _gdn_gate_fwd(a, b, a_log, dt_bias, *, name):
    s = a.shape[0]
    nh = 2 * GDN_HEADS

    def body(a_ref, b_ref, al_ref, dt_ref, g_ref, be_ref):
        g_ref[...] = -jnp.exp(al_ref[...]) * _softplus(a_ref[...] + dt_ref[...])
        be_ref[...] = _sigmoid(b_ref[...])

    return pl.pallas_call(
        body, name=name,
        out_shape=[jax.ShapeDtypeStruct((s, nh), F32), jax.ShapeDtypeStruct((s, nh), F32)],
        compiler_params=_cparams(),
    )(a, b, a_log.reshape(1, nh), dt_bias.reshape(1, nh))


def _gdn_gate_bwd(a, b, a_log, dt_bias, dg, dbeta, *, name):
    s = a.shape[0]
    nh = 2 * GDN_HEADS

    def body(a_ref, b_ref, al_ref, dt_ref, dg_ref, db_ref, da_ref, dbb_ref, dal_ref, ddt_ref):
        ea = jnp.exp(al_ref[...])
        z = a_ref[...] + dt_ref[...]
        dgv = dg_ref[...]
        dz = dgv * (-ea) * _sigmoid(z)
        dal_ref[...] = jnp.sum(dgv * (-ea) * _softplus(z), axis=0, keepdims=True)
        ddt_ref[...] = jnp.sum(dz, axis=0, keepdims=True)
        sb = _sigmoid(b_ref[...])
        da_ref[...] = dz
        dbb_ref[...] = db_ref[...] * sb * (1.0 - sb)

    return pl.pallas_call(
        body, name=name,
        out_shape=[jax.ShapeDtypeStruct((s, nh), F32), jax.ShapeDtypeStruct((s, nh), F32),
                   jax.ShapeDtypeStruct((1, nh), F32), jax.ShapeDtypeStruct((1, nh), F32)],
        compiler_params=_cparams(),
    )(a, b, a_log.reshape(1, nh), dt_bias.reshape(1, nh), dg, dbeta)


def _chunk_masks(d):
    c = GDN_CHUNK
    ii = lax.broadcasted_iota(jnp.int32, (c, c), 0)
    jj = lax.broadcasted_iota(jnp.int32, (c, c), 1)
    dif = (ii - jj) * (1 - 2 * d)
    mi = dif >= 0
    mit = dif <= 0
    ms = dif > 0
    eye = ii == jj
    bds = [(ii >> sh) == (jj >> sh) for sh in range(3, c.bit_length() - 1)]
    return dict(mi=mi, mit=mit, ms=ms, eye=eye, bds=bds,
                mif=mi.astype(F32), mitf=mit.astype(F32), eyef=eye.astype(F32))


class _V:
    def __init__(self, xs):
        self.xs = tuple(xs)

    def __add__(self, o):
        return _lift(lambda a, b: a + b)(self, o)

    def __radd__(self, o):
        return _lift(lambda a, b: b + a)(self, o)

    def __sub__(self, o):
        return _lift(lambda a, b: a - b)(self, o)

    def __rsub__(self, o):
        return _lift(lambda a, b: b - a)(self, o)

    def __mul__(self, o):
        return _lift(lambda a, b: a * b)(self, o)

    def __rmul__(self, o):
        return _lift(lambda a, b: b * a)(self, o)

    def __and__(self, o):
        return _lift(lambda a, b: a & b)(self, o)

    def __neg__(self):
        return _lift(lambda a: -a)(self)

    def __rtruediv__(self, o):
        return _lift(lambda a, b: b / a)(self, o)


def _lift(f):
    def g(*args, **kw):
        n = next(len(a.xs) for a in args if isinstance(a, _V))
        return _V(f(*[a.xs[i] if isinstance(a, _V) else a for a in args], **kw) for i in range(n))
    return g


_vwhere, _vsum, _vexp, _vnot = _lift(jnp.where), _lift(jnp.sum), _lift(jnp.exp), _lift(jnp.logical_not)
_vhdot, _vhdot_tn = _lift(_bdot), _lift(_bdot_tn)
_vbdot, _vbdot_nt, _vbdot_tn = _lift(_bdot), _lift(_bdot_nt), _lift(_bdot_tn)
_vcat = _lift(lambda a, b: jnp.concatenate([a, b], axis=1))
_vlo = _lift(lambda a, n: a[:, :n])
_vhi = _lift(lambda a, n: a[:, n:])


def _both_masks(n):
    m = [_chunk_masks(d) for d in range(2)]
    mk = {key: _V([m[0][key]] * n + [m[1][key]] * n) for key in m[0] if key != "bds"}
    mk["bds"] = [_V([m[0]["bds"][i]] * n + [m[1]["bds"][i]] * n) for i in range(len(m[0]["bds"]))]
    return mk


def _tri_inv(a, mk):
    eyef = mk["eyef"]
    bds = mk["bds"]
    a8 = _vwhere(bds[0], a, 0.0)
    a2 = _vhdot(a8, a8)
    a4 = _vhdot(a2, a2)
    t = _vhdot(_vhdot(eyef - a8, eyef + a2), eyef + a4)
    for inner, outer in zip(bds, bds[1:] + [None]):
        off = _vnot(inner) if outer is None else (outer & _vnot(inner))
        low = _vwhere(off, a, 0.0)
        t = t - _vhdot(_vhdot(t, low), t)
    return t


def _chunk_prep(q, k, v, g_row, b_row, mk):
    dv = GDN_DK
    g_col = _vsum(mk["eyef"] * g_row, axis=1, keepdims=True)
    b_col = _vsum(mk["eyef"] * b_row, axis=1, keepdims=True)
    gc_col = _vsum(mk["mif"] * g_row, axis=1, keepdims=True)
    gc_row = _vsum(mk["mitf"] * g_col, axis=0, keepdims=True)
    gl = _vsum(g_row, axis=1, keepdims=True)
    decay = _vwhere(mk["mi"], _vexp(_vwhere(mk["mi"], gc_col - gc_row, 0.0)), 0.0)
    eg = _vexp(gc_col)
    e2 = _vexp(gl - gc_col)
    egl = _vexp(gl)
    kb = k * b_col
    pm = _vbdot_nt(kb, k)
    a = _vwhere(mk["ms"], pm * decay, 0.0)
    t = _tri_inv(a, mk)
    sol = _vhdot(t, _vcat(v * b_col, kb * eg))
    u, w = _vlo(sol, dv), _vhi(sol, dv)
    qm = _vbdot_nt(q, k)
    return dict(b_col=b_col, decay=decay, eg=eg, e2=e2, egl=egl, kb=kb, pm=pm, t=t, u=u, w=w,
                qm=qm, intra=qm * decay, qd=q * eg, kd=k * e2)


def _chunk_fwd_step(p, state):
    v_new = p["u"] - _vbdot(p["w"], state)
    o = _vbdot(p["qd"], state) + _vbdot(p["intra"], v_new)
    new_state = state * p["egl"] + _vbdot_tn(p["kd"], v_new)
    return o, new_state


def _chunk_bwd_step(q, k, v, p, mk, state, dso, do):
    dv_dim = GDN_DK
    v_new = p["u"] - _vbdot(p["w"], state)
    dvn = _vbdot_tn(p["intra"], do) + _vbdot(p["kd"], dso)
    dintra = _vbdot_nt(do, v_new)
    dqd = _vbdot_nt(do, state)
    ds = p["egl"] * dso + _vbdot_tn(p["qd"], do) - _vbdot_tn(p["w"], dvn)
    dkd = _vbdot_nt(v_new, dso)
    dgl = _vsum(_vsum(dso * state, axis=1, keepdims=True), axis=0, keepdims=True) * p["egl"]
    dw = -_vbdot_nt(dvn, state)
    drhs = _vhdot_tn(p["t"], _vcat(dvn, dw))
    dru, drw = _vlo(drhs, dv_dim), _vhi(drhs, dv_dim)
    da = -_vwhere(mk["ms"], _vbdot_nt(drhs, _vcat(p["u"], p["w"])), 0.0)
    b_col = p["b_col"]
    dv = dru * b_col
    dbeta = _vsum(dru * v, axis=1, keepdims=True)
    dkb = drw * p["eg"]
    deg = _vsum(drw * p["kb"], axis=1, keepdims=True)
    dp = da * p["decay"]
    ddecay = da * p["pm"]
    dkb = dkb + _vbdot(dp, k)
    dk = _vbdot_tn(dp, p["kb"])
    dqm = dintra * p["decay"]
    ddecay = ddecay + dintra * p["qm"]
    dq = _vbdot(dqm, k)
    dk = dk + _vbdot_tn(dqm, q)
    dd = ddecay * p["decay"]
    dgc_col = _vsum(dd, axis=1, keepdims=True)
    dgc_row = -_vsum(dd, axis=0, keepdims=True)
    dq = dq + dqd * p["eg"]
    deg = deg + _vsum(dqd * q, axis=1, keepdims=True)
    dk = dk + dkd * p["e2"]
    de2 = _vsum(dkd * k, axis=1, keepdims=True) * p["e2"]
    dgl = dgl + _vsum(de2, axis=0, keepdims=True)
    dgc_col = dgc_col - de2 + deg * p["eg"]
    dk = dk + dkb * b_col
    dbeta = dbeta + _vsum(dkb * k, axis=1, keepdims=True)
    dgc_col = dgc_col + _vsum(mk["eyef"] * dgc_row, axis=1, keepdims=True)
    dg_row = _vsum(mk["mif"] * dgc_col, axis=0, keepdims=True) + dgl
    dbeta_row = _vsum(mk["eyef"] * dbeta, axis=0, keepdims=True)
    return dq, dk, dv, dg_row, dbeta_row, ds


def _gdn_chunk_fwd(qkvn, g5, b5, *, name, comm=None):
    s = qkvn.shape[0]
    c = GDN_CHUNK
    nc = s // c
    h_, dk = GDN_HEADS, GDN_DK

    def body(*refs):
        n = pl.program_id(0)
        ins, outs, (st_scr,) = _comm_hooks(comm, refs, 6, 4, 1, n == 0, n == nc // 2, n == nc - 1)
        x_refs, g_refs, b_refs = ins[0:2], ins[2:4], ins[4:6]
        o_refs, st_refs = outs[0:2], outs[2:4]

        @pl.when(n == 0)
        def _():
            st_scr[...] = jnp.zeros_like(st_scr)

        ch = [(d, h) for d in range(2) for h in range(h_)]
        mk = _both_masks(h_)
        q, k, v = (_V(x_refs[d][:, (t * h_ + h) * dk:(t * h_ + h + 1) * dk] for d, h in ch) for t in range(3))
        g, b = (_V(r[d][0, h, 0] for d, h in ch) for r in (g_refs, b_refs))
        state = _V(st_scr[d * h_ + h] for d, h in ch)
        o, new_state = _chunk_fwd_step(_chunk_prep(q, k, v, g, b, mk), state)
        for i, (d, h) in enumerate(ch):
            st_refs[d][h, 0] = state.xs[i]
            st_scr[d * h_ + h] = new_state.xs[i]
            o_refs[d][:, h * dk:(h + 1) * dk] = o.xs[i]

    ce = (lambda n: n, lambda n: nc - 1 - n)
    xs = [pl.BlockSpec((c, 3 * h_ * dk), lambda n, d=d: (ce[d](n), 0)) for d in range(2)]
    gates = [pl.BlockSpec((1, h_, 1, 1, c), lambda n, d=d: (d, 0, ce[d](n), 0, 0)) for d in range(2)]
    os_ = [pl.BlockSpec((c, h_ * dk), lambda n, d=d: (ce[d](n), 0)) for d in range(2)]
    sts = [pl.BlockSpec((h_, 1, dk, dk), lambda n, d=d: (0, ce[d](n), 0, 0)) for d in range(2)]
    c_in, c_out, c_shape, c_scr = _comm_specs(comm)
    res = pl.pallas_call(
        body, name=name, grid=(nc,),
        in_specs=xs + gates + gates + c_in,
        out_specs=os_ + sts + c_out,
        out_shape=[jax.ShapeDtypeStruct((s, h_ * dk), F32)] * 2 + [jax.ShapeDtypeStruct((h_, nc, dk, dk), F32)] * 2 + c_shape,
        scratch_shapes=[pltpu.VMEM((2 * h_, dk, dk), F32)] + c_scr,
        compiler_params=_cparams(("arbitrary",)),
    )(qkvn, qkvn, g5, g5, b5, b5, *(comm.arrays if comm else []))
    return res[0:2], res[2:4], res[4:]


def _gdn_chunk_bwd(qkvn, g5, b5, states, do, *, name, comm=None):
    s = qkvn.shape[0]
    c = GDN_CHUNK
    nc = s // c
    h_, dk = GDN_HEADS, GDN_DK

    def body(*refs):
        i = pl.program_id(0)
        ins, outs, (ds_scr,) = _comm_hooks(comm, refs, 10, 6, 1, i == 0, i == nc // 2, i == nc - 1)
        x_refs, g_refs, b_refs, st_refs, do_refs = ins[0:2], ins[2:4], ins[4:6], ins[6:8], ins[8:10]
        dx_refs, dg_refs, db_refs = outs[0:2], outs[2:4], outs[4:6]

        @pl.when(i == 0)
        def _():
            ds_scr[...] = jnp.zeros_like(ds_scr)

        ch = [(d, h) for d in range(2) for h in range(h_)]
        mk = _both_masks(h_)
        q, k, v = (_V(x_refs[d][:, (t * h_ + h) * dk:(t * h_ + h + 1) * dk] for d, h in ch) for t in range(3))
        g, b = (_V(r[d][0, h, 0] for d, h in ch) for r in (g_refs, b_refs))
        state = _V(st_refs[d][h, 0] for d, h in ch)
        dso = _V(ds_scr[d * h_ + h] for d, h in ch)
        dov = _V(do_refs[d][:, h * dk:(h + 1) * dk] for d, h in ch)
        res = _chunk_bwd_step(q, k, v, _chunk_prep(q, k, v, g, b, mk), mk, state, dso, dov)
        for (d, h), (dq, dkk, dvv, dg_r, db_r, ds) in zip(ch, zip(*[r.xs for r in res])):
            ds_scr[d * h_ + h] = ds
            dg_refs[d][h, 0] = dg_r
            db_refs[d][h, 0] = db_r
            for t, val in enumerate((dq, dkk, dvv)):
                dx_refs[d][:, (t * h_ + h) * dk:(t * h_ + h + 1) * dk] = val

    ce = (lambda i: nc - 1 - i, lambda i: i)
    both = lambda mk_spec: [mk_spec(d) for d in range(2)]
    xs = both(lambda d: pl.BlockSpec((c, 3 * h_ * dk), lambda i: (ce[d](i), 0)))
    gates = both(lambda d: pl.BlockSpec((1, h_, 1, 1, c), lambda i: (d, 0, ce[d](i), 0, 0)))
    sts = both(lambda d: pl.BlockSpec((h_, 1, dk, dk), lambda i: (0, ce[d](i), 0, 0)))
    dos = both(lambda d: pl.BlockSpec((c, h_ * dk), lambda i: (ce[d](i), 0)))
    gouts = both(lambda d: pl.BlockSpec((h_, 1, 1, c), lambda i: (0, ce[d](i), 0, 0)))
    c_in, c_out, c_shape, c_scr = _comm_specs(comm)
    res = pl.pallas_call(
        body, name=name, grid=(nc,),
        in_specs=xs + gates + gates + sts + dos + c_in,
        out_specs=xs + gouts + gouts + c_out,
        out_shape=[jax.ShapeDtypeStruct((s, 3 * h_ * dk), F32)] * 2
        + [jax.ShapeDtypeStruct((h_, nc, 1, c), F32)] * 4 + c_shape,
        scratch_shapes=[pltpu.VMEM((2 * h_, dk, dk), F32)] + c_scr,
        compiler_params=_cparams(("arbitrary",)),
    )(qkvn, qkvn, g5, g5, b5, b5, states[0], states[1], do, do, *(comm.arrays if comm else []))
    return res[0:2], jnp.stack(res[2:4]), jnp.stack(res[4:6]), res[6:]


def _gdn_post_fwd(o, z, norm_w, *, name):
    s = o[0].shape[0]
    h_, dk = GDN_HEADS, GDN_DK

    def body(of_ref, ob_ref, z_ref, w_ref, a_ref):
        ov = of_ref[...] + ob_ref[...]
        zv = z_ref[...]
        r = lax.rsqrt(jnp.mean(ov * ov, axis=-1, keepdims=True) + RMS_EPS)
        a_ref[...] = (ov * r * w_ref[...] * (zv * _sigmoid(zv))).astype(a_ref.dtype)

    col = pl.BlockSpec((s, dk), lambda h: (0, h))
    return pl.pallas_call(
        body, name=name, grid=(h_,),
        in_specs=[col, col, pl.BlockSpec((s, dk), lambda h: (0, 3 * h_ + h)), pl.BlockSpec((1, dk), lambda h: (0, 0))],
        out_specs=col,
        out_shape=jax.ShapeDtypeStruct((s, h_ * dk), BF16),
        compiler_params=_cparams(("parallel",)),
    )(o[0], o[1], z, norm_w.reshape(1, dk))


def _gdn_post_bwd(o, z, norm_w, dact, *, name):
    s = o[0].shape[0]
    h_, dk = GDN_HEADS, GDN_DK

    def body(of_ref, ob_ref, z_ref, w_ref, da_ref, do_ref, dz_ref, dw_ref):
        h = pl.program_id(0)
        ov = of_ref[...] + ob_ref[...]
        zv = z_ref[...]
        wv = w_ref[...]
        dav = da_ref[...]
        r = lax.rsqrt(jnp.mean(ov * ov, axis=-1, keepdims=True) + RMS_EPS)
        nrm = ov * r
        sg = _sigmoid(zv)
        sz = zv * sg
        dn = dav * wv * sz
        do_ref[...] = r * (dn - nrm * jnp.mean(dn * nrm, axis=-1, keepdims=True))
        dz_ref[...] = (dav * nrm * wv * (sg * (1.0 + zv * (1.0 - sg)))).astype(dz_ref.dtype)
        part = jnp.sum(dav * nrm * sz, axis=0, keepdims=True)

        @pl.when(h == 0)
        def _():
            dw_ref[...] = part

        @pl.when(h > 0)
        def _():
            dw_ref[...] += part

    col = pl.BlockSpec((s, dk), lambda h: (0, h))
    vec = pl.BlockSpec((1, dk), lambda h: (0, 0))
    return pl.pallas_call(
        body, name=name, grid=(h_,),
        in_specs=[col, col, pl.BlockSpec((s, dk), lambda h: (0, 3 * h_ + h)), vec, col],
        out_specs=[col, pl.BlockSpec((s, dk), lambda h: (0, 3 * h_ + h)), vec],
        out_shape=[jax.ShapeDtypeStruct((s, h_ * dk), F32), jax.ShapeDtypeStruct((s, GDN_MAIN), BF16),
                   jax.ShapeDtypeStruct((1, dk), F32)],
        compiler_params=_cparams(("arbitrary",)),
    )(o[0], o[1], z, norm_w.reshape(1, dk), dact)


def _rel_bucket(rel):
    nb = REL_BUCKETS // 2
    max_exact = nb // 2
    ret = jnp.where(rel > 0, nb, 0)
    n = jnp.abs(rel)
    nf = jnp.maximum(n, 1).astype(F32)
    large = max_exact + (jnp.log(nf / max_exact) / math.log(REL_MAX_DIST / max_exact)
                         * (nb - max_exact)).astype(jnp.int32)
    large = jnp.minimum(large, nb - 1)
    return ret + jnp.where(n < max_exact, n, large)


def _bucket_onehot():
    half = DSWA_HALF
    outs = []
    for dil in DSWA_DILS:
        rel = (jnp.arange(3 * half)[None, :] - half - jnp.arange(half)[:, None]) * dil
        outs.append(jax.nn.one_hot(_rel_bucket(rel).reshape(-1), REL_BUCKETS, dtype=F32, axis=0))
    return jnp.stack(outs)


def _head_group_select(vals):
    rows = lax.broadcasted_iota(jnp.int32, vals[0].shape, 0)
    return jnp.where(rows < DSWA_HG, vals[0], jnp.where(rows < 2 * DSWA_HG, vals[1], vals[2]))


def _dswa_bias(table_t, onehot, *, name):
    p = onehot.shape[-1]

    def body(t_ref, oh_ref, b_ref):
        b_ref[...] = _head_group_select([_hdot(t_ref[...], oh_ref[g]) for g in range(3)])

    return pl.pallas_call(body, name=name, out_shape=jax.ShapeDtypeStruct((DSWA_HEADS, p), F32),
                          compiler_params=_cparams())(table_t, onehot)


def _dswa_dtable(dbias, onehot, *, name):
    def body(d_ref, oh_ref, t_ref):
        t_ref[...] = _head_group_select([_hdot_nt(d_ref[...], oh_ref[g]) for g in range(3)])

    return pl.pallas_call(body, name=name, out_shape=jax.ShapeDtypeStruct((DSWA_HEADS, REL_BUCKETS), F32),
                          compiler_params=_cparams())(dbias, onehot)


def _rows(start, dil):
    if dil == 1:
        return pl.ds(pl.multiple_of(start, DSWA_HALF), DSWA_HALF)
    return pl.ds(start, DSWA_HALF, stride=dil)


def _attn_blocks(it, s, dil):
    half = DSWA_HALF
    nbs = s // half // dil
    ii = lax.broadcasted_iota(jnp.int32, (half, 3 * half), 0)
    jj = lax.broadcasted_iota(jnp.int32, (half, 3 * half), 1)
    band = jnp.abs(jj - half - ii) <= half
    out = []
    for u in range(DSWA_UNROLL):
        blk = it * DSWA_UNROLL + u
        r, b = blk // nbs, blk % nbs
        own = r + dil * half * b
        prev = own - jnp.where(b > 0, dil * half, 0)
        nxt = own + jnp.where(b < nbs - 1, dil * half, 0)
        ok = band & ((jj >= half) | (b > 0)) & ((jj < 2 * half) | (b < nbs - 1))
        out.append(((prev, own, nxt), ok))
    return out


def _attn_chains(q_ref, k_ref, v_ref, blocks, dil):
    lane = lax.broadcasted_iota(jnp.int32, (DSWA_HALF, 2 * DSWA_E), 1)
    qm, kw, vw, valid, hmask = [], [], [], [], []
    for (prev, own, nxt), ok in blocks:
        q = q_ref[_rows(own, dil), :].astype(BF16)
        k = jnp.concatenate([k_ref[_rows(st, dil), :] for st in (prev, own, nxt)], axis=0).astype(BF16)
        v = jnp.concatenate([v_ref[_rows(st, dil), :] for st in (prev, own, nxt)], axis=0).astype(BF16)
        for hd in range(2):
            mine = (lane < DSWA_E) if hd == 0 else (lane >= DSWA_E)
            qm.append(jnp.where(mine, q, jnp.zeros_like(q)))
            kw.append(k)
            vw.append(v)
            valid.append(ok)
            hmask.append(mine)
    return _V(qm), _V(kw), _V(vw), _V(valid), _V(hmask)


def _per_group(pr, fn):
    for gi, dil in enumerate(DSWA_DILS):
        pl.when(pr // DSWA_PG == gi)(functools.partial(fn, dil))


_vmax, _vlog = _lift(jnp.max), _lift(jnp.log)


def _dswa_attn_fwd(qkv, bias, *, name, comm=None):
    s = qkv.shape[0]
    half, e = DSWA_HALF, DSWA_E
    npair = DSWA_HEADS // 2

    def body(*refs):
        pr = pl.program_id(0)
        (q_ref, k_ref, v_ref, bias_ref), (o_ref, lse_ref), _ = _comm_hooks(
            comm, refs, 4, 2, 0, pr == 0, pr == npair // 2, pr == npair - 1)
        bias_v = _V([bias_ref[0], bias_ref[1]] * DSWA_UNROLL)

        def run(dil):
            def step(it, carry):
                blocks = _attn_blocks(it, s, dil)
                qm, kw, vw, valid, hmask = _attn_chains(q_ref, k_ref, v_ref, blocks, dil)
                sc = _vwhere(valid, _vbdot_nt(qm, kw) * (e ** -0.5) + bias_v, NEG_INF)
                m = _vmax(sc, axis=-1, keepdims=True)
                p = _vexp(sc - m)
                l = _vsum(p, axis=-1, keepdims=True)
                o = _vbdot(p * (1.0 / l), vw)
                lse = m + _vlog(l)
                for u, ((_, own, _), _) in enumerate(blocks):
                    is_a = hmask.xs[2 * u]
                    o_ref[_rows(own, dil), :] = jnp.where(is_a, o.xs[2 * u], o.xs[2 * u + 1])
                    lse_ref[_rows(own, dil), :] = jnp.where(is_a, lse.xs[2 * u], lse.xs[2 * u + 1])
                return carry

            lax.fori_loop(0, s // half // DSWA_UNROLL, step, 0)

        _per_group(pr, run)

    col = lambda t: pl.BlockSpec((s, 2 * e), lambda p: (0, t * npair + p))
    pair = pl.BlockSpec((s, 2 * e), lambda p: (0, p))
    c_in, c_out, c_shape, c_scr = _comm_specs(comm)
    res = pl.pallas_call(
        body, name=name, grid=(npair,),
        in_specs=[col(0), col(1), col(2), pl.BlockSpec((2, half, 3 * half), lambda p: (p, 0, 0))] + c_in,
        out_specs=[pair, pair] + c_out,
        out_shape=[jax.ShapeDtypeStruct((s, npair * 2 * e), F32)] * 2 + c_shape,
        scratch_shapes=c_scr,
        compiler_params=_cparams(("arbitrary",)),
    )(qkv, qkv, qkv, bias, *(comm.arrays if comm else []))
    return res[0], res[1], res[2:]


def _dswa_attn_bwd(qkv, bias, lse, do, corr, *, name, comm=None):
    s = qkv.shape[0]
    half, e = DSWA_HALF, DSWA_E
    npair = DSWA_HEADS // 2
    w = 2 * e

    def body(*refs):
        pr = pl.program_id(0)
        (q_ref, k_ref, v_ref, bias_ref, lse_ref, do_ref, corr_ref), (dq_ref, dk_ref, dv_ref, db_ref), _ = _comm_hooks(
            comm, refs, 7, 4, 0, pr == 0, pr == npair // 2, pr == npair - 1)
        bias_v = _V([bias_ref[0], bias_ref[1]] * DSWA_UNROLL)
        dk_ref[...] = jnp.zeros_like(dk_ref)
        dv_ref[...] = jnp.zeros_like(dv_ref)

        def run(dil):
            def step(it, dbias):
                blocks = _attn_blocks(it, s, dil)
                qm, kw, vw, valid, hmask = _attn_chains(q_ref, k_ref, v_ref, blocks, dil)
                hd = [0, 1] * DSWA_UNROLL
                rows = [_rows(own, dil) for (_, own, _), _ in blocks for _ in range(2)]
                lse_c = _V(lse_ref[rw, :][:, h * e:h * e + 1] for rw, h in zip(rows, hd))
                corr_c = _V(corr_ref[rw, :][:, h * e:h * e + 1] for rw, h in zip(rows, hd))
                dov = _vwhere(hmask, _V(do_ref[rw, :] for rw in rows), 0.0)
                sc = _vbdot_nt(qm, kw) * (e ** -0.5) + bias_v
                p = _vwhere(valid, _vexp(_vwhere(valid, sc, 0.0) - lse_c), 0.0)
                dsc = p * (_vbdot_nt(dov, vw) + corr_c)
                dq = _vbdot(dsc, kw) * (e ** -0.5)
                dkc = _vbdot_tn(dsc, qm) * (e ** -0.5)
                dvc = _vbdot_tn(p, dov)
                for u, (starts, _) in enumerate(blocks):
                    dq_ref[_rows(starts[1], dil), :] = jnp.where(hmask.xs[2 * u], dq.xs[2 * u], dq.xs[2 * u + 1])
                    dk_u = dkc.xs[2 * u] + dkc.xs[2 * u + 1]
                    dv_u = dvc.xs[2 * u] + dvc.xs[2 * u + 1]
                    for t, st in enumerate(starts):
                        dk_ref[_rows(st, dil), :] += dk_u[t * half:(t + 1) * half]
                        dv_ref[_rows(st, dil), :] += dv_u[t * half:(t + 1) * half]
                da, db = dbias
                for u in range(DSWA_UNROLL):
                    da, db = da + dsc.xs[2 * u], db + dsc.xs[2 * u + 1]
                return da, db

            zero = jnp.zeros((half, 3 * half), F32)
            da, db = lax.fori_loop(0, s // half // DSWA_UNROLL, step, (zero, zero))
            db_ref[0] = da
            db_ref[1] = db

        _per_group(pr, run)

    col = lambda t: pl.BlockSpec((s, w), lambda p: (0, t * npair + p))
    ps = pl.BlockSpec((s, w), lambda p: (0, p))
    bs = pl.BlockSpec((2, half, 3 * half), lambda p: (p, 0, 0))
    c_in, c_out, c_shape, c_scr = _comm_specs(comm)
    res = pl.pallas_call(
        body, name=name, grid=(npair,),
        in_specs=[col(0), col(1), col(2), bs, ps, ps, ps] + c_in,
        out_specs=[ps, ps, ps, bs] + c_out,
        out_shape=[jax.ShapeDtypeStruct((s, npair * w), F32)] * 3
        + [jax.ShapeDtypeStruct((DSWA_HEADS, half, 3 * half), F32)] + c_shape,
        scratch_shapes=c_scr,
        compiler_params=_cparams(("arbitrary",)),
    )(qkv, qkv, qkv, bias, lse, do, corr, *(comm.arrays if comm else []))
    return res[0], res[1], res[2], res[3], res[4:]


def _pair_cols(g, j):
    w = 2 * DSWA_E
    return slice((g * DSWA_PG + j) * w, (g * DSWA_PG + j + 1) * w)


def _group_weights(l_ref, j):
    ls = [l_ref[:, _pair_cols(g, j)] for g in range(3)]
    m = jnp.maximum(jnp.maximum(ls[0], ls[1]), ls[2])
    es = [jnp.exp(x - m) for x in ls]
    inv = 1.0 / (es[0] + es[1] + es[2])
    return [x * inv for x in es]


def _dswa_combine_fwd(o, lse, *, name):
    s, wd = o.shape
    tr = _pick(s, (512, 256, 128))

    def body(o_ref, l_ref, c_ref):
        for j in range(DSWA_PG):
            al = _group_weights(l_ref, j)
            for g in range(3):
                c_ref[:, _pair_cols(g, j)] = (o_ref[:, _pair_cols(g, j)] * al[g]).astype(c_ref.dtype)

    row = pl.BlockSpec((tr, wd), lambda i: (i, 0))
    return pl.pallas_call(
        body, name=name, grid=(s // tr,),
        in_specs=[row, row], out_specs=row,
        out_shape=jax.ShapeDtypeStruct(o.shape, BF16),
        compiler_params=_cparams(("parallel",)),
    )(o, lse)


def _dswa_combine_bwd(o, lse, dc, *, name):
    s, wd = o.shape
    tr = _pick(s, (512, 256, 128))

    def body(o_ref, l_ref, dc_ref, do_ref, corr_ref):
        lane = lax.broadcasted_iota(jnp.int32, (tr, 2 * DSWA_E), 1)
        is_a = lane < DSWA_E
        for j in range(DSWA_PG):
            al = _group_weights(l_ref, j)
            tot = jnp.zeros((tr, 2 * DSWA_E), F32)
            for g in range(3):
                cols = _pair_cols(g, j)
                dcv = dc_ref[:, cols]
                do_ref[:, cols] = dcv * al[g]
                prod = dcv * o_ref[:, cols]
                dal = jnp.where(is_a, jnp.sum(jnp.where(is_a, prod, 0.0), axis=-1, keepdims=True),
                                jnp.sum(jnp.where(is_a, 0.0, prod), axis=-1, keepdims=True))
                tot = tot + al[g] * dal
            for g in range(3):
                corr_ref[:, _pair_cols(g, j)] = -al[g] * tot

    row = pl.BlockSpec((tr, wd), lambda i: (i, 0))
    return pl.pallas_call(
        body, name=name, grid=(s // tr,),
        in_specs=[row, row, row], out_specs=[row, row],
        out_shape=[jax.ShapeDtypeStruct(o.shape, F32)] * 2,
        compiler_params=_cparams(("parallel",)),
    )(o, lse, dc)


class _Comm:
    def __init__(self, mode, arrays, kinds=None):
        self.mode, self.arrays, self.kinds = mode, list(arrays), kinds
        self.n = len(self.arrays)

    def out_shapes(self):
        if self.mode == "exchange":
            return [jax.ShapeDtypeStruct(x.shape, x.dtype) for x in self.arrays]
        shapes = []
        for x, kd in zip(self.arrays, self.kinds):
            shp = list(x.shape)
            if kd == "stack":
                shp = [N_DEV] + shp
            else:
                shp[-2 if kd == "rows" else -1] *= N_DEV
            shapes.append(jax.ShapeDtypeStruct(tuple(shp), x.dtype))
        return shapes

    def scratch(self):
        return [pltpu.SemaphoreType.DMA((7 * self.n,)), pltpu.SemaphoreType.DMA((7 * self.n,)),
                pltpu.SemaphoreType.DMA((self.n,))]

    def bind(self, in_refs, out_refs, sems):
        self.x, self.o = in_refs, out_refs
        self.send_sems, self.recv_sems, self.local_sems = sems
        self.pos = (lax.axis_index("x"), lax.axis_index("y"), lax.axis_index("c"))

    def _slot(self, i, px, py, pc):
        p = 4 * px + 2 * py + pc
        kd = self.kinds[i]
        if kd == "stack":
            return self.o[i].at[p]
        nd = len(self.x[i].shape)
        ax = nd - 2 if kd == "rows" else nd - 1
        size = self.x[i].shape[ax]
        idx = tuple(pl.ds(p * size, size) if a == ax else slice(None) for a in range(nd))
        return self.o[i].at[idx]

    def _gcopy(self, i, k, block, to, src=None):
        return pltpu.make_async_remote_copy(
            src_ref=self._slot(i, *block) if src is None else src, dst_ref=self._slot(i, *block),
            send_sem=self.send_sems.at[7 * i + k], recv_sem=self.recv_sems.at[7 * i + k],
            device_id=to, device_id_type=pl.DeviceIdType.MESH)

    def _chips(self):
        mx, my, _ = self.pos
        return [(1 - mx, my), (mx, 1 - my), (1 - mx, 1 - my)]

    def _xcopies(self):
        mx, my, mc = self.pos
        me = 4 * mx + 2 * my + mc
        copies = []
        for k in range(1, N_DEV):
            px = 1 - mx if (k >> 2) & 1 else mx
            py = 1 - my if (k >> 1) & 1 else my
            pc = 1 - mc if k & 1 else mc
            for i in range(self.n):
                copies.append(pltpu.make_async_remote_copy(
                    src_ref=self.x[i].at[4 * px + 2 * py + pc], dst_ref=self.o[i].at[me],
                    send_sem=self.send_sems.at[7 * i + k - 1], recv_sem=self.recv_sems.at[7 * i + k - 1],
                    device_id=(px, py, pc), device_id_type=pl.DeviceIdType.MESH))
        return copies

    def _local(self):
        mx, my, mc = self.pos
        if self.mode == "exchange":
            me = 4 * mx + 2 * my + mc
            return [pltpu.make_async_copy(self.x[i].at[me], self.o[i].at[me], self.local_sems.at[i]) for i in range(self.n)]
        return [pltpu.make_async_copy(self.x[i], self._slot(i, mx, my, mc), self.local_sems.at[i]) for i in range(self.n)]

    def _first(self):
        mx, my, mc = self.pos
        me, sibling = (mx, my, mc), (mx, my, 1 - mc)
        first = [self._gcopy(i, 0, me, sibling, src=self.x[i]) for i in range(self.n)]
        first += [self._gcopy(i, 1 + j, me, (*chip, mc), src=self.x[i]) for j, chip in enumerate(self._chips())
                  for i in range(self.n)]
        return first

    def _passed(self):
        mx, my, mc = self.pos
        return [self._gcopy(i, 4 + j, (*chip, mc), (mx, my, 1 - mc)) for j, chip in enumerate(self._chips())
                for i in range(self.n)]

    def start(self):
        for cp in self._local() + (self._xcopies() if self.mode == "exchange" else self._first()):
            cp.start()

    def mid(self):
        if self.mode == "exchange":
            return
        mx, my, mc = self.pos
        passed = self._passed()
        for j, chip in enumerate(self._chips()):
            for i in range(self.n):
                self._gcopy(i, 1 + j, (*chip, mc), (mx, my, mc)).wait_recv()
                passed[j * self.n + i].start()

    def end(self):
        mx, my, mc = self.pos
        if self.mode == "exchange":
            copies = self._xcopies()
            for cp in copies:
                cp.wait_recv()
            for cp in copies:
                cp.wait_send()
        else:
            for i in range(self.n):
                self._gcopy(i, 0, (mx, my, 1 - mc), (mx, my, mc)).wait_recv()
                for j, chip in enumerate(self._chips()):
                    self._gcopy(i, 4 + j, (*chip, 1 - mc), (mx, my, mc)).wait_recv()
            for cp in self._first() + self._passed():
                cp.wait_send()
        for cp in self._local():
            cp.wait()

    def run(self, *, name):
        n = self.n

        def body(*refs):
            self.bind(refs[:n], refs[n:2 * n], refs[2 * n:])
            self.start()
            self.mid()
            self.end()

        anyspec = pl.BlockSpec(memory_space=pl.ANY)
        return pl.pallas_call(body, name=name, in_specs=[anyspec] * n, out_specs=[anyspec] * n,
                              out_shape=self.out_shapes(), scratch_shapes=self.scratch())(*self.arrays)


def _comm_specs(comm):
    if comm is None:
        return [], [], [], []
    anyspec = pl.BlockSpec(memory_space=pl.ANY)
    return [anyspec] * comm.n, [anyspec] * comm.n, comm.out_shapes(), comm.scratch()


def _comm_hooks(comm, refs, n_in, n_out, n_scr, first, mid, last):
    if comm is None:
        return refs[:n_in], refs[n_in:n_in + n_out], refs[n_in + n_out:]
    c = comm.n
    ins, cin = refs[:n_in], refs[n_in:n_in + c]
    outs, cout = refs[n_in + c:n_in + c + n_out], refs[n_in + c + n_out:n_in + 2 * c + n_out]
    scr, sems = refs[n_in + 2 * c + n_out:n_in + 2 * c + n_out + n_scr], refs[n_in + 2 * c + n_out + n_scr:]
    comm.bind(cin, cout, sems)
    pl.when(first)(comm.start)
    pl.when(mid)(comm.mid)
    pl.when(last)(comm.end)
    return ins, outs, scr


def _adamw_update(g, w, m, v):
    mn = ADAM_B1 * m + (1.0 - ADAM_B1) * g
    vn = ADAM_B2 * v + (1.0 - ADAM_B2) * (g * g)
    m_hat = mn / (1.0 - ADAM_B1 ** ADAM_STEP)
    v_hat = vn / (1.0 - ADAM_B2 ** ADAM_STEP)
    return -ADAM_LR * (m_hat / (jnp.sqrt(v_hat) + ADAM_EPS) + ADAM_WD * w), mn, vn


def _adamw_layers(recvs, w, m, v, *, name):
    nl, ks, ns = w.shape
    tr = _pick(ks, (64, 48))

    def body(*refs):
        rv_refs = refs[:nl]
        w_ref, m_ref, v_ref, g_ref, d_ref, nm_ref, nv_ref = refs[nl:]
        for l in range(nl):
            g = rv_refs[l][0].astype(F32)
            for q in range(1, N_DEV):
                g = g + rv_refs[l][q].astype(F32)
            delta, mn, vn = _adamw_update(g, w_ref[l], m_ref[l], v_ref[l])
            g_ref[l] = g
            d_ref[l] = delta
            nm_ref[l] = mn
            nv_ref[l] = vn

    row = pl.BlockSpec((nl, tr, ns), lambda i: (0, i, 0))
    return pl.pallas_call(
        body, name=name, grid=(ks // tr,),
        in_specs=[pl.BlockSpec((N_DEV, tr, ns), lambda i: (0, i, 0))] * nl + [row] * 3,
        out_specs=[row] * 4,
        out_shape=[jax.ShapeDtypeStruct((nl, ks, ns), F32)] * 4,
        compiler_params=_cparams(("parallel",)),
    )(*recvs, w, m, v)


def _adamw_reduce(recv, w, m, v, *, name):
    r, c = w.shape
    tr = _pick(r, (128, 64, 8))

    def body(rv_ref, w_ref, m_ref, v_ref, g_ref, d_ref, nm_ref, nv_ref):
        g = rv_ref[0]
        for q in range(1, N_DEV):
            g = g + rv_ref[q]
        delta, mn, vn = _adamw_update(g, w_ref[...], m_ref[...], v_ref[...])
        g_ref[...] = g
        d_ref[...] = delta
        nm_ref[...] = mn
        nv_ref[...] = vn

    row = pl.BlockSpec((tr, c), lambda i: (i, 0))
    return pl.pallas_call(
        body, name=name, grid=(r // tr,),
        in_specs=[pl.BlockSpec((N_DEV, tr, c), lambda i: (0, i, 0)), row, row, row],
        out_specs=[row] * 4,
        out_shape=[jax.ShapeDtypeStruct((r, c), F32)] * 4,
        compiler_params=_cparams(("parallel",)),
    )(recv, w, m, v)


_BIG = ("gdn_w_in", "gdn_w_out", "dswa_w_in", "dswa_w_out", "mlp_w1", "mlp_w2")
_SMALL = ("gdn_conv_w", "norm_mix", "norm_mlp", "norm_final", "rel_bias", "gdn_a_log", "gdn_dt_bias", "gdn_norm_w")
_ORDER = ("norm_mix", "norm_mlp", "norm_final", "rel_bias", "gdn_w_in", "gdn_conv_w", "gdn_a_log", "gdn_dt_bias",
          "gdn_norm_w", "gdn_w_out", "dswa_w_in", "dswa_w_out", "mlp_w1", "mlp_w2")
_KIND = dict(gdn_w_in="stack", gdn_w_out="rows", dswa_w_in="stack", dswa_w_out="rows", mlp_w1="cols", mlp_w2="rows")


def _pack_rows(arrs, align):
    rows, counts = [], []
    for a in arrs:
        flat = a.reshape(-1)
        n = -(-flat.shape[0] // D_MODEL)
        flat = jnp.pad(flat, (0, n * D_MODEL - flat.shape[0]))
        rows.append(flat.reshape(n, D_MODEL))
        counts.append(n)
    out = jnp.concatenate(rows, axis=0)
    total = -(-out.shape[0] // align) * align
    return jnp.pad(out, ((0, total - out.shape[0]), (0, 0))), counts


def _unpack_rows(slab, shapes):
    outs, r = [], 0
    for shp in shapes:
        size = int(np.prod(shp))
        n = -(-size // D_MODEL)
        outs.append(slab[r:r + n].reshape(-1)[:size].reshape(shp))
        r += n
    return outs


def _col_shards(full, nshard):
    lead = full.shape[:-1]
    n = full.shape[-1] // nshard
    t = full.reshape(lead + (nshard, n))
    return jnp.moveaxis(t, -2, 0)


def _from_col_shards(g):
    t = jnp.moveaxis(g, 0, -2)
    return t.reshape(t.shape[:-2] + (t.shape[-2] * t.shape[-1],))


def kernel(x, norm_mix, norm_mlp, norm_final, rel_bias, gdn_w_in, gdn_conv_w, gdn_a_log, gdn_dt_bias, gdn_norm_w, gdn_w_out, dswa_w_in, dswa_w_out, mlp_w1, mlp_w2, loss_target, m_norm_mix, m_norm_mlp, m_norm_final, m_rel_bias, m_gdn_w_in, m_gdn_conv_w, m_gdn_a_log, m_gdn_dt_bias, m_gdn_norm_w, m_gdn_w_out, m_dswa_w_in, m_dswa_w_out, m_mlp_w1, m_mlp_w2, v_norm_mix, v_norm_mlp, v_norm_final, v_rel_bias, v_gdn_w_in, v_gdn_conv_w, v_gdn_a_log, v_gdn_dt_bias, v_gdn_norm_w, v_gdn_w_out, v_dswa_w_in, v_dswa_w_out, v_mlp_w1, v_mlp_w2):
    params = dict(norm_mix=norm_mix, norm_mlp=norm_mlp, norm_final=norm_final, rel_bias=rel_bias,
                  gdn_w_in=gdn_w_in, gdn_conv_w=gdn_conv_w, gdn_a_log=gdn_a_log, gdn_dt_bias=gdn_dt_bias,
                  gdn_norm_w=gdn_norm_w, gdn_w_out=gdn_w_out, dswa_w_in=dswa_w_in, dswa_w_out=dswa_w_out,
                  mlp_w1=mlp_w1, mlp_w2=mlp_w2)
    mom_m = dict(norm_mix=m_norm_mix, norm_mlp=m_norm_mlp, norm_final=m_norm_final, rel_bias=m_rel_bias,
                 gdn_w_in=m_gdn_w_in, gdn_conv_w=m_gdn_conv_w, gdn_a_log=m_gdn_a_log, gdn_dt_bias=m_gdn_dt_bias,
                 gdn_norm_w=m_gdn_norm_w, gdn_w_out=m_gdn_w_out, dswa_w_in=m_dswa_w_in, dswa_w_out=m_dswa_w_out,
                 mlp_w1=m_mlp_w1, mlp_w2=m_mlp_w2)
    mom_v = dict(norm_mix=v_norm_mix, norm_mlp=v_norm_mlp, norm_final=v_norm_final, rel_bias=v_rel_bias,
                 gdn_w_in=v_gdn_w_in, gdn_conv_w=v_gdn_conv_w, gdn_a_log=v_gdn_a_log, gdn_dt_bias=v_gdn_dt_bias,
                 gdn_norm_w=v_gdn_norm_w, gdn_w_out=v_gdn_w_out, dswa_w_in=v_dswa_w_in, dswa_w_out=v_dswa_w_out,
                 mlp_w1=v_mlp_w1, mlp_w2=v_mlp_w2)
    xs = x[0]
    target = loss_target[0]
    dist = _Dist(params)
    conv_tail, _ = _pack_rows([gdn_conv_w], 8)
    (conv_g,) = dist.put("start", dist.gather_comm("start", extra=[(conv_tail, "stack")]).run(name="ag_start"))
    conv_parts = [_unpack_rows(conv_g[dev], [gdn_conv_w.shape])[0] for dev in range(N_DEV)]
    conv_full = _from_col_shards(jnp.stack(conv_parts))[:, :, 0, :]

    loss_part, dcur, g_big, rep, g_conv = _local_step(
        xs, target, dict(norm_mix=norm_mix, norm_mlp=norm_mlp, norm_final=norm_final, rel_bias=rel_bias,
                         gdn_a_log=gdn_a_log, gdn_dt_bias=gdn_dt_bias, gdn_norm_w=gdn_norm_w), dist.full, conv_full, dist)
    loss = lax.psum(loss_part[0, 0], ("x", "y", "c"))
    grad_x = dcur[None]

    conv_dev = _col_shards(jnp.stack(g_conv)[:, :, None, :], N_DEV)
    small_send = jnp.stack([_pack_rows([conv_dev[dev]] + [rep[n] for n in _SMALL[1:]], 8)[0] for dev in range(N_DEV)])
    (small_recv,) = dist.got("end", dist.send_comm("end", g_big, extra=[small_send]).run(name="grad_exchange"))

    outs = {}
    for n in _BIG:
        recvs = [dist.recv[(n, l)] for l in range(params[n].shape[0])]
        res = _adamw_layers(recvs, params[n], mom_m[n], mom_v[n], name=f"adamw_{n}")
        for tag, t in zip(("grad", "delta", "new_m", "new_v"), res):
            outs[(tag, n)] = t
    w_slab, _ = _pack_rows([params[n] for n in _SMALL], 8)
    m_slab, _ = _pack_rows([mom_m[n] for n in _SMALL], 8)
    v_slab, _ = _pack_rows([mom_v[n] for n in _SMALL], 8)
    small = _adamw_reduce(small_recv, w_slab, m_slab, v_slab, name="adamw_small")
    shapes = [params[n].shape for n in _SMALL]
    for tag, slab in zip(("grad", "delta", "new_m", "new_v"), small):
        for n, t in zip(_SMALL, _unpack_rows(slab, shapes)):
            outs[(tag, n)] = t
    result = [loss, grad_x]
    for tag in ("grad", "delta", "new_m", "new_v"):
        result += [outs[(tag, n)] for n in _ORDER]
    return tuple(result)


_GATHER = {
    "start": (("gdn_w_in", 0),),
    "gdn_proj0": (("gdn_w_out", 0), ("mlp_w1", 0)),
    "chunk_fwd0": (("mlp_w2", 0), ("dswa_w_in", 0), ("dswa_w_out", 0), ("mlp_w1", 1)),
    "mlp_up0": (("mlp_w2", 1),),
    "mlp_down0": (("gdn_w_in", 1),),
    "attn_fwd1": (("gdn_w_out", 1), ("mlp_w1", 2), ("mlp_w2", 2)),
    "chunk_fwd2": (("dswa_w_in", 1), ("dswa_w_out", 1), ("mlp_w1", 3), ("mlp_w2", 3)),
}
_SEND = {
    "attn_bwd3": (("mlp_w1", 3), ("mlp_w2", 3)),
    "chunk_bwd2": (("dswa_w_in", 1), ("dswa_w_out", 1), ("mlp_w1", 2), ("mlp_w2", 2)),
    "attn_bwd1": (("mlp_w1", 1), ("mlp_w2", 1)),
    "chunk_bwd0": (("gdn_w_in", 1), ("gdn_w_out", 1), ("dswa_w_in", 0), ("dswa_w_out", 0), ("mlp_w2", 0)),
    "pre_bwd0": (("mlp_w1", 0),),
    "end": (("gdn_w_in", 0), ("gdn_w_out", 0)),
}


class _Dist:
    def __init__(self, params):
        self.shards = {n: params[n].astype(BF16) for n in _BIG}
        self.full = {n: [None] * params[n].shape[0] for n in _BIG}
        self.recv = {}

    def gather_comm(self, tag, extra=()):
        if tag not in _GATHER:
            return None
        arrays = [self.shards[n][l] for n, l in _GATHER[tag]] + [a for a, _ in extra]
        return _Comm("gather", arrays, [_KIND[n] for n, _ in _GATHER[tag]] + [k for _, k in extra])

    def put(self, tag, outs):
        for (n, l), t in zip(_GATHER.get(tag, ()), outs):
            self.full[n][l] = _from_col_shards(t) if _KIND[n] == "stack" else t
        return outs[len(_GATHER.get(tag, ())):]

    def send_comm(self, tag, g_big, extra=()):
        if tag not in _SEND:
            return None
        arrays = [_col_shards(g_big[n][l], N_DEV) if _KIND[n] == "stack" else g_big[n][l] for n, l in _SEND[tag]]
        return _Comm("exchange", arrays + list(extra))

    def got(self, tag, outs):
        for item, t in zip(_SEND.get(tag, ()), outs):
            self.recv[item] = t
        return outs[len(_SEND.get(tag, ())):]


def _mm_gather(dist, tag, *args, **kw):
    comm = dist and dist.gather_comm(tag)
    if not comm:
        return _mm(*args, **kw)
    res, got = _mm(*args, comm=comm, **kw)
    dist.put(tag, got)
    return res


def _local_step(xs, target, sp, full, conv_full, dist=None):
    s = xs.shape[0]
    norm_mix, norm_mlp, norm_final = sp["norm_mix"], sp["norm_mlp"], sp["norm_final"]
    gdn_a_log, gdn_dt_bias, gdn_norm_w = sp["gdn_a_log"], sp["gdn_dt_bias"], sp["gdn_norm_w"]
    onehot = _bucket_onehot()
    table_t = sp["rel_bias"].T
    bias = _dswa_bias(table_t, onehot, name="dswa_bias").reshape(DSWA_HEADS, DSWA_HALF, 3 * DSWA_HALF)

    saved = []
    cur = xs
    for i in range(DEPTH):
        j = i // 2
        sv = dict(x_in=cur)
        h = _rms_fwd(cur, norm_mix[i], name=f"rms_mix_fwd{i}")
        sv["h"] = h
        if i % 2 == 0:
            w_in = full["gdn_w_in"][j]
            proj = _mm_gather(dist, f"gdn_proj{i}", h, w_in[:, :GDN_MAIN], name=f"gdn_proj{i}")
            ab = _mm(h, w_in[:, GDN_MAIN:], name=f"gdn_proj_ab{i}")
            qkvn = _gdn_pre_fwd(proj, conv_full[j], name=f"gdn_pre_fwd{i}")
            g_all, beta_all = _gdn_gate_fwd(ab[:, :2 * GDN_HEADS], ab[:, 2 * GDN_HEADS:], gdn_a_log[j], gdn_dt_bias[j],
                                            name=f"gdn_gate_fwd{i}")
            gshape = (2, GDN_HEADS, s // GDN_CHUNK, 1, GDN_CHUNK)
            g_row = g_all.T.reshape(gshape)
            b_row = beta_all.T.reshape(gshape)
            o, states, got = _gdn_chunk_fwd(qkvn, g_row, b_row, name=f"gdn_chunk_fwd{i}",
                                            comm=dist and dist.gather_comm(f"chunk_fwd{i}"))
            if dist:
                dist.put(f"chunk_fwd{i}", got)
            act = _gdn_post_fwd(o, proj, gdn_norm_w[j], name=f"gdn_post_fwd{i}")
            sv.update(proj=proj, ab=ab, qkvn=qkvn, g_row=g_row, b_row=b_row, o=o, states=states, act=act)
            w_out = full["gdn_w_out"][j]
        else:
            w_in = full["dswa_w_in"][j]
            qkv = _mm(h, w_in, name=f"dswa_proj{i}")
            o_n, lse_n, got = _dswa_attn_fwd(qkv, bias, name=f"dswa_attn_fwd{i}",
                                             comm=dist and dist.gather_comm(f"attn_fwd{i}"))
            if dist:
                dist.put(f"attn_fwd{i}", got)
            act = _dswa_combine_fwd(o_n, lse_n, name=f"dswa_comb_fwd{i}")
            sv.update(qkv=qkv, o_n=o_n, lse_n=lse_n, act=act)
            w_out = full["dswa_w_out"][j]
        cur = _mm(act, w_out, name=f"mix_out{i}", epilogue=lambda acc, r: (acc + r,), extras=(cur,))
        sv["x_mid"] = cur
        h2 = _rms_fwd(cur, norm_mlp[i], name=f"rms_mlp_fwd{i}")
        u, a = _mm_gather(dist, f"mlp_up{i}", h2, full["mlp_w1"][i], name=f"mlp_up{i}", out_dtypes=(BF16, BF16),
                          epilogue=lambda acc: (acc, jnp.square(jnp.maximum(acc, 0.0))))
        cur = _mm_gather(dist, f"mlp_down{i}", a, full["mlp_w2"][i], name=f"mlp_down{i}",
                         epilogue=lambda acc, r: (acc + r,), extras=(cur,))
        sv.update(h2=h2, u=u, a=a)
        saved.append(sv)

    loss_part, dcur, dcur_b, dg_final = _loss_head(cur, norm_final, target, name="loss_head")

    g_norm_mix, g_norm_mlp = [None] * DEPTH, [None] * DEPTH
    g_big = {n: [None] * len(full[n]) for n in _BIG}
    g_conv, g_alog, g_dt, g_nw = [None] * 2, [None] * 2, [None] * 2, [None] * 2
    d_table_t = jnp.zeros((DSWA_HEADS, REL_BUCKETS), F32)
    for i in reversed(range(DEPTH)):
        j = i // 2
        sv = saved[i]
        w1, w2 = full["mlp_w1"][i], full["mlp_w2"][i]
        du = _mm(dcur_b, w2, tb=True, name=f"mlp_down_bwd{i}", out_dtypes=(BF16,),
                 epilogue=lambda acc, uu: (acc * (2.0 * jnp.maximum(uu.astype(F32), 0.0)),), extras=(sv["u"],))
        g_big["mlp_w2"][i] = _mm(sv["a"], dcur_b, ta=True, name=f"mlp_w2_grad{i}", out_dtypes=(BF16,), shard="rows")
        g_big["mlp_w1"][i] = _mm(sv["h2"], du, ta=True, name=f"mlp_w1_grad{i}", out_dtypes=(BF16,), shard="cols")
        dh2 = _mm(du, w1, tb=True, name=f"mlp_up_bwd{i}")
        dmid, dmid_b, g_norm_mlp[i] = _rms_bwd(sv["x_mid"], norm_mlp[i], dh2, dcur, name=f"rms_mlp_bwd{i}")
        if i % 2 == 0:
            w_in, w_out = full["gdn_w_in"][j], full["gdn_w_out"][j]
            dact = _mm(dmid_b, w_out, tb=True, name=f"mix_out_bwd{i}")
            g_big["gdn_w_out"][j] = _mm(sv["act"], dmid_b, ta=True, name=f"mix_out_grad{i}", out_dtypes=(BF16,),
                                        shard="rows")
            do, dz, g_nw[j] = _gdn_post_bwd(sv["o"], sv["proj"], gdn_norm_w[j], dact, name=f"gdn_post_bwd{i}")
            dqkvn, dg_row, db_row, got = _gdn_chunk_bwd(sv["qkvn"], sv["g_row"], sv["b_row"], sv["states"], do,
                                                        name=f"gdn_chunk_bwd{i}",
                                                        comm=dist and dist.send_comm(f"chunk_bwd{i}", g_big))
            if dist:
                dist.got(f"chunk_bwd{i}", got)
            dproj, g_conv[j], got = _gdn_pre_bwd(sv["proj"], conv_full[j], dqkvn, dz, name=f"gdn_pre_bwd{i}",
                                                 comm=dist and dist.send_comm(f"pre_bwd{i}", g_big))
            if dist:
                dist.got(f"pre_bwd{i}", got)
            nh2 = 2 * GDN_HEADS
            da_, db_, g_alog[j], g_dt[j] = _gdn_gate_bwd(sv["ab"][:, :nh2], sv["ab"][:, nh2:], gdn_a_log[j], gdn_dt_bias[j],
                                                         dg_row.reshape(nh2, s).T, db_row.reshape(nh2, s).T,
                                                         name=f"gdn_gate_bwd{i}")
            dab = jnp.concatenate([da_, db_], axis=1)
            gw_main = _mm(sv["h"], dproj, ta=True, name=f"gdn_w_in_grad{i}", out_dtypes=(BF16,))
            gw_ab = _mm(sv["h"], dab, ta=True, name=f"gdn_w_ab_grad{i}", out_dtypes=(BF16,))
            g_big["gdn_w_in"][j] = jnp.concatenate([gw_main, gw_ab], axis=1)
            dh_ab = _mm(dab, w_in[:, GDN_MAIN:], tb=True, name=f"gdn_proj_ab_bwd{i}")
            dh = _mm(dproj, w_in[:, :GDN_MAIN], tb=True, name=f"gdn_proj_bwd{i}",
                     epilogue=lambda acc, r: (acc + r,), extras=(dh_ab,))
        else:
            w_in, w_out = full["dswa_w_in"][j], full["dswa_w_out"][j]
            dact = _mm(dmid_b, w_out, tb=True, name=f"mix_out_bwd{i}")
            g_big["dswa_w_out"][j] = _mm(sv["act"], dmid_b, ta=True, name=f"mix_out_grad{i}", out_dtypes=(BF16,),
                                         shard="rows")
            do_n, corr_n = _dswa_combine_bwd(sv["o_n"], sv["lse_n"], dact, name=f"dswa_comb_bwd{i}")
            *dqkv, dbias, got = _dswa_attn_bwd(sv["qkv"], bias, sv["lse_n"], do_n, corr_n, name=f"dswa_attn_bwd{i}",
                                               comm=dist and dist.send_comm(f"attn_bwd{i}", g_big))
            if dist:
                dist.got(f"attn_bwd{i}", got)
            d_table_t = d_table_t + _dswa_dtable(dbias.reshape(DSWA_HEADS, -1), onehot, name=f"dswa_dtable{i}")
            g_big["dswa_w_in"][j] = jnp.concatenate(
                [_mm(sv["h"], dt, ta=True, name=f"dswa_w_in_grad{i}_{t}", out_dtypes=(BF16,)) for t, dt in enumerate(dqkv)],
                axis=1)
            dh = None
            for t, dt in enumerate(dqkv):
                w_t = w_in[:, t * DSWA_WIDTH:(t + 1) * DSWA_WIDTH]
                if dh is None:
                    dh = _mm(dt, w_t, tb=True, name=f"dswa_proj_bwd{i}_{t}")
                else:
                    dh = _mm(dt, w_t, tb=True, name=f"dswa_proj_bwd{i}_{t}", epilogue=lambda acc, r: (acc + r,), extras=(dh,))
        dcur, dcur_b, g_norm_mix[i] = _rms_bwd(sv["x_in"], norm_mix[i], dh, dmid, name=f"rms_mix_bwd{i}")

    rep = dict(norm_mix=jnp.concatenate(g_norm_mix, axis=0), norm_mlp=jnp.concatenate(g_norm_mlp, axis=0),
               norm_final=dg_final.reshape(-1), rel_bias=d_table_t.T,
               gdn_a_log=jnp.stack(g_alog).reshape(gdn_a_log.shape), gdn_dt_bias=jnp.stack(g_dt).reshape(gdn_dt_bias.shape),
               gdn_norm_w=jnp.stack(g_nw).reshape(gdn_norm_w.shape))
    return loss_part, dcur, g_big, rep, g_conv
```

```python
import functools
import math

import jax
import jax.numpy as jnp
import numpy as np
from jax import lax
from jax.experimental import pallas as pl
from jax.experimental.pallas import tpu as pltpu

F32 = jnp.float32
BF16 = jnp.bfloat16
HP = lax.Precision.HIGHEST

N_DEV = 8
D_MODEL = 1024
DEPTH = 4
RMS_EPS = 1e-6
NEG_INF = -1e30

GDN_HEADS = 8
GDN_DK = 128
GDN_CONV = 5
GDN_CHUNK = 128
GDN_QKV = 3 * GDN_HEADS * GDN_DK
GDN_MAIN = GDN_QKV + GDN_HEADS * GDN_DK
GDN_AB = 4 * GDN_HEADS

DSWA_DILS = (1, 4, 16)
DSWA_HG = 6
DSWA_E = 64
DSWA_HEADS = 18
DSWA_WIDTH = DSWA_HEADS * DSWA_E
DSWA_HALF = 64
DSWA_PG = DSWA_HG // 2
DSWA_UNROLL = 4
REL_BUCKETS = 32
REL_MAX_DIST = 1024

ADAM_LR = 0.001
ADAM_B1 = 0.9
ADAM_B2 = 0.999
ADAM_EPS = 1e-08
ADAM_WD = 0.01
ADAM_STEP = 10

VMEM_LIMIT = 56 * 1024 * 1024


def _cparams(sem=None, **kw):
    return pltpu.CompilerParams(dimension_semantics=sem, vmem_limit_bytes=VMEM_LIMIT, **kw)


def _pick(dim, cands):
    for c in cands:
        if dim % c == 0:
            return c
    return dim


def _bdot(a, b):
    return jnp.dot(a.astype(BF16), b.astype(BF16), preferred_element_type=F32)


def _bdot_nt(a, b):
    return lax.dot_general(a.astype(BF16), b.astype(BF16), (((1,), (1,)), ((), ())),
                           preferred_element_type=F32)


def _bdot_tn(a, b):
    return lax.dot_general(a.astype(BF16), b.astype(BF16), (((0,), (0,)), ((), ())),
                           preferred_element_type=F32)


def _hdot(a, b):
    return jnp.dot(a, b, precision=HP, preferred_element_type=F32)


def _hdot_tn(a, b):
    return lax.dot_general(a, b, (((0,), (0,)), ((), ())), precision=HP, preferred_element_type=F32)


def _hdot_nt(a, b):
    return lax.dot_general(a, b, (((1,), (1,)), ((), ())), precision=HP, preferred_element_type=F32)


def _sigmoid(x):
    return 1.0 / (1.0 + jnp.exp(-x))


def _mm(a, b, *, name, ta=False, tb=False, out_dtypes=(F32,), epilogue=None, extras=(),
        tm=None, tn=None, tk=None, shard=None, comm=None):
    if ta:
        kdim, m = a.shape
    else:
        m, kdim = a.shape
    n = b.shape[0] if tb else b.shape[1]
    if shard == "rows":
        tm = m // N_DEV if (m // N_DEV) % 128 == 0 else m
    if shard == "cols":
        tn = n // N_DEV
    tm = tm or _pick(m, (1024, 1152, 512, 384, 256, 128))
    tn = tn or _pick(n, (1024, 1152, 512, 384, 256, 128))
    tk = tk or _pick(kdim, (1024, 1152, 512, 384, 256, 128))
    nk = kdim // tk
    n_out = len(out_dtypes)
    n_ex = len(extras)
    rows_all = shard == "rows" and tm == m

    gi, gj = m // tm, n // tn

    def body(*refs):
        i, j, k = pl.program_id(0), pl.program_id(1), pl.program_id(2)
        inner = (j == 0) & (k == 0)
        ins, out_refs, (acc_ref,) = _comm_hooks(
            comm, refs, 2 + n_ex, n_out, 1, (i == 0) & inner, (i == (3 * gi) // 4) & inner,
            (i == gi - 1) & (j == gj - 1) & (k == nk - 1))
        a_ref, b_ref, ex_refs = ins[0], ins[1], ins[2:]

        @pl.when(k == 0)
        def _():
            acc_ref[...] = jnp.zeros_like(acc_ref)

        av = a_ref[...].astype(BF16)
        bv = b_ref[...].astype(BF16)
        dims = (((0 if ta else 1,), (1 if tb else 0,)), ((), ()))
        acc_ref[...] += lax.dot_general(av, bv, dims, preferred_element_type=F32)

        @pl.when(k == nk - 1)
        def _():
            acc = acc_ref[...]
            outs = (acc,) if epilogue is None else epilogue(acc, *[r[...] for r in ex_refs])
            for r, o in zip(out_refs, outs):
                if rows_all:
                    for p in range(N_DEV):
                        r[p] = o[p * (m // N_DEV):(p + 1) * (m // N_DEV)].astype(r.dtype)
                else:
                    r[...] = o.astype(r.dtype)

    a_spec = pl.BlockSpec((tk, tm), lambda i, j, k: (k, i)) if ta else pl.BlockSpec((tm, tk), lambda i, j, k: (i, k))
    b_spec = pl.BlockSpec((tn, tk), lambda i, j, k: (j, k)) if tb else pl.BlockSpec((tk, tn), lambda i, j, k: (k, j))
    o_spec = pl.BlockSpec((tm, tn), lambda i, j, k: (i, j))
    out_specs = [o_spec] * n_out
    out_shape = [jax.ShapeDtypeStruct((m, n), dt) for dt in out_dtypes]
    if shard == "rows":
        out_shape = [jax.ShapeDtypeStruct((N_DEV, m // N_DEV, n), out_dtypes[0])]
        out_specs = [pl.BlockSpec((N_DEV, m // N_DEV, tn), lambda i, j, k: (0, 0, j)) if rows_all
                     else pl.BlockSpec((None, tm, tn), lambda i, j, k: (i, 0, j))]
    if shard == "cols":
        out_shape = [jax.ShapeDtypeStruct((N_DEV, m, tn), out_dtypes[0])]
        out_specs = [pl.BlockSpec((None, tm, tn), lambda i, j, k: (j, i, 0))]
    c_in, c_out, c_shape, c_scr = _comm_specs(comm)
    outs = pl.pallas_call(
        body, name=name,
        grid=(gi, gj, nk),
        in_specs=[a_spec, b_spec] + [o_spec] * n_ex + c_in,
        out_specs=out_specs + c_out,
        out_shape=out_shape + c_shape,
        scratch_shapes=[pltpu.VMEM((tm, tn), F32)] + c_scr,
        compiler_params=_cparams(("arbitrary",) * 3 if comm else ("parallel", "parallel", "arbitrary")),
    )(a, b, *extras, *(comm.arrays if comm else []))
    res = outs[0] if n_out == 1 else tuple(outs[:n_out])
    return (res, outs[n_out:]) if comm else res


def _rms_fwd(x, g, *, name):
    s, d = x.shape
    tr = _pick(s, (512, 256, 128))

    def body(x_ref, g_ref, h_ref):
        xv = x_ref[...]
        r = lax.rsqrt(jnp.mean(xv * xv, axis=-1, keepdims=True) + RMS_EPS)
        h_ref[...] = (xv * r * g_ref[...]).astype(h_ref.dtype)

    return pl.pallas_call(
        body, name=name, grid=(s // tr,),
        in_specs=[pl.BlockSpec((tr, d), lambda i: (i, 0)), pl.BlockSpec((1, d), lambda i: (0, 0))],
        out_specs=pl.BlockSpec((tr, d), lambda i: (i, 0)),
        out_shape=jax.ShapeDtypeStruct((s, d), BF16),
        compiler_params=_cparams(("parallel",)),
    )(x, g.reshape(1, d))


def _rms_bwd(x, g, dh, dres, *, name):
    s, d = x.shape
    tr = _pick(s, (512, 256, 128))

    def body(x_ref, g_ref, dh_ref, dres_ref, dx_ref, dxb_ref, dg_ref):
        i = pl.program_id(0)
        xv = x_ref[...]
        r = lax.rsqrt(jnp.mean(xv * xv, axis=-1, keepdims=True) + RMS_EPS)
        xn = xv * r
        dhv = dh_ref[...]
        dn = dhv * g_ref[...]
        dx = dres_ref[...] + r * (dn - xn * jnp.mean(dn * xn, axis=-1, keepdims=True))
        dx_ref[...] = dx
        dxb_ref[...] = dx.astype(dxb_ref.dtype)
        part = jnp.sum(dhv * xn, axis=0, keepdims=True)

        @pl.when(i == 0)
        def _():
            dg_ref[...] = part

        @pl.when(i > 0)
        def _():
            dg_ref[...] += part

    row = pl.BlockSpec((tr, d), lambda i: (i, 0))
    vec = pl.BlockSpec((1, d), lambda i: (0, 0))
    return pl.pallas_call(
        body, name=name, grid=(s // tr,),
        in_specs=[row, vec, row, row], out_specs=[row, row, vec],
        out_shape=[jax.ShapeDtypeStruct((s, d), F32), jax.ShapeDtypeStruct((s, d), BF16),
                   jax.ShapeDtypeStruct((1, d), F32)],
        compiler_params=_cparams(("arbitrary",)),
    )(x, g.reshape(1, d), dh, dres)


def _loss_head(x, g, target, *, name):
    s, d = x.shape
    tr = _pick(s, (512, 256, 128))

    def body(x_ref, g_ref, t_ref, loss_ref, dx_ref, dxb_ref, dg_ref):
        i = pl.program_id(0)
        xv = x_ref[...]
        gv = g_ref[...]
        r = lax.rsqrt(jnp.mean(xv * xv, axis=-1, keepdims=True) + RMS_EPS)
        xn = xv * r
        err = xn * gv - t_ref[...]
        lpart = 0.5 * jnp.sum(jnp.mean(err * err, axis=-1, keepdims=True), axis=0, keepdims=True)
        dy = err * (1.0 / d)
        dn = dy * gv
        dx = r * (dn - xn * jnp.mean(dn * xn, axis=-1, keepdims=True))
        dx_ref[...] = dx
        dxb_ref[...] = dx.astype(dxb_ref.dtype)
        gpart = jnp.sum(dy * xn, axis=0, keepdims=True)

        @pl.when(i == 0)
        def _():
            dg_ref[...] = gpart
            loss_ref[...] = lpart

        @pl.when(i > 0)
        def _():
            dg_ref[...] += gpart
            loss_ref[...] += lpart

    row = pl.BlockSpec((tr, d), lambda i: (i, 0))
    vec = pl.BlockSpec((1, d), lambda i: (0, 0))
    one = pl.BlockSpec((1, 1), lambda i: (0, 0))
    return pl.pallas_call(
        body, name=name, grid=(s // tr,),
        in_specs=[row, vec, row], out_specs=[one, row, row, vec],
        out_shape=[jax.ShapeDtypeStruct((1, 1), F32), jax.ShapeDtypeStruct((s, d), F32),
                   jax.ShapeDtypeStruct((s, d), BF16), jax.ShapeDtypeStruct((1, d), F32)],
        compiler_params=_cparams(("arbitrary",)),
    )(x, g.reshape(1, d), target)


def _shift_rows(x, sft, rows):
    s = x.shape[0]
    if sft == 0:
        return x
    y = pltpu.roll(x, (-sft) % s, 0)
    ok = (rows + sft >= 0) & (rows + sft < s)
    return jnp.where(ok, y, 0.0)


def _gdn_pre_fwd(proj, conv_w, *, name):
    s = proj.shape[0]
    nblk = GDN_QKV // 128
    pad = GDN_CONV // 2

    def body(x_ref, w_ref, o_ref):
        j = pl.program_id(0)
        x = x_ref[...]
        rows = lax.broadcasted_iota(jnp.int32, x.shape, 0)
        c = jnp.zeros_like(x)
        for t in range(GDN_CONV):
            c = c + w_ref[pl.ds(t, 1), :] * _shift_rows(x, t - pad, rows)
        a = c * _sigmoid(c)
        rinv = lax.rsqrt(jnp.sum(a * a, axis=-1, keepdims=True) + 1e-6)
        scale = jnp.where(j < GDN_HEADS, GDN_DK ** -0.5, 1.0)
        o_ref[...] = jnp.where(j >= 2 * GDN_HEADS, a, a * (rinv * scale))

    return pl.pallas_call(
        body, name=name, grid=(nblk,),
        in_specs=[pl.BlockSpec((s, 128), lambda j: (0, j)), pl.BlockSpec((GDN_CONV, 128), lambda j: (0, j))],
        out_specs=pl.BlockSpec((s, 128), lambda j: (0, j)),
        out_shape=jax.ShapeDtypeStruct((s, GDN_QKV), F32),
        compiler_params=_cparams(("parallel",)),
    )(proj, conv_w)


def _gdn_pre_bwd(proj, conv_w, dqkv, dproj, *, name, comm=None):
    s = proj.shape[0]
    nblk = GDN_QKV // 128
    pad = GDN_CONV // 2

    def body(*refs):
        j = pl.program_id(0)
        (x_ref, w_ref, df_ref, dbk_ref, _), (dx_ref, dw_ref), _ = _comm_hooks(
            comm, refs, 5, 2, 0, j == 0, j == nblk // 2, j == nblk - 1)
        x = x_ref[...]
        rows = lax.broadcasted_iota(jnp.int32, x.shape, 0)
        xs = [_shift_rows(x, t - pad, rows) for t in range(GDN_CONV)]
        c = jnp.zeros_like(x)
        for t in range(GDN_CONV):
            c = c + w_ref[pl.ds(t, 1), :] * xs[t]
        sg = _sigmoid(c)
        a = c * sg
        rinv = lax.rsqrt(jnp.sum(a * a, axis=-1, keepdims=True) + 1e-6)
        scale = jnp.where(j < GDN_HEADS, GDN_DK ** -0.5, 1.0)
        dy = df_ref[...] + dbk_ref[...]
        nh = a * rinv
        da_n = (rinv * scale) * (dy - nh * jnp.sum(dy * nh, axis=-1, keepdims=True))
        da = jnp.where(j >= 2 * GDN_HEADS, dy, da_n)
        dc = da * (sg * (1.0 + c * (1.0 - sg)))
        dx = jnp.zeros_like(x)
        for t in range(GDN_CONV):
            dx = dx + w_ref[pl.ds(t, 1), :] * _shift_rows(dc, pad - t, rows)
            dw_ref[pl.ds(t, 1), :] = jnp.sum(dc * xs[t], axis=0, keepdims=True)
        dx_ref[...] = dx.astype(dx_ref.dtype)

    col = pl.BlockSpec((s, 128), lambda j: (0, j))
    wsp = pl.BlockSpec((GDN_CONV, 128), lambda j: (0, j))
    c_in, c_out, c_shape, c_scr = _comm_specs(comm)
    res = pl.pallas_call(
        body, name=name, grid=(nblk,),
        in_specs=[col, wsp, col, col, pl.BlockSpec(memory_space=pl.ANY)] + c_in, out_specs=[col, wsp] + c_out,
        out_shape=[jax.ShapeDtypeStruct(dproj.shape, BF16), jax.ShapeDtypeStruct((GDN_CONV, GDN_QKV), F32)] + c_shape,
        input_output_aliases={4: 0},
        scratch_shapes=c_scr,
        compiler_params=_cparams(("arbitrary",) if comm else ("parallel",)),
    )(proj, conv_w, dqkv[0], dqkv[1], dproj, *(comm.arrays if comm else []))
    return res[0], res[1], res[2:]


def _softplus(x):
    return jnp.maximum(x, 0.0) + jnp.log(1.0 + jnp.exp(-jnp.abs(x)))


def _gdn_gate_fwd(a, b, a_log, dt_bias, *, name):
    s = a.shape[0]
    nh = 2 * GDN_HEADS

    def body(a_ref, b_ref, al_ref, dt_ref, g_ref, be_ref):
        g_ref[...] = -jnp.exp(al_ref[...]) * _softplus(a_ref[...] + dt_ref[...])
        be_ref[...] = _sigmoid(b_ref[...])

    return pl.pallas_call(
        body, name=name,
        out_shape=[jax.ShapeDtypeStruct((s, nh), F32), jax.ShapeDtypeStruct((s, nh), F32)],
        compiler_params=_cparams(),
    )(a, b, a_log.reshape(1, nh), dt_bias.reshape(1, nh))


def _gdn_gate_bwd(a, b, a_log, dt_bias, dg, dbeta, *, name):
    s = a.shape[0]
    nh = 2 * GDN_HEADS

    def body(a_ref, b_ref, al_ref, dt_ref, dg_ref, db_ref, da_ref, dbb_ref, dal_ref, ddt_ref):
        ea = jnp.exp(al_ref[...])
        z = a_ref[...] + dt_ref[...]
        dgv = dg_ref[...]
        dz = dgv * (-ea) * _sigmoid(z)
        dal_ref[...] = jnp.sum(dgv * (-ea) * _softplus(z), axis=0, keepdims=True)
        ddt_ref[...] = jnp.sum(dz, axis=0, keepdims=True)
        sb = _sigmoid(b_ref[...])
        da_ref[...] = dz
        dbb_ref[...] = db_ref[...] * sb * (1.0 - sb)

    return pl.pallas_call(
        body, name=name,
        out_shape=[jax.ShapeDtypeStruct((s, nh), F32), jax.ShapeDtypeStruct((s, nh), F32),
                   jax.ShapeDtypeStruct((1, nh), F32), jax.ShapeDtypeStruct((1, nh), F32)],
        compiler_params=_cparams(),
    )(a, b, a_log.reshape(1, nh), dt_bias.reshape(1, nh), dg, dbeta)


def _chunk_masks(d):
    c = GDN_CHUNK
    ii = lax.broadcasted_iota(jnp.int32, (c, c), 0)
    jj = lax.broadcasted_iota(jnp.int32, (c, c), 1)
    dif = (ii - jj) * (1 - 2 * d)
    mi = dif >= 0
    mit = dif <= 0
    ms = dif > 0
    eye = ii == jj
    bds = [(ii >> sh) == (jj >> sh) for sh in range(3, c.bit_length() - 1)]
    return dict(mi=mi, mit=mit, ms=ms, eye=eye, bds=bds,
                mif=mi.astype(F32), mitf=mit.astype(F32), eyef=eye.astype(F32))


class _V:
    def __init__(self, xs):
        self.xs = tuple(xs)

    def __add__(self, o):
        return _lift(lambda a, b: a + b)(self, o)

    def __radd__(self, o):
        return _lift(lambda a, b: b + a)(self, o)

    def __sub__(self, o):
        return _lift(lambda a, b: a - b)(self, o)

    def __rsub__(self, o):
        return _lift(lambda a, b: b - a)(self, o)

    def __mul__(self, o):
        return _lift(lambda a, b: a * b)(self, o)

    def __rmul__(self, o):
        return _lift(lambda a, b: b * a)(self, o)

    def __and__(self, o):
        return _lift(lambda a, b: a & b)(self, o)

    def __neg__(self):
        return _lift(lambda a: -a)(self)

    def __rtruediv__(self, o):
        return _lift(lambda a, b: b / a)(self, o)


def _lift(f):
    def g(*args, **kw):
        n = next(len(a.xs) for a in args if isinstance(a, _V))
        return _V(f(*[a.xs[i] if isinstance(a, _V) else a for a in args], **kw) for i in range(n))
    return g


_vwhere, _vsum, _vexp, _vnot = _lift(jnp.where), _lift(jnp.sum), _lift(jnp.exp), _lift(jnp.logical_not)
_vhdot, _vhdot_tn = _lift(_bdot), _lift(_bdot_tn)
_vbdot, _vbdot_nt, _vbdot_tn = _lift(_bdot), _lift(_bdot_nt), _lift(_bdot_tn)
_vcat = _lift(lambda a, b: jnp.concatenate([a, b], axis=1))
_vlo = _lift(lambda a, n: a[:, :n])
_vhi = _lift(lambda a, n: a[:, n:])


def _both_masks(n):
    m = [_chunk_masks(d) for d in range(2)]
    mk = {key: _V([m[0][key]] * n + [m[1][key]] * n) for key in m[0] if key != "bds"}
    mk["bds"] = [_V([m[0]["bds"][i]] * n + [m[1]["bds"][i]] * n) for i in range(len(m[0]["bds"]))]
    return mk


def _tri_inv(a, mk):
    eyef = mk["eyef"]
    bds = mk["bds"]
    a8 = _vwhere(bds[0], a, 0.0)
    a2 = _vhdot(a8, a8)
    a4 = _vhdot(a2, a2)
    t = _vhdot(_vhdot(eyef - a8, eyef + a2), eyef + a4)
    for inner, outer in zip(bds, bds[1:] + [None]):
        off = _vnot(inner) if outer is None else (outer & _vnot(inner))
        low = _vwhere(off, a, 0.0)
        t = t - _vhdot(_vhdot(t, low), t)
    return t


def _chunk_prep(q, k, v, g_row, b_row, mk):
    dv = GDN_DK
    g_col = _vsum(mk["eyef"] * g_row, axis=1, keepdims=True)
    b_col = _vsum(mk["eyef"] * b_row, axis=1, keepdims=True)
    gc_col = _vsum(mk["mif"] * g_row, axis=1, keepdims=True)
    gc_row = _vsum(mk["mitf"] * g_col, axis=0, keepdims=True)
    gl = _vsum(g_row, axis=1, keepdims=True)
    decay = _vwhere(mk["mi"], _vexp(_vwhere(mk["mi"], gc_col - gc_row, 0.0)), 0.0)
    eg = _vexp(gc_col)
    e2 = _vexp(gl - gc_col)
    egl = _vexp(gl)
    kb = k * b_col
    pm = _vbdot_nt(kb, k)
    a = _vwhere(mk["ms"], pm * decay, 0.0)
    t = _tri_inv(a, mk)
    sol = _vhdot(t, _vcat(v * b_col, kb * eg))
    u, w = _vlo(sol, dv), _vhi(sol, dv)
    qm = _vbdot_nt(q, k)
    return dict(b_col=b_col, decay=decay, eg=eg, e2=e2, egl=egl, kb=kb, pm=pm, t=t, u=u, w=w,
                qm=qm, intra=qm * decay, qd=q * eg, kd=k * e2)


def _chunk_fwd_step(p, state):
    v_new = p["u"] - _vbdot(p["w"], state)
    o = _vbdot(p["qd"], state) + _vbdot(p["intra"], v_new)
    new_state = state * p["egl"] + _vbdot_tn(p["kd"], v_new)
    return o, new_state


def _chunk_bwd_step(q, k, v, p, mk, state, dso, do):
    dv_dim = GDN_DK
    v_new = p["u"] - _vbdot(p["w"], state)
    dvn = _vbdot_tn(p["intra"], do) + _vbdot(p["kd"], dso)
    dintra = _vbdot_nt(do, v_new)
    dqd = _vbdot_nt(do, state)
    ds = p["egl"] * dso + _vbdot_tn(p["qd"], do) - _vbdot_tn(p["w"], dvn)
    dkd = _vbdot_nt(v_new, dso)
    dgl = _vsum(_vsum(dso * state, axis=1, keepdims=True), axis=0, keepdims=True) * p["egl"]
    dw = -_vbdot_nt(dvn, state)
    drhs = _vhdot_tn(p["t"], _vcat(dvn, dw))
    dru, drw = _vlo(drhs, dv_dim), _vhi(drhs, dv_dim)
    da = -_vwhere(mk["ms"], _vbdot_nt(drhs, _vcat(p["u"], p["w"])), 0.0)
    b_col = p["b_col"]
    dv = dru * b_col
    dbeta = _vsum(dru * v, axis=1, keepdims=True)
    dkb = drw * p["eg"]
    deg = _vsum(drw * p["kb"], axis=1, keepdims=True)
    dp = da * p["decay"]
    ddecay = da * p["pm"]
    dkb = dkb + _vbdot(dp, k)
    dk = _vbdot_tn(dp, p["kb"])
    dqm = dintra * p["decay"]
    ddecay = ddecay + dintra * p["qm"]
    dq = _vbdot(dqm, k)
    dk = dk + _vbdot_tn(dqm, q)
    dd = ddecay * p["decay"]
    dgc_col = _vsum(dd, axis=1, keepdims=True)
    dgc_row = -_vsum(dd, axis=0, keepdims=True)
    dq = dq + dqd * p["eg"]
    deg = deg + _vsum(dqd * q, axis=1, keepdims=True)
    dk = dk + dkd * p["e2"]
    de2 = _vsum(dkd * k, axis=1, keepdims=True) * p["e2"]
    dgl = dgl + _vsum(de2, axis=0, keepdims=True)
    dgc_col = dgc_col - de2 + deg * p["eg"]
    dk = dk + dkb * b_col
    dbeta = dbeta + _vsum(dkb * k, axis=1, keepdims=True)
    dgc_col = dgc_col + _vsum(mk["eyef"] * dgc_row, axis=1, keepdims=True)
    dg_row = _vsum(mk["mif"] * dgc_col, axis=0, keepdims=True) + dgl
    dbeta_row = _vsum(mk["eyef"] * dbeta, axis=0, keepdims=True)
    return dq, dk, dv, dg_row, dbeta_row, ds


def _gdn_chunk_fwd(qkvn, g5, b5, *, name, comm=None):
    s = qkvn.shape[0]
    c = GDN_CHUNK
    nc = s // c
    h_, dk = GDN_HEADS, GDN_DK

    def body(*refs):
        n = pl.program_id(0)
        ins, outs, (st_scr,) = _comm_hooks(comm, refs, 6, 4, 1, n == 0, n == (3 * nc) // 4, n == nc - 1)
        x_refs, g_refs, b_refs = ins[0:2], ins[2:4], ins[4:6]
        o_refs, st_refs = outs[0:2], outs[2:4]

        @pl.when(n == 0)
        def _():
            st_scr[...] = jnp.zeros_like(st_scr)

        ch = [(d, h) for d in range(2) for h in range(h_)]
        mk = _both_masks(h_)
        q, k, v = (_V(x_refs[d][:, (t * h_ + h) * dk:(t * h_ + h + 1) * dk] for d, h in ch) for t in range(3))
        g, b = (_V(r[d][0, h, 0] for d, h in ch) for r in (g_refs, b_refs))
        state = _V(st_scr[d * h_ + h] for d, h in ch)
        o, new_state = _chunk_fwd_step(_chunk_prep(q, k, v, g, b, mk), state)
        for i, (d, h) in enumerate(ch):
            st_refs[d][h, 0] = state.xs[i]
            st_scr[d * h_ + h] = new_state.xs[i]
            o_refs[d][:, h * dk:(h + 1) * dk] = o.xs[i]

    ce = (lambda n: n, lambda n: nc - 1 - n)
    xs = [pl.BlockSpec((c, 3 * h_ * dk), lambda n, d=d: (ce[d](n), 0)) for d in range(2)]
    gates = [pl.BlockSpec((1, h_, 1, 1, c), lambda n, d=d: (d, 0, ce[d](n), 0, 0)) for d in range(2)]
    os_ = [pl.BlockSpec((c, h_ * dk), lambda n, d=d: (ce[d](n), 0)) for d in range(2)]
    sts = [pl.BlockSpec((h_, 1, dk, dk), lambda n, d=d: (0, ce[d](n), 0, 0)) for d in range(2)]
    c_in, c_out, c_shape, c_scr = _comm_specs(comm)
    res = pl.pallas_call(
        body, name=name, grid=(nc,),
        in_specs=xs + gates + gates + c_in,
        out_specs=os_ + sts + c_out,
        out_shape=[jax.ShapeDtypeStruct((s, h_ * dk), F32)] * 2 + [jax.ShapeDtypeStruct((h_, nc, dk, dk), F32)] * 2 + c_shape,
        scratch_shapes=[pltpu.VMEM((2 * h_, dk, dk), F32)] + c_scr,
        compiler_params=_cparams(("arbitrary",)),
    )(qkvn, qkvn, g5, g5, b5, b5, *(comm.arrays if comm else []))
    return res[0:2], res[2:4], res[4:]


def _gdn_chunk_bwd(qkvn, g5, b5, states, do, *, name, comm=None):
    s = qkvn.shape[0]
    c = GDN_CHUNK
    nc = s // c
    h_, dk = GDN_HEADS, GDN_DK

    def body(*refs):
        i = pl.program_id(0)
        ins, outs, (ds_scr,) = _comm_hooks(comm, refs, 10, 6, 1, i == 0, i == nc // 2, i == nc - 1)
        x_refs, g_refs, b_refs, st_refs, do_refs = ins[0:2], ins[2:4], ins[4:6], ins[6:8], ins[8:10]
        dx_refs, dg_refs, db_refs = outs[0:2], outs[2:4], outs[4:6]

        @pl.when(i == 0)
        def _():
            ds_scr[...] = jnp.zeros_like(ds_scr)

        ch = [(d, h) for d in range(2) for h in range(h_)]
        mk = _both_masks(h_)
        q, k, v = (_V(x_refs[d][:, (t * h_ + h) * dk:(t * h_ + h + 1) * dk] for d, h in ch) for t in range(3))
        g, b = (_V(r[d][0, h, 0] for d, h in ch) for r in (g_refs, b_refs))
        state = _V(st_refs[d][h, 0] for d, h in ch)
        dso = _V(ds_scr[d * h_ + h] for d, h in ch)
        dov = _V(do_refs[d][:, h * dk:(h + 1) * dk] for d, h in ch)
        res = _chunk_bwd_step(q, k, v, _chunk_prep(q, k, v, g, b, mk), mk, state, dso, dov)
        for (d, h), (dq, dkk, dvv, dg_r, db_r, ds) in zip(ch, zip(*[r.xs for r in res])):
            ds_scr[d * h_ + h] = ds
            dg_refs[d][h, 0] = dg_r
            db_refs[d][h, 0] = db_r
            for t, val in enumerate((dq, dkk, dvv)):
                dx_refs[d][:, (t * h_ + h) * dk:(t * h_ + h + 1) * dk] = val

    ce = (lambda i: nc - 1 - i, lambda i: i)
    both = lambda mk_spec: [mk_spec(d) for d in range(2)]
    xs = both(lambda d: pl.BlockSpec((c, 3 * h_ * dk), lambda i: (ce[d](i), 0)))
    gates = both(lambda d: pl.BlockSpec((1, h_, 1, 1, c), lambda i: (d, 0, ce[d](i), 0, 0)))
    sts = both(lambda d: pl.BlockSpec((h_, 1, dk, dk), lambda i: (0, ce[d](i), 0, 0)))
    dos = both(lambda d: pl.BlockSpec((c, h_ * dk), lambda i: (ce[d](i), 0)))
    gouts = both(lambda d: pl.BlockSpec((h_, 1, 1, c), lambda i: (0, ce[d](i), 0, 0)))
    c_in, c_out, c_shape, c_scr = _comm_specs(comm)
    res = pl.pallas_call(
        body, name=name, grid=(nc,),
        in_specs=xs + gates + gates + sts + dos + c_in,
        out_specs=xs + gouts + gouts + c_out,
        out_shape=[jax.ShapeDtypeStruct((s, 3 * h_ * dk), F32)] * 2
        + [jax.ShapeDtypeStruct((h_, nc, 1, c), F32)] * 4 + c_shape,
        scratch_shapes=[pltpu.VMEM((2 * h_, dk, dk), F32)] + c_scr,
        compiler_params=_cparams(("arbitrary",)),
    )(qkvn, qkvn, g5, g5, b5, b5, states[0], states[1], do, do, *(comm.arrays if comm else []))
    return res[0:2], jnp.stack(res[2:4]), jnp.stack(res[4:6]), res[6:]


def _gdn_post_fwd(o, z, norm_w, *, name):
    s = o[0].shape[0]
    h_, dk = GDN_HEADS, GDN_DK

    def body(of_ref, ob_ref, z_ref, w_ref, a_ref):
        ov = of_ref[...] + ob_ref[...]
        zv = z_ref[...]
        r = lax.rsqrt(jnp.mean(ov * ov, axis=-1, keepdims=True) + RMS_EPS)
        a_ref[...] = (ov * r * w_ref[...] * (zv * _sigmoid(zv))).astype(a_ref.dtype)

    col = pl.BlockSpec((s, dk), lambda h: (0, h))
    return pl.pallas_call(
        body, name=name, grid=(h_,),
        in_specs=[col, col, pl.BlockSpec((s, dk), lambda h: (0, 3 * h_ + h)), pl.BlockSpec((1, dk), lambda h: (0, 0))],
        out_specs=col,
        out_shape=jax.ShapeDtypeStruct((s, h_ * dk), BF16),
        compiler_params=_cparams(("parallel",)),
    )(o[0], o[1], z, norm_w.reshape(1, dk))


def _gdn_post_bwd(o, z, norm_w, dact, *, name):
    s = o[0].shape[0]
    h_, dk = GDN_HEADS, GDN_DK

    def body(of_ref, ob_ref, z_ref, w_ref, da_ref, do_ref, dz_ref, dw_ref):
        h = pl.program_id(0)
        ov = of_ref[...] + ob_ref[...]
        zv = z_ref[...]
        wv = w_ref[...]
        dav = da_ref[...]
        r = lax.rsqrt(jnp.mean(ov * ov, axis=-1, keepdims=True) + RMS_EPS)
        nrm = ov * r
        sg = _sigmoid(zv)
        sz = zv * sg
        dn = dav * wv * sz
        do_ref[...] = r * (dn - nrm * jnp.mean(dn * nrm, axis=-1, keepdims=True))
        dz_ref[...] = (dav * nrm * wv * (sg * (1.0 + zv * (1.0 - sg)))).astype(dz_ref.dtype)
        part = jnp.sum(dav * nrm * sz, axis=0, keepdims=True)

        @pl.when(h == 0)
        def _():
            dw_ref[...] = part

        @pl.when(h > 0)
        def _():
            dw_ref[...] += part

    col = pl.BlockSpec((s, dk), lambda h: (0, h))
    vec = pl.BlockSpec((1, dk), lambda h: (0, 0))
    return pl.pallas_call(
        body, name=name, grid=(h_,),
        in_specs=[col, col, pl.BlockSpec((s, dk), lambda h: (0, 3 * h_ + h)), vec, col],
        out_specs=[col, pl.BlockSpec((s, dk), lambda h: (0, 3 * h_ + h)), vec],
        out_shape=[jax.ShapeDtypeStruct((s, h_ * dk), F32), jax.ShapeDtypeStruct((s, GDN_MAIN), BF16),
                   jax.ShapeDtypeStruct((1, dk), F32)],
        compiler_params=_cparams(("arbitrary",)),
    )(o[0], o[1], z, norm_w.reshape(1, dk), dact)


def _rel_bucket(rel):
    nb = REL_BUCKETS // 2
    max_exact = nb // 2
    ret = jnp.where(rel > 0, nb, 0)
    n = jnp.abs(rel)
    nf = jnp.maximum(n, 1).astype(F32)
    large = max_exact + (jnp.log(nf / max_exact) / math.log(REL_MAX_DIST / max_exact)
                         * (nb - max_exact)).astype(jnp.int32)
    large = jnp.minimum(large, nb - 1)
    return ret + jnp.where(n < max_exact, n, large)


def _bucket_onehot():
    half = DSWA_HALF
    outs = []
    for dil in DSWA_DILS:
        rel = (jnp.arange(3 * half)[None, :] - half - jnp.arange(half)[:, None]) * dil
        outs.append(jax.nn.one_hot(_rel_bucket(rel).reshape(-1), REL_BUCKETS, dtype=F32, axis=0))
    return jnp.stack(outs)


def _head_group_select(vals):
    rows = lax.broadcasted_iota(jnp.int32, vals[0].shape, 0)
    return jnp.where(rows < DSWA_HG, vals[0], jnp.where(rows < 2 * DSWA_HG, vals[1], vals[2]))


def _dswa_bias(table_t, onehot, *, name):
    p = onehot.shape[-1]

    def body(t_ref, oh_ref, b_ref):
        b_ref[...] = _head_group_select([_hdot(t_ref[...], oh_ref[g]) for g in range(3)])

    return pl.pallas_call(body, name=name, out_shape=jax.ShapeDtypeStruct((DSWA_HEADS, p), F32),
                          compiler_params=_cparams())(table_t, onehot)


def _dswa_dtable(dbias, onehot, *, name):
    def body(d_ref, oh_ref, t_ref):
        t_ref[...] = _head_group_select([_hdot_nt(d_ref[...], oh_ref[g]) for g in range(3)])

    return pl.pallas_call(body, name=name, out_shape=jax.ShapeDtypeStruct((DSWA_HEADS, REL_BUCKETS), F32),
                          compiler_params=_cparams())(dbias, onehot)


def _rows(start, dil):
    if dil == 1:
        return pl.ds(pl.multiple_of(start, DSWA_HALF), DSWA_HALF)
    return pl.ds(start, DSWA_HALF, stride=dil)


def _attn_blocks(it, s, dil):
    half = DSWA_HALF
    nbs = s // half // dil
    ii = lax.broadcasted_iota(jnp.int32, (half, 3 * half), 0)
    jj = lax.broadcasted_iota(jnp.int32, (half, 3 * half), 1)
    band = jnp.abs(jj - half - ii) <= half
    out = []
    for u in range(DSWA_UNROLL):
        blk = it * DSWA_UNROLL + u
        r, b = blk // nbs, blk % nbs
        own = r + dil * half * b
        prev = own - jnp.where(b > 0, dil * half, 0)
        nxt = own + jnp.where(b < nbs - 1, dil * half, 0)
        ok = band & ((jj >= half) | (b > 0)) & ((jj < 2 * half) | (b < nbs - 1))
        out.append(((prev, own, nxt), ok))
    return out


def _attn_chains(q_ref, k_ref, v_ref, blocks, dil):
    lane = lax.broadcasted_iota(jnp.int32, (DSWA_HALF, 2 * DSWA_E), 1)
    qm, kw, vw, valid, hmask = [], [], [], [], []
    for (prev, own, nxt), ok in blocks:
        q = q_ref[_rows(own, dil), :].astype(BF16)
        k = jnp.concatenate([k_ref[_rows(st, dil), :] for st in (prev, own, nxt)], axis=0).astype(BF16)
        v = jnp.concatenate([v_ref[_rows(st, dil), :] for st in (prev, own, nxt)], axis=0).astype(BF16)
        for hd in range(2):
            mine = (lane < DSWA_E) if hd == 0 else (lane >= DSWA_E)
            qm.append(jnp.where(mine, q, jnp.zeros_like(q)))
            kw.append(k)
            vw.append(v)
            valid.append(ok)
            hmask.append(mine)
    return _V(qm), _V(kw), _V(vw), _V(valid), _V(hmask)


def _per_group(pr, fn):
    for gi, dil in enumerate(DSWA_DILS):
        pl.when(pr // DSWA_PG == gi)(functools.partial(fn, dil))


_vmax, _vlog = _lift(jnp.max), _lift(jnp.log)


def _dswa_attn_fwd(qkv, bias, *, name, comm=None):
    s = qkv.shape[0]
    half, e = DSWA_HALF, DSWA_E
    npair = DSWA_HEADS // 2

    def body(*refs):
        pr = pl.program_id(0)
        (q_ref, k_ref, v_ref, bias_ref), (o_ref, lse_ref), _ = _comm_hooks(
            comm, refs, 4, 2, 0, pr == 0, pr == (3 * npair) // 4, pr == npair - 1)
        bias_v = _V([bias_ref[0], bias_ref[1]] * DSWA_UNROLL)

        def run(dil):
            def step(it, carry):
                blocks = _attn_blocks(it, s, dil)
                qm, kw, vw, valid, hmask = _attn_chains(q_ref, k_ref, v_ref, blocks, dil)
                sc = _vwhere(valid, _vbdot_nt(qm, kw) * (e ** -0.5) + bias_v, NEG_INF)
                m = _vmax(sc, axis=-1, keepdims=True)
                p = _vexp(sc - m)
                l = _vsum(p, axis=-1, keepdims=True)
                o = _vbdot(p * (1.0 / l), vw)
                lse = m + _vlog(l)
                for u, ((_, own, _), _) in enumerate(blocks):
                    is_a = hmask.xs[2 * u]
                    o_ref[_rows(own, dil), :] = jnp.where(is_a, o.xs[2 * u], o.xs[2 * u + 1])
                    lse_ref[_rows(own, dil), :] = jnp.where(is_a, lse.xs[2 * u], lse.xs[2 * u + 1])
                return carry

            lax.fori_loop(0, s // half // DSWA_UNROLL, step, 0)

        _per_group(pr, run)

    col = lambda t: pl.BlockSpec((s, 2 * e), lambda p: (0, t * npair + p))
    pair = pl.BlockSpec((s, 2 * e), lambda p: (0, p))
    c_in, c_out, c_shape, c_scr = _comm_specs(comm)
    res = pl.pallas_call(
        body, name=name, grid=(npair,),
        in_specs=[col(0), col(1), col(2), pl.BlockSpec((2, half, 3 * half), lambda p: (p, 0, 0))] + c_in,
        out_specs=[pair, pair] + c_out,
        out_shape=[jax.ShapeDtypeStruct((s, npair * 2 * e), F32)] * 2 + c_shape,
        scratch_shapes=c_scr,
        compiler_params=_cparams(("arbitrary",)),
    )(qkv, qkv, qkv, bias, *(comm.arrays if comm else []))
    return res[0], res[1], res[2:]


def _dswa_attn_bwd(qkv, bias, lse, do, corr, *, name, comm=None):
    s = qkv.shape[0]
    half, e = DSWA_HALF, DSWA_E
    npair = DSWA_HEADS // 2
    w = 2 * e

    def body(*refs):
        pr = pl.program_id(0)
        (q_ref, k_ref, v_ref, bias_ref, lse_ref, do_ref, corr_ref), (dq_ref, dk_ref, dv_ref, db_ref), _ = _comm_hooks(
            comm, refs, 7, 4, 0, pr == 0, pr == npair // 2, pr == npair - 1)
        bias_v = _V([bias_ref[0], bias_ref[1]] * DSWA_UNROLL)
        dk_ref[...] = jnp.zeros_like(dk_ref)
        dv_ref[...] = jnp.zeros_like(dv_ref)

        def run(dil):
            def step(it, dbias):
                blocks = _attn_blocks(it, s, dil)
                qm, kw, vw, valid, hmask = _attn_chains(q_ref, k_ref, v_ref, blocks, dil)
                hd = [0, 1] * DSWA_UNROLL
                rows = [_rows(own, dil) for (_, own, _), _ in blocks for _ in range(2)]
                lse_c = _V(lse_ref[rw, :][:, h * e:h * e + 1] for rw, h in zip(rows, hd))
                corr_c = _V(corr_ref[rw, :][:, h * e:h * e + 1] for rw, h in zip(rows, hd))
                dov = _vwhere(hmask, _V(do_ref[rw, :] for rw in rows), 0.0)
                sc = _vbdot_nt(qm, kw) * (e ** -0.5) + bias_v
                p = _vwhere(valid, _vexp(_vwhere(valid, sc, 0.0) - lse_c), 0.0)
                dsc = p * (_vbdot_nt(dov, vw) + corr_c)
                dq = _vbdot(dsc, kw) * (e ** -0.5)
                dkc = _vbdot_tn(dsc, qm) * (e ** -0.5)
                dvc = _vbdot_tn(p, dov)
                for u, (starts, _) in enumerate(blocks):
                    dq_ref[_rows(starts[1], dil), :] = jnp.where(hmask.xs[2 * u], dq.xs[2 * u], dq.xs[2 * u + 1])
                    dk_u = dkc.xs[2 * u] + dkc.xs[2 * u + 1]
                    dv_u = dvc.xs[2 * u] + dvc.xs[2 * u + 1]
                    for t, st in enumerate(starts):
                        dk_ref[_rows(st, dil), :] += dk_u[t * half:(t + 1) * half]
                        dv_ref[_rows(st, dil), :] += dv_u[t * half:(t + 1) * half]
                da, db = dbias
                for u in range(DSWA_UNROLL):
                    da, db = da + dsc.xs[2 * u], db + dsc.xs[2 * u + 1]
                return da, db

            zero = jnp.zeros((half, 3 * half), F32)
            da, db = lax.fori_loop(0, s // half // DSWA_UNROLL, step, (zero, zero))
            db_ref[0] = da
            db_ref[1] = db

        _per_group(pr, run)

    col = lambda t: pl.BlockSpec((s, w), lambda p: (0, t * npair + p))
    ps = pl.BlockSpec((s, w), lambda p: (0, p))
    bs = pl.BlockSpec((2, half, 3 * half), lambda p: (p, 0, 0))
    c_in, c_out, c_shape, c_scr = _comm_specs(comm)
    res = pl.pallas_call(
        body, name=name, grid=(npair,),
        in_specs=[col(0), col(1), col(2), bs, ps, ps, ps] + c_in,
        out_specs=[ps, ps, ps, bs] + c_out,
        out_shape=[jax.ShapeDtypeStruct((s, npair * w), F32)] * 3
        + [jax.ShapeDtypeStruct((DSWA_HEADS, half, 3 * half), F32)] + c_shape,
        scratch_shapes=c_scr,
        compiler_params=_cparams(("arbitrary",)),
    )(qkv, qkv, qkv, bias, lse, do, corr, *(comm.arrays if comm else []))
    return res[0], res[1], res[2], res[3], res[4:]


def _pair_cols(g, j):
    w = 2 * DSWA_E
    return slice((g * DSWA_PG + j) * w, (g * DSWA_PG + j + 1) * w)


def _group_weights(l_ref, j):
    ls = [l_ref[:, _pair_cols(g, j)] for g in range(3)]
    m = jnp.maximum(jnp.maximum(ls[0], ls[1]), ls[2])
    es = [jnp.exp(x - m) for x in ls]
    inv = 1.0 / (es[0] + es[1] + es[2])
    return [x * inv for x in es]


def _dswa_combine_fwd(o, lse, *, name):
    s, wd = o.shape
    tr = _pick(s, (512, 256, 128))

    def body(o_ref, l_ref, c_ref):
        for j in range(DSWA_PG):
            al = _group_weights(l_ref, j)
            for g in range(3):
                c_ref[:, _pair_cols(g, j)] = (o_ref[:, _pair_cols(g, j)] * al[g]).astype(c_ref.dtype)

    row = pl.BlockSpec((tr, wd), lambda i: (i, 0))
    return pl.pallas_call(
        body, name=name, grid=(s // tr,),
        in_specs=[row, row], out_specs=row,
        out_shape=jax.ShapeDtypeStruct(o.shape, BF16),
        compiler_params=_cparams(("parallel",)),
    )(o, lse)


def _dswa_combine_bwd(o, lse, dc, *, name):
    s, wd = o.shape
    tr = _pick(s, (512, 256, 128))

    def body(o_ref, l_ref, dc_ref, do_ref, corr_ref):
        lane = lax.broadcasted_iota(jnp.int32, (tr, 2 * DSWA_E), 1)
        is_a = lane < DSWA_E
        for j in range(DSWA_PG):
            al = _group_weights(l_ref, j)
            tot = jnp.zeros((tr, 2 * DSWA_E), F32)
            for g in range(3):
                cols = _pair_cols(g, j)
                dcv = dc_ref[:, cols]
                do_ref[:, cols] = dcv * al[g]
                prod = dcv * o_ref[:, cols]
                dal = jnp.where(is_a, jnp.sum(jnp.where(is_a, prod, 0.0), axis=-1, keepdims=True),
                                jnp.sum(jnp.where(is_a, 0.0, prod), axis=-1, keepdims=True))
                tot = tot + al[g] * dal
            for g in range(3):
                corr_ref[:, _pair_cols(g, j)] = -al[g] * tot

    row = pl.BlockSpec((tr, wd), lambda i: (i, 0))
    return pl.pallas_call(
        body, name=name, grid=(s // tr,),
        in_specs=[row, row, row], out_specs=[row, row],
        out_shape=[jax.ShapeDtypeStruct(o.shape, F32)] * 2,
        compiler_params=_cparams(("parallel",)),
    )(o, lse, dc)


class _Comm:
    def __init__(self, mode, arrays, kinds=None):
        self.mode, self.arrays, self.kinds = mode, list(arrays), kinds
        self.n = len(self.arrays)

    def out_shapes(self):
        if self.mode == "exchange":
            return [jax.ShapeDtypeStruct(x.shape, x.dtype) for x in self.arrays]
        shapes = []
        for x, kd in zip(self.arrays, self.kinds):
            shp = list(x.shape)
            if kd == "stack":
                shp = [N_DEV] + shp
            else:
                shp[-2 if kd == "rows" else -1] *= N_DEV
            shapes.append(jax.ShapeDtypeStruct(tuple(shp), x.dtype))
        return shapes

    def scratch(self):
        return [pltpu.SemaphoreType.DMA((7 * self.n,)), pltpu.SemaphoreType.DMA((7 * self.n,)),
                pltpu.SemaphoreType.DMA((self.n,))]

    def bind(self, in_refs, out_refs, sems):
        self.x, self.o = in_refs, out_refs
        self.send_sems, self.recv_sems, self.local_sems = sems
        self.pos = (lax.axis_index("x"), lax.axis_index("y"), lax.axis_index("c"))

    def _slot(self, i, px, py, pc):
        p = 4 * px + 2 * py + pc
        kd = self.kinds[i]
        if kd == "stack":
            return self.o[i].at[p]
        nd = len(self.x[i].shape)
        ax = nd - 2 if kd == "rows" else nd - 1
        size = self.x[i].shape[ax]
        idx = tuple(pl.ds(p * size, size) if a == ax else slice(None) for a in range(nd))
        return self.o[i].at[idx]

    def _gcopy(self, i, k, block, to, src=None):
        return pltpu.make_async_remote_copy(
            src_ref=self._slot(i, *block) if src is None else src, dst_ref=self._slot(i, *block),
            send_sem=self.send_sems.at[7 * i + k], recv_sem=self.recv_sems.at[7 * i + k],
            device_id=to, device_id_type=pl.DeviceIdType.MESH)

    def _chips(self):
        mx, my, _ = self.pos
        return [(1 - mx, my), (mx, 1 - my), (1 - mx, 1 - my)]

    def _xcopies(self):
        mx, my, mc = self.pos
        me = 4 * mx + 2 * my + mc
        copies = []
        for k in range(1, N_DEV):
            px = 1 - mx if (k >> 2) & 1 else mx
            py = 1 - my if (k >> 1) & 1 else my
            pc = 1 - mc if k & 1 else mc
            for i in range(self.n):
                copies.append(pltpu.make_async_remote_copy(
                    src_ref=self.x[i].at[4 * px + 2 * py + pc], dst_ref=self.o[i].at[me],
                    send_sem=self.send_sems.at[7 * i + k - 1], recv_sem=self.recv_sems.at[7 * i + k - 1],
                    device_id=(px, py, pc), device_id_type=pl.DeviceIdType.MESH))
        return copies

    def _local(self):
        mx, my, mc = self.pos
        if self.mode == "exchange":
            me = 4 * mx + 2 * my + mc
            return [pltpu.make_async_copy(self.x[i].at[me], self.o[i].at[me], self.local_sems.at[i]) for i in range(self.n)]
        return [pltpu.make_async_copy(self.x[i], self._slot(i, mx, my, mc), self.local_sems.at[i]) for i in range(self.n)]

    def _first(self):
        mx, my, mc = self.pos
        me, sibling = (mx, my, mc), (mx, my, 1 - mc)
        first = [self._gcopy(i, 0, me, sibling, src=self.x[i]) for i in range(self.n)]
        first += [self._gcopy(i, 1 + j, me, (*chip, mc), src=self.x[i]) for j, chip in enumerate(self._chips())
                  for i in range(self.n)]
        return first

    def _passed(self):
        mx, my, mc = self.pos
        return [self._gcopy(i, 4 + j, (*chip, mc), (mx, my, 1 - mc)) for j, chip in enumerate(self._chips())
                for i in range(self.n)]

    def start(self):
        for cp in self._local() + (self._xcopies() if self.mode == "exchange" else self._first()):
            cp.start()

    def mid(self):
        if self.mode == "exchange":
            return
        mx, my, mc = self.pos
        passed = self._passed()
        for j, chip in enumerate(self._chips()):
            for i in range(self.n):
                self._gcopy(i, 1 + j, (*chip, mc), (mx, my, mc)).wait_recv()
                passed[j * self.n + i].start()

    def end(self):
        mx, my, mc = self.pos
        if self.mode == "exchange":
            copies = self._xcopies()
            for cp in copies:
                cp.wait_recv()
            for cp in copies:
                cp.wait_send()
        else:
            for i in range(self.n):
                self._gcopy(i, 0, (mx, my, 1 - mc), (mx, my, mc)).wait_recv()
                for j, chip in enumerate(self._chips()):
                    self._gcopy(i, 4 + j, (*chip, 1 - mc), (mx, my, mc)).wait_recv()
            for cp in self._first() + self._passed():
                cp.wait_send()
        for cp in self._local():
            cp.wait()

    def run(self, *, name):
        n = self.n

        def body(*refs):
            self.bind(refs[:n], refs[n:2 * n], refs[2 * n:])
            self.start()
            self.mid()
            self.end()

        anyspec = pl.BlockSpec(memory_space=pl.ANY)
        return pl.pallas_call(body, name=name, in_specs=[anyspec] * n, out_specs=[anyspec] * n,
                              out_shape=self.out_shapes(), scratch_shapes=self.scratch())(*self.arrays)


def _comm_specs(comm):
    if comm is None:
        return [], [], [], []
    anyspec = pl.BlockSpec(memory_space=pl.ANY)
    return [anyspec] * comm.n, [anyspec] * comm.n, comm.out_shapes(), comm.scratch()


def _comm_hooks(comm, refs, n_in, n_out, n_scr, first, mid, last):
    if comm is None:
        return refs[:n_in], refs[n_in:n_in + n_out], refs[n_in + n_out:]
    c = comm.n
    ins, cin = refs[:n_in], refs[n_in:n_in + c]
    outs, cout = refs[n_in + c:n_in + c + n_out], refs[n_in + c + n_out:n_in + 2 * c + n_out]
    scr, sems = refs[n_in + 2 * c + n_out:n_in + 2 * c + n_out + n_scr], refs[n_in + 2 * c + n_out + n_scr:]
    comm.bind(cin, cout, sems)
    pl.when(first)(comm.start)
    pl.when(mid)(comm.mid)
    pl.when(last)(comm.end)
    return ins, outs, scr


def _adamw_update(g, w, m, v):
    mn = ADAM_B1 * m + (1.0 - ADAM_B1) * g
    vn = ADAM_B2 * v + (1.0 - ADAM_B2) * (g * g)
    m_hat = mn / (1.0 - ADAM_B1 ** ADAM_STEP)
    v_hat = vn / (1.0 - ADAM_B2 ** ADAM_STEP)
    return -ADAM_LR * (m_hat / (jnp.sqrt(v_hat) + ADAM_EPS) + ADAM_WD * w), mn, vn


def _adamw_layers(recvs, w, m, v, *, name):
    nl, ks, ns = w.shape
    tr = _pick(ks, (64, 48))

    def body(*refs):
        rv_refs = refs[:nl]
        w_ref, m_ref, v_ref, g_ref, d_ref, nm_ref, nv_ref = refs[nl:]
        for l in range(nl):
            g = rv_refs[l][0].astype(F32)
            for q in range(1, N_DEV):
                g = g + rv_refs[l][q].astype(F32)
            delta, mn, vn = _adamw_update(g, w_ref[l], m_ref[l], v_ref[l])
            g_ref[l] = g
            d_ref[l] = delta
            nm_ref[l] = mn
            nv_ref[l] = vn

    row = pl.BlockSpec((nl, tr, ns), lambda i: (0, i, 0))
    return pl.pallas_call(
        body, name=name, grid=(ks // tr,),
        in_specs=[pl.BlockSpec((N_DEV, tr, ns), lambda i: (0, i, 0))] * nl + [row] * 3,
        out_specs=[row] * 4,
        out_shape=[jax.ShapeDtypeStruct((nl, ks, ns), F32)] * 4,
        compiler_params=_cparams(("parallel",)),
    )(*recvs, w, m, v)


def _adamw_reduce(recv, w, m, v, *, name):
    r, c = w.shape
    tr = _pick(r, (128, 64, 8))

    def body(rv_ref, w_ref, m_ref, v_ref, g_ref, d_ref, nm_ref, nv_ref):
        g = rv_ref[0]
        for q in range(1, N_DEV):
            g = g + rv_ref[q]
        delta, mn, vn = _adamw_update(g, w_ref[...], m_ref[...], v_ref[...])
        g_ref[...] = g
        d_ref[...] = delta
        nm_ref[...] = mn
        nv_ref[...] = vn

    row = pl.BlockSpec((tr, c), lambda i: (i, 0))
    return pl.pallas_call(
        body, name=name, grid=(r // tr,),
        in_specs=[pl.BlockSpec((N_DEV, tr, c), lambda i: (0, i, 0)), row, row, row],
        out_specs=[row] * 4,
        out_shape=[jax.ShapeDtypeStruct((r, c), F32)] * 4,
        compiler_params=_cparams(("parallel",)),
    )(recv, w, m, v)


_BIG = ("gdn_w_in", "gdn_w_out", "dswa_w_in", "dswa_w_out", "mlp_w1", "mlp_w2")
_SMALL = ("gdn_conv_w", "norm_mix", "norm_mlp", "norm_final", "rel_bias", "gdn_a_log", "gdn_dt_bias", "gdn_norm_w")
_ORDER = ("norm_mix", "norm_mlp", "norm_final", "rel_bias", "gdn_w_in", "gdn_conv_w", "gdn_a_log", "gdn_dt_bias",
          "gdn_norm_w", "gdn_w_out", "dswa_w_in", "dswa_w_out", "mlp_w1", "mlp_w2")
_KIND = dict(gdn_w_in="stack", gdn_w_out="rows", dswa_w_in="stack", dswa_w_out="rows", mlp_w1="cols", mlp_w2="rows")


def _pack_rows(arrs, align):
    rows, counts = [], []
    for a in arrs:
        flat = a.reshape(-1)
        n = -(-flat.shape[0] // D_MODEL)
        flat = jnp.pad(flat, (0, n * D_MODEL - flat.shape[0]))
        rows.append(flat.reshape(n, D_MODEL))
        counts.append(n)
    out = jnp.concatenate(rows, axis=0)
    total = -(-out.shape[0] // align) * align
    return jnp.pad(out, ((0, total - out.shape[0]), (0, 0))), counts


def _unpack_rows(slab, shapes):
    outs, r = [], 0
    for shp in shapes:
        size = int(np.prod(shp))
        n = -(-size // D_MODEL)
        outs.append(slab[r:r + n].reshape(-1)[:size].reshape(shp))
        r += n
    return outs


def _col_shards(full, nshard):
    lead = full.shape[:-1]
    n = full.shape[-1] // nshard
    t = full.reshape(lead + (nshard, n))
    return jnp.moveaxis(t, -2, 0)


def _from_col_shards(g):
    t = jnp.moveaxis(g, 0, -2)
    return t.reshape(t.shape[:-2] + (t.shape[-2] * t.shape[-1],))


def kernel(x, norm_mix, norm_mlp, norm_final, rel_bias, gdn_w_in, gdn_conv_w, gdn_a_log, gdn_dt_bias, gdn_norm_w, gdn_w_out, dswa_w_in, dswa_w_out, mlp_w1, mlp_w2, loss_target, m_norm_mix, m_norm_mlp, m_norm_final, m_rel_bias, m_gdn_w_in, m_gdn_conv_w, m_gdn_a_log, m_gdn_dt_bias, m_gdn_norm_w, m_gdn_w_out, m_dswa_w_in, m_dswa_w_out, m_mlp_w1, m_mlp_w2, v_norm_mix, v_norm_mlp, v_norm_final, v_rel_bias, v_gdn_w_in, v_gdn_conv_w, v_gdn_a_log, v_gdn_dt_bias, v_gdn_norm_w, v_gdn_w_out, v_dswa_w_in, v_dswa_w_out, v_mlp_w1, v_mlp_w2):
    params = dict(norm_mix=norm_mix, norm_mlp=norm_mlp, norm_final=norm_final, rel_bias=rel_bias,
                  gdn_w_in=gdn_w_in, gdn_conv_w=gdn_conv_w, gdn_a_log=gdn_a_log, gdn_dt_bias=gdn_dt_bias,
                  gdn_norm_w=gdn_norm_w, gdn_w_out=gdn_w_out, dswa_w_in=dswa_w_in, dswa_w_out=dswa_w_out,
                  mlp_w1=mlp_w1, mlp_w2=mlp_w2)
    mom_m = dict(norm_mix=m_norm_mix, norm_mlp=m_norm_mlp, norm_final=m_norm_final, rel_bias=m_rel_bias,
                 gdn_w_in=m_gdn_w_in, gdn_conv_w=m_gdn_conv_w, gdn_a_log=m_gdn_a_log, gdn_dt_bias=m_gdn_dt_bias,
                 gdn_norm_w=m_gdn_norm_w, gdn_w_out=m_gdn_w_out, dswa_w_in=m_dswa_w_in, dswa_w_out=m_dswa_w_out,
                 mlp_w1=m_mlp_w1, mlp_w2=m_mlp_w2)
    mom_v = dict(norm_mix=v_norm_mix, norm_mlp=v_norm_mlp, norm_final=v_norm_final, rel_bias=v_rel_bias,
                 gdn_w_in=v_gdn_w_in, gdn_conv_w=v_gdn_conv_w, gdn_a_log=v_gdn_a_log, gdn_dt_bias=v_gdn_dt_bias,
                 gdn_norm_w=v_gdn_norm_w, gdn_w_out=v_gdn_w_out, dswa_w_in=v_dswa_w_in, dswa_w_out=v_dswa_w_out,
                 mlp_w1=v_mlp_w1, mlp_w2=v_mlp_w2)
    xs = x[0]
    target = loss_target[0]
    dist = _Dist(params)
    conv_tail, _ = _pack_rows([gdn_conv_w], 8)
    (conv_g,) = dist.put("start", dist.gather_comm("start", extra=[(conv_tail, "stack")]).run(name="ag_start"))
    conv_parts = [_unpack_rows(conv_g[dev], [gdn_conv_w.shape])[0] for dev in range(N_DEV)]
    conv_full = _from_col_shards(jnp.stack(conv_parts))[:, :, 0, :]

    loss_part, dcur, g_big, rep, g_conv = _local_step(
        xs, target, dict(norm_mix=norm_mix, norm_mlp=norm_mlp, norm_final=norm_final, rel_bias=rel_bias,
                         gdn_a_log=gdn_a_log, gdn_dt_bias=gdn_dt_bias, gdn_norm_w=gdn_norm_w), dist.full, conv_full, dist)
    loss = lax.psum(loss_part[0, 0], ("x", "y", "c"))
    grad_x = dcur[None]

    conv_dev = _col_shards(jnp.stack(g_conv)[:, :, None, :], N_DEV)
    small_send = jnp.stack([_pack_rows([conv_dev[dev]] + [rep[n] for n in _SMALL[1:]], 8)[0] for dev in range(N_DEV)])
    (small_recv,) = dist.got("end", dist.send_comm("end", g_big, extra=[small_send]).run(name="grad_exchange"))

    outs = {}
    for n in _BIG:
        recvs = [dist.recv[(n, l)] for l in range(params[n].shape[0])]
        res = _adamw_layers(recvs, params[n], mom_m[n], mom_v[n], name=f"adamw_{n}")
        for tag, t in zip(("grad", "delta", "new_m", "new_v"), res):
            outs[(tag, n)] = t
    w_slab, _ = _pack_rows([params[n] for n in _SMALL], 8)
    m_slab, _ = _pack_rows([mom_m[n] for n in _SMALL], 8)
    v_slab, _ = _pack_rows([mom_v[n] for n in _SMALL], 8)
    small = _adamw_reduce(small_recv, w_slab, m_slab, v_slab, name="adamw_small")
    shapes = [params[n].shape for n in _SMALL]
    for tag, slab in zip(("grad", "delta", "new_m", "new_v"), small):
        for n, t in zip(_SMALL, _unpack_rows(slab, shapes)):
            outs[(tag, n)] = t
    result = [loss, grad_x]
    for tag in ("grad", "delta", "new_m", "new_v"):
        result += [outs[(tag, n)] for n in _ORDER]
    return tuple(result)


_GATHER = {
    "start": (("gdn_w_in", 0),),
    "gdn_proj0": (("gdn_w_out", 0), ("mlp_w1", 0)),
    "chunk_fwd0": (("mlp_w2", 0), ("dswa_w_in", 0), ("dswa_w_out", 0), ("mlp_w1", 1)),
    "mlp_up0": (("mlp_w2", 1),),
    "mlp_down0": (("gdn_w_in", 1),),
    "attn_fwd1": (("gdn_w_out", 1), ("mlp_w1", 2), ("mlp_w2", 2)),
    "chunk_fwd2": (("dswa_w_in", 1), ("dswa_w_out", 1), ("mlp_w1", 3), ("mlp_w2", 3)),
}
_SEND = {
    "attn_bwd3": (("mlp_w1", 3), ("mlp_w2", 3)),
    "chunk_bwd2": (("dswa_w_in", 1), ("dswa_w_out", 1), ("mlp_w1", 2), ("mlp_w2", 2)),
    "attn_bwd1": (("mlp_w1", 1), ("mlp_w2", 1)),
    "chunk_bwd0": (("gdn_w_in", 1), ("gdn_w_out", 1), ("dswa_w_in", 0), ("dswa_w_out", 0), ("mlp_w2", 0)),
    "pre_bwd0": (("mlp_w1", 0), ("gdn_w_out", 0)),
    "gdn_proj_bwd0": (("gdn_w_in", 0),),
    "end": (),
}


class _Dist:
    def __init__(self, params):
        self.shards = {n: params[n].astype(BF16) for n in _BIG}
        self.full = {n: [None] * params[n].shape[0] for n in _BIG}
        self.recv = {}

    def gather_comm(self, tag, extra=()):
        if tag not in _GATHER:
            return None
        arrays = [self.shards[n][l] for n, l in _GATHER[tag]] + [a for a, _ in extra]
        return _Comm("gather", arrays, [_KIND[n] for n, _ in _GATHER[tag]] + [k for _, k in extra])

    def put(self, tag, outs):
        for (n, l), t in zip(_GATHER.get(tag, ()), outs):
            self.full[n][l] = _from_col_shards(t) if _KIND[n] == "stack" else t
        return outs[len(_GATHER.get(tag, ())):]

    def send_comm(self, tag, g_big, extra=()):
        if tag not in _SEND:
            return None
        arrays = [_col_shards(g_big[n][l], N_DEV) if _KIND[n] == "stack" else g_big[n][l] for n, l in _SEND[tag]]
        return _Comm("exchange", arrays + list(extra))

    def got(self, tag, outs):
        for item, t in zip(_SEND.get(tag, ()), outs):
            self.recv[item] = t
        return outs[len(_SEND.get(tag, ())):]


def _mm_gather(dist, tag, *args, **kw):
    comm = dist and dist.gather_comm(tag)
    if not comm:
        return _mm(*args, **kw)
    res, got = _mm(*args, comm=comm, **kw)
    dist.put(tag, got)
    return res


def _local_step(xs, target, sp, full, conv_full, dist=None):
    s = xs.shape[0]
    norm_mix, norm_mlp, norm_final = sp["norm_mix"], sp["norm_mlp"], sp["norm_final"]
    gdn_a_log, gdn_dt_bias, gdn_norm_w = sp["gdn_a_log"], sp["gdn_dt_bias"], sp["gdn_norm_w"]
    onehot = _bucket_onehot()
    table_t = sp["rel_bias"].T
    bias = _dswa_bias(table_t, onehot, name="dswa_bias").reshape(DSWA_HEADS, DSWA_HALF, 3 * DSWA_HALF)

    saved = []
    cur = xs
    for i in range(DEPTH):
        j = i // 2
        sv = dict(x_in=cur)
        h = _rms_fwd(cur, norm_mix[i], name=f"rms_mix_fwd{i}")
        sv["h"] = h
        if i % 2 == 0:
            w_in = full["gdn_w_in"][j]
            proj = _mm_gather(dist, f"gdn_proj{i}", h, w_in[:, :GDN_MAIN], name=f"gdn_proj{i}")
            ab = _mm(h, w_in[:, GDN_MAIN:], name=f"gdn_proj_ab{i}")
            qkvn = _gdn_pre_fwd(proj, conv_full[j], name=f"gdn_pre_fwd{i}")
            g_all, beta_all = _gdn_gate_fwd(ab[:, :2 * GDN_HEADS], ab[:, 2 * GDN_HEADS:], gdn_a_log[j], gdn_dt_bias[j],
                                            name=f"gdn_gate_fwd{i}")
            gshape = (2, GDN_HEADS, s // GDN_CHUNK, 1, GDN_CHUNK)
            g_row = g_all.T.reshape(gshape)
            b_row = beta_all.T.reshape(gshape)
            o, states, got = _gdn_chunk_fwd(qkvn, g_row, b_row, name=f"gdn_chunk_fwd{i}",
                                            comm=dist and dist.gather_comm(f"chunk_fwd{i}"))
            if dist:
                dist.put(f"chunk_fwd{i}", got)
            act = _gdn_post_fwd(o, proj, gdn_norm_w[j], name=f"gdn_post_fwd{i}")
            sv.update(proj=proj, ab=ab, qkvn=qkvn, g_row=g_row, b_row=b_row, o=o, states=states, act=act)
            w_out = full["gdn_w_out"][j]
        else:
            w_in = full["dswa_w_in"][j]
            qkv = _mm(h, w_in, name=f"dswa_proj{i}")
            o_n, lse_n, got = _dswa_attn_fwd(qkv, bias, name=f"dswa_attn_fwd{i}",
                                             comm=dist and dist.gather_comm(f"attn_fwd{i}"))
            if dist:
                dist.put(f"attn_fwd{i}", got)
            act = _dswa_combine_fwd(o_n, lse_n, name=f"dswa_comb_fwd{i}")
            sv.update(qkv=qkv, o_n=o_n, lse_n=lse_n, act=act)
            w_out = full["dswa_w_out"][j]
        cur = _mm(act, w_out, name=f"mix_out{i}", epilogue=lambda acc, r: (acc + r,), extras=(cur,))
        sv["x_mid"] = cur
        h2 = _rms_fwd(cur, norm_mlp[i], name=f"rms_mlp_fwd{i}")
        u, a = _mm_gather(dist, f"mlp_up{i}", h2, full["mlp_w1"][i], name=f"mlp_up{i}", out_dtypes=(BF16, BF16),
                          epilogue=lambda acc: (acc, jnp.square(jnp.maximum(acc, 0.0))))
        cur = _mm_gather(dist, f"mlp_down{i}", a, full["mlp_w2"][i], name=f"mlp_down{i}",
                         epilogue=lambda acc, r: (acc + r,), extras=(cur,))
        sv.update(h2=h2, u=u, a=a)
        saved.append(sv)

    loss_part, dcur, dcur_b, dg_final = _loss_head(cur, norm_final, target, name="loss_head")

    g_norm_mix, g_norm_mlp = [None] * DEPTH, [None] * DEPTH
    g_big = {n: [None] * len(full[n]) for n in _BIG}
    g_conv, g_alog, g_dt, g_nw = [None] * 2, [None] * 2, [None] * 2, [None] * 2
    d_table_t = jnp.zeros((DSWA_HEADS, REL_BUCKETS), F32)
    for i in reversed(range(DEPTH)):
        j = i // 2
        sv = saved[i]
        w1, w2 = full["mlp_w1"][i], full["mlp_w2"][i]
        du = _mm(dcur_b, w2, tb=True, name=f"mlp_down_bwd{i}", out_dtypes=(BF16,),
                 epilogue=lambda acc, uu: (acc * (2.0 * jnp.maximum(uu.astype(F32), 0.0)),), extras=(sv["u"],))
        g_big["mlp_w2"][i] = _mm(sv["a"], dcur_b, ta=True, name=f"mlp_w2_grad{i}", out_dtypes=(BF16,), shard="rows")
        g_big["mlp_w1"][i] = _mm(sv["h2"], du, ta=True, name=f"mlp_w1_grad{i}", out_dtypes=(BF16,), shard="cols")
        dh2 = _mm(du, w1, tb=True, name=f"mlp_up_bwd{i}")
        dmid, dmid_b, g_norm_mlp[i] = _rms_bwd(sv["x_mid"], norm_mlp[i], dh2, dcur, name=f"rms_mlp_bwd{i}")
        if i % 2 == 0:
            w_in, w_out = full["gdn_w_in"][j], full["gdn_w_out"][j]
            dact = _mm(dmid_b, w_out, tb=True, name=f"mix_out_bwd{i}")
            g_big["gdn_w_out"][j] = _mm(sv["act"], dmid_b, ta=True, name=f"mix_out_grad{i}", out_dtypes=(BF16,),
                                        shard="rows")
            do, dz, g_nw[j] = _gdn_post_bwd(sv["o"], sv["proj"], gdn_norm_w[j], dact, name=f"gdn_post_bwd{i}")
            dqkvn, dg_row, db_row, got = _gdn_chunk_bwd(sv["qkvn"], sv["g_row"], sv["b_row"], sv["states"], do,
                                                        name=f"gdn_chunk_bwd{i}",
                                                        comm=dist and dist.send_comm(f"chunk_bwd{i}", g_big))
            if dist:
                dist.got(f"chunk_bwd{i}", got)
            dproj, g_conv[j], got = _gdn_pre_bwd(sv["proj"], conv_full[j], dqkvn, dz, name=f"gdn_pre_bwd{i}",
                                                 comm=dist and dist.send_comm(f"pre_bwd{i}", g_big))
            if dist:
                dist.got(f"pre_bwd{i}", got)
            nh2 = 2 * GDN_HEADS
            da_, db_, g_alog[j], g_dt[j] = _gdn_gate_bwd(sv["ab"][:, :nh2], sv["ab"][:, nh2:], gdn_a_log[j], gdn_dt_bias[j],
                                                         dg_row.reshape(nh2, s).T, db_row.reshape(nh2, s).T,
                                                         name=f"gdn_gate_bwd{i}")
            dab = jnp.concatenate([da_, db_], axis=1)
            gw_main = _mm(sv["h"], dproj, ta=True, name=f"gdn_w_in_grad{i}", out_dtypes=(BF16,))
            gw_ab = _mm(sv["h"], dab, ta=True, name=f"gdn_w_ab_grad{i}", out_dtypes=(BF16,))
            g_big["gdn_w_in"][j] = jnp.concatenate([gw_main, gw_ab], axis=1)
            dh_ab = _mm(dab, w_in[:, GDN_MAIN:], tb=True, name=f"gdn_proj_ab_bwd{i}")
            comm = dist and dist.send_comm(f"gdn_proj_bwd{i}", g_big)
            dh = _mm(dproj, w_in[:, :GDN_MAIN], tb=True, name=f"gdn_proj_bwd{i}",
                     epilogue=lambda acc, r: (acc + r,), extras=(dh_ab,), comm=comm)
            if comm:
                dh, got = dh
                dist.got(f"gdn_proj_bwd{i}", got)
        else:
            w_in, w_out = full["dswa_w_in"][j], full["dswa_w_out"][j]
            dact = _mm(dmid_b, w_out, tb=True, name=f"mix_out_bwd{i}")
            g_big["dswa_w_out"][j] = _mm(sv["act"], dmid_b, ta=True, name=f"mix_out_grad{i}", out_dtypes=(BF16,),
                                         shard="rows")
            do_n, corr_n = _dswa_combine_bwd(sv["o_n"], sv["lse_n"], dact, name=f"dswa_comb_bwd{i}")
            *dqkv, dbias, got = _dswa_attn_bwd(sv["qkv"], bias, sv["lse_n"], do_n, corr_n, name=f"dswa_attn_bwd{i}",
                                               comm=dist and dist.send_comm(f"attn_bwd{i}", g_big))
            if dist:
                dist.got(f"attn_bwd{i}", got)
            d_table_t = d_table_t + _dswa_dtable(dbias.reshape(DSWA_HEADS, -1), onehot, name=f"dswa_dtable{i}")
            g_big["dswa_w_in"][j] = jnp.concatenate(
                [_mm(sv["h"], dt, ta=True, name=f"dswa_w_in_grad{i}_{t}", out_dtypes=(BF16,)) for t, dt in enumerate(dqkv)],
                axis=1)
            dh = None
            for t, dt in enumerate(dqkv):
                w_t = w_in[:, t * DSWA_WIDTH:(t + 1) * DSWA_WIDTH]
                if dh is None:
                    dh = _mm(dt, w_t, tb=True, name=f"dswa_proj_bwd{i}_{t}")
                else:
                    dh = _mm(dt, w_t, tb=True, name=f"dswa_proj_bwd{i}_{t}", epilogue=lambda acc, r: (acc + r,), extras=(dh,))
        dcur, dcur_b, g_norm_mix[i] = _rms_bwd(sv["x_in"], norm_mix[i], dh, dmid, name=f"rms_mix_bwd{i}")

    rep = dict(norm_mix=jnp.concatenate(g_norm_mix, axis=0), norm_mlp=jnp.concatenate(g_norm_mlp, axis=0),
               norm_final=dg_final.reshape(-1), rel_bias=d_table_t.T,
               gdn_a_log=jnp.stack(g_alog).reshape(gdn_a_log.shape), gdn_dt_bias=jnp.stack(g_dt).reshape(gdn_dt_bias.shape),
               gdn_norm_w=jnp.stack(g_nw).reshape(gdn_norm_w.shape))
    return loss_part, dcur, g_big, rep, g_conv
```

```python
import functools
import math

import jax
import jax.numpy as jnp
import numpy as np
from jax import lax
from jax.experimental import pallas as pl
from jax.experimental.pallas import tpu as pltpu

F32 = jnp.float32
BF16 = jnp.bfloat16
HP = lax.Precision.HIGHEST

N_DEV = 8
D_MODEL = 1024
DEPTH = 4
RMS_EPS = 1e-6
NEG_INF = -1e30

GDN_HEADS = 8
GDN_DK = 128
GDN_CONV = 5
GDN_CHUNK = 128
GDN_QKV = 3 * GDN_HEADS * GDN_DK
GDN_MAIN = GDN_QKV + GDN_HEADS * GDN_DK
GDN_AB = 4 * GDN_HEADS

DSWA_DILS = (1, 4, 16)
DSWA_HG = 6
DSWA_E = 64
DSWA_HEADS = 18
DSWA_WIDTH = DSWA_HEADS * DSWA_E
DSWA_HALF = 64
DSWA_PG = DSWA_HG // 2
DSWA_UNROLL = 4
REL_BUCKETS = 32
REL_MAX_DIST = 1024

ADAM_LR = 0.001
ADAM_B1 = 0.9
ADAM_B2 = 0.999
ADAM_EPS = 1e-08
ADAM_WD = 0.01
ADAM_STEP = 10

VMEM_LIMIT = 56 * 1024 * 1024


def _cparams(sem=None, **kw):
    return pltpu.CompilerParams(dimension_semantics=sem, vmem_limit_bytes=VMEM_LIMIT, **kw)


def _pick(dim, cands):
    for c in cands:
        if dim % c == 0:
            return c
    return dim


def _bdot(a, b):
    return jnp.dot(a.astype(BF16), b.astype(BF16), preferred_element_type=F32)


def _bdot_nt(a, b):
    return lax.dot_general(a.astype(BF16), b.astype(BF16), (((1,), (1,)), ((), ())),
                           preferred_element_type=F32)


def _bdot_tn(a, b):
    return lax.dot_general(a.astype(BF16), b.astype(BF16), (((0,), (0,)), ((), ())),
                           preferred_element_type=F32)


def _hdot(a, b):
    return jnp.dot(a, b, precision=HP, preferred_element_type=F32)


def _hdot_tn(a, b):
    return lax.dot_general(a, b, (((0,), (0,)), ((), ())), precision=HP, preferred_element_type=F32)


def _hdot_nt(a, b):
    return lax.dot_general(a, b, (((1,), (1,)), ((), ())), precision=HP, preferred_element_type=F32)


def _sigmoid(x):
    return 1.0 / (1.0 + jnp.exp(-x))


def _mm(a, b, *, name, ta=False, tb=False, out_dtypes=(F32,), epilogue=None, extras=(), vecs=(), vec_out=False,
        tm=None, tn=None, tk=None, shard=None, comm=None):
    if ta:
        kdim, m = a.shape
    else:
        m, kdim = a.shape
    n = b.shape[0] if tb else b.shape[1]
    if shard == "rows":
        tm = m // N_DEV if (m // N_DEV) % 128 == 0 else m
    if shard == "cols":
        tn = n // N_DEV
    tm = tm or _pick(m, (1024, 1152, 512, 384, 256, 128))
    tn = tn or _pick(n, (1024, 1152, 512, 384, 256, 128))
    tk = tk or _pick(kdim, (1024, 1152, 512, 384, 256, 128))
    nk = kdim // tk
    n_out = len(out_dtypes) + (1 if vec_out else 0)
    n_ex = len(extras) + len(vecs)
    rows_all = shard == "rows" and tm == m

    gi, gj = m // tm, n // tn

    def body(*refs):
        i, j, k = pl.program_id(0), pl.program_id(1), pl.program_id(2)
        inner = (j == 0) & (k == 0)
        ins, out_refs, (acc_ref,) = _comm_hooks(
            comm, refs, 2 + n_ex, n_out, 1, (i == 0) & inner, (i == (3 * gi) // 4) & inner,
            (i == gi - 1) & (j == gj - 1) & (k == nk - 1))
        a_ref, b_ref, ex_refs = ins[0], ins[1], ins[2:]
        if vec_out:
            out_refs, vec_ref = out_refs[:-1], out_refs[-1]

        @pl.when(k == 0)
        def _():
            acc_ref[...] = jnp.zeros_like(acc_ref)

        av = a_ref[...].astype(BF16)
        bv = b_ref[...].astype(BF16)
        dims = (((0 if ta else 1,), (1 if tb else 0,)), ((), ()))
        acc_ref[...] += lax.dot_general(av, bv, dims, preferred_element_type=F32)

        @pl.when(k == nk - 1)
        def _():
            acc = acc_ref[...]
            outs = (acc,) if epilogue is None else epilogue(acc, *[r[...] for r in ex_refs])
            if vec_out:
                part = outs[-1]

                @pl.when(i == 0)
                def _():
                    vec_ref[...] = part

                @pl.when(i > 0)
                def _():
                    vec_ref[...] += part
            for r, o in zip(out_refs, outs):
                if rows_all:
                    for p in range(N_DEV):
                        r[p] = o[p * (m // N_DEV):(p + 1) * (m // N_DEV)].astype(r.dtype)
                else:
                    r[...] = o.astype(r.dtype)

    a_spec = pl.BlockSpec((tk, tm), lambda i, j, k: (k, i)) if ta else pl.BlockSpec((tm, tk), lambda i, j, k: (i, k))
    b_spec = pl.BlockSpec((tn, tk), lambda i, j, k: (j, k)) if tb else pl.BlockSpec((tk, tn), lambda i, j, k: (k, j))
    o_spec = pl.BlockSpec((tm, tn), lambda i, j, k: (i, j))
    v_spec = pl.BlockSpec((1, tn), lambda i, j, k: (0, j))
    out_specs = [o_spec] * len(out_dtypes) + ([v_spec] if vec_out else [])
    out_shape = [jax.ShapeDtypeStruct((m, n), dt) for dt in out_dtypes]
    out_shape += [jax.ShapeDtypeStruct((1, n), F32)] if vec_out else []
    if shard == "rows":
        out_shape = [jax.ShapeDtypeStruct((N_DEV, m // N_DEV, n), out_dtypes[0])]
        out_specs = [pl.BlockSpec((N_DEV, m // N_DEV, tn), lambda i, j, k: (0, 0, j)) if rows_all
                     else pl.BlockSpec((None, tm, tn), lambda i, j, k: (i, 0, j))]
    if shard == "cols":
        out_shape = [jax.ShapeDtypeStruct((N_DEV, m, tn), out_dtypes[0])]
        out_specs = [pl.BlockSpec((None, tm, tn), lambda i, j, k: (j, i, 0))]
    c_in, c_out, c_shape, c_scr = _comm_specs(comm)
    outs = pl.pallas_call(
        body, name=name,
        grid=(gi, gj, nk),
        in_specs=[a_spec, b_spec] + [o_spec] * len(extras) + [v_spec] * len(vecs) + c_in,
        out_specs=out_specs + c_out,
        out_shape=out_shape + c_shape,
        scratch_shapes=[pltpu.VMEM((tm, tn), F32)] + c_scr,
        compiler_params=_cparams(("arbitrary",) * 3 if comm or vec_out else ("parallel", "parallel", "arbitrary")),
    )(a, b, *extras, *vecs, *(comm.arrays if comm else []))
    res = outs[0] if n_out == 1 else tuple(outs[:n_out])
    return (res, outs[n_out:]) if comm else res


def _rms_fwd(x, g, *, name):
    s, d = x.shape
    tr = _pick(s, (512, 256, 128))

    def body(x_ref, g_ref, h_ref):
        xv = x_ref[...]
        r = lax.rsqrt(jnp.mean(xv * xv, axis=-1, keepdims=True) + RMS_EPS)
        h_ref[...] = (xv * r * g_ref[...]).astype(h_ref.dtype)

    return pl.pallas_call(
        body, name=name, grid=(s // tr,),
        in_specs=[pl.BlockSpec((tr, d), lambda i: (i, 0)), pl.BlockSpec((1, d), lambda i: (0, 0))],
        out_specs=pl.BlockSpec((tr, d), lambda i: (i, 0)),
        out_shape=jax.ShapeDtypeStruct((s, d), BF16),
        compiler_params=_cparams(("parallel",)),
    )(x, g.reshape(1, d))


def _rms_bwd(x, g, dh, dres, *, name):
    s, d = x.shape
    tr = _pick(s, (512, 256, 128))

    def body(x_ref, g_ref, dh_ref, dres_ref, dx_ref, dxb_ref, dg_ref):
        i = pl.program_id(0)
        xv = x_ref[...]
        r = lax.rsqrt(jnp.mean(xv * xv, axis=-1, keepdims=True) + RMS_EPS)
        xn = xv * r
        dhv = dh_ref[...]
        dn = dhv * g_ref[...]
        dx = dres_ref[...] + r * (dn - xn * jnp.mean(dn * xn, axis=-1, keepdims=True))
        dx_ref[...] = dx
        dxb_ref[...] = dx.astype(dxb_ref.dtype)
        part = jnp.sum(dhv * xn, axis=0, keepdims=True)

        @pl.when(i == 0)
        def _():
            dg_ref[...] = part

        @pl.when(i > 0)
        def _():
            dg_ref[...] += part

    row = pl.BlockSpec((tr, d), lambda i: (i, 0))
    vec = pl.BlockSpec((1, d), lambda i: (0, 0))
    return pl.pallas_call(
        body, name=name, grid=(s // tr,),
        in_specs=[row, vec, row, row], out_specs=[row, row, vec],
        out_shape=[jax.ShapeDtypeStruct((s, d), F32), jax.ShapeDtypeStruct((s, d), BF16),
                   jax.ShapeDtypeStruct((1, d), F32)],
        compiler_params=_cparams(("arbitrary",)),
    )(x, g.reshape(1, d), dh, dres)


def _loss_head(x, g, target, *, name):
    s, d = x.shape
    tr = _pick(s, (512, 256, 128))

    def body(x_ref, g_ref, t_ref, loss_ref, dx_ref, dxb_ref, dg_ref):
        i = pl.program_id(0)
        xv = x_ref[...]
        gv = g_ref[...]
        r = lax.rsqrt(jnp.mean(xv * xv, axis=-1, keepdims=True) + RMS_EPS)
        xn = xv * r
        err = xn * gv - t_ref[...]
        lpart = 0.5 * jnp.sum(jnp.mean(err * err, axis=-1, keepdims=True), axis=0, keepdims=True)
        dy = err * (1.0 / d)
        dn = dy * gv
        dx = r * (dn - xn * jnp.mean(dn * xn, axis=-1, keepdims=True))
        dx_ref[...] = dx
        dxb_ref[...] = dx.astype(dxb_ref.dtype)
        gpart = jnp.sum(dy * xn, axis=0, keepdims=True)

        @pl.when(i == 0)
        def _():
            dg_ref[...] = gpart
            loss_ref[...] = lpart

        @pl.when(i > 0)
        def _():
            dg_ref[...] += gpart
            loss_ref[...] += lpart

    row = pl.BlockSpec((tr, d), lambda i: (i, 0))
    vec = pl.BlockSpec((1, d), lambda i: (0, 0))
    one = pl.BlockSpec((1, 1), lambda i: (0, 0))
    return pl.pallas_call(
        body, name=name, grid=(s // tr,),
        in_specs=[row, vec, row], out_specs=[one, row, row, vec],
        out_shape=[jax.ShapeDtypeStruct((1, 1), F32), jax.ShapeDtypeStruct((s, d), F32),
                   jax.ShapeDtypeStruct((s, d), BF16), jax.ShapeDtypeStruct((1, d), F32)],
        compiler_params=_cparams(("arbitrary",)),
    )(x, g.reshape(1, d), target)


def _shift_rows(x, sft, rows):
    s = x.shape[0]
    if sft == 0:
        return x
    y = pltpu.roll(x, (-sft) % s, 0)
    ok = (rows + sft >= 0) & (rows + sft < s)
    return jnp.where(ok, y, 0.0)


def _gdn_pre_fwd(proj, conv_w, *, name):
    s = proj.shape[0]
    nblk = GDN_QKV // 128
    pad = GDN_CONV // 2

    def body(x_ref, w_ref, o_ref):
        j = pl.program_id(0)
        x = x_ref[...]
        rows = lax.broadcasted_iota(jnp.int32, x.shape, 0)
        c = jnp.zeros_like(x)
        for t in range(GDN_CONV):
            c = c + w_ref[pl.ds(t, 1), :] * _shift_rows(x, t - pad, rows)
        a = c * _sigmoid(c)
        rinv = lax.rsqrt(jnp.sum(a * a, axis=-1, keepdims=True) + 1e-6)
        scale = jnp.where(j < GDN_HEADS, GDN_DK ** -0.5, 1.0)
        o_ref[...] = jnp.where(j >= 2 * GDN_HEADS, a, a * (rinv * scale))

    return pl.pallas_call(
        body, name=name, grid=(nblk,),
        in_specs=[pl.BlockSpec((s, 128), lambda j: (0, j)), pl.BlockSpec((GDN_CONV, 128), lambda j: (0, j))],
        out_specs=pl.BlockSpec((s, 128), lambda j: (0, j)),
        out_shape=jax.ShapeDtypeStruct((s, GDN_QKV), F32),
        compiler_params=_cparams(("parallel",)),
    )(proj, conv_w)


def _gdn_pre_bwd(proj, conv_w, dqkv, dproj, *, name, comm=None):
    s = proj.shape[0]
    nblk = GDN_QKV // 128
    pad = GDN_CONV // 2

    def body(*refs):
        j = pl.program_id(0)
        (x_ref, w_ref, df_ref, dbk_ref, _), (dx_ref, dw_ref), _ = _comm_hooks(
            comm, refs, 5, 2, 0, j == 0, j == nblk // 2, j == nblk - 1)
        x = x_ref[...]
        rows = lax.broadcasted_iota(jnp.int32, x.shape, 0)
        xs = [_shift_rows(x, t - pad, rows) for t in range(GDN_CONV)]
        c = jnp.zeros_like(x)
        for t in range(GDN_CONV):
            c = c + w_ref[pl.ds(t, 1), :] * xs[t]
        sg = _sigmoid(c)
        a = c * sg
        rinv = lax.rsqrt(jnp.sum(a * a, axis=-1, keepdims=True) + 1e-6)
        scale = jnp.where(j < GDN_HEADS, GDN_DK ** -0.5, 1.0)
        dy = df_ref[...] + dbk_ref[...]
        nh = a * rinv
        da_n = (rinv * scale) * (dy - nh * jnp.sum(dy * nh, axis=-1, keepdims=True))
        da = jnp.where(j >= 2 * GDN_HEADS, dy, da_n)
        dc = da * (sg * (1.0 + c * (1.0 - sg)))
        dx = jnp.zeros_like(x)
        for t in range(GDN_CONV):
            dx = dx + w_ref[pl.ds(t, 1), :] * _shift_rows(dc, pad - t, rows)
            dw_ref[pl.ds(t, 1), :] = jnp.sum(dc * xs[t], axis=0, keepdims=True)
        dx_ref[...] = dx.astype(dx_ref.dtype)

    col = pl.BlockSpec((s, 128), lambda j: (0, j))
    wsp = pl.BlockSpec((GDN_CONV, 128), lambda j: (0, j))
    c_in, c_out, c_shape, c_scr = _comm_specs(comm)
    res = pl.pallas_call(
        body, name=name, grid=(nblk,),
        in_specs=[col, wsp, col, col, pl.BlockSpec(memory_space=pl.ANY)] + c_in, out_specs=[col, wsp] + c_out,
        out_shape=[jax.ShapeDtypeStruct(dproj.shape, BF16), jax.ShapeDtypeStruct((GDN_CONV, GDN_QKV), F32)] + c_shape,
        input_output_aliases={4: 0},
        scratch_shapes=c_scr,
        compiler_params=_cparams(("arbitrary",) if comm else ("parallel",)),
    )(proj, conv_w, dqkv[0], dqkv[1], dproj, *(comm.arrays if comm else []))
    return res[0], res[1], res[2:]


def _softplus(x):
    return jnp.maximum(x, 0.0) + jnp.log(1.0 + jnp.exp(-jnp.abs(x)))


def _gdn_gate_fwd(a, b, a_log, dt_bias, *, name):
    s = a.shape[0]
    nh = 2 * GDN_HEADS

    def body(a_ref, b_ref, al_ref, dt_ref, g_ref, be_ref):
        g_ref[...] = -jnp.exp(al_ref[...]) * _softplus(a_ref[...] + dt_ref[...])
        be_ref[...] = _sigmoid(b_ref[...])

    return pl.pallas_call(
        body, name=name,
        out_shape=[jax.ShapeDtypeStruct((s, nh), F32), jax.ShapeDtypeStruct((s, nh), F32)],
        compiler_params=_cparams(),
    )(a, b, a_log.reshape(1, nh), dt_bias.reshape(1, nh))


def _gdn_gate_bwd(a, b, a_log, dt_bias, dg, dbeta, *, name):
    s = a.shape[0]
    nh = 2 * GDN_HEADS

    def body(a_ref, b_ref, al_ref, dt_ref, dg_ref, db_ref, da_ref, dbb_ref, dal_ref, ddt_ref):
        ea = jnp.exp(al_ref[...])
        z = a_ref[...] + dt_ref[...]
        dgv = dg_ref[...]
        dz = dgv * (-ea) * _sigmoid(z)
        dal_ref[...] = jnp.sum(dgv * (-ea) * _softplus(z), axis=0, keepdims=True)
        ddt_ref[...] = jnp.sum(dz, axis=0, keepdims=True)
        sb = _sigmoid(b_ref[...])
        da_ref[...] = dz
        dbb_ref[...] = db_ref[...] * sb * (1.0 - sb)

    return pl.pallas_call(
        body, name=name,
        out_shape=[jax.ShapeDtypeStruct((s, nh), F32), jax.ShapeDtypeStruct((s, nh), F32),
                   jax.ShapeDtypeStruct((1, nh), F32), jax.ShapeDtypeStruct((1, nh), F32)],
        compiler_params=_cparams(),
    )(a, b, a_log.reshape(1, nh), dt_bias.reshape(1, nh), dg, dbeta)


def _chunk_masks(d):
    c = GDN_CHUNK
    ii = lax.broadcasted_iota(jnp.int32, (c, c), 0)
    jj = lax.broadcasted_iota(jnp.int32, (c, c), 1)
    dif = (ii - jj) * (1 - 2 * d)
    mi = dif >= 0
    mit = dif <= 0
    ms = dif > 0
    eye = ii == jj
    bds = [(ii >> sh) == (jj >> sh) for sh in range(3, c.bit_length() - 1)]
    return dict(mi=mi, mit=mit, ms=ms, eye=eye, bds=bds,
                mif=mi.astype(F32), mitf=mit.astype(F32), eyef=eye.astype(F32))


class _V:
    def __init__(self, xs):
        self.xs = tuple(xs)

    def __add__(self, o):
        return _lift(lambda a, b: a + b)(self, o)

    def __radd__(self, o):
        return _lift(lambda a, b: b + a)(self, o)

    def __sub__(self, o):
        return _lift(lambda a, b: a - b)(self, o)

    def __rsub__(self, o):
        return _lift(lambda a, b: b - a)(self, o)

    def __mul__(self, o):
        return _lift(lambda a, b: a * b)(self, o)

    def __rmul__(self, o):
        return _lift(lambda a, b: b * a)(self, o)

    def __and__(self, o):
        return _lift(lambda a, b: a & b)(self, o)

    def __neg__(self):
        return _lift(lambda a: -a)(self)

    def __rtruediv__(self, o):
        return _lift(lambda a, b: b / a)(self, o)


def _lift(f):
    def g(*args, **kw):
        n = next(len(a.xs) for a in args if isinstance(a, _V))
        return _V(f(*[a.xs[i] if isinstance(a, _V) else a for a in args], **kw) for i in range(n))
    return g


_vwhere, _vsum, _vexp, _vnot = _lift(jnp.where), _lift(jnp.sum), _lift(jnp.exp), _lift(jnp.logical_not)
_vhdot, _vhdot_tn = _lift(_bdot), _lift(_bdot_tn)
_vbdot, _vbdot_nt, _vbdot_tn = _lift(_bdot), _lift(_bdot_nt), _lift(_bdot_tn)
_vcat = _lift(lambda a, b: jnp.concatenate([a, b], axis=1))
_vlo = _lift(lambda a, n: a[:, :n])
_vhi = _lift(lambda a, n: a[:, n:])


def _both_masks(n):
    m = [_chunk_masks(d) for d in range(2)]
    mk = {key: _V([m[0][key]] * n + [m[1][key]] * n) for key in m[0] if key != "bds"}
    mk["bds"] = [_V([m[0]["bds"][i]] * n + [m[1]["bds"][i]] * n) for i in range(len(m[0]["bds"]))]
    return mk


def _tri_inv(a, mk):
    eyef = mk["eyef"]
    bds = mk["bds"]
    a8 = _vwhere(bds[0], a, 0.0)
    a2 = _vhdot(a8, a8)
    a4 = _vhdot(a2, a2)
    t = _vhdot(_vhdot(eyef - a8, eyef + a2), eyef + a4)
    for inner, outer in zip(bds, bds[1:] + [None]):
        off = _vnot(inner) if outer is None else (outer & _vnot(inner))
        low = _vwhere(off, a, 0.0)
        t = t - _vhdot(_vhdot(t, low), t)
    return t


def _chunk_prep(q, k, v, g_row, b_row, mk):
    dv = GDN_DK
    g_col = _vsum(mk["eyef"] * g_row, axis=1, keepdims=True)
    b_col = _vsum(mk["eyef"] * b_row, axis=1, keepdims=True)
    gc_col = _vsum(mk["mif"] * g_row, axis=1, keepdims=True)
    gc_row = _vsum(mk["mitf"] * g_col, axis=0, keepdims=True)
    gl = _vsum(g_row, axis=1, keepdims=True)
    decay = _vwhere(mk["mi"], _vexp(_vwhere(mk["mi"], gc_col - gc_row, 0.0)), 0.0)
    eg = _vexp(gc_col)
    e2 = _vexp(gl - gc_col)
    egl = _vexp(gl)
    kb = k * b_col
    pm = _vbdot_nt(kb, k)
    a = _vwhere(mk["ms"], pm * decay, 0.0)
    t = _tri_inv(a, mk)
    sol = _vhdot(t, _vcat(v * b_col, kb * eg))
    u, w = _vlo(sol, dv), _vhi(sol, dv)
    qm = _vbdot_nt(q, k)
    return dict(b_col=b_col, decay=decay, eg=eg, e2=e2, egl=egl, kb=kb, pm=pm, t=t, u=u, w=w,
                qm=qm, intra=qm * decay, qd=q * eg, kd=k * e2)


def _chunk_fwd_step(p, state):
    v_new = p["u"] - _vbdot(p["w"], state)
    o = _vbdot(p["qd"], state) + _vbdot(p["intra"], v_new)
    new_state = state * p["egl"] + _vbdot_tn(p["kd"], v_new)
    return o, new_state


def _chunk_bwd_step(q, k, v, p, mk, state, dso, do):
    dv_dim = GDN_DK
    v_new = p["u"] - _vbdot(p["w"], state)
    dvn = _vbdot_tn(p["intra"], do) + _vbdot(p["kd"], dso)
    dintra = _vbdot_nt(do, v_new)
    dqd = _vbdot_nt(do, state)
    ds = p["egl"] * dso + _vbdot_tn(p["qd"], do) - _vbdot_tn(p["w"], dvn)
    dkd = _vbdot_nt(v_new, dso)
    dgl = _vsum(_vsum(dso * state, axis=1, keepdims=True), axis=0, keepdims=True) * p["egl"]
    dw = -_vbdot_nt(dvn, state)
    drhs = _vhdot_tn(p["t"], _vcat(dvn, dw))
    dru, drw = _vlo(drhs, dv_dim), _vhi(drhs, dv_dim)
    da = -_vwhere(mk["ms"], _vbdot_nt(drhs, _vcat(p["u"], p["w"])), 0.0)
    b_col = p["b_col"]
    dv = dru * b_col
    dbeta = _vsum(dru * v, axis=1, keepdims=True)
    dkb = drw * p["eg"]
    deg = _vsum(drw * p["kb"], axis=1, keepdims=True)
    dp = da * p["decay"]
    ddecay = da * p["pm"]
    dkb = dkb + _vbdot(dp, k)
    dk = _vbdot_tn(dp, p["kb"])
    dqm = dintra * p["decay"]
    ddecay = ddecay + dintra * p["qm"]
    dq = _vbdot(dqm, k)
    dk = dk + _vbdot_tn(dqm, q)
    dd = ddecay * p["decay"]
    dgc_col = _vsum(dd, axis=1, keepdims=True)
    dgc_row = -_vsum(dd, axis=0, keepdims=True)
    dq = dq + dqd * p["eg"]
    deg = deg + _vsum(dqd * q, axis=1, keepdims=True)
    dk = dk + dkd * p["e2"]
    de2 = _vsum(dkd * k, axis=1, keepdims=True) * p["e2"]
    dgl = dgl + _vsum(de2, axis=0, keepdims=True)
    dgc_col = dgc_col - de2 + deg * p["eg"]
    dk = dk + dkb * b_col
    dbeta = dbeta + _vsum(dkb * k, axis=1, keepdims=True)
    dgc_col = dgc_col + _vsum(mk["eyef"] * dgc_row, axis=1, keepdims=True)
    dg_row = _vsum(mk["mif"] * dgc_col, axis=0, keepdims=True) + dgl
    dbeta_row = _vsum(mk["eyef"] * dbeta, axis=0, keepdims=True)
    return dq, dk, dv, dg_row, dbeta_row, ds


def _gdn_chunk_fwd(qkvn, g5, b5, *, name, comm=None):
    s = qkvn.shape[0]
    c = GDN_CHUNK
    nc = s // c
    h_, dk = GDN_HEADS, GDN_DK

    def body(*refs):
        n = pl.program_id(0)
        ins, outs, (st_scr,) = _comm_hooks(comm, refs, 6, 4, 1, n == 0, n == (3 * nc) // 4, n == nc - 1)
        x_refs, g_refs, b_refs = ins[0:2], ins[2:4], ins[4:6]
        o_refs, st_refs = outs[0:2], outs[2:4]

        @pl.when(n == 0)
        def _():
            st_scr[...] = jnp.zeros_like(st_scr)

        ch = [(d, h) for d in range(2) for h in range(h_)]
        mk = _both_masks(h_)
        q, k, v = (_V(x_refs[d][:, (t * h_ + h) * dk:(t * h_ + h + 1) * dk] for d, h in ch) for t in range(3))
        g, b = (_V(r[d][0, h, 0] for d, h in ch) for r in (g_refs, b_refs))
        state = _V(st_scr[d * h_ + h] for d, h in ch)
        o, new_state = _chunk_fwd_step(_chunk_prep(q, k, v, g, b, mk), state)
        for i, (d, h) in enumerate(ch):
            st_refs[d][h, 0] = state.xs[i]
            st_scr[d * h_ + h] = new_state.xs[i]
            o_refs[d][:, h * dk:(h + 1) * dk] = o.xs[i]

    ce = (lambda n: n, lambda n: nc - 1 - n)
    xs = [pl.BlockSpec((c, 3 * h_ * dk), lambda n, d=d: (ce[d](n), 0)) for d in range(2)]
    gates = [pl.BlockSpec((1, h_, 1, 1, c), lambda n, d=d: (d, 0, ce[d](n), 0, 0)) for d in range(2)]
    os_ = [pl.BlockSpec((c, h_ * dk), lambda n, d=d: (ce[d](n), 0)) for d in range(2)]
    sts = [pl.BlockSpec((h_, 1, dk, dk), lambda n, d=d: (0, ce[d](n), 0, 0)) for d in range(2)]
    c_in, c_out, c_shape, c_scr = _comm_specs(comm)
    res = pl.pallas_call(
        body, name=name, grid=(nc,),
        in_specs=xs + gates + gates + c_in,
        out_specs=os_ + sts + c_out,
        out_shape=[jax.ShapeDtypeStruct((s, h_ * dk), F32)] * 2 + [jax.ShapeDtypeStruct((h_, nc, dk, dk), F32)] * 2 + c_shape,
        scratch_shapes=[pltpu.VMEM((2 * h_, dk, dk), F32)] + c_scr,
        compiler_params=_cparams(("arbitrary",)),
    )(qkvn, qkvn, g5, g5, b5, b5, *(comm.arrays if comm else []))
    return res[0:2], res[2:4], res[4:]


def _gdn_chunk_bwd(qkvn, g5, b5, states, do, *, name, comm=None):
    s = qkvn.shape[0]
    c = GDN_CHUNK
    nc = s // c
    h_, dk = GDN_HEADS, GDN_DK

    def body(*refs):
        i = pl.program_id(0)
        ins, outs, (ds_scr,) = _comm_hooks(comm, refs, 10, 6, 1, i == 0, i == nc // 2, i == nc - 1)
        x_refs, g_refs, b_refs, st_refs, do_refs = ins[0:2], ins[2:4], ins[4:6], ins[6:8], ins[8:10]
        dx_refs, dg_refs, db_refs = outs[0:2], outs[2:4], outs[4:6]

        @pl.when(i == 0)
        def _():
            ds_scr[...] = jnp.zeros_like(ds_scr)

        ch = [(d, h) for d in range(2) for h in range(h_)]
        mk = _both_masks(h_)
        q, k, v = (_V(x_refs[d][:, (t * h_ + h) * dk:(t * h_ + h + 1) * dk] for d, h in ch) for t in range(3))
        g, b = (_V(r[d][0, h, 0] for d, h in ch) for r in (g_refs, b_refs))
        state = _V(st_refs[d][h, 0] for d, h in ch)
        dso = _V(ds_scr[d * h_ + h] for d, h in ch)
        dov = _V(do_refs[d][:, h * dk:(h + 1) * dk] for d, h in ch)
        res = _chunk_bwd_step(q, k, v, _chunk_prep(q, k, v, g, b, mk), mk, state, dso, dov)
        for (d, h), (dq, dkk, dvv, dg_r, db_r, ds) in zip(ch, zip(*[r.xs for r in res])):
            ds_scr[d * h_ + h] = ds
            dg_refs[d][h, 0] = dg_r
            db_refs[d][h, 0] = db_r
            for t, val in enumerate((dq, dkk, dvv)):
                dx_refs[d][:, (t * h_ + h) * dk:(t * h_ + h + 1) * dk] = val

    ce = (lambda i: nc - 1 - i, lambda i: i)
    both = lambda mk_spec: [mk_spec(d) for d in range(2)]
    xs = both(lambda d: pl.BlockSpec((c, 3 * h_ * dk), lambda i: (ce[d](i), 0)))
    gates = both(lambda d: pl.BlockSpec((1, h_, 1, 1, c), lambda i: (d, 0, ce[d](i), 0, 0)))
    sts = both(lambda d: pl.BlockSpec((h_, 1, dk, dk), lambda i: (0, ce[d](i), 0, 0)))
    dos = both(lambda d: pl.BlockSpec((c, h_ * dk), lambda i: (ce[d](i), 0)))
    gouts = both(lambda d: pl.BlockSpec((h_, 1, 1, c), lambda i: (0, ce[d](i), 0, 0)))
    c_in, c_out, c_shape, c_scr = _comm_specs(comm)
    res = pl.pallas_call(
        body, name=name, grid=(nc,),
        in_specs=xs + gates + gates + sts + dos + c_in,
        out_specs=xs + gouts + gouts + c_out,
        out_shape=[jax.ShapeDtypeStruct((s, 3 * h_ * dk), F32)] * 2
        + [jax.ShapeDtypeStruct((h_, nc, 1, c), F32)] * 4 + c_shape,
        scratch_shapes=[pltpu.VMEM((2 * h_, dk, dk), F32)] + c_scr,
        compiler_params=_cparams(("arbitrary",)),
    )(qkvn, qkvn, g5, g5, b5, b5, states[0], states[1], do, do, *(comm.arrays if comm else []))
    return res[0:2], jnp.stack(res[2:4]), jnp.stack(res[4:6]), res[6:]


def _gdn_post_fwd(o, z, norm_w, *, name):
    s = o[0].shape[0]
    h_, dk = GDN_HEADS, GDN_DK

    def body(of_ref, ob_ref, z_ref, w_ref, a_ref):
        ov = of_ref[...] + ob_ref[...]
        zv = z_ref[...]
        r = lax.rsqrt(jnp.mean(ov * ov, axis=-1, keepdims=True) + RMS_EPS)
        a_ref[...] = (ov * r * w_ref[...] * (zv * _sigmoid(zv))).astype(a_ref.dtype)

    col = pl.BlockSpec((s, dk), lambda h: (0, h))
    return pl.pallas_call(
        body, name=name, grid=(h_,),
        in_specs=[col, col, pl.BlockSpec((s, dk), lambda h: (0, 3 * h_ + h)), pl.BlockSpec((1, dk), lambda h: (0, 0))],
        out_specs=col,
        out_shape=jax.ShapeDtypeStruct((s, h_ * dk), BF16),
        compiler_params=_cparams(("parallel",)),
    )(o[0], o[1], z, norm_w.reshape(1, dk))


def _gdn_post_bwd(o, z, norm_w, dact, *, name):
    s = o[0].shape[0]
    h_, dk = GDN_HEADS, GDN_DK

    def body(of_ref, ob_ref, z_ref, w_ref, da_ref, do_ref, dz_ref, dw_ref):
        h = pl.program_id(0)
        ov = of_ref[...] + ob_ref[...]
        zv = z_ref[...]
        wv = w_ref[...]
        dav = da_ref[...]
        r = lax.rsqrt(jnp.mean(ov * ov, axis=-1, keepdims=True) + RMS_EPS)
        nrm = ov * r
        sg = _sigmoid(zv)
        sz = zv * sg
        dn = dav * wv * sz
        do_ref[...] = r * (dn - nrm * jnp.mean(dn * nrm, axis=-1, keepdims=True))
        dz_ref[...] = (dav * nrm * wv * (sg * (1.0 + zv * (1.0 - sg)))).astype(dz_ref.dtype)
        part = jnp.sum(dav * nrm * sz, axis=0, keepdims=True)

        @pl.when(h == 0)
        def _():
            dw_ref[...] = part

        @pl.when(h > 0)
        def _():
            dw_ref[...] += part

    col = pl.BlockSpec((s, dk), lambda h: (0, h))
    vec = pl.BlockSpec((1, dk), lambda h: (0, 0))
    return pl.pallas_call(
        body, name=name, grid=(h_,),
        in_specs=[col, col, pl.BlockSpec((s, dk), lambda h: (0, 3 * h_ + h)), vec, col],
        out_specs=[col, pl.BlockSpec((s, dk), lambda h: (0, 3 * h_ + h)), vec],
        out_shape=[jax.ShapeDtypeStruct((s, h_ * dk), F32), jax.ShapeDtypeStruct((s, GDN_MAIN), BF16),
                   jax.ShapeDtypeStruct((1, dk), F32)],
        compiler_params=_cparams(("arbitrary",)),
    )(o[0], o[1], z, norm_w.reshape(1, dk), dact)


def _rel_bucket(rel):
    nb = REL_BUCKETS // 2
    max_exact = nb // 2
    ret = jnp.where(rel > 0, nb, 0)
    n = jnp.abs(rel)
    nf = jnp.maximum(n, 1).astype(F32)
    large = max_exact + (jnp.log(nf / max_exact) / math.log(REL_MAX_DIST / max_exact)
                         * (nb - max_exact)).astype(jnp.int32)
    large = jnp.minimum(large, nb - 1)
    return ret + jnp.where(n < max_exact, n, large)


def _bucket_onehot():
    half = DSWA_HALF
    outs = []
    for dil in DSWA_DILS:
        rel = (jnp.arange(3 * half)[None, :] - half - jnp.arange(half)[:, None]) * dil
        outs.append(jax.nn.one_hot(_rel_bucket(rel).reshape(-1), REL_BUCKETS, dtype=F32, axis=0))
    return jnp.stack(outs)


def _head_group_select(vals):
    rows = lax.broadcasted_iota(jnp.int32, vals[0].shape, 0)
    return jnp.where(rows < DSWA_HG, vals[0], jnp.where(rows < 2 * DSWA_HG, vals[1], vals[2]))


def _dswa_bias(table_t, onehot, *, name):
    p = onehot.shape[-1]

    def body(t_ref, oh_ref, b_ref):
        b_ref[...] = _head_group_select([_hdot(t_ref[...], oh_ref[g]) for g in range(3)])

    return pl.pallas_call(body, name=name, out_shape=jax.ShapeDtypeStruct((DSWA_HEADS, p), F32),
                          compiler_params=_cparams())(table_t, onehot)


def _dswa_dtable(dbias, onehot, *, name):
    def body(d_ref, oh_ref, t_ref):
        t_ref[...] = _head_group_select([_hdot_nt(d_ref[...], oh_ref[g]) for g in range(3)])

    return pl.pallas_call(body, name=name, out_shape=jax.ShapeDtypeStruct((DSWA_HEADS, REL_BUCKETS), F32),
                          compiler_params=_cparams())(dbias, onehot)


def _rows(start, dil):
    if dil == 1:
        return pl.ds(pl.multiple_of(start, DSWA_HALF), DSWA_HALF)
    return pl.ds(start, DSWA_HALF, stride=dil)


def _attn_blocks(it, s, dil):
    half = DSWA_HALF
    nbs = s // half // dil
    ii = lax.broadcasted_iota(jnp.int32, (half, 3 * half), 0)
    jj = lax.broadcasted_iota(jnp.int32, (half, 3 * half), 1)
    band = jnp.abs(jj - half - ii) <= half
    out = []
    for u in range(DSWA_UNROLL):
        blk = it * DSWA_UNROLL + u
        r, b = blk // nbs, blk % nbs
        own = r + dil * half * b
        prev = own - jnp.where(b > 0, dil * half, 0)
        nxt = own + jnp.where(b < nbs - 1, dil * half, 0)
        ok = band & ((jj >= half) | (b > 0)) & ((jj < 2 * half) | (b < nbs - 1))
        out.append(((prev, own, nxt), ok))
    return out


def _attn_chains(q_ref, k_ref, v_ref, blocks, dil):
    lane = lax.broadcasted_iota(jnp.int32, (DSWA_HALF, 2 * DSWA_E), 1)
    qm, kw, vw, valid, hmask = [], [], [], [], []
    for (prev, own, nxt), ok in blocks:
        q = q_ref[_rows(own, dil), :].astype(BF16)
        k = jnp.concatenate([k_ref[_rows(st, dil), :] for st in (prev, own, nxt)], axis=0).astype(BF16)
        v = jnp.concatenate([v_ref[_rows(st, dil), :] for st in (prev, own, nxt)], axis=0).astype(BF16)
        for hd in range(2):
            mine = (lane < DSWA_E) if hd == 0 else (lane >= DSWA_E)
            qm.append(jnp.where(mine, q, jnp.zeros_like(q)))
            kw.append(k)
            vw.append(v)
            valid.append(ok)
            hmask.append(mine)
    return _V(qm), _V(kw), _V(vw), _V(valid), _V(hmask)


def _per_group(pr, fn):
    for gi, dil in enumerate(DSWA_DILS):
        pl.when(pr // DSWA_PG == gi)(functools.partial(fn, dil))


_vmax, _vlog = _lift(jnp.max), _lift(jnp.log)


def _dswa_attn_fwd(qkv, bias, *, name, comm=None):
    s = qkv.shape[0]
    half, e = DSWA_HALF, DSWA_E
    npair = DSWA_HEADS // 2

    def body(*refs):
        pr = pl.program_id(0)
        (q_ref, k_ref, v_ref, bias_ref), (o_ref, lse_ref), _ = _comm_hooks(
            comm, refs, 4, 2, 0, pr == 0, pr == (3 * npair) // 4, pr == npair - 1)
        bias_v = _V([bias_ref[0], bias_ref[1]] * DSWA_UNROLL)

        def run(dil):
            def step(it, carry):
                blocks = _attn_blocks(it, s, dil)
                qm, kw, vw, valid, hmask = _attn_chains(q_ref, k_ref, v_ref, blocks, dil)
                sc = _vwhere(valid, _vbdot_nt(qm, kw) * (e ** -0.5) + bias_v, NEG_INF)
                m = _vmax(sc, axis=-1, keepdims=True)
                p = _vexp(sc - m)
                l = _vsum(p, axis=-1, keepdims=True)
                o = _vbdot(p * (1.0 / l), vw)
                lse = m + _vlog(l)
                for u, ((_, own, _), _) in enumerate(blocks):
                    is_a = hmask.xs[2 * u]
                    o_ref[_rows(own, dil), :] = jnp.where(is_a, o.xs[2 * u], o.xs[2 * u + 1])
                    lse_ref[_rows(own, dil), :] = jnp.where(is_a, lse.xs[2 * u], lse.xs[2 * u + 1])
                return carry

            lax.fori_loop(0, s // half // DSWA_UNROLL, step, 0)

        _per_group(pr, run)

    col = lambda t: pl.BlockSpec((s, 2 * e), lambda p: (0, t * npair + p))
    pair = pl.BlockSpec((s, 2 * e), lambda p: (0, p))
    c_in, c_out, c_shape, c_scr = _comm_specs(comm)
    res = pl.pallas_call(
        body, name=name, grid=(npair,),
        in_specs=[col(0), col(1), col(2), pl.BlockSpec((2, half, 3 * half), lambda p: (p, 0, 0))] + c_in,
        out_specs=[pair, pair] + c_out,
        out_shape=[jax.ShapeDtypeStruct((s, npair * 2 * e), F32)] * 2 + c_shape,
        scratch_shapes=c_scr,
        compiler_params=_cparams(("arbitrary",)),
    )(qkv, qkv, qkv, bias, *(comm.arrays if comm else []))
    return res[0], res[1], res[2:]


def _dswa_attn_bwd(qkv, bias, lse, do, corr, *, name, comm=None):
    s = qkv.shape[0]
    half, e = DSWA_HALF, DSWA_E
    npair = DSWA_HEADS // 2
    w = 2 * e

    def body(*refs):
        pr = pl.program_id(0)
        (q_ref, k_ref, v_ref, bias_ref, lse_ref, do_ref, corr_ref), (dq_ref, dk_ref, dv_ref, db_ref), _ = _comm_hooks(
            comm, refs, 7, 4, 0, pr == 0, pr == npair // 2, pr == npair - 1)
        bias_v = _V([bias_ref[0], bias_ref[1]] * DSWA_UNROLL)
        dk_ref[...] = jnp.zeros_like(dk_ref)
        dv_ref[...] = jnp.zeros_like(dv_ref)

        def run(dil):
            def step(it, dbias):
                blocks = _attn_blocks(it, s, dil)
                qm, kw, vw, valid, hmask = _attn_chains(q_ref, k_ref, v_ref, blocks, dil)
                hd = [0, 1] * DSWA_UNROLL
                rows = [_rows(own, dil) for (_, own, _), _ in blocks for _ in range(2)]
                lse_c = _V(lse_ref[rw, :][:, h * e:h * e + 1] for rw, h in zip(rows, hd))
                corr_c = _V(corr_ref[rw, :][:, h * e:h * e + 1] for rw, h in zip(rows, hd))
                dov = _vwhere(hmask, _V(do_ref[rw, :] for rw in rows), 0.0)
                sc = _vbdot_nt(qm, kw) * (e ** -0.5) + bias_v
                p = _vwhere(valid, _vexp(_vwhere(valid, sc, 0.0) - lse_c), 0.0)
                dsc = p * (_vbdot_nt(dov, vw) + corr_c)
                dq = _vbdot(dsc, kw) * (e ** -0.5)
                dkc = _vbdot_tn(dsc, qm) * (e ** -0.5)
                dvc = _vbdot_tn(p, dov)
                for u, (starts, _) in enumerate(blocks):
                    dq_ref[_rows(starts[1], dil), :] = jnp.where(hmask.xs[2 * u], dq.xs[2 * u], dq.xs[2 * u + 1])
                    dk_u = dkc.xs[2 * u] + dkc.xs[2 * u + 1]
                    dv_u = dvc.xs[2 * u] + dvc.xs[2 * u + 1]
                    for t, st in enumerate(starts):
                        dk_ref[_rows(st, dil), :] += dk_u[t * half:(t + 1) * half]
                        dv_ref[_rows(st, dil), :] += dv_u[t * half:(t + 1) * half]
                da, db = dbias
                for u in range(DSWA_UNROLL):
                    da, db = da + dsc.xs[2 * u], db + dsc.xs[2 * u + 1]
                return da, db

            zero = jnp.zeros((half, 3 * half), F32)
            da, db = lax.fori_loop(0, s // half // DSWA_UNROLL, step, (zero, zero))
            db_ref[0] = da
            db_ref[1] = db

        _per_group(pr, run)

    col = lambda t: pl.BlockSpec((s, w), lambda p: (0, t * npair + p))
    ps = pl.BlockSpec((s, w), lambda p: (0, p))
    bs = pl.BlockSpec((2, half, 3 * half), lambda p: (p, 0, 0))
    c_in, c_out, c_shape, c_scr = _comm_specs(comm)
    res = pl.pallas_call(
        body, name=name, grid=(npair,),
        in_specs=[col(0), col(1), col(2), bs, ps, ps, ps] + c_in,
        out_specs=[ps, ps, ps, bs] + c_out,
        out_shape=[jax.ShapeDtypeStruct((s, npair * w), F32)] * 3
        + [jax.ShapeDtypeStruct((DSWA_HEADS, half, 3 * half), F32)] + c_shape,
        scratch_shapes=c_scr,
        compiler_params=_cparams(("arbitrary",)),
    )(qkv, qkv, qkv, bias, lse, do, corr, *(comm.arrays if comm else []))
    return res[0], res[1], res[2], res[3], res[4:]


def _pair_cols(g, j):
    w = 2 * DSWA_E
    return slice((g * DSWA_PG + j) * w, (g * DSWA_PG + j + 1) * w)


def _group_weights(l_ref, j):
    ls = [l_ref[:, _pair_cols(g, j)] for g in range(3)]
    m = jnp.maximum(jnp.maximum(ls[0], ls[1]), ls[2])
    es = [jnp.exp(x - m) for x in ls]
    inv = 1.0 / (es[0] + es[1] + es[2])
    return [x * inv for x in es]


def _dswa_combine_fwd(o, lse, *, name):
    s, wd = o.shape
    tr = _pick(s, (512, 256, 128))

    def body(o_ref, l_ref, c_ref):
        for j in range(DSWA_PG):
            al = _group_weights(l_ref, j)
            for g in range(3):
                c_ref[:, _pair_cols(g, j)] = (o_ref[:, _pair_cols(g, j)] * al[g]).astype(c_ref.dtype)

    row = pl.BlockSpec((tr, wd), lambda i: (i, 0))
    return pl.pallas_call(
        body, name=name, grid=(s // tr,),
        in_specs=[row, row], out_specs=row,
        out_shape=jax.ShapeDtypeStruct(o.shape, BF16),
        compiler_params=_cparams(("parallel",)),
    )(o, lse)


def _dswa_combine_bwd(o, lse, dc, *, name):
    s, wd = o.shape
    tr = _pick(s, (512, 256, 128))

    def body(o_ref, l_ref, dc_ref, do_ref, corr_ref):
        lane = lax.broadcasted_iota(jnp.int32, (tr, 2 * DSWA_E), 1)
        is_a = lane < DSWA_E
        for j in range(DSWA_PG):
            al = _group_weights(l_ref, j)
            tot = jnp.zeros((tr, 2 * DSWA_E), F32)
            for g in range(3):
                cols = _pair_cols(g, j)
                dcv = dc_ref[:, cols]
                do_ref[:, cols] = dcv * al[g]
                prod = dcv * o_ref[:, cols]
                dal = jnp.where(is_a, jnp.sum(jnp.where(is_a, prod, 0.0), axis=-1, keepdims=True),
                                jnp.sum(jnp.where(is_a, 0.0, prod), axis=-1, keepdims=True))
                tot = tot + al[g] * dal
            for g in range(3):
                corr_ref[:, _pair_cols(g, j)] = -al[g] * tot

    row = pl.BlockSpec((tr, wd), lambda i: (i, 0))
    return pl.pallas_call(
        body, name=name, grid=(s // tr,),
        in_specs=[row, row, row], out_specs=[row, row],
        out_shape=[jax.ShapeDtypeStruct(o.shape, F32)] * 2,
        compiler_params=_cparams(("parallel",)),
    )(o, lse, dc)


class _Comm:
    def __init__(self, mode, arrays, kinds=None):
        self.mode, self.arrays, self.kinds = mode, list(arrays), kinds
        self.n = len(self.arrays)

    def out_shapes(self):
        if self.mode == "exchange":
            return [jax.ShapeDtypeStruct(x.shape, x.dtype) for x in self.arrays]
        shapes = []
        for x, kd in zip(self.arrays, self.kinds):
            shp = list(x.shape)
            if kd == "stack":
                shp = [N_DEV] + shp
            else:
                shp[-2 if kd == "rows" else -1] *= N_DEV
            shapes.append(jax.ShapeDtypeStruct(tuple(shp), x.dtype))
        return shapes

    def scratch(self):
        return [pltpu.SemaphoreType.DMA((7 * self.n,)), pltpu.SemaphoreType.DMA((7 * self.n,)),
                pltpu.SemaphoreType.DMA((self.n,))]

    def bind(self, in_refs, out_refs, sems):
        self.x, self.o = in_refs, out_refs
        self.send_sems, self.recv_sems, self.local_sems = sems
        self.pos = (lax.axis_index("x"), lax.axis_index("y"), lax.axis_index("c"))

    def _slot(self, i, px, py, pc):
        p = 4 * px + 2 * py + pc
        kd = self.kinds[i]
        if kd == "stack":
            return self.o[i].at[p]
        nd = len(self.x[i].shape)
        ax = nd - 2 if kd == "rows" else nd - 1
        size = self.x[i].shape[ax]
        idx = tuple(pl.ds(p * size, size) if a == ax else slice(None) for a in range(nd))
        return self.o[i].at[idx]

    def _gcopy(self, i, k, block, to, src=None):
        return pltpu.make_async_remote_copy(
            src_ref=self._slot(i, *block) if src is None else src, dst_ref=self._slot(i, *block),
            send_sem=self.send_sems.at[7 * i + k], recv_sem=self.recv_sems.at[7 * i + k],
            device_id=to, device_id_type=pl.DeviceIdType.MESH)

    def _chips(self):
        mx, my, _ = self.pos
        return [(1 - mx, my), (mx, 1 - my), (1 - mx, 1 - my)]

    def _xcopies(self):
        mx, my, mc = self.pos
        me = 4 * mx + 2 * my + mc
        copies = []
        for k in range(1, N_DEV):
            px = 1 - mx if (k >> 2) & 1 else mx
            py = 1 - my if (k >> 1) & 1 else my
            pc = 1 - mc if k & 1 else mc
            for i in range(self.n):
                copies.append(pltpu.make_async_remote_copy(
                    src_ref=self.x[i].at[4 * px + 2 * py + pc], dst_ref=self.o[i].at[me],
                    send_sem=self.send_sems.at[7 * i + k - 1], recv_sem=self.recv_sems.at[7 * i + k - 1],
                    device_id=(px, py, pc), device_id_type=pl.DeviceIdType.MESH))
        return copies

    def _local(self):
        mx, my, mc = self.pos
        if self.mode == "exchange":
            me = 4 * mx + 2 * my + mc
            return [pltpu.make_async_copy(self.x[i].at[me], self.o[i].at[me], self.local_sems.at[i]) for i in range(self.n)]
        return [pltpu.make_async_copy(self.x[i], self._slot(i, mx, my, mc), self.local_sems.at[i]) for i in range(self.n)]

    def _first(self):
        mx, my, mc = self.pos
        me, sibling = (mx, my, mc), (mx, my, 1 - mc)
        first = [self._gcopy(i, 0, me, sibling, src=self.x[i]) for i in range(self.n)]
        first += [self._gcopy(i, 1 + j, me, (*chip, mc), src=self.x[i]) for j, chip in enumerate(self._chips())
                  for i in range(self.n)]
        return first

    def _passed(self):
        mx, my, mc = self.pos
        return [self._gcopy(i, 4 + j, (*chip, mc), (mx, my, 1 - mc)) for j, chip in enumerate(self._chips())
                for i in range(self.n)]

    def start(self):
        for cp in self._local() + (self._xcopies() if self.mode == "exchange" else self._first()):
            cp.start()

    def mid(self):
        if self.mode == "exchange":
            return
        mx, my, mc = self.pos
        passed = self._passed()
        for j, chip in enumerate(self._chips()):
            for i in range(self.n):
                self._gcopy(i, 1 + j, (*chip, mc), (mx, my, mc)).wait_recv()
                passed[j * self.n + i].start()

    def end(self):
        mx, my, mc = self.pos
        if self.mode == "exchange":
            copies = self._xcopies()
            for cp in copies:
                cp.wait_recv()
            for cp in copies:
                cp.wait_send()
        else:
            for i in range(self.n):
                self._gcopy(i, 0, (mx, my, 1 - mc), (mx, my, mc)).wait_recv()
                for j, chip in enumerate(self._chips()):
                    self._gcopy(i, 4 + j, (*chip, 1 - mc), (mx, my, mc)).wait_recv()
            for cp in self._first() + self._passed():
                cp.wait_send()
        for cp in self._local():
            cp.wait()

    def run(self, *, name):
        n = self.n

        def body(*refs):
            self.bind(refs[:n], refs[n:2 * n], refs[2 * n:])
            self.start()
            self.mid()
            self.end()

        anyspec = pl.BlockSpec(memory_space=pl.ANY)
        return pl.pallas_call(body, name=name, in_specs=[anyspec] * n, out_specs=[anyspec] * n,
                              out_shape=self.out_shapes(), scratch_shapes=self.scratch())(*self.arrays)


def _comm_specs(comm):
    if comm is None:
        return [], [], [], []
    anyspec = pl.BlockSpec(memory_space=pl.ANY)
    return [anyspec] * comm.n, [anyspec] * comm.n, comm.out_shapes(), comm.scratch()


def _comm_hooks(comm, refs, n_in, n_out, n_scr, first, mid, last):
    if comm is None:
        return refs[:n_in], refs[n_in:n_in + n_out], refs[n_in + n_out:]
    c = comm.n
    ins, cin = refs[:n_in], refs[n_in:n_in + c]
    outs, cout = refs[n_in + c:n_in + c + n_out], refs[n_in + c + n_out:n_in + 2 * c + n_out]
    scr, sems = refs[n_in + 2 * c + n_out:n_in + 2 * c + n_out + n_scr], refs[n_in + 2 * c + n_out + n_scr:]
    comm.bind(cin, cout, sems)
    pl.when(first)(comm.start)
    pl.when(mid)(comm.mid)
    pl.when(last)(comm.end)
    return ins, outs, scr


def _adamw_update(g, w, m, v):
    mn = ADAM_B1 * m + (1.0 - ADAM_B1) * g
    vn = ADAM_B2 * v + (1.0 - ADAM_B2) * (g * g)
    m_hat = mn / (1.0 - ADAM_B1 ** ADAM_STEP)
    v_hat = vn / (1.0 - ADAM_B2 ** ADAM_STEP)
    return -ADAM_LR * (m_hat / (jnp.sqrt(v_hat) + ADAM_EPS) + ADAM_WD * w), mn, vn


def _adamw_layers(recvs, w, m, v, *, name):
    nl, ks, ns = w.shape
    tr = _pick(ks, (64, 48))

    def body(*refs):
        rv_refs = refs[:nl]
        w_ref, m_ref, v_ref, g_ref, d_ref, nm_ref, nv_ref = refs[nl:]
        for l in range(nl):
            g = rv_refs[l][0].astype(F32)
            for q in range(1, N_DEV):
                g = g + rv_refs[l][q].astype(F32)
            delta, mn, vn = _adamw_update(g, w_ref[l], m_ref[l], v_ref[l])
            g_ref[l] = g
            d_ref[l] = delta
            nm_ref[l] = mn
            nv_ref[l] = vn

    row = pl.BlockSpec((nl, tr, ns), lambda i: (0, i, 0))
    return pl.pallas_call(
        body, name=name, grid=(ks // tr,),
        in_specs=[pl.BlockSpec((N_DEV, tr, ns), lambda i: (0, i, 0))] * nl + [row] * 3,
        out_specs=[row] * 4,
        out_shape=[jax.ShapeDtypeStruct((nl, ks, ns), F32)] * 4,
        compiler_params=_cparams(("parallel",)),
    )(*recvs, w, m, v)


def _adamw_reduce(recv, w, m, v, *, name):
    r, c = w.shape
    tr = _pick(r, (128, 64, 8))

    def body(rv_ref, w_ref, m_ref, v_ref, g_ref, d_ref, nm_ref, nv_ref):
        g = rv_ref[0]
        for q in range(1, N_DEV):
            g = g + rv_ref[q]
        delta, mn, vn = _adamw_update(g, w_ref[...], m_ref[...], v_ref[...])
        g_ref[...] = g
        d_ref[...] = delta
        nm_ref[...] = mn
        nv_ref[...] = vn

    row = pl.BlockSpec((tr, c), lambda i: (i, 0))
    return pl.pallas_call(
        body, name=name, grid=(r // tr,),
        in_specs=[pl.BlockSpec((N_DEV, tr, c), lambda i: (0, i, 0)), row, row, row],
        out_specs=[row] * 4,
        out_shape=[jax.ShapeDtypeStruct((r, c), F32)] * 4,
        compiler_params=_cparams(("parallel",)),
    )(recv, w, m, v)


_BIG = ("gdn_w_in", "gdn_w_out", "dswa_w_in", "dswa_w_out", "mlp_w1", "mlp_w2")
_SMALL = ("gdn_conv_w", "norm_mix", "norm_mlp", "norm_final", "rel_bias", "gdn_a_log", "gdn_dt_bias", "gdn_norm_w")
_ORDER = ("norm_mix", "norm_mlp", "norm_final", "rel_bias", "gdn_w_in", "gdn_conv_w", "gdn_a_log", "gdn_dt_bias",
          "gdn_norm_w", "gdn_w_out", "dswa_w_in", "dswa_w_out", "mlp_w1", "mlp_w2")
_KIND = dict(gdn_w_in="stack", gdn_w_out="rows", dswa_w_in="stack", dswa_w_out="rows", mlp_w1="cols", mlp_w2="rows")


def _pack_rows(arrs, align):
    rows, counts = [], []
    for a in arrs:
        flat = a.reshape(-1)
        n = -(-flat.shape[0] // D_MODEL)
        flat = jnp.pad(flat, (0, n * D_MODEL - flat.shape[0]))
        rows.append(flat.reshape(n, D_MODEL))
        counts.append(n)
    out = jnp.concatenate(rows, axis=0)
    total = -(-out.shape[0] // align) * align
    return jnp.pad(out, ((0, total - out.shape[0]), (0, 0))), counts


def _unpack_rows(slab, shapes):
    outs, r = [], 0
    for shp in shapes:
        size = int(np.prod(shp))
        n = -(-size // D_MODEL)
        outs.append(slab[r:r + n].reshape(-1)[:size].reshape(shp))
        r += n
    return outs


def _col_shards(full, nshard):
    lead = full.shape[:-1]
    n = full.shape[-1] // nshard
    t = full.reshape(lead + (nshard, n))
    return jnp.moveaxis(t, -2, 0)


def _from_col_shards(g):
    t = jnp.moveaxis(g, 0, -2)
    return t.reshape(t.shape[:-2] + (t.shape[-2] * t.shape[-1],))


def kernel(x, norm_mix, norm_mlp, norm_final, rel_bias, gdn_w_in, gdn_conv_w, gdn_a_log, gdn_dt_bias, gdn_norm_w, gdn_w_out, dswa_w_in, dswa_w_out, mlp_w1, mlp_w2, loss_target, m_norm_mix, m_norm_mlp, m_norm_final, m_rel_bias, m_gdn_w_in, m_gdn_conv_w, m_gdn_a_log, m_gdn_dt_bias, m_gdn_norm_w, m_gdn_w_out, m_dswa_w_in, m_dswa_w_out, m_mlp_w1, m_mlp_w2, v_norm_mix, v_norm_mlp, v_norm_final, v_rel_bias, v_gdn_w_in, v_gdn_conv_w, v_gdn_a_log, v_gdn_dt_bias, v_gdn_norm_w, v_gdn_w_out, v_dswa_w_in, v_dswa_w_out, v_mlp_w1, v_mlp_w2):
    params = dict(norm_mix=norm_mix, norm_mlp=norm_mlp, norm_final=norm_final, rel_bias=rel_bias,
                  gdn_w_in=gdn_w_in, gdn_conv_w=gdn_conv_w, gdn_a_log=gdn_a_log, gdn_dt_bias=gdn_dt_bias,
                  gdn_norm_w=gdn_norm_w, gdn_w_out=gdn_w_out, dswa_w_in=dswa_w_in, dswa_w_out=dswa_w_out,
                  mlp_w1=mlp_w1, mlp_w2=mlp_w2)
    mom_m = dict(norm_mix=m_norm_mix, norm_mlp=m_norm_mlp, norm_final=m_norm_final, rel_bias=m_rel_bias,
                 gdn_w_in=m_gdn_w_in, gdn_conv_w=m_gdn_conv_w, gdn_a_log=m_gdn_a_log, gdn_dt_bias=m_gdn_dt_bias,
                 gdn_norm_w=m_gdn_norm_w, gdn_w_out=m_gdn_w_out, dswa_w_in=m_dswa_w_in, dswa_w_out=m_dswa_w_out,
                 mlp_w1=m_mlp_w1, mlp_w2=m_mlp_w2)
    mom_v = dict(norm_mix=v_norm_mix, norm_mlp=v_norm_mlp, norm_final=v_norm_final, rel_bias=v_rel_bias,
                 gdn_w_in=v_gdn_w_in, gdn_conv_w=v_gdn_conv_w, gdn_a_log=v_gdn_a_log, gdn_dt_bias=v_gdn_dt_bias,
                 gdn_norm_w=v_gdn_norm_w, gdn_w_out=v_gdn_w_out, dswa_w_in=v_dswa_w_in, dswa_w_out=v_dswa_w_out,
                 mlp_w1=v_mlp_w1, mlp_w2=v_mlp_w2)
    xs = x[0]
    target = loss_target[0]
    dist = _Dist(params)
    conv_tail, _ = _pack_rows([gdn_conv_w], 8)
    (conv_g,) = dist.put("start", dist.gather_comm("start", extra=[(conv_tail, "stack")]).run(name="ag_start"))
    conv_parts = [_unpack_rows(conv_g[dev], [gdn_conv_w.shape])[0] for dev in range(N_DEV)]
    conv_full = _from_col_shards(jnp.stack(conv_parts))[:, :, 0, :]

    loss_part, dcur, g_big, rep, g_conv = _local_step(
        xs, target, dict(norm_mix=norm_mix, norm_mlp=norm_mlp, norm_final=norm_final, rel_bias=rel_bias,
                         gdn_a_log=gdn_a_log, gdn_dt_bias=gdn_dt_bias, gdn_norm_w=gdn_norm_w), dist.full, conv_full, dist)
    loss = lax.psum(loss_part[0, 0], ("x", "y", "c"))
    grad_x = dcur[None]

    conv_dev = _col_shards(jnp.stack(g_conv)[:, :, None, :], N_DEV)
    small_send = jnp.stack([_pack_rows([conv_dev[dev]] + [rep[n] for n in _SMALL[1:]], 8)[0] for dev in range(N_DEV)])
    (small_recv,) = dist.got("end", dist.send_comm("end", g_big, extra=[small_send]).run(name="grad_exchange"))

    outs = {}
    for n in _BIG:
        recvs = [dist.recv[(n, l)] for l in range(params[n].shape[0])]
        res = _adamw_layers(recvs, params[n], mom_m[n], mom_v[n], name=f"adamw_{n}")
        for tag, t in zip(("grad", "delta", "new_m", "new_v"), res):
            outs[(tag, n)] = t
    w_slab, _ = _pack_rows([params[n] for n in _SMALL], 8)
    m_slab, _ = _pack_rows([mom_m[n] for n in _SMALL], 8)
    v_slab, _ = _pack_rows([mom_v[n] for n in _SMALL], 8)
    small = _adamw_reduce(small_recv, w_slab, m_slab, v_slab, name="adamw_small")
    shapes = [params[n].shape for n in _SMALL]
    for tag, slab in zip(("grad", "delta", "new_m", "new_v"), small):
        for n, t in zip(_SMALL, _unpack_rows(slab, shapes)):
            outs[(tag, n)] = t
    result = [loss, grad_x]
    for tag in ("grad", "delta", "new_m", "new_v"):
        result += [outs[(tag, n)] for n in _ORDER]
    return tuple(result)


_GATHER = {
    "start": (("gdn_w_in", 0),),
    "gdn_proj0": (("gdn_w_out", 0), ("mlp_w1", 0)),
    "chunk_fwd0": (("mlp_w2", 0), ("dswa_w_in", 0), ("dswa_w_out", 0), ("mlp_w1", 1)),
    "mlp_up0": (("mlp_w2", 1),),
    "mlp_down0": (("gdn_w_in", 1),),
    "attn_fwd1": (("gdn_w_out", 1), ("mlp_w1", 2), ("mlp_w2", 2)),
    "chunk_fwd2": (("dswa_w_in", 1), ("dswa_w_out", 1), ("mlp_w1", 3), ("mlp_w2", 3)),
}
_SEND = {
    "attn_bwd3": (("mlp_w1", 3), ("mlp_w2", 3)),
    "chunk_bwd2": (("dswa_w_in", 1), ("dswa_w_out", 1), ("mlp_w1", 2), ("mlp_w2", 2)),
    "attn_bwd1": (("mlp_w1", 1), ("mlp_w2", 1)),
    "chunk_bwd0": (("gdn_w_in", 1), ("gdn_w_out", 1), ("dswa_w_in", 0), ("dswa_w_out", 0), ("mlp_w2", 0)),
    "pre_bwd0": (("mlp_w1", 0), ("gdn_w_out", 0)),
    "gdn_proj_bwd0": (("gdn_w_in", 0),),
    "end": (),
}


class _Dist:
    def __init__(self, params):
        self.shards = {n: params[n].astype(BF16) for n in _BIG}
        self.full = {n: [None] * params[n].shape[0] for n in _BIG}
        self.recv = {}

    def gather_comm(self, tag, extra=()):
        if tag not in _GATHER:
            return None
        arrays = [self.shards[n][l] for n, l in _GATHER[tag]] + [a for a, _ in extra]
        return _Comm("gather", arrays, [_KIND[n] for n, _ in _GATHER[tag]] + [k for _, k in extra])

    def put(self, tag, outs):
        for (n, l), t in zip(_GATHER.get(tag, ()), outs):
            self.full[n][l] = _from_col_shards(t) if _KIND[n] == "stack" else t
        return outs[len(_GATHER.get(tag, ())):]

    def send_comm(self, tag, g_big, extra=()):
        if tag not in _SEND:
            return None
        arrays = [_col_shards(g_big[n][l], N_DEV) if _KIND[n] == "stack" else g_big[n][l] for n, l in _SEND[tag]]
        return _Comm("exchange", arrays + list(extra))

    def got(self, tag, outs):
        for item, t in zip(_SEND.get(tag, ()), outs):
            self.recv[item] = t
        return outs[len(_SEND.get(tag, ())):]


def _mm_gather(dist, tag, *args, **kw):
    comm = dist and dist.gather_comm(tag)
    if not comm:
        return _mm(*args, **kw)
    res, got = _mm(*args, comm=comm, **kw)
    dist.put(tag, got)
    return res


def _ep_residual_norm(acc, res, g):
    x = acc + res
    r = lax.rsqrt(jnp.mean(x * x, axis=-1, keepdims=True) + RMS_EPS)
    return x, x * r * g


def _ep_rms_bwd(dh, x, dres, g):
    r = lax.rsqrt(jnp.mean(x * x, axis=-1, keepdims=True) + RMS_EPS)
    xn = x * r
    dn = dh * g
    dx = dres + r * (dn - xn * jnp.mean(dn * xn, axis=-1, keepdims=True))
    return dx, dx, jnp.sum(dh * xn, axis=0, keepdims=True)


def _local_step(xs, target, sp, full, conv_full, dist=None):
    s = xs.shape[0]
    norm_mix, norm_mlp, norm_final = sp["norm_mix"], sp["norm_mlp"], sp["norm_final"]
    gdn_a_log, gdn_dt_bias, gdn_norm_w = sp["gdn_a_log"], sp["gdn_dt_bias"], sp["gdn_norm_w"]
    onehot = _bucket_onehot()
    table_t = sp["rel_bias"].T
    bias = _dswa_bias(table_t, onehot, name="dswa_bias").reshape(DSWA_HEADS, DSWA_HALF, 3 * DSWA_HALF)

    saved = []
    cur = xs
    row = lambda v: v.reshape(1, -1)
    h = _rms_fwd(cur, norm_mix[0], name="rms_mix_fwd0")
    for i in range(DEPTH):
        j = i // 2
        sv = dict(x_in=cur, h=h)
        if i % 2 == 0:
            w_in = full["gdn_w_in"][j]
            proj = _mm_gather(dist, f"gdn_proj{i}", h, w_in[:, :GDN_MAIN], name=f"gdn_proj{i}")
            ab = _mm(h, w_in[:, GDN_MAIN:], name=f"gdn_proj_ab{i}")
            qkvn = _gdn_pre_fwd(proj, conv_full[j], name=f"gdn_pre_fwd{i}")
            g_all, beta_all = _gdn_gate_fwd(ab[:, :2 * GDN_HEADS], ab[:, 2 * GDN_HEADS:], gdn_a_log[j], gdn_dt_bias[j],
                                            name=f"gdn_gate_fwd{i}")
            gshape = (2, GDN_HEADS, s // GDN_CHUNK, 1, GDN_CHUNK)
            g_row = g_all.T.reshape(gshape)
            b_row = beta_all.T.reshape(gshape)
            o, states, got = _gdn_chunk_fwd(qkvn, g_row, b_row, name=f"gdn_chunk_fwd{i}",
                                            comm=dist and dist.gather_comm(f"chunk_fwd{i}"))
            if dist:
                dist.put(f"chunk_fwd{i}", got)
            act = _gdn_post_fwd(o, proj, gdn_norm_w[j], name=f"gdn_post_fwd{i}")
            sv.update(proj=proj, ab=ab, qkvn=qkvn, g_row=g_row, b_row=b_row, o=o, states=states, act=act)
            w_out = full["gdn_w_out"][j]
        else:
            w_in = full["dswa_w_in"][j]
            qkv = _mm(h, w_in, name=f"dswa_proj{i}")
            o_n, lse_n, got = _dswa_attn_fwd(qkv, bias, name=f"dswa_attn_fwd{i}",
                                             comm=dist and dist.gather_comm(f"attn_fwd{i}"))
            if dist:
                dist.put(f"attn_fwd{i}", got)
            act = _dswa_combine_fwd(o_n, lse_n, name=f"dswa_comb_fwd{i}")
            sv.update(qkv=qkv, o_n=o_n, lse_n=lse_n, act=act)
            w_out = full["dswa_w_out"][j]
        cur, h2 = _mm(act, w_out, name=f"mix_out{i}", out_dtypes=(F32, BF16), epilogue=_ep_residual_norm,
                      extras=(cur,), vecs=(row(norm_mlp[i]),))
        sv["x_mid"] = cur
        u, a = _mm_gather(dist, f"mlp_up{i}", h2, full["mlp_w1"][i], name=f"mlp_up{i}", out_dtypes=(BF16, BF16),
                          epilogue=lambda acc: (acc, jnp.square(jnp.maximum(acc, 0.0))))
        if i + 1 < DEPTH:
            cur, h = _mm_gather(dist, f"mlp_down{i}", a, full["mlp_w2"][i], name=f"mlp_down{i}", out_dtypes=(F32, BF16),
                                epilogue=_ep_residual_norm, extras=(cur,), vecs=(row(norm_mix[i + 1]),))
        else:
            cur = _mm_gather(dist, f"mlp_down{i}", a, full["mlp_w2"][i], name=f"mlp_down{i}",
                             epilogue=lambda acc, r: (acc + r,), extras=(cur,))
        sv.update(h2=h2, u=u, a=a)
        saved.append(sv)

    loss_part, dcur, dcur_b, dg_final = _loss_head(cur, norm_final, target, name="loss_head")

    g_norm_mix, g_norm_mlp = [None] * DEPTH, [None] * DEPTH
    g_big = {n: [None] * len(full[n]) for n in _BIG}
    g_conv, g_alog, g_dt, g_nw = [None] * 2, [None] * 2, [None] * 2, [None] * 2
    d_table_t = jnp.zeros((DSWA_HEADS, REL_BUCKETS), F32)
    for i in reversed(range(DEPTH)):
        j = i // 2
        sv = saved[i]
        w1, w2 = full["mlp_w1"][i], full["mlp_w2"][i]
        du = _mm(dcur_b, w2, tb=True, name=f"mlp_down_bwd{i}", out_dtypes=(BF16,),
                 epilogue=lambda acc, uu: (acc * (2.0 * jnp.maximum(uu.astype(F32), 0.0)),), extras=(sv["u"],))
        g_big["mlp_w2"][i] = _mm(sv["a"], dcur_b, ta=True, name=f"mlp_w2_grad{i}", out_dtypes=(BF16,), shard="rows")
        g_big["mlp_w1"][i] = _mm(sv["h2"], du, ta=True, name=f"mlp_w1_grad{i}", out_dtypes=(BF16,), shard="cols")
        dmid, dmid_b, g_norm_mlp[i] = _mm(du, w1, tb=True, name=f"mlp_up_bwd{i}", out_dtypes=(F32, BF16), tm=512,
                                          epilogue=_ep_rms_bwd, extras=(sv["x_mid"], dcur), vecs=(row(norm_mlp[i]),),
                                          vec_out=True)
        if i % 2 == 0:
            w_in, w_out = full["gdn_w_in"][j], full["gdn_w_out"][j]
            dact = _mm(dmid_b, w_out, tb=True, name=f"mix_out_bwd{i}")
            g_big["gdn_w_out"][j] = _mm(sv["act"], dmid_b, ta=True, name=f"mix_out_grad{i}", out_dtypes=(BF16,),
                                        shard="rows")
            do, dz, g_nw[j] = _gdn_post_bwd(sv["o"], sv["proj"], gdn_norm_w[j], dact, name=f"gdn_post_bwd{i}")
            dqkvn, dg_row, db_row, got = _gdn_chunk_bwd(sv["qkvn"], sv["g_row"], sv["b_row"], sv["states"], do,
                                                        name=f"gdn_chunk_bwd{i}",
                                                        comm=dist and dist.send_comm(f"chunk_bwd{i}", g_big))
            if dist:
                dist.got(f"chunk_bwd{i}", got)
            dproj, g_conv[j], got = _gdn_pre_bwd(sv["proj"], conv_full[j], dqkvn, dz, name=f"gdn_pre_bwd{i}",
                                                 comm=dist and dist.send_comm(f"pre_bwd{i}", g_big))
            if dist:
                dist.got(f"pre_bwd{i}", got)
            nh2 = 2 * GDN_HEADS
            da_, db_, g_alog[j], g_dt[j] = _gdn_gate_bwd(sv["ab"][:, :nh2], sv["ab"][:, nh2:], gdn_a_log[j], gdn_dt_bias[j],
                                                         dg_row.reshape(nh2, s).T, db_row.reshape(nh2, s).T,
                                                         name=f"gdn_gate_bwd{i}")
            dab = jnp.concatenate([da_, db_], axis=1)
            gw_main = _mm(sv["h"], dproj, ta=True, name=f"gdn_w_in_grad{i}", out_dtypes=(BF16,))
            gw_ab = _mm(sv["h"], dab, ta=True, name=f"gdn_w_ab_grad{i}", out_dtypes=(BF16,))
            g_big["gdn_w_in"][j] = jnp.concatenate([gw_main, gw_ab], axis=1)
            dh_ab = _mm(dab, w_in[:, GDN_MAIN:], tb=True, name=f"gdn_proj_ab_bwd{i}")
            comm = dist and dist.send_comm(f"gdn_proj_bwd{i}", g_big)
            res = _mm(dproj, w_in[:, :GDN_MAIN], tb=True, name=f"gdn_proj_bwd{i}", out_dtypes=(F32, BF16), tm=512,
                      epilogue=lambda acc, r, x, dres, g: _ep_rms_bwd(acc + r, x, dres, g),
                      extras=(dh_ab, sv["x_in"], dmid), vecs=(row(norm_mix[i]),), vec_out=True, comm=comm)
            if comm:
                res, got = res
                dist.got(f"gdn_proj_bwd{i}", got)
            dcur, dcur_b, g_norm_mix[i] = res
        else:
            w_in, w_out = full["dswa_w_in"][j], full["dswa_w_out"][j]
            dact = _mm(dmid_b, w_out, tb=True, name=f"mix_out_bwd{i}")
            g_big["dswa_w_out"][j] = _mm(sv["act"], dmid_b, ta=True, name=f"mix_out_grad{i}", out_dtypes=(BF16,),
                                         shard="rows")
            do_n, corr_n = _dswa_combine_bwd(sv["o_n"], sv["lse_n"], dact, name=f"dswa_comb_bwd{i}")
            *dqkv, dbias, got = _dswa_attn_bwd(sv["qkv"], bias, sv["lse_n"], do_n, corr_n, name=f"dswa_attn_bwd{i}",
                                               comm=dist and dist.send_comm(f"attn_bwd{i}", g_big))
            if dist:
                dist.got(f"attn_bwd{i}", got)
            d_table_t = d_table_t + _dswa_dtable(dbias.reshape(DSWA_HEADS, -1), onehot, name=f"dswa_dtable{i}")
            g_big["dswa_w_in"][j] = jnp.concatenate(
                [_mm(sv["h"], dt, ta=True, name=f"dswa_w_in_grad{i}_{t}", out_dtypes=(BF16,)) for t, dt in enumerate(dqkv)],
                axis=1)
            w_t = [w_in[:, t * DSWA_WIDTH:(t + 1) * DSWA_WIDTH] for t in range(3)]
            dh = _mm(dqkv[0], w_t[0], tb=True, name=f"dswa_proj_bwd{i}_0")
            dh = _mm(dqkv[1], w_t[1], tb=True, name=f"dswa_proj_bwd{i}_1", epilogue=lambda acc, r: (acc + r,), extras=(dh,))
            dcur, dcur_b, g_norm_mix[i] = _mm(
                dqkv[2], w_t[2], tb=True, name=f"dswa_proj_bwd{i}_2", out_dtypes=(F32, BF16), tm=512,
                epilogue=lambda acc, r, x, dres, g: _ep_rms_bwd(acc + r, x, dres, g),
                extras=(dh, sv["x_in"], dmid), vecs=(row(norm_mix[i]),), vec_out=True)

    rep = dict(norm_mix=jnp.concatenate(g_norm_mix, axis=0), norm_mlp=jnp.concatenate(g_norm_mlp, axis=0),
               norm_final=dg_final.reshape(-1), rel_bias=d_table_t.T,
               gdn_a_log=jnp.stack(g_alog).reshape(gdn_a_log.shape), gdn_dt_bias=jnp.stack(g_dt).reshape(gdn_dt_bias.shape),
               gdn_norm_w=jnp.stack(g_nw).reshape(gdn_norm_w.shape))
    return loss_part, dcur, g_big, rep, g_conv
```

```python
import functools
import math

import jax
import jax.numpy as jnp
import numpy as np
from jax import lax
from jax.experimental import pallas as pl
from jax.experimental.pallas import tpu as pltpu

F32 = jnp.float32
BF16 = jnp.bfloat16
HP = lax.Precision.HIGHEST

N_DEV = 8
D_MODEL = 1024
DEPTH = 4
RMS_EPS = 1e-6
NEG_INF = -1e30

GDN_HEADS = 8
GDN_DK = 128
GDN_CONV = 5
GDN_CHUNK = 128
GDN_QKV = 3 * GDN_HEADS * GDN_DK
GDN_MAIN = GDN_QKV + GDN_HEADS * GDN_DK
GDN_AB = 4 * GDN_HEADS

DSWA_DILS = (1, 4, 16)
DSWA_HG = 6
DSWA_E = 64
DSWA_HEADS = 18
DSWA_WIDTH = DSWA_HEADS * DSWA_E
DSWA_HALF = 64
DSWA_PG = DSWA_HG // 2
DSWA_UNROLL = 4
REL_BUCKETS = 32
REL_MAX_DIST = 1024

ADAM_LR = 0.001
ADAM_B1 = 0.9
ADAM_B2 = 0.999
ADAM_EPS = 1e-08
ADAM_WD = 0.01
ADAM_STEP = 10

VMEM_LIMIT = 56 * 1024 * 1024


def _cparams(sem=None, **kw):
    return pltpu.CompilerParams(dimension_semantics=sem, vmem_limit_bytes=VMEM_LIMIT, **kw)


def _pick(dim, cands):
    for c in cands:
        if dim % c == 0:
            return c
    return dim


def _bdot(a, b):
    return jnp.dot(a.astype(BF16), b.astype(BF16), preferred_element_type=F32)


def _bdot_nt(a, b):
    return lax.dot_general(a.astype(BF16), b.astype(BF16), (((1,), (1,)), ((), ())),
                           preferred_element_type=F32)


def _bdot_tn(a, b):
    return lax.dot_general(a.astype(BF16), b.astype(BF16), (((0,), (0,)), ((), ())),
                           preferred_element_type=F32)


def _hdot(a, b):
    return jnp.dot(a, b, precision=HP, preferred_element_type=F32)


def _hdot_tn(a, b):
    return lax.dot_general(a, b, (((0,), (0,)), ((), ())), precision=HP, preferred_element_type=F32)


def _hdot_nt(a, b):
    return lax.dot_general(a, b, (((1,), (1,)), ((), ())), precision=HP, preferred_element_type=F32)


def _sigmoid(x):
    return 1.0 / (1.0 + jnp.exp(-x))


def _mm(a, b, *, name, ta=False, tb=False, out_dtypes=(F32,), epilogue=None, extras=(), vecs=(), vec_out=False,
        tm=None, tn=None, tk=None, shard=None, comm=None, b_cols=None):
    if ta:
        kdim, m = a.shape
    else:
        m, kdim = a.shape
    b0, bsz = b_cols or (0, b.shape[1])
    n = b.shape[0] if tb else bsz
    assert not tb or kdim == bsz
    if shard == "rows":
        tm = m // N_DEV if (m // N_DEV) % 128 == 0 else m
    if shard == "cols":
        tn = n // N_DEV
    tm = tm or _pick(m, (1024, 1152, 512, 384, 256, 128))
    tn = tn or _pick(n, (1024, 1152, 512, 384, 256, 128))
    tk = tk or _pick(kdim, (1024, 1152, 512, 384, 256, 128))
    nk = kdim // tk
    n_out = len(out_dtypes) + (1 if vec_out else 0)
    n_ex = len(extras) + len(vecs)
    rows_all = shard == "rows" and tm == m

    gi, gj = m // tm, n // tn

    def body(*refs):
        i, j, k = pl.program_id(0), pl.program_id(1), pl.program_id(2)
        inner = (j == 0) & (k == 0)
        ins, out_refs, (acc_ref,) = _comm_hooks(
            comm, refs, 2 + n_ex, n_out, 1, (i == 0) & inner, (i == (3 * gi) // 4) & inner,
            (i == gi - 1) & (j == gj - 1) & (k == nk - 1))
        a_ref, b_ref, ex_refs = ins[0], ins[1], ins[2:]
        if vec_out:
            out_refs, vec_ref = out_refs[:-1], out_refs[-1]

        @pl.when(k == 0)
        def _():
            acc_ref[...] = jnp.zeros_like(acc_ref)

        av = a_ref[...].astype(BF16)
        bv = b_ref[...].astype(BF16)
        dims = (((0 if ta else 1,), (1 if tb else 0,)), ((), ()))
        acc_ref[...] += lax.dot_general(av, bv, dims, preferred_element_type=F32)

        @pl.when(k == nk - 1)
        def _():
            acc = acc_ref[...]
            outs = (acc,) if epilogue is None else epilogue(acc, *[r[...] for r in ex_refs])
            if vec_out:
                part = outs[-1]

                @pl.when(i == 0)
                def _():
                    vec_ref[...] = part

                @pl.when(i > 0)
                def _():
                    vec_ref[...] += part
            for r, o in zip(out_refs, outs):
                if rows_all:
                    for p in range(N_DEV):
                        r[p] = o[p * (m // N_DEV):(p + 1) * (m // N_DEV)].astype(r.dtype)
                else:
                    r[...] = o.astype(r.dtype)

    a_spec = pl.BlockSpec((tk, tm), lambda i, j, k: (k, i)) if ta else pl.BlockSpec((tm, tk), lambda i, j, k: (i, k))
    assert b0 % (tk if tb else tn) == 0
    boff = b0 // (tk if tb else tn)
    b_spec = (pl.BlockSpec((tn, tk), lambda i, j, k: (j, k + boff)) if tb
              else pl.BlockSpec((tk, tn), lambda i, j, k: (k, j + boff)))
    o_spec = pl.BlockSpec((tm, tn), lambda i, j, k: (i, j))
    v_spec = pl.BlockSpec((1, tn), lambda i, j, k: (0, j))
    out_specs = [o_spec] * len(out_dtypes) + ([v_spec] if vec_out else [])
    out_shape = [jax.ShapeDtypeStruct((m, n), dt) for dt in out_dtypes]
    out_shape += [jax.ShapeDtypeStruct((1, n), F32)] if vec_out else []
    if shard == "rows":
        out_shape = [jax.ShapeDtypeStruct((N_DEV, m // N_DEV, n), out_dtypes[0])]
        out_specs = [pl.BlockSpec((N_DEV, m // N_DEV, tn), lambda i, j, k: (0, 0, j)) if rows_all
                     else pl.BlockSpec((None, tm, tn), lambda i, j, k: (i, 0, j))]
    if shard == "cols":
        out_shape = [jax.ShapeDtypeStruct((N_DEV, m, tn), out_dtypes[0])]
        out_specs = [pl.BlockSpec((None, tm, tn), lambda i, j, k: (j, i, 0))]
    c_in, c_out, c_shape, c_scr = _comm_specs(comm)
    outs = pl.pallas_call(
        body, name=name,
        grid=(gi, gj, nk),
        in_specs=[a_spec, b_spec] + [o_spec] * len(extras) + [v_spec] * len(vecs) + c_in,
        out_specs=out_specs + c_out,
        out_shape=out_shape + c_shape,
        scratch_shapes=[pltpu.VMEM((tm, tn), F32)] + c_scr,
        compiler_params=_cparams(("arbitrary",) * 3 if comm or vec_out else ("parallel", "parallel", "arbitrary")),
    )(a, b, *extras, *vecs, *(comm.arrays if comm else []))
    res = outs[0] if n_out == 1 else tuple(outs[:n_out])
    return (res, outs[n_out:]) if comm else res


def _rms_fwd(x, g, *, name):
    s, d = x.shape
    tr = _pick(s, (512, 256, 128))

    def body(x_ref, g_ref, h_ref):
        xv = x_ref[...]
        r = lax.rsqrt(jnp.mean(xv * xv, axis=-1, keepdims=True) + RMS_EPS)
        h_ref[...] = (xv * r * g_ref[...]).astype(h_ref.dtype)

    return pl.pallas_call(
        body, name=name, grid=(s // tr,),
        in_specs=[pl.BlockSpec((tr, d), lambda i: (i, 0)), pl.BlockSpec((1, d), lambda i: (0, 0))],
        out_specs=pl.BlockSpec((tr, d), lambda i: (i, 0)),
        out_shape=jax.ShapeDtypeStruct((s, d), BF16),
        compiler_params=_cparams(("parallel",)),
    )(x, g.reshape(1, d))


def _rms_bwd(x, g, dh, dres, *, name):
    s, d = x.shape
    tr = _pick(s, (512, 256, 128))

    def body(x_ref, g_ref, dh_ref, dres_ref, dx_ref, dxb_ref, dg_ref):
        i = pl.program_id(0)
        xv = x_ref[...]
        r = lax.rsqrt(jnp.mean(xv * xv, axis=-1, keepdims=True) + RMS_EPS)
        xn = xv * r
        dhv = dh_ref[...]
        dn = dhv * g_ref[...]
        dx = dres_ref[...] + r * (dn - xn * jnp.mean(dn * xn, axis=-1, keepdims=True))
        dx_ref[...] = dx
        dxb_ref[...] = dx.astype(dxb_ref.dtype)
        part = jnp.sum(dhv * xn, axis=0, keepdims=True)

        @pl.when(i == 0)
        def _():
            dg_ref[...] = part

        @pl.when(i > 0)
        def _():
            dg_ref[...] += part

    row = pl.BlockSpec((tr, d), lambda i: (i, 0))
    vec = pl.BlockSpec((1, d), lambda i: (0, 0))
    return pl.pallas_call(
        body, name=name, grid=(s // tr,),
        in_specs=[row, vec, row, row], out_specs=[row, row, vec],
        out_shape=[jax.ShapeDtypeStruct((s, d), F32), jax.ShapeDtypeStruct((s, d), BF16),
                   jax.ShapeDtypeStruct((1, d), F32)],
        compiler_params=_cparams(("arbitrary",)),
    )(x, g.reshape(1, d), dh, dres)


def _loss_head(x, g, target, *, name):
    s, d = x.shape
    tr = _pick(s, (512, 256, 128))

    def body(x_ref, g_ref, t_ref, loss_ref, dx_ref, dxb_ref, dg_ref):
        i = pl.program_id(0)
        xv = x_ref[...]
        gv = g_ref[...]
        r = lax.rsqrt(jnp.mean(xv * xv, axis=-1, keepdims=True) + RMS_EPS)
        xn = xv * r
        err = xn * gv - t_ref[...]
        lpart = 0.5 * jnp.sum(jnp.mean(err * err, axis=-1, keepdims=True), axis=0, keepdims=True)
        dy = err * (1.0 / d)
        dn = dy * gv
        dx = r * (dn - xn * jnp.mean(dn * xn, axis=-1, keepdims=True))
        dx_ref[...] = dx
        dxb_ref[...] = dx.astype(dxb_ref.dtype)
        gpart = jnp.sum(dy * xn, axis=0, keepdims=True)

        @pl.when(i == 0)
        def _():
            dg_ref[...] = gpart
            loss_ref[...] = lpart

        @pl.when(i > 0)
        def _():
            dg_ref[...] += gpart
            loss_ref[...] += lpart

    row = pl.BlockSpec((tr, d), lambda i: (i, 0))
    vec = pl.BlockSpec((1, d), lambda i: (0, 0))
    one = pl.BlockSpec((1, 1), lambda i: (0, 0))
    return pl.pallas_call(
        body, name=name, grid=(s // tr,),
        in_specs=[row, vec, row], out_specs=[one, row, row, vec],
        out_shape=[jax.ShapeDtypeStruct((1, 1), F32), jax.ShapeDtypeStruct((s, d), F32),
                   jax.ShapeDtypeStruct((s, d), BF16), jax.ShapeDtypeStruct((1, d), F32)],
        compiler_params=_cparams(("arbitrary",)),
    )(x, g.reshape(1, d), target)


def _shift_rows(x, sft, rows):
    s = x.shape[0]
    if sft == 0:
        return x
    y = pltpu.roll(x, (-sft) % s, 0)
    edge = slice(0, 8) if sft < 0 else slice(s - 8, s)
    ok = (rows[edge] + sft >= 0) & (rows[edge] + sft < s)
    fixed = jnp.where(ok, y[edge], 0.0)
    return jnp.concatenate([fixed, y[8:]] if sft < 0 else [y[:s - 8], fixed], axis=0)


def _gdn_pre_fwd(proj, conv_w, *, name):
    s = proj.shape[0]
    nblk = GDN_QKV // 128
    pad = GDN_CONV // 2

    def body(x_ref, w_ref, o_ref):
        j = pl.program_id(0)
        x = x_ref[...]
        rows = lax.broadcasted_iota(jnp.int32, x.shape, 0)
        c = jnp.zeros_like(x)
        for t in range(GDN_CONV):
            c = c + w_ref[pl.ds(t, 1), :] * _shift_rows(x, t - pad, rows)
        a = c * _sigmoid(c)
        rinv = lax.rsqrt(jnp.sum(a * a, axis=-1, keepdims=True) + 1e-6)
        scale = jnp.where(j < GDN_HEADS, GDN_DK ** -0.5, 1.0)
        o_ref[...] = jnp.where(j >= 2 * GDN_HEADS, a, a * (rinv * scale))

    return pl.pallas_call(
        body, name=name, grid=(nblk,),
        in_specs=[pl.BlockSpec((s, 128), lambda j: (0, j)), pl.BlockSpec((GDN_CONV, 128), lambda j: (0, j))],
        out_specs=pl.BlockSpec((s, 128), lambda j: (0, j)),
        out_shape=jax.ShapeDtypeStruct((s, GDN_QKV), F32),
        compiler_params=_cparams(("parallel",)),
    )(proj, conv_w)


def _gdn_pre_bwd(proj, conv_w, dqkv, dproj, *, name, comm=None):
    s = proj.shape[0]
    nblk = GDN_QKV // 128
    pad = GDN_CONV // 2

    def body(*refs):
        j = pl.program_id(0)
        (x_ref, w_ref, df_ref, dbk_ref, _), (dx_ref, dw_ref), _ = _comm_hooks(
            comm, refs, 5, 2, 0, j == 0, j == nblk // 2, j == nblk - 1)
        x = x_ref[...]
        rows = lax.broadcasted_iota(jnp.int32, x.shape, 0)
        xs = [_shift_rows(x, t - pad, rows) for t in range(GDN_CONV)]
        c = jnp.zeros_like(x)
        for t in range(GDN_CONV):
            c = c + w_ref[pl.ds(t, 1), :] * xs[t]
        sg = _sigmoid(c)
        a = c * sg
        rinv = lax.rsqrt(jnp.sum(a * a, axis=-1, keepdims=True) + 1e-6)
        scale = jnp.where(j < GDN_HEADS, GDN_DK ** -0.5, 1.0)
        dy = df_ref[...] + dbk_ref[...]
        nh = a * rinv
        da_n = (rinv * scale) * (dy - nh * jnp.sum(dy * nh, axis=-1, keepdims=True))
        da = jnp.where(j >= 2 * GDN_HEADS, dy, da_n)
        dc = da * (sg * (1.0 + c * (1.0 - sg)))
        dx = jnp.zeros_like(x)
        for t in range(GDN_CONV):
            dx = dx + w_ref[pl.ds(t, 1), :] * _shift_rows(dc, pad - t, rows)
            dw_ref[pl.ds(t, 1), :] = jnp.sum(dc * xs[t], axis=0, keepdims=True)
        dx_ref[...] = dx.astype(dx_ref.dtype)

    col = pl.BlockSpec((s, 128), lambda j: (0, j))
    wsp = pl.BlockSpec((GDN_CONV, 128), lambda j: (0, j))
    c_in, c_out, c_shape, c_scr = _comm_specs(comm)
    res = pl.pallas_call(
        body, name=name, grid=(nblk,),
        in_specs=[col, wsp, col, col, pl.BlockSpec(memory_space=pl.ANY)] + c_in, out_specs=[col, wsp] + c_out,
        out_shape=[jax.ShapeDtypeStruct(dproj.shape, BF16), jax.ShapeDtypeStruct((GDN_CONV, GDN_QKV), F32)] + c_shape,
        input_output_aliases={4: 0},
        scratch_shapes=c_scr,
        compiler_params=_cparams(("arbitrary",) if comm else ("parallel",)),
    )(proj, conv_w, dqkv[0], dqkv[1], dproj, *(comm.arrays if comm else []))
    return res[0], res[1], res[2:]


def _softplus(x):
    return jnp.maximum(x, 0.0) + jnp.log(1.0 + jnp.exp(-jnp.abs(x)))


def _gdn_gate_fwd(a, b, a_log, dt_bias, *, name):
    s = a.shape[0]
    nh = 2 * GDN_HEADS

    def body(a_ref, b_ref, al_ref, dt_ref, g_ref, be_ref):
        g_ref[...] = -jnp.exp(al_ref[...]) * _softplus(a_ref[...] + dt_ref[...])
        be_ref[...] = _sigmoid(b_ref[...])

    return pl.pallas_call(
        body, name=name,
        out_shape=[jax.ShapeDtypeStruct((s, nh), F32), jax.ShapeDtypeStruct((s, nh), F32)],
        compiler_params=_cparams(),
    )(a, b, a_log.reshape(1, nh), dt_bias.reshape(1, nh))


def _gdn_gate_bwd(a, b, a_log, dt_bias, dg, dbeta, *, name):
    s = a.shape[0]
    nh = 2 * GDN_HEADS

    def body(a_ref, b_ref, al_ref, dt_ref, dg_ref, db_ref, da_ref, dbb_ref, dal_ref, ddt_ref):
        ea = jnp.exp(al_ref[...])
        z = a_ref[...] + dt_ref[...]
        dgv = dg_ref[...]
        dz = dgv * (-ea) * _sigmoid(z)
        dal_ref[...] = jnp.sum(dgv * (-ea) * _softplus(z), axis=0, keepdims=True)
        ddt_ref[...] = jnp.sum(dz, axis=0, keepdims=True)
        sb = _sigmoid(b_ref[...])
        da_ref[...] = dz
        dbb_ref[...] = db_ref[...] * sb * (1.0 - sb)

    return pl.pallas_call(
        body, name=name,
        out_shape=[jax.ShapeDtypeStruct((s, nh), F32), jax.ShapeDtypeStruct((s, nh), F32),
                   jax.ShapeDtypeStruct((1, nh), F32), jax.ShapeDtypeStruct((1, nh), F32)],
        compiler_params=_cparams(),
    )(a, b, a_log.reshape(1, nh), dt_bias.reshape(1, nh), dg, dbeta)


def _chunk_masks(d):
    c = GDN_CHUNK
    ii = lax.broadcasted_iota(jnp.int32, (c, c), 0)
    jj = lax.broadcasted_iota(jnp.int32, (c, c), 1)
    dif = (ii - jj) * (1 - 2 * d)
    mi = dif >= 0
    mit = dif <= 0
    ms = dif > 0
    eye = ii == jj
    bds = [(ii >> sh) == (jj >> sh) for sh in range(3, c.bit_length() - 1)]
    return dict(mi=mi, mit=mit, ms=ms, eye=eye, bds=bds,
                mif=mi.astype(F32), mitf=mit.astype(F32), eyef=eye.astype(F32))


class _V:
    def __init__(self, xs):
        self.xs = tuple(xs)

    def __add__(self, o):
        return _lift(lambda a, b: a + b)(self, o)

    def __radd__(self, o):
        return _lift(lambda a, b: b + a)(self, o)

    def __sub__(self, o):
        return _lift(lambda a, b: a - b)(self, o)

    def __rsub__(self, o):
        return _lift(lambda a, b: b - a)(self, o)

    def __mul__(self, o):
        return _lift(lambda a, b: a * b)(self, o)

    def __rmul__(self, o):
        return _lift(lambda a, b: b * a)(self, o)

    def __and__(self, o):
        return _lift(lambda a, b: a & b)(self, o)

    def __neg__(self):
        return _lift(lambda a: -a)(self)

    def __rtruediv__(self, o):
        return _lift(lambda a, b: b / a)(self, o)


def _lift(f):
    def g(*args, **kw):
        n = next(len(a.xs) for a in args if isinstance(a, _V))
        return _V(f(*[a.xs[i] if isinstance(a, _V) else a for a in args], **kw) for i in range(n))
    return g


_vwhere, _vsum, _vexp, _vnot = _lift(jnp.where), _lift(jnp.sum), _lift(jnp.exp), _lift(jnp.logical_not)
_vhdot, _vhdot_tn = _lift(_bdot), _lift(_bdot_tn)
_vbdot, _vbdot_nt, _vbdot_tn = _lift(_bdot), _lift(_bdot_nt), _lift(_bdot_tn)
_vcat = _lift(lambda a, b: jnp.concatenate([a, b], axis=1))
_vlo = _lift(lambda a, n: a[:, :n])
_vhi = _lift(lambda a, n: a[:, n:])


def _both_masks(n):
    m = [_chunk_masks(d) for d in range(2)]
    mk = {key: _V([m[0][key]] * n + [m[1][key]] * n) for key in m[0] if key != "bds"}
    mk["bds"] = [_V([m[0]["bds"][i]] * n + [m[1]["bds"][i]] * n) for i in range(len(m[0]["bds"]))]
    return mk


def _tri_inv(a, mk):
    eyef = mk["eyef"]
    bds = mk["bds"]
    a8 = _vwhere(bds[0], a, 0.0)
    a2 = _vhdot(a8, a8)
    a4 = _vhdot(a2, a2)
    t = _vhdot(_vhdot(eyef - a8, eyef + a2), eyef + a4)
    for inner, outer in zip(bds, bds[1:] + [None]):
        off = _vnot(inner) if outer is None else (outer & _vnot(inner))
        low = _vwhere(off, a, 0.0)
        t = t - _vhdot(_vhdot(t, low), t)
    return t


def _chunk_prep(q, k, v, g_row, b_row, mk):
    dv = GDN_DK
    g_col = _vsum(mk["eyef"] * g_row, axis=1, keepdims=True)
    b_col = _vsum(mk["eyef"] * b_row, axis=1, keepdims=True)
    gc_col = _vsum(mk["mif"] * g_row, axis=1, keepdims=True)
    gc_row = _vsum(mk["mitf"] * g_col, axis=0, keepdims=True)
    gl = _vsum(g_row, axis=1, keepdims=True)
    decay = _vwhere(mk["mi"], _vexp(_vwhere(mk["mi"], gc_col - gc_row, 0.0)), 0.0)
    eg = _vexp(gc_col)
    e2 = _vexp(gl - gc_col)
    egl = _vexp(gl)
    kb = k * b_col
    pm = _vbdot_nt(kb, k)
    a = _vwhere(mk["ms"], pm * decay, 0.0)
    t = _tri_inv(a, mk)
    sol = _vhdot(t, _vcat(v * b_col, kb * eg))
    u, w = _vlo(sol, dv), _vhi(sol, dv)
    qm = _vbdot_nt(q, k)
    return dict(b_col=b_col, decay=decay, eg=eg, e2=e2, egl=egl, kb=kb, pm=pm, t=t, u=u, w=w,
                qm=qm, intra=qm * decay, qd=q * eg, kd=k * e2)


def _chunk_fwd_step(p, state):
    v_new = p["u"] - _vbdot(p["w"], state)
    o = _vbdot(p["qd"], state) + _vbdot(p["intra"], v_new)
    new_state = state * p["egl"] + _vbdot_tn(p["kd"], v_new)
    return o, new_state


def _chunk_bwd_step(q, k, v, p, mk, state, dso, do):
    dv_dim = GDN_DK
    v_new = p["u"] - _vbdot(p["w"], state)
    dvn = _vbdot_tn(p["intra"], do) + _vbdot(p["kd"], dso)
    dintra = _vbdot_nt(do, v_new)
    dqd = _vbdot_nt(do, state)
    ds = p["egl"] * dso + _vbdot_tn(p["qd"], do) - _vbdot_tn(p["w"], dvn)
    dkd = _vbdot_nt(v_new, dso)
    dgl = _vsum(_vsum(dso * state, axis=1, keepdims=True), axis=0, keepdims=True) * p["egl"]
    dw = -_vbdot_nt(dvn, state)
    drhs = _vhdot_tn(p["t"], _vcat(dvn, dw))
    dru, drw = _vlo(drhs, dv_dim), _vhi(drhs, dv_dim)
    da = -_vwhere(mk["ms"], _vbdot_nt(drhs, _vcat(p["u"], p["w"])), 0.0)
    b_col = p["b_col"]
    dv = dru * b_col
    dbeta = _vsum(dru * v, axis=1, keepdims=True)
    dkb = drw * p["eg"]
    deg = _vsum(drw * p["kb"], axis=1, keepdims=True)
    dp = da * p["decay"]
    ddecay = da * p["pm"]
    dkb = dkb + _vbdot(dp, k)
    dk = _vbdot_tn(dp, p["kb"])
    dqm = dintra * p["decay"]
    ddecay = ddecay + dintra * p["qm"]
    dq = _vbdot(dqm, k)
    dk = dk + _vbdot_tn(dqm, q)
    dd = ddecay * p["decay"]
    dgc_col = _vsum(dd, axis=1, keepdims=True)
    dgc_row = -_vsum(dd, axis=0, keepdims=True)
    dq = dq + dqd * p["eg"]
    deg = deg + _vsum(dqd * q, axis=1, keepdims=True)
    dk = dk + dkd * p["e2"]
    de2 = _vsum(dkd * k, axis=1, keepdims=True) * p["e2"]
    dgl = dgl + _vsum(de2, axis=0, keepdims=True)
    dgc_col = dgc_col - de2 + deg * p["eg"]
    dk = dk + dkb * b_col
    dbeta = dbeta + _vsum(dkb * k, axis=1, keepdims=True)
    dgc_col = dgc_col + _vsum(mk["eyef"] * dgc_row, axis=1, keepdims=True)
    dg_row = _vsum(mk["mif"] * dgc_col, axis=0, keepdims=True) + dgl
    dbeta_row = _vsum(mk["eyef"] * dbeta, axis=0, keepdims=True)
    return dq, dk, dv, dg_row, dbeta_row, ds


def _gdn_chunk_fwd(qkvn, g5, b5, *, name, comm=None):
    s = qkvn.shape[0]
    c = GDN_CHUNK
    nc = s // c
    h_, dk = GDN_HEADS, GDN_DK

    def body(*refs):
        n = pl.program_id(0)
        ins, outs, (st_scr,) = _comm_hooks(comm, refs, 6, 4, 1, n == 0, n == (3 * nc) // 4, n == nc - 1)
        x_refs, g_refs, b_refs = ins[0:2], ins[2:4], ins[4:6]
        o_refs, st_refs = outs[0:2], outs[2:4]

        @pl.when(n == 0)
        def _():
            st_scr[...] = jnp.zeros_like(st_scr)

        ch = [(d, h) for d in range(2) for h in range(h_)]
        mk = _both_masks(h_)
        q, k, v = (_V(x_refs[d][:, (t * h_ + h) * dk:(t * h_ + h + 1) * dk] for d, h in ch) for t in range(3))
        g, b = (_V(r[d][0, h, 0] for d, h in ch) for r in (g_refs, b_refs))
        state = _V(st_scr[d * h_ + h] for d, h in ch)
        o, new_state = _chunk_fwd_step(_chunk_prep(q, k, v, g, b, mk), state)
        for i, (d, h) in enumerate(ch):
            st_refs[d][h, 0] = state.xs[i]
            st_scr[d * h_ + h] = new_state.xs[i]
            o_refs[d][:, h * dk:(h + 1) * dk] = o.xs[i]

    ce = (lambda n: n, lambda n: nc - 1 - n)
    xs = [pl.BlockSpec((c, 3 * h_ * dk), lambda n, d=d: (ce[d](n), 0)) for d in range(2)]
    gates = [pl.BlockSpec((1, h_, 1, 1, c), lambda n, d=d: (d, 0, ce[d](n), 0, 0)) for d in range(2)]
    os_ = [pl.BlockSpec((c, h_ * dk), lambda n, d=d: (ce[d](n), 0)) for d in range(2)]
    sts = [pl.BlockSpec((h_, 1, dk, dk), lambda n, d=d: (0, ce[d](n), 0, 0)) for d in range(2)]
    c_in, c_out, c_shape, c_scr = _comm_specs(comm)
    res = pl.pallas_call(
        body, name=name, grid=(nc,),
        in_specs=xs + gates + gates + c_in,
        out_specs=os_ + sts + c_out,
        out_shape=[jax.ShapeDtypeStruct((s, h_ * dk), F32)] * 2 + [jax.ShapeDtypeStruct((h_, nc, dk, dk), F32)] * 2 + c_shape,
        scratch_shapes=[pltpu.VMEM((2 * h_, dk, dk), F32)] + c_scr,
        compiler_params=_cparams(("arbitrary",)),
    )(qkvn, qkvn, g5, g5, b5, b5, *(comm.arrays if comm else []))
    return res[0:2], res[2:4], res[4:]


def _gdn_chunk_bwd(qkvn, g5, b5, states, do, *, name, comm=None):
    s = qkvn.shape[0]
    c = GDN_CHUNK
    nc = s // c
    h_, dk = GDN_HEADS, GDN_DK

    def body(*refs):
        i = pl.program_id(0)
        ins, outs, (ds_scr,) = _comm_hooks(comm, refs, 10, 6, 1, i == 0, i == nc // 2, i == nc - 1)
        x_refs, g_refs, b_refs, st_refs, do_refs = ins[0:2], ins[2:4], ins[4:6], ins[6:8], ins[8:10]
        dx_refs, dg_refs, db_refs = outs[0:2], outs[2:4], outs[4:6]

        @pl.when(i == 0)
        def _():
            ds_scr[...] = jnp.zeros_like(ds_scr)

        ch = [(d, h) for d in range(2) for h in range(h_)]
        mk = _both_masks(h_)
        q, k, v = (_V(x_refs[d][:, (t * h_ + h) * dk:(t * h_ + h + 1) * dk] for d, h in ch) for t in range(3))
        g, b = (_V(r[d][0, h, 0] for d, h in ch) for r in (g_refs, b_refs))
        state = _V(st_refs[d][h, 0] for d, h in ch)
        dso = _V(ds_scr[d * h_ + h] for d, h in ch)
        dov = _V(do_refs[d][:, h * dk:(h + 1) * dk] for d, h in ch)
        res = _chunk_bwd_step(q, k, v, _chunk_prep(q, k, v, g, b, mk), mk, state, dso, dov)
        for (d, h), (dq, dkk, dvv, dg_r, db_r, ds) in zip(ch, zip(*[r.xs for r in res])):
            ds_scr[d * h_ + h] = ds
            dg_refs[d][h, 0] = dg_r
            db_refs[d][h, 0] = db_r
            for t, val in enumerate((dq, dkk, dvv)):
                dx_refs[d][:, (t * h_ + h) * dk:(t * h_ + h + 1) * dk] = val

    ce = (lambda i: nc - 1 - i, lambda i: i)
    both = lambda mk_spec: [mk_spec(d) for d in range(2)]
    xs = both(lambda d: pl.BlockSpec((c, 3 * h_ * dk), lambda i: (ce[d](i), 0)))
    gates = both(lambda d: pl.BlockSpec((1, h_, 1, 1, c), lambda i: (d, 0, ce[d](i), 0, 0)))
    sts = both(lambda d: pl.BlockSpec((h_, 1, dk, dk), lambda i: (0, ce[d](i), 0, 0)))
    dos = both(lambda d: pl.BlockSpec((c, h_ * dk), lambda i: (ce[d](i), 0)))
    gouts = both(lambda d: pl.BlockSpec((h_, 1, 1, c), lambda i: (0, ce[d](i), 0, 0)))
    c_in, c_out, c_shape, c_scr = _comm_specs(comm)
    res = pl.pallas_call(
        body, name=name, grid=(nc,),
        in_specs=xs + gates + gates + sts + dos + c_in,
        out_specs=xs + gouts + gouts + c_out,
        out_shape=[jax.ShapeDtypeStruct((s, 3 * h_ * dk), F32)] * 2
        + [jax.ShapeDtypeStruct((h_, nc, 1, c), F32)] * 4 + c_shape,
        scratch_shapes=[pltpu.VMEM((2 * h_, dk, dk), F32)] + c_scr,
        compiler_params=_cparams(("arbitrary",)),
    )(qkvn, qkvn, g5, g5, b5, b5, states[0], states[1], do, do, *(comm.arrays if comm else []))
    return res[0:2], jnp.stack(res[2:4]), jnp.stack(res[4:6]), res[6:]


def _gdn_post_fwd(o, z, norm_w, *, name):
    s = o[0].shape[0]
    h_, dk = GDN_HEADS, GDN_DK

    def body(of_ref, ob_ref, z_ref, w_ref, a_ref):
        ov = of_ref[...] + ob_ref[...]
        zv = z_ref[...]
        r = lax.rsqrt(jnp.mean(ov * ov, axis=-1, keepdims=True) + RMS_EPS)
        a_ref[...] = (ov * r * w_ref[...] * (zv * _sigmoid(zv))).astype(a_ref.dtype)

    col = pl.BlockSpec((s, dk), lambda h: (0, h))
    return pl.pallas_call(
        body, name=name, grid=(h_,),
        in_specs=[col, col, pl.BlockSpec((s, dk), lambda h: (0, 3 * h_ + h)), pl.BlockSpec((1, dk), lambda h: (0, 0))],
        out_specs=col,
        out_shape=jax.ShapeDtypeStruct((s, h_ * dk), BF16),
        compiler_params=_cparams(("parallel",)),
    )(o[0], o[1], z, norm_w.reshape(1, dk))


def _gdn_post_bwd(o, z, norm_w, dact, *, name):
    s = o[0].shape[0]
    h_, dk = GDN_HEADS, GDN_DK

    def body(of_ref, ob_ref, z_ref, w_ref, da_ref, do_ref, dz_ref, dw_ref):
        h = pl.program_id(0)
        ov = of_ref[...] + ob_ref[...]
        zv = z_ref[...]
        wv = w_ref[...]
        dav = da_ref[...]
        r = lax.rsqrt(jnp.mean(ov * ov, axis=-1, keepdims=True) + RMS_EPS)
        nrm = ov * r
        sg = _sigmoid(zv)
        sz = zv * sg
        dn = dav * wv * sz
        do_ref[...] = r * (dn - nrm * jnp.mean(dn * nrm, axis=-1, keepdims=True))
        dz_ref[...] = (dav * nrm * wv * (sg * (1.0 + zv * (1.0 - sg)))).astype(dz_ref.dtype)
        part = jnp.sum(dav * nrm * sz, axis=0, keepdims=True)

        @pl.when(h == 0)
        def _():
            dw_ref[...] = part

        @pl.when(h > 0)
        def _():
            dw_ref[...] += part

    col = pl.BlockSpec((s, dk), lambda h: (0, h))
    vec = pl.BlockSpec((1, dk), lambda h: (0, 0))
    return pl.pallas_call(
        body, name=name, grid=(h_,),
        in_specs=[col, col, pl.BlockSpec((s, dk), lambda h: (0, 3 * h_ + h)), vec, col],
        out_specs=[col, pl.BlockSpec((s, dk), lambda h: (0, 3 * h_ + h)), vec],
        out_shape=[jax.ShapeDtypeStruct((s, h_ * dk), F32), jax.ShapeDtypeStruct((s, GDN_MAIN), BF16),
                   jax.ShapeDtypeStruct((1, dk), F32)],
        compiler_params=_cparams(("arbitrary",)),
    )(o[0], o[1], z, norm_w.reshape(1, dk), dact)


def _rel_bucket(rel):
    nb = REL_BUCKETS // 2
    max_exact = nb // 2
    ret = jnp.where(rel > 0, nb, 0)
    n = jnp.abs(rel)
    nf = jnp.maximum(n, 1).astype(F32)
    large = max_exact + (jnp.log(nf / max_exact) / math.log(REL_MAX_DIST / max_exact)
                         * (nb - max_exact)).astype(jnp.int32)
    large = jnp.minimum(large, nb - 1)
    return ret + jnp.where(n < max_exact, n, large)


def _bucket_onehot():
    half = DSWA_HALF
    outs = []
    for dil in DSWA_DILS:
        rel = (jnp.arange(3 * half)[None, :] - half - jnp.arange(half)[:, None]) * dil
        outs.append(jax.nn.one_hot(_rel_bucket(rel).reshape(-1), REL_BUCKETS, dtype=F32, axis=0))
    return jnp.stack(outs)


def _head_group_select(vals):
    rows = lax.broadcasted_iota(jnp.int32, vals[0].shape, 0)
    return jnp.where(rows < DSWA_HG, vals[0], jnp.where(rows < 2 * DSWA_HG, vals[1], vals[2]))


def _dswa_bias(table_t, onehot, *, name):
    p = onehot.shape[-1]

    def body(t_ref, oh_ref, b_ref):
        b_ref[...] = _head_group_select([_hdot(t_ref[...], oh_ref[g]) for g in range(3)])

    return pl.pallas_call(body, name=name, out_shape=jax.ShapeDtypeStruct((DSWA_HEADS, p), F32),
                          compiler_params=_cparams())(table_t, onehot)


def _dswa_dtable(dbias, onehot, *, name):
    def body(d_ref, oh_ref, t_ref):
        t_ref[...] = _head_group_select([_hdot_nt(d_ref[...], oh_ref[g]) for g in range(3)])

    return pl.pallas_call(body, name=name, out_shape=jax.ShapeDtypeStruct((DSWA_HEADS, REL_BUCKETS), F32),
                          compiler_params=_cparams())(dbias, onehot)


def _rows(start, dil):
    if dil == 1:
        return pl.ds(pl.multiple_of(start, DSWA_HALF), DSWA_HALF)
    return pl.ds(start, DSWA_HALF, stride=dil)


def _attn_blocks(it, s, dil):
    half = DSWA_HALF
    nbs = s // half // dil
    ii = lax.broadcasted_iota(jnp.int32, (half, 3 * half), 0)
    jj = lax.broadcasted_iota(jnp.int32, (half, 3 * half), 1)
    band = jnp.abs(jj - half - ii) <= half
    out = []
    for u in range(DSWA_UNROLL):
        blk = it * DSWA_UNROLL + u
        r, b = blk // nbs, blk % nbs
        own = r + dil * half * b
        prev = own - jnp.where(b > 0, dil * half, 0)
        nxt = own + jnp.where(b < nbs - 1, dil * half, 0)
        ok = band & ((jj >= half) | (b > 0)) & ((jj < 2 * half) | (b < nbs - 1))
        out.append(((prev, own, nxt), ok))
    return out


def _attn_chains(q_ref, k_ref, v_ref, blocks, dil):
    lane = lax.broadcasted_iota(jnp.int32, (DSWA_HALF, 2 * DSWA_E), 1)
    qm, kw, vw, valid, hmask = [], [], [], [], []
    for (prev, own, nxt), ok in blocks:
        q = q_ref[_rows(own, dil), :].astype(BF16)
        k = jnp.concatenate([k_ref[_rows(st, dil), :] for st in (prev, own, nxt)], axis=0).astype(BF16)
        v = jnp.concatenate([v_ref[_rows(st, dil), :] for st in (prev, own, nxt)], axis=0).astype(BF16)
        for hd in range(2):
            mine = (lane < DSWA_E) if hd == 0 else (lane >= DSWA_E)
            qm.append(jnp.where(mine, q, jnp.zeros_like(q)))
            kw.append(k)
            vw.append(v)
            valid.append(ok)
            hmask.append(mine)
    return _V(qm), _V(kw), _V(vw), _V(valid), _V(hmask)


def _per_group(pr, fn):
    for gi, dil in enumerate(DSWA_DILS):
        pl.when(pr // DSWA_PG == gi)(functools.partial(fn, dil))


_vmax, _vlog = _lift(jnp.max), _lift(jnp.log)


def _dswa_attn_fwd(qkv, bias, *, name, comm=None):
    s = qkv.shape[0]
    half, e = DSWA_HALF, DSWA_E
    npair = DSWA_HEADS // 2

    def body(*refs):
        pr = pl.program_id(0)
        (q_ref, k_ref, v_ref, bias_ref), (o_ref, lse_ref), _ = _comm_hooks(
            comm, refs, 4, 2, 0, pr == 0, pr == (3 * npair) // 4, pr == npair - 1)
        bias_v = _V([bias_ref[0], bias_ref[1]] * DSWA_UNROLL)

        def run(dil):
            def step(it, carry):
                blocks = _attn_blocks(it, s, dil)
                qm, kw, vw, valid, hmask = _attn_chains(q_ref, k_ref, v_ref, blocks, dil)
                sc = _vwhere(valid, _vbdot_nt(qm, kw) * (e ** -0.5) + bias_v, NEG_INF)
                m = _vmax(sc, axis=-1, keepdims=True)
                p = _vexp(sc - m)
                l = _vsum(p, axis=-1, keepdims=True)
                o = _vbdot(p * (1.0 / l), vw)
                lse = m + _vlog(l)
                for u, ((_, own, _), _) in enumerate(blocks):
                    is_a = hmask.xs[2 * u]
                    o_ref[_rows(own, dil), :] = jnp.where(is_a, o.xs[2 * u], o.xs[2 * u + 1])
                    lse_ref[_rows(own, dil), :] = jnp.where(is_a, lse.xs[2 * u], lse.xs[2 * u + 1])
                return carry

            lax.fori_loop(0, s // half // DSWA_UNROLL, step, 0)

        _per_group(pr, run)

    col = lambda t: pl.BlockSpec((s, 2 * e), lambda p: (0, t * npair + p))
    pair = pl.BlockSpec((s, 2 * e), lambda p: (0, p))
    c_in, c_out, c_shape, c_scr = _comm_specs(comm)
    res = pl.pallas_call(
        body, name=name, grid=(npair,),
        in_specs=[col(0), col(1), col(2), pl.BlockSpec((2, half, 3 * half), lambda p: (p, 0, 0))] + c_in,
        out_specs=[pair, pair] + c_out,
        out_shape=[jax.ShapeDtypeStruct((s, npair * 2 * e), F32)] * 2 + c_shape,
        scratch_shapes=c_scr,
        compiler_params=_cparams(("arbitrary",)),
    )(qkv, qkv, qkv, bias, *(comm.arrays if comm else []))
    return res[0], res[1], res[2:]


def _dswa_attn_bwd(qkv, bias, lse, do, corr, *, name, comm=None):
    s = qkv.shape[0]
    half, e = DSWA_HALF, DSWA_E
    npair = DSWA_HEADS // 2
    w = 2 * e

    def body(*refs):
        pr = pl.program_id(0)
        (q_ref, k_ref, v_ref, bias_ref, lse_ref, do_ref, corr_ref), (dq_ref, dk_ref, dv_ref, db_ref), _ = _comm_hooks(
            comm, refs, 7, 4, 0, pr == 0, pr == npair // 2, pr == npair - 1)
        bias_v = _V([bias_ref[0], bias_ref[1]] * DSWA_UNROLL)
        dk_ref[...] = jnp.zeros_like(dk_ref)
        dv_ref[...] = jnp.zeros_like(dv_ref)

        def run(dil):
            def step(it, dbias):
                blocks = _attn_blocks(it, s, dil)
                qm, kw, vw, valid, hmask = _attn_chains(q_ref, k_ref, v_ref, blocks, dil)
                hd = [0, 1] * DSWA_UNROLL
                rows = [_rows(own, dil) for (_, own, _), _ in blocks for _ in range(2)]
                lse_c = _V(lse_ref[rw, :][:, h * e:h * e + 1] for rw, h in zip(rows, hd))
                corr_c = _V(corr_ref[rw, :][:, h * e:h * e + 1] for rw, h in zip(rows, hd))
                dov = _vwhere(hmask, _V(do_ref[rw, :] for rw in rows), 0.0)
                sc = _vbdot_nt(qm, kw) * (e ** -0.5) + bias_v
                p = _vwhere(valid, _vexp(_vwhere(valid, sc, 0.0) - lse_c), 0.0)
                dsc = p * (_vbdot_nt(dov, vw) + corr_c)
                dq = _vbdot(dsc, kw) * (e ** -0.5)
                dkc = _vbdot_tn(dsc, qm) * (e ** -0.5)
                dvc = _vbdot_tn(p, dov)
                for u, (starts, _) in enumerate(blocks):
                    dq_ref[_rows(starts[1], dil), :] = jnp.where(hmask.xs[2 * u], dq.xs[2 * u], dq.xs[2 * u + 1])
                    dk_u = dkc.xs[2 * u] + dkc.xs[2 * u + 1]
                    dv_u = dvc.xs[2 * u] + dvc.xs[2 * u + 1]
                    for t, st in enumerate(starts):
                        dk_ref[_rows(st, dil), :] += dk_u[t * half:(t + 1) * half]
                        dv_ref[_rows(st, dil), :] += dv_u[t * half:(t + 1) * half]
                da, db = dbias
                for u in range(DSWA_UNROLL):
                    da, db = da + dsc.xs[2 * u], db + dsc.xs[2 * u + 1]
                return da, db

            zero = jnp.zeros((half, 3 * half), F32)
            da, db = lax.fori_loop(0, s // half // DSWA_UNROLL, step, (zero, zero))
            db_ref[0] = da
            db_ref[1] = db

        _per_group(pr, run)

    col = lambda t: pl.BlockSpec((s, w), lambda p: (0, t * npair + p))
    ps = pl.BlockSpec((s, w), lambda p: (0, p))
    bs = pl.BlockSpec((2, half, 3 * half), lambda p: (p, 0, 0))
    c_in, c_out, c_shape, c_scr = _comm_specs(comm)
    res = pl.pallas_call(
        body, name=name, grid=(npair,),
        in_specs=[col(0), col(1), col(2), bs, ps, ps, ps] + c_in,
        out_specs=[ps, ps, ps, bs] + c_out,
        out_shape=[jax.ShapeDtypeStruct((s, npair * w), F32)] * 3
        + [jax.ShapeDtypeStruct((DSWA_HEADS, half, 3 * half), F32)] + c_shape,
        scratch_shapes=c_scr,
        compiler_params=_cparams(("arbitrary",)),
    )(qkv, qkv, qkv, bias, lse, do, corr, *(comm.arrays if comm else []))
    return res[0], res[1], res[2], res[3], res[4:]


def _pair_cols(g, j):
    w = 2 * DSWA_E
    return slice((g * DSWA_PG + j) * w, (g * DSWA_PG + j + 1) * w)


def _group_weights(l_ref, j):
    ls = [l_ref[:, _pair_cols(g, j)] for g in range(3)]
    m = jnp.maximum(jnp.maximum(ls[0], ls[1]), ls[2])
    es = [jnp.exp(x - m) for x in ls]
    inv = 1.0 / (es[0] + es[1] + es[2])
    return [x * inv for x in es]


def _dswa_combine_fwd(o, lse, *, name):
    s, wd = o.shape
    tr = _pick(s, (512, 256, 128))

    def body(o_ref, l_ref, c_ref):
        for j in range(DSWA_PG):
            al = _group_weights(l_ref, j)
            for g in range(3):
                c_ref[:, _pair_cols(g, j)] = (o_ref[:, _pair_cols(g, j)] * al[g]).astype(c_ref.dtype)

    row = pl.BlockSpec((tr, wd), lambda i: (i, 0))
    return pl.pallas_call(
        body, name=name, grid=(s // tr,),
        in_specs=[row, row], out_specs=row,
        out_shape=jax.ShapeDtypeStruct(o.shape, BF16),
        compiler_params=_cparams(("parallel",)),
    )(o, lse)


def _dswa_combine_bwd(o, lse, dc, *, name):
    s, wd = o.shape
    tr = _pick(s, (512, 256, 128))

    def body(o_ref, l_ref, dc_ref, do_ref, corr_ref):
        lane = lax.broadcasted_iota(jnp.int32, (tr, 2 * DSWA_E), 1)
        is_a = lane < DSWA_E
        for j in range(DSWA_PG):
            al = _group_weights(l_ref, j)
            tot = jnp.zeros((tr, 2 * DSWA_E), F32)
            for g in range(3):
                cols = _pair_cols(g, j)
                dcv = dc_ref[:, cols]
                do_ref[:, cols] = dcv * al[g]
                prod = dcv * o_ref[:, cols]
                dal = jnp.where(is_a, jnp.sum(jnp.where(is_a, prod, 0.0), axis=-1, keepdims=True),
                                jnp.sum(jnp.where(is_a, 0.0, prod), axis=-1, keepdims=True))
                tot = tot + al[g] * dal
            for g in range(3):
                corr_ref[:, _pair_cols(g, j)] = -al[g] * tot

    row = pl.BlockSpec((tr, wd), lambda i: (i, 0))
    return pl.pallas_call(
        body, name=name, grid=(s // tr,),
        in_specs=[row, row, row], out_specs=[row, row],
        out_shape=[jax.ShapeDtypeStruct(o.shape, F32)] * 2,
        compiler_params=_cparams(("parallel",)),
    )(o, lse, dc)


class _Comm:
    def __init__(self, mode, arrays, kinds=None):
        self.mode, self.arrays, self.kinds = mode, list(arrays), kinds
        self.n = len(self.arrays)

    def out_shapes(self):
        if self.mode == "exchange":
            return [jax.ShapeDtypeStruct(x.shape, x.dtype) for x in self.arrays]
        shapes = []
        for x, kd in zip(self.arrays, self.kinds):
            shp = list(x.shape)
            if kd == "stack":
                shp = [N_DEV] + shp
            else:
                shp[-2 if kd == "rows" else -1] *= N_DEV
            shapes.append(jax.ShapeDtypeStruct(tuple(shp), x.dtype))
        return shapes

    def scratch(self):
        return [pltpu.SemaphoreType.DMA((7 * self.n,)), pltpu.SemaphoreType.DMA((7 * self.n,)),
                pltpu.SemaphoreType.DMA((self.n,))]

    def bind(self, in_refs, out_refs, sems):
        self.x, self.o = in_refs, out_refs
        self.send_sems, self.recv_sems, self.local_sems = sems
        self.pos = (lax.axis_index("x"), lax.axis_index("y"), lax.axis_index("c"))

    def _slot(self, i, px, py, pc):
        p = 4 * px + 2 * py + pc
        kd = self.kinds[i]
        if kd == "stack":
            return self.o[i].at[p]
        nd = len(self.x[i].shape)
        ax = nd - 2 if kd == "rows" else nd - 1
        size = self.x[i].shape[ax]
        idx = tuple(pl.ds(p * size, size) if a == ax else slice(None) for a in range(nd))
        return self.o[i].at[idx]

    def _gcopy(self, i, k, block, to, src=None):
        return pltpu.make_async_remote_copy(
            src_ref=self._slot(i, *block) if src is None else src, dst_ref=self._slot(i, *block),
            send_sem=self.send_sems.at[7 * i + k], recv_sem=self.recv_sems.at[7 * i + k],
            device_id=to, device_id_type=pl.DeviceIdType.MESH)

    def _chips(self):
        mx, my, _ = self.pos
        return [(1 - mx, my), (mx, 1 - my), (1 - mx, 1 - my)]

    def _xcopies(self):
        mx, my, mc = self.pos
        me = 4 * mx + 2 * my + mc
        copies = []
        for k in range(1, N_DEV):
            px = 1 - mx if (k >> 2) & 1 else mx
            py = 1 - my if (k >> 1) & 1 else my
            pc = 1 - mc if k & 1 else mc
            for i in range(self.n):
                copies.append(pltpu.make_async_remote_copy(
                    src_ref=self.x[i].at[4 * px + 2 * py + pc], dst_ref=self.o[i].at[me],
                    send_sem=self.send_sems.at[7 * i + k - 1], recv_sem=self.recv_sems.at[7 * i + k - 1],
                    device_id=(px, py, pc), device_id_type=pl.DeviceIdType.MESH))
        return copies

    def _local(self):
        mx, my, mc = self.pos
        if self.mode == "exchange":
            me = 4 * mx + 2 * my + mc
            return [pltpu.make_async_copy(self.x[i].at[me], self.o[i].at[me], self.local_sems.at[i]) for i in range(self.n)]
        return [pltpu.make_async_copy(self.x[i], self._slot(i, mx, my, mc), self.local_sems.at[i]) for i in range(self.n)]

    def _first(self):
        mx, my, mc = self.pos
        me, sibling = (mx, my, mc), (mx, my, 1 - mc)
        first = [self._gcopy(i, 0, me, sibling, src=self.x[i]) for i in range(self.n)]
        first += [self._gcopy(i, 1 + j, me, (*chip, mc), src=self.x[i]) for j, chip in enumerate(self._chips())
                  for i in range(self.n)]
        return first

    def _passed(self):
        mx, my, mc = self.pos
        return [self._gcopy(i, 4 + j, (*chip, mc), (mx, my, 1 - mc)) for j, chip in enumerate(self._chips())
                for i in range(self.n)]

    def start(self):
        for cp in self._local() + (self._xcopies() if self.mode == "exchange" else self._first()):
            cp.start()

    def mid(self):
        if self.mode == "exchange":
            return
        mx, my, mc = self.pos
        passed = self._passed()
        for j, chip in enumerate(self._chips()):
            for i in range(self.n):
                self._gcopy(i, 1 + j, (*chip, mc), (mx, my, mc)).wait_recv()
                passed[j * self.n + i].start()

    def end(self):
        mx, my, mc = self.pos
        if self.mode == "exchange":
            copies = self._xcopies()
            for cp in copies:
                cp.wait_recv()
            for cp in copies:
                cp.wait_send()
        else:
            for i in range(self.n):
                self._gcopy(i, 0, (mx, my, 1 - mc), (mx, my, mc)).wait_recv()
                for j, chip in enumerate(self._chips()):
                    self._gcopy(i, 4 + j, (*chip, 1 - mc), (mx, my, mc)).wait_recv()
            for cp in self._first() + self._passed():
                cp.wait_send()
        for cp in self._local():
            cp.wait()

    def run(self, *, name):
        n = self.n

        def body(*refs):
            self.bind(refs[:n], refs[n:2 * n], refs[2 * n:])
            self.start()
            self.mid()
            self.end()

        anyspec = pl.BlockSpec(memory_space=pl.ANY)
        return pl.pallas_call(body, name=name, in_specs=[anyspec] * n, out_specs=[anyspec] * n,
                              out_shape=self.out_shapes(), scratch_shapes=self.scratch())(*self.arrays)


def _comm_specs(comm):
    if comm is None:
        return [], [], [], []
    anyspec = pl.BlockSpec(memory_space=pl.ANY)
    return [anyspec] * comm.n, [anyspec] * comm.n, comm.out_shapes(), comm.scratch()


def _comm_hooks(comm, refs, n_in, n_out, n_scr, first, mid, last):
    if comm is None:
        return refs[:n_in], refs[n_in:n_in + n_out], refs[n_in + n_out:]
    c = comm.n
    ins, cin = refs[:n_in], refs[n_in:n_in + c]
    outs, cout = refs[n_in + c:n_in + c + n_out], refs[n_in + c + n_out:n_in + 2 * c + n_out]
    scr, sems = refs[n_in + 2 * c + n_out:n_in + 2 * c + n_out + n_scr], refs[n_in + 2 * c + n_out + n_scr:]
    comm.bind(cin, cout, sems)
    pl.when(first)(comm.start)
    pl.when(mid)(comm.mid)
    pl.when(last)(comm.end)
    return ins, outs, scr


def _adamw_update(g, w, m, v):
    mn = ADAM_B1 * m + (1.0 - ADAM_B1) * g
    vn = ADAM_B2 * v + (1.0 - ADAM_B2) * (g * g)
    m_hat = mn / (1.0 - ADAM_B1 ** ADAM_STEP)
    v_hat = vn / (1.0 - ADAM_B2 ** ADAM_STEP)
    return -ADAM_LR * (m_hat / (jnp.sqrt(v_hat) + ADAM_EPS) + ADAM_WD * w), mn, vn


def _adamw_layers(recvs, w, m, v, *, name):
    nl, ks, ns = w.shape
    tr = _pick(ks, (64, 48))

    def body(*refs):
        rv_refs = refs[:nl]
        w_ref, m_ref, v_ref, g_ref, d_ref, nm_ref, nv_ref = refs[nl:]
        for l in range(nl):
            g = rv_refs[l][0].astype(F32)
            for q in range(1, N_DEV):
                g = g + rv_refs[l][q].astype(F32)
            delta, mn, vn = _adamw_update(g, w_ref[l], m_ref[l], v_ref[l])
            g_ref[l] = g
            d_ref[l] = delta
            nm_ref[l] = mn
            nv_ref[l] = vn

    row = pl.BlockSpec((nl, tr, ns), lambda i: (0, i, 0))
    return pl.pallas_call(
        body, name=name, grid=(ks // tr,),
        in_specs=[pl.BlockSpec((N_DEV, tr, ns), lambda i: (0, i, 0))] * nl + [row] * 3,
        out_specs=[row] * 4,
        out_shape=[jax.ShapeDtypeStruct((nl, ks, ns), F32)] * 4,
        compiler_params=_cparams(("parallel",)),
    )(*recvs, w, m, v)


def _adamw_reduce(recv, w, m, v, *, name):
    r, c = w.shape
    tr = _pick(r, (128, 64, 8))

    def body(rv_ref, w_ref, m_ref, v_ref, g_ref, d_ref, nm_ref, nv_ref):
        g = rv_ref[0]
        for q in range(1, N_DEV):
            g = g + rv_ref[q]
        delta, mn, vn = _adamw_update(g, w_ref[...], m_ref[...], v_ref[...])
        g_ref[...] = g
        d_ref[...] = delta
        nm_ref[...] = mn
        nv_ref[...] = vn

    row = pl.BlockSpec((tr, c), lambda i: (i, 0))
    return pl.pallas_call(
        body, name=name, grid=(r // tr,),
        in_specs=[pl.BlockSpec((N_DEV, tr, c), lambda i: (0, i, 0)), row, row, row],
        out_specs=[row] * 4,
        out_shape=[jax.ShapeDtypeStruct((r, c), F32)] * 4,
        compiler_params=_cparams(("parallel",)),
    )(recv, w, m, v)


_BIG = ("gdn_w_in", "gdn_w_out", "dswa_w_in", "dswa_w_out", "mlp_w1", "mlp_w2")
_SMALL = ("gdn_conv_w", "norm_mix", "norm_mlp", "norm_final", "rel_bias", "gdn_a_log", "gdn_dt_bias", "gdn_norm_w")
_ORDER = ("norm_mix", "norm_mlp", "norm_final", "rel_bias", "gdn_w_in", "gdn_conv_w", "gdn_a_log", "gdn_dt_bias",
          "gdn_norm_w", "gdn_w_out", "dswa_w_in", "dswa_w_out", "mlp_w1", "mlp_w2")
_KIND = dict(gdn_w_in="stack", gdn_w_out="rows", dswa_w_in="stack", dswa_w_out="rows", mlp_w1="cols", mlp_w2="rows")


def _pack_rows(arrs, align):
    rows, counts = [], []
    for a in arrs:
        flat = a.reshape(-1)
        n = -(-flat.shape[0] // D_MODEL)
        flat = jnp.pad(flat, (0, n * D_MODEL - flat.shape[0]))
        rows.append(flat.reshape(n, D_MODEL))
        counts.append(n)
    out = jnp.concatenate(rows, axis=0)
    total = -(-out.shape[0] // align) * align
    return jnp.pad(out, ((0, total - out.shape[0]), (0, 0))), counts


def _unpack_rows(slab, shapes):
    outs, r = [], 0
    for shp in shapes:
        size = int(np.prod(shp))
        n = -(-size // D_MODEL)
        outs.append(slab[r:r + n].reshape(-1)[:size].reshape(shp))
        r += n
    return outs


def _col_shards(full, nshard):
    lead = full.shape[:-1]
    n = full.shape[-1] // nshard
    t = full.reshape(lead + (nshard, n))
    return jnp.moveaxis(t, -2, 0)


def _from_col_shards(g):
    t = jnp.moveaxis(g, 0, -2)
    return t.reshape(t.shape[:-2] + (t.shape[-2] * t.shape[-1],))


def kernel(x, norm_mix, norm_mlp, norm_final, rel_bias, gdn_w_in, gdn_conv_w, gdn_a_log, gdn_dt_bias, gdn_norm_w, gdn_w_out, dswa_w_in, dswa_w_out, mlp_w1, mlp_w2, loss_target, m_norm_mix, m_norm_mlp, m_norm_final, m_rel_bias, m_gdn_w_in, m_gdn_conv_w, m_gdn_a_log, m_gdn_dt_bias, m_gdn_norm_w, m_gdn_w_out, m_dswa_w_in, m_dswa_w_out, m_mlp_w1, m_mlp_w2, v_norm_mix, v_norm_mlp, v_norm_final, v_rel_bias, v_gdn_w_in, v_gdn_conv_w, v_gdn_a_log, v_gdn_dt_bias, v_gdn_norm_w, v_gdn_w_out, v_dswa_w_in, v_dswa_w_out, v_mlp_w1, v_mlp_w2):
    params = dict(norm_mix=norm_mix, norm_mlp=norm_mlp, norm_final=norm_final, rel_bias=rel_bias,
                  gdn_w_in=gdn_w_in, gdn_conv_w=gdn_conv_w, gdn_a_log=gdn_a_log, gdn_dt_bias=gdn_dt_bias,
                  gdn_norm_w=gdn_norm_w, gdn_w_out=gdn_w_out, dswa_w_in=dswa_w_in, dswa_w_out=dswa_w_out,
                  mlp_w1=mlp_w1, mlp_w2=mlp_w2)
    mom_m = dict(norm_mix=m_norm_mix, norm_mlp=m_norm_mlp, norm_final=m_norm_final, rel_bias=m_rel_bias,
                 gdn_w_in=m_gdn_w_in, gdn_conv_w=m_gdn_conv_w, gdn_a_log=m_gdn_a_log, gdn_dt_bias=m_gdn_dt_bias,
                 gdn_norm_w=m_gdn_norm_w, gdn_w_out=m_gdn_w_out, dswa_w_in=m_dswa_w_in, dswa_w_out=m_dswa_w_out,
                 mlp_w1=m_mlp_w1, mlp_w2=m_mlp_w2)
    mom_v = dict(norm_mix=v_norm_mix, norm_mlp=v_norm_mlp, norm_final=v_norm_final, rel_bias=v_rel_bias,
                 gdn_w_in=v_gdn_w_in, gdn_conv_w=v_gdn_conv_w, gdn_a_log=v_gdn_a_log, gdn_dt_bias=v_gdn_dt_bias,
                 gdn_norm_w=v_gdn_norm_w, gdn_w_out=v_gdn_w_out, dswa_w_in=v_dswa_w_in, dswa_w_out=v_dswa_w_out,
                 mlp_w1=v_mlp_w1, mlp_w2=v_mlp_w2)
    xs = x[0]
    target = loss_target[0]
    dist = _Dist(params)
    conv_tail, _ = _pack_rows([gdn_conv_w], 8)
    (conv_g,) = dist.put("start", dist.gather_comm("start", extra=[(conv_tail, "stack")]).run(name="ag_start"))
    conv_parts = [_unpack_rows(conv_g[dev], [gdn_conv_w.shape])[0] for dev in range(N_DEV)]
    conv_full = _from_col_shards(jnp.stack(conv_parts))[:, :, 0, :]

    loss_part, dcur, g_big, rep, g_conv = _local_step(
        xs, target, dict(norm_mix=norm_mix, norm_mlp=norm_mlp, norm_final=norm_final, rel_bias=rel_bias,
                         gdn_a_log=gdn_a_log, gdn_dt_bias=gdn_dt_bias, gdn_norm_w=gdn_norm_w), dist.full, conv_full, dist)
    loss = lax.psum(loss_part[0, 0], ("x", "y", "c"))
    grad_x = dcur[None]

    conv_dev = _col_shards(jnp.stack(g_conv)[:, :, None, :], N_DEV)
    small_send = jnp.stack([_pack_rows([conv_dev[dev]] + [rep[n] for n in _SMALL[1:]], 8)[0] for dev in range(N_DEV)])
    (small_recv,) = dist.got("end", dist.send_comm("end", g_big, extra=[small_send]).run(name="grad_exchange"))

    outs = {}
    for n in _BIG:
        recvs = [dist.recv[(n, l)] for l in range(params[n].shape[0])]
        res = _adamw_layers(recvs, params[n], mom_m[n], mom_v[n], name=f"adamw_{n}")
        for tag, t in zip(("grad", "delta", "new_m", "new_v"), res):
            outs[(tag, n)] = t
    w_slab, _ = _pack_rows([params[n] for n in _SMALL], 8)
    m_slab, _ = _pack_rows([mom_m[n] for n in _SMALL], 8)
    v_slab, _ = _pack_rows([mom_v[n] for n in _SMALL], 8)
    small = _adamw_reduce(small_recv, w_slab, m_slab, v_slab, name="adamw_small")
    shapes = [params[n].shape for n in _SMALL]
    for tag, slab in zip(("grad", "delta", "new_m", "new_v"), small):
        for n, t in zip(_SMALL, _unpack_rows(slab, shapes)):
            outs[(tag, n)] = t
    result = [loss, grad_x]
    for tag in ("grad", "delta", "new_m", "new_v"):
        result += [outs[(tag, n)] for n in _ORDER]
    return tuple(result)


_GATHER = {
    "start": (("gdn_w_in", 0),),
    "gdn_proj0": (("gdn_w_out", 0), ("mlp_w1", 0)),
    "chunk_fwd0": (("mlp_w2", 0), ("dswa_w_in", 0), ("dswa_w_out", 0), ("mlp_w1", 1)),
    "mlp_up0": (("mlp_w2", 1),),
    "mlp_down0": (("gdn_w_in", 1),),
    "attn_fwd1": (("gdn_w_out", 1), ("mlp_w1", 2), ("mlp_w2", 2)),
    "chunk_fwd2": (("dswa_w_in", 1), ("dswa_w_out", 1), ("mlp_w1", 3), ("mlp_w2", 3)),
}
_SEND = {
    "attn_bwd3": (("mlp_w1", 3), ("mlp_w2", 3)),
    "chunk_bwd2": (("dswa_w_in", 1), ("dswa_w_out", 1), ("mlp_w1", 2), ("mlp_w2", 2)),
    "attn_bwd1": (("mlp_w1", 1), ("mlp_w2", 1)),
    "chunk_bwd0": (("gdn_w_in", 1), ("gdn_w_out", 1), ("dswa_w_in", 0), ("dswa_w_out", 0), ("mlp_w2", 0)),
    "pre_bwd0": (("mlp_w1", 0), ("gdn_w_out", 0)),
    "gdn_proj_bwd0": (("gdn_w_in", 0),),
    "end": (),
}


class _Dist:
    def __init__(self, params):
        self.shards = {n: params[n].astype(BF16) for n in _BIG}
        self.full = {n: [None] * params[n].shape[0] for n in _BIG}
        self.recv = {}

    def gather_comm(self, tag, extra=()):
        if tag not in _GATHER:
            return None
        arrays = [self.shards[n][l] for n, l in _GATHER[tag]] + [a for a, _ in extra]
        return _Comm("gather", arrays, [_KIND[n] for n, _ in _GATHER[tag]] + [k for _, k in extra])

    def put(self, tag, outs):
        for (n, l), t in zip(_GATHER.get(tag, ()), outs):
            self.full[n][l] = _from_col_shards(t) if _KIND[n] == "stack" else t
        return outs[len(_GATHER.get(tag, ())):]

    def send_comm(self, tag, g_big, extra=()):
        if tag not in _SEND:
            return None
        arrays = [_col_shards(g_big[n][l], N_DEV) if _KIND[n] == "stack" else g_big[n][l] for n, l in _SEND[tag]]
        return _Comm("exchange", arrays + list(extra))

    def got(self, tag, outs):
        for item, t in zip(_SEND.get(tag, ()), outs):
            self.recv[item] = t
        return outs[len(_SEND.get(tag, ())):]


def _mm_gather(dist, tag, *args, **kw):
    comm = dist and dist.gather_comm(tag)
    if not comm:
        return _mm(*args, **kw)
    res, got = _mm(*args, comm=comm, **kw)
    dist.put(tag, got)
    return res


def _ep_residual_norm(acc, res, g):
    x = acc + res
    r = lax.rsqrt(jnp.mean(x * x, axis=-1, keepdims=True) + RMS_EPS)
    return x, x * r * g


def _ep_rms_bwd(dh, x, dres, g):
    r = lax.rsqrt(jnp.mean(x * x, axis=-1, keepdims=True) + RMS_EPS)
    xn = x * r
    dn = dh * g
    dx = dres + r * (dn - xn * jnp.mean(dn * xn, axis=-1, keepdims=True))
    return dx, dx, jnp.sum(dh * xn, axis=0, keepdims=True)


def _local_step(xs, target, sp, full, conv_full, dist=None):
    s = xs.shape[0]
    norm_mix, norm_mlp, norm_final = sp["norm_mix"], sp["norm_mlp"], sp["norm_final"]
    gdn_a_log, gdn_dt_bias, gdn_norm_w = sp["gdn_a_log"], sp["gdn_dt_bias"], sp["gdn_norm_w"]
    onehot = _bucket_onehot()
    table_t = sp["rel_bias"].T
    bias = _dswa_bias(table_t, onehot, name="dswa_bias").reshape(DSWA_HEADS, DSWA_HALF, 3 * DSWA_HALF)

    saved = []
    cur = xs
    row = lambda v: v.reshape(1, -1)
    h = _rms_fwd(cur, norm_mix[0], name="rms_mix_fwd0")
    for i in range(DEPTH):
        j = i // 2
        sv = dict(x_in=cur, h=h)
        if i % 2 == 0:
            w_in = full["gdn_w_in"][j]
            proj = _mm_gather(dist, f"gdn_proj{i}", h, w_in, b_cols=(0, GDN_MAIN), name=f"gdn_proj{i}")
            ab = _mm(h, w_in[:, GDN_MAIN:], name=f"gdn_proj_ab{i}")
            qkvn = _gdn_pre_fwd(proj, conv_full[j], name=f"gdn_pre_fwd{i}")
            g_all, beta_all = _gdn_gate_fwd(ab[:, :2 * GDN_HEADS], ab[:, 2 * GDN_HEADS:], gdn_a_log[j], gdn_dt_bias[j],
                                            name=f"gdn_gate_fwd{i}")
            gshape = (2, GDN_HEADS, s // GDN_CHUNK, 1, GDN_CHUNK)
            g_row = g_all.T.reshape(gshape)
            b_row = beta_all.T.reshape(gshape)
            o, states, got = _gdn_chunk_fwd(qkvn, g_row, b_row, name=f"gdn_chunk_fwd{i}",
                                            comm=dist and dist.gather_comm(f"chunk_fwd{i}"))
            if dist:
                dist.put(f"chunk_fwd{i}", got)
            act = _gdn_post_fwd(o, proj, gdn_norm_w[j], name=f"gdn_post_fwd{i}")
            sv.update(proj=proj, ab=ab, qkvn=qkvn, g_row=g_row, b_row=b_row, o=o, states=states, act=act)
            w_out = full["gdn_w_out"][j]
        else:
            w_in = full["dswa_w_in"][j]
            qkv = _mm(h, w_in, name=f"dswa_proj{i}")
            o_n, lse_n, got = _dswa_attn_fwd(qkv, bias, name=f"dswa_attn_fwd{i}",
                                             comm=dist and dist.gather_comm(f"attn_fwd{i}"))
            if dist:
                dist.put(f"attn_fwd{i}", got)
            act = _dswa_combine_fwd(o_n, lse_n, name=f"dswa_comb_fwd{i}")
            sv.update(qkv=qkv, o_n=o_n, lse_n=lse_n, act=act)
            w_out = full["dswa_w_out"][j]
        cur, h2 = _mm(act, w_out, name=f"mix_out{i}", out_dtypes=(F32, BF16), epilogue=_ep_residual_norm,
                      extras=(cur,), vecs=(row(norm_mlp[i]),))
        sv["x_mid"] = cur
        u, a = _mm_gather(dist, f"mlp_up{i}", h2, full["mlp_w1"][i], name=f"mlp_up{i}", out_dtypes=(BF16, BF16),
                          epilogue=lambda acc: (acc, jnp.square(jnp.maximum(acc, 0.0))))
        if i + 1 < DEPTH:
            cur, h = _mm_gather(dist, f"mlp_down{i}", a, full["mlp_w2"][i], name=f"mlp_down{i}", out_dtypes=(F32, BF16),
                                epilogue=_ep_residual_norm, extras=(cur,), vecs=(row(norm_mix[i + 1]),))
        else:
            cur = _mm_gather(dist, f"mlp_down{i}", a, full["mlp_w2"][i], name=f"mlp_down{i}",
                             epilogue=lambda acc, r: (acc + r,), extras=(cur,))
        sv.update(h2=h2, u=u, a=a)
        saved.append(sv)

    loss_part, dcur, dcur_b, dg_final = _loss_head(cur, norm_final, target, name="loss_head")

    g_norm_mix, g_norm_mlp = [None] * DEPTH, [None] * DEPTH
    g_big = {n: [None] * len(full[n]) for n in _BIG}
    g_conv, g_alog, g_dt, g_nw = [None] * 2, [None] * 2, [None] * 2, [None] * 2
    d_table_t = jnp.zeros((DSWA_HEADS, REL_BUCKETS), F32)
    for i in reversed(range(DEPTH)):
        j = i // 2
        sv = saved[i]
        w1, w2 = full["mlp_w1"][i], full["mlp_w2"][i]
        du = _mm(dcur_b, w2, tb=True, name=f"mlp_down_bwd{i}", out_dtypes=(BF16,),
                 epilogue=lambda acc, uu: (acc * (2.0 * jnp.maximum(uu.astype(F32), 0.0)),), extras=(sv["u"],))
        g_big["mlp_w2"][i] = _mm(sv["a"], dcur_b, ta=True, name=f"mlp_w2_grad{i}", out_dtypes=(BF16,), shard="rows")
        g_big["mlp_w1"][i] = _mm(sv["h2"], du, ta=True, name=f"mlp_w1_grad{i}", out_dtypes=(BF16,), shard="cols")
        dmid, dmid_b, g_norm_mlp[i] = _mm(du, w1, tb=True, name=f"mlp_up_bwd{i}", out_dtypes=(F32, BF16), tm=512,
                                          epilogue=_ep_rms_bwd, extras=(sv["x_mid"], dcur), vecs=(row(norm_mlp[i]),),
                                          vec_out=True)
        if i % 2 == 0:
            w_in, w_out = full["gdn_w_in"][j], full["gdn_w_out"][j]
            dact = _mm(dmid_b, w_out, tb=True, name=f"mix_out_bwd{i}")
            g_big["gdn_w_out"][j] = _mm(sv["act"], dmid_b, ta=True, name=f"mix_out_grad{i}", out_dtypes=(BF16,),
                                        shard="rows")
            do, dz, g_nw[j] = _gdn_post_bwd(sv["o"], sv["proj"], gdn_norm_w[j], dact, name=f"gdn_post_bwd{i}")
            dqkvn, dg_row, db_row, got = _gdn_chunk_bwd(sv["qkvn"], sv["g_row"], sv["b_row"], sv["states"], do,
                                                        name=f"gdn_chunk_bwd{i}",
                                                        comm=dist and dist.send_comm(f"chunk_bwd{i}", g_big))
            if dist:
                dist.got(f"chunk_bwd{i}", got)
            dproj, g_conv[j], got = _gdn_pre_bwd(sv["proj"], conv_full[j], dqkvn, dz, name=f"gdn_pre_bwd{i}",
                                                 comm=dist and dist.send_comm(f"pre_bwd{i}", g_big))
            if dist:
                dist.got(f"pre_bwd{i}", got)
            nh2 = 2 * GDN_HEADS
            da_, db_, g_alog[j], g_dt[j] = _gdn_gate_bwd(sv["ab"][:, :nh2], sv["ab"][:, nh2:], gdn_a_log[j], gdn_dt_bias[j],
                                                         dg_row.reshape(nh2, s).T, db_row.reshape(nh2, s).T,
                                                         name=f"gdn_gate_bwd{i}")
            dab = jnp.concatenate([da_, db_], axis=1)
            gw_main = _mm(sv["h"], dproj, ta=True, name=f"gdn_w_in_grad{i}", out_dtypes=(BF16,))
            gw_ab = _mm(sv["h"], dab, ta=True, name=f"gdn_w_ab_grad{i}", out_dtypes=(BF16,))
            g_big["gdn_w_in"][j] = jnp.concatenate([gw_main, gw_ab], axis=1)
            dh_ab = _mm(dab, w_in[:, GDN_MAIN:], tb=True, name=f"gdn_proj_ab_bwd{i}")
            comm = dist and dist.send_comm(f"gdn_proj_bwd{i}", g_big)
            res = _mm(dproj, w_in, b_cols=(0, GDN_MAIN), tb=True, name=f"gdn_proj_bwd{i}", out_dtypes=(F32, BF16), tm=512,
                      epilogue=lambda acc, r, x, dres, g: _ep_rms_bwd(acc + r, x, dres, g),
                      extras=(dh_ab, sv["x_in"], dmid), vecs=(row(norm_mix[i]),), vec_out=True, comm=comm)
            if comm:
                res, got = res
                dist.got(f"gdn_proj_bwd{i}", got)
            dcur, dcur_b, g_norm_mix[i] = res
        else:
            w_in, w_out = full["dswa_w_in"][j], full["dswa_w_out"][j]
            dact = _mm(dmid_b, w_out, tb=True, name=f"mix_out_bwd{i}")
            g_big["dswa_w_out"][j] = _mm(sv["act"], dmid_b, ta=True, name=f"mix_out_grad{i}", out_dtypes=(BF16,),
                                         shard="rows")
            do_n, corr_n = _dswa_combine_bwd(sv["o_n"], sv["lse_n"], dact, name=f"dswa_comb_bwd{i}")
            *dqkv, dbias, got = _dswa_attn_bwd(sv["qkv"], bias, sv["lse_n"], do_n, corr_n, name=f"dswa_attn_bwd{i}",
                                               comm=dist and dist.send_comm(f"attn_bwd{i}", g_big))
            if dist:
                dist.got(f"attn_bwd{i}", got)
            d_table_t = d_table_t + _dswa_dtable(dbias.reshape(DSWA_HEADS, -1), onehot, name=f"dswa_dtable{i}")
            g_big["dswa_w_in"][j] = jnp.concatenate(
                [_mm(sv["h"], dt, ta=True, name=f"dswa_w_in_grad{i}_{t}", out_dtypes=(BF16,)) for t, dt in enumerate(dqkv)],
                axis=1)
            cols = [(t * DSWA_WIDTH, DSWA_WIDTH) for t in range(3)]
            dh = _mm(dqkv[0], w_in, b_cols=cols[0], tb=True, name=f"dswa_proj_bwd{i}_0")
            dh = _mm(dqkv[1], w_in, b_cols=cols[1], tb=True, name=f"dswa_proj_bwd{i}_1",
                     epilogue=lambda acc, r: (acc + r,), extras=(dh,))
            dcur, dcur_b, g_norm_mix[i] = _mm(
                dqkv[2], w_in, b_cols=cols[2], tb=True, name=f"dswa_proj_bwd{i}_2", out_dtypes=(F32, BF16), tm=512,
                epilogue=lambda acc, r, x, dres, g: _ep_rms_bwd(acc + r, x, dres, g),
                extras=(dh, sv["x_in"], dmid), vecs=(row(norm_mix[i]),), vec_out=True)

    rep = dict(norm_mix=jnp.concatenate(g_norm_mix, axis=0), norm_mlp=jnp.concatenate(g_norm_mlp, axis=0),
               norm_final=dg_final.reshape(-1), rel_bias=d_table_t.T,
               gdn_a_log=jnp.stack(g_alog).reshape(gdn_a_log.shape), gdn_dt_bias=jnp.stack(g_dt).reshape(gdn_dt_bias.shape),
               gdn_norm_w=jnp.stack(g_nw).reshape(gdn_norm_w.shape))
    return loss_part, dcur, g_big, rep, g_conv
```

```python
import functools
import math

import jax
import jax.numpy as jnp
import numpy as np
from jax import lax
from jax.experimental import pallas as pl
from jax.experimental.pallas import tpu as pltpu

F32 = jnp.float32
BF16 = jnp.bfloat16
HP = lax.Precision.HIGHEST

N_DEV = 8
D_MODEL = 1024
DEPTH = 4
RMS_EPS = 1e-6
NEG_INF = -1e30

GDN_HEADS = 8
GDN_DK = 128
GDN_CONV = 5
GDN_CHUNK = 128
GDN_QKV = 3 * GDN_HEADS * GDN_DK
GDN_MAIN = GDN_QKV + GDN_HEADS * GDN_DK
GDN_AB = 4 * GDN_HEADS

DSWA_DILS = (1, 4, 16)
DSWA_HG = 6
DSWA_E = 64
DSWA_HEADS = 18
DSWA_WIDTH = DSWA_HEADS * DSWA_E
DSWA_HALF = 64
DSWA_PG = DSWA_HG // 2
DSWA_UNROLL = 8
REL_BUCKETS = 32
REL_MAX_DIST = 1024

ADAM_LR = 0.001
ADAM_B1 = 0.9
ADAM_B2 = 0.999
ADAM_EPS = 1e-08
ADAM_WD = 0.01
ADAM_STEP = 10

VMEM_LIMIT = 56 * 1024 * 1024


def _cparams(sem=None, **kw):
    return pltpu.CompilerParams(dimension_semantics=sem, vmem_limit_bytes=VMEM_LIMIT, **kw)


def _pick(dim, cands):
    for c in cands:
        if dim % c == 0:
            return c
    return dim


def _bdot(a, b):
    return jnp.dot(a.astype(BF16), b.astype(BF16), preferred_element_type=F32)


def _bdot_nt(a, b):
    return lax.dot_general(a.astype(BF16), b.astype(BF16), (((1,), (1,)), ((), ())),
                           preferred_element_type=F32)


def _bdot_tn(a, b):
    return lax.dot_general(a.astype(BF16), b.astype(BF16), (((0,), (0,)), ((), ())),
                           preferred_element_type=F32)


def _hdot(a, b):
    return jnp.dot(a, b, precision=HP, preferred_element_type=F32)


def _hdot_tn(a, b):
    return lax.dot_general(a, b, (((0,), (0,)), ((), ())), precision=HP, preferred_element_type=F32)


def _hdot_nt(a, b):
    return lax.dot_general(a, b, (((1,), (1,)), ((), ())), precision=HP, preferred_element_type=F32)


def _sigmoid(x):
    return 1.0 / (1.0 + jnp.exp(-x))


def _mm(a, b, *, name, ta=False, tb=False, out_dtypes=(F32,), epilogue=None, extras=(), vecs=(), vec_out=False,
        tm=None, tn=None, tk=None, shard=None, comm=None, b_cols=None):
    if ta:
        kdim, m = a.shape
    else:
        m, kdim = a.shape
    b0, bsz = b_cols or (0, b.shape[1])
    n = b.shape[0] if tb else bsz
    assert not tb or kdim == bsz
    if shard == "rows":
        tm = m // N_DEV if (m // N_DEV) % 128 == 0 else m
    if shard == "cols":
        tn = n // N_DEV
    tm = tm or _pick(m, (1024, 1152, 512, 384, 256, 128))
    tn = tn or _pick(n, (1024, 1152, 512, 384, 256, 128))
    tk = tk or _pick(kdim, (1024, 1152, 512, 384, 256, 128))
    nk = kdim // tk
    n_out = len(out_dtypes) + (1 if vec_out else 0)
    n_ex = len(extras) + len(vecs)
    rows_all = shard == "rows" and tm == m

    gi, gj = m // tm, n // tn

    def body(*refs):
        i, j, k = pl.program_id(0), pl.program_id(1), pl.program_id(2)
        inner = (j == 0) & (k == 0)
        ins, out_refs, (acc_ref,) = _comm_hooks(
            comm, refs, 2 + n_ex, n_out, 1, (i == 0) & inner, (i == (3 * gi) // 4) & inner,
            (i == gi - 1) & (j == gj - 1) & (k == nk - 1))
        a_ref, b_ref, ex_refs = ins[0], ins[1], ins[2:]
        if vec_out:
            out_refs, vec_ref = out_refs[:-1], out_refs[-1]

        @pl.when(k == 0)
        def _():
            acc_ref[...] = jnp.zeros_like(acc_ref)

        av = a_ref[...].astype(BF16)
        bv = b_ref[...].astype(BF16)
        dims = (((0 if ta else 1,), (1 if tb else 0,)), ((), ()))
        acc_ref[...] += lax.dot_general(av, bv, dims, preferred_element_type=F32)

        @pl.when(k == nk - 1)
        def _():
            acc = acc_ref[...]
            outs = (acc,) if epilogue is None else epilogue(acc, *[r[...] for r in ex_refs])
            if vec_out:
                part = outs[-1]

                @pl.when(i == 0)
                def _():
                    vec_ref[...] = part

                @pl.when(i > 0)
                def _():
                    vec_ref[...] += part
            for r, o in zip(out_refs, outs):
                if rows_all:
                    for p in range(N_DEV):
                        r[p] = o[p * (m // N_DEV):(p + 1) * (m // N_DEV)].astype(r.dtype)
                else:
                    r[...] = o.astype(r.dtype)

    a_spec = pl.BlockSpec((tk, tm), lambda i, j, k: (k, i)) if ta else pl.BlockSpec((tm, tk), lambda i, j, k: (i, k))
    assert b0 % (tk if tb else tn) == 0
    boff = b0 // (tk if tb else tn)
    b_spec = (pl.BlockSpec((tn, tk), lambda i, j, k: (j, k + boff)) if tb
              else pl.BlockSpec((tk, tn), lambda i, j, k: (k, j + boff)))
    o_spec = pl.BlockSpec((tm, tn), lambda i, j, k: (i, j))
    v_spec = pl.BlockSpec((1, tn), lambda i, j, k: (0, j))
    out_specs = [o_spec] * len(out_dtypes) + ([v_spec] if vec_out else [])
    out_shape = [jax.ShapeDtypeStruct((m, n), dt) for dt in out_dtypes]
    out_shape += [jax.ShapeDtypeStruct((1, n), F32)] if vec_out else []
    if shard == "rows":
        out_shape = [jax.ShapeDtypeStruct((N_DEV, m // N_DEV, n), out_dtypes[0])]
        out_specs = [pl.BlockSpec((N_DEV, m // N_DEV, tn), lambda i, j, k: (0, 0, j)) if rows_all
                     else pl.BlockSpec((None, tm, tn), lambda i, j, k: (i, 0, j))]
    if shard == "cols":
        out_shape = [jax.ShapeDtypeStruct((N_DEV, m, tn), out_dtypes[0])]
        out_specs = [pl.BlockSpec((None, tm, tn), lambda i, j, k: (j, i, 0))]
    c_in, c_out, c_shape, c_scr = _comm_specs(comm)
    outs = pl.pallas_call(
        body, name=name,
        grid=(gi, gj, nk),
        in_specs=[a_spec, b_spec] + [o_spec] * len(extras) + [v_spec] * len(vecs) + c_in,
        out_specs=out_specs + c_out,
        out_shape=out_shape + c_shape,
        scratch_shapes=[pltpu.VMEM((tm, tn), F32)] + c_scr,
        compiler_params=_cparams(("arbitrary",) * 3 if comm or vec_out else ("parallel", "parallel", "arbitrary")),
    )(a, b, *extras, *vecs, *(comm.arrays if comm else []))
    res = outs[0] if n_out == 1 else tuple(outs[:n_out])
    return (res, outs[n_out:]) if comm else res


def _rms_fwd(x, g, *, name):
    s, d = x.shape
    tr = _pick(s, (512, 256, 128))

    def body(x_ref, g_ref, h_ref):
        xv = x_ref[...]
        r = lax.rsqrt(jnp.mean(xv * xv, axis=-1, keepdims=True) + RMS_EPS)
        h_ref[...] = (xv * r * g_ref[...]).astype(h_ref.dtype)

    return pl.pallas_call(
        body, name=name, grid=(s // tr,),
        in_specs=[pl.BlockSpec((tr, d), lambda i: (i, 0)), pl.BlockSpec((1, d), lambda i: (0, 0))],
        out_specs=pl.BlockSpec((tr, d), lambda i: (i, 0)),
        out_shape=jax.ShapeDtypeStruct((s, d), BF16),
        compiler_params=_cparams(("parallel",)),
    )(x, g.reshape(1, d))


def _rms_bwd(x, g, dh, dres, *, name):
    s, d = x.shape
    tr = _pick(s, (512, 256, 128))

    def body(x_ref, g_ref, dh_ref, dres_ref, dx_ref, dxb_ref, dg_ref):
        i = pl.program_id(0)
        xv = x_ref[...]
        r = lax.rsqrt(jnp.mean(xv * xv, axis=-1, keepdims=True) + RMS_EPS)
        xn = xv * r
        dhv = dh_ref[...]
        dn = dhv * g_ref[...]
        dx = dres_ref[...] + r * (dn - xn * jnp.mean(dn * xn, axis=-1, keepdims=True))
        dx_ref[...] = dx
        dxb_ref[...] = dx.astype(dxb_ref.dtype)
        part = jnp.sum(dhv * xn, axis=0, keepdims=True)

        @pl.when(i == 0)
        def _():
            dg_ref[...] = part

        @pl.when(i > 0)
        def _():
            dg_ref[...] += part

    row = pl.BlockSpec((tr, d), lambda i: (i, 0))
    vec = pl.BlockSpec((1, d), lambda i: (0, 0))
    return pl.pallas_call(
        body, name=name, grid=(s // tr,),
        in_specs=[row, vec, row, row], out_specs=[row, row, vec],
        out_shape=[jax.ShapeDtypeStruct((s, d), F32), jax.ShapeDtypeStruct((s, d), BF16),
                   jax.ShapeDtypeStruct((1, d), F32)],
        compiler_params=_cparams(("arbitrary",)),
    )(x, g.reshape(1, d), dh, dres)


def _loss_head(x, g, target, *, name):
    s, d = x.shape
    tr = _pick(s, (512, 256, 128))

    def body(x_ref, g_ref, t_ref, loss_ref, dx_ref, dxb_ref, dg_ref):
        i = pl.program_id(0)
        xv = x_ref[...]
        gv = g_ref[...]
        r = lax.rsqrt(jnp.mean(xv * xv, axis=-1, keepdims=True) + RMS_EPS)
        xn = xv * r
        err = xn * gv - t_ref[...]
        lpart = 0.5 * jnp.sum(jnp.mean(err * err, axis=-1, keepdims=True), axis=0, keepdims=True)
        dy = err * (1.0 / d)
        dn = dy * gv
        dx = r * (dn - xn * jnp.mean(dn * xn, axis=-1, keepdims=True))
        dx_ref[...] = dx
        dxb_ref[...] = dx.astype(dxb_ref.dtype)
        gpart = jnp.sum(dy * xn, axis=0, keepdims=True)

        @pl.when(i == 0)
        def _():
            dg_ref[...] = gpart
            loss_ref[...] = lpart

        @pl.when(i > 0)
        def _():
            dg_ref[...] += gpart
            loss_ref[...] += lpart

    row = pl.BlockSpec((tr, d), lambda i: (i, 0))
    vec = pl.BlockSpec((1, d), lambda i: (0, 0))
    one = pl.BlockSpec((1, 1), lambda i: (0, 0))
    return pl.pallas_call(
        body, name=name, grid=(s // tr,),
        in_specs=[row, vec, row], out_specs=[one, row, row, vec],
        out_shape=[jax.ShapeDtypeStruct((1, 1), F32), jax.ShapeDtypeStruct((s, d), F32),
                   jax.ShapeDtypeStruct((s, d), BF16), jax.ShapeDtypeStruct((1, d), F32)],
        compiler_params=_cparams(("arbitrary",)),
    )(x, g.reshape(1, d), target)


def _shift_rows(x, sft, rows):
    s = x.shape[0]
    if sft == 0:
        return x
    y = pltpu.roll(x, (-sft) % s, 0)
    edge = slice(0, 8) if sft < 0 else slice(s - 8, s)
    ok = (rows[edge] + sft >= 0) & (rows[edge] + sft < s)
    fixed = jnp.where(ok, y[edge], 0.0)
    return jnp.concatenate([fixed, y[8:]] if sft < 0 else [y[:s - 8], fixed], axis=0)


def _gdn_pre_fwd(proj, conv_w, *, name):
    s = proj.shape[0]
    nblk = GDN_QKV // 128
    pad = GDN_CONV // 2

    def body(x_ref, w_ref, o_ref):
        j = pl.program_id(0)
        x = x_ref[...]
        rows = lax.broadcasted_iota(jnp.int32, x.shape, 0)
        c = jnp.zeros_like(x)
        for t in range(GDN_CONV):
            c = c + w_ref[pl.ds(t, 1), :] * _shift_rows(x, t - pad, rows)
        a = c * _sigmoid(c)
        rinv = lax.rsqrt(jnp.sum(a * a, axis=-1, keepdims=True) + 1e-6)
        scale = jnp.where(j < GDN_HEADS, GDN_DK ** -0.5, 1.0)
        o_ref[...] = jnp.where(j >= 2 * GDN_HEADS, a, a * (rinv * scale))

    return pl.pallas_call(
        body, name=name, grid=(nblk,),
        in_specs=[pl.BlockSpec((s, 128), lambda j: (0, j)), pl.BlockSpec((GDN_CONV, 128), lambda j: (0, j))],
        out_specs=pl.BlockSpec((s, 128), lambda j: (0, j)),
        out_shape=jax.ShapeDtypeStruct((s, GDN_QKV), F32),
        compiler_params=_cparams(("parallel",)),
    )(proj, conv_w)


def _gdn_pre_bwd(proj, conv_w, dqkv, dproj, *, name, comm=None):
    s = proj.shape[0]
    nblk = GDN_QKV // 128
    pad = GDN_CONV // 2

    def body(*refs):
        j = pl.program_id(0)
        (x_ref, w_ref, df_ref, dbk_ref, _), (dx_ref, dw_ref), _ = _comm_hooks(
            comm, refs, 5, 2, 0, j == 0, j == nblk // 2, j == nblk - 1)
        x = x_ref[...]
        rows = lax.broadcasted_iota(jnp.int32, x.shape, 0)
        xs = [_shift_rows(x, t - pad, rows) for t in range(GDN_CONV)]
        c = jnp.zeros_like(x)
        for t in range(GDN_CONV):
            c = c + w_ref[pl.ds(t, 1), :] * xs[t]
        sg = _sigmoid(c)
        a = c * sg
        rinv = lax.rsqrt(jnp.sum(a * a, axis=-1, keepdims=True) + 1e-6)
        scale = jnp.where(j < GDN_HEADS, GDN_DK ** -0.5, 1.0)
        dy = df_ref[...] + dbk_ref[...]
        nh = a * rinv
        da_n = (rinv * scale) * (dy - nh * jnp.sum(dy * nh, axis=-1, keepdims=True))
        da = jnp.where(j >= 2 * GDN_HEADS, dy, da_n)
        dc = da * (sg * (1.0 + c * (1.0 - sg)))
        dx = jnp.zeros_like(x)
        for t in range(GDN_CONV):
            dx = dx + w_ref[pl.ds(t, 1), :] * _shift_rows(dc, pad - t, rows)
            dw_ref[pl.ds(t, 1), :] = jnp.sum(dc * xs[t], axis=0, keepdims=True)
        dx_ref[...] = dx.astype(dx_ref.dtype)

    col = pl.BlockSpec((s, 128), lambda j: (0, j))
    wsp = pl.BlockSpec((GDN_CONV, 128), lambda j: (0, j))
    c_in, c_out, c_shape, c_scr = _comm_specs(comm)
    res = pl.pallas_call(
        body, name=name, grid=(nblk,),
        in_specs=[col, wsp, col, col, pl.BlockSpec(memory_space=pl.ANY)] + c_in, out_specs=[col, wsp] + c_out,
        out_shape=[jax.ShapeDtypeStruct(dproj.shape, BF16), jax.ShapeDtypeStruct((GDN_CONV, GDN_QKV), F32)] + c_shape,
        input_output_aliases={4: 0},
        scratch_shapes=c_scr,
        compiler_params=_cparams(("arbitrary",) if comm else ("parallel",)),
    )(proj, conv_w, dqkv[0], dqkv[1], dproj, *(comm.arrays if comm else []))
    return res[0], res[1], res[2:]


def _softplus(x):
    return jnp.maximum(x, 0.0) + jnp.log(1.0 + jnp.exp(-jnp.abs(x)))


def _gdn_gate_fwd(a, b, a_log, dt_bias, *, name):
    s = a.shape[0]
    nh = 2 * GDN_HEADS

    def body(a_ref, b_ref, al_ref, dt_ref, g_ref, be_ref):
        g_ref[...] = -jnp.exp(al_ref[...]) * _softplus(a_ref[...] + dt_ref[...])
        be_ref[...] = _sigmoid(b_ref[...])

    return pl.pallas_call(
        body, name=name,
        out_shape=[jax.ShapeDtypeStruct((s, nh), F32), jax.ShapeDtypeStruct((s, nh), F32)],
        compiler_params=_cparams(),
    )(a, b, a_log.reshape(1, nh), dt_bias.reshape(1, nh))


def _gdn_gate_bwd(a, b, a_log, dt_bias, dg, dbeta, *, name):
    s = a.shape[0]
    nh = 2 * GDN_HEADS

    def body(a_ref, b_ref, al_ref, dt_ref, dg_ref, db_ref, da_ref, dbb_ref, dal_ref, ddt_ref):
        ea = jnp.exp(al_ref[...])
        z = a_ref[...] + dt_ref[...]
        dgv = dg_ref[...]
        dz = dgv * (-ea) * _sigmoid(z)
        dal_ref[...] = jnp.sum(dgv * (-ea) * _softplus(z), axis=0, keepdims=True)
        ddt_ref[...] = jnp.sum(dz, axis=0, keepdims=True)
        sb = _sigmoid(b_ref[...])
        da_ref[...] = dz
        dbb_ref[...] = db_ref[...] * sb * (1.0 - sb)

    return pl.pallas_call(
        body, name=name,
        out_shape=[jax.ShapeDtypeStruct((s, nh), F32), jax.ShapeDtypeStruct((s, nh), F32),
                   jax.ShapeDtypeStruct((1, nh), F32), jax.ShapeDtypeStruct((1, nh), F32)],
        compiler_params=_cparams(),
    )(a, b, a_log.reshape(1, nh), dt_bias.reshape(1, nh), dg, dbeta)


def _chunk_masks(d):
    c = GDN_CHUNK
    ii = lax.broadcasted_iota(jnp.int32, (c, c), 0)
    jj = lax.broadcasted_iota(jnp.int32, (c, c), 1)
    dif = (ii - jj) * (1 - 2 * d)
    mi = dif >= 0
    mit = dif <= 0
    ms = dif > 0
    eye = ii == jj
    bds = [(ii >> sh) == (jj >> sh) for sh in range(3, c.bit_length() - 1)]
    return dict(mi=mi, mit=mit, ms=ms, eye=eye, bds=bds,
                mif=mi.astype(F32), mitf=mit.astype(F32), eyef=eye.astype(F32))


class _V:
    def __init__(self, xs):
        self.xs = tuple(xs)

    def __add__(self, o):
        return _lift(lambda a, b: a + b)(self, o)

    def __radd__(self, o):
        return _lift(lambda a, b: b + a)(self, o)

    def __sub__(self, o):
        return _lift(lambda a, b: a - b)(self, o)

    def __rsub__(self, o):
        return _lift(lambda a, b: b - a)(self, o)

    def __mul__(self, o):
        return _lift(lambda a, b: a * b)(self, o)

    def __rmul__(self, o):
        return _lift(lambda a, b: b * a)(self, o)

    def __and__(self, o):
        return _lift(lambda a, b: a & b)(self, o)

    def __neg__(self):
        return _lift(lambda a: -a)(self)

    def __rtruediv__(self, o):
        return _lift(lambda a, b: b / a)(self, o)


def _lift(f):
    def g(*args, **kw):
        n = next(len(a.xs) for a in args if isinstance(a, _V))
        return _V(f(*[a.xs[i] if isinstance(a, _V) else a for a in args], **kw) for i in range(n))
    return g


_vwhere, _vsum, _vexp, _vnot = _lift(jnp.where), _lift(jnp.sum), _lift(jnp.exp), _lift(jnp.logical_not)
_vhdot, _vhdot_tn = _lift(_bdot), _lift(_bdot_tn)
_vbdot, _vbdot_nt, _vbdot_tn = _lift(_bdot), _lift(_bdot_nt), _lift(_bdot_tn)
_vcat = _lift(lambda a, b: jnp.concatenate([a, b], axis=1))
_vlo = _lift(lambda a, n: a[:, :n])
_vhi = _lift(lambda a, n: a[:, n:])


def _both_masks(n):
    m = [_chunk_masks(d) for d in range(2)]
    mk = {key: _V([m[0][key]] * n + [m[1][key]] * n) for key in m[0] if key != "bds"}
    mk["bds"] = [_V([m[0]["bds"][i]] * n + [m[1]["bds"][i]] * n) for i in range(len(m[0]["bds"]))]
    return mk


def _tri_inv(a, mk):
    eyef = mk["eyef"]
    bds = mk["bds"]
    a8 = _vwhere(bds[0], a, 0.0)
    a2 = _vhdot(a8, a8)
    a4 = _vhdot(a2, a2)
    t = _vhdot(_vhdot(eyef - a8, eyef + a2), eyef + a4)
    for inner, outer in zip(bds, bds[1:] + [None]):
        off = _vnot(inner) if outer is None else (outer & _vnot(inner))
        low = _vwhere(off, a, 0.0)
        t = t - _vhdot(_vhdot(t, low), t)
    return t


def _chunk_prep(q, k, v, g_row, b_row, mk):
    dv = GDN_DK
    g_col = _vsum(mk["eyef"] * g_row, axis=1, keepdims=True)
    b_col = _vsum(mk["eyef"] * b_row, axis=1, keepdims=True)
    gc_col = _vsum(mk["mif"] * g_row, axis=1, keepdims=True)
    gc_row = _vsum(mk["mitf"] * g_col, axis=0, keepdims=True)
    gl = _vsum(g_row, axis=1, keepdims=True)
    decay = _vwhere(mk["mi"], _vexp(_vwhere(mk["mi"], gc_col - gc_row, 0.0)), 0.0)
    eg = _vexp(gc_col)
    e2 = _vexp(gl - gc_col)
    egl = _vexp(gl)
    kb = k * b_col
    pm = _vbdot_nt(kb, k)
    a = _vwhere(mk["ms"], pm * decay, 0.0)
    t = _tri_inv(a, mk)
    sol = _vhdot(t, _vcat(v * b_col, kb * eg))
    u, w = _vlo(sol, dv), _vhi(sol, dv)
    qm = _vbdot_nt(q, k)
    return dict(b_col=b_col, decay=decay, eg=eg, e2=e2, egl=egl, kb=kb, pm=pm, t=t, u=u, w=w,
                qm=qm, intra=qm * decay, qd=q * eg, kd=k * e2)


def _chunk_fwd_step(p, state):
    v_new = p["u"] - _vbdot(p["w"], state)
    o = _vbdot(p["qd"], state) + _vbdot(p["intra"], v_new)
    new_state = state * p["egl"] + _vbdot_tn(p["kd"], v_new)
    return o, new_state


def _chunk_bwd_step(q, k, v, p, mk, state, dso, do):
    dv_dim = GDN_DK
    v_new = p["u"] - _vbdot(p["w"], state)
    dvn = _vbdot_tn(p["intra"], do) + _vbdot(p["kd"], dso)
    dintra = _vbdot_nt(do, v_new)
    dqd = _vbdot_nt(do, state)
    ds = p["egl"] * dso + _vbdot_tn(p["qd"], do) - _vbdot_tn(p["w"], dvn)
    dkd = _vbdot_nt(v_new, dso)
    dgl = _vsum(_vsum(dso * state, axis=1, keepdims=True), axis=0, keepdims=True) * p["egl"]
    dw = -_vbdot_nt(dvn, state)
    drhs = _vhdot_tn(p["t"], _vcat(dvn, dw))
    dru, drw = _vlo(drhs, dv_dim), _vhi(drhs, dv_dim)
    da = -_vwhere(mk["ms"], _vbdot_nt(drhs, _vcat(p["u"], p["w"])), 0.0)
    b_col = p["b_col"]
    dv = dru * b_col
    dbeta = _vsum(dru * v, axis=1, keepdims=True)
    dkb = drw * p["eg"]
    deg = _vsum(drw * p["kb"], axis=1, keepdims=True)
    dp = da * p["decay"]
    ddecay = da * p["pm"]
    dkb = dkb + _vbdot(dp, k)
    dk = _vbdot_tn(dp, p["kb"])
    dqm = dintra * p["decay"]
    ddecay = ddecay + dintra * p["qm"]
    dq = _vbdot(dqm, k)
    dk = dk + _vbdot_tn(dqm, q)
    dd = ddecay * p["decay"]
    dgc_col = _vsum(dd, axis=1, keepdims=True)
    dgc_row = -_vsum(dd, axis=0, keepdims=True)
    dq = dq + dqd * p["eg"]
    deg = deg + _vsum(dqd * q, axis=1, keepdims=True)
    dk = dk + dkd * p["e2"]
    de2 = _vsum(dkd * k, axis=1, keepdims=True) * p["e2"]
    dgl = dgl + _vsum(de2, axis=0, keepdims=True)
    dgc_col = dgc_col - de2 + deg * p["eg"]
    dk = dk + dkb * b_col
    dbeta = dbeta + _vsum(dkb * k, axis=1, keepdims=True)
    dgc_col = dgc_col + _vsum(mk["eyef"] * dgc_row, axis=1, keepdims=True)
    dg_row = _vsum(mk["mif"] * dgc_col, axis=0, keepdims=True) + dgl
    dbeta_row = _vsum(mk["eyef"] * dbeta, axis=0, keepdims=True)
    return dq, dk, dv, dg_row, dbeta_row, ds


def _gdn_chunk_fwd(qkvn, g5, b5, *, name, comm=None):
    s = qkvn.shape[0]
    c = GDN_CHUNK
    nc = s // c
    h_, dk = GDN_HEADS, GDN_DK

    def body(*refs):
        n = pl.program_id(0)
        ins, outs, (st_scr,) = _comm_hooks(comm, refs, 6, 4, 1, n == 0, n == (3 * nc) // 4, n == nc - 1)
        x_refs, g_refs, b_refs = ins[0:2], ins[2:4], ins[4:6]
        o_refs, st_refs = outs[0:2], outs[2:4]

        @pl.when(n == 0)
        def _():
            st_scr[...] = jnp.zeros_like(st_scr)

        ch = [(d, h) for d in range(2) for h in range(h_)]
        mk = _both_masks(h_)
        q, k, v = (_V(x_refs[d][:, (t * h_ + h) * dk:(t * h_ + h + 1) * dk] for d, h in ch) for t in range(3))
        g, b = (_V(r[d][0, h, 0] for d, h in ch) for r in (g_refs, b_refs))
        state = _V(st_scr[d * h_ + h] for d, h in ch)
        o, new_state = _chunk_fwd_step(_chunk_prep(q, k, v, g, b, mk), state)
        for i, (d, h) in enumerate(ch):
            st_refs[d][h, 0] = state.xs[i]
            st_scr[d * h_ + h] = new_state.xs[i]
            o_refs[d][:, h * dk:(h + 1) * dk] = o.xs[i]

    ce = (lambda n: n, lambda n: nc - 1 - n)
    xs = [pl.BlockSpec((c, 3 * h_ * dk), lambda n, d=d: (ce[d](n), 0)) for d in range(2)]
    gates = [pl.BlockSpec((1, h_, 1, 1, c), lambda n, d=d: (d, 0, ce[d](n), 0, 0)) for d in range(2)]
    os_ = [pl.BlockSpec((c, h_ * dk), lambda n, d=d: (ce[d](n), 0)) for d in range(2)]
    sts = [pl.BlockSpec((h_, 1, dk, dk), lambda n, d=d: (0, ce[d](n), 0, 0)) for d in range(2)]
    c_in, c_out, c_shape, c_scr = _comm_specs(comm)
    res = pl.pallas_call(
        body, name=name, grid=(nc,),
        in_specs=xs + gates + gates + c_in,
        out_specs=os_ + sts + c_out,
        out_shape=[jax.ShapeDtypeStruct((s, h_ * dk), F32)] * 2 + [jax.ShapeDtypeStruct((h_, nc, dk, dk), F32)] * 2 + c_shape,
        scratch_shapes=[pltpu.VMEM((2 * h_, dk, dk), F32)] + c_scr,
        compiler_params=_cparams(("arbitrary",)),
    )(qkvn, qkvn, g5, g5, b5, b5, *(comm.arrays if comm else []))
    return res[0:2], res[2:4], res[4:]


def _gdn_chunk_bwd(qkvn, g5, b5, states, do, *, name, comm=None):
    s = qkvn.shape[0]
    c = GDN_CHUNK
    nc = s // c
    h_, dk = GDN_HEADS, GDN_DK

    def body(*refs):
        i = pl.program_id(0)
        ins, outs, (ds_scr,) = _comm_hooks(comm, refs, 10, 6, 1, i == 0, i == nc // 2, i == nc - 1)
        x_refs, g_refs, b_refs, st_refs, do_refs = ins[0:2], ins[2:4], ins[4:6], ins[6:8], ins[8:10]
        dx_refs, dg_refs, db_refs = outs[0:2], outs[2:4], outs[4:6]

        @pl.when(i == 0)
        def _():
            ds_scr[...] = jnp.zeros_like(ds_scr)

        ch = [(d, h) for d in range(2) for h in range(h_)]
        mk = _both_masks(h_)
        q, k, v = (_V(x_refs[d][:, (t * h_ + h) * dk:(t * h_ + h + 1) * dk] for d, h in ch) for t in range(3))
        g, b = (_V(r[d][0, h, 0] for d, h in ch) for r in (g_refs, b_refs))
        state = _V(st_refs[d][h, 0] for d, h in ch)
        dso = _V(ds_scr[d * h_ + h] for d, h in ch)
        dov = _V(do_refs[d][:, h * dk:(h + 1) * dk] for d, h in ch)
        res = _chunk_bwd_step(q, k, v, _chunk_prep(q, k, v, g, b, mk), mk, state, dso, dov)
        for (d, h), (dq, dkk, dvv, dg_r, db_r, ds) in zip(ch, zip(*[r.xs for r in res])):
            ds_scr[d * h_ + h] = ds
            dg_refs[d][h, 0] = dg_r
            db_refs[d][h, 0] = db_r
            for t, val in enumerate((dq, dkk, dvv)):
                dx_refs[d][:, (t * h_ + h) * dk:(t * h_ + h + 1) * dk] = val

    ce = (lambda i: nc - 1 - i, lambda i: i)
    both = lambda mk_spec: [mk_spec(d) for d in range(2)]
    xs = both(lambda d: pl.BlockSpec((c, 3 * h_ * dk), lambda i: (ce[d](i), 0)))
    gates = both(lambda d: pl.BlockSpec((1, h_, 1, 1, c), lambda i: (d, 0, ce[d](i), 0, 0)))
    sts = both(lambda d: pl.BlockSpec((h_, 1, dk, dk), lambda i: (0, ce[d](i), 0, 0)))
    dos = both(lambda d: pl.BlockSpec((c, h_ * dk), lambda i: (ce[d](i), 0)))
    gouts = both(lambda d: pl.BlockSpec((h_, 1, 1, c), lambda i: (0, ce[d](i), 0, 0)))
    c_in, c_out, c_shape, c_scr = _comm_specs(comm)
    res = pl.pallas_call(
        body, name=name, grid=(nc,),
        in_specs=xs + gates + gates + sts + dos + c_in,
        out_specs=xs + gouts + gouts + c_out,
        out_shape=[jax.ShapeDtypeStruct((s, 3 * h_ * dk), F32)] * 2
        + [jax.ShapeDtypeStruct((h_, nc, 1, c), F32)] * 4 + c_shape,
        scratch_shapes=[pltpu.VMEM((2 * h_, dk, dk), F32)] + c_scr,
        compiler_params=_cparams(("arbitrary",)),
    )(qkvn, qkvn, g5, g5, b5, b5, states[0], states[1], do, do, *(comm.arrays if comm else []))
    return res[0:2], jnp.stack(res[2:4]), jnp.stack(res[4:6]), res[6:]


def _gdn_post_fwd(o, z, norm_w, *, name):
    s = o[0].shape[0]
    h_, dk = GDN_HEADS, GDN_DK

    def body(of_ref, ob_ref, z_ref, w_ref, a_ref):
        ov = of_ref[...] + ob_ref[...]
        zv = z_ref[...]
        r = lax.rsqrt(jnp.mean(ov * ov, axis=-1, keepdims=True) + RMS_EPS)
        a_ref[...] = (ov * r * w_ref[...] * (zv * _sigmoid(zv))).astype(a_ref.dtype)

    col = pl.BlockSpec((s, dk), lambda h: (0, h))
    return pl.pallas_call(
        body, name=name, grid=(h_,),
        in_specs=[col, col, pl.BlockSpec((s, dk), lambda h: (0, 3 * h_ + h)), pl.BlockSpec((1, dk), lambda h: (0, 0))],
        out_specs=col,
        out_shape=jax.ShapeDtypeStruct((s, h_ * dk), BF16),
        compiler_params=_cparams(("parallel",)),
    )(o[0], o[1], z, norm_w.reshape(1, dk))


def _gdn_post_bwd(o, z, norm_w, dact, *, name):
    s = o[0].shape[0]
    h_, dk = GDN_HEADS, GDN_DK

    def body(of_ref, ob_ref, z_ref, w_ref, da_ref, do_ref, dz_ref, dw_ref):
        h = pl.program_id(0)
        ov = of_ref[...] + ob_ref[...]
        zv = z_ref[...]
        wv = w_ref[...]
        dav = da_ref[...]
        r = lax.rsqrt(jnp.mean(ov * ov, axis=-1, keepdims=True) + RMS_EPS)
        nrm = ov * r
        sg = _sigmoid(zv)
        sz = zv * sg
        dn = dav * wv * sz
        do_ref[...] = r * (dn - nrm * jnp.mean(dn * nrm, axis=-1, keepdims=True))
        dz_ref[...] = (dav * nrm * wv * (sg * (1.0 + zv * (1.0 - sg)))).astype(dz_ref.dtype)
        part = jnp.sum(dav * nrm * sz, axis=0, keepdims=True)

        @pl.when(h == 0)
        def _():
            dw_ref[...] = part

        @pl.when(h > 0)
        def _():
            dw_ref[...] += part

    col = pl.BlockSpec((s, dk), lambda h: (0, h))
    vec = pl.BlockSpec((1, dk), lambda h: (0, 0))
    return pl.pallas_call(
        body, name=name, grid=(h_,),
        in_specs=[col, col, pl.BlockSpec((s, dk), lambda h: (0, 3 * h_ + h)), vec, col],
        out_specs=[col, pl.BlockSpec((s, dk), lambda h: (0, 3 * h_ + h)), vec],
        out_shape=[jax.ShapeDtypeStruct((s, h_ * dk), F32), jax.ShapeDtypeStruct((s, GDN_MAIN), BF16),
                   jax.ShapeDtypeStruct((1, dk), F32)],
        compiler_params=_cparams(("arbitrary",)),
    )(o[0], o[1], z, norm_w.reshape(1, dk), dact)


def _rel_bucket(rel):
    nb = REL_BUCKETS // 2
    max_exact = nb // 2
    ret = jnp.where(rel > 0, nb, 0)
    n = jnp.abs(rel)
    nf = jnp.maximum(n, 1).astype(F32)
    large = max_exact + (jnp.log(nf / max_exact) / math.log(REL_MAX_DIST / max_exact)
                         * (nb - max_exact)).astype(jnp.int32)
    large = jnp.minimum(large, nb - 1)
    return ret + jnp.where(n < max_exact, n, large)


def _bucket_onehot():
    half = DSWA_HALF
    outs = []
    for dil in DSWA_DILS:
        rel = (jnp.arange(3 * half)[None, :] - half - jnp.arange(half)[:, None]) * dil
        outs.append(jax.nn.one_hot(_rel_bucket(rel).reshape(-1), REL_BUCKETS, dtype=F32, axis=0))
    return jnp.stack(outs)


def _head_group_select(vals):
    rows = lax.broadcasted_iota(jnp.int32, vals[0].shape, 0)
    return jnp.where(rows < DSWA_HG, vals[0], jnp.where(rows < 2 * DSWA_HG, vals[1], vals[2]))


def _dswa_bias(table_t, onehot, *, name):
    p = onehot.shape[-1]

    def body(t_ref, oh_ref, b_ref):
        b_ref[...] = _head_group_select([_hdot(t_ref[...], oh_ref[g]) for g in range(3)])

    return pl.pallas_call(body, name=name, out_shape=jax.ShapeDtypeStruct((DSWA_HEADS, p), F32),
                          compiler_params=_cparams())(table_t, onehot)


def _dswa_dtable(dbias, onehot, *, name):
    def body(d_ref, oh_ref, t_ref):
        t_ref[...] = _head_group_select([_hdot_nt(d_ref[...], oh_ref[g]) for g in range(3)])

    return pl.pallas_call(body, name=name, out_shape=jax.ShapeDtypeStruct((DSWA_HEADS, REL_BUCKETS), F32),
                          compiler_params=_cparams())(dbias, onehot)


def _rows(start, dil):
    if dil == 1:
        return pl.ds(pl.multiple_of(start, DSWA_HALF), DSWA_HALF)
    return pl.ds(start, DSWA_HALF, stride=dil)


def _attn_blocks(it, s, dil):
    half = DSWA_HALF
    nbs = s // half // dil
    ii = lax.broadcasted_iota(jnp.int32, (half, 3 * half), 0)
    jj = lax.broadcasted_iota(jnp.int32, (half, 3 * half), 1)
    band = jnp.abs(jj - half - ii) <= half
    out = []
    for u in range(DSWA_UNROLL):
        blk = it * DSWA_UNROLL + u
        r, b = blk // nbs, blk % nbs
        own = r + dil * half * b
        prev = own - jnp.where(b > 0, dil * half, 0)
        nxt = own + jnp.where(b < nbs - 1, dil * half, 0)
        ok = band & ((jj >= half) | (b > 0)) & ((jj < 2 * half) | (b < nbs - 1))
        out.append(((prev, own, nxt), ok))
    return out


def _attn_chains(q_ref, k_ref, v_ref, blocks, dil):
    lane = lax.broadcasted_iota(jnp.int32, (DSWA_HALF, 2 * DSWA_E), 1)
    qm, kw, vw, valid, hmask = [], [], [], [], []
    for (prev, own, nxt), ok in blocks:
        q = q_ref[_rows(own, dil), :].astype(BF16)
        k = jnp.concatenate([k_ref[_rows(st, dil), :] for st in (prev, own, nxt)], axis=0).astype(BF16)
        v = jnp.concatenate([v_ref[_rows(st, dil), :] for st in (prev, own, nxt)], axis=0).astype(BF16)
        for hd in range(2):
            mine = (lane < DSWA_E) if hd == 0 else (lane >= DSWA_E)
            qm.append(jnp.where(mine, q, jnp.zeros_like(q)))
            kw.append(k)
            vw.append(v)
            valid.append(ok)
            hmask.append(mine)
    return _V(qm), _V(kw), _V(vw), _V(valid), _V(hmask)


def _per_group(pr, fn):
    for gi, dil in enumerate(DSWA_DILS):
        pl.when(pr // DSWA_PG == gi)(functools.partial(fn, dil))


_vmax, _vlog = _lift(jnp.max), _lift(jnp.log)


def _dswa_attn_fwd(qkv, bias, *, name, comm=None):
    s = qkv.shape[0]
    half, e = DSWA_HALF, DSWA_E
    npair = DSWA_HEADS // 2

    def body(*refs):
        pr = pl.program_id(0)
        (q_ref, k_ref, v_ref, bias_ref), (o_ref, lse_ref), _ = _comm_hooks(
            comm, refs, 4, 2, 0, pr == 0, pr == (3 * npair) // 4, pr == npair - 1)
        bias_v = _V([bias_ref[0], bias_ref[1]] * DSWA_UNROLL)

        def run(dil):
            def step(it, carry):
                blocks = _attn_blocks(it, s, dil)
                qm, kw, vw, valid, hmask = _attn_chains(q_ref, k_ref, v_ref, blocks, dil)
                sc = _vwhere(valid, _vbdot_nt(qm, kw) * (e ** -0.5) + bias_v, NEG_INF)
                m = _vmax(sc, axis=-1, keepdims=True)
                p = _vexp(sc - m)
                l = _vsum(p, axis=-1, keepdims=True)
                o = _vbdot(p * (1.0 / l), vw)
                lse = m + _vlog(l)
                for u, ((_, own, _), _) in enumerate(blocks):
                    is_a = hmask.xs[2 * u]
                    o_ref[_rows(own, dil), :] = jnp.where(is_a, o.xs[2 * u], o.xs[2 * u + 1])
                    lse_ref[_rows(own, dil), :] = jnp.where(is_a, lse.xs[2 * u], lse.xs[2 * u + 1])
                return carry

            lax.fori_loop(0, s // half // DSWA_UNROLL, step, 0)

        _per_group(pr, run)

    col = lambda t: pl.BlockSpec((s, 2 * e), lambda p: (0, t * npair + p))
    pair = pl.BlockSpec((s, 2 * e), lambda p: (0, p))
    c_in, c_out, c_shape, c_scr = _comm_specs(comm)
    res = pl.pallas_call(
        body, name=name, grid=(npair,),
        in_specs=[col(0), col(1), col(2), pl.BlockSpec((2, half, 3 * half), lambda p: (p, 0, 0))] + c_in,
        out_specs=[pair, pair] + c_out,
        out_shape=[jax.ShapeDtypeStruct((s, npair * 2 * e), F32)] * 2 + c_shape,
        scratch_shapes=c_scr,
        compiler_params=_cparams(("arbitrary",)),
    )(qkv, qkv, qkv, bias, *(comm.arrays if comm else []))
    return res[0], res[1], res[2:]


def _dswa_attn_bwd(qkv, bias, lse, do, corr, *, name, comm=None):
    s = qkv.shape[0]
    half, e = DSWA_HALF, DSWA_E
    npair = DSWA_HEADS // 2
    w = 2 * e

    def body(*refs):
        pr = pl.program_id(0)
        (q_ref, k_ref, v_ref, bias_ref, lse_ref, do_ref, corr_ref), (dq_ref, dk_ref, dv_ref, db_ref), _ = _comm_hooks(
            comm, refs, 7, 4, 0, pr == 0, pr == npair // 2, pr == npair - 1)
        bias_v = _V([bias_ref[0], bias_ref[1]] * DSWA_UNROLL)
        dk_ref[...] = jnp.zeros_like(dk_ref)
        dv_ref[...] = jnp.zeros_like(dv_ref)

        def run(dil):
            def step(it, dbias):
                blocks = _attn_blocks(it, s, dil)
                qm, kw, vw, valid, hmask = _attn_chains(q_ref, k_ref, v_ref, blocks, dil)
                hd = [0, 1] * DSWA_UNROLL
                rows = [_rows(own, dil) for (_, own, _), _ in blocks for _ in range(2)]
                lse_c = _V(lse_ref[rw, :][:, h * e:h * e + 1] for rw, h in zip(rows, hd))
                corr_c = _V(corr_ref[rw, :][:, h * e:h * e + 1] for rw, h in zip(rows, hd))
                dov = _vwhere(hmask, _V(do_ref[rw, :] for rw in rows), 0.0)
                sc = _vbdot_nt(qm, kw) * (e ** -0.5) + bias_v
                p = _vwhere(valid, _vexp(_vwhere(valid, sc, 0.0) - lse_c), 0.0)
                dsc = p * (_vbdot_nt(dov, vw) + corr_c)
                dq = _vbdot(dsc, kw) * (e ** -0.5)
                dkc = _vbdot_tn(dsc, qm) * (e ** -0.5)
                dvc = _vbdot_tn(p, dov)
                for u, (starts, _) in enumerate(blocks):
                    dq_ref[_rows(starts[1], dil), :] = jnp.where(hmask.xs[2 * u], dq.xs[2 * u], dq.xs[2 * u + 1])
                    dk_u = dkc.xs[2 * u] + dkc.xs[2 * u + 1]
                    dv_u = dvc.xs[2 * u] + dvc.xs[2 * u + 1]
                    for t, st in enumerate(starts):
                        dk_ref[_rows(st, dil), :] += dk_u[t * half:(t + 1) * half]
                        dv_ref[_rows(st, dil), :] += dv_u[t * half:(t + 1) * half]
                da, db = dbias
                for u in range(DSWA_UNROLL):
                    da, db = da + dsc.xs[2 * u], db + dsc.xs[2 * u + 1]
                return da, db

            zero = jnp.zeros((half, 3 * half), F32)
            da, db = lax.fori_loop(0, s // half // DSWA_UNROLL, step, (zero, zero))
            db_ref[0] = da
            db_ref[1] = db

        _per_group(pr, run)

    col = lambda t: pl.BlockSpec((s, w), lambda p: (0, t * npair + p))
    ps = pl.BlockSpec((s, w), lambda p: (0, p))
    bs = pl.BlockSpec((2, half, 3 * half), lambda p: (p, 0, 0))
    c_in, c_out, c_shape, c_scr = _comm_specs(comm)
    res = pl.pallas_call(
        body, name=name, grid=(npair,),
        in_specs=[col(0), col(1), col(2), bs, ps, ps, ps] + c_in,
        out_specs=[ps, ps, ps, bs] + c_out,
        out_shape=[jax.ShapeDtypeStruct((s, npair * w), F32)] * 3
        + [jax.ShapeDtypeStruct((DSWA_HEADS, half, 3 * half), F32)] + c_shape,
        scratch_shapes=c_scr,
        compiler_params=_cparams(("arbitrary",)),
    )(qkv, qkv, qkv, bias, lse, do, corr, *(comm.arrays if comm else []))
    return res[0], res[1], res[2], res[3], res[4:]


def _pair_cols(g, j):
    w = 2 * DSWA_E
    return slice((g * DSWA_PG + j) * w, (g * DSWA_PG + j + 1) * w)


def _group_weights(l_ref, j):
    ls = [l_ref[:, _pair_cols(g, j)] for g in range(3)]
    m = jnp.maximum(jnp.maximum(ls[0], ls[1]), ls[2])
    es = [jnp.exp(x - m) for x in ls]
    inv = 1.0 / (es[0] + es[1] + es[2])
    return [x * inv for x in es]


def _dswa_combine_fwd(o, lse, *, name):
    s, wd = o.shape
    tr = _pick(s, (512, 256, 128))

    def body(o_ref, l_ref, c_ref):
        for j in range(DSWA_PG):
            al = _group_weights(l_ref, j)
            for g in range(3):
                c_ref[:, _pair_cols(g, j)] = (o_ref[:, _pair_cols(g, j)] * al[g]).astype(c_ref.dtype)

    row = pl.BlockSpec((tr, wd), lambda i: (i, 0))
    return pl.pallas_call(
        body, name=name, grid=(s // tr,),
        in_specs=[row, row], out_specs=row,
        out_shape=jax.ShapeDtypeStruct(o.shape, BF16),
        compiler_params=_cparams(("parallel",)),
    )(o, lse)


def _dswa_combine_bwd(o, lse, dc, *, name):
    s, wd = o.shape
    tr = _pick(s, (512, 256, 128))

    def body(o_ref, l_ref, dc_ref, do_ref, corr_ref):
        lane = lax.broadcasted_iota(jnp.int32, (tr, 2 * DSWA_E), 1)
        is_a = lane < DSWA_E
        for j in range(DSWA_PG):
            al = _group_weights(l_ref, j)
            tot = jnp.zeros((tr, 2 * DSWA_E), F32)
            for g in range(3):
                cols = _pair_cols(g, j)
                dcv = dc_ref[:, cols]
                do_ref[:, cols] = dcv * al[g]
                prod = dcv * o_ref[:, cols]
                dal = jnp.where(is_a, jnp.sum(jnp.where(is_a, prod, 0.0), axis=-1, keepdims=True),
                                jnp.sum(jnp.where(is_a, 0.0, prod), axis=-1, keepdims=True))
                tot = tot + al[g] * dal
            for g in range(3):
                corr_ref[:, _pair_cols(g, j)] = -al[g] * tot

    row = pl.BlockSpec((tr, wd), lambda i: (i, 0))
    return pl.pallas_call(
        body, name=name, grid=(s // tr,),
        in_specs=[row, row, row], out_specs=[row, row],
        out_shape=[jax.ShapeDtypeStruct(o.shape, F32)] * 2,
        compiler_params=_cparams(("parallel",)),
    )(o, lse, dc)


class _Comm:
    def __init__(self, mode, arrays, kinds=None):
        self.mode, self.arrays, self.kinds = mode, list(arrays), kinds
        self.n = len(self.arrays)

    def out_shapes(self):
        if self.mode == "exchange":
            return [jax.ShapeDtypeStruct(x.shape, x.dtype) for x in self.arrays]
        shapes = []
        for x, kd in zip(self.arrays, self.kinds):
            shp = list(x.shape)
            if kd == "stack":
                shp = [N_DEV] + shp
            else:
                shp[-2 if kd == "rows" else -1] *= N_DEV
            shapes.append(jax.ShapeDtypeStruct(tuple(shp), x.dtype))
        return shapes

    def scratch(self):
        return [pltpu.SemaphoreType.DMA((7 * self.n,)), pltpu.SemaphoreType.DMA((7 * self.n,)),
                pltpu.SemaphoreType.DMA((self.n,))]

    def bind(self, in_refs, out_refs, sems):
        self.x, self.o = in_refs, out_refs
        self.send_sems, self.recv_sems, self.local_sems = sems
        self.pos = (lax.axis_index("x"), lax.axis_index("y"), lax.axis_index("c"))

    def _slot(self, i, px, py, pc):
        p = 4 * px + 2 * py + pc
        kd = self.kinds[i]
        if kd == "stack":
            return self.o[i].at[p]
        nd = len(self.x[i].shape)
        ax = nd - 2 if kd == "rows" else nd - 1
        size = self.x[i].shape[ax]
        idx = tuple(pl.ds(p * size, size) if a == ax else slice(None) for a in range(nd))
        return self.o[i].at[idx]

    def _gcopy(self, i, k, block, to, src=None):
        return pltpu.make_async_remote_copy(
            src_ref=self._slot(i, *block) if src is None else src, dst_ref=self._slot(i, *block),
            send_sem=self.send_sems.at[7 * i + k], recv_sem=self.recv_sems.at[7 * i + k],
            device_id=to, device_id_type=pl.DeviceIdType.MESH)

    def _chips(self):
        mx, my, _ = self.pos
        return [(1 - mx, my), (mx, 1 - my), (1 - mx, 1 - my)]

    def _xcopies(self):
        mx, my, mc = self.pos
        me = 4 * mx + 2 * my + mc
        copies = []
        for k in range(1, N_DEV):
            px = 1 - mx if (k >> 2) & 1 else mx
            py = 1 - my if (k >> 1) & 1 else my
            pc = 1 - mc if k & 1 else mc
            for i in range(self.n):
                copies.append(pltpu.make_async_remote_copy(
                    src_ref=self.x[i].at[4 * px + 2 * py + pc], dst_ref=self.o[i].at[me],
                    send_sem=self.send_sems.at[7 * i + k - 1], recv_sem=self.recv_sems.at[7 * i + k - 1],
                    device_id=(px, py, pc), device_id_type=pl.DeviceIdType.MESH))
        return copies

    def _local(self):
        mx, my, mc = self.pos
        if self.mode == "exchange":
            me = 4 * mx + 2 * my + mc
            return [pltpu.make_async_copy(self.x[i].at[me], self.o[i].at[me], self.local_sems.at[i]) for i in range(self.n)]
        return [pltpu.make_async_copy(self.x[i], self._slot(i, mx, my, mc), self.local_sems.at[i]) for i in range(self.n)]

    def _first(self):
        mx, my, mc = self.pos
        me, sibling = (mx, my, mc), (mx, my, 1 - mc)
        first = [self._gcopy(i, 0, me, sibling, src=self.x[i]) for i in range(self.n)]
        first += [self._gcopy(i, 1 + j, me, (*chip, mc), src=self.x[i]) for j, chip in enumerate(self._chips())
                  for i in range(self.n)]
        return first

    def _passed(self):
        mx, my, mc = self.pos
        return [self._gcopy(i, 4 + j, (*chip, mc), (mx, my, 1 - mc)) for j, chip in enumerate(self._chips())
                for i in range(self.n)]

    def start(self):
        for cp in self._local() + (self._xcopies() if self.mode == "exchange" else self._first()):
            cp.start()

    def mid(self):
        if self.mode == "exchange":
            return
        mx, my, mc = self.pos
        passed = self._passed()
        for j, chip in enumerate(self._chips()):
            for i in range(self.n):
                self._gcopy(i, 1 + j, (*chip, mc), (mx, my, mc)).wait_recv()
                passed[j * self.n + i].start()

    def end(self):
        mx, my, mc = self.pos
        if self.mode == "exchange":
            copies = self._xcopies()
            for cp in copies:
                cp.wait_recv()
            for cp in copies:
                cp.wait_send()
        else:
            for i in range(self.n):
                self._gcopy(i, 0, (mx, my, 1 - mc), (mx, my, mc)).wait_recv()
                for j, chip in enumerate(self._chips()):
                    self._gcopy(i, 4 + j, (*chip, 1 - mc), (mx, my, mc)).wait_recv()
            for cp in self._first() + self._passed():
                cp.wait_send()
        for cp in self._local():
            cp.wait()

    def run(self, *, name):
        n = self.n

        def body(*refs):
            self.bind(refs[:n], refs[n:2 * n], refs[2 * n:])
            self.start()
            self.mid()
            self.end()

        anyspec = pl.BlockSpec(memory_space=pl.ANY)
        return pl.pallas_call(body, name=name, in_specs=[anyspec] * n, out_specs=[anyspec] * n,
                              out_shape=self.out_shapes(), scratch_shapes=self.scratch())(*self.arrays)


def _comm_specs(comm):
    if comm is None:
        return [], [], [], []
    anyspec = pl.BlockSpec(memory_space=pl.ANY)
    return [anyspec] * comm.n, [anyspec] * comm.n, comm.out_shapes(), comm.scratch()


def _comm_hooks(comm, refs, n_in, n_out, n_scr, first, mid, last):
    if comm is None:
        return refs[:n_in], refs[n_in:n_in + n_out], refs[n_in + n_out:]
    c = comm.n
    ins, cin = refs[:n_in], refs[n_in:n_in + c]
    outs, cout = refs[n_in + c:n_in + c + n_out], refs[n_in + c + n_out:n_in + 2 * c + n_out]
    scr, sems = refs[n_in + 2 * c + n_out:n_in + 2 * c + n_out + n_scr], refs[n_in + 2 * c + n_out + n_scr:]
    comm.bind(cin, cout, sems)
    pl.when(first)(comm.start)
    pl.when(mid)(comm.mid)
    pl.when(last)(comm.end)
    return ins, outs, scr


def _adamw_update(g, w, m, v):
    mn = ADAM_B1 * m + (1.0 - ADAM_B1) * g
    vn = ADAM_B2 * v + (1.0 - ADAM_B2) * (g * g)
    m_hat = mn / (1.0 - ADAM_B1 ** ADAM_STEP)
    v_hat = vn / (1.0 - ADAM_B2 ** ADAM_STEP)
    return -ADAM_LR * (m_hat / (jnp.sqrt(v_hat) + ADAM_EPS) + ADAM_WD * w), mn, vn


def _adamw_layers(recvs, w, m, v, *, name):
    nl, ks, ns = w.shape
    tr = _pick(ks, (64, 48))

    def body(*refs):
        rv_refs = refs[:nl]
        w_ref, m_ref, v_ref, g_ref, d_ref, nm_ref, nv_ref = refs[nl:]
        for l in range(nl):
            g = rv_refs[l][0].astype(F32)
            for q in range(1, N_DEV):
                g = g + rv_refs[l][q].astype(F32)
            delta, mn, vn = _adamw_update(g, w_ref[l], m_ref[l], v_ref[l])
            g_ref[l] = g
            d_ref[l] = delta
            nm_ref[l] = mn
            nv_ref[l] = vn

    row = pl.BlockSpec((nl, tr, ns), lambda i: (0, i, 0))
    return pl.pallas_call(
        body, name=name, grid=(ks // tr,),
        in_specs=[pl.BlockSpec((N_DEV, tr, ns), lambda i: (0, i, 0))] * nl + [row] * 3,
        out_specs=[row] * 4,
        out_shape=[jax.ShapeDtypeStruct((nl, ks, ns), F32)] * 4,
        compiler_params=_cparams(("parallel",)),
    )(*recvs, w, m, v)


def _adamw_reduce(recv, w, m, v, *, name):
    r, c = w.shape
    tr = _pick(r, (128, 64, 8))

    def body(rv_ref, w_ref, m_ref, v_ref, g_ref, d_ref, nm_ref, nv_ref):
        g = rv_ref[0]
        for q in range(1, N_DEV):
            g = g + rv_ref[q]
        delta, mn, vn = _adamw_update(g, w_ref[...], m_ref[...], v_ref[...])
        g_ref[...] = g
        d_ref[...] = delta
        nm_ref[...] = mn
        nv_ref[...] = vn

    row = pl.BlockSpec((tr, c), lambda i: (i, 0))
    return pl.pallas_call(
        body, name=name, grid=(r // tr,),
        in_specs=[pl.BlockSpec((N_DEV, tr, c), lambda i: (0, i, 0)), row, row, row],
        out_specs=[row] * 4,
        out_shape=[jax.ShapeDtypeStruct((r, c), F32)] * 4,
        compiler_params=_cparams(("parallel",)),
    )(recv, w, m, v)


_BIG = ("gdn_w_in", "gdn_w_out", "dswa_w_in", "dswa_w_out", "mlp_w1", "mlp_w2")
_SMALL = ("gdn_conv_w", "norm_mix", "norm_mlp", "norm_final", "rel_bias", "gdn_a_log", "gdn_dt_bias", "gdn_norm_w")
_ORDER = ("norm_mix", "norm_mlp", "norm_final", "rel_bias", "gdn_w_in", "gdn_conv_w", "gdn_a_log", "gdn_dt_bias",
          "gdn_norm_w", "gdn_w_out", "dswa_w_in", "dswa_w_out", "mlp_w1", "mlp_w2")
_KIND = dict(gdn_w_in="stack", gdn_w_out="rows", dswa_w_in="stack", dswa_w_out="rows", mlp_w1="cols", mlp_w2="rows")


def _pack_rows(arrs, align):
    rows, counts = [], []
    for a in arrs:
        flat = a.reshape(-1)
        n = -(-flat.shape[0] // D_MODEL)
        flat = jnp.pad(flat, (0, n * D_MODEL - flat.shape[0]))
        rows.append(flat.reshape(n, D_MODEL))
        counts.append(n)
    out = jnp.concatenate(rows, axis=0)
    total = -(-out.shape[0] // align) * align
    return jnp.pad(out, ((0, total - out.shape[0]), (0, 0))), counts


def _unpack_rows(slab, shapes):
    outs, r = [], 0
    for shp in shapes:
        size = int(np.prod(shp))
        n = -(-size // D_MODEL)
        outs.append(slab[r:r + n].reshape(-1)[:size].reshape(shp))
        r += n
    return outs


def _col_shards(full, nshard):
    lead = full.shape[:-1]
    n = full.shape[-1] // nshard
    t = full.reshape(lead + (nshard, n))
    return jnp.moveaxis(t, -2, 0)


def _from_col_shards(g):
    t = jnp.moveaxis(g, 0, -2)
    return t.reshape(t.shape[:-2] + (t.shape[-2] * t.shape[-1],))


def kernel(x, norm_mix, norm_mlp, norm_final, rel_bias, gdn_w_in, gdn_conv_w, gdn_a_log, gdn_dt_bias, gdn_norm_w, gdn_w_out, dswa_w_in, dswa_w_out, mlp_w1, mlp_w2, loss_target, m_norm_mix, m_norm_mlp, m_norm_final, m_rel_bias, m_gdn_w_in, m_gdn_conv_w, m_gdn_a_log, m_gdn_dt_bias, m_gdn_norm_w, m_gdn_w_out, m_dswa_w_in, m_dswa_w_out, m_mlp_w1, m_mlp_w2, v_norm_mix, v_norm_mlp, v_norm_final, v_rel_bias, v_gdn_w_in, v_gdn_conv_w, v_gdn_a_log, v_gdn_dt_bias, v_gdn_norm_w, v_gdn_w_out, v_dswa_w_in, v_dswa_w_out, v_mlp_w1, v_mlp_w2):
    params = dict(norm_mix=norm_mix, norm_mlp=norm_mlp, norm_final=norm_final, rel_bias=rel_bias,
                  gdn_w_in=gdn_w_in, gdn_conv_w=gdn_conv_w, gdn_a_log=gdn_a_log, gdn_dt_bias=gdn_dt_bias,
                  gdn_norm_w=gdn_norm_w, gdn_w_out=gdn_w_out, dswa_w_in=dswa_w_in, dswa_w_out=dswa_w_out,
                  mlp_w1=mlp_w1, mlp_w2=mlp_w2)
    mom_m = dict(norm_mix=m_norm_mix, norm_mlp=m_norm_mlp, norm_final=m_norm_final, rel_bias=m_rel_bias,
                 gdn_w_in=m_gdn_w_in, gdn_conv_w=m_gdn_conv_w, gdn_a_log=m_gdn_a_log, gdn_dt_bias=m_gdn_dt_bias,
                 gdn_norm_w=m_gdn_norm_w, gdn_w_out=m_gdn_w_out, dswa_w_in=m_dswa_w_in, dswa_w_out=m_dswa_w_out,
                 mlp_w1=m_mlp_w1, mlp_w2=m_mlp_w2)
    mom_v = dict(norm_mix=v_norm_mix, norm_mlp=v_norm_mlp, norm_final=v_norm_final, rel_bias=v_rel_bias,
                 gdn_w_in=v_gdn_w_in, gdn_conv_w=v_gdn_conv_w, gdn_a_log=v_gdn_a_log, gdn_dt_bias=v_gdn_dt_bias,
                 gdn_norm_w=v_gdn_norm_w, gdn_w_out=v_gdn_w_out, dswa_w_in=v_dswa_w_in, dswa_w_out=v_dswa_w_out,
                 mlp_w1=v_mlp_w1, mlp_w2=v_mlp_w2)
    xs = x[0]
    target = loss_target[0]
    dist = _Dist(params)
    conv_tail, _ = _pack_rows([gdn_conv_w], 8)
    (conv_g,) = dist.put("start", dist.gather_comm("start", extra=[(conv_tail, "stack")]).run(name="ag_start"))
    conv_parts = [_unpack_rows(conv_g[dev], [gdn_conv_w.shape])[0] for dev in range(N_DEV)]
    conv_full = _from_col_shards(jnp.stack(conv_parts))[:, :, 0, :]

    loss_part, dcur, g_big, rep, g_conv = _local_step(
        xs, target, dict(norm_mix=norm_mix, norm_mlp=norm_mlp, norm_final=norm_final, rel_bias=rel_bias,
                         gdn_a_log=gdn_a_log, gdn_dt_bias=gdn_dt_bias, gdn_norm_w=gdn_norm_w), dist.full, conv_full, dist)
    loss = lax.psum(loss_part[0, 0], ("x", "y", "c"))
    grad_x = dcur[None]

    conv_dev = _col_shards(jnp.stack(g_conv)[:, :, None, :], N_DEV)
    small_send = jnp.stack([_pack_rows([conv_dev[dev]] + [rep[n] for n in _SMALL[1:]], 8)[0] for dev in range(N_DEV)])
    (small_recv,) = dist.got("end", dist.send_comm("end", g_big, extra=[small_send]).run(name="grad_exchange"))

    outs = {}
    for n in _BIG:
        recvs = [dist.recv[(n, l)] for l in range(params[n].shape[0])]
        res = _adamw_layers(recvs, params[n], mom_m[n], mom_v[n], name=f"adamw_{n}")
        for tag, t in zip(("grad", "delta", "new_m", "new_v"), res):
            outs[(tag, n)] = t
    w_slab, _ = _pack_rows([params[n] for n in _SMALL], 8)
    m_slab, _ = _pack_rows([mom_m[n] for n in _SMALL], 8)
    v_slab, _ = _pack_rows([mom_v[n] for n in _SMALL], 8)
    small = _adamw_reduce(small_recv, w_slab, m_slab, v_slab, name="adamw_small")
    shapes = [params[n].shape for n in _SMALL]
    for tag, slab in zip(("grad", "delta", "new_m", "new_v"), small):
        for n, t in zip(_SMALL, _unpack_rows(slab, shapes)):
            outs[(tag, n)] = t
    result = [loss, grad_x]
    for tag in ("grad", "delta", "new_m", "new_v"):
        result += [outs[(tag, n)] for n in _ORDER]
    return tuple(result)


_GATHER = {
    "start": (("gdn_w_in", 0),),
    "gdn_proj0": (("gdn_w_out", 0), ("mlp_w1", 0)),
    "chunk_fwd0": (("mlp_w2", 0), ("dswa_w_in", 0), ("dswa_w_out", 0), ("mlp_w1", 1)),
    "mlp_up0": (("mlp_w2", 1),),
    "mlp_down0": (("gdn_w_in", 1),),
    "attn_fwd1": (("gdn_w_out", 1), ("mlp_w1", 2), ("mlp_w2", 2)),
    "chunk_fwd2": (("dswa_w_in", 1), ("dswa_w_out", 1), ("mlp_w1", 3), ("mlp_w2", 3)),
}
_SEND = {
    "attn_bwd3": (("mlp_w1", 3), ("mlp_w2", 3)),
    "chunk_bwd2": (("dswa_w_in", 1), ("dswa_w_out", 1), ("mlp_w1", 2), ("mlp_w2", 2)),
    "attn_bwd1": (("mlp_w1", 1), ("mlp_w2", 1)),
    "chunk_bwd0": (("gdn_w_in", 1), ("gdn_w_out", 1), ("dswa_w_in", 0), ("dswa_w_out", 0), ("mlp_w2", 0)),
    "pre_bwd0": (("mlp_w1", 0), ("gdn_w_out", 0)),
    "gdn_proj_bwd0": (("gdn_w_in", 0),),
    "end": (),
}


class _Dist:
    def __init__(self, params):
        self.shards = {n: params[n].astype(BF16) for n in _BIG}
        self.full = {n: [None] * params[n].shape[0] for n in _BIG}
        self.recv = {}

    def gather_comm(self, tag, extra=()):
        if tag not in _GATHER:
            return None
        arrays = [self.shards[n][l] for n, l in _GATHER[tag]] + [a for a, _ in extra]
        return _Comm("gather", arrays, [_KIND[n] for n, _ in _GATHER[tag]] + [k for _, k in extra])

    def put(self, tag, outs):
        for (n, l), t in zip(_GATHER.get(tag, ()), outs):
            self.full[n][l] = _from_col_shards(t) if _KIND[n] == "stack" else t
        return outs[len(_GATHER.get(tag, ())):]

    def send_comm(self, tag, g_big, extra=()):
        if tag not in _SEND:
            return None
        arrays = [_col_shards(g_big[n][l], N_DEV) if _KIND[n] == "stack" else g_big[n][l] for n, l in _SEND[tag]]
        return _Comm("exchange", arrays + list(extra))

    def got(self, tag, outs):
        for item, t in zip(_SEND.get(tag, ()), outs):
            self.recv[item] = t
        return outs[len(_SEND.get(tag, ())):]


def _mm_gather(dist, tag, *args, **kw):
    comm = dist and dist.gather_comm(tag)
    if not comm:
        return _mm(*args, **kw)
    res, got = _mm(*args, comm=comm, **kw)
    dist.put(tag, got)
    return res


def _ep_residual_norm(acc, res, g):
    x = acc + res
    r = lax.rsqrt(jnp.mean(x * x, axis=-1, keepdims=True) + RMS_EPS)
    return x, x * r * g


def _ep_rms_bwd(dh, x, dres, g):
    r = lax.rsqrt(jnp.mean(x * x, axis=-1, keepdims=True) + RMS_EPS)
    xn = x * r
    dn = dh * g
    dx = dres + r * (dn - xn * jnp.mean(dn * xn, axis=-1, keepdims=True))
    return dx, dx, jnp.sum(dh * xn, axis=0, keepdims=True)


def _local_step(xs, target, sp, full, conv_full, dist=None):
    s = xs.shape[0]
    norm_mix, norm_mlp, norm_final = sp["norm_mix"], sp["norm_mlp"], sp["norm_final"]
    gdn_a_log, gdn_dt_bias, gdn_norm_w = sp["gdn_a_log"], sp["gdn_dt_bias"], sp["gdn_norm_w"]
    onehot = _bucket_onehot()
    table_t = sp["rel_bias"].T
    bias = _dswa_bias(table_t, onehot, name="dswa_bias").reshape(DSWA_HEADS, DSWA_HALF, 3 * DSWA_HALF)

    saved = []
    cur = xs
    row = lambda v: v.reshape(1, -1)
    h = _rms_fwd(cur, norm_mix[0], name="rms_mix_fwd0")
    for i in range(DEPTH):
        j = i // 2
        sv = dict(x_in=cur, h=h)
        if i % 2 == 0:
            w_in = full["gdn_w_in"][j]
            proj = _mm_gather(dist, f"gdn_proj{i}", h, w_in, b_cols=(0, GDN_MAIN), name=f"gdn_proj{i}")
            ab = _mm(h, w_in[:, GDN_MAIN:], name=f"gdn_proj_ab{i}")
            qkvn = _gdn_pre_fwd(proj, conv_full[j], name=f"gdn_pre_fwd{i}")
            g_all, beta_all = _gdn_gate_fwd(ab[:, :2 * GDN_HEADS], ab[:, 2 * GDN_HEADS:], gdn_a_log[j], gdn_dt_bias[j],
                                            name=f"gdn_gate_fwd{i}")
            gshape = (2, GDN_HEADS, s // GDN_CHUNK, 1, GDN_CHUNK)
            g_row = g_all.T.reshape(gshape)
            b_row = beta_all.T.reshape(gshape)
            o, states, got = _gdn_chunk_fwd(qkvn, g_row, b_row, name=f"gdn_chunk_fwd{i}",
                                            comm=dist and dist.gather_comm(f"chunk_fwd{i}"))
            if dist:
                dist.put(f"chunk_fwd{i}", got)
            act = _gdn_post_fwd(o, proj, gdn_norm_w[j], name=f"gdn_post_fwd{i}")
            sv.update(proj=proj, ab=ab, qkvn=qkvn, g_row=g_row, b_row=b_row, o=o, states=states, act=act)
            w_out = full["gdn_w_out"][j]
        else:
            w_in = full["dswa_w_in"][j]
            qkv = _mm(h, w_in, name=f"dswa_proj{i}")
            o_n, lse_n, got = _dswa_attn_fwd(qkv, bias, name=f"dswa_attn_fwd{i}",
                                             comm=dist and dist.gather_comm(f"attn_fwd{i}"))
            if dist:
                dist.put(f"attn_fwd{i}", got)
            act = _dswa_combine_fwd(o_n, lse_n, name=f"dswa_comb_fwd{i}")
            sv.update(qkv=qkv, o_n=o_n, lse_n=lse_n, act=act)
            w_out = full["dswa_w_out"][j]
        cur, h2 = _mm(act, w_out, name=f"mix_out{i}", out_dtypes=(F32, BF16), epilogue=_ep_residual_norm,
                      extras=(cur,), vecs=(row(norm_mlp[i]),))
        sv["x_mid"] = cur
        u, a = _mm_gather(dist, f"mlp_up{i}", h2, full["mlp_w1"][i], name=f"mlp_up{i}", out_dtypes=(BF16, BF16),
                          epilogue=lambda acc: (acc, jnp.square(jnp.maximum(acc, 0.0))))
        if i + 1 < DEPTH:
            cur, h = _mm_gather(dist, f"mlp_down{i}", a, full["mlp_w2"][i], name=f"mlp_down{i}", out_dtypes=(F32, BF16),
                                epilogue=_ep_residual_norm, extras=(cur,), vecs=(row(norm_mix[i + 1]),))
        else:
            cur = _mm_gather(dist, f"mlp_down{i}", a, full["mlp_w2"][i], name=f"mlp_down{i}",
                             epilogue=lambda acc, r: (acc + r,), extras=(cur,))
        sv.update(h2=h2, u=u, a=a)
        saved.append(sv)

    loss_part, dcur, dcur_b, dg_final = _loss_head(cur, norm_final, target, name="loss_head")

    g_norm_mix, g_norm_mlp = [None] * DEPTH, [None] * DEPTH
    g_big = {n: [None] * len(full[n]) for n in _BIG}
    g_conv, g_alog, g_dt, g_nw = [None] * 2, [None] * 2, [None] * 2, [None] * 2
    d_table_t = jnp.zeros((DSWA_HEADS, REL_BUCKETS), F32)
    for i in reversed(range(DEPTH)):
        j = i // 2
        sv = saved[i]
        w1, w2 = full["mlp_w1"][i], full["mlp_w2"][i]
        du = _mm(dcur_b, w2, tb=True, name=f"mlp_down_bwd{i}", out_dtypes=(BF16,),
                 epilogue=lambda acc, uu: (acc * (2.0 * jnp.maximum(uu.astype(F32), 0.0)),), extras=(sv["u"],))
        g_big["mlp_w2"][i] = _mm(sv["a"], dcur_b, ta=True, name=f"mlp_w2_grad{i}", out_dtypes=(BF16,), shard="rows")
        g_big["mlp_w1"][i] = _mm(sv["h2"], du, ta=True, name=f"mlp_w1_grad{i}", out_dtypes=(BF16,), shard="cols")
        dmid, dmid_b, g_norm_mlp[i] = _mm(du, w1, tb=True, name=f"mlp_up_bwd{i}", out_dtypes=(F32, BF16),
                                          epilogue=_ep_rms_bwd, extras=(sv["x_mid"], dcur), vecs=(row(norm_mlp[i]),),
                                          vec_out=True)
        if i % 2 == 0:
            w_in, w_out = full["gdn_w_in"][j], full["gdn_w_out"][j]
            dact = _mm(dmid_b, w_out, tb=True, name=f"mix_out_bwd{i}")
            g_big["gdn_w_out"][j] = _mm(sv["act"], dmid_b, ta=True, name=f"mix_out_grad{i}", out_dtypes=(BF16,),
                                        shard="rows")
            do, dz, g_nw[j] = _gdn_post_bwd(sv["o"], sv["proj"], gdn_norm_w[j], dact, name=f"gdn_post_bwd{i}")
            dqkvn, dg_row, db_row, got = _gdn_chunk_bwd(sv["qkvn"], sv["g_row"], sv["b_row"], sv["states"], do,
                                                        name=f"gdn_chunk_bwd{i}",
                                                        comm=dist and dist.send_comm(f"chunk_bwd{i}", g_big))
            if dist:
                dist.got(f"chunk_bwd{i}", got)
            dproj, g_conv[j], got = _gdn_pre_bwd(sv["proj"], conv_full[j], dqkvn, dz, name=f"gdn_pre_bwd{i}",
                                                 comm=dist and dist.send_comm(f"pre_bwd{i}", g_big))
            if dist:
                dist.got(f"pre_bwd{i}", got)
            nh2 = 2 * GDN_HEADS
            da_, db_, g_alog[j], g_dt[j] = _gdn_gate_bwd(sv["ab"][:, :nh2], sv["ab"][:, nh2:], gdn_a_log[j], gdn_dt_bias[j],
                                                         dg_row.reshape(nh2, s).T, db_row.reshape(nh2, s).T,
                                                         name=f"gdn_gate_bwd{i}")
            dab = jnp.concatenate([da_, db_], axis=1)
            gw_main = _mm(sv["h"], dproj, ta=True, name=f"gdn_w_in_grad{i}", out_dtypes=(BF16,))
            gw_ab = _mm(sv["h"], dab, ta=True, name=f"gdn_w_ab_grad{i}", out_dtypes=(BF16,))
            g_big["gdn_w_in"][j] = jnp.concatenate([gw_main, gw_ab], axis=1)
            dh_ab = _mm(dab, w_in[:, GDN_MAIN:], tb=True, name=f"gdn_proj_ab_bwd{i}")
            comm = dist and dist.send_comm(f"gdn_proj_bwd{i}", g_big)
            res = _mm(dproj, w_in, b_cols=(0, GDN_MAIN), tb=True, name=f"gdn_proj_bwd{i}", out_dtypes=(F32, BF16), tm=512,
                      epilogue=lambda acc, r, x, dres, g: _ep_rms_bwd(acc + r, x, dres, g),
                      extras=(dh_ab, sv["x_in"], dmid), vecs=(row(norm_mix[i]),), vec_out=True, comm=comm)
            if comm:
                res, got = res
                dist.got(f"gdn_proj_bwd{i}", got)
            dcur, dcur_b, g_norm_mix[i] = res
        else:
            w_in, w_out = full["dswa_w_in"][j], full["dswa_w_out"][j]
            dact = _mm(dmid_b, w_out, tb=True, name=f"mix_out_bwd{i}")
            g_big["dswa_w_out"][j] = _mm(sv["act"], dmid_b, ta=True, name=f"mix_out_grad{i}", out_dtypes=(BF16,),
                                         shard="rows")
            do_n, corr_n = _dswa_combine_bwd(sv["o_n"], sv["lse_n"], dact, name=f"dswa_comb_bwd{i}")
            *dqkv, dbias, got = _dswa_attn_bwd(sv["qkv"], bias, sv["lse_n"], do_n, corr_n, name=f"dswa_attn_bwd{i}",
                                               comm=dist and dist.send_comm(f"attn_bwd{i}", g_big))
            if dist:
                dist.got(f"attn_bwd{i}", got)
            d_table_t = d_table_t + _dswa_dtable(dbias.reshape(DSWA_HEADS, -1), onehot, name=f"dswa_dtable{i}")
            g_big["dswa_w_in"][j] = jnp.concatenate(
                [_mm(sv["h"], dt, ta=True, name=f"dswa_w_in_grad{i}_{t}", out_dtypes=(BF16,)) for t, dt in enumerate(dqkv)],
                axis=1)
            cols = [(t * DSWA_WIDTH, DSWA_WIDTH) for t in range(3)]
            dh = _mm(dqkv[0], w_in, b_cols=cols[0], tb=True, name=f"dswa_proj_bwd{i}_0")
            dh = _mm(dqkv[1], w_in, b_cols=cols[1], tb=True, name=f"dswa_proj_bwd{i}_1",
                     epilogue=lambda acc, r: (acc + r,), extras=(dh,))
            dcur, dcur_b, g_norm_mix[i] = _mm(
                dqkv[2], w_in, b_cols=cols[2], tb=True, name=f"dswa_proj_bwd{i}_2", out_dtypes=(F32, BF16), tm=512,
                epilogue=lambda acc, r, x, dres, g: _ep_rms_bwd(acc + r, x, dres, g),
                extras=(dh, sv["x_in"], dmid), vecs=(row(norm_mix[i]),), vec_out=True)

    rep = dict(norm_mix=jnp.concatenate(g_norm_mix, axis=0), norm_mlp=jnp.concatenate(g_norm_mlp, axis=0),
               norm_final=dg_final.reshape(-1), rel_bias=d_table_t.T,
               gdn_a_log=jnp.stack(g_alog).reshape(gdn_a_log.shape), gdn_dt_bias=jnp.stack(g_dt).reshape(gdn_dt_bias.shape),
               gdn_norm_w=jnp.stack(g_nw).reshape(gdn_norm_w.shape))
    return loss_part, dcur, g_big, rep, g_conv
```

```python
import functools
import math

import jax
import jax.numpy as jnp
import numpy as np
from jax import lax
from jax.experimental import pallas as pl
from jax.experimental.pallas import tpu as pltpu

F32 = jnp.float32
BF16 = jnp.bfloat16
HP = lax.Precision.HIGHEST

N_DEV = 8
D_MODEL = 1024
DEPTH = 4
RMS_EPS = 1e-6
NEG_INF = -1e30

GDN_HEADS = 8
GDN_DK = 128
GDN_CONV = 5
GDN_CHUNK = 128
GDN_QKV = 3 * GDN_HEADS * GDN_DK
GDN_MAIN = GDN_QKV + GDN_HEADS * GDN_DK
GDN_AB = 4 * GDN_HEADS

DSWA_DILS = (1, 4, 16)
DSWA_HG = 6
DSWA_E = 64
DSWA_HEADS = 18
DSWA_WIDTH = DSWA_HEADS * DSWA_E
DSWA_HALF = 64
DSWA_PG = DSWA_HG // 2
DSWA_UNROLL = 8
REL_BUCKETS = 32
REL_MAX_DIST = 1024

ADAM_LR = 0.001
ADAM_B1 = 0.9
ADAM_B2 = 0.999
ADAM_EPS = 1e-08
ADAM_WD = 0.01
ADAM_STEP = 10

VMEM_LIMIT = 56 * 1024 * 1024


def _cparams(sem=None, **kw):
    return pltpu.CompilerParams(dimension_semantics=sem, vmem_limit_bytes=VMEM_LIMIT, **kw)


def _pick(dim, cands):
    for c in cands:
        if dim % c == 0:
            return c
    return dim


def _bdot(a, b):
    return jnp.dot(a.astype(BF16), b.astype(BF16), preferred_element_type=F32)


def _bdot_nt(a, b):
    return lax.dot_general(a.astype(BF16), b.astype(BF16), (((1,), (1,)), ((), ())),
                           preferred_element_type=F32)


def _bdot_tn(a, b):
    return lax.dot_general(a.astype(BF16), b.astype(BF16), (((0,), (0,)), ((), ())),
                           preferred_element_type=F32)


def _hdot(a, b):
    return jnp.dot(a, b, precision=HP, preferred_element_type=F32)


def _hdot_tn(a, b):
    return lax.dot_general(a, b, (((0,), (0,)), ((), ())), precision=HP, preferred_element_type=F32)


def _hdot_nt(a, b):
    return lax.dot_general(a, b, (((1,), (1,)), ((), ())), precision=HP, preferred_element_type=F32)


def _sigmoid(x):
    return 1.0 / (1.0 + jnp.exp(-x))


def _mm(a, b, *, name, ta=False, tb=False, out_dtypes=(F32,), epilogue=None, extras=(), vecs=(), vec_out=False,
        tm=None, tn=None, tk=None, shard=None, comm=None, b_cols=None):
    if ta:
        kdim, m = a.shape
    else:
        m, kdim = a.shape
    b0, bsz = b_cols or (0, b.shape[1])
    n = b.shape[0] if tb else bsz
    assert not tb or kdim == bsz
    if shard == "rows":
        tm = m // N_DEV if (m // N_DEV) % 128 == 0 else m
    if shard == "cols":
        tn = n // N_DEV
    tm = tm or _pick(m, (1024, 1152, 512, 384, 256, 128))
    tn = tn or _pick(n, (1024, 1152, 512, 384, 256, 128))
    tk = tk or _pick(kdim, (1024, 1152, 512, 384, 256, 128))
    nk = kdim // tk
    n_out = len(out_dtypes) + (1 if vec_out else 0)
    n_ex = len(extras) + len(vecs)
    rows_all = shard == "rows" and tm == m

    gi, gj = m // tm, n // tn

    def body(*refs):
        i, j, k = pl.program_id(0), pl.program_id(1), pl.program_id(2)
        inner = (j == 0) & (k == 0)
        ins, out_refs, (acc_ref,) = _comm_hooks(
            comm, refs, 2 + n_ex, n_out, 1, (i == 0) & inner, (i == (3 * gi) // 4) & inner,
            (i == gi - 1) & (j == gj - 1) & (k == nk - 1))
        a_ref, b_ref, ex_refs = ins[0], ins[1], ins[2:]
        if vec_out:
            out_refs, vec_ref = out_refs[:-1], out_refs[-1]

        @pl.when(k == 0)
        def _():
            acc_ref[...] = jnp.zeros_like(acc_ref)

        av = a_ref[...].astype(BF16)
        bv = b_ref[...].astype(BF16)
        dims = (((0 if ta else 1,), (1 if tb else 0,)), ((), ()))
        acc_ref[...] += lax.dot_general(av, bv, dims, preferred_element_type=F32)

        @pl.when(k == nk - 1)
        def _():
            acc = acc_ref[...]
            outs = (acc,) if epilogue is None else epilogue(acc, *[r[...] for r in ex_refs])
            if vec_out:
                part = outs[-1]

                @pl.when(i == 0)
                def _():
                    vec_ref[...] = part

                @pl.when(i > 0)
                def _():
                    vec_ref[...] += part
            for r, o in zip(out_refs, outs):
                if rows_all:
                    for p in range(N_DEV):
                        r[p] = o[p * (m // N_DEV):(p + 1) * (m // N_DEV)].astype(r.dtype)
                else:
                    r[...] = o.astype(r.dtype)

    a_spec = pl.BlockSpec((tk, tm), lambda i, j, k: (k, i)) if ta else pl.BlockSpec((tm, tk), lambda i, j, k: (i, k))
    assert b0 % (tk if tb else tn) == 0
    boff = b0 // (tk if tb else tn)
    b_spec = (pl.BlockSpec((tn, tk), lambda i, j, k: (j, k + boff)) if tb
              else pl.BlockSpec((tk, tn), lambda i, j, k: (k, j + boff)))
    o_spec = pl.BlockSpec((tm, tn), lambda i, j, k: (i, j))
    v_spec = pl.BlockSpec((1, tn), lambda i, j, k: (0, j))
    out_specs = [o_spec] * len(out_dtypes) + ([v_spec] if vec_out else [])
    out_shape = [jax.ShapeDtypeStruct((m, n), dt) for dt in out_dtypes]
    out_shape += [jax.ShapeDtypeStruct((1, n), F32)] if vec_out else []
    if shard == "rows":
        out_shape = [jax.ShapeDtypeStruct((N_DEV, m // N_DEV, n), out_dtypes[0])]
        out_specs = [pl.BlockSpec((N_DEV, m // N_DEV, tn), lambda i, j, k: (0, 0, j)) if rows_all
                     else pl.BlockSpec((None, tm, tn), lambda i, j, k: (i, 0, j))]
    if shard == "cols":
        out_shape = [jax.ShapeDtypeStruct((N_DEV, m, tn), out_dtypes[0])]
        out_specs = [pl.BlockSpec((None, tm, tn), lambda i, j, k: (j, i, 0))]
    c_in, c_out, c_shape, c_scr = _comm_specs(comm)
    outs = pl.pallas_call(
        body, name=name,
        grid=(gi, gj, nk),
        in_specs=[a_spec, b_spec] + [o_spec] * len(extras) + [v_spec] * len(vecs) + c_in,
        out_specs=out_specs + c_out,
        out_shape=out_shape + c_shape,
        scratch_shapes=[pltpu.VMEM((tm, tn), F32)] + c_scr,
        compiler_params=_cparams(("arbitrary",) * 3 if comm or vec_out else ("parallel", "parallel", "arbitrary")),
    )(a, b, *extras, *vecs, *(comm.arrays if comm else []))
    res = outs[0] if n_out == 1 else tuple(outs[:n_out])
    return (res, outs[n_out:]) if comm else res


def _rms_fwd(x, g, *, name):
    s, d = x.shape
    tr = _pick(s, (512, 256, 128))

    def body(x_ref, g_ref, h_ref):
        xv = x_ref[...]
        r = lax.rsqrt(jnp.mean(xv * xv, axis=-1, keepdims=True) + RMS_EPS)
        h_ref[...] = (xv * r * g_ref[...]).astype(h_ref.dtype)

    return pl.pallas_call(
        body, name=name, grid=(s // tr,),
        in_specs=[pl.BlockSpec((tr, d), lambda i: (i, 0)), pl.BlockSpec((1, d), lambda i: (0, 0))],
        out_specs=pl.BlockSpec((tr, d), lambda i: (i, 0)),
        out_shape=jax.ShapeDtypeStruct((s, d), BF16),
        compiler_params=_cparams(("parallel",)),
    )(x, g.reshape(1, d))


def _rms_bwd(x, g, dh, dres, *, name):
    s, d = x.shape
    tr = _pick(s, (512, 256, 128))

    def body(x_ref, g_ref, dh_ref, dres_ref, dx_ref, dxb_ref, dg_ref):
        i = pl.program_id(0)
        xv = x_ref[...]
        r = lax.rsqrt(jnp.mean(xv * xv, axis=-1, keepdims=True) + RMS_EPS)
        xn = xv * r
        dhv = dh_ref[...]
        dn = dhv * g_ref[...]
        dx = dres_ref[...] + r * (dn - xn * jnp.mean(dn * xn, axis=-1, keepdims=True))
        dx_ref[...] = dx
        dxb_ref[...] = dx.astype(dxb_ref.dtype)
        part = jnp.sum(dhv * xn, axis=0, keepdims=True)

        @pl.when(i == 0)
        def _():
            dg_ref[...] = part

        @pl.when(i > 0)
        def _():
            dg_ref[...] += part

    row = pl.BlockSpec((tr, d), lambda i: (i, 0))
    vec = pl.BlockSpec((1, d), lambda i: (0, 0))
    return pl.pallas_call(
        body, name=name, grid=(s // tr,),
        in_specs=[row, vec, row, row], out_specs=[row, row, vec],
        out_shape=[jax.ShapeDtypeStruct((s, d), F32), jax.ShapeDtypeStruct((s, d), BF16),
                   jax.ShapeDtypeStruct((1, d), F32)],
        compiler_params=_cparams(("arbitrary",)),
    )(x, g.reshape(1, d), dh, dres)


def _loss_head(x, g, target, *, name):
    s, d = x.shape
    tr = _pick(s, (512, 256, 128))

    def body(x_ref, g_ref, t_ref, loss_ref, dx_ref, dxb_ref, dg_ref):
        i = pl.program_id(0)
        xv = x_ref[...]
        gv = g_ref[...]
        r = lax.rsqrt(jnp.mean(xv * xv, axis=-1, keepdims=True) + RMS_EPS)
        xn = xv * r
        err = xn * gv - t_ref[...]
        lpart = 0.5 * jnp.sum(jnp.mean(err * err, axis=-1, keepdims=True), axis=0, keepdims=True)
        dy = err * (1.0 / d)
        dn = dy * gv
        dx = r * (dn - xn * jnp.mean(dn * xn, axis=-1, keepdims=True))
        dx_ref[...] = dx
        dxb_ref[...] = dx.astype(dxb_ref.dtype)
        gpart = jnp.sum(dy * xn, axis=0, keepdims=True)

        @pl.when(i == 0)
        def _():
            dg_ref[...] = gpart
            loss_ref[...] = lpart

        @pl.when(i > 0)
        def _():
            dg_ref[...] += gpart
            loss_ref[...] += lpart

    row = pl.BlockSpec((tr, d), lambda i: (i, 0))
    vec = pl.BlockSpec((1, d), lambda i: (0, 0))
    one = pl.BlockSpec((1, 1), lambda i: (0, 0))
    return pl.pallas_call(
        body, name=name, grid=(s // tr,),
        in_specs=[row, vec, row], out_specs=[one, row, row, vec],
        out_shape=[jax.ShapeDtypeStruct((1, 1), F32), jax.ShapeDtypeStruct((s, d), F32),
                   jax.ShapeDtypeStruct((s, d), BF16), jax.ShapeDtypeStruct((1, d), F32)],
        compiler_params=_cparams(("arbitrary",)),
    )(x, g.reshape(1, d), target)


def _shift_rows(x, sft, rows):
    s = x.shape[0]
    if sft == 0:
        return x
    y = pltpu.roll(x, (-sft) % s, 0)
    edge = slice(0, 8) if sft < 0 else slice(s - 8, s)
    ok = (rows[edge] + sft >= 0) & (rows[edge] + sft < s)
    fixed = jnp.where(ok, y[edge], 0.0)
    return jnp.concatenate([fixed, y[8:]] if sft < 0 else [y[:s - 8], fixed], axis=0)


def _gdn_pre_fwd(proj, conv_w, *, name):
    s = proj.shape[0]
    nblk = GDN_QKV // 128
    pad = GDN_CONV // 2

    def body(x_ref, w_ref, o_ref):
        j = pl.program_id(0)
        x = x_ref[...]
        rows = lax.broadcasted_iota(jnp.int32, x.shape, 0)
        c = jnp.zeros_like(x)
        for t in range(GDN_CONV):
            c = c + w_ref[pl.ds(t, 1), :] * _shift_rows(x, t - pad, rows)
        a = c * _sigmoid(c)
        rinv = lax.rsqrt(jnp.sum(a * a, axis=-1, keepdims=True) + 1e-6)
        scale = jnp.where(j < GDN_HEADS, GDN_DK ** -0.5, 1.0)
        o_ref[...] = jnp.where(j >= 2 * GDN_HEADS, a, a * (rinv * scale))

    return pl.pallas_call(
        body, name=name, grid=(nblk,),
        in_specs=[pl.BlockSpec((s, 128), lambda j: (0, j)), pl.BlockSpec((GDN_CONV, 128), lambda j: (0, j))],
        out_specs=pl.BlockSpec((s, 128), lambda j: (0, j)),
        out_shape=jax.ShapeDtypeStruct((s, GDN_QKV), F32),
        compiler_params=_cparams(("parallel",)),
    )(proj, conv_w)


def _gdn_pre_bwd(proj, conv_w, dqkv, dproj, *, name, comm=None):
    s = proj.shape[0]
    nblk = GDN_QKV // 128
    pad = GDN_CONV // 2

    def body(*refs):
        j = pl.program_id(0)
        (x_ref, w_ref, df_ref, dbk_ref, _), (dx_ref, dw_ref), _ = _comm_hooks(
            comm, refs, 5, 2, 0, j == 0, j == nblk // 2, j == nblk - 1)
        x = x_ref[...]
        rows = lax.broadcasted_iota(jnp.int32, x.shape, 0)
        xs = [_shift_rows(x, t - pad, rows) for t in range(GDN_CONV)]
        c = jnp.zeros_like(x)
        for t in range(GDN_CONV):
            c = c + w_ref[pl.ds(t, 1), :] * xs[t]
        sg = _sigmoid(c)
        a = c * sg
        rinv = lax.rsqrt(jnp.sum(a * a, axis=-1, keepdims=True) + 1e-6)
        scale = jnp.where(j < GDN_HEADS, GDN_DK ** -0.5, 1.0)
        dy = df_ref[...] + dbk_ref[...]
        nh = a * rinv
        da_n = (rinv * scale) * (dy - nh * jnp.sum(dy * nh, axis=-1, keepdims=True))
        da = jnp.where(j >= 2 * GDN_HEADS, dy, da_n)
        dc = da * (sg * (1.0 + c * (1.0 - sg)))
        dx = jnp.zeros_like(x)
        for t in range(GDN_CONV):
            dx = dx + w_ref[pl.ds(t, 1), :] * _shift_rows(dc, pad - t, rows)
            dw_ref[pl.ds(t, 1), :] = jnp.sum(dc * xs[t], axis=0, keepdims=True)
        dx_ref[...] = dx.astype(dx_ref.dtype)

    col = pl.BlockSpec((s, 128), lambda j: (0, j))
    wsp = pl.BlockSpec((GDN_CONV, 128), lambda j: (0, j))
    c_in, c_out, c_shape, c_scr = _comm_specs(comm)
    res = pl.pallas_call(
        body, name=name, grid=(nblk,),
        in_specs=[col, wsp, col, col, pl.BlockSpec(memory_space=pl.ANY)] + c_in, out_specs=[col, wsp] + c_out,
        out_shape=[jax.ShapeDtypeStruct(dproj.shape, BF16), jax.ShapeDtypeStruct((GDN_CONV, GDN_QKV), F32)] + c_shape,
        input_output_aliases={4: 0},
        scratch_shapes=c_scr,
        compiler_params=_cparams(("arbitrary",) if comm else ("parallel",)),
    )(proj, conv_w, dqkv[0], dqkv[1], dproj, *(comm.arrays if comm else []))
    return res[0], res[1], res[2:]


def _softplus(x):
    return jnp.maximum(x, 0.0) + jnp.log(1.0 + jnp.exp(-jnp.abs(x)))


def _gdn_gate_fwd(a, b, a_log, dt_bias, *, name):
    s = a.shape[0]
    nh = 2 * GDN_HEADS

    def body(a_ref, b_ref, al_ref, dt_ref, g_ref, be_ref):
        g_ref[...] = -jnp.exp(al_ref[...]) * _softplus(a_ref[...] + dt_ref[...])
        be_ref[...] = _sigmoid(b_ref[...])

    return pl.pallas_call(
        body, name=name,
        out_shape=[jax.ShapeDtypeStruct((s, nh), F32), jax.ShapeDtypeStruct((s, nh), F32)],
        compiler_params=_cparams(),
    )(a, b, a_log.reshape(1, nh), dt_bias.reshape(1, nh))


def _gdn_gate_bwd(a, b, a_log, dt_bias, dg, dbeta, *, name):
    s = a.shape[0]
    nh = 2 * GDN_HEADS

    def body(a_ref, b_ref, al_ref, dt_ref, dg_ref, db_ref, da_ref, dbb_ref, dal_ref, ddt_ref):
        ea = jnp.exp(al_ref[...])
        z = a_ref[...] + dt_ref[...]
        dgv = dg_ref[...]
        dz = dgv * (-ea) * _sigmoid(z)
        dal_ref[...] = jnp.sum(dgv * (-ea) * _softplus(z), axis=0, keepdims=True)
        ddt_ref[...] = jnp.sum(dz, axis=0, keepdims=True)
        sb = _sigmoid(b_ref[...])
        da_ref[...] = dz
        dbb_ref[...] = db_ref[...] * sb * (1.0 - sb)

    return pl.pallas_call(
        body, name=name,
        out_shape=[jax.ShapeDtypeStruct((s, nh), F32), jax.ShapeDtypeStruct((s, nh), F32),
                   jax.ShapeDtypeStruct((1, nh), F32), jax.ShapeDtypeStruct((1, nh), F32)],
        compiler_params=_cparams(),
    )(a, b, a_log.reshape(1, nh), dt_bias.reshape(1, nh), dg, dbeta)


def _chunk_masks(d):
    c = GDN_CHUNK
    ii = lax.broadcasted_iota(jnp.int32, (c, c), 0)
    jj = lax.broadcasted_iota(jnp.int32, (c, c), 1)
    dif = (ii - jj) * (1 - 2 * d)
    mi = dif >= 0
    mit = dif <= 0
    ms = dif > 0
    eye = ii == jj
    bds = [(ii >> sh) == (jj >> sh) for sh in range(3, c.bit_length() - 1)]
    return dict(mi=mi, mit=mit, ms=ms, eye=eye, bds=bds,
                mif=mi.astype(F32), mitf=mit.astype(F32), eyef=eye.astype(F32))


class _V:
    def __init__(self, xs):
        self.xs = tuple(xs)

    def __add__(self, o):
        return _lift(lambda a, b: a + b)(self, o)

    def __radd__(self, o):
        return _lift(lambda a, b: b + a)(self, o)

    def __sub__(self, o):
        return _lift(lambda a, b: a - b)(self, o)

    def __rsub__(self, o):
        return _lift(lambda a, b: b - a)(self, o)

    def __mul__(self, o):
        return _lift(lambda a, b: a * b)(self, o)

    def __rmul__(self, o):
        return _lift(lambda a, b: b * a)(self, o)

    def __and__(self, o):
        return _lift(lambda a, b: a & b)(self, o)

    def __neg__(self):
        return _lift(lambda a: -a)(self)

    def __rtruediv__(self, o):
        return _lift(lambda a, b: b / a)(self, o)


def _lift(f):
    def g(*args, **kw):
        n = next(len(a.xs) for a in args if isinstance(a, _V))
        return _V(f(*[a.xs[i] if isinstance(a, _V) else a for a in args], **kw) for i in range(n))
    return g


_vwhere, _vsum, _vexp, _vnot = _lift(jnp.where), _lift(jnp.sum), _lift(jnp.exp), _lift(jnp.logical_not)
_vhdot, _vhdot_tn = _lift(_bdot), _lift(_bdot_tn)
_vbdot, _vbdot_nt, _vbdot_tn = _lift(_bdot), _lift(_bdot_nt), _lift(_bdot_tn)
_vcat = _lift(lambda a, b: jnp.concatenate([a, b], axis=1))
_vlo = _lift(lambda a, n: a[:, :n])
_vhi = _lift(lambda a, n: a[:, n:])


def _both_masks(n):
    m = [_chunk_masks(d) for d in range(2)]
    mk = {key: _V([m[0][key]] * n + [m[1][key]] * n) for key in m[0] if key != "bds"}
    mk["bds"] = [_V([m[0]["bds"][i]] * n + [m[1]["bds"][i]] * n) for i in range(len(m[0]["bds"]))]
    return mk


def _tri_inv(a, mk):
    eyef = mk["eyef"]
    bds = mk["bds"]
    a8 = _vwhere(bds[0], a, 0.0)
    a2 = _vhdot(a8, a8)
    a4 = _vhdot(a2, a2)
    t = _vhdot(_vhdot(eyef - a8, eyef + a2), eyef + a4)
    for inner, outer in zip(bds, bds[1:] + [None]):
        off = _vnot(inner) if outer is None else (outer & _vnot(inner))
        low = _vwhere(off, a, 0.0)
        t = t - _vhdot(_vhdot(t, low), t)
    return t


def _chunk_prep(q, k, v, g_row, b_row, mk):
    dv = GDN_DK
    g_col = _vsum(mk["eyef"] * g_row, axis=1, keepdims=True)
    b_col = _vsum(mk["eyef"] * b_row, axis=1, keepdims=True)
    gc_col = _vsum(mk["mif"] * g_row, axis=1, keepdims=True)
    gc_row = _vsum(mk["mitf"] * g_col, axis=0, keepdims=True)
    gl = _vsum(g_row, axis=1, keepdims=True)
    decay = _vwhere(mk["mi"], _vexp(_vwhere(mk["mi"], gc_col - gc_row, 0.0)), 0.0)
    eg = _vexp(gc_col)
    e2 = _vexp(gl - gc_col)
    egl = _vexp(gl)
    kb = k * b_col
    pm = _vbdot_nt(kb, k)
    a = _vwhere(mk["ms"], pm * decay, 0.0)
    t = _tri_inv(a, mk)
    sol = _vhdot(t, _vcat(v * b_col, kb * eg))
    u, w = _vlo(sol, dv), _vhi(sol, dv)
    qm = _vbdot_nt(q, k)
    return dict(b_col=b_col, decay=decay, eg=eg, e2=e2, egl=egl, kb=kb, pm=pm, t=t, u=u, w=w,
                qm=qm, intra=qm * decay, qd=q * eg, kd=k * e2)


def _chunk_fwd_step(p, state):
    v_new = p["u"] - _vbdot(p["w"], state)
    o = _vbdot(p["qd"], state) + _vbdot(p["intra"], v_new)
    new_state = state * p["egl"] + _vbdot_tn(p["kd"], v_new)
    return o, new_state


def _chunk_bwd_step(q, k, v, p, mk, state, dso, do):
    dv_dim = GDN_DK
    v_new = p["u"] - _vbdot(p["w"], state)
    dvn = _vbdot_tn(p["intra"], do) + _vbdot(p["kd"], dso)
    dintra = _vbdot_nt(do, v_new)
    dqd = _vbdot_nt(do, state)
    ds = p["egl"] * dso + _vbdot_tn(p["qd"], do) - _vbdot_tn(p["w"], dvn)
    dkd = _vbdot_nt(v_new, dso)
    dgl = _vsum(_vsum(dso * state, axis=1, keepdims=True), axis=0, keepdims=True) * p["egl"]
    dw = -_vbdot_nt(dvn, state)
    drhs = _vhdot_tn(p["t"], _vcat(dvn, dw))
    dru, drw = _vlo(drhs, dv_dim), _vhi(drhs, dv_dim)
    da = -_vwhere(mk["ms"], _vbdot_nt(drhs, _vcat(p["u"], p["w"])), 0.0)
    b_col = p["b_col"]
    dv = dru * b_col
    dbeta = _vsum(dru * v, axis=1, keepdims=True)
    dkb = drw * p["eg"]
    deg = _vsum(drw * p["kb"], axis=1, keepdims=True)
    dp = da * p["decay"]
    ddecay = da * p["pm"]
    dkb = dkb + _vbdot(dp, k)
    dk = _vbdot_tn(dp, p["kb"])
    dqm = dintra * p["decay"]
    ddecay = ddecay + dintra * p["qm"]
    dq = _vbdot(dqm, k)
    dk = dk + _vbdot_tn(dqm, q)
    dd = ddecay * p["decay"]
    dgc_col = _vsum(dd, axis=1, keepdims=True)
    dgc_row = -_vsum(dd, axis=0, keepdims=True)
    dq = dq + dqd * p["eg"]
    deg = deg + _vsum(dqd * q, axis=1, keepdims=True)
    dk = dk + dkd * p["e2"]
    de2 = _vsum(dkd * k, axis=1, keepdims=True) * p["e2"]
    dgl = dgl + _vsum(de2, axis=0, keepdims=True)
    dgc_col = dgc_col - de2 + deg * p["eg"]
    dk = dk + dkb * b_col
    dbeta = dbeta + _vsum(dkb * k, axis=1, keepdims=True)
    dgc_col = dgc_col + _vsum(mk["eyef"] * dgc_row, axis=1, keepdims=True)
    dg_row = _vsum(mk["mif"] * dgc_col, axis=0, keepdims=True) + dgl
    dbeta_row = _vsum(mk["eyef"] * dbeta, axis=0, keepdims=True)
    return dq, dk, dv, dg_row, dbeta_row, ds


def _gdn_chunk_fwd(qkvn, g5, b5, *, name, comm=None):
    s = qkvn.shape[0]
    c = GDN_CHUNK
    nc = s // c
    h_, dk = GDN_HEADS, GDN_DK

    def body(*refs):
        n = pl.program_id(0)
        ins, outs, (st_scr,) = _comm_hooks(comm, refs, 6, 4, 1, n == 0, n == (3 * nc) // 4, n == nc - 1)
        x_refs, g_refs, b_refs = ins[0:2], ins[2:4], ins[4:6]
        o_refs, st_refs = outs[0:2], outs[2:4]

        @pl.when(n == 0)
        def _():
            st_scr[...] = jnp.zeros_like(st_scr)

        ch = [(d, h) for d in range(2) for h in range(h_)]
        mk = _both_masks(h_)
        q, k, v = (_V(x_refs[d][:, (t * h_ + h) * dk:(t * h_ + h + 1) * dk] for d, h in ch) for t in range(3))
        g, b = (_V(r[d][0, h, 0] for d, h in ch) for r in (g_refs, b_refs))
        state = _V(st_scr[d * h_ + h] for d, h in ch)
        o, new_state = _chunk_fwd_step(_chunk_prep(q, k, v, g, b, mk), state)
        for i, (d, h) in enumerate(ch):
            st_refs[d][h, 0] = state.xs[i]
            st_scr[d * h_ + h] = new_state.xs[i]
            o_refs[d][:, h * dk:(h + 1) * dk] = o.xs[i]

    ce = (lambda n: n, lambda n: nc - 1 - n)
    xs = [pl.BlockSpec((c, 3 * h_ * dk), lambda n, d=d: (ce[d](n), 0)) for d in range(2)]
    gates = [pl.BlockSpec((1, h_, 1, 1, c), lambda n, d=d: (d, 0, ce[d](n), 0, 0)) for d in range(2)]
    os_ = [pl.BlockSpec((c, h_ * dk), lambda n, d=d: (ce[d](n), 0)) for d in range(2)]
    sts = [pl.BlockSpec((h_, 1, dk, dk), lambda n, d=d: (0, ce[d](n), 0, 0)) for d in range(2)]
    c_in, c_out, c_shape, c_scr = _comm_specs(comm)
    res = pl.pallas_call(
        body, name=name, grid=(nc,),
        in_specs=xs + gates + gates + c_in,
        out_specs=os_ + sts + c_out,
        out_shape=[jax.ShapeDtypeStruct((s, h_ * dk), F32)] * 2 + [jax.ShapeDtypeStruct((h_, nc, dk, dk), F32)] * 2 + c_shape,
        scratch_shapes=[pltpu.VMEM((2 * h_, dk, dk), F32)] + c_scr,
        compiler_params=_cparams(("arbitrary",)),
    )(qkvn, qkvn, g5, g5, b5, b5, *(comm.arrays if comm else []))
    return res[0:2], res[2:4], res[4:]


def _gdn_chunk_bwd(qkvn, g5, b5, states, do, *, name, comm=None):
    s = qkvn.shape[0]
    c = GDN_CHUNK
    nc = s // c
    h_, dk = GDN_HEADS, GDN_DK

    def body(*refs):
        i = pl.program_id(0)
        ins, outs, (ds_scr,) = _comm_hooks(comm, refs, 10, 6, 1, i == 0, i == nc // 2, i == nc - 1)
        x_refs, g_refs, b_refs, st_refs, do_refs = ins[0:2], ins[2:4], ins[4:6], ins[6:8], ins[8:10]
        dx_refs, dg_refs, db_refs = outs[0:2], outs[2:4], outs[4:6]

        @pl.when(i == 0)
        def _():
            ds_scr[...] = jnp.zeros_like(ds_scr)

        ch = [(d, h) for d in range(2) for h in range(h_)]
        mk = _both_masks(h_)
        q, k, v = (_V(x_refs[d][:, (t * h_ + h) * dk:(t * h_ + h + 1) * dk] for d, h in ch) for t in range(3))
        g, b = (_V(r[d][0, h, 0] for d, h in ch) for r in (g_refs, b_refs))
        state = _V(st_refs[d][h, 0] for d, h in ch)
        dso = _V(ds_scr[d * h_ + h] for d, h in ch)
        dov = _V(do_refs[d][:, h * dk:(h + 1) * dk] for d, h in ch)
        res = _chunk_bwd_step(q, k, v, _chunk_prep(q, k, v, g, b, mk), mk, state, dso, dov)
        for (d, h), (dq, dkk, dvv, dg_r, db_r, ds) in zip(ch, zip(*[r.xs for r in res])):
            ds_scr[d * h_ + h] = ds
            dg_refs[d][h, 0] = dg_r
            db_refs[d][h, 0] = db_r
            for t, val in enumerate((dq, dkk, dvv)):
                dx_refs[d][:, (t * h_ + h) * dk:(t * h_ + h + 1) * dk] = val

    ce = (lambda i: nc - 1 - i, lambda i: i)
    both = lambda mk_spec: [mk_spec(d) for d in range(2)]
    xs = both(lambda d: pl.BlockSpec((c, 3 * h_ * dk), lambda i: (ce[d](i), 0)))
    gates = both(lambda d: pl.BlockSpec((1, h_, 1, 1, c), lambda i: (d, 0, ce[d](i), 0, 0)))
    sts = both(lambda d: pl.BlockSpec((h_, 1, dk, dk), lambda i: (0, ce[d](i), 0, 0)))
    dos = both(lambda d: pl.BlockSpec((c, h_ * dk), lambda i: (ce[d](i), 0)))
    gouts = both(lambda d: pl.BlockSpec((h_, 1, 1, c), lambda i: (0, ce[d](i), 0, 0)))
    c_in, c_out, c_shape, c_scr = _comm_specs(comm)
    res = pl.pallas_call(
        body, name=name, grid=(nc,),
        in_specs=xs + gates + gates + sts + dos + c_in,
        out_specs=xs + gouts + gouts + c_out,
        out_shape=[jax.ShapeDtypeStruct((s, 3 * h_ * dk), F32)] * 2
        + [jax.ShapeDtypeStruct((h_, nc, 1, c), F32)] * 4 + c_shape,
        scratch_shapes=[pltpu.VMEM((2 * h_, dk, dk), F32)] + c_scr,
        compiler_params=_cparams(("arbitrary",)),
    )(qkvn, qkvn, g5, g5, b5, b5, states[0], states[1], do, do, *(comm.arrays if comm else []))
    return res[0:2], jnp.stack(res[2:4]), jnp.stack(res[4:6]), res[6:]


def _gdn_post_fwd(o, z, norm_w, *, name):
    s = o[0].shape[0]
    h_, dk = GDN_HEADS, GDN_DK

    def body(of_ref, ob_ref, z_ref, w_ref, a_ref):
        ov = of_ref[...] + ob_ref[...]
        zv = z_ref[...]
        r = lax.rsqrt(jnp.mean(ov * ov, axis=-1, keepdims=True) + RMS_EPS)
        a_ref[...] = (ov * r * w_ref[...] * (zv * _sigmoid(zv))).astype(a_ref.dtype)

    col = pl.BlockSpec((s, dk), lambda h: (0, h))
    return pl.pallas_call(
        body, name=name, grid=(h_,),
        in_specs=[col, col, pl.BlockSpec((s, dk), lambda h: (0, 3 * h_ + h)), pl.BlockSpec((1, dk), lambda h: (0, 0))],
        out_specs=col,
        out_shape=jax.ShapeDtypeStruct((s, h_ * dk), BF16),
        compiler_params=_cparams(("parallel",)),
    )(o[0], o[1], z, norm_w.reshape(1, dk))


def _gdn_post_bwd(o, z, norm_w, dact, *, name):
    s = o[0].shape[0]
    h_, dk = GDN_HEADS, GDN_DK

    def body(of_ref, ob_ref, z_ref, w_ref, da_ref, do_ref, dz_ref, dw_ref):
        h = pl.program_id(0)
        ov = of_ref[...] + ob_ref[...]
        zv = z_ref[...]
        wv = w_ref[...]
        dav = da_ref[...]
        r = lax.rsqrt(jnp.mean(ov * ov, axis=-1, keepdims=True) + RMS_EPS)
        nrm = ov * r
        sg = _sigmoid(zv)
        sz = zv * sg
        dn = dav * wv * sz
        do_ref[...] = r * (dn - nrm * jnp.mean(dn * nrm, axis=-1, keepdims=True))
        dz_ref[...] = (dav * nrm * wv * (sg * (1.0 + zv * (1.0 - sg)))).astype(dz_ref.dtype)
        part = jnp.sum(dav * nrm * sz, axis=0, keepdims=True)

        @pl.when(h == 0)
        def _():
            dw_ref[...] = part

        @pl.when(h > 0)
        def _():
            dw_ref[...] += part

    col = pl.BlockSpec((s, dk), lambda h: (0, h))
    vec = pl.BlockSpec((1, dk), lambda h: (0, 0))
    return pl.pallas_call(
        body, name=name, grid=(h_,),
        in_specs=[col, col, pl.BlockSpec((s, dk), lambda h: (0, 3 * h_ + h)), vec, col],
        out_specs=[col, pl.BlockSpec((s, dk), lambda h: (0, 3 * h_ + h)), vec],
        out_shape=[jax.ShapeDtypeStruct((s, h_ * dk), F32), jax.ShapeDtypeStruct((s, GDN_MAIN), BF16),
                   jax.ShapeDtypeStruct((1, dk), F32)],
        compiler_params=_cparams(("arbitrary",)),
    )(o[0], o[1], z, norm_w.reshape(1, dk), dact)


def _rel_bucket(rel):
    nb = REL_BUCKETS // 2
    max_exact = nb // 2
    ret = jnp.where(rel > 0, nb, 0)
    n = jnp.abs(rel)
    nf = jnp.maximum(n, 1).astype(F32)
    large = max_exact + (jnp.log(nf / max_exact) / math.log(REL_MAX_DIST / max_exact)
                         * (nb - max_exact)).astype(jnp.int32)
    large = jnp.minimum(large, nb - 1)
    return ret + jnp.where(n < max_exact, n, large)


def _bucket_onehot():
    half = DSWA_HALF
    outs = []
    for dil in DSWA_DILS:
        rel = (jnp.arange(3 * half)[None, :] - half - jnp.arange(half)[:, None]) * dil
        outs.append(jax.nn.one_hot(_rel_bucket(rel).reshape(-1), REL_BUCKETS, dtype=F32, axis=0))
    return jnp.stack(outs)


def _head_group_select(vals):
    rows = lax.broadcasted_iota(jnp.int32, vals[0].shape, 0)
    return jnp.where(rows < DSWA_HG, vals[0], jnp.where(rows < 2 * DSWA_HG, vals[1], vals[2]))


def _dswa_bias(table_t, onehot, *, name):
    p = onehot.shape[-1]

    def body(t_ref, oh_ref, b_ref):
        b_ref[...] = _head_group_select([_hdot(t_ref[...], oh_ref[g]) for g in range(3)])

    return pl.pallas_call(body, name=name, out_shape=jax.ShapeDtypeStruct((DSWA_HEADS, p), F32),
                          compiler_params=_cparams())(table_t, onehot)


def _dswa_dtable(dbias, onehot, *, name):
    def body(d_ref, oh_ref, t_ref):
        t_ref[...] = _head_group_select([_hdot_nt(d_ref[...], oh_ref[g]) for g in range(3)])

    return pl.pallas_call(body, name=name, out_shape=jax.ShapeDtypeStruct((DSWA_HEADS, REL_BUCKETS), F32),
                          compiler_params=_cparams())(dbias, onehot)


def _rows(start, dil):
    if dil == 1:
        return pl.ds(pl.multiple_of(start, DSWA_HALF), DSWA_HALF)
    return pl.ds(start, DSWA_HALF, stride=dil)


def _attn_blocks(it, s, dil):
    half = DSWA_HALF
    nbs = s // half // dil
    ii = lax.broadcasted_iota(jnp.int32, (half, 3 * half), 0)
    jj = lax.broadcasted_iota(jnp.int32, (half, 3 * half), 1)
    band = jnp.abs(jj - half - ii) <= half
    out = []
    for u in range(DSWA_UNROLL):
        blk = it * DSWA_UNROLL + u
        r, b = blk // nbs, blk % nbs
        own = r + dil * half * b
        prev = own - jnp.where(b > 0, dil * half, 0)
        nxt = own + jnp.where(b < nbs - 1, dil * half, 0)
        ok = band & ((jj >= half) | (b > 0)) & ((jj < 2 * half) | (b < nbs - 1))
        out.append(((prev, own, nxt), ok))
    return out


def _attn_chains(q_ref, k_ref, v_ref, blocks, dil):
    lane = lax.broadcasted_iota(jnp.int32, (DSWA_HALF, 2 * DSWA_E), 1)
    qm, kw, vw, valid, hmask = [], [], [], [], []
    for (prev, own, nxt), ok in blocks:
        q = q_ref[_rows(own, dil), :].astype(BF16)
        k = jnp.concatenate([k_ref[_rows(st, dil), :] for st in (prev, own, nxt)], axis=0).astype(BF16)
        v = jnp.concatenate([v_ref[_rows(st, dil), :] for st in (prev, own, nxt)], axis=0).astype(BF16)
        for hd in range(2):
            mine = (lane < DSWA_E) if hd == 0 else (lane >= DSWA_E)
            qm.append(jnp.where(mine, q, jnp.zeros_like(q)))
            kw.append(k)
            vw.append(v)
            valid.append(ok)
            hmask.append(mine)
    return _V(qm), _V(kw), _V(vw), _V(valid), _V(hmask)


def _per_group(pr, fn):
    for gi, dil in enumerate(DSWA_DILS):
        pl.when(pr // DSWA_PG == gi)(functools.partial(fn, dil))


_vmax, _vlog = _lift(jnp.max), _lift(jnp.log)


def _dswa_attn_fwd(qkv, bias, *, name, comm=None):
    s = qkv.shape[0]
    half, e = DSWA_HALF, DSWA_E
    npair = DSWA_HEADS // 2

    def body(*refs):
        pr = pl.program_id(0)
        (q_ref, k_ref, v_ref, bias_ref), (o_ref, lse_ref), _ = _comm_hooks(
            comm, refs, 4, 2, 0, pr == 0, pr == (3 * npair) // 4, pr == npair - 1)
        bias_v = _V([bias_ref[0], bias_ref[1]] * DSWA_UNROLL)

        def run(dil):
            def step(it, carry):
                blocks = _attn_blocks(it, s, dil)
                qm, kw, vw, valid, hmask = _attn_chains(q_ref, k_ref, v_ref, blocks, dil)
                sc = _vwhere(valid, _vbdot_nt(qm, kw) * (e ** -0.5) + bias_v, NEG_INF)
                m = _vmax(sc, axis=-1, keepdims=True)
                p = _vexp(sc - m)
                l = _vsum(p, axis=-1, keepdims=True)
                o = _vbdot(p * (1.0 / l), vw)
                lse = m + _vlog(l)
                for u, ((_, own, _), _) in enumerate(blocks):
                    is_a = hmask.xs[2 * u]
                    o_ref[_rows(own, dil), :] = jnp.where(is_a, o.xs[2 * u], o.xs[2 * u + 1])
                    lse_ref[_rows(own, dil), :] = jnp.where(is_a, lse.xs[2 * u], lse.xs[2 * u + 1])
                return carry

            lax.fori_loop(0, s // half // DSWA_UNROLL, step, 0)

        _per_group(pr, run)

    col = lambda t: pl.BlockSpec((s, 2 * e), lambda p: (0, t * npair + p))
    pair = pl.BlockSpec((s, 2 * e), lambda p: (0, p))
    c_in, c_out, c_shape, c_scr = _comm_specs(comm)
    res = pl.pallas_call(
        body, name=name, grid=(npair,),
        in_specs=[col(0), col(1), col(2), pl.BlockSpec((2, half, 3 * half), lambda p: (p, 0, 0))] + c_in,
        out_specs=[pair, pair] + c_out,
        out_shape=[jax.ShapeDtypeStruct((s, npair * 2 * e), F32)] * 2 + c_shape,
        scratch_shapes=c_scr,
        compiler_params=_cparams(("arbitrary",)),
    )(qkv, qkv, qkv, bias, *(comm.arrays if comm else []))
    return res[0], res[1], res[2:]


def _dswa_attn_bwd(qkv, bias, lse, do, corr, *, name, comm=None):
    s = qkv.shape[0]
    half, e = DSWA_HALF, DSWA_E
    npair = DSWA_HEADS // 2
    w = 2 * e

    def body(*refs):
        pr = pl.program_id(0)
        (q_ref, k_ref, v_ref, bias_ref, lse_ref, do_ref, corr_ref), (dq_ref, dk_ref, dv_ref, db_ref), _ = _comm_hooks(
            comm, refs, 7, 4, 0, pr == 0, pr == npair // 2, pr == npair - 1)
        bias_v = _V([bias_ref[0], bias_ref[1]] * DSWA_UNROLL)
        dk_ref[...] = jnp.zeros_like(dk_ref)
        dv_ref[...] = jnp.zeros_like(dv_ref)

        def run(dil):
            def step(it, dbias):
                blocks = _attn_blocks(it, s, dil)
                qm, kw, vw, valid, hmask = _attn_chains(q_ref, k_ref, v_ref, blocks, dil)
                hd = [0, 1] * DSWA_UNROLL
                rows = [_rows(own, dil) for (_, own, _), _ in blocks for _ in range(2)]
                lse_c = _V(lse_ref[rw, :][:, h * e:h * e + 1] for rw, h in zip(rows, hd))
                corr_c = _V(corr_ref[rw, :][:, h * e:h * e + 1] for rw, h in zip(rows, hd))
                dov = _vwhere(hmask, _V(do_ref[rw, :] for rw in rows), 0.0)
                sc = _vbdot_nt(qm, kw) * (e ** -0.5) + bias_v
                p = _vwhere(valid, _vexp(_vwhere(valid, sc, 0.0) - lse_c), 0.0)
                dsc = p * (_vbdot_nt(dov, vw) + corr_c)
                dq = _vbdot(dsc, kw) * (e ** -0.5)
                dkc = _vbdot_tn(dsc, qm) * (e ** -0.5)
                dvc = _vbdot_tn(p, dov)
                for u, (starts, _) in enumerate(blocks):
                    dq_ref[_rows(starts[1], dil), :] = jnp.where(hmask.xs[2 * u], dq.xs[2 * u], dq.xs[2 * u + 1])
                    dk_u = dkc.xs[2 * u] + dkc.xs[2 * u + 1]
                    dv_u = dvc.xs[2 * u] + dvc.xs[2 * u + 1]
                    for t, st in enumerate(starts):
                        dk_ref[_rows(st, dil), :] += dk_u[t * half:(t + 1) * half]
                        dv_ref[_rows(st, dil), :] += dv_u[t * half:(t + 1) * half]
                da, db = dbias
                for u in range(DSWA_UNROLL):
                    da, db = da + dsc.xs[2 * u], db + dsc.xs[2 * u + 1]
                return da, db

            zero = jnp.zeros((half, 3 * half), F32)
            da, db = lax.fori_loop(0, s // half // DSWA_UNROLL, step, (zero, zero))
            db_ref[0] = da
            db_ref[1] = db

        _per_group(pr, run)

    col = lambda t: pl.BlockSpec((s, w), lambda p: (0, t * npair + p))
    ps = pl.BlockSpec((s, w), lambda p: (0, p))
    bs = pl.BlockSpec((2, half, 3 * half), lambda p: (p, 0, 0))
    c_in, c_out, c_shape, c_scr = _comm_specs(comm)
    res = pl.pallas_call(
        body, name=name, grid=(npair,),
        in_specs=[col(0), col(1), col(2), bs, ps, ps, ps] + c_in,
        out_specs=[ps, ps, ps, bs] + c_out,
        out_shape=[jax.ShapeDtypeStruct((s, npair * w), F32)] * 3
        + [jax.ShapeDtypeStruct((DSWA_HEADS, half, 3 * half), F32)] + c_shape,
        scratch_shapes=c_scr,
        compiler_params=_cparams(("arbitrary",)),
    )(qkv, qkv, qkv, bias, lse, do, corr, *(comm.arrays if comm else []))
    return res[0], res[1], res[2], res[3], res[4:]


def _pair_cols(g, j):
    w = 2 * DSWA_E
    return slice((g * DSWA_PG + j) * w, (g * DSWA_PG + j + 1) * w)


def _group_weights(l_ref, j):
    ls = [l_ref[:, _pair_cols(g, j)] for g in range(3)]
    m = jnp.maximum(jnp.maximum(ls[0], ls[1]), ls[2])
    es = [jnp.exp(x - m) for x in ls]
    inv = 1.0 / (es[0] + es[1] + es[2])
    return [x * inv for x in es]


def _dswa_combine_fwd(o, lse, *, name):
    s, wd = o.shape
    tr = _pick(s, (512, 256, 128))

    def body(o_ref, l_ref, c_ref):
        for j in range(DSWA_PG):
            al = _group_weights(l_ref, j)
            for g in range(3):
                c_ref[:, _pair_cols(g, j)] = (o_ref[:, _pair_cols(g, j)] * al[g]).astype(c_ref.dtype)

    row = pl.BlockSpec((tr, wd), lambda i: (i, 0))
    return pl.pallas_call(
        body, name=name, grid=(s // tr,),
        in_specs=[row, row], out_specs=row,
        out_shape=jax.ShapeDtypeStruct(o.shape, BF16),
        compiler_params=_cparams(("parallel",)),
    )(o, lse)


def _dswa_combine_bwd(o, lse, dc, *, name):
    s, wd = o.shape
    tr = _pick(s, (512, 256, 128))

    def body(o_ref, l_ref, dc_ref, do_ref, corr_ref):
        lane = lax.broadcasted_iota(jnp.int32, (tr, 2 * DSWA_E), 1)
        is_a = lane < DSWA_E
        for j in range(DSWA_PG):
            al = _group_weights(l_ref, j)
            tot = jnp.zeros((tr, 2 * DSWA_E), F32)
            for g in range(3):
                cols = _pair_cols(g, j)
                dcv = dc_ref[:, cols]
                do_ref[:, cols] = dcv * al[g]
                prod = dcv * o_ref[:, cols]
                dal = jnp.where(is_a, jnp.sum(jnp.where(is_a, prod, 0.0), axis=-1, keepdims=True),
                                jnp.sum(jnp.where(is_a, 0.0, prod), axis=-1, keepdims=True))
                tot = tot + al[g] * dal
            for g in range(3):
                corr_ref[:, _pair_cols(g, j)] = -al[g] * tot

    row = pl.BlockSpec((tr, wd), lambda i: (i, 0))
    return pl.pallas_call(
        body, name=name, grid=(s // tr,),
        in_specs=[row, row, row], out_specs=[row, row],
        out_shape=[jax.ShapeDtypeStruct(o.shape, F32)] * 2,
        compiler_params=_cparams(("parallel",)),
    )(o, lse, dc)


class _Comm:
    def __init__(self, mode, arrays, kinds=None):
        self.mode, self.arrays, self.kinds = mode, list(arrays), kinds
        self.n = len(self.arrays)

    def out_shapes(self):
        if self.mode == "exchange":
            return [jax.ShapeDtypeStruct(x.shape, x.dtype) for x in self.arrays]
        shapes = []
        for x, kd in zip(self.arrays, self.kinds):
            shp = list(x.shape)
            if kd == "stack":
                shp = [N_DEV] + shp
            else:
                shp[-2 if kd == "rows" else -1] *= N_DEV
            shapes.append(jax.ShapeDtypeStruct(tuple(shp), x.dtype))
        return shapes

    def scratch(self):
        return [pltpu.SemaphoreType.DMA((7 * self.n,)), pltpu.SemaphoreType.DMA((7 * self.n,)),
                pltpu.SemaphoreType.DMA((self.n,))]

    def bind(self, in_refs, out_refs, sems):
        self.x, self.o = in_refs, out_refs
        self.send_sems, self.recv_sems, self.local_sems = sems
        self.pos = (lax.axis_index("x"), lax.axis_index("y"), lax.axis_index("c"))

    def _slot(self, i, px, py, pc):
        p = 4 * px + 2 * py + pc
        kd = self.kinds[i]
        if kd == "stack":
            return self.o[i].at[p]
        nd = len(self.x[i].shape)
        ax = nd - 2 if kd == "rows" else nd - 1
        size = self.x[i].shape[ax]
        idx = tuple(pl.ds(p * size, size) if a == ax else slice(None) for a in range(nd))
        return self.o[i].at[idx]

    def _gcopy(self, i, k, block, to, src=None):
        return pltpu.make_async_remote_copy(
            src_ref=self._slot(i, *block) if src is None else src, dst_ref=self._slot(i, *block),
            send_sem=self.send_sems.at[7 * i + k], recv_sem=self.recv_sems.at[7 * i + k],
            device_id=to, device_id_type=pl.DeviceIdType.MESH)

    def _chips(self):
        mx, my, _ = self.pos
        return [(1 - mx, my), (mx, 1 - my), (1 - mx, 1 - my)]

    def _xcopies(self):
        mx, my, mc = self.pos
        me = 4 * mx + 2 * my + mc
        copies = []
        for k in range(1, N_DEV):
            px = 1 - mx if (k >> 2) & 1 else mx
            py = 1 - my if (k >> 1) & 1 else my
            pc = 1 - mc if k & 1 else mc
            for i in range(self.n):
                copies.append(pltpu.make_async_remote_copy(
                    src_ref=self.x[i].at[4 * px + 2 * py + pc], dst_ref=self.o[i].at[me],
                    send_sem=self.send_sems.at[7 * i + k - 1], recv_sem=self.recv_sems.at[7 * i + k - 1],
                    device_id=(px, py, pc), device_id_type=pl.DeviceIdType.MESH))
        return copies

    def _local(self):
        mx, my, mc = self.pos
        if self.mode == "exchange":
            me = 4 * mx + 2 * my + mc
            return [pltpu.make_async_copy(self.x[i].at[me], self.o[i].at[me], self.local_sems.at[i]) for i in range(self.n)]
        return [pltpu.make_async_copy(self.x[i], self._slot(i, mx, my, mc), self.local_sems.at[i]) for i in range(self.n)]

    def _first(self):
        mx, my, mc = self.pos
        me, sibling = (mx, my, mc), (mx, my, 1 - mc)
        first = [self._gcopy(i, 0, me, sibling, src=self.x[i]) for i in range(self.n)]
        first += [self._gcopy(i, 1 + j, me, (*chip, mc), src=self.x[i]) for j, chip in enumerate(self._chips())
                  for i in range(self.n)]
        return first

    def _passed(self):
        mx, my, mc = self.pos
        return [self._gcopy(i, 4 + j, (*chip, mc), (mx, my, 1 - mc)) for j, chip in enumerate(self._chips())
                for i in range(self.n)]

    def start(self):
        for cp in self._local() + (self._xcopies() if self.mode == "exchange" else self._first()):
            cp.start()

    def mid(self):
        if self.mode == "exchange":
            return
        mx, my, mc = self.pos
        passed = self._passed()
        for j, chip in enumerate(self._chips()):
            for i in range(self.n):
                self._gcopy(i, 1 + j, (*chip, mc), (mx, my, mc)).wait_recv()
                passed[j * self.n + i].start()

    def end(self):
        mx, my, mc = self.pos
        if self.mode == "exchange":
            copies = self._xcopies()
            for cp in copies:
                cp.wait_recv()
            for cp in copies:
                cp.wait_send()
        else:
            for i in range(self.n):
                self._gcopy(i, 0, (mx, my, 1 - mc), (mx, my, mc)).wait_recv()
                for j, chip in enumerate(self._chips()):
                    self._gcopy(i, 4 + j, (*chip, 1 - mc), (mx, my, mc)).wait_recv()
            for cp in self._first() + self._passed():
                cp.wait_send()
        for cp in self._local():
            cp.wait()

    def run(self, *, name):
        n = self.n

        def body(*refs):
            self.bind(refs[:n], refs[n:2 * n], refs[2 * n:])
            self.start()
            self.mid()
            self.end()

        anyspec = pl.BlockSpec(memory_space=pl.ANY)
        return pl.pallas_call(body, name=name, in_specs=[anyspec] * n, out_specs=[anyspec] * n,
                              out_shape=self.out_shapes(), scratch_shapes=self.scratch())(*self.arrays)


def _comm_specs(comm):
    if comm is None:
        return [], [], [], []
    anyspec = pl.BlockSpec(memory_space=pl.ANY)
    return [anyspec] * comm.n, [anyspec] * comm.n, comm.out_shapes(), comm.scratch()


def _comm_hooks(comm, refs, n_in, n_out, n_scr, first, mid, last):
    if comm is None:
        return refs[:n_in], refs[n_in:n_in + n_out], refs[n_in + n_out:]
    c = comm.n
    ins, cin = refs[:n_in], refs[n_in:n_in + c]
    outs, cout = refs[n_in + c:n_in + c + n_out], refs[n_in + c + n_out:n_in + 2 * c + n_out]
    scr, sems = refs[n_in + 2 * c + n_out:n_in + 2 * c + n_out + n_scr], refs[n_in + 2 * c + n_out + n_scr:]
    comm.bind(cin, cout, sems)
    pl.when(first)(comm.start)
    pl.when(mid)(comm.mid)
    pl.when(last)(comm.end)
    return ins, outs, scr


def _adamw_update(g, w, m, v):
    mn = ADAM_B1 * m + (1.0 - ADAM_B1) * g
    vn = ADAM_B2 * v + (1.0 - ADAM_B2) * (g * g)
    m_hat = mn / (1.0 - ADAM_B1 ** ADAM_STEP)
    v_hat = vn / (1.0 - ADAM_B2 ** ADAM_STEP)
    return -ADAM_LR * (m_hat / (jnp.sqrt(v_hat) + ADAM_EPS) + ADAM_WD * w), mn, vn


def _adamw_layers(recvs, w, m, v, *, name):
    nl, ks, ns = w.shape
    tr = _pick(ks, (64, 48))

    def body(*refs):
        rv_refs = refs[:nl]
        w_ref, m_ref, v_ref, g_ref, d_ref, nm_ref, nv_ref = refs[nl:]
        for l in range(nl):
            g = rv_refs[l][0].astype(F32)
            for q in range(1, N_DEV):
                g = g + rv_refs[l][q].astype(F32)
            delta, mn, vn = _adamw_update(g, w_ref[l], m_ref[l], v_ref[l])
            g_ref[l] = g
            d_ref[l] = delta
            nm_ref[l] = mn
            nv_ref[l] = vn

    row = pl.BlockSpec((nl, tr, ns), lambda i: (0, i, 0))
    return pl.pallas_call(
        body, name=name, grid=(ks // tr,),
        in_specs=[pl.BlockSpec((N_DEV, tr, ns), lambda i: (0, i, 0))] * nl + [row] * 3,
        out_specs=[row] * 4,
        out_shape=[jax.ShapeDtypeStruct((nl, ks, ns), F32)] * 4,
        compiler_params=_cparams(("parallel",)),
    )(*recvs, w, m, v)


def _adamw_reduce(recv, w, m, v, *, name):
    r, c = w.shape
    tr = _pick(r, (128, 64, 8))

    def body(rv_ref, w_ref, m_ref, v_ref, g_ref, d_ref, nm_ref, nv_ref):
        g = rv_ref[0]
        for q in range(1, N_DEV):
            g = g + rv_ref[q]
        delta, mn, vn = _adamw_update(g, w_ref[...], m_ref[...], v_ref[...])
        g_ref[...] = g
        d_ref[...] = delta
        nm_ref[...] = mn
        nv_ref[...] = vn

    row = pl.BlockSpec((tr, c), lambda i: (i, 0))
    return pl.pallas_call(
        body, name=name, grid=(r // tr,),
        in_specs=[pl.BlockSpec((N_DEV, tr, c), lambda i: (0, i, 0)), row, row, row],
        out_specs=[row] * 4,
        out_shape=[jax.ShapeDtypeStruct((r, c), F32)] * 4,
        compiler_params=_cparams(("parallel",)),
    )(recv, w, m, v)


_BIG = ("gdn_w_in", "gdn_w_out", "dswa_w_in", "dswa_w_out", "mlp_w1", "mlp_w2")
_SMALL = ("gdn_conv_w", "norm_mix", "norm_mlp", "norm_final", "rel_bias", "gdn_a_log", "gdn_dt_bias", "gdn_norm_w")
_ORDER = ("norm_mix", "norm_mlp", "norm_final", "rel_bias", "gdn_w_in", "gdn_conv_w", "gdn_a_log", "gdn_dt_bias",
          "gdn_norm_w", "gdn_w_out", "dswa_w_in", "dswa_w_out", "mlp_w1", "mlp_w2")
_KIND = dict(gdn_w_in="stack", gdn_w_out="rows", dswa_w_in="stack", dswa_w_out="rows", mlp_w1="cols", mlp_w2="rows")


def _pack_rows(arrs, align):
    rows, counts = [], []
    for a in arrs:
        flat = a.reshape(-1)
        n = -(-flat.shape[0] // D_MODEL)
        flat = jnp.pad(flat, (0, n * D_MODEL - flat.shape[0]))
        rows.append(flat.reshape(n, D_MODEL))
        counts.append(n)
    out = jnp.concatenate(rows, axis=0)
    total = -(-out.shape[0] // align) * align
    return jnp.pad(out, ((0, total - out.shape[0]), (0, 0))), counts


def _unpack_rows(slab, shapes):
    outs, r = [], 0
    for shp in shapes:
        size = int(np.prod(shp))
        n = -(-size // D_MODEL)
        outs.append(slab[r:r + n].reshape(-1)[:size].reshape(shp))
        r += n
    return outs


def _col_shards(full, nshard):
    lead = full.shape[:-1]
    n = full.shape[-1] // nshard
    t = full.reshape(lead + (nshard, n))
    return jnp.moveaxis(t, -2, 0)


def _from_col_shards(g):
    t = jnp.moveaxis(g, 0, -2)
    return t.reshape(t.shape[:-2] + (t.shape[-2] * t.shape[-1],))


def kernel(x, norm_mix, norm_mlp, norm_final, rel_bias, gdn_w_in, gdn_conv_w, gdn_a_log, gdn_dt_bias, gdn_norm_w, gdn_w_out, dswa_w_in, dswa_w_out, mlp_w1, mlp_w2, loss_target, m_norm_mix, m_norm_mlp, m_norm_final, m_rel_bias, m_gdn_w_in, m_gdn_conv_w, m_gdn_a_log, m_gdn_dt_bias, m_gdn_norm_w, m_gdn_w_out, m_dswa_w_in, m_dswa_w_out, m_mlp_w1, m_mlp_w2, v_norm_mix, v_norm_mlp, v_norm_final, v_rel_bias, v_gdn_w_in, v_gdn_conv_w, v_gdn_a_log, v_gdn_dt_bias, v_gdn_norm_w, v_gdn_w_out, v_dswa_w_in, v_dswa_w_out, v_mlp_w1, v_mlp_w2):
    params = dict(norm_mix=norm_mix, norm_mlp=norm_mlp, norm_final=norm_final, rel_bias=rel_bias,
                  gdn_w_in=gdn_w_in, gdn_conv_w=gdn_conv_w, gdn_a_log=gdn_a_log, gdn_dt_bias=gdn_dt_bias,
                  gdn_norm_w=gdn_norm_w, gdn_w_out=gdn_w_out, dswa_w_in=dswa_w_in, dswa_w_out=dswa_w_out,
                  mlp_w1=mlp_w1, mlp_w2=mlp_w2)
    mom_m = dict(norm_mix=m_norm_mix, norm_mlp=m_norm_mlp, norm_final=m_norm_final, rel_bias=m_rel_bias,
                 gdn_w_in=m_gdn_w_in, gdn_conv_w=m_gdn_conv_w, gdn_a_log=m_gdn_a_log, gdn_dt_bias=m_gdn_dt_bias,
                 gdn_norm_w=m_gdn_norm_w, gdn_w_out=m_gdn_w_out, dswa_w_in=m_dswa_w_in, dswa_w_out=m_dswa_w_out,
                 mlp_w1=m_mlp_w1, mlp_w2=m_mlp_w2)
    mom_v = dict(norm_mix=v_norm_mix, norm_mlp=v_norm_mlp, norm_final=v_norm_final, rel_bias=v_rel_bias,
                 gdn_w_in=v_gdn_w_in, gdn_conv_w=v_gdn_conv_w, gdn_a_log=v_gdn_a_log, gdn_dt_bias=v_gdn_dt_bias,
                 gdn_norm_w=v_gdn_norm_w, gdn_w_out=v_gdn_w_out, dswa_w_in=v_dswa_w_in, dswa_w_out=v_dswa_w_out,
                 mlp_w1=v_mlp_w1, mlp_w2=v_mlp_w2)
    xs = x[0]
    target = loss_target[0]
    dist = _Dist(params)
    conv_tail, _ = _pack_rows([gdn_conv_w], 8)
    (conv_g,) = dist.put("start", dist.gather_comm("start", extra=[(conv_tail, "stack")]).run(name="ag_start"))
    conv_parts = [_unpack_rows(conv_g[dev], [gdn_conv_w.shape])[0] for dev in range(N_DEV)]
    conv_full = _from_col_shards(jnp.stack(conv_parts))[:, :, 0, :]

    loss_part, dcur, g_big, rep, g_conv = _local_step(
        xs, target, dict(norm_mix=norm_mix, norm_mlp=norm_mlp, norm_final=norm_final, rel_bias=rel_bias,
                         gdn_a_log=gdn_a_log, gdn_dt_bias=gdn_dt_bias, gdn_norm_w=gdn_norm_w), dist.full, conv_full, dist)
    loss = lax.psum(loss_part[0, 0], ("x", "y", "c"))
    grad_x = dcur[None]

    conv_dev = _col_shards(jnp.stack(g_conv)[:, :, None, :], N_DEV)
    small_send = jnp.stack([_pack_rows([conv_dev[dev]] + [rep[n] for n in _SMALL[1:]], 8)[0] for dev in range(N_DEV)])
    (small_recv,) = dist.got("end", dist.send_comm("end", g_big, extra=[small_send]).run(name="grad_exchange"))

    outs = {}
    for n in _BIG:
        recvs = [dist.recv[(n, l)] for l in range(params[n].shape[0])]
        res = _adamw_layers(recvs, params[n], mom_m[n], mom_v[n], name=f"adamw_{n}")
        for tag, t in zip(("grad", "delta", "new_m", "new_v"), res):
            outs[(tag, n)] = t
    w_slab, _ = _pack_rows([params[n] for n in _SMALL], 8)
    m_slab, _ = _pack_rows([mom_m[n] for n in _SMALL], 8)
    v_slab, _ = _pack_rows([mom_v[n] for n in _SMALL], 8)
    small = _adamw_reduce(small_recv, w_slab, m_slab, v_slab, name="adamw_small")
    shapes = [params[n].shape for n in _SMALL]
    for tag, slab in zip(("grad", "delta", "new_m", "new_v"), small):
        for n, t in zip(_SMALL, _unpack_rows(slab, shapes)):
            outs[(tag, n)] = t
    result = [loss, grad_x]
    for tag in ("grad", "delta", "new_m", "new_v"):
        result += [outs[(tag, n)] for n in _ORDER]
    return tuple(result)


_GATHER = {
    "start": (("gdn_w_in", 0),),
    "gdn_proj0": (("gdn_w_out", 0), ("mlp_w1", 0)),
    "chunk_fwd0": (("mlp_w2", 0), ("dswa_w_in", 0), ("dswa_w_out", 0), ("mlp_w1", 1), ("gdn_w_out", 1)),
    "mlp_up0": (("mlp_w2", 1),),
    "mlp_down0": (("gdn_w_in", 1),),
    "attn_fwd1": (("mlp_w1", 2),),
    "mlp_up1": (("mlp_w2", 2),),
    "chunk_fwd2": (("dswa_w_in", 1), ("dswa_w_out", 1), ("mlp_w1", 3), ("mlp_w2", 3)),
}
_SEND = {
    "attn_bwd3": (("mlp_w1", 3), ("mlp_w2", 3)),
    "chunk_bwd2": (("dswa_w_in", 1), ("dswa_w_out", 1), ("mlp_w2", 2)),
    "pre_bwd2": (("mlp_w1", 2),),
    "gdn_proj_bwd2": (("gdn_w_in", 1),),
    "attn_bwd1": (("mlp_w1", 1), ("mlp_w2", 1)),
    "chunk_bwd0": (("gdn_w_out", 1), ("dswa_w_in", 0), ("dswa_w_out", 0), ("mlp_w2", 0)),
    "pre_bwd0": (("mlp_w1", 0), ("gdn_w_out", 0)),
    "gdn_proj_bwd0": (("gdn_w_in", 0),),
    "end": (),
}


class _Dist:
    def __init__(self, params):
        self.shards = {n: params[n].astype(BF16) for n in _BIG}
        self.full = {n: [None] * params[n].shape[0] for n in _BIG}
        self.recv = {}

    def gather_comm(self, tag, extra=()):
        if tag not in _GATHER:
            return None
        arrays = [self.shards[n][l] for n, l in _GATHER[tag]] + [a for a, _ in extra]
        return _Comm("gather", arrays, [_KIND[n] for n, _ in _GATHER[tag]] + [k for _, k in extra])

    def put(self, tag, outs):
        for (n, l), t in zip(_GATHER.get(tag, ()), outs):
            self.full[n][l] = _from_col_shards(t) if _KIND[n] == "stack" else t
        return outs[len(_GATHER.get(tag, ())):]

    def send_comm(self, tag, g_big, extra=()):
        if tag not in _SEND:
            return None
        arrays = [_col_shards(g_big[n][l], N_DEV) if _KIND[n] == "stack" else g_big[n][l] for n, l in _SEND[tag]]
        return _Comm("exchange", arrays + list(extra))

    def got(self, tag, outs):
        for item, t in zip(_SEND.get(tag, ()), outs):
            self.recv[item] = t
        return outs[len(_SEND.get(tag, ())):]


def _mm_gather(dist, tag, *args, **kw):
    comm = dist and dist.gather_comm(tag)
    if not comm:
        return _mm(*args, **kw)
    res, got = _mm(*args, comm=comm, **kw)
    dist.put(tag, got)
    return res


def _ep_residual_norm(acc, res, g):
    x = acc + res
    r = lax.rsqrt(jnp.mean(x * x, axis=-1, keepdims=True) + RMS_EPS)
    return x, x * r * g


def _ep_rms_bwd(dh, x, dres, g):
    r = lax.rsqrt(jnp.mean(x * x, axis=-1, keepdims=True) + RMS_EPS)
    xn = x * r
    dn = dh * g
    dx = dres + r * (dn - xn * jnp.mean(dn * xn, axis=-1, keepdims=True))
    return dx, dx, jnp.sum(dh * xn, axis=0, keepdims=True)


def _local_step(xs, target, sp, full, conv_full, dist=None):
    s = xs.shape[0]
    norm_mix, norm_mlp, norm_final = sp["norm_mix"], sp["norm_mlp"], sp["norm_final"]
    gdn_a_log, gdn_dt_bias, gdn_norm_w = sp["gdn_a_log"], sp["gdn_dt_bias"], sp["gdn_norm_w"]
    onehot = _bucket_onehot()
    table_t = sp["rel_bias"].T
    bias = _dswa_bias(table_t, onehot, name="dswa_bias").reshape(DSWA_HEADS, DSWA_HALF, 3 * DSWA_HALF)

    saved = []
    cur = xs
    row = lambda v: v.reshape(1, -1)
    h = _rms_fwd(cur, norm_mix[0], name="rms_mix_fwd0")
    for i in range(DEPTH):
        j = i // 2
        sv = dict(x_in=cur, h=h)
        if i % 2 == 0:
            w_in = full["gdn_w_in"][j]
            proj = _mm_gather(dist, f"gdn_proj{i}", h, w_in, b_cols=(0, GDN_MAIN), name=f"gdn_proj{i}")
            ab = _mm(h, w_in[:, GDN_MAIN:], name=f"gdn_proj_ab{i}")
            qkvn = _gdn_pre_fwd(proj, conv_full[j], name=f"gdn_pre_fwd{i}")
            g_all, beta_all = _gdn_gate_fwd(ab[:, :2 * GDN_HEADS], ab[:, 2 * GDN_HEADS:], gdn_a_log[j], gdn_dt_bias[j],
                                            name=f"gdn_gate_fwd{i}")
            gshape = (2, GDN_HEADS, s // GDN_CHUNK, 1, GDN_CHUNK)
            g_row = g_all.T.reshape(gshape)
            b_row = beta_all.T.reshape(gshape)
            o, states, got = _gdn_chunk_fwd(qkvn, g_row, b_row, name=f"gdn_chunk_fwd{i}",
                                            comm=dist and dist.gather_comm(f"chunk_fwd{i}"))
            if dist:
                dist.put(f"chunk_fwd{i}", got)
            act = _gdn_post_fwd(o, proj, gdn_norm_w[j], name=f"gdn_post_fwd{i}")
            sv.update(proj=proj, ab=ab, qkvn=qkvn, g_row=g_row, b_row=b_row, o=o, states=states, act=act)
            w_out = full["gdn_w_out"][j]
        else:
            w_in = full["dswa_w_in"][j]
            qkv = _mm(h, w_in, name=f"dswa_proj{i}")
            o_n, lse_n, got = _dswa_attn_fwd(qkv, bias, name=f"dswa_attn_fwd{i}",
                                             comm=dist and dist.gather_comm(f"attn_fwd{i}"))
            if dist:
                dist.put(f"attn_fwd{i}", got)
            act = _dswa_combine_fwd(o_n, lse_n, name=f"dswa_comb_fwd{i}")
            sv.update(qkv=qkv, o_n=o_n, lse_n=lse_n, act=act)
            w_out = full["dswa_w_out"][j]
        cur, h2 = _mm(act, w_out, name=f"mix_out{i}", out_dtypes=(F32, BF16), epilogue=_ep_residual_norm,
                      extras=(cur,), vecs=(row(norm_mlp[i]),))
        sv["x_mid"] = cur
        u, a = _mm_gather(dist, f"mlp_up{i}", h2, full["mlp_w1"][i], name=f"mlp_up{i}", out_dtypes=(BF16, BF16),
                          epilogue=lambda acc: (acc, jnp.square(jnp.maximum(acc, 0.0))))
        if i + 1 < DEPTH:
            cur, h = _mm_gather(dist, f"mlp_down{i}", a, full["mlp_w2"][i], name=f"mlp_down{i}", out_dtypes=(F32, BF16),
                                epilogue=_ep_residual_norm, extras=(cur,), vecs=(row(norm_mix[i + 1]),))
        else:
            cur = _mm_gather(dist, f"mlp_down{i}", a, full["mlp_w2"][i], name=f"mlp_down{i}",
                             epilogue=lambda acc, r: (acc + r,), extras=(cur,))
        sv.update(h2=h2, u=u, a=a)
        saved.append(sv)

    loss_part, dcur, dcur_b, dg_final = _loss_head(cur, norm_final, target, name="loss_head")

    g_norm_mix, g_norm_mlp = [None] * DEPTH, [None] * DEPTH
    g_big = {n: [None] * len(full[n]) for n in _BIG}
    g_conv, g_alog, g_dt, g_nw = [None] * 2, [None] * 2, [None] * 2, [None] * 2
    d_table_t = jnp.zeros((DSWA_HEADS, REL_BUCKETS), F32)
    for i in reversed(range(DEPTH)):
        j = i // 2
        sv = saved[i]
        w1, w2 = full["mlp_w1"][i], full["mlp_w2"][i]
        du = _mm(dcur_b, w2, tb=True, name=f"mlp_down_bwd{i}", out_dtypes=(BF16,),
                 epilogue=lambda acc, uu: (acc * (2.0 * jnp.maximum(uu.astype(F32), 0.0)),), extras=(sv["u"],))
        g_big["mlp_w2"][i] = _mm(sv["a"], dcur_b, ta=True, name=f"mlp_w2_grad{i}", out_dtypes=(BF16,), shard="rows")
        g_big["mlp_w1"][i] = _mm(sv["h2"], du, ta=True, name=f"mlp_w1_grad{i}", out_dtypes=(BF16,), shard="cols")
        dmid, dmid_b, g_norm_mlp[i] = _mm(du, w1, tb=True, name=f"mlp_up_bwd{i}", out_dtypes=(F32, BF16),
                                          epilogue=_ep_rms_bwd, extras=(sv["x_mid"], dcur), vecs=(row(norm_mlp[i]),),
                                          vec_out=True)
        if i % 2 == 0:
            w_in, w_out = full["gdn_w_in"][j], full["gdn_w_out"][j]
            dact = _mm(dmid_b, w_out, tb=True, name=f"mix_out_bwd{i}")
            g_big["gdn_w_out"][j] = _mm(sv["act"], dmid_b, ta=True, name=f"mix_out_grad{i}", out_dtypes=(BF16,),
                                        shard="rows")
            do, dz, g_nw[j] = _gdn_post_bwd(sv["o"], sv["proj"], gdn_norm_w[j], dact, name=f"gdn_post_bwd{i}")
            dqkvn, dg_row, db_row, got = _gdn_chunk_bwd(sv["qkvn"], sv["g_row"], sv["b_row"], sv["states"], do,
                                                        name=f"gdn_chunk_bwd{i}",
                                                        comm=dist and dist.send_comm(f"chunk_bwd{i}", g_big))
            if dist:
                dist.got(f"chunk_bwd{i}", got)
            dproj, g_conv[j], got = _gdn_pre_bwd(sv["proj"], conv_full[j], dqkvn, dz, name=f"gdn_pre_bwd{i}",
                                                 comm=dist and dist.send_comm(f"pre_bwd{i}", g_big))
            if dist:
                dist.got(f"pre_bwd{i}", got)
            nh2 = 2 * GDN_HEADS
            da_, db_, g_alog[j], g_dt[j] = _gdn_gate_bwd(sv["ab"][:, :nh2], sv["ab"][:, nh2:], gdn_a_log[j], gdn_dt_bias[j],
                                                         dg_row.reshape(nh2, s).T, db_row.reshape(nh2, s).T,
                                                         name=f"gdn_gate_bwd{i}")
            dab = jnp.concatenate([da_, db_], axis=1)
            gw_main = _mm(sv["h"], dproj, ta=True, name=f"gdn_w_in_grad{i}", out_dtypes=(BF16,))
            gw_ab = _mm(sv["h"], dab, ta=True, name=f"gdn_w_ab_grad{i}", out_dtypes=(BF16,))
            g_big["gdn_w_in"][j] = jnp.concatenate([gw_main, gw_ab], axis=1)
            dh_ab = _mm(dab, w_in[:, GDN_MAIN:], tb=True, name=f"gdn_proj_ab_bwd{i}")
            comm = dist and dist.send_comm(f"gdn_proj_bwd{i}", g_big)
            res = _mm(dproj, w_in, b_cols=(0, GDN_MAIN), tb=True, name=f"gdn_proj_bwd{i}", out_dtypes=(F32, BF16), tm=512,
                      epilogue=lambda acc, r, x, dres, g: _ep_rms_bwd(acc + r, x, dres, g),
                      extras=(dh_ab, sv["x_in"], dmid), vecs=(row(norm_mix[i]),), vec_out=True, comm=comm)
            if comm:
                res, got = res
                dist.got(f"gdn_proj_bwd{i}", got)
            dcur, dcur_b, g_norm_mix[i] = res
        else:
            w_in, w_out = full["dswa_w_in"][j], full["dswa_w_out"][j]
            dact = _mm(dmid_b, w_out, tb=True, name=f"mix_out_bwd{i}")
            g_big["dswa_w_out"][j] = _mm(sv["act"], dmid_b, ta=True, name=f"mix_out_grad{i}", out_dtypes=(BF16,),
                                         shard="rows")
            do_n, corr_n = _dswa_combine_bwd(sv["o_n"], sv["lse_n"], dact, name=f"dswa_comb_bwd{i}")
            *dqkv, dbias, got = _dswa_attn_bwd(sv["qkv"], bias, sv["lse_n"], do_n, corr_n, name=f"dswa_attn_bwd{i}",
                                               comm=dist and dist.send_comm(f"attn_bwd{i}", g_big))
            if dist:
                dist.got(f"attn_bwd{i}", got)
            d_table_t = d_table_t + _dswa_dtable(dbias.reshape(DSWA_HEADS, -1), onehot, name=f"dswa_dtable{i}")
            g_big["dswa_w_in"][j] = jnp.concatenate(
                [_mm(sv["h"], dt, ta=True, name=f"dswa_w_in_grad{i}_{t}", out_dtypes=(BF16,)) for t, dt in enumerate(dqkv)],
                axis=1)
            cols = [(t * DSWA_WIDTH, DSWA_WIDTH) for t in range(3)]
            dh = _mm(dqkv[0], w_in, b_cols=cols[0], tb=True, name=f"dswa_proj_bwd{i}_0")
            dh = _mm(dqkv[1], w_in, b_cols=cols[1], tb=True, name=f"dswa_proj_bwd{i}_1",
                     epilogue=lambda acc, r: (acc + r,), extras=(dh,))
            dcur, dcur_b, g_norm_mix[i] = _mm(
                dqkv[2], w_in, b_cols=cols[2], tb=True, name=f"dswa_proj_bwd{i}_2", out_dtypes=(F32, BF16), tm=512,
                epilogue=lambda acc, r, x, dres, g: _ep_rms_bwd(acc + r, x, dres, g),
                extras=(dh, sv["x_in"], dmid), vecs=(row(norm_mix[i]),), vec_out=True)

    rep = dict(norm_mix=jnp.concatenate(g_norm_mix, axis=0), norm_mlp=jnp.concatenate(g_norm_mlp, axis=0),
               norm_final=dg_final.reshape(-1), rel_bias=d_table_t.T,
               gdn_a_log=jnp.stack(g_alog).reshape(gdn_a_log.shape), gdn_dt_bias=jnp.stack(g_dt).reshape(gdn_dt_bias.shape),
               gdn_norm_w=jnp.stack(g_nw).reshape(gdn_norm_w.shape))
    return loss_part, dcur, g_big, rep, g_conv
```

```python
import functools
import math

import jax
import jax.numpy as jnp
import numpy as np
from jax import lax
from jax.experimental import pallas as pl
from jax.experimental.pallas import tpu as pltpu

F32 = jnp.float32
BF16 = jnp.bfloat16
HP = lax.Precision.HIGHEST

N_DEV = 8
D_MODEL = 1024
DEPTH = 4
RMS_EPS = 1e-6
NEG_INF = -1e30

GDN_HEADS = 8
GDN_DK = 128
GDN_CONV = 5
GDN_CHUNK = 128
GDN_QKV = 3 * GDN_HEADS * GDN_DK
GDN_MAIN = GDN_QKV + GDN_HEADS * GDN_DK
GDN_AB = 4 * GDN_HEADS

DSWA_DILS = (1, 4, 16)
DSWA_HG = 6
DSWA_E = 64
DSWA_HEADS = 18
DSWA_WIDTH = DSWA_HEADS * DSWA_E
DSWA_HALF = 64
DSWA_PG = DSWA_HG // 2
DSWA_UNROLL = 8
REL_BUCKETS = 32
REL_MAX_DIST = 1024

ADAM_LR = 0.001
ADAM_B1 = 0.9
ADAM_B2 = 0.999
ADAM_EPS = 1e-08
ADAM_WD = 0.01
ADAM_STEP = 10

VMEM_LIMIT = 56 * 1024 * 1024


def _cparams(sem=None, **kw):
    return pltpu.CompilerParams(dimension_semantics=sem, vmem_limit_bytes=VMEM_LIMIT, **kw)


def _pick(dim, cands):
    for c in cands:
        if dim % c == 0:
            return c
    return dim


def _bdot(a, b):
    return jnp.dot(a.astype(BF16), b.astype(BF16), preferred_element_type=F32)


def _bdot_nt(a, b):
    return lax.dot_general(a.astype(BF16), b.astype(BF16), (((1,), (1,)), ((), ())),
                           preferred_element_type=F32)


def _bdot_tn(a, b):
    return lax.dot_general(a.astype(BF16), b.astype(BF16), (((0,), (0,)), ((), ())),
                           preferred_element_type=F32)


def _hdot(a, b):
    return jnp.dot(a, b, precision=HP, preferred_element_type=F32)


def _hdot_tn(a, b):
    return lax.dot_general(a, b, (((0,), (0,)), ((), ())), precision=HP, preferred_element_type=F32)


def _hdot_nt(a, b):
    return lax.dot_general(a, b, (((1,), (1,)), ((), ())), precision=HP, preferred_element_type=F32)


def _sigmoid(x):
    return 1.0 / (1.0 + jnp.exp(-x))


def _mm(a, b, *, name, ta=False, tb=False, out_dtypes=(F32,), epilogue=None, extras=(), vecs=(), vec_out=False,
        tm=None, tn=None, tk=None, shard=None, comm=None, b_cols=None):
    if ta:
        kdim, m = a.shape
    else:
        m, kdim = a.shape
    b0, bsz = b_cols or (0, b.shape[1])
    n = b.shape[0] if tb else bsz
    assert not tb or kdim == bsz
    if shard == "rows":
        tm = m // N_DEV if (m // N_DEV) % 128 == 0 else m
    if shard == "cols":
        tn = n // N_DEV
    tm = tm or _pick(m, (1024, 1152, 512, 384, 256, 128))
    tn = tn or _pick(n, (1024, 1152, 512, 384, 256, 128))
    tk = tk or _pick(kdim, (1024, 1152, 512, 384, 256, 128))
    nk = kdim // tk
    n_out = len(out_dtypes) + (1 if vec_out else 0)
    n_ex = len(extras) + len(vecs)
    rows_all = shard == "rows" and tm == m

    gi, gj = m // tm, n // tn

    def body(*refs):
        i, j, k = pl.program_id(0), pl.program_id(1), pl.program_id(2)
        inner = (j == 0) & (k == 0)
        ins, out_refs, (acc_ref,) = _comm_hooks(
            comm, refs, 2 + n_ex, n_out, 1, (i == 0) & inner, (i == (3 * gi) // 4) & inner,
            (i == gi - 1) & (j == gj - 1) & (k == nk - 1))
        a_ref, b_ref, ex_refs = ins[0], ins[1], ins[2:]
        if vec_out:
            out_refs, vec_ref = out_refs[:-1], out_refs[-1]

        @pl.when(k == 0)
        def _():
            acc_ref[...] = jnp.zeros_like(acc_ref)

        av = a_ref[...].astype(BF16)
        bv = b_ref[...].astype(BF16)
        dims = (((0 if ta else 1,), (1 if tb else 0,)), ((), ()))
        acc_ref[...] += lax.dot_general(av, bv, dims, preferred_element_type=F32)

        @pl.when(k == nk - 1)
        def _():
            acc = acc_ref[...]
            outs = (acc,) if epilogue is None else epilogue(acc, *[r[...] for r in ex_refs])
            if vec_out:
                part = outs[-1]

                @pl.when(i == 0)
                def _():
                    vec_ref[...] = part

                @pl.when(i > 0)
                def _():
                    vec_ref[...] += part
            for r, o in zip(out_refs, outs):
                if rows_all:
                    for p in range(N_DEV):
                        r[p] = o[p * (m // N_DEV):(p + 1) * (m // N_DEV)].astype(r.dtype)
                else:
                    r[...] = o.astype(r.dtype)

    a_spec = pl.BlockSpec((tk, tm), lambda i, j, k: (k, i)) if ta else pl.BlockSpec((tm, tk), lambda i, j, k: (i, k))
    assert b0 % (tk if tb else tn) == 0
    boff = b0 // (tk if tb else tn)
    b_spec = (pl.BlockSpec((tn, tk), lambda i, j, k: (j, k + boff)) if tb
              else pl.BlockSpec((tk, tn), lambda i, j, k: (k, j + boff)))
    o_spec = pl.BlockSpec((tm, tn), lambda i, j, k: (i, j))
    v_spec = pl.BlockSpec((1, tn), lambda i, j, k: (0, j))
    out_specs = [o_spec] * len(out_dtypes) + ([v_spec] if vec_out else [])
    out_shape = [jax.ShapeDtypeStruct((m, n), dt) for dt in out_dtypes]
    out_shape += [jax.ShapeDtypeStruct((1, n), F32)] if vec_out else []
    if shard == "rows":
        out_shape = [jax.ShapeDtypeStruct((N_DEV, m // N_DEV, n), out_dtypes[0])]
        out_specs = [pl.BlockSpec((N_DEV, m // N_DEV, tn), lambda i, j, k: (0, 0, j)) if rows_all
                     else pl.BlockSpec((None, tm, tn), lambda i, j, k: (i, 0, j))]
    if shard == "cols":
        out_shape = [jax.ShapeDtypeStruct((N_DEV, m, tn), out_dtypes[0])]
        out_specs = [pl.BlockSpec((None, tm, tn), lambda i, j, k: (j, i, 0))]
    c_in, c_out, c_shape, c_scr = _comm_specs(comm)
    outs = pl.pallas_call(
        body, name=name,
        grid=(gi, gj, nk),
        in_specs=[a_spec, b_spec] + [o_spec] * len(extras) + [v_spec] * len(vecs) + c_in,
        out_specs=out_specs + c_out,
        out_shape=out_shape + c_shape,
        scratch_shapes=[pltpu.VMEM((tm, tn), F32)] + c_scr,
        compiler_params=_cparams(("arbitrary",) * 3 if comm or vec_out else ("parallel", "parallel", "arbitrary")),
    )(a, b, *extras, *vecs, *(comm.arrays if comm else []))
    res = outs[0] if n_out == 1 else tuple(outs[:n_out])
    return (res, outs[n_out:]) if comm else res


def _rms_fwd(x, g, *, name):
    s, d = x.shape
    tr = _pick(s, (512, 256, 128))

    def body(x_ref, g_ref, h_ref):
        xv = x_ref[...]
        r = lax.rsqrt(jnp.mean(xv * xv, axis=-1, keepdims=True) + RMS_EPS)
        h_ref[...] = (xv * r * g_ref[...]).astype(h_ref.dtype)

    return pl.pallas_call(
        body, name=name, grid=(s // tr,),
        in_specs=[pl.BlockSpec((tr, d), lambda i: (i, 0)), pl.BlockSpec((1, d), lambda i: (0, 0))],
        out_specs=pl.BlockSpec((tr, d), lambda i: (i, 0)),
        out_shape=jax.ShapeDtypeStruct((s, d), BF16),
        compiler_params=_cparams(("parallel",)),
    )(x, g.reshape(1, d))


def _rms_bwd(x, g, dh, dres, *, name):
    s, d = x.shape
    tr = _pick(s, (512, 256, 128))

    def body(x_ref, g_ref, dh_ref, dres_ref, dx_ref, dxb_ref, dg_ref):
        i = pl.program_id(0)
        xv = x_ref[...]
        r = lax.rsqrt(jnp.mean(xv * xv, axis=-1, keepdims=True) + RMS_EPS)
        xn = xv * r
        dhv = dh_ref[...]
        dn = dhv * g_ref[...]
        dx = dres_ref[...] + r * (dn - xn * jnp.mean(dn * xn, axis=-1, keepdims=True))
        dx_ref[...] = dx
        dxb_ref[...] = dx.astype(dxb_ref.dtype)
        part = jnp.sum(dhv * xn, axis=0, keepdims=True)

        @pl.when(i == 0)
        def _():
            dg_ref[...] = part

        @pl.when(i > 0)
        def _():
            dg_ref[...] += part

    row = pl.BlockSpec((tr, d), lambda i: (i, 0))
    vec = pl.BlockSpec((1, d), lambda i: (0, 0))
    return pl.pallas_call(
        body, name=name, grid=(s // tr,),
        in_specs=[row, vec, row, row], out_specs=[row, row, vec],
        out_shape=[jax.ShapeDtypeStruct((s, d), F32), jax.ShapeDtypeStruct((s, d), BF16),
                   jax.ShapeDtypeStruct((1, d), F32)],
        compiler_params=_cparams(("arbitrary",)),
    )(x, g.reshape(1, d), dh, dres)


def _loss_head(x, g, target, *, name):
    s, d = x.shape
    tr = _pick(s, (512, 256, 128))

    def body(x_ref, g_ref, t_ref, loss_ref, dx_ref, dxb_ref, dg_ref):
        i = pl.program_id(0)
        xv = x_ref[...]
        gv = g_ref[...]
        r = lax.rsqrt(jnp.mean(xv * xv, axis=-1, keepdims=True) + RMS_EPS)
        xn = xv * r
        err = xn * gv - t_ref[...]
        lpart = 0.5 * jnp.sum(jnp.mean(err * err, axis=-1, keepdims=True), axis=0, keepdims=True)
        dy = err * (1.0 / d)
        dn = dy * gv
        dx = r * (dn - xn * jnp.mean(dn * xn, axis=-1, keepdims=True))
        dx_ref[...] = dx
        dxb_ref[...] = dx.astype(dxb_ref.dtype)
        gpart = jnp.sum(dy * xn, axis=0, keepdims=True)

        @pl.when(i == 0)
        def _():
            dg_ref[...] = gpart
            loss_ref[...] = lpart

        @pl.when(i > 0)
        def _():
            dg_ref[...] += gpart
            loss_ref[...] += lpart

    row = pl.BlockSpec((tr, d), lambda i: (i, 0))
    vec = pl.BlockSpec((1, d), lambda i: (0, 0))
    one = pl.BlockSpec((1, 1), lambda i: (0, 0))
    return pl.pallas_call(
        body, name=name, grid=(s // tr,),
        in_specs=[row, vec, row], out_specs=[one, row, row, vec],
        out_shape=[jax.ShapeDtypeStruct((1, 1), F32), jax.ShapeDtypeStruct((s, d), F32),
                   jax.ShapeDtypeStruct((s, d), BF16), jax.ShapeDtypeStruct((1, d), F32)],
        compiler_params=_cparams(("arbitrary",)),
    )(x, g.reshape(1, d), target)


def _shift_rows(x, sft, rows):
    s = x.shape[0]
    if sft == 0:
        return x
    y = pltpu.roll(x, (-sft) % s, 0)
    edge = slice(0, 8) if sft < 0 else slice(s - 8, s)
    ok = (rows[edge] + sft >= 0) & (rows[edge] + sft < s)
    fixed = jnp.where(ok, y[edge], 0.0)
    return jnp.concatenate([fixed, y[8:]] if sft < 0 else [y[:s - 8], fixed], axis=0)


def _gdn_pre_fwd(proj, conv_w, *, name):
    s = proj.shape[0]
    nblk = GDN_QKV // 128
    pad = GDN_CONV // 2

    def body(x_ref, w_ref, o_ref):
        j = pl.program_id(0)
        x = x_ref[...]
        rows = lax.broadcasted_iota(jnp.int32, x.shape, 0)
        c = jnp.zeros_like(x)
        for t in range(GDN_CONV):
            c = c + w_ref[pl.ds(t, 1), :] * _shift_rows(x, t - pad, rows)
        a = c * _sigmoid(c)
        rinv = lax.rsqrt(jnp.sum(a * a, axis=-1, keepdims=True) + 1e-6)
        scale = jnp.where(j < GDN_HEADS, GDN_DK ** -0.5, 1.0)
        o_ref[...] = jnp.where(j >= 2 * GDN_HEADS, a, a * (rinv * scale))

    return pl.pallas_call(
        body, name=name, grid=(nblk,),
        in_specs=[pl.BlockSpec((s, 128), lambda j: (0, j)), pl.BlockSpec((GDN_CONV, 128), lambda j: (0, j))],
        out_specs=pl.BlockSpec((s, 128), lambda j: (0, j)),
        out_shape=jax.ShapeDtypeStruct((s, GDN_QKV), F32),
        compiler_params=_cparams(("parallel",)),
    )(proj, conv_w)


def _gdn_pre_bwd(proj, conv_w, dqkv, dproj, *, name, comm=None):
    s = proj.shape[0]
    nblk = GDN_QKV // 128
    pad = GDN_CONV // 2

    def body(*refs):
        j = pl.program_id(0)
        (x_ref, w_ref, df_ref, dbk_ref, _), (dx_ref, dw_ref), _ = _comm_hooks(
            comm, refs, 5, 2, 0, j == 0, j == nblk // 2, j == nblk - 1)
        x = x_ref[...]
        rows = lax.broadcasted_iota(jnp.int32, x.shape, 0)
        xs = [_shift_rows(x, t - pad, rows) for t in range(GDN_CONV)]
        c = jnp.zeros_like(x)
        for t in range(GDN_CONV):
            c = c + w_ref[pl.ds(t, 1), :] * xs[t]
        sg = _sigmoid(c)
        a = c * sg
        rinv = lax.rsqrt(jnp.sum(a * a, axis=-1, keepdims=True) + 1e-6)
        scale = jnp.where(j < GDN_HEADS, GDN_DK ** -0.5, 1.0)
        dy = df_ref[...] + dbk_ref[...]
        nh = a * rinv
        da_n = (rinv * scale) * (dy - nh * jnp.sum(dy * nh, axis=-1, keepdims=True))
        da = jnp.where(j >= 2 * GDN_HEADS, dy, da_n)
        dc = da * (sg * (1.0 + c * (1.0 - sg)))
        dx = jnp.zeros_like(x)
        for t in range(GDN_CONV):
            dx = dx + w_ref[pl.ds(t, 1), :] * _shift_rows(dc, pad - t, rows)
            dw_ref[pl.ds(t, 1), :] = jnp.sum(dc * xs[t], axis=0, keepdims=True)
        dx_ref[...] = dx.astype(dx_ref.dtype)

    col = pl.BlockSpec((s, 128), lambda j: (0, j))
    wsp = pl.BlockSpec((GDN_CONV, 128), lambda j: (0, j))
    c_in, c_out, c_shape, c_scr = _comm_specs(comm)
    res = pl.pallas_call(
        body, name=name, grid=(nblk,),
        in_specs=[col, wsp, col, col, pl.BlockSpec(memory_space=pl.ANY)] + c_in, out_specs=[col, wsp] + c_out,
        out_shape=[jax.ShapeDtypeStruct(dproj.shape, BF16), jax.ShapeDtypeStruct((GDN_CONV, GDN_QKV), F32)] + c_shape,
        input_output_aliases={4: 0},
        scratch_shapes=c_scr,
        compiler_params=_cparams(("arbitrary",) if comm else ("parallel",)),
    )(proj, conv_w, dqkv[0], dqkv[1], dproj, *(comm.arrays if comm else []))
    return res[0], res[1], res[2:]


def _softplus(x):
    return jnp.maximum(x, 0.0) + jnp.log(1.0 + jnp.exp(-jnp.abs(x)))


def _gdn_gate_fwd(a, b, a_log, dt_bias, *, name):
    s = a.shape[0]
    nh = 2 * GDN_HEADS

    def body(a_ref, b_ref, al_ref, dt_ref, g_ref, be_ref):
        g_ref[...] = -jnp.exp(al_ref[...]) * _softplus(a_ref[...] + dt_ref[...])
        be_ref[...] = _sigmoid(b_ref[...])

    return pl.pallas_call(
        body, name=name,
        out_shape=[jax.ShapeDtypeStruct((s, nh), F32), jax.ShapeDtypeStruct((s, nh), F32)],
        compiler_params=_cparams(),
    )(a, b, a_log.reshape(1, nh), dt_bias.reshape(1, nh))


def _gdn_gate_bwd(a, b, a_log, dt_bias, dg, dbeta, *, name):
    s = a.shape[0]
    nh = 2 * GDN_HEADS

    def body(a_ref, b_ref, al_ref, dt_ref, dg_ref, db_ref, da_ref, dbb_ref, dal_ref, ddt_ref):
        ea = jnp.exp(al_ref[...])
        z = a_ref[...] + dt_ref[...]
        dgv = dg_ref[...]
        dz = dgv * (-ea) * _sigmoid(z)
        dal_ref[...] = jnp.sum(dgv * (-ea) * _softplus(z), axis=0, keepdims=True)
        ddt_ref[...] = jnp.sum(dz, axis=0, keepdims=True)
        sb = _sigmoid(b_ref[...])
        da_ref[...] = dz
        dbb_ref[...] = db_ref[...] * sb * (1.0 - sb)

    return pl.pallas_call(
        body, name=name,
        out_shape=[jax.ShapeDtypeStruct((s, nh), F32), jax.ShapeDtypeStruct((s, nh), F32),
                   jax.ShapeDtypeStruct((1, nh), F32), jax.ShapeDtypeStruct((1, nh), F32)],
        compiler_params=_cparams(),
    )(a, b, a_log.reshape(1, nh), dt_bias.reshape(1, nh), dg, dbeta)


def _chunk_masks(d):
    c = GDN_CHUNK
    ii = lax.broadcasted_iota(jnp.int32, (c, c), 0)
    jj = lax.broadcasted_iota(jnp.int32, (c, c), 1)
    dif = (ii - jj) * (1 - 2 * d)
    mi = dif >= 0
    mit = dif <= 0
    ms = dif > 0
    eye = ii == jj
    bds = [(ii >> sh) == (jj >> sh) for sh in range(3, c.bit_length() - 1)]
    return dict(mi=mi, mit=mit, ms=ms, eye=eye, bds=bds,
                mif=mi.astype(F32), mitf=mit.astype(F32), eyef=eye.astype(F32))


class _V:
    def __init__(self, xs):
        self.xs = tuple(xs)

    def __add__(self, o):
        return _lift(lambda a, b: a + b)(self, o)

    def __radd__(self, o):
        return _lift(lambda a, b: b + a)(self, o)

    def __sub__(self, o):
        return _lift(lambda a, b: a - b)(self, o)

    def __rsub__(self, o):
        return _lift(lambda a, b: b - a)(self, o)

    def __mul__(self, o):
        return _lift(lambda a, b: a * b)(self, o)

    def __rmul__(self, o):
        return _lift(lambda a, b: b * a)(self, o)

    def __and__(self, o):
        return _lift(lambda a, b: a & b)(self, o)

    def __neg__(self):
        return _lift(lambda a: -a)(self)

    def __rtruediv__(self, o):
        return _lift(lambda a, b: b / a)(self, o)


def _lift(f):
    def g(*args, **kw):
        n = next(len(a.xs) for a in args if isinstance(a, _V))
        return _V(f(*[a.xs[i] if isinstance(a, _V) else a for a in args], **kw) for i in range(n))
    return g


_vwhere, _vsum, _vexp, _vnot = _lift(jnp.where), _lift(jnp.sum), _lift(jnp.exp), _lift(jnp.logical_not)
_vhdot, _vhdot_tn = _lift(_bdot), _lift(_bdot_tn)
_vbdot, _vbdot_nt, _vbdot_tn = _lift(_bdot), _lift(_bdot_nt), _lift(_bdot_tn)
_vcat = _lift(lambda a, b: jnp.concatenate([a, b], axis=1))
_vlo = _lift(lambda a, n: a[:, :n])
_vhi = _lift(lambda a, n: a[:, n:])


def _both_masks(n):
    m = [_chunk_masks(d) for d in range(2)]
    mk = {key: _V([m[0][key]] * n + [m[1][key]] * n) for key in m[0] if key != "bds"}
    mk["bds"] = [_V([m[0]["bds"][i]] * n + [m[1]["bds"][i]] * n) for i in range(len(m[0]["bds"]))]
    return mk


def _tri_inv(a, mk):
    eyef = mk["eyef"]
    bds = mk["bds"]
    a8 = _vwhere(bds[0], a, 0.0)
    a2 = _vhdot(a8, a8)
    a4 = _vhdot(a2, a2)
    t = _vhdot(_vhdot(eyef - a8, eyef + a2), eyef + a4)
    for inner, outer in zip(bds, bds[1:] + [None]):
        off = _vnot(inner) if outer is None else (outer & _vnot(inner))
        low = _vwhere(off, a, 0.0)
        t = t - _vhdot(_vhdot(t, low), t)
    return t


def _chunk_prep(q, k, v, g_row, b_row, mk, tuw=None):
    dv = GDN_DK
    g_col = _vsum(mk["eyef"] * g_row, axis=1, keepdims=True)
    b_col = _vsum(mk["eyef"] * b_row, axis=1, keepdims=True)
    gc_col = _vsum(mk["mif"] * g_row, axis=1, keepdims=True)
    gc_row = _vsum(mk["mitf"] * g_col, axis=0, keepdims=True)
    gl = _vsum(g_row, axis=1, keepdims=True)
    decay = _vwhere(mk["mi"], _vexp(_vwhere(mk["mi"], gc_col - gc_row, 0.0)), 0.0)
    eg = _vexp(gc_col)
    e2 = _vexp(gl - gc_col)
    egl = _vexp(gl)
    kb = k * b_col
    pm = _vbdot_nt(kb, k)
    if tuw is None:
        t = _tri_inv(_vwhere(mk["ms"], pm * decay, 0.0), mk)
        sol = _vhdot(t, _vcat(v * b_col, kb * eg))
        u, w = _vlo(sol, dv), _vhi(sol, dv)
    else:
        t, u, w = tuw
    qm = _vbdot_nt(q, k)
    return dict(b_col=b_col, decay=decay, eg=eg, e2=e2, egl=egl, kb=kb, pm=pm, t=t, u=u, w=w,
                qm=qm, intra=qm * decay, qd=q * eg, kd=k * e2)


def _chunk_fwd_step(p, state):
    v_new = p["u"] - _vbdot(p["w"], state)
    o = _vbdot(p["qd"], state) + _vbdot(p["intra"], v_new)
    new_state = state * p["egl"] + _vbdot_tn(p["kd"], v_new)
    return o, new_state


def _chunk_bwd_step(q, k, v, p, mk, state, dso, do):
    dv_dim = GDN_DK
    v_new = p["u"] - _vbdot(p["w"], state)
    dvn = _vbdot_tn(p["intra"], do) + _vbdot(p["kd"], dso)
    dintra = _vbdot_nt(do, v_new)
    dqd = _vbdot_nt(do, state)
    ds = p["egl"] * dso + _vbdot_tn(p["qd"], do) - _vbdot_tn(p["w"], dvn)
    dkd = _vbdot_nt(v_new, dso)
    dgl = _vsum(_vsum(dso * state, axis=1, keepdims=True), axis=0, keepdims=True) * p["egl"]
    dw = -_vbdot_nt(dvn, state)
    drhs = _vhdot_tn(p["t"], _vcat(dvn, dw))
    dru, drw = _vlo(drhs, dv_dim), _vhi(drhs, dv_dim)
    da = -_vwhere(mk["ms"], _vbdot_nt(drhs, _vcat(p["u"], p["w"])), 0.0)
    b_col = p["b_col"]
    dv = dru * b_col
    dbeta = _vsum(dru * v, axis=1, keepdims=True)
    dkb = drw * p["eg"]
    deg = _vsum(drw * p["kb"], axis=1, keepdims=True)
    dp = da * p["decay"]
    ddecay = da * p["pm"]
    dkb = dkb + _vbdot(dp, k)
    dk = _vbdot_tn(dp, p["kb"])
    dqm = dintra * p["decay"]
    ddecay = ddecay + dintra * p["qm"]
    dq = _vbdot(dqm, k)
    dk = dk + _vbdot_tn(dqm, q)
    dd = ddecay * p["decay"]
    dgc_col = _vsum(dd, axis=1, keepdims=True)
    dgc_row = -_vsum(dd, axis=0, keepdims=True)
    dq = dq + dqd * p["eg"]
    deg = deg + _vsum(dqd * q, axis=1, keepdims=True)
    dk = dk + dkd * p["e2"]
    de2 = _vsum(dkd * k, axis=1, keepdims=True) * p["e2"]
    dgl = dgl + _vsum(de2, axis=0, keepdims=True)
    dgc_col = dgc_col - de2 + deg * p["eg"]
    dk = dk + dkb * b_col
    dbeta = dbeta + _vsum(dkb * k, axis=1, keepdims=True)
    dgc_col = dgc_col + _vsum(mk["eyef"] * dgc_row, axis=1, keepdims=True)
    dg_row = _vsum(mk["mif"] * dgc_col, axis=0, keepdims=True) + dgl
    dbeta_row = _vsum(mk["eyef"] * dbeta, axis=0, keepdims=True)
    return dq, dk, dv, dg_row, dbeta_row, ds


def _gdn_chunk_fwd(qkvn, g5, b5, *, name, comm=None):
    s = qkvn.shape[0]
    c = GDN_CHUNK
    nc = s // c
    h_, dk = GDN_HEADS, GDN_DK

    def body(*refs):
        n = pl.program_id(0)
        ins, outs, (st_scr,) = _comm_hooks(comm, refs, 6, 10, 1, n == 0, n == (3 * nc) // 4, n == nc - 1)
        x_refs, g_refs, b_refs = ins[0:2], ins[2:4], ins[4:6]
        o_refs, st_refs, t_refs, u_refs, w_refs = outs[0:2], outs[2:4], outs[4:6], outs[6:8], outs[8:10]

        @pl.when(n == 0)
        def _():
            st_scr[...] = jnp.zeros_like(st_scr)

        ch = [(d, h) for d in range(2) for h in range(h_)]
        mk = _both_masks(h_)
        q, k, v = (_V(x_refs[d][:, (t * h_ + h) * dk:(t * h_ + h + 1) * dk] for d, h in ch) for t in range(3))
        g, b = (_V(r[d][0, h, 0] for d, h in ch) for r in (g_refs, b_refs))
        state = _V(st_scr[d * h_ + h] for d, h in ch)
        p = _chunk_prep(q, k, v, g, b, mk)
        o, new_state = _chunk_fwd_step(p, state)
        for i, (d, h) in enumerate(ch):
            st_refs[d][h, 0] = state.xs[i]
            st_scr[d * h_ + h] = new_state.xs[i]
            o_refs[d][:, h * dk:(h + 1) * dk] = o.xs[i]
            t_refs[d][h, 0] = p["t"].xs[i].astype(BF16)
            u_refs[d][h, 0] = p["u"].xs[i]
            w_refs[d][h, 0] = p["w"].xs[i].astype(BF16)

    ce = (lambda n: n, lambda n: nc - 1 - n)
    xs = [pl.BlockSpec((c, 3 * h_ * dk), lambda n, d=d: (ce[d](n), 0)) for d in range(2)]
    gates = [pl.BlockSpec((1, h_, 1, 1, c), lambda n, d=d: (d, 0, ce[d](n), 0, 0)) for d in range(2)]
    os_ = [pl.BlockSpec((c, h_ * dk), lambda n, d=d: (ce[d](n), 0)) for d in range(2)]
    sts = [pl.BlockSpec((h_, 1, dk, dk), lambda n, d=d: (0, ce[d](n), 0, 0)) for d in range(2)]
    tcc = [pl.BlockSpec((h_, 1, c, c), lambda n, d=d: (0, ce[d](n), 0, 0)) for d in range(2)]
    tck = [pl.BlockSpec((h_, 1, c, dk), lambda n, d=d: (0, ce[d](n), 0, 0)) for d in range(2)]
    per_chunk = lambda last, dt: [jax.ShapeDtypeStruct((h_, nc, c, last), dt)] * 2
    c_in, c_out, c_shape, c_scr = _comm_specs(comm)
    res = pl.pallas_call(
        body, name=name, grid=(nc,),
        in_specs=xs + gates + gates + c_in,
        out_specs=os_ + sts + tcc + tck + tck + c_out,
        out_shape=[jax.ShapeDtypeStruct((s, h_ * dk), F32)] * 2 + [jax.ShapeDtypeStruct((h_, nc, dk, dk), F32)] * 2
        + per_chunk(c, BF16) + per_chunk(dk, F32) + per_chunk(dk, BF16) + c_shape,
        scratch_shapes=[pltpu.VMEM((2 * h_, dk, dk), F32)] + c_scr,
        compiler_params=_cparams(("arbitrary",)),
    )(qkvn, qkvn, g5, g5, b5, b5, *(comm.arrays if comm else []))
    return res[0:2], res[2:10], res[10:]


def _gdn_chunk_bwd(qkvn, g5, b5, states, do, *, name, comm=None):
    s = qkvn.shape[0]
    c = GDN_CHUNK
    nc = s // c
    h_, dk = GDN_HEADS, GDN_DK

    def body(*refs):
        i = pl.program_id(0)
        ins, outs, (ds_scr,) = _comm_hooks(comm, refs, 16, 6, 1, i == 0, i == nc // 2, i == nc - 1)
        x_refs, g_refs, b_refs, st_refs = ins[0:2], ins[2:4], ins[4:6], ins[6:8]
        t_refs, u_refs, w_refs, do_refs = ins[8:10], ins[10:12], ins[12:14], ins[14:16]
        dx_refs, dg_refs, db_refs = outs[0:2], outs[2:4], outs[4:6]

        @pl.when(i == 0)
        def _():
            ds_scr[...] = jnp.zeros_like(ds_scr)

        ch = [(d, h) for d in range(2) for h in range(h_)]
        mk = _both_masks(h_)
        q, k, v = (_V(x_refs[d][:, (t * h_ + h) * dk:(t * h_ + h + 1) * dk] for d, h in ch) for t in range(3))
        g, b = (_V(r[d][0, h, 0] for d, h in ch) for r in (g_refs, b_refs))
        state = _V(st_refs[d][h, 0] for d, h in ch)
        dso = _V(ds_scr[d * h_ + h] for d, h in ch)
        dov = _V(do_refs[d][:, h * dk:(h + 1) * dk] for d, h in ch)
        tuw = tuple(_V(r[d][h, 0] for d, h in ch) for r in (t_refs, u_refs, w_refs))
        res = _chunk_bwd_step(q, k, v, _chunk_prep(q, k, v, g, b, mk, tuw), mk, state, dso, dov)
        for (d, h), (dq, dkk, dvv, dg_r, db_r, ds) in zip(ch, zip(*[r.xs for r in res])):
            ds_scr[d * h_ + h] = ds
            dg_refs[d][h, 0] = dg_r
            db_refs[d][h, 0] = db_r
            for t, val in enumerate((dq, dkk, dvv)):
                dx_refs[d][:, (t * h_ + h) * dk:(t * h_ + h + 1) * dk] = val

    ce = (lambda i: nc - 1 - i, lambda i: i)
    both = lambda mk_spec: [mk_spec(d) for d in range(2)]
    xs = both(lambda d: pl.BlockSpec((c, 3 * h_ * dk), lambda i: (ce[d](i), 0)))
    gates = both(lambda d: pl.BlockSpec((1, h_, 1, 1, c), lambda i: (d, 0, ce[d](i), 0, 0)))
    sts = both(lambda d: pl.BlockSpec((h_, 1, dk, dk), lambda i: (0, ce[d](i), 0, 0)))
    tcc = both(lambda d: pl.BlockSpec((h_, 1, c, c), lambda i: (0, ce[d](i), 0, 0)))
    tck = both(lambda d: pl.BlockSpec((h_, 1, c, dk), lambda i: (0, ce[d](i), 0, 0)))
    dos = both(lambda d: pl.BlockSpec((c, h_ * dk), lambda i: (ce[d](i), 0)))
    gouts = both(lambda d: pl.BlockSpec((h_, 1, 1, c), lambda i: (0, ce[d](i), 0, 0)))
    c_in, c_out, c_shape, c_scr = _comm_specs(comm)
    res = pl.pallas_call(
        body, name=name, grid=(nc,),
        in_specs=xs + gates + gates + sts + tcc + tck + tck + dos + c_in,
        out_specs=xs + gouts + gouts + c_out,
        out_shape=[jax.ShapeDtypeStruct((s, 3 * h_ * dk), F32)] * 2
        + [jax.ShapeDtypeStruct((h_, nc, 1, c), F32)] * 4 + c_shape,
        scratch_shapes=[pltpu.VMEM((2 * h_, dk, dk), F32)] + c_scr,
        compiler_params=_cparams(("arbitrary",)),
    )(qkvn, qkvn, g5, g5, b5, b5, *states, do, do, *(comm.arrays if comm else []))
    return res[0:2], jnp.stack(res[2:4]), jnp.stack(res[4:6]), res[6:]


def _gdn_post_fwd(o, z, norm_w, *, name):
    s = o[0].shape[0]
    h_, dk = GDN_HEADS, GDN_DK

    def body(of_ref, ob_ref, z_ref, w_ref, a_ref):
        ov = of_ref[...] + ob_ref[...]
        zv = z_ref[...]
        r = lax.rsqrt(jnp.mean(ov * ov, axis=-1, keepdims=True) + RMS_EPS)
        a_ref[...] = (ov * r * w_ref[...] * (zv * _sigmoid(zv))).astype(a_ref.dtype)

    col = pl.BlockSpec((s, dk), lambda h: (0, h))
    return pl.pallas_call(
        body, name=name, grid=(h_,),
        in_specs=[col, col, pl.BlockSpec((s, dk), lambda h: (0, 3 * h_ + h)), pl.BlockSpec((1, dk), lambda h: (0, 0))],
        out_specs=col,
        out_shape=jax.ShapeDtypeStruct((s, h_ * dk), BF16),
        compiler_params=_cparams(("parallel",)),
    )(o[0], o[1], z, norm_w.reshape(1, dk))


def _gdn_post_bwd(o, z, norm_w, dact, *, name):
    s = o[0].shape[0]
    h_, dk = GDN_HEADS, GDN_DK

    def body(of_ref, ob_ref, z_ref, w_ref, da_ref, do_ref, dz_ref, dw_ref):
        h = pl.program_id(0)
        ov = of_ref[...] + ob_ref[...]
        zv = z_ref[...]
        wv = w_ref[...]
        dav = da_ref[...]
        r = lax.rsqrt(jnp.mean(ov * ov, axis=-1, keepdims=True) + RMS_EPS)
        nrm = ov * r
        sg = _sigmoid(zv)
        sz = zv * sg
        dn = dav * wv * sz
        do_ref[...] = r * (dn - nrm * jnp.mean(dn * nrm, axis=-1, keepdims=True))
        dz_ref[...] = (dav * nrm * wv * (sg * (1.0 + zv * (1.0 - sg)))).astype(dz_ref.dtype)
        part = jnp.sum(dav * nrm * sz, axis=0, keepdims=True)

        @pl.when(h == 0)
        def _():
            dw_ref[...] = part

        @pl.when(h > 0)
        def _():
            dw_ref[...] += part

    col = pl.BlockSpec((s, dk), lambda h: (0, h))
    vec = pl.BlockSpec((1, dk), lambda h: (0, 0))
    return pl.pallas_call(
        body, name=name, grid=(h_,),
        in_specs=[col, col, pl.BlockSpec((s, dk), lambda h: (0, 3 * h_ + h)), vec, col],
        out_specs=[col, pl.BlockSpec((s, dk), lambda h: (0, 3 * h_ + h)), vec],
        out_shape=[jax.ShapeDtypeStruct((s, h_ * dk), F32), jax.ShapeDtypeStruct((s, GDN_MAIN), BF16),
                   jax.ShapeDtypeStruct((1, dk), F32)],
        compiler_params=_cparams(("arbitrary",)),
    )(o[0], o[1], z, norm_w.reshape(1, dk), dact)


def _rel_bucket(rel):
    nb = REL_BUCKETS // 2
    max_exact = nb // 2
    ret = jnp.where(rel > 0, nb, 0)
    n = jnp.abs(rel)
    nf = jnp.maximum(n, 1).astype(F32)
    large = max_exact + (jnp.log(nf / max_exact) / math.log(REL_MAX_DIST / max_exact)
                         * (nb - max_exact)).astype(jnp.int32)
    large = jnp.minimum(large, nb - 1)
    return ret + jnp.where(n < max_exact, n, large)


def _bucket_onehot():
    half = DSWA_HALF
    outs = []
    for dil in DSWA_DILS:
        rel = (jnp.arange(3 * half)[None, :] - half - jnp.arange(half)[:, None]) * dil
        outs.append(jax.nn.one_hot(_rel_bucket(rel).reshape(-1), REL_BUCKETS, dtype=F32, axis=0))
    return jnp.stack(outs)


def _head_group_select(vals):
    rows = lax.broadcasted_iota(jnp.int32, vals[0].shape, 0)
    return jnp.where(rows < DSWA_HG, vals[0], jnp.where(rows < 2 * DSWA_HG, vals[1], vals[2]))


def _dswa_bias(table_t, onehot, *, name):
    p = onehot.shape[-1]

    def body(t_ref, oh_ref, b_ref):
        b_ref[...] = _head_group_select([_hdot(t_ref[...], oh_ref[g]) for g in range(3)])

    return pl.pallas_call(body, name=name, out_shape=jax.ShapeDtypeStruct((DSWA_HEADS, p), F32),
                          compiler_params=_cparams())(table_t, onehot)


def _dswa_dtable(dbias, onehot, *, name):
    def body(d_ref, oh_ref, t_ref):
        t_ref[...] = _head_group_select([_hdot_nt(d_ref[...], oh_ref[g]) for g in range(3)])

    return pl.pallas_call(body, name=name, out_shape=jax.ShapeDtypeStruct((DSWA_HEADS, REL_BUCKETS), F32),
                          compiler_params=_cparams())(dbias, onehot)


def _rows(start, dil):
    if dil == 1:
        return pl.ds(pl.multiple_of(start, DSWA_HALF), DSWA_HALF)
    return pl.ds(start, DSWA_HALF, stride=dil)


def _attn_blocks(it, s, dil):
    half = DSWA_HALF
    nbs = s // half // dil
    ii = lax.broadcasted_iota(jnp.int32, (half, 3 * half), 0)
    jj = lax.broadcasted_iota(jnp.int32, (half, 3 * half), 1)
    band = jnp.abs(jj - half - ii) <= half
    out = []
    for u in range(DSWA_UNROLL):
        blk = it * DSWA_UNROLL + u
        r, b = blk // nbs, blk % nbs
        own = r + dil * half * b
        prev = own - jnp.where(b > 0, dil * half, 0)
        nxt = own + jnp.where(b < nbs - 1, dil * half, 0)
        ok = band & ((jj >= half) | (b > 0)) & ((jj < 2 * half) | (b < nbs - 1))
        out.append(((prev, own, nxt), ok))
    return out


def _attn_chains(q_ref, k_ref, v_ref, blocks, dil):
    lane = lax.broadcasted_iota(jnp.int32, (DSWA_HALF, 2 * DSWA_E), 1)
    qm, kw, vw, valid, hmask = [], [], [], [], []
    for (prev, own, nxt), ok in blocks:
        q = q_ref[_rows(own, dil), :].astype(BF16)
        k = jnp.concatenate([k_ref[_rows(st, dil), :] for st in (prev, own, nxt)], axis=0).astype(BF16)
        v = jnp.concatenate([v_ref[_rows(st, dil), :] for st in (prev, own, nxt)], axis=0).astype(BF16)
        for hd in range(2):
            mine = (lane < DSWA_E) if hd == 0 else (lane >= DSWA_E)
            qm.append(jnp.where(mine, q, jnp.zeros_like(q)))
            kw.append(k)
            vw.append(v)
            valid.append(ok)
            hmask.append(mine)
    return _V(qm), _V(kw), _V(vw), _V(valid), _V(hmask)


def _per_group(pr, fn):
    for gi, dil in enumerate(DSWA_DILS):
        pl.when(pr // DSWA_PG == gi)(functools.partial(fn, dil))


_vmax, _vlog = _lift(jnp.max), _lift(jnp.log)


def _dswa_attn_fwd(qkv, bias, *, name, comm=None):
    s = qkv.shape[0]
    half, e = DSWA_HALF, DSWA_E
    npair = DSWA_HEADS // 2

    def body(*refs):
        pr = pl.program_id(0)
        (q_ref, k_ref, v_ref, bias_ref), (o_ref, lse_ref), _ = _comm_hooks(
            comm, refs, 4, 2, 0, pr == 0, pr == (3 * npair) // 4, pr == npair - 1)
        bias_v = _V([bias_ref[0], bias_ref[1]] * DSWA_UNROLL)

        def run(dil):
            def step(it, carry):
                blocks = _attn_blocks(it, s, dil)
                qm, kw, vw, valid, hmask = _attn_chains(q_ref, k_ref, v_ref, blocks, dil)
                sc = _vwhere(valid, _vbdot_nt(qm, kw) * (e ** -0.5) + bias_v, NEG_INF)
                m = _vmax(sc, axis=-1, keepdims=True)
                p = _vexp(sc - m)
                l = _vsum(p, axis=-1, keepdims=True)
                o = _vbdot(p * (1.0 / l), vw)
                lse = m + _vlog(l)
                for u, ((_, own, _), _) in enumerate(blocks):
                    is_a = hmask.xs[2 * u]
                    o_ref[_rows(own, dil), :] = jnp.where(is_a, o.xs[2 * u], o.xs[2 * u + 1])
                    lse_ref[_rows(own, dil), :] = jnp.where(is_a, lse.xs[2 * u], lse.xs[2 * u + 1])
                return carry

            lax.fori_loop(0, s // half // DSWA_UNROLL, step, 0)

        _per_group(pr, run)

    col = lambda t: pl.BlockSpec((s, 2 * e), lambda p: (0, t * npair + p))
    pair = pl.BlockSpec((s, 2 * e), lambda p: (0, p))
    c_in, c_out, c_shape, c_scr = _comm_specs(comm)
    res = pl.pallas_call(
        body, name=name, grid=(npair,),
        in_specs=[col(0), col(1), col(2), pl.BlockSpec((2, half, 3 * half), lambda p: (p, 0, 0))] + c_in,
        out_specs=[pair, pair] + c_out,
        out_shape=[jax.ShapeDtypeStruct((s, npair * 2 * e), F32)] * 2 + c_shape,
        scratch_shapes=c_scr,
        compiler_params=_cparams(("arbitrary",)),
    )(qkv, qkv, qkv, bias, *(comm.arrays if comm else []))
    return res[0], res[1], res[2:]


def _dswa_attn_bwd(qkv, bias, lse, do, corr, *, name, comm=None):
    s = qkv.shape[0]
    half, e = DSWA_HALF, DSWA_E
    npair = DSWA_HEADS // 2
    w = 2 * e

    def body(*refs):
        pr = pl.program_id(0)
        (q_ref, k_ref, v_ref, bias_ref, lse_ref, do_ref, corr_ref), (dq_ref, dk_ref, dv_ref, db_ref), _ = _comm_hooks(
            comm, refs, 7, 4, 0, pr == 0, pr == npair // 2, pr == npair - 1)
        bias_v = _V([bias_ref[0], bias_ref[1]] * DSWA_UNROLL)
        dk_ref[...] = jnp.zeros_like(dk_ref)
        dv_ref[...] = jnp.zeros_like(dv_ref)

        def run(dil):
            def step(it, dbias):
                blocks = _attn_blocks(it, s, dil)
                qm, kw, vw, valid, hmask = _attn_chains(q_ref, k_ref, v_ref, blocks, dil)
                hd = [0, 1] * DSWA_UNROLL
                rows = [_rows(own, dil) for (_, own, _), _ in blocks for _ in range(2)]
                lse_c = _V(lse_ref[rw, :][:, h * e:h * e + 1] for rw, h in zip(rows, hd))
                corr_c = _V(corr_ref[rw, :][:, h * e:h * e + 1] for rw, h in zip(rows, hd))
                dov = _vwhere(hmask, _V(do_ref[rw, :] for rw in rows), 0.0)
                sc = _vbdot_nt(qm, kw) * (e ** -0.5) + bias_v
                p = _vwhere(valid, _vexp(_vwhere(valid, sc, 0.0) - lse_c), 0.0)
                dsc = p * (_vbdot_nt(dov, vw) + corr_c)
                dq = _vbdot(dsc, kw) * (e ** -0.5)
                dkc = _vbdot_tn(dsc, qm) * (e ** -0.5)
                dvc = _vbdot_tn(p, dov)
                for u, (starts, _) in enumerate(blocks):
                    dq_ref[_rows(starts[1], dil), :] = jnp.where(hmask.xs[2 * u], dq.xs[2 * u], dq.xs[2 * u + 1])
                    dk_u = dkc.xs[2 * u] + dkc.xs[2 * u + 1]
                    dv_u = dvc.xs[2 * u] + dvc.xs[2 * u + 1]
                    for t, st in enumerate(starts):
                        dk_ref[_rows(st, dil), :] += dk_u[t * half:(t + 1) * half]
                        dv_ref[_rows(st, dil), :] += dv_u[t * half:(t + 1) * half]
                da, db = dbias
                for u in range(DSWA_UNROLL):
                    da, db = da + dsc.xs[2 * u], db + dsc.xs[2 * u + 1]
                return da, db

            zero = jnp.zeros((half, 3 * half), F32)
            da, db = lax.fori_loop(0, s // half // DSWA_UNROLL, step, (zero, zero))
            db_ref[0] = da
            db_ref[1] = db

        _per_group(pr, run)

    col = lambda t: pl.BlockSpec((s, w), lambda p: (0, t * npair + p))
    ps = pl.BlockSpec((s, w), lambda p: (0, p))
    bs = pl.BlockSpec((2, half, 3 * half), lambda p: (p, 0, 0))
    c_in, c_out, c_shape, c_scr = _comm_specs(comm)
    res = pl.pallas_call(
        body, name=name, grid=(npair,),
        in_specs=[col(0), col(1), col(2), bs, ps, ps, ps] + c_in,
        out_specs=[ps, ps, ps, bs] + c_out,
        out_shape=[jax.ShapeDtypeStruct((s, npair * w), F32)] * 3
        + [jax.ShapeDtypeStruct((DSWA_HEADS, half, 3 * half), F32)] + c_shape,
        scratch_shapes=c_scr,
        compiler_params=_cparams(("arbitrary",)),
    )(qkv, qkv, qkv, bias, lse, do, corr, *(comm.arrays if comm else []))
    return res[0], res[1], res[2], res[3], res[4:]


def _pair_cols(g, j):
    w = 2 * DSWA_E
    return slice((g * DSWA_PG + j) * w, (g * DSWA_PG + j + 1) * w)


def _group_weights(l_ref, j):
    ls = [l_ref[:, _pair_cols(g, j)] for g in range(3)]
    m = jnp.maximum(jnp.maximum(ls[0], ls[1]), ls[2])
    es = [jnp.exp(x - m) for x in ls]
    inv = 1.0 / (es[0] + es[1] + es[2])
    return [x * inv for x in es]


def _dswa_combine_fwd(o, lse, *, name):
    s, wd = o.shape
    tr = _pick(s, (512, 256, 128))

    def body(o_ref, l_ref, c_ref):
        for j in range(DSWA_PG):
            al = _group_weights(l_ref, j)
            for g in range(3):
                c_ref[:, _pair_cols(g, j)] = (o_ref[:, _pair_cols(g, j)] * al[g]).astype(c_ref.dtype)

    row = pl.BlockSpec((tr, wd), lambda i: (i, 0))
    return pl.pallas_call(
        body, name=name, grid=(s // tr,),
        in_specs=[row, row], out_specs=row,
        out_shape=jax.ShapeDtypeStruct(o.shape, BF16),
        compiler_params=_cparams(("parallel",)),
    )(o, lse)


def _dswa_combine_bwd(o, lse, dc, *, name):
    s, wd = o.shape
    tr = _pick(s, (512, 256, 128))

    def body(o_ref, l_ref, dc_ref, do_ref, corr_ref):
        lane = lax.broadcasted_iota(jnp.int32, (tr, 2 * DSWA_E), 1)
        is_a = lane < DSWA_E
        for j in range(DSWA_PG):
            al = _group_weights(l_ref, j)
            tot = jnp.zeros((tr, 2 * DSWA_E), F32)
            for g in range(3):
                cols = _pair_cols(g, j)
                dcv = dc_ref[:, cols]
                do_ref[:, cols] = dcv * al[g]
                prod = dcv * o_ref[:, cols]
                dal = jnp.where(is_a, jnp.sum(jnp.where(is_a, prod, 0.0), axis=-1, keepdims=True),
                                jnp.sum(jnp.where(is_a, 0.0, prod), axis=-1, keepdims=True))
                tot = tot + al[g] * dal
            for g in range(3):
                corr_ref[:, _pair_cols(g, j)] = -al[g] * tot

    row = pl.BlockSpec((tr, wd), lambda i: (i, 0))
    return pl.pallas_call(
        body, name=name, grid=(s // tr,),
        in_specs=[row, row, row], out_specs=[row, row],
        out_shape=[jax.ShapeDtypeStruct(o.shape, F32)] * 2,
        compiler_params=_cparams(("parallel",)),
    )(o, lse, dc)


class _Comm:
    def __init__(self, mode, arrays, kinds=None):
        self.mode, self.arrays, self.kinds = mode, list(arrays), kinds
        self.n = len(self.arrays)

    def out_shapes(self):
        if self.mode == "exchange":
            return [jax.ShapeDtypeStruct(x.shape, x.dtype) for x in self.arrays]
        shapes = []
        for x, kd in zip(self.arrays, self.kinds):
            shp = list(x.shape)
            if kd == "stack":
                shp = [N_DEV] + shp
            else:
                shp[-2 if kd == "rows" else -1] *= N_DEV
            shapes.append(jax.ShapeDtypeStruct(tuple(shp), x.dtype))
        return shapes

    def scratch(self):
        return [pltpu.SemaphoreType.DMA((7 * self.n,)), pltpu.SemaphoreType.DMA((7 * self.n,)),
                pltpu.SemaphoreType.DMA((self.n,))]

    def bind(self, in_refs, out_refs, sems):
        self.x, self.o = in_refs, out_refs
        self.send_sems, self.recv_sems, self.local_sems = sems
        self.pos = (lax.axis_index("x"), lax.axis_index("y"), lax.axis_index("c"))

    def _slot(self, i, px, py, pc):
        p = 4 * px + 2 * py + pc
        kd = self.kinds[i]
        if kd == "stack":
            return self.o[i].at[p]
        nd = len(self.x[i].shape)
        ax = nd - 2 if kd == "rows" else nd - 1
        size = self.x[i].shape[ax]
        idx = tuple(pl.ds(p * size, size) if a == ax else slice(None) for a in range(nd))
        return self.o[i].at[idx]

    def _gcopy(self, i, k, block, to, src=None):
        return pltpu.make_async_remote_copy(
            src_ref=self._slot(i, *block) if src is None else src, dst_ref=self._slot(i, *block),
            send_sem=self.send_sems.at[7 * i + k], recv_sem=self.recv_sems.at[7 * i + k],
            device_id=to, device_id_type=pl.DeviceIdType.MESH)

    def _chips(self):
        mx, my, _ = self.pos
        return [(1 - mx, my), (mx, 1 - my), (1 - mx, 1 - my)]

    def _xcopies(self):
        mx, my, mc = self.pos
        me = 4 * mx + 2 * my + mc
        copies = []
        for k in range(1, N_DEV):
            px = 1 - mx if (k >> 2) & 1 else mx
            py = 1 - my if (k >> 1) & 1 else my
            pc = 1 - mc if k & 1 else mc
            for i in range(self.n):
                copies.append(pltpu.make_async_remote_copy(
                    src_ref=self.x[i].at[4 * px + 2 * py + pc], dst_ref=self.o[i].at[me],
                    send_sem=self.send_sems.at[7 * i + k - 1], recv_sem=self.recv_sems.at[7 * i + k - 1],
                    device_id=(px, py, pc), device_id_type=pl.DeviceIdType.MESH))
        return copies

    def _local(self):
        mx, my, mc = self.pos
        if self.mode == "exchange":
            me = 4 * mx + 2 * my + mc
            return [pltpu.make_async_copy(self.x[i].at[me], self.o[i].at[me], self.local_sems.at[i]) for i in range(self.n)]
        return [pltpu.make_async_copy(self.x[i], self._slot(i, mx, my, mc), self.local_sems.at[i]) for i in range(self.n)]

    def _first(self):
        mx, my, mc = self.pos
        me, sibling = (mx, my, mc), (mx, my, 1 - mc)
        first = [self._gcopy(i, 0, me, sibling, src=self.x[i]) for i in range(self.n)]
        first += [self._gcopy(i, 1 + j, me, (*chip, mc), src=self.x[i]) for j, chip in enumerate(self._chips())
                  for i in range(self.n)]
        return first

    def _passed(self):
        mx, my, mc = self.pos
        return [self._gcopy(i, 4 + j, (*chip, mc), (mx, my, 1 - mc)) for j, chip in enumerate(self._chips())
                for i in range(self.n)]

    def start(self):
        for cp in self._local() + (self._xcopies() if self.mode == "exchange" else self._first()):
            cp.start()

    def mid(self):
        if self.mode == "exchange":
            return
        mx, my, mc = self.pos
        passed = self._passed()
        for j, chip in enumerate(self._chips()):
            for i in range(self.n):
                self._gcopy(i, 1 + j, (*chip, mc), (mx, my, mc)).wait_recv()
                passed[j * self.n + i].start()

    def end(self):
        mx, my, mc = self.pos
        if self.mode == "exchange":
            copies = self._xcopies()
            for cp in copies:
                cp.wait_recv()
            for cp in copies:
                cp.wait_send()
        else:
            for i in range(self.n):
                self._gcopy(i, 0, (mx, my, 1 - mc), (mx, my, mc)).wait_recv()
                for j, chip in enumerate(self._chips()):
                    self._gcopy(i, 4 + j, (*chip, 1 - mc), (mx, my, mc)).wait_recv()
            for cp in self._first() + self._passed():
                cp.wait_send()
        for cp in self._local():
            cp.wait()

    def run(self, *, name):
        n = self.n

        def body(*refs):
            self.bind(refs[:n], refs[n:2 * n], refs[2 * n:])
            self.start()
            self.mid()
            self.end()

        anyspec = pl.BlockSpec(memory_space=pl.ANY)
        return pl.pallas_call(body, name=name, in_specs=[anyspec] * n, out_specs=[anyspec] * n,
                              out_shape=self.out_shapes(), scratch_shapes=self.scratch())(*self.arrays)


def _comm_specs(comm):
    if comm is None:
        return [], [], [], []
    anyspec = pl.BlockSpec(memory_space=pl.ANY)
    return [anyspec] * comm.n, [anyspec] * comm.n, comm.out_shapes(), comm.scratch()


def _comm_hooks(comm, refs, n_in, n_out, n_scr, first, mid, last):
    if comm is None:
        return refs[:n_in], refs[n_in:n_in + n_out], refs[n_in + n_out:]
    c = comm.n
    ins, cin = refs[:n_in], refs[n_in:n_in + c]
    outs, cout = refs[n_in + c:n_in + c + n_out], refs[n_in + c + n_out:n_in + 2 * c + n_out]
    scr, sems = refs[n_in + 2 * c + n_out:n_in + 2 * c + n_out + n_scr], refs[n_in + 2 * c + n_out + n_scr:]
    comm.bind(cin, cout, sems)
    pl.when(first)(comm.start)
    pl.when(mid)(comm.mid)
    pl.when(last)(comm.end)
    return ins, outs, scr


def _adamw_update(g, w, m, v):
    mn = ADAM_B1 * m + (1.0 - ADAM_B1) * g
    vn = ADAM_B2 * v + (1.0 - ADAM_B2) * (g * g)
    m_hat = mn / (1.0 - ADAM_B1 ** ADAM_STEP)
    v_hat = vn / (1.0 - ADAM_B2 ** ADAM_STEP)
    return -ADAM_LR * (m_hat / (jnp.sqrt(v_hat) + ADAM_EPS) + ADAM_WD * w), mn, vn


def _adamw_layers(recvs, w, m, v, *, name):
    nl, ks, ns = w.shape
    tr = _pick(ks, (64, 48))

    def body(*refs):
        rv_refs = refs[:nl]
        w_ref, m_ref, v_ref, g_ref, d_ref, nm_ref, nv_ref = refs[nl:]
        for l in range(nl):
            g = rv_refs[l][0].astype(F32)
            for q in range(1, N_DEV):
                g = g + rv_refs[l][q].astype(F32)
            delta, mn, vn = _adamw_update(g, w_ref[l], m_ref[l], v_ref[l])
            g_ref[l] = g
            d_ref[l] = delta
            nm_ref[l] = mn
            nv_ref[l] = vn

    row = pl.BlockSpec((nl, tr, ns), lambda i: (0, i, 0))
    return pl.pallas_call(
        body, name=name, grid=(ks // tr,),
        in_specs=[pl.BlockSpec((N_DEV, tr, ns), lambda i: (0, i, 0))] * nl + [row] * 3,
        out_specs=[row] * 4,
        out_shape=[jax.ShapeDtypeStruct((nl, ks, ns), F32)] * 4,
        compiler_params=_cparams(("parallel",)),
    )(*recvs, w, m, v)


def _adamw_reduce(recv, w, m, v, *, name):
    r, c = w.shape
    tr = _pick(r, (128, 64, 8))

    def body(rv_ref, w_ref, m_ref, v_ref, g_ref, d_ref, nm_ref, nv_ref):
        g = rv_ref[0]
        for q in range(1, N_DEV):
            g = g + rv_ref[q]
        delta, mn, vn = _adamw_update(g, w_ref[...], m_ref[...], v_ref[...])
        g_ref[...] = g
        d_ref[...] = delta
        nm_ref[...] = mn
        nv_ref[...] = vn

    row = pl.BlockSpec((tr, c), lambda i: (i, 0))
    return pl.pallas_call(
        body, name=name, grid=(r // tr,),
        in_specs=[pl.BlockSpec((N_DEV, tr, c), lambda i: (0, i, 0)), row, row, row],
        out_specs=[row] * 4,
        out_shape=[jax.ShapeDtypeStruct((r, c), F32)] * 4,
        compiler_params=_cparams(("parallel",)),
    )(recv, w, m, v)


_BIG = ("gdn_w_in", "gdn_w_out", "dswa_w_in", "dswa_w_out", "mlp_w1", "mlp_w2")
_SMALL = ("gdn_conv_w", "norm_mix", "norm_mlp", "norm_final", "rel_bias", "gdn_a_log", "gdn_dt_bias", "gdn_norm_w")
_ORDER = ("norm_mix", "norm_mlp", "norm_final", "rel_bias", "gdn_w_in", "gdn_conv_w", "gdn_a_log", "gdn_dt_bias",
          "gdn_norm_w", "gdn_w_out", "dswa_w_in", "dswa_w_out", "mlp_w1", "mlp_w2")
_KIND = dict(gdn_w_in="stack", gdn_w_out="rows", dswa_w_in="stack", dswa_w_out="rows", mlp_w1="cols", mlp_w2="rows")


def _pack_rows(arrs, align):
    rows, counts = [], []
    for a in arrs:
        flat = a.reshape(-1)
        n = -(-flat.shape[0] // D_MODEL)
        flat = jnp.pad(flat, (0, n * D_MODEL - flat.shape[0]))
        rows.append(flat.reshape(n, D_MODEL))
        counts.append(n)
    out = jnp.concatenate(rows, axis=0)
    total = -(-out.shape[0] // align) * align
    return jnp.pad(out, ((0, total - out.shape[0]), (0, 0))), counts


def _unpack_rows(slab, shapes):
    outs, r = [], 0
    for shp in shapes:
        size = int(np.prod(shp))
        n = -(-size // D_MODEL)
        outs.append(slab[r:r + n].reshape(-1)[:size].reshape(shp))
        r += n
    return outs


def _col_shards(full, nshard):
    lead = full.shape[:-1]
    n = full.shape[-1] // nshard
    t = full.reshape(lead + (nshard, n))
    return jnp.moveaxis(t, -2, 0)


def _from_col_shards(g):
    t = jnp.moveaxis(g, 0, -2)
    return t.reshape(t.shape[:-2] + (t.shape[-2] * t.shape[-1],))


def kernel(x, norm_mix, norm_mlp, norm_final, rel_bias, gdn_w_in, gdn_conv_w, gdn_a_log, gdn_dt_bias, gdn_norm_w, gdn_w_out, dswa_w_in, dswa_w_out, mlp_w1, mlp_w2, loss_target, m_norm_mix, m_norm_mlp, m_norm_final, m_rel_bias, m_gdn_w_in, m_gdn_conv_w, m_gdn_a_log, m_gdn_dt_bias, m_gdn_norm_w, m_gdn_w_out, m_dswa_w_in, m_dswa_w_out, m_mlp_w1, m_mlp_w2, v_norm_mix, v_norm_mlp, v_norm_final, v_rel_bias, v_gdn_w_in, v_gdn_conv_w, v_gdn_a_log, v_gdn_dt_bias, v_gdn_norm_w, v_gdn_w_out, v_dswa_w_in, v_dswa_w_out, v_mlp_w1, v_mlp_w2):
    params = dict(norm_mix=norm_mix, norm_mlp=norm_mlp, norm_final=norm_final, rel_bias=rel_bias,
                  gdn_w_in=gdn_w_in, gdn_conv_w=gdn_conv_w, gdn_a_log=gdn_a_log, gdn_dt_bias=gdn_dt_bias,
                  gdn_norm_w=gdn_norm_w, gdn_w_out=gdn_w_out, dswa_w_in=dswa_w_in, dswa_w_out=dswa_w_out,
                  mlp_w1=mlp_w1, mlp_w2=mlp_w2)
    mom_m = dict(norm_mix=m_norm_mix, norm_mlp=m_norm_mlp, norm_final=m_norm_final, rel_bias=m_rel_bias,
                 gdn_w_in=m_gdn_w_in, gdn_conv_w=m_gdn_conv_w, gdn_a_log=m_gdn_a_log, gdn_dt_bias=m_gdn_dt_bias,
                 gdn_norm_w=m_gdn_norm_w, gdn_w_out=m_gdn_w_out, dswa_w_in=m_dswa_w_in, dswa_w_out=m_dswa_w_out,
                 mlp_w1=m_mlp_w1, mlp_w2=m_mlp_w2)
    mom_v = dict(norm_mix=v_norm_mix, norm_mlp=v_norm_mlp, norm_final=v_norm_final, rel_bias=v_rel_bias,
                 gdn_w_in=v_gdn_w_in, gdn_conv_w=v_gdn_conv_w, gdn_a_log=v_gdn_a_log, gdn_dt_bias=v_gdn_dt_bias,
                 gdn_norm_w=v_gdn_norm_w, gdn_w_out=v_gdn_w_out, dswa_w_in=v_dswa_w_in, dswa_w_out=v_dswa_w_out,
                 mlp_w1=v_mlp_w1, mlp_w2=v_mlp_w2)
    xs = x[0]
    target = loss_target[0]
    dist = _Dist(params)
    conv_tail, _ = _pack_rows([gdn_conv_w], 8)
    (conv_g,) = dist.put("start", dist.gather_comm("start", extra=[(conv_tail, "stack")]).run(name="ag_start"))
    conv_parts = [_unpack_rows(conv_g[dev], [gdn_conv_w.shape])[0] for dev in range(N_DEV)]
    conv_full = _from_col_shards(jnp.stack(conv_parts))[:, :, 0, :]

    loss_part, dcur, g_big, rep, g_conv = _local_step(
        xs, target, dict(norm_mix=norm_mix, norm_mlp=norm_mlp, norm_final=norm_final, rel_bias=rel_bias,
                         gdn_a_log=gdn_a_log, gdn_dt_bias=gdn_dt_bias, gdn_norm_w=gdn_norm_w), dist.full, conv_full, dist)
    loss = lax.psum(loss_part[0, 0], ("x", "y", "c"))
    grad_x = dcur[None]

    conv_dev = _col_shards(jnp.stack(g_conv)[:, :, None, :], N_DEV)
    small_send = jnp.stack([_pack_rows([conv_dev[dev]] + [rep[n] for n in _SMALL[1:]], 8)[0] for dev in range(N_DEV)])
    (small_recv,) = dist.got("end", dist.send_comm("end", g_big, extra=[small_send]).run(name="grad_exchange"))

    outs = {}
    for n in _BIG:
        recvs = [dist.recv[(n, l)] for l in range(params[n].shape[0])]
        res = _adamw_layers(recvs, params[n], mom_m[n], mom_v[n], name=f"adamw_{n}")
        for tag, t in zip(("grad", "delta", "new_m", "new_v"), res):
            outs[(tag, n)] = t
    w_slab, _ = _pack_rows([params[n] for n in _SMALL], 8)
    m_slab, _ = _pack_rows([mom_m[n] for n in _SMALL], 8)
    v_slab, _ = _pack_rows([mom_v[n] for n in _SMALL], 8)
    small = _adamw_reduce(small_recv, w_slab, m_slab, v_slab, name="adamw_small")
    shapes = [params[n].shape for n in _SMALL]
    for tag, slab in zip(("grad", "delta", "new_m", "new_v"), small):
        for n, t in zip(_SMALL, _unpack_rows(slab, shapes)):
            outs[(tag, n)] = t
    result = [loss, grad_x]
    for tag in ("grad", "delta", "new_m", "new_v"):
        result += [outs[(tag, n)] for n in _ORDER]
    return tuple(result)


_GATHER = {
    "start": (("gdn_w_in", 0),),
    "gdn_proj0": (("gdn_w_out", 0), ("mlp_w1", 0)),
    "chunk_fwd0": (("mlp_w2", 0), ("dswa_w_in", 0), ("dswa_w_out", 0), ("mlp_w1", 1), ("gdn_w_out", 1)),
    "mlp_up0": (("mlp_w2", 1),),
    "mlp_down0": (("gdn_w_in", 1),),
    "attn_fwd1": (("mlp_w1", 2),),
    "mlp_up1": (("mlp_w2", 2),),
    "chunk_fwd2": (("dswa_w_in", 1), ("dswa_w_out", 1), ("mlp_w1", 3), ("mlp_w2", 3)),
}
_SEND = {
    "attn_bwd3": (("mlp_w1", 3), ("mlp_w2", 3)),
    "chunk_bwd2": (("dswa_w_in", 1), ("dswa_w_out", 1), ("mlp_w2", 2)),
    "pre_bwd2": (("mlp_w1", 2),),
    "gdn_proj_bwd2": (("gdn_w_in", 1),),
    "attn_bwd1": (("mlp_w1", 1), ("mlp_w2", 1)),
    "chunk_bwd0": (("gdn_w_out", 1), ("dswa_w_in", 0), ("dswa_w_out", 0), ("mlp_w2", 0)),
    "pre_bwd0": (("mlp_w1", 0), ("gdn_w_out", 0)),
    "gdn_proj_bwd0": (("gdn_w_in", 0),),
    "end": (),
}


class _Dist:
    def __init__(self, params):
        self.shards = {n: params[n].astype(BF16) for n in _BIG}
        self.full = {n: [None] * params[n].shape[0] for n in _BIG}
        self.recv = {}

    def gather_comm(self, tag, extra=()):
        if tag not in _GATHER:
            return None
        arrays = [self.shards[n][l] for n, l in _GATHER[tag]] + [a for a, _ in extra]
        return _Comm("gather", arrays, [_KIND[n] for n, _ in _GATHER[tag]] + [k for _, k in extra])

    def put(self, tag, outs):
        for (n, l), t in zip(_GATHER.get(tag, ()), outs):
            self.full[n][l] = _from_col_shards(t) if _KIND[n] == "stack" else t
        return outs[len(_GATHER.get(tag, ())):]

    def send_comm(self, tag, g_big, extra=()):
        if tag not in _SEND:
            return None
        arrays = [_col_shards(g_big[n][l], N_DEV) if _KIND[n] == "stack" else g_big[n][l] for n, l in _SEND[tag]]
        return _Comm("exchange", arrays + list(extra))

    def got(self, tag, outs):
        for item, t in zip(_SEND.get(tag, ()), outs):
            self.recv[item] = t
        return outs[len(_SEND.get(tag, ())):]


def _mm_gather(dist, tag, *args, **kw):
    comm = dist and dist.gather_comm(tag)
    if not comm:
        return _mm(*args, **kw)
    res, got = _mm(*args, comm=comm, **kw)
    dist.put(tag, got)
    return res


def _ep_residual_norm(acc, res, g):
    x = acc + res
    r = lax.rsqrt(jnp.mean(x * x, axis=-1, keepdims=True) + RMS_EPS)
    return x, x * r * g


def _ep_rms_bwd(dh, x, dres, g):
    r = lax.rsqrt(jnp.mean(x * x, axis=-1, keepdims=True) + RMS_EPS)
    xn = x * r
    dn = dh * g
    dx = dres + r * (dn - xn * jnp.mean(dn * xn, axis=-1, keepdims=True))
    return dx, dx, jnp.sum(dh * xn, axis=0, keepdims=True)


def _local_step(xs, target, sp, full, conv_full, dist=None):
    s = xs.shape[0]
    norm_mix, norm_mlp, norm_final = sp["norm_mix"], sp["norm_mlp"], sp["norm_final"]
    gdn_a_log, gdn_dt_bias, gdn_norm_w = sp["gdn_a_log"], sp["gdn_dt_bias"], sp["gdn_norm_w"]
    onehot = _bucket_onehot()
    table_t = sp["rel_bias"].T
    bias = _dswa_bias(table_t, onehot, name="dswa_bias").reshape(DSWA_HEADS, DSWA_HALF, 3 * DSWA_HALF)

    saved = []
    cur = xs
    row = lambda v: v.reshape(1, -1)
    h = _rms_fwd(cur, norm_mix[0], name="rms_mix_fwd0")
    for i in range(DEPTH):
        j = i // 2
        sv = dict(x_in=cur, h=h)
        if i % 2 == 0:
            w_in = full["gdn_w_in"][j]
            proj = _mm_gather(dist, f"gdn_proj{i}", h, w_in, b_cols=(0, GDN_MAIN), name=f"gdn_proj{i}")
            ab = _mm(h, w_in[:, GDN_MAIN:], name=f"gdn_proj_ab{i}")
            qkvn = _gdn_pre_fwd(proj, conv_full[j], name=f"gdn_pre_fwd{i}")
            g_all, beta_all = _gdn_gate_fwd(ab[:, :2 * GDN_HEADS], ab[:, 2 * GDN_HEADS:], gdn_a_log[j], gdn_dt_bias[j],
                                            name=f"gdn_gate_fwd{i}")
            gshape = (2, GDN_HEADS, s // GDN_CHUNK, 1, GDN_CHUNK)
            g_row = g_all.T.reshape(gshape)
            b_row = beta_all.T.reshape(gshape)
            o, states, got = _gdn_chunk_fwd(qkvn, g_row, b_row, name=f"gdn_chunk_fwd{i}",
                                            comm=dist and dist.gather_comm(f"chunk_fwd{i}"))
            if dist:
                dist.put(f"chunk_fwd{i}", got)
            act = _gdn_post_fwd(o, proj, gdn_norm_w[j], name=f"gdn_post_fwd{i}")
            sv.update(proj=proj, ab=ab, qkvn=qkvn, g_row=g_row, b_row=b_row, o=o, states=states, act=act)
            w_out = full["gdn_w_out"][j]
        else:
            w_in = full["dswa_w_in"][j]
            qkv = _mm(h, w_in, name=f"dswa_proj{i}")
            o_n, lse_n, got = _dswa_attn_fwd(qkv, bias, name=f"dswa_attn_fwd{i}",
                                             comm=dist and dist.gather_comm(f"attn_fwd{i}"))
            if dist:
                dist.put(f"attn_fwd{i}", got)
            act = _dswa_combine_fwd(o_n, lse_n, name=f"dswa_comb_fwd{i}")
            sv.update(qkv=qkv, o_n=o_n, lse_n=lse_n, act=act)
            w_out = full["dswa_w_out"][j]
        cur, h2 = _mm(act, w_out, name=f"mix_out{i}", out_dtypes=(F32, BF16), epilogue=_ep_residual_norm,
                      extras=(cur,), vecs=(row(norm_mlp[i]),))
        sv["x_mid"] = cur
        u, a = _mm_gather(dist, f"mlp_up{i}", h2, full["mlp_w1"][i], name=f"mlp_up{i}", out_dtypes=(BF16, BF16),
                          epilogue=lambda acc: (acc, jnp.square(jnp.maximum(acc, 0.0))))
        if i + 1 < DEPTH:
            cur, h = _mm_gather(dist, f"mlp_down{i}", a, full["mlp_w2"][i], name=f"mlp_down{i}", out_dtypes=(F32, BF16),
                                epilogue=_ep_residual_norm, extras=(cur,), vecs=(row(norm_mix[i + 1]),))
        else:
            cur = _mm_gather(dist, f"mlp_down{i}", a, full["mlp_w2"][i], name=f"mlp_down{i}",
                             epilogue=lambda acc, r: (acc + r,), extras=(cur,))
        sv.update(h2=h2, u=u, a=a)
        saved.append(sv)

    loss_part, dcur, dcur_b, dg_final = _loss_head(cur, norm_final, target, name="loss_head")

    g_norm_mix, g_norm_mlp = [None] * DEPTH, [None] * DEPTH
    g_big = {n: [None] * len(full[n]) for n in _BIG}
    g_conv, g_alog, g_dt, g_nw = [None] * 2, [None] * 2, [None] * 2, [None] * 2
    d_table_t = jnp.zeros((DSWA_HEADS, REL_BUCKETS), F32)
    for i in reversed(range(DEPTH)):
        j = i // 2
        sv = saved[i]
        w1, w2 = full["mlp_w1"][i], full["mlp_w2"][i]
        du = _mm(dcur_b, w2, tb=True, name=f"mlp_down_bwd{i}", out_dtypes=(BF16,),
                 epilogue=lambda acc, uu: (acc * (2.0 * jnp.maximum(uu.astype(F32), 0.0)),), extras=(sv["u"],))
        g_big["mlp_w2"][i] = _mm(sv["a"], dcur_b, ta=True, name=f"mlp_w2_grad{i}", out_dtypes=(BF16,), shard="rows")
        g_big["mlp_w1"][i] = _mm(sv["h2"], du, ta=True, name=f"mlp_w1_grad{i}", out_dtypes=(BF16,), shard="cols")
        dmid, dmid_b, g_norm_mlp[i] = _mm(du, w1, tb=True, name=f"mlp_up_bwd{i}", out_dtypes=(F32, BF16),
                                          epilogue=_ep_rms_bwd, extras=(sv["x_mid"], dcur), vecs=(row(norm_mlp[i]),),
                                          vec_out=True)
        if i % 2 == 0:
            w_in, w_out = full["gdn_w_in"][j], full["gdn_w_out"][j]
            dact = _mm(dmid_b, w_out, tb=True, name=f"mix_out_bwd{i}")
            g_big["gdn_w_out"][j] = _mm(sv["act"], dmid_b, ta=True, name=f"mix_out_grad{i}", out_dtypes=(BF16,),
                                        shard="rows")
            do, dz, g_nw[j] = _gdn_post_bwd(sv["o"], sv["proj"], gdn_norm_w[j], dact, name=f"gdn_post_bwd{i}")
            dqkvn, dg_row, db_row, got = _gdn_chunk_bwd(sv["qkvn"], sv["g_row"], sv["b_row"], sv["states"], do,
                                                        name=f"gdn_chunk_bwd{i}",
                                                        comm=dist and dist.send_comm(f"chunk_bwd{i}", g_big))
            if dist:
                dist.got(f"chunk_bwd{i}", got)
            dproj, g_conv[j], got = _gdn_pre_bwd(sv["proj"], conv_full[j], dqkvn, dz, name=f"gdn_pre_bwd{i}",
                                                 comm=dist and dist.send_comm(f"pre_bwd{i}", g_big))
            if dist:
                dist.got(f"pre_bwd{i}", got)
            nh2 = 2 * GDN_HEADS
            da_, db_, g_alog[j], g_dt[j] = _gdn_gate_bwd(sv["ab"][:, :nh2], sv["ab"][:, nh2:], gdn_a_log[j], gdn_dt_bias[j],
                                                         dg_row.reshape(nh2, s).T, db_row.reshape(nh2, s).T,
                                                         name=f"gdn_gate_bwd{i}")
            dab = jnp.concatenate([da_, db_], axis=1)
            gw_main = _mm(sv["h"], dproj, ta=True, name=f"gdn_w_in_grad{i}", out_dtypes=(BF16,))
            gw_ab = _mm(sv["h"], dab, ta=True, name=f"gdn_w_ab_grad{i}", out_dtypes=(BF16,))
            g_big["gdn_w_in"][j] = jnp.concatenate([gw_main, gw_ab], axis=1)
            dh_ab = _mm(dab, w_in[:, GDN_MAIN:], tb=True, name=f"gdn_proj_ab_bwd{i}")
            comm = dist and dist.send_comm(f"gdn_proj_bwd{i}", g_big)
            res = _mm(dproj, w_in, b_cols=(0, GDN_MAIN), tb=True, name=f"gdn_proj_bwd{i}", out_dtypes=(F32, BF16), tm=512,
                      epilogue=lambda acc, r, x, dres, g: _ep_rms_bwd(acc + r, x, dres, g),
                      extras=(dh_ab, sv["x_in"], dmid), vecs=(row(norm_mix[i]),), vec_out=True, comm=comm)
            if comm:
                res, got = res
                dist.got(f"gdn_proj_bwd{i}", got)
            dcur, dcur_b, g_norm_mix[i] = res
        else:
            w_in, w_out = full["dswa_w_in"][j], full["dswa_w_out"][j]
            dact = _mm(dmid_b, w_out, tb=True, name=f"mix_out_bwd{i}")
            g_big["dswa_w_out"][j] = _mm(sv["act"], dmid_b, ta=True, name=f"mix_out_grad{i}", out_dtypes=(BF16,),
                                         shard="rows")
            do_n, corr_n = _dswa_combine_bwd(sv["o_n"], sv["lse_n"], dact, name=f"dswa_comb_bwd{i}")
            *dqkv, dbias, got = _dswa_attn_bwd(sv["qkv"], bias, sv["lse_n"], do_n, corr_n, name=f"dswa_attn_bwd{i}",
                                               comm=dist and dist.send_comm(f"attn_bwd{i}", g_big))
            if dist:
                dist.got(f"attn_bwd{i}", got)
            d_table_t = d_table_t + _dswa_dtable(dbias.reshape(DSWA_HEADS, -1), onehot, name=f"dswa_dtable{i}")
            g_big["dswa_w_in"][j] = jnp.concatenate(
                [_mm(sv["h"], dt, ta=True, name=f"dswa_w_in_grad{i}_{t}", out_dtypes=(BF16,)) for t, dt in enumerate(dqkv)],
                axis=1)
            cols = [(t * DSWA_WIDTH, DSWA_WIDTH) for t in range(3)]
            dh = _mm(dqkv[0], w_in, b_cols=cols[0], tb=True, name=f"dswa_proj_bwd{i}_0")
            dh = _mm(dqkv[1], w_in, b_cols=cols[1], tb=True, name=f"dswa_proj_bwd{i}_1",
                     epilogue=lambda acc, r: (acc + r,), extras=(dh,))
            dcur, dcur_b, g_norm_mix[i] = _mm(
                dqkv[2], w_in, b_cols=cols[2], tb=True, name=f"dswa_proj_bwd{i}_2", out_dtypes=(F32, BF16), tm=512,
                epilogue=lambda acc, r, x, dres, g: _ep_rms_bwd(acc + r, x, dres, g),
                extras=(dh, sv["x_in"], dmid), vecs=(row(norm_mix[i]),), vec_out=True)

    rep = dict(norm_mix=jnp.concatenate(g_norm_mix, axis=0), norm_mlp=jnp.concatenate(g_norm_mlp, axis=0),
               norm_final=dg_final.reshape(-1), rel_bias=d_table_t.T,
               gdn_a_log=jnp.stack(g_alog).reshape(gdn_a_log.shape), gdn_dt_bias=jnp.stack(g_dt).reshape(gdn_dt_bias.shape),
               gdn_norm_w=jnp.stack(g_nw).reshape(gdn_norm_w.shape))
    return loss_part, dcur, g_big, rep, g_conv
```

```python
import functools
import math

import jax
import jax.numpy as jnp
import numpy as np
from jax import lax
from jax.experimental import pallas as pl
from jax.experimental.pallas import tpu as pltpu

F32 = jnp.float32
BF16 = jnp.bfloat16
HP = lax.Precision.HIGHEST

N_DEV = 8
D_MODEL = 1024
DEPTH = 4
RMS_EPS = 1e-6
NEG_INF = -1e30

GDN_HEADS = 8
GDN_DK = 128
GDN_CONV = 5
GDN_CHUNK = 128
GDN_QKV = 3 * GDN_HEADS * GDN_DK
GDN_MAIN = GDN_QKV + GDN_HEADS * GDN_DK
GDN_AB = 4 * GDN_HEADS

DSWA_DILS = (1, 4, 16)
DSWA_HG = 6
DSWA_E = 64
DSWA_HEADS = 18
DSWA_WIDTH = DSWA_HEADS * DSWA_E
DSWA_HALF = 64
DSWA_PG = DSWA_HG // 2
DSWA_UNROLL = 8
REL_BUCKETS = 32
REL_MAX_DIST = 1024

ADAM_LR = 0.001
ADAM_B1 = 0.9
ADAM_B2 = 0.999
ADAM_EPS = 1e-08
ADAM_WD = 0.01
ADAM_STEP = 10

VMEM_LIMIT = 56 * 1024 * 1024


def _cparams(sem=None, **kw):
    return pltpu.CompilerParams(dimension_semantics=sem, vmem_limit_bytes=VMEM_LIMIT, **kw)


def _pick(dim, cands):
    for c in cands:
        if dim % c == 0:
            return c
    return dim


def _bdot(a, b):
    return jnp.dot(a.astype(BF16), b.astype(BF16), preferred_element_type=F32)


def _bdot_nt(a, b):
    return lax.dot_general(a.astype(BF16), b.astype(BF16), (((1,), (1,)), ((), ())),
                           preferred_element_type=F32)


def _bdot_tn(a, b):
    return lax.dot_general(a.astype(BF16), b.astype(BF16), (((0,), (0,)), ((), ())),
                           preferred_element_type=F32)


def _hdot(a, b):
    return jnp.dot(a, b, precision=HP, preferred_element_type=F32)


def _hdot_nt(a, b):
    return lax.dot_general(a, b, (((1,), (1,)), ((), ())), precision=HP, preferred_element_type=F32)


def _sigmoid(x):
    return 1.0 / (1.0 + jnp.exp(-x))


def _mm(a, b, *, name, ta=False, tb=False, out_dtypes=(F32,), epilogue=None, extras=(), vecs=(), vec_out=False,
        tm=None, tn=None, tk=None, shard=None, comm=None, b_cols=None):
    if ta:
        kdim, m = a.shape
    else:
        m, kdim = a.shape
    b0, bsz = b_cols or (0, b.shape[1])
    n = b.shape[0] if tb else bsz
    assert not tb or kdim == bsz
    if shard == "rows":
        tm = m // N_DEV if (m // N_DEV) % 128 == 0 else m
    if shard == "cols":
        tn = n // N_DEV
    tm = tm or _pick(m, (1024, 1152, 512, 384, 256, 128))
    tn = tn or _pick(n, (1024, 1152, 512, 384, 256, 128))
    tk = tk or _pick(kdim, ((2048,) if ta else ()) + (1024, 1152, 512, 384, 256, 128))
    nk = kdim // tk
    n_out = len(out_dtypes) + (1 if vec_out else 0)
    n_ex = len(extras) + len(vecs)
    rows_all = shard == "rows" and tm == m

    gi, gj = m // tm, n // tn

    def body(*refs):
        i, j, k = pl.program_id(0), pl.program_id(1), pl.program_id(2)
        inner = (j == 0) & (k == 0)
        ins, out_refs, (acc_ref,) = _comm_hooks(
            comm, refs, 2 + n_ex, n_out, 1, (i == 0) & inner, (i == (3 * gi) // 4) & inner,
            (i == gi - 1) & (j == gj - 1) & (k == nk - 1))
        a_ref, b_ref, ex_refs = ins[0], ins[1], ins[2:]
        if vec_out:
            out_refs, vec_ref = out_refs[:-1], out_refs[-1]

        @pl.when(k == 0)
        def _():
            acc_ref[...] = jnp.zeros_like(acc_ref)

        av = a_ref[...].astype(BF16)
        bv = b_ref[...].astype(BF16)
        dims = (((0 if ta else 1,), (1 if tb else 0,)), ((), ()))
        acc_ref[...] += lax.dot_general(av, bv, dims, preferred_element_type=F32)

        @pl.when(k == nk - 1)
        def _():
            acc = acc_ref[...]
            outs = (acc,) if epilogue is None else epilogue(acc, *[r[...] for r in ex_refs])
            if vec_out:
                part = outs[-1]

                @pl.when(i == 0)
                def _():
                    vec_ref[...] = part

                @pl.when(i > 0)
                def _():
                    vec_ref[...] += part
            for r, o in zip(out_refs, outs):
                if rows_all:
                    for p in range(N_DEV):
                        r[p] = o[p * (m // N_DEV):(p + 1) * (m // N_DEV)].astype(r.dtype)
                else:
                    r[...] = o.astype(r.dtype)

    a_spec = pl.BlockSpec((tk, tm), lambda i, j, k: (k, i)) if ta else pl.BlockSpec((tm, tk), lambda i, j, k: (i, k))
    assert b0 % (tk if tb else tn) == 0
    boff = b0 // (tk if tb else tn)
    b_spec = (pl.BlockSpec((tn, tk), lambda i, j, k: (j, k + boff)) if tb
              else pl.BlockSpec((tk, tn), lambda i, j, k: (k, j + boff)))
    o_spec = pl.BlockSpec((tm, tn), lambda i, j, k: (i, j))
    v_spec = pl.BlockSpec((1, tn), lambda i, j, k: (0, j))
    out_specs = [o_spec] * len(out_dtypes) + ([v_spec] if vec_out else [])
    out_shape = [jax.ShapeDtypeStruct((m, n), dt) for dt in out_dtypes]
    out_shape += [jax.ShapeDtypeStruct((1, n), F32)] if vec_out else []
    if shard == "rows":
        out_shape = [jax.ShapeDtypeStruct((N_DEV, m // N_DEV, n), out_dtypes[0])]
        out_specs = [pl.BlockSpec((N_DEV, m // N_DEV, tn), lambda i, j, k: (0, 0, j)) if rows_all
                     else pl.BlockSpec((None, tm, tn), lambda i, j, k: (i, 0, j))]
    if shard == "cols":
        out_shape = [jax.ShapeDtypeStruct((N_DEV, m, tn), out_dtypes[0])]
        out_specs = [pl.BlockSpec((None, tm, tn), lambda i, j, k: (j, i, 0))]
    c_in, c_out, c_shape, c_scr = _comm_specs(comm)
    outs = pl.pallas_call(
        body, name=name,
        grid=(gi, gj, nk),
        in_specs=[a_spec, b_spec] + [o_spec] * len(extras) + [v_spec] * len(vecs) + c_in,
        out_specs=out_specs + c_out,
        out_shape=out_shape + c_shape,
        scratch_shapes=[pltpu.VMEM((tm, tn), F32)] + c_scr,
        compiler_params=_cparams(("arbitrary",) * 3 if comm or vec_out else ("parallel", "parallel", "arbitrary")),
    )(a, b, *extras, *vecs, *(comm.arrays if comm else []))
    res = outs[0] if n_out == 1 else tuple(outs[:n_out])
    return (res, outs[n_out:]) if comm else res


def _rms_fwd(x, g, *, name):
    s, d = x.shape
    tr = _pick(s, (512, 256, 128))

    def body(x_ref, g_ref, h_ref):
        xv = x_ref[...]
        r = lax.rsqrt(jnp.mean(xv * xv, axis=-1, keepdims=True) + RMS_EPS)
        h_ref[...] = (xv * r * g_ref[...]).astype(h_ref.dtype)

    return pl.pallas_call(
        body, name=name, grid=(s // tr,),
        in_specs=[pl.BlockSpec((tr, d), lambda i: (i, 0)), pl.BlockSpec((1, d), lambda i: (0, 0))],
        out_specs=pl.BlockSpec((tr, d), lambda i: (i, 0)),
        out_shape=jax.ShapeDtypeStruct((s, d), BF16),
        compiler_params=_cparams(("parallel",)),
    )(x, g.reshape(1, d))


def _loss_head(x, g, target, *, name):
    s, d = x.shape
    tr = _pick(s, (512, 256, 128))

    def body(x_ref, g_ref, t_ref, loss_ref, dx_ref, dxb_ref, dg_ref):
        i = pl.program_id(0)
        xv = x_ref[...]
        gv = g_ref[...]
        r = lax.rsqrt(jnp.mean(xv * xv, axis=-1, keepdims=True) + RMS_EPS)
        xn = xv * r
        err = xn * gv - t_ref[...]
        lpart = 0.5 * jnp.sum(jnp.mean(err * err, axis=-1, keepdims=True), axis=0, keepdims=True)
        dy = err * (1.0 / d)
        dn = dy * gv
        dx = r * (dn - xn * jnp.mean(dn * xn, axis=-1, keepdims=True))
        dx_ref[...] = dx
        dxb_ref[...] = dx.astype(dxb_ref.dtype)
        gpart = jnp.sum(dy * xn, axis=0, keepdims=True)

        @pl.when(i == 0)
        def _():
            dg_ref[...] = gpart
            loss_ref[...] = lpart

        @pl.when(i > 0)
        def _():
            dg_ref[...] += gpart
            loss_ref[...] += lpart

    row = pl.BlockSpec((tr, d), lambda i: (i, 0))
    vec = pl.BlockSpec((1, d), lambda i: (0, 0))
    one = pl.BlockSpec((1, 1), lambda i: (0, 0))
    return pl.pallas_call(
        body, name=name, grid=(s // tr,),
        in_specs=[row, vec, row], out_specs=[one, row, row, vec],
        out_shape=[jax.ShapeDtypeStruct((1, 1), F32), jax.ShapeDtypeStruct((s, d), F32),
                   jax.ShapeDtypeStruct((s, d), BF16), jax.ShapeDtypeStruct((1, d), F32)],
        compiler_params=_cparams(("arbitrary",)),
    )(x, g.reshape(1, d), target)


def _shift_rows(x, sft, rows):
    s = x.shape[0]
    if sft == 0:
        return x
    y = pltpu.roll(x, (-sft) % s, 0)
    edge = slice(0, 8) if sft < 0 else slice(s - 8, s)
    ok = (rows[edge] + sft >= 0) & (rows[edge] + sft < s)
    fixed = jnp.where(ok, y[edge], 0.0)
    return jnp.concatenate([fixed, y[8:]] if sft < 0 else [y[:s - 8], fixed], axis=0)


def _gdn_pre_fwd(proj, conv_w, *, name):
    s = proj.shape[0]
    nblk = GDN_QKV // 128
    pad = GDN_CONV // 2

    def body(x_ref, w_ref, o_ref):
        j = pl.program_id(0)
        x = x_ref[...]
        rows = lax.broadcasted_iota(jnp.int32, x.shape, 0)
        c = jnp.zeros_like(x)
        for t in range(GDN_CONV):
            c = c + w_ref[pl.ds(t, 1), :] * _shift_rows(x, t - pad, rows)
        a = c * _sigmoid(c)
        rinv = lax.rsqrt(jnp.sum(a * a, axis=-1, keepdims=True) + 1e-6)
        scale = jnp.where(j < GDN_HEADS, GDN_DK ** -0.5, 1.0)
        o_ref[...] = jnp.where(j >= 2 * GDN_HEADS, a, a * (rinv * scale))

    return pl.pallas_call(
        body, name=name, grid=(nblk,),
        in_specs=[pl.BlockSpec((s, 128), lambda j: (0, j)), pl.BlockSpec((GDN_CONV, 128), lambda j: (0, j))],
        out_specs=pl.BlockSpec((s, 128), lambda j: (0, j)),
        out_shape=jax.ShapeDtypeStruct((s, GDN_QKV), F32),
        compiler_params=_cparams(("parallel",)),
    )(proj, conv_w)


def _gdn_pre_bwd(proj, conv_w, dqkv, dproj, *, name, comm=None):
    s = proj.shape[0]
    nblk = GDN_QKV // 128
    pad = GDN_CONV // 2

    def body(*refs):
        j = pl.program_id(0)
        (x_ref, w_ref, df_ref, dbk_ref, _), (dx_ref, dw_ref), _ = _comm_hooks(
            comm, refs, 5, 2, 0, j == 0, j == nblk // 2, j == nblk - 1)
        x = x_ref[...]
        rows = lax.broadcasted_iota(jnp.int32, x.shape, 0)
        xs = [_shift_rows(x, t - pad, rows) for t in range(GDN_CONV)]
        c = jnp.zeros_like(x)
        for t in range(GDN_CONV):
            c = c + w_ref[pl.ds(t, 1), :] * xs[t]
        sg = _sigmoid(c)
        a = c * sg
        rinv = lax.rsqrt(jnp.sum(a * a, axis=-1, keepdims=True) + 1e-6)
        scale = jnp.where(j < GDN_HEADS, GDN_DK ** -0.5, 1.0)
        dy = df_ref[...] + dbk_ref[...]
        nh = a * rinv
        da_n = (rinv * scale) * (dy - nh * jnp.sum(dy * nh, axis=-1, keepdims=True))
        da = jnp.where(j >= 2 * GDN_HEADS, dy, da_n)
        dc = da * (sg * (1.0 + c * (1.0 - sg)))
        dx = jnp.zeros_like(x)
        for t in range(GDN_CONV):
            dx = dx + w_ref[pl.ds(t, 1), :] * _shift_rows(dc, pad - t, rows)
            dw_ref[pl.ds(t, 1), :] = jnp.sum(dc * xs[t], axis=0, keepdims=True)
        dx_ref[...] = dx.astype(dx_ref.dtype)

    col = pl.BlockSpec((s, 128), lambda j: (0, j))
    wsp = pl.BlockSpec((GDN_CONV, 128), lambda j: (0, j))
    c_in, c_out, c_shape, c_scr = _comm_specs(comm)
    res = pl.pallas_call(
        body, name=name, grid=(nblk,),
        in_specs=[col, wsp, col, col, pl.BlockSpec(memory_space=pl.ANY)] + c_in, out_specs=[col, wsp] + c_out,
        out_shape=[jax.ShapeDtypeStruct(dproj.shape, BF16), jax.ShapeDtypeStruct((GDN_CONV, GDN_QKV), F32)] + c_shape,
        input_output_aliases={4: 0},
        scratch_shapes=c_scr,
        compiler_params=_cparams(("arbitrary",) if comm else ("parallel",)),
    )(proj, conv_w, dqkv[0], dqkv[1], dproj, *(comm.arrays if comm else []))
    return res[0], res[1], res[2:]


def _softplus(x):
    return jnp.maximum(x, 0.0) + jnp.log(1.0 + jnp.exp(-jnp.abs(x)))


def _gdn_gate_fwd(a, b, a_log, dt_bias, *, name):
    s = a.shape[0]
    nh = 2 * GDN_HEADS

    def body(a_ref, b_ref, al_ref, dt_ref, g_ref, be_ref):
        g_ref[...] = -jnp.exp(al_ref[...]) * _softplus(a_ref[...] + dt_ref[...])
        be_ref[...] = _sigmoid(b_ref[...])

    return pl.pallas_call(
        body, name=name,
        out_shape=[jax.ShapeDtypeStruct((s, nh), F32), jax.ShapeDtypeStruct((s, nh), F32)],
        compiler_params=_cparams(),
    )(a, b, a_log.reshape(1, nh), dt_bias.reshape(1, nh))


def _gdn_gate_bwd(a, b, a_log, dt_bias, dg, dbeta, *, name):
    s = a.shape[0]
    nh = 2 * GDN_HEADS

    def body(a_ref, b_ref, al_ref, dt_ref, dg_ref, db_ref, da_ref, dbb_ref, dal_ref, ddt_ref):
        ea = jnp.exp(al_ref[...])
        z = a_ref[...] + dt_ref[...]
        dgv = dg_ref[...]
        dz = dgv * (-ea) * _sigmoid(z)
        dal_ref[...] = jnp.sum(dgv * (-ea) * _softplus(z), axis=0, keepdims=True)
        ddt_ref[...] = jnp.sum(dz, axis=0, keepdims=True)
        sb = _sigmoid(b_ref[...])
        da_ref[...] = dz
        dbb_ref[...] = db_ref[...] * sb * (1.0 - sb)

    return pl.pallas_call(
        body, name=name,
        out_shape=[jax.ShapeDtypeStruct((s, nh), F32), jax.ShapeDtypeStruct((s, nh), F32),
                   jax.ShapeDtypeStruct((1, nh), F32), jax.ShapeDtypeStruct((1, nh), F32)],
        compiler_params=_cparams(),
    )(a, b, a_log.reshape(1, nh), dt_bias.reshape(1, nh), dg, dbeta)


def _chunk_masks(d):
    c = GDN_CHUNK
    ii = lax.broadcasted_iota(jnp.int32, (c, c), 0)
    jj = lax.broadcasted_iota(jnp.int32, (c, c), 1)
    dif = (ii - jj) * (1 - 2 * d)
    mi = dif >= 0
    mit = dif <= 0
    ms = dif > 0
    eye = ii == jj
    bds = [(ii >> sh) == (jj >> sh) for sh in range(3, c.bit_length() - 1)]
    return dict(mi=mi, mit=mit, ms=ms, eye=eye, bds=bds,
                mif=mi.astype(F32), mitf=mit.astype(F32), eyef=eye.astype(F32))


class _V:
    def __init__(self, xs):
        self.xs = tuple(xs)

    def __add__(self, o):
        return _lift(lambda a, b: a + b)(self, o)

    def __radd__(self, o):
        return _lift(lambda a, b: b + a)(self, o)

    def __sub__(self, o):
        return _lift(lambda a, b: a - b)(self, o)

    def __rsub__(self, o):
        return _lift(lambda a, b: b - a)(self, o)

    def __mul__(self, o):
        return _lift(lambda a, b: a * b)(self, o)

    def __rmul__(self, o):
        return _lift(lambda a, b: b * a)(self, o)

    def __and__(self, o):
        return _lift(lambda a, b: a & b)(self, o)

    def __neg__(self):
        return _lift(lambda a: -a)(self)

    def __rtruediv__(self, o):
        return _lift(lambda a, b: b / a)(self, o)


def _lift(f):
    def g(*args, **kw):
        n = next(len(a.xs) for a in args if isinstance(a, _V))
        return _V(f(*[a.xs[i] if isinstance(a, _V) else a for a in args], **kw) for i in range(n))
    return g


_vwhere, _vsum, _vexp, _vnot = _lift(jnp.where), _lift(jnp.sum), _lift(jnp.exp), _lift(jnp.logical_not)
_vbdot, _vbdot_nt, _vbdot_tn = _lift(_bdot), _lift(_bdot_nt), _lift(_bdot_tn)
_vcat = _lift(lambda a, b: jnp.concatenate([a, b], axis=1))
_vlo = _lift(lambda a, n: a[:, :n])
_vhi = _lift(lambda a, n: a[:, n:])


def _both_masks(n):
    m = [_chunk_masks(d) for d in range(2)]
    mk = {key: _V([m[0][key]] * n + [m[1][key]] * n) for key in m[0] if key != "bds"}
    mk["bds"] = [_V([m[0]["bds"][i]] * n + [m[1]["bds"][i]] * n) for i in range(len(m[0]["bds"]))]
    return mk


def _tri_inv(a, mk):
    eyef = mk["eyef"]
    bds = mk["bds"]
    a8 = _vwhere(bds[0], a, 0.0)
    a2 = _vbdot(a8, a8)
    a4 = _vbdot(a2, a2)
    t = _vbdot(_vbdot(eyef - a8, eyef + a2), eyef + a4)
    for inner, outer in zip(bds, bds[1:] + [None]):
        off = _vnot(inner) if outer is None else (outer & _vnot(inner))
        low = _vwhere(off, a, 0.0)
        t = t - _vbdot(_vbdot(t, low), t)
    return t


def _chunk_prep(q, k, v, g_row, b_row, mk, tuw=None):
    dv = GDN_DK
    g_col = _vsum(mk["eyef"] * g_row, axis=1, keepdims=True)
    b_col = _vsum(mk["eyef"] * b_row, axis=1, keepdims=True)
    gc_col = _vsum(mk["mif"] * g_row, axis=1, keepdims=True)
    gc_row = _vsum(mk["mitf"] * g_col, axis=0, keepdims=True)
    gl = _vsum(g_row, axis=1, keepdims=True)
    decay = _vwhere(mk["mi"], _vexp(_vwhere(mk["mi"], gc_col - gc_row, 0.0)), 0.0)
    eg = _vexp(gc_col)
    e2 = _vexp(gl - gc_col)
    egl = _vexp(gl)
    kb = k * b_col
    pm = _vbdot_nt(kb, k)
    if tuw is None:
        t = _tri_inv(_vwhere(mk["ms"], pm * decay, 0.0), mk)
        sol = _vbdot(t, _vcat(v * b_col, kb * eg))
        u, w = _vlo(sol, dv), _vhi(sol, dv)
    else:
        t, u, w = tuw
    qm = _vbdot_nt(q, k)
    return dict(b_col=b_col, decay=decay, eg=eg, e2=e2, egl=egl, kb=kb, pm=pm, t=t, u=u, w=w,
                qm=qm, intra=qm * decay, qd=q * eg, kd=k * e2)


def _chunk_fwd_step(p, state):
    v_new = p["u"] - _vbdot(p["w"], state)
    o = _vbdot(p["qd"], state) + _vbdot(p["intra"], v_new)
    new_state = state * p["egl"] + _vbdot_tn(p["kd"], v_new)
    return o, new_state


def _chunk_bwd_step(q, k, v, p, mk, state, dso, do):
    dv_dim = GDN_DK
    v_new = p["u"] - _vbdot(p["w"], state)
    dvn = _vbdot_tn(p["intra"], do) + _vbdot(p["kd"], dso)
    dintra = _vbdot_nt(do, v_new)
    dqd = _vbdot_nt(do, state)
    ds = p["egl"] * dso + _vbdot_tn(p["qd"], do) - _vbdot_tn(p["w"], dvn)
    dkd = _vbdot_nt(v_new, dso)
    dgl = _vsum(_vsum(dso * state, axis=1, keepdims=True), axis=0, keepdims=True) * p["egl"]
    dw = -_vbdot_nt(dvn, state)
    drhs = _vbdot_tn(p["t"], _vcat(dvn, dw))
    dru, drw = _vlo(drhs, dv_dim), _vhi(drhs, dv_dim)
    da = -_vwhere(mk["ms"], _vbdot_nt(drhs, _vcat(p["u"], p["w"])), 0.0)
    b_col = p["b_col"]
    dv = dru * b_col
    dbeta = _vsum(dru * v, axis=1, keepdims=True)
    dkb = drw * p["eg"]
    deg = _vsum(drw * p["kb"], axis=1, keepdims=True)
    dp = da * p["decay"]
    ddecay = da * p["pm"]
    dkb = dkb + _vbdot(dp, k)
    dk = _vbdot_tn(dp, p["kb"])
    dqm = dintra * p["decay"]
    ddecay = ddecay + dintra * p["qm"]
    dq = _vbdot(dqm, k)
    dk = dk + _vbdot_tn(dqm, q)
    dd = ddecay * p["decay"]
    dgc_col = _vsum(dd, axis=1, keepdims=True)
    dgc_row = -_vsum(dd, axis=0, keepdims=True)
    dq = dq + dqd * p["eg"]
    deg = deg + _vsum(dqd * q, axis=1, keepdims=True)
    dk = dk + dkd * p["e2"]
    de2 = _vsum(dkd * k, axis=1, keepdims=True) * p["e2"]
    dgl = dgl + _vsum(de2, axis=0, keepdims=True)
    dgc_col = dgc_col - de2 + deg * p["eg"]
    dk = dk + dkb * b_col
    dbeta = dbeta + _vsum(dkb * k, axis=1, keepdims=True)
    dgc_col = dgc_col + _vsum(mk["eyef"] * dgc_row, axis=1, keepdims=True)
    dg_row = _vsum(mk["mif"] * dgc_col, axis=0, keepdims=True) + dgl
    dbeta_row = _vsum(mk["eyef"] * dbeta, axis=0, keepdims=True)
    return dq, dk, dv, dg_row, dbeta_row, ds


def _gdn_chunk_fwd(qkvn, g5, b5, *, name, comm=None):
    s = qkvn.shape[0]
    c = GDN_CHUNK
    nc = s // c
    h_, dk = GDN_HEADS, GDN_DK

    def body(*refs):
        n = pl.program_id(0)
        ins, outs, (st_scr,) = _comm_hooks(comm, refs, 6, 10, 1, n == 0, n == (3 * nc) // 4, n == nc - 1)
        x_refs, g_refs, b_refs = ins[0:2], ins[2:4], ins[4:6]
        o_refs, st_refs, t_refs, u_refs, w_refs = outs[0:2], outs[2:4], outs[4:6], outs[6:8], outs[8:10]

        @pl.when(n == 0)
        def _():
            st_scr[...] = jnp.zeros_like(st_scr)

        ch = [(d, h) for d in range(2) for h in range(h_)]
        mk = _both_masks(h_)
        q, k, v = (_V(x_refs[d][:, (t * h_ + h) * dk:(t * h_ + h + 1) * dk] for d, h in ch) for t in range(3))
        g, b = (_V(r[d][0, h, 0] for d, h in ch) for r in (g_refs, b_refs))
        state = _V(st_scr[d * h_ + h] for d, h in ch)
        p = _chunk_prep(q, k, v, g, b, mk)
        o, new_state = _chunk_fwd_step(p, state)
        for i, (d, h) in enumerate(ch):
            st_refs[d][h, 0] = state.xs[i]
            st_scr[d * h_ + h] = new_state.xs[i]
            o_refs[d][:, h * dk:(h + 1) * dk] = o.xs[i]
            t_refs[d][h, 0] = p["t"].xs[i].astype(BF16)
            u_refs[d][h, 0] = p["u"].xs[i]
            w_refs[d][h, 0] = p["w"].xs[i].astype(BF16)

    ce = (lambda n: n, lambda n: nc - 1 - n)
    xs = [pl.BlockSpec((c, 3 * h_ * dk), lambda n, d=d: (ce[d](n), 0)) for d in range(2)]
    gates = [pl.BlockSpec((1, h_, 1, 1, c), lambda n, d=d: (d, 0, ce[d](n), 0, 0)) for d in range(2)]
    os_ = [pl.BlockSpec((c, h_ * dk), lambda n, d=d: (ce[d](n), 0)) for d in range(2)]
    sts = [pl.BlockSpec((h_, 1, dk, dk), lambda n, d=d: (0, ce[d](n), 0, 0)) for d in range(2)]
    tcc = [pl.BlockSpec((h_, 1, c, c), lambda n, d=d: (0, ce[d](n), 0, 0)) for d in range(2)]
    tck = [pl.BlockSpec((h_, 1, c, dk), lambda n, d=d: (0, ce[d](n), 0, 0)) for d in range(2)]
    per_chunk = lambda last, dt: [jax.ShapeDtypeStruct((h_, nc, c, last), dt)] * 2
    c_in, c_out, c_shape, c_scr = _comm_specs(comm)
    res = pl.pallas_call(
        body, name=name, grid=(nc,),
        in_specs=xs + gates + gates + c_in,
        out_specs=os_ + sts + tcc + tck + tck + c_out,
        out_shape=[jax.ShapeDtypeStruct((s, h_ * dk), F32)] * 2 + [jax.ShapeDtypeStruct((h_, nc, dk, dk), F32)] * 2
        + per_chunk(c, BF16) + per_chunk(dk, F32) + per_chunk(dk, BF16) + c_shape,
        scratch_shapes=[pltpu.VMEM((2 * h_, dk, dk), F32)] + c_scr,
        compiler_params=_cparams(("arbitrary",)),
    )(qkvn, qkvn, g5, g5, b5, b5, *(comm.arrays if comm else []))
    return res[0:2], res[2:10], res[10:]


def _gdn_chunk_bwd(qkvn, g5, b5, states, do, *, name, comm=None):
    s = qkvn.shape[0]
    c = GDN_CHUNK
    nc = s // c
    h_, dk = GDN_HEADS, GDN_DK

    def body(*refs):
        i = pl.program_id(0)
        ins, outs, (ds_scr,) = _comm_hooks(comm, refs, 16, 6, 1, i == 0, i == nc // 2, i == nc - 1)
        x_refs, g_refs, b_refs, st_refs = ins[0:2], ins[2:4], ins[4:6], ins[6:8]
        t_refs, u_refs, w_refs, do_refs = ins[8:10], ins[10:12], ins[12:14], ins[14:16]
        dx_refs, dg_refs, db_refs = outs[0:2], outs[2:4], outs[4:6]

        @pl.when(i == 0)
        def _():
            ds_scr[...] = jnp.zeros_like(ds_scr)

        ch = [(d, h) for d in range(2) for h in range(h_)]
        mk = _both_masks(h_)
        q, k, v = (_V(x_refs[d][:, (t * h_ + h) * dk:(t * h_ + h + 1) * dk] for d, h in ch) for t in range(3))
        g, b = (_V(r[d][0, h, 0] for d, h in ch) for r in (g_refs, b_refs))
        state = _V(st_refs[d][h, 0] for d, h in ch)
        dso = _V(ds_scr[d * h_ + h] for d, h in ch)
        dov = _V(do_refs[d][:, h * dk:(h + 1) * dk] for d, h in ch)
        tuw = tuple(_V(r[d][h, 0] for d, h in ch) for r in (t_refs, u_refs, w_refs))
        res = _chunk_bwd_step(q, k, v, _chunk_prep(q, k, v, g, b, mk, tuw), mk, state, dso, dov)
        for (d, h), (dq, dkk, dvv, dg_r, db_r, ds) in zip(ch, zip(*[r.xs for r in res])):
            ds_scr[d * h_ + h] = ds
            dg_refs[d][h, 0] = dg_r
            db_refs[d][h, 0] = db_r
            for t, val in enumerate((dq, dkk, dvv)):
                dx_refs[d][:, (t * h_ + h) * dk:(t * h_ + h + 1) * dk] = val

    ce = (lambda i: nc - 1 - i, lambda i: i)
    both = lambda mk_spec: [mk_spec(d) for d in range(2)]
    xs = both(lambda d: pl.BlockSpec((c, 3 * h_ * dk), lambda i: (ce[d](i), 0)))
    gates = both(lambda d: pl.BlockSpec((1, h_, 1, 1, c), lambda i: (d, 0, ce[d](i), 0, 0)))
    sts = both(lambda d: pl.BlockSpec((h_, 1, dk, dk), lambda i: (0, ce[d](i), 0, 0)))
    tcc = both(lambda d: pl.BlockSpec((h_, 1, c, c), lambda i: (0, ce[d](i), 0, 0)))
    tck = both(lambda d: pl.BlockSpec((h_, 1, c, dk), lambda i: (0, ce[d](i), 0, 0)))
    dos = both(lambda d: pl.BlockSpec((c, h_ * dk), lambda i: (ce[d](i), 0)))
    gouts = both(lambda d: pl.BlockSpec((h_, 1, 1, c), lambda i: (0, ce[d](i), 0, 0)))
    c_in, c_out, c_shape, c_scr = _comm_specs(comm)
    res = pl.pallas_call(
        body, name=name, grid=(nc,),
        in_specs=xs + gates + gates + sts + tcc + tck + tck + dos + c_in,
        out_specs=xs + gouts + gouts + c_out,
        out_shape=[jax.ShapeDtypeStruct((s, 3 * h_ * dk), F32)] * 2
        + [jax.ShapeDtypeStruct((h_, nc, 1, c), F32)] * 4 + c_shape,
        scratch_shapes=[pltpu.VMEM((2 * h_, dk, dk), F32)] + c_scr,
        compiler_params=_cparams(("arbitrary",)),
    )(qkvn, qkvn, g5, g5, b5, b5, *states, do, do, *(comm.arrays if comm else []))
    return res[0:2], jnp.stack(res[2:4]), jnp.stack(res[4:6]), res[6:]


def _gdn_post_fwd(o, z, norm_w, *, name):
    s = o[0].shape[0]
    h_, dk = GDN_HEADS, GDN_DK

    def body(of_ref, ob_ref, z_ref, w_ref, a_ref):
        ov = of_ref[...] + ob_ref[...]
        zv = z_ref[...]
        r = lax.rsqrt(jnp.mean(ov * ov, axis=-1, keepdims=True) + RMS_EPS)
        a_ref[...] = (ov * r * w_ref[...] * (zv * _sigmoid(zv))).astype(a_ref.dtype)

    col = pl.BlockSpec((s, dk), lambda h: (0, h))
    return pl.pallas_call(
        body, name=name, grid=(h_,),
        in_specs=[col, col, pl.BlockSpec((s, dk), lambda h: (0, 3 * h_ + h)), pl.BlockSpec((1, dk), lambda h: (0, 0))],
        out_specs=col,
        out_shape=jax.ShapeDtypeStruct((s, h_ * dk), BF16),
        compiler_params=_cparams(("parallel",)),
    )(o[0], o[1], z, norm_w.reshape(1, dk))


def _gdn_post_bwd(o, z, norm_w, dact, *, name):
    s = o[0].shape[0]
    h_, dk = GDN_HEADS, GDN_DK

    def body(of_ref, ob_ref, z_ref, w_ref, da_ref, do_ref, dz_ref, dw_ref):
        h = pl.program_id(0)
        ov = of_ref[...] + ob_ref[...]
        zv = z_ref[...]
        wv = w_ref[...]
        dav = da_ref[...]
        r = lax.rsqrt(jnp.mean(ov * ov, axis=-1, keepdims=True) + RMS_EPS)
        nrm = ov * r
        sg = _sigmoid(zv)
        sz = zv * sg
        dn = dav * wv * sz
        do_ref[...] = r * (dn - nrm * jnp.mean(dn * nrm, axis=-1, keepdims=True))
        dz_ref[...] = (dav * nrm * wv * (sg * (1.0 + zv * (1.0 - sg)))).astype(dz_ref.dtype)
        part = jnp.sum(dav * nrm * sz, axis=0, keepdims=True)

        @pl.when(h == 0)
        def _():
            dw_ref[...] = part

        @pl.when(h > 0)
        def _():
            dw_ref[...] += part

    col = pl.BlockSpec((s, dk), lambda h: (0, h))
    vec = pl.BlockSpec((1, dk), lambda h: (0, 0))
    return pl.pallas_call(
        body, name=name, grid=(h_,),
        in_specs=[col, col, pl.BlockSpec((s, dk), lambda h: (0, 3 * h_ + h)), vec, col],
        out_specs=[col, pl.BlockSpec((s, dk), lambda h: (0, 3 * h_ + h)), vec],
        out_shape=[jax.ShapeDtypeStruct((s, h_ * dk), F32), jax.ShapeDtypeStruct((s, GDN_MAIN), BF16),
                   jax.ShapeDtypeStruct((1, dk), F32)],
        compiler_params=_cparams(("arbitrary",)),
    )(o[0], o[1], z, norm_w.reshape(1, dk), dact)


def _rel_bucket(rel):
    nb = REL_BUCKETS // 2
    max_exact = nb // 2
    ret = jnp.where(rel > 0, nb, 0)
    n = jnp.abs(rel)
    nf = jnp.maximum(n, 1).astype(F32)
    large = max_exact + (jnp.log(nf / max_exact) / math.log(REL_MAX_DIST / max_exact)
                         * (nb - max_exact)).astype(jnp.int32)
    large = jnp.minimum(large, nb - 1)
    return ret + jnp.where(n < max_exact, n, large)


def _bucket_onehot():
    half = DSWA_HALF
    outs = []
    for dil in DSWA_DILS:
        rel = (jnp.arange(3 * half)[None, :] - half - jnp.arange(half)[:, None]) * dil
        outs.append(jax.nn.one_hot(_rel_bucket(rel).reshape(-1), REL_BUCKETS, dtype=F32, axis=0))
    return jnp.stack(outs)


def _head_group_select(vals):
    rows = lax.broadcasted_iota(jnp.int32, vals[0].shape, 0)
    return jnp.where(rows < DSWA_HG, vals[0], jnp.where(rows < 2 * DSWA_HG, vals[1], vals[2]))


def _dswa_bias(table_t, onehot, *, name):
    p = onehot.shape[-1]

    def body(t_ref, oh_ref, b_ref):
        b_ref[...] = _head_group_select([_hdot(t_ref[...], oh_ref[g]) for g in range(3)])

    return pl.pallas_call(body, name=name, out_shape=jax.ShapeDtypeStruct((DSWA_HEADS, p), F32),
                          compiler_params=_cparams())(table_t, onehot)


def _dswa_dtable(dbias, onehot, *, name):
    def body(d_ref, oh_ref, t_ref):
        t_ref[...] = _head_group_select([_hdot_nt(d_ref[...], oh_ref[g]) for g in range(3)])

    return pl.pallas_call(body, name=name, out_shape=jax.ShapeDtypeStruct((DSWA_HEADS, REL_BUCKETS), F32),
                          compiler_params=_cparams())(dbias, onehot)


def _rows(start, dil):
    if dil == 1:
        return pl.ds(pl.multiple_of(start, DSWA_HALF), DSWA_HALF)
    return pl.ds(start, DSWA_HALF, stride=dil)


def _attn_blocks(it, s, dil):
    half = DSWA_HALF
    nbs = s // half // dil
    ii = lax.broadcasted_iota(jnp.int32, (half, 3 * half), 0)
    jj = lax.broadcasted_iota(jnp.int32, (half, 3 * half), 1)
    band = jnp.abs(jj - half - ii) <= half
    out = []
    for u in range(DSWA_UNROLL):
        blk = it * DSWA_UNROLL + u
        r, b = blk // nbs, blk % nbs
        own = r + dil * half * b
        prev = own - jnp.where(b > 0, dil * half, 0)
        nxt = own + jnp.where(b < nbs - 1, dil * half, 0)
        ok = band & ((jj >= half) | (b > 0)) & ((jj < 2 * half) | (b < nbs - 1))
        out.append(((prev, own, nxt), ok))
    return out


def _attn_chains(q_ref, k_ref, v_ref, blocks, dil):
    lane = lax.broadcasted_iota(jnp.int32, (DSWA_HALF, 2 * DSWA_E), 1)
    qm, kw, vw, valid, hmask = [], [], [], [], []
    for (prev, own, nxt), ok in blocks:
        q = q_ref[_rows(own, dil), :].astype(BF16)
        k = jnp.concatenate([k_ref[_rows(st, dil), :] for st in (prev, own, nxt)], axis=0).astype(BF16)
        v = jnp.concatenate([v_ref[_rows(st, dil), :] for st in (prev, own, nxt)], axis=0).astype(BF16)
        for hd in range(2):
            mine = (lane < DSWA_E) if hd == 0 else (lane >= DSWA_E)
            qm.append(jnp.where(mine, q, jnp.zeros_like(q)))
            kw.append(k)
            vw.append(v)
            valid.append(ok)
            hmask.append(mine)
    return _V(qm), _V(kw), _V(vw), _V(valid), _V(hmask)


def _per_group(pr, fn):
    for gi, dil in enumerate(DSWA_DILS):
        pl.when(pr // DSWA_PG == gi)(functools.partial(fn, dil))


_vmax, _vlog = _lift(jnp.max), _lift(jnp.log)


def _dswa_attn_fwd(qkv, bias, *, name, comm=None):
    s = qkv.shape[0]
    half, e = DSWA_HALF, DSWA_E
    npair = DSWA_HEADS // 2

    def body(*refs):
        pr = pl.program_id(0)
        (q_ref, k_ref, v_ref, bias_ref), (o_ref, lse_ref), _ = _comm_hooks(
            comm, refs, 4, 2, 0, pr == 0, pr == (3 * npair) // 4, pr == npair - 1)
        bias_v = _V([bias_ref[0], bias_ref[1]] * DSWA_UNROLL)

        def run(dil):
            def step(it, carry):
                blocks = _attn_blocks(it, s, dil)
                qm, kw, vw, valid, hmask = _attn_chains(q_ref, k_ref, v_ref, blocks, dil)
                sc = _vwhere(valid, _vbdot_nt(qm, kw) * (e ** -0.5) + bias_v, NEG_INF)
                m = _vmax(sc, axis=-1, keepdims=True)
                p = _vexp(sc - m)
                l = _vsum(p, axis=-1, keepdims=True)
                o = _vbdot(p * (1.0 / l), vw)
                lse = m + _vlog(l)
                for u, ((_, own, _), _) in enumerate(blocks):
                    is_a = hmask.xs[2 * u]
                    o_ref[_rows(own, dil), :] = jnp.where(is_a, o.xs[2 * u], o.xs[2 * u + 1])
                    lse_ref[_rows(own, dil), :] = jnp.where(is_a, lse.xs[2 * u], lse.xs[2 * u + 1])
                return carry

            lax.fori_loop(0, s // half // DSWA_UNROLL, step, 0)

        _per_group(pr, run)

    col = lambda t: pl.BlockSpec((s, 2 * e), lambda p: (0, t * npair + p))
    pair = pl.BlockSpec((s, 2 * e), lambda p: (0, p))
    c_in, c_out, c_shape, c_scr = _comm_specs(comm)
    res = pl.pallas_call(
        body, name=name, grid=(npair,),
        in_specs=[col(0), col(1), col(2), pl.BlockSpec((2, half, 3 * half), lambda p: (p, 0, 0))] + c_in,
        out_specs=[pair, pair] + c_out,
        out_shape=[jax.ShapeDtypeStruct((s, npair * 2 * e), F32)] * 2 + c_shape,
        scratch_shapes=c_scr,
        compiler_params=_cparams(("arbitrary",)),
    )(qkv, qkv, qkv, bias, *(comm.arrays if comm else []))
    return res[0], res[1], res[2:]


def _dswa_attn_bwd(qkv, bias, lse, do, corr, *, name, comm=None):
    s = qkv.shape[0]
    half, e = DSWA_HALF, DSWA_E
    npair = DSWA_HEADS // 2
    w = 2 * e

    def body(*refs):
        pr = pl.program_id(0)
        (q_ref, k_ref, v_ref, bias_ref, lse_ref, do_ref, corr_ref), (dq_ref, dk_ref, dv_ref, db_ref), _ = _comm_hooks(
            comm, refs, 7, 4, 0, pr == 0, pr == npair // 2, pr == npair - 1)
        bias_v = _V([bias_ref[0], bias_ref[1]] * DSWA_UNROLL)
        dk_ref[...] = jnp.zeros_like(dk_ref)
        dv_ref[...] = jnp.zeros_like(dv_ref)

        def run(dil):
            def step(it, dbias):
                blocks = _attn_blocks(it, s, dil)
                qm, kw, vw, valid, hmask = _attn_chains(q_ref, k_ref, v_ref, blocks, dil)
                hd = [0, 1] * DSWA_UNROLL
                rows = [_rows(own, dil) for (_, own, _), _ in blocks for _ in range(2)]
                lse_c = _V(lse_ref[rw, :][:, h * e:h * e + 1] for rw, h in zip(rows, hd))
                corr_c = _V(corr_ref[rw, :][:, h * e:h * e + 1] for rw, h in zip(rows, hd))
                dov = _vwhere(hmask, _V(do_ref[rw, :] for rw in rows), 0.0)
                sc = _vbdot_nt(qm, kw) * (e ** -0.5) + bias_v
                p = _vwhere(valid, _vexp(_vwhere(valid, sc, 0.0) - lse_c), 0.0)
                dsc = p * (_vbdot_nt(dov, vw) + corr_c)
                dq = _vbdot(dsc, kw) * (e ** -0.5)
                dkc = _vbdot_tn(dsc, qm) * (e ** -0.5)
                dvc = _vbdot_tn(p, dov)
                for u, (starts, _) in enumerate(blocks):
                    dq_ref[_rows(starts[1], dil), :] = jnp.where(hmask.xs[2 * u], dq.xs[2 * u], dq.xs[2 * u + 1])
                    dk_u = dkc.xs[2 * u] + dkc.xs[2 * u + 1]
                    dv_u = dvc.xs[2 * u] + dvc.xs[2 * u + 1]
                    for t, st in enumerate(starts):
                        dk_ref[_rows(st, dil), :] += dk_u[t * half:(t + 1) * half]
                        dv_ref[_rows(st, dil), :] += dv_u[t * half:(t + 1) * half]
                da, db = dbias
                for u in range(DSWA_UNROLL):
                    da, db = da + dsc.xs[2 * u], db + dsc.xs[2 * u + 1]
                return da, db

            zero = jnp.zeros((half, 3 * half), F32)
            da, db = lax.fori_loop(0, s // half // DSWA_UNROLL, step, (zero, zero))
            db_ref[0] = da
            db_ref[1] = db

        _per_group(pr, run)

    col = lambda t: pl.BlockSpec((s, w), lambda p: (0, t * npair + p))
    ps = pl.BlockSpec((s, w), lambda p: (0, p))
    bs = pl.BlockSpec((2, half, 3 * half), lambda p: (p, 0, 0))
    c_in, c_out, c_shape, c_scr = _comm_specs(comm)
    res = pl.pallas_call(
        body, name=name, grid=(npair,),
        in_specs=[col(0), col(1), col(2), bs, ps, ps, ps] + c_in,
        out_specs=[ps, ps, ps, bs] + c_out,
        out_shape=[jax.ShapeDtypeStruct((s, npair * w), F32)] * 3
        + [jax.ShapeDtypeStruct((DSWA_HEADS, half, 3 * half), F32)] + c_shape,
        scratch_shapes=c_scr,
        compiler_params=_cparams(("arbitrary",)),
    )(qkv, qkv, qkv, bias, lse, do, corr, *(comm.arrays if comm else []))
    return res[0], res[1], res[2], res[3], res[4:]


def _pair_cols(g, j):
    w = 2 * DSWA_E
    return slice((g * DSWA_PG + j) * w, (g * DSWA_PG + j + 1) * w)


def _group_weights(l_ref, j):
    ls = [l_ref[:, _pair_cols(g, j)] for g in range(3)]
    m = jnp.maximum(jnp.maximum(ls[0], ls[1]), ls[2])
    es = [jnp.exp(x - m) for x in ls]
    inv = 1.0 / (es[0] + es[1] + es[2])
    return [x * inv for x in es]


def _dswa_combine_fwd(o, lse, *, name):
    s, wd = o.shape
    tr = _pick(s, (512, 256, 128))

    def body(o_ref, l_ref, c_ref):
        for j in range(DSWA_PG):
            al = _group_weights(l_ref, j)
            for g in range(3):
                c_ref[:, _pair_cols(g, j)] = (o_ref[:, _pair_cols(g, j)] * al[g]).astype(c_ref.dtype)

    row = pl.BlockSpec((tr, wd), lambda i: (i, 0))
    return pl.pallas_call(
        body, name=name, grid=(s // tr,),
        in_specs=[row, row], out_specs=row,
        out_shape=jax.ShapeDtypeStruct(o.shape, BF16),
        compiler_params=_cparams(("parallel",)),
    )(o, lse)


def _dswa_combine_bwd(o, lse, dc, *, name):
    s, wd = o.shape
    tr = _pick(s, (512, 256, 128))

    def body(o_ref, l_ref, dc_ref, do_ref, corr_ref):
        lane = lax.broadcasted_iota(jnp.int32, (tr, 2 * DSWA_E), 1)
        is_a = lane < DSWA_E
        for j in range(DSWA_PG):
            al = _group_weights(l_ref, j)
            tot = jnp.zeros((tr, 2 * DSWA_E), F32)
            for g in range(3):
                cols = _pair_cols(g, j)
                dcv = dc_ref[:, cols]
                do_ref[:, cols] = dcv * al[g]
                prod = dcv * o_ref[:, cols]
                dal = jnp.where(is_a, jnp.sum(jnp.where(is_a, prod, 0.0), axis=-1, keepdims=True),
                                jnp.sum(jnp.where(is_a, 0.0, prod), axis=-1, keepdims=True))
                tot = tot + al[g] * dal
            for g in range(3):
                corr_ref[:, _pair_cols(g, j)] = -al[g] * tot

    row = pl.BlockSpec((tr, wd), lambda i: (i, 0))
    return pl.pallas_call(
        body, name=name, grid=(s // tr,),
        in_specs=[row, row, row], out_specs=[row, row],
        out_shape=[jax.ShapeDtypeStruct(o.shape, F32)] * 2,
        compiler_params=_cparams(("parallel",)),
    )(o, lse, dc)


class _Comm:
    def __init__(self, mode, arrays, kinds=None):
        self.mode, self.arrays, self.kinds = mode, list(arrays), kinds
        self.n = len(self.arrays)

    def out_shapes(self):
        if self.mode == "exchange":
            return [jax.ShapeDtypeStruct(x.shape, x.dtype) for x in self.arrays]
        shapes = []
        for x, kd in zip(self.arrays, self.kinds):
            shp = list(x.shape)
            if kd == "stack":
                shp = [N_DEV] + shp
            else:
                shp[-2 if kd == "rows" else -1] *= N_DEV
            shapes.append(jax.ShapeDtypeStruct(tuple(shp), x.dtype))
        return shapes

    def scratch(self):
        return [pltpu.SemaphoreType.DMA((7 * self.n,)), pltpu.SemaphoreType.DMA((7 * self.n,)),
                pltpu.SemaphoreType.DMA((self.n,))]

    def bind(self, in_refs, out_refs, sems):
        self.x, self.o = in_refs, out_refs
        self.send_sems, self.recv_sems, self.local_sems = sems
        self.pos = (lax.axis_index("x"), lax.axis_index("y"), lax.axis_index("c"))

    def _slot(self, i, px, py, pc):
        p = 4 * px + 2 * py + pc
        kd = self.kinds[i]
        if kd == "stack":
            return self.o[i].at[p]
        nd = len(self.x[i].shape)
        ax = nd - 2 if kd == "rows" else nd - 1
        size = self.x[i].shape[ax]
        idx = tuple(pl.ds(p * size, size) if a == ax else slice(None) for a in range(nd))
        return self.o[i].at[idx]

    def _gcopy(self, i, k, block, to, src=None):
        return pltpu.make_async_remote_copy(
            src_ref=self._slot(i, *block) if src is None else src, dst_ref=self._slot(i, *block),
            send_sem=self.send_sems.at[7 * i + k], recv_sem=self.recv_sems.at[7 * i + k],
            device_id=to, device_id_type=pl.DeviceIdType.MESH)

    def _chips(self):
        mx, my, _ = self.pos
        return [(1 - mx, my), (mx, 1 - my), (1 - mx, 1 - my)]

    def _xcopies(self):
        mx, my, mc = self.pos
        me = 4 * mx + 2 * my + mc
        copies = []
        for k in range(1, N_DEV):
            px = 1 - mx if (k >> 2) & 1 else mx
            py = 1 - my if (k >> 1) & 1 else my
            pc = 1 - mc if k & 1 else mc
            for i in range(self.n):
                copies.append(pltpu.make_async_remote_copy(
                    src_ref=self.x[i].at[4 * px + 2 * py + pc], dst_ref=self.o[i].at[me],
                    send_sem=self.send_sems.at[7 * i + k - 1], recv_sem=self.recv_sems.at[7 * i + k - 1],
                    device_id=(px, py, pc), device_id_type=pl.DeviceIdType.MESH))
        return copies

    def _local(self):
        mx, my, mc = self.pos
        if self.mode == "exchange":
            me = 4 * mx + 2 * my + mc
            return [pltpu.make_async_copy(self.x[i].at[me], self.o[i].at[me], self.local_sems.at[i]) for i in range(self.n)]
        return [pltpu.make_async_copy(self.x[i], self._slot(i, mx, my, mc), self.local_sems.at[i]) for i in range(self.n)]

    def _first(self):
        mx, my, mc = self.pos
        me, sibling = (mx, my, mc), (mx, my, 1 - mc)
        first = [self._gcopy(i, 0, me, sibling, src=self.x[i]) for i in range(self.n)]
        first += [self._gcopy(i, 1 + j, me, (*chip, mc), src=self.x[i]) for j, chip in enumerate(self._chips())
                  for i in range(self.n)]
        return first

    def _passed(self):
        mx, my, mc = self.pos
        return [self._gcopy(i, 4 + j, (*chip, mc), (mx, my, 1 - mc)) for j, chip in enumerate(self._chips())
                for i in range(self.n)]

    def start(self):
        for cp in self._local() + (self._xcopies() if self.mode == "exchange" else self._first()):
            cp.start()

    def mid(self):
        if self.mode == "exchange":
            return
        mx, my, mc = self.pos
        passed = self._passed()
        for j, chip in enumerate(self._chips()):
            for i in range(self.n):
                self._gcopy(i, 1 + j, (*chip, mc), (mx, my, mc)).wait_recv()
                passed[j * self.n + i].start()

    def end(self):
        mx, my, mc = self.pos
        if self.mode == "exchange":
            copies = self._xcopies()
            for cp in copies:
                cp.wait_recv()
            for cp in copies:
                cp.wait_send()
        else:
            for i in range(self.n):
                self._gcopy(i, 0, (mx, my, 1 - mc), (mx, my, mc)).wait_recv()
                for j, chip in enumerate(self._chips()):
                    self._gcopy(i, 4 + j, (*chip, 1 - mc), (mx, my, mc)).wait_recv()
            for cp in self._first() + self._passed():
                cp.wait_send()
        for cp in self._local():
            cp.wait()

    def run(self, *, name):
        n = self.n

        def body(*refs):
            self.bind(refs[:n], refs[n:2 * n], refs[2 * n:])
            self.start()
            self.mid()
            self.end()

        anyspec = pl.BlockSpec(memory_space=pl.ANY)
        return pl.pallas_call(body, name=name, in_specs=[anyspec] * n, out_specs=[anyspec] * n,
                              out_shape=self.out_shapes(), scratch_shapes=self.scratch())(*self.arrays)


def _comm_specs(comm):
    if comm is None:
        return [], [], [], []
    anyspec = pl.BlockSpec(memory_space=pl.ANY)
    return [anyspec] * comm.n, [anyspec] * comm.n, comm.out_shapes(), comm.scratch()


def _comm_hooks(comm, refs, n_in, n_out, n_scr, first, mid, last):
    if comm is None:
        return refs[:n_in], refs[n_in:n_in + n_out], refs[n_in + n_out:]
    c = comm.n
    ins, cin = refs[:n_in], refs[n_in:n_in + c]
    outs, cout = refs[n_in + c:n_in + c + n_out], refs[n_in + c + n_out:n_in + 2 * c + n_out]
    scr, sems = refs[n_in + 2 * c + n_out:n_in + 2 * c + n_out + n_scr], refs[n_in + 2 * c + n_out + n_scr:]
    comm.bind(cin, cout, sems)
    pl.when(first)(comm.start)
    pl.when(mid)(comm.mid)
    pl.when(last)(comm.end)
    return ins, outs, scr


def _adamw_update(g, w, m, v):
    mn = ADAM_B1 * m + (1.0 - ADAM_B1) * g
    vn = ADAM_B2 * v + (1.0 - ADAM_B2) * (g * g)
    m_hat = mn / (1.0 - ADAM_B1 ** ADAM_STEP)
    v_hat = vn / (1.0 - ADAM_B2 ** ADAM_STEP)
    return -ADAM_LR * (m_hat / (jnp.sqrt(v_hat) + ADAM_EPS) + ADAM_WD * w), mn, vn


def _adamw_layers(recvs, w, m, v, *, name):
    nl, ks, ns = w.shape
    tr = _pick(ks, (64, 48))

    def body(*refs):
        rv_refs = refs[:nl]
        w_ref, m_ref, v_ref, g_ref, d_ref, nm_ref, nv_ref = refs[nl:]
        for l in range(nl):
            g = rv_refs[l][0].astype(F32)
            for q in range(1, N_DEV):
                g = g + rv_refs[l][q].astype(F32)
            delta, mn, vn = _adamw_update(g, w_ref[l], m_ref[l], v_ref[l])
            g_ref[l] = g
            d_ref[l] = delta
            nm_ref[l] = mn
            nv_ref[l] = vn

    row = pl.BlockSpec((nl, tr, ns), lambda i: (0, i, 0))
    return pl.pallas_call(
        body, name=name, grid=(ks // tr,),
        in_specs=[pl.BlockSpec((N_DEV, tr, ns), lambda i: (0, i, 0))] * nl + [row] * 3,
        out_specs=[row] * 4,
        out_shape=[jax.ShapeDtypeStruct((nl, ks, ns), F32)] * 4,
        compiler_params=_cparams(("parallel",)),
    )(*recvs, w, m, v)


def _adamw_reduce(recv, w, m, v, *, name):
    r, c = w.shape
    tr = _pick(r, (128, 64, 8))

    def body(rv_ref, w_ref, m_ref, v_ref, g_ref, d_ref, nm_ref, nv_ref):
        g = rv_ref[0]
        for q in range(1, N_DEV):
            g = g + rv_ref[q]
        delta, mn, vn = _adamw_update(g, w_ref[...], m_ref[...], v_ref[...])
        g_ref[...] = g
        d_ref[...] = delta
        nm_ref[...] = mn
        nv_ref[...] = vn

    row = pl.BlockSpec((tr, c), lambda i: (i, 0))
    return pl.pallas_call(
        body, name=name, grid=(r // tr,),
        in_specs=[pl.BlockSpec((N_DEV, tr, c), lambda i: (0, i, 0)), row, row, row],
        out_specs=[row] * 4,
        out_shape=[jax.ShapeDtypeStruct((r, c), F32)] * 4,
        compiler_params=_cparams(("parallel",)),
    )(recv, w, m, v)


_BIG = ("gdn_w_in", "gdn_w_out", "dswa_w_in", "dswa_w_out", "mlp_w1", "mlp_w2")
_SMALL = ("gdn_conv_w", "norm_mix", "norm_mlp", "norm_final", "rel_bias", "gdn_a_log", "gdn_dt_bias", "gdn_norm_w")
_ORDER = ("norm_mix", "norm_mlp", "norm_final", "rel_bias", "gdn_w_in", "gdn_conv_w", "gdn_a_log", "gdn_dt_bias",
          "gdn_norm_w", "gdn_w_out", "dswa_w_in", "dswa_w_out", "mlp_w1", "mlp_w2")
_KIND = dict(gdn_w_in="stack", gdn_w_out="rows", dswa_w_in="stack", dswa_w_out="rows", mlp_w1="cols", mlp_w2="rows")


def _pack_rows(arrs, align):
    rows, counts = [], []
    for a in arrs:
        flat = a.reshape(-1)
        n = -(-flat.shape[0] // D_MODEL)
        flat = jnp.pad(flat, (0, n * D_MODEL - flat.shape[0]))
        rows.append(flat.reshape(n, D_MODEL))
        counts.append(n)
    out = jnp.concatenate(rows, axis=0)
    total = -(-out.shape[0] // align) * align
    return jnp.pad(out, ((0, total - out.shape[0]), (0, 0))), counts


def _unpack_rows(slab, shapes):
    outs, r = [], 0
    for shp in shapes:
        size = int(np.prod(shp))
        n = -(-size // D_MODEL)
        outs.append(slab[r:r + n].reshape(-1)[:size].reshape(shp))
        r += n
    return outs


def _col_shards(full, nshard):
    lead = full.shape[:-1]
    n = full.shape[-1] // nshard
    t = full.reshape(lead + (nshard, n))
    return jnp.moveaxis(t, -2, 0)


def _from_col_shards(g):
    t = jnp.moveaxis(g, 0, -2)
    return t.reshape(t.shape[:-2] + (t.shape[-2] * t.shape[-1],))


def kernel(x, norm_mix, norm_mlp, norm_final, rel_bias, gdn_w_in, gdn_conv_w, gdn_a_log, gdn_dt_bias, gdn_norm_w, gdn_w_out, dswa_w_in, dswa_w_out, mlp_w1, mlp_w2, loss_target, m_norm_mix, m_norm_mlp, m_norm_final, m_rel_bias, m_gdn_w_in, m_gdn_conv_w, m_gdn_a_log, m_gdn_dt_bias, m_gdn_norm_w, m_gdn_w_out, m_dswa_w_in, m_dswa_w_out, m_mlp_w1, m_mlp_w2, v_norm_mix, v_norm_mlp, v_norm_final, v_rel_bias, v_gdn_w_in, v_gdn_conv_w, v_gdn_a_log, v_gdn_dt_bias, v_gdn_norm_w, v_gdn_w_out, v_dswa_w_in, v_dswa_w_out, v_mlp_w1, v_mlp_w2):
    params = dict(norm_mix=norm_mix, norm_mlp=norm_mlp, norm_final=norm_final, rel_bias=rel_bias,
                  gdn_w_in=gdn_w_in, gdn_conv_w=gdn_conv_w, gdn_a_log=gdn_a_log, gdn_dt_bias=gdn_dt_bias,
                  gdn_norm_w=gdn_norm_w, gdn_w_out=gdn_w_out, dswa_w_in=dswa_w_in, dswa_w_out=dswa_w_out,
                  mlp_w1=mlp_w1, mlp_w2=mlp_w2)
    mom_m = dict(norm_mix=m_norm_mix, norm_mlp=m_norm_mlp, norm_final=m_norm_final, rel_bias=m_rel_bias,
                 gdn_w_in=m_gdn_w_in, gdn_conv_w=m_gdn_conv_w, gdn_a_log=m_gdn_a_log, gdn_dt_bias=m_gdn_dt_bias,
                 gdn_norm_w=m_gdn_norm_w, gdn_w_out=m_gdn_w_out, dswa_w_in=m_dswa_w_in, dswa_w_out=m_dswa_w_out,
                 mlp_w1=m_mlp_w1, mlp_w2=m_mlp_w2)
    mom_v = dict(norm_mix=v_norm_mix, norm_mlp=v_norm_mlp, norm_final=v_norm_final, rel_bias=v_rel_bias,
                 gdn_w_in=v_gdn_w_in, gdn_conv_w=v_gdn_conv_w, gdn_a_log=v_gdn_a_log, gdn_dt_bias=v_gdn_dt_bias,
                 gdn_norm_w=v_gdn_norm_w, gdn_w_out=v_gdn_w_out, dswa_w_in=v_dswa_w_in, dswa_w_out=v_dswa_w_out,
                 mlp_w1=v_mlp_w1, mlp_w2=v_mlp_w2)
    xs = x[0]
    target = loss_target[0]
    dist = _Dist(params)
    conv_tail, _ = _pack_rows([gdn_conv_w], 8)
    (conv_g,) = dist.put("start", dist.gather_comm("start", extra=[(conv_tail, "stack")]).run(name="ag_start"))
    conv_parts = [_unpack_rows(conv_g[dev], [gdn_conv_w.shape])[0] for dev in range(N_DEV)]
    conv_full = _from_col_shards(jnp.stack(conv_parts))[:, :, 0, :]

    loss_part, dcur, g_big, rep, g_conv = _local_step(
        xs, target, dict(norm_mix=norm_mix, norm_mlp=norm_mlp, norm_final=norm_final, rel_bias=rel_bias,
                         gdn_a_log=gdn_a_log, gdn_dt_bias=gdn_dt_bias, gdn_norm_w=gdn_norm_w), dist.full, conv_full, dist)
    loss = lax.psum(loss_part[0, 0], ("x", "y", "c"))
    grad_x = dcur[None]

    conv_dev = _col_shards(jnp.stack(g_conv)[:, :, None, :], N_DEV)
    small_send = jnp.stack([_pack_rows([conv_dev[dev]] + [rep[n] for n in _SMALL[1:]], 8)[0] for dev in range(N_DEV)])
    (small_recv,) = dist.got("end", dist.send_comm("end", g_big, extra=[small_send]).run(name="grad_exchange"))

    outs = {}
    for n in _BIG:
        recvs = [dist.recv[(n, l)] for l in range(params[n].shape[0])]
        res = _adamw_layers(recvs, params[n], mom_m[n], mom_v[n], name=f"adamw_{n}")
        for tag, t in zip(("grad", "delta", "new_m", "new_v"), res):
            outs[(tag, n)] = t
    w_slab, _ = _pack_rows([params[n] for n in _SMALL], 8)
    m_slab, _ = _pack_rows([mom_m[n] for n in _SMALL], 8)
    v_slab, _ = _pack_rows([mom_v[n] for n in _SMALL], 8)
    small = _adamw_reduce(small_recv, w_slab, m_slab, v_slab, name="adamw_small")
    shapes = [params[n].shape for n in _SMALL]
    for tag, slab in zip(("grad", "delta", "new_m", "new_v"), small):
        for n, t in zip(_SMALL, _unpack_rows(slab, shapes)):
            outs[(tag, n)] = t
    result = [loss, grad_x]
    for tag in ("grad", "delta", "new_m", "new_v"):
        result += [outs[(tag, n)] for n in _ORDER]
    return tuple(result)


_GATHER = {
    "start": (("gdn_w_in", 0),),
    "gdn_proj0": (("gdn_w_out", 0), ("mlp_w1", 0)),
    "chunk_fwd0": (("mlp_w2", 0), ("dswa_w_in", 0), ("dswa_w_out", 0), ("mlp_w1", 1), ("gdn_w_out", 1)),
    "mlp_up0": (("mlp_w2", 1),),
    "mlp_down0": (("gdn_w_in", 1),),
    "attn_fwd1": (("mlp_w1", 2),),
    "mlp_up1": (("mlp_w2", 2),),
    "chunk_fwd2": (("dswa_w_in", 1), ("dswa_w_out", 1), ("mlp_w1", 3), ("mlp_w2", 3)),
}
_SEND = {
    "attn_bwd3": (("mlp_w1", 3), ("mlp_w2", 3)),
    "chunk_bwd2": (("dswa_w_in", 1), ("dswa_w_out", 1), ("mlp_w2", 2)),
    "pre_bwd2": (("mlp_w1", 2),),
    "gdn_proj_bwd2": (("gdn_w_in", 1),),
    "attn_bwd1": (("mlp_w1", 1), ("mlp_w2", 1)),
    "chunk_bwd0": (("gdn_w_out", 1), ("dswa_w_in", 0), ("dswa_w_out", 0), ("mlp_w2", 0)),
    "pre_bwd0": (("mlp_w1", 0), ("gdn_w_out", 0)),
    "gdn_proj_bwd0": (("gdn_w_in", 0),),
    "end": (),
}


class _Dist:
    def __init__(self, params):
        self.shards = {n: params[n].astype(BF16) for n in _BIG}
        self.full = {n: [None] * params[n].shape[0] for n in _BIG}
        self.recv = {}

    def gather_comm(self, tag, extra=()):
        if tag not in _GATHER:
            return None
        arrays = [self.shards[n][l] for n, l in _GATHER[tag]] + [a for a, _ in extra]
        return _Comm("gather", arrays, [_KIND[n] for n, _ in _GATHER[tag]] + [k for _, k in extra])

    def put(self, tag, outs):
        for (n, l), t in zip(_GATHER.get(tag, ()), outs):
            self.full[n][l] = _from_col_shards(t) if _KIND[n] == "stack" else t
        return outs[len(_GATHER.get(tag, ())):]

    def send_comm(self, tag, g_big, extra=()):
        if tag not in _SEND:
            return None
        arrays = [_col_shards(g_big[n][l], N_DEV) if _KIND[n] == "stack" else g_big[n][l] for n, l in _SEND[tag]]
        return _Comm("exchange", arrays + list(extra))

    def got(self, tag, outs):
        for item, t in zip(_SEND.get(tag, ()), outs):
            self.recv[item] = t
        return outs[len(_SEND.get(tag, ())):]


def _mm_gather(dist, tag, *args, **kw):
    comm = dist and dist.gather_comm(tag)
    if not comm:
        return _mm(*args, **kw)
    res, got = _mm(*args, comm=comm, **kw)
    dist.put(tag, got)
    return res


def _ep_residual_norm(acc, res, g):
    x = acc + res
    r = lax.rsqrt(jnp.mean(x * x, axis=-1, keepdims=True) + RMS_EPS)
    return x, x * r * g


def _ep_rms_bwd(dh, x, dres, g):
    r = lax.rsqrt(jnp.mean(x * x, axis=-1, keepdims=True) + RMS_EPS)
    xn = x * r
    dn = dh * g
    dx = dres + r * (dn - xn * jnp.mean(dn * xn, axis=-1, keepdims=True))
    return dx, dx, jnp.sum(dh * xn, axis=0, keepdims=True)


def _local_step(xs, target, sp, full, conv_full, dist=None):
    s = xs.shape[0]
    norm_mix, norm_mlp, norm_final = sp["norm_mix"], sp["norm_mlp"], sp["norm_final"]
    gdn_a_log, gdn_dt_bias, gdn_norm_w = sp["gdn_a_log"], sp["gdn_dt_bias"], sp["gdn_norm_w"]
    onehot = _bucket_onehot()
    table_t = sp["rel_bias"].T
    bias = _dswa_bias(table_t, onehot, name="dswa_bias").reshape(DSWA_HEADS, DSWA_HALF, 3 * DSWA_HALF)

    saved = []
    cur = xs
    row = lambda v: v.reshape(1, -1)
    h = _rms_fwd(cur, norm_mix[0], name="rms_mix_fwd0")
    for i in range(DEPTH):
        j = i // 2
        sv = dict(x_in=cur, h=h)
        if i % 2 == 0:
            w_in = full["gdn_w_in"][j]
            proj = _mm_gather(dist, f"gdn_proj{i}", h, w_in, b_cols=(0, GDN_MAIN), name=f"gdn_proj{i}")
            ab = _mm(h, w_in[:, GDN_MAIN:], name=f"gdn_proj_ab{i}")
            qkvn = _gdn_pre_fwd(proj, conv_full[j], name=f"gdn_pre_fwd{i}")
            g_all, beta_all = _gdn_gate_fwd(ab[:, :2 * GDN_HEADS], ab[:, 2 * GDN_HEADS:], gdn_a_log[j], gdn_dt_bias[j],
                                            name=f"gdn_gate_fwd{i}")
            gshape = (2, GDN_HEADS, s // GDN_CHUNK, 1, GDN_CHUNK)
            g_row = g_all.T.reshape(gshape)
            b_row = beta_all.T.reshape(gshape)
            o, states, got = _gdn_chunk_fwd(qkvn, g_row, b_row, name=f"gdn_chunk_fwd{i}",
                                            comm=dist and dist.gather_comm(f"chunk_fwd{i}"))
            if dist:
                dist.put(f"chunk_fwd{i}", got)
            act = _gdn_post_fwd(o, proj, gdn_norm_w[j], name=f"gdn_post_fwd{i}")
            sv.update(proj=proj, ab=ab, qkvn=qkvn, g_row=g_row, b_row=b_row, o=o, states=states, act=act)
            w_out = full["gdn_w_out"][j]
        else:
            w_in = full["dswa_w_in"][j]
            qkv = _mm(h, w_in, name=f"dswa_proj{i}")
            o_n, lse_n, got = _dswa_attn_fwd(qkv, bias, name=f"dswa_attn_fwd{i}",
                                             comm=dist and dist.gather_comm(f"attn_fwd{i}"))
            if dist:
                dist.put(f"attn_fwd{i}", got)
            act = _dswa_combine_fwd(o_n, lse_n, name=f"dswa_comb_fwd{i}")
            sv.update(qkv=qkv, o_n=o_n, lse_n=lse_n, act=act)
            w_out = full["dswa_w_out"][j]
        cur, h2 = _mm(act, w_out, name=f"mix_out{i}", out_dtypes=(F32, BF16), epilogue=_ep_residual_norm,
                      extras=(cur,), vecs=(row(norm_mlp[i]),))
        sv["x_mid"] = cur
        u, a = _mm_gather(dist, f"mlp_up{i}", h2, full["mlp_w1"][i], name=f"mlp_up{i}", out_dtypes=(BF16, BF16),
                          epilogue=lambda acc: (acc, jnp.square(jnp.maximum(acc, 0.0))))
        if i + 1 < DEPTH:
            cur, h = _mm_gather(dist, f"mlp_down{i}", a, full["mlp_w2"][i], name=f"mlp_down{i}", out_dtypes=(F32, BF16),
                                epilogue=_ep_residual_norm, extras=(cur,), vecs=(row(norm_mix[i + 1]),))
        else:
            cur = _mm_gather(dist, f"mlp_down{i}", a, full["mlp_w2"][i], name=f"mlp_down{i}",
                             epilogue=lambda acc, r: (acc + r,), extras=(cur,))
        sv.update(h2=h2, u=u, a=a)
        saved.append(sv)

    loss_part, dcur, dcur_b, dg_final = _loss_head(cur, norm_final, target, name="loss_head")

    g_norm_mix, g_norm_mlp = [None] * DEPTH, [None] * DEPTH
    g_big = {n: [None] * len(full[n]) for n in _BIG}
    g_conv, g_alog, g_dt, g_nw = [None] * 2, [None] * 2, [None] * 2, [None] * 2
    d_table_t = jnp.zeros((DSWA_HEADS, REL_BUCKETS), F32)
    for i in reversed(range(DEPTH)):
        j = i // 2
        sv = saved[i]
        w1, w2 = full["mlp_w1"][i], full["mlp_w2"][i]
        du = _mm(dcur_b, w2, tb=True, name=f"mlp_down_bwd{i}", out_dtypes=(BF16,),
                 epilogue=lambda acc, uu: (acc * (2.0 * jnp.maximum(uu.astype(F32), 0.0)),), extras=(sv["u"],))
        g_big["mlp_w2"][i] = _mm(sv["a"], dcur_b, ta=True, name=f"mlp_w2_grad{i}", out_dtypes=(BF16,), shard="rows")
        g_big["mlp_w1"][i] = _mm(sv["h2"], du, ta=True, name=f"mlp_w1_grad{i}", out_dtypes=(BF16,), shard="cols")
        dmid, dmid_b, g_norm_mlp[i] = _mm(du, w1, tb=True, name=f"mlp_up_bwd{i}", out_dtypes=(F32, BF16),
                                          epilogue=_ep_rms_bwd, extras=(sv["x_mid"], dcur), vecs=(row(norm_mlp[i]),),
                                          vec_out=True)
        if i % 2 == 0:
            w_in, w_out = full["gdn_w_in"][j], full["gdn_w_out"][j]
            dact = _mm(dmid_b, w_out, tb=True, name=f"mix_out_bwd{i}")
            g_big["gdn_w_out"][j] = _mm(sv["act"], dmid_b, ta=True, name=f"mix_out_grad{i}", out_dtypes=(BF16,),
                                        shard="rows")
            do, dz, g_nw[j] = _gdn_post_bwd(sv["o"], sv["proj"], gdn_norm_w[j], dact, name=f"gdn_post_bwd{i}")
            dqkvn, dg_row, db_row, got = _gdn_chunk_bwd(sv["qkvn"], sv["g_row"], sv["b_row"], sv["states"], do,
                                                        name=f"gdn_chunk_bwd{i}",
                                                        comm=dist and dist.send_comm(f"chunk_bwd{i}", g_big))
            if dist:
                dist.got(f"chunk_bwd{i}", got)
            dproj, g_conv[j], got = _gdn_pre_bwd(sv["proj"], conv_full[j], dqkvn, dz, name=f"gdn_pre_bwd{i}",
                                                 comm=dist and dist.send_comm(f"pre_bwd{i}", g_big))
            if dist:
                dist.got(f"pre_bwd{i}", got)
            nh2 = 2 * GDN_HEADS
            da_, db_, g_alog[j], g_dt[j] = _gdn_gate_bwd(sv["ab"][:, :nh2], sv["ab"][:, nh2:], gdn_a_log[j], gdn_dt_bias[j],
                                                         dg_row.reshape(nh2, s).T, db_row.reshape(nh2, s).T,
                                                         name=f"gdn_gate_bwd{i}")
            dab = jnp.concatenate([da_, db_], axis=1)
            gw_main = _mm(sv["h"], dproj, ta=True, name=f"gdn_w_in_grad{i}", out_dtypes=(BF16,))
            gw_ab = _mm(sv["h"], dab, ta=True, name=f"gdn_w_ab_grad{i}", out_dtypes=(BF16,))
            g_big["gdn_w_in"][j] = jnp.concatenate([gw_main, gw_ab], axis=1)
            dh_ab = _mm(dab, w_in[:, GDN_MAIN:], tb=True, name=f"gdn_proj_ab_bwd{i}")
            comm = dist and dist.send_comm(f"gdn_proj_bwd{i}", g_big)
            res = _mm(dproj, w_in, b_cols=(0, GDN_MAIN), tb=True, name=f"gdn_proj_bwd{i}", out_dtypes=(F32, BF16), tm=512,
                      epilogue=lambda acc, r, x, dres, g: _ep_rms_bwd(acc + r, x, dres, g),
                      extras=(dh_ab, sv["x_in"], dmid), vecs=(row(norm_mix[i]),), vec_out=True, comm=comm)
            if comm:
                res, got = res
                dist.got(f"gdn_proj_bwd{i}", got)
            dcur, dcur_b, g_norm_mix[i] = res
        else:
            w_in, w_out = full["dswa_w_in"][j], full["dswa_w_out"][j]
            dact = _mm(dmid_b, w_out, tb=True, name=f"mix_out_bwd{i}")
            g_big["dswa_w_out"][j] = _mm(sv["act"], dmid_b, ta=True, name=f"mix_out_grad{i}", out_dtypes=(BF16,),
                                         shard="rows")
            do_n, corr_n = _dswa_combine_bwd(sv["o_n"], sv["lse_n"], dact, name=f"dswa_comb_bwd{i}")
            *dqkv, dbias, got = _dswa_attn_bwd(sv["qkv"], bias, sv["lse_n"], do_n, corr_n, name=f"dswa_attn_bwd{i}",
                                               comm=dist and dist.send_comm(f"attn_bwd{i}", g_big))
            if dist:
                dist.got(f"attn_bwd{i}", got)
            d_table_t = d_table_t + _dswa_dtable(dbias.reshape(DSWA_HEADS, -1), onehot, name=f"dswa_dtable{i}")
            g_big["dswa_w_in"][j] = jnp.concatenate(
                [_mm(sv["h"], dt, ta=True, name=f"dswa_w_in_grad{i}_{t}", out_dtypes=(BF16,)) for t, dt in enumerate(dqkv)],
                axis=1)
            cols = [(t * DSWA_WIDTH, DSWA_WIDTH) for t in range(3)]
            dh = _mm(dqkv[0], w_in, b_cols=cols[0], tb=True, name=f"dswa_proj_bwd{i}_0")
            dh = _mm(dqkv[1], w_in, b_cols=cols[1], tb=True, name=f"dswa_proj_bwd{i}_1",
                     epilogue=lambda acc, r: (acc + r,), extras=(dh,))
            dcur, dcur_b, g_norm_mix[i] = _mm(
                dqkv[2], w_in, b_cols=cols[2], tb=True, name=f"dswa_proj_bwd{i}_2", out_dtypes=(F32, BF16), tm=512,
                epilogue=lambda acc, r, x, dres, g: _ep_rms_bwd(acc + r, x, dres, g),
                extras=(dh, sv["x_in"], dmid), vecs=(row(norm_mix[i]),), vec_out=True)

    rep = dict(norm_mix=jnp.concatenate(g_norm_mix, axis=0), norm_mlp=jnp.concatenate(g_norm_mlp, axis=0),
               norm_final=dg_final.reshape(-1), rel_bias=d_table_t.T,
               gdn_a_log=jnp.stack(g_alog).reshape(gdn_a_log.shape), gdn_dt_bias=jnp.stack(g_dt).reshape(gdn_dt_bias.shape),
               gdn_norm_w=jnp.stack(g_nw).reshape(gdn_norm_w.shape))
    return loss_part, dcur, g_big, rep, g_conv
```

```python
import functools
import math

import jax
import jax.numpy as jnp
import numpy as np
from jax import lax
from jax.experimental import pallas as pl
from jax.experimental.pallas import tpu as pltpu

F32 = jnp.float32
BF16 = jnp.bfloat16
HP = lax.Precision.HIGHEST

N_DEV = 8
D_MODEL = 1024
DEPTH = 4
RMS_EPS = 1e-6
NEG_INF = -1e30

GDN_HEADS = 8
GDN_DK = 128
GDN_CONV = 5
GDN_CHUNK = 128
GDN_QKV = 3 * GDN_HEADS * GDN_DK
GDN_MAIN = GDN_QKV + GDN_HEADS * GDN_DK
GDN_AB = 4 * GDN_HEADS

DSWA_DILS = (1, 4, 16)
DSWA_HG = 6
DSWA_E = 64
DSWA_HEADS = 18
DSWA_WIDTH = DSWA_HEADS * DSWA_E
DSWA_HALF = 64
DSWA_PG = DSWA_HG // 2
DSWA_UNROLL = 8
REL_BUCKETS = 32
REL_MAX_DIST = 1024

ADAM_LR = 0.001
ADAM_B1 = 0.9
ADAM_B2 = 0.999
ADAM_EPS = 1e-08
ADAM_WD = 0.01
ADAM_STEP = 10

VMEM_LIMIT = 56 * 1024 * 1024


def _cparams(sem=None, **kw):
    return pltpu.CompilerParams(dimension_semantics=sem, vmem_limit_bytes=VMEM_LIMIT, **kw)


def _pick(dim, cands):
    for c in cands:
        if dim % c == 0:
            return c
    return dim


def _bdot(a, b):
    return jnp.dot(a.astype(BF16), b.astype(BF16), preferred_element_type=F32)


def _bdot_nt(a, b):
    return lax.dot_general(a.astype(BF16), b.astype(BF16), (((1,), (1,)), ((), ())),
                           preferred_element_type=F32)


def _bdot_tn(a, b):
    return lax.dot_general(a.astype(BF16), b.astype(BF16), (((0,), (0,)), ((), ())),
                           preferred_element_type=F32)


def _hdot(a, b):
    return jnp.dot(a, b, precision=HP, preferred_element_type=F32)


def _hdot_nt(a, b):
    return lax.dot_general(a, b, (((1,), (1,)), ((), ())), precision=HP, preferred_element_type=F32)


def _sigmoid(x):
    return 1.0 / (1.0 + jnp.exp(-x))


def _mm(a, b, *, name, ta=False, tb=False, out_dtypes=(F32,), epilogue=None, extras=(), vecs=(), vec_out=False,
        tm=None, tn=None, tk=None, shard=None, comm=None, b_cols=None):
    if ta:
        kdim, m = a.shape
    else:
        m, kdim = a.shape
    b0, bsz = b_cols or (0, b.shape[1])
    n = b.shape[0] if tb else bsz
    assert not tb or kdim == bsz
    if shard == "rows":
        tm = m // N_DEV if (m // N_DEV) % 128 == 0 else m
    if shard == "cols":
        tn = n // N_DEV
    tm = tm or _pick(m, (1024, 1152, 512, 384, 256, 128))
    tn = tn or _pick(n, (1024, 1152, 512, 384, 256, 128))
    tk = tk or _pick(kdim, ((2048,) if ta else ()) + (1024, 1152, 512, 384, 256, 128))
    nk = kdim // tk
    n_out = len(out_dtypes) + (1 if vec_out else 0)
    n_ex = len(extras) + len(vecs)
    rows_all = shard == "rows" and tm == m

    gi, gj = m // tm, n // tn

    def body(*refs):
        i, j, k = pl.program_id(0), pl.program_id(1), pl.program_id(2)
        inner = (j == 0) & (k == 0)
        ins, out_refs, (acc_ref,) = _comm_hooks(
            comm, refs, 2 + n_ex, n_out, 1, (i == 0) & inner, (i == (3 * gi) // 4) & inner,
            (i == gi - 1) & (j == gj - 1) & (k == nk - 1))
        a_ref, b_ref, ex_refs = ins[0], ins[1], ins[2:]
        if vec_out:
            out_refs, vec_ref = out_refs[:-1], out_refs[-1]

        @pl.when(k == 0)
        def _():
            acc_ref[...] = jnp.zeros_like(acc_ref)

        av = a_ref[...].astype(BF16)
        bv = b_ref[...].astype(BF16)
        dims = (((0 if ta else 1,), (1 if tb else 0,)), ((), ()))
        acc_ref[...] += lax.dot_general(av, bv, dims, preferred_element_type=F32)

        @pl.when(k == nk - 1)
        def _():
            acc = acc_ref[...]
            outs = (acc,) if epilogue is None else epilogue(acc, *[r[...] for r in ex_refs])
            if vec_out:
                part = outs[-1]

                @pl.when(i == 0)
                def _():
                    vec_ref[...] = part

                @pl.when(i > 0)
                def _():
                    vec_ref[...] += part
            for r, o in zip(out_refs, outs):
                if rows_all:
                    for p in range(N_DEV):
                        r[p] = o[p * (m // N_DEV):(p + 1) * (m // N_DEV)].astype(r.dtype)
                else:
                    r[...] = o.astype(r.dtype)

    a_spec = pl.BlockSpec((tk, tm), lambda i, j, k: (k, i)) if ta else pl.BlockSpec((tm, tk), lambda i, j, k: (i, k))
    assert b0 % (tk if tb else tn) == 0
    boff = b0 // (tk if tb else tn)
    b_spec = (pl.BlockSpec((tn, tk), lambda i, j, k: (j, k + boff)) if tb
              else pl.BlockSpec((tk, tn), lambda i, j, k: (k, j + boff)))
    o_spec = pl.BlockSpec((tm, tn), lambda i, j, k: (i, j))
    v_spec = pl.BlockSpec((1, tn), lambda i, j, k: (0, j))
    out_specs = [o_spec] * len(out_dtypes) + ([v_spec] if vec_out else [])
    out_shape = [jax.ShapeDtypeStruct((m, n), dt) for dt in out_dtypes]
    out_shape += [jax.ShapeDtypeStruct((1, n), F32)] if vec_out else []
    if shard == "rows":
        out_shape = [jax.ShapeDtypeStruct((N_DEV, m // N_DEV, n), out_dtypes[0])]
        out_specs = [pl.BlockSpec((N_DEV, m // N_DEV, tn), lambda i, j, k: (0, 0, j)) if rows_all
                     else pl.BlockSpec((None, tm, tn), lambda i, j, k: (i, 0, j))]
    if shard == "cols":
        out_shape = [jax.ShapeDtypeStruct((N_DEV, m, tn), out_dtypes[0])]
        out_specs = [pl.BlockSpec((None, tm, tn), lambda i, j, k: (j, i, 0))]
    c_in, c_out, c_shape, c_scr = _comm_specs(comm)
    outs = pl.pallas_call(
        body, name=name,
        grid=(gi, gj, nk),
        in_specs=[a_spec, b_spec] + [o_spec] * len(extras) + [v_spec] * len(vecs) + c_in,
        out_specs=out_specs + c_out,
        out_shape=out_shape + c_shape,
        scratch_shapes=[pltpu.VMEM((tm, tn), F32)] + c_scr,
        compiler_params=_cparams(("arbitrary",) * 3 if comm or vec_out else ("parallel", "parallel", "arbitrary")),
    )(a, b, *extras, *vecs, *(comm.arrays if comm else []))
    res = outs[0] if n_out == 1 else tuple(outs[:n_out])
    return (res, outs[n_out:]) if comm else res


def _rms_fwd(x, g, *, name):
    s, d = x.shape
    tr = _pick(s, (512, 256, 128))

    def body(x_ref, g_ref, h_ref):
        xv = x_ref[...]
        r = lax.rsqrt(jnp.mean(xv * xv, axis=-1, keepdims=True) + RMS_EPS)
        h_ref[...] = (xv * r * g_ref[...]).astype(h_ref.dtype)

    return pl.pallas_call(
        body, name=name, grid=(s // tr,),
        in_specs=[pl.BlockSpec((tr, d), lambda i: (i, 0)), pl.BlockSpec((1, d), lambda i: (0, 0))],
        out_specs=pl.BlockSpec((tr, d), lambda i: (i, 0)),
        out_shape=jax.ShapeDtypeStruct((s, d), BF16),
        compiler_params=_cparams(("parallel",)),
    )(x, g.reshape(1, d))


def _loss_head(x, g, target, *, name):
    s, d = x.shape
    tr = _pick(s, (512, 256, 128))

    def body(x_ref, g_ref, t_ref, loss_ref, dx_ref, dxb_ref, dg_ref):
        i = pl.program_id(0)
        xv = x_ref[...]
        gv = g_ref[...]
        r = lax.rsqrt(jnp.mean(xv * xv, axis=-1, keepdims=True) + RMS_EPS)
        xn = xv * r
        err = xn * gv - t_ref[...]
        lpart = 0.5 * jnp.sum(jnp.mean(err * err, axis=-1, keepdims=True), axis=0, keepdims=True)
        dy = err * (1.0 / d)
        dn = dy * gv
        dx = r * (dn - xn * jnp.mean(dn * xn, axis=-1, keepdims=True))
        dx_ref[...] = dx
        dxb_ref[...] = dx.astype(dxb_ref.dtype)
        gpart = jnp.sum(dy * xn, axis=0, keepdims=True)

        @pl.when(i == 0)
        def _():
            dg_ref[...] = gpart
            loss_ref[...] = lpart

        @pl.when(i > 0)
        def _():
            dg_ref[...] += gpart
            loss_ref[...] += lpart

    row = pl.BlockSpec((tr, d), lambda i: (i, 0))
    vec = pl.BlockSpec((1, d), lambda i: (0, 0))
    one = pl.BlockSpec((1, 1), lambda i: (0, 0))
    return pl.pallas_call(
        body, name=name, grid=(s // tr,),
        in_specs=[row, vec, row], out_specs=[one, row, row, vec],
        out_shape=[jax.ShapeDtypeStruct((1, 1), F32), jax.ShapeDtypeStruct((s, d), F32),
                   jax.ShapeDtypeStruct((s, d), BF16), jax.ShapeDtypeStruct((1, d), F32)],
        compiler_params=_cparams(("arbitrary",)),
    )(x, g.reshape(1, d), target)


def _shift_rows(x, sft, rows):
    s = x.shape[0]
    if sft == 0:
        return x
    y = pltpu.roll(x, (-sft) % s, 0)
    edge = slice(0, 8) if sft < 0 else slice(s - 8, s)
    ok = (rows[edge] + sft >= 0) & (rows[edge] + sft < s)
    fixed = jnp.where(ok, y[edge], 0.0)
    return jnp.concatenate([fixed, y[8:]] if sft < 0 else [y[:s - 8], fixed], axis=0)


def _gdn_pre_fwd(proj, conv_w, *, name):
    s = proj.shape[0]
    nblk = GDN_QKV // 128
    pad = GDN_CONV // 2

    def body(x_ref, w_ref, o_ref):
        j = pl.program_id(0)
        x = x_ref[...]
        rows = lax.broadcasted_iota(jnp.int32, x.shape, 0)
        c = jnp.zeros_like(x)
        for t in range(GDN_CONV):
            c = c + w_ref[pl.ds(t, 1), :] * _shift_rows(x, t - pad, rows)
        a = c * _sigmoid(c)
        rinv = lax.rsqrt(jnp.sum(a * a, axis=-1, keepdims=True) + 1e-6)
        scale = jnp.where(j < GDN_HEADS, GDN_DK ** -0.5, 1.0)
        o_ref[...] = jnp.where(j >= 2 * GDN_HEADS, a, a * (rinv * scale))

    return pl.pallas_call(
        body, name=name, grid=(nblk,),
        in_specs=[pl.BlockSpec((s, 128), lambda j: (0, j)), pl.BlockSpec((GDN_CONV, 128), lambda j: (0, j))],
        out_specs=pl.BlockSpec((s, 128), lambda j: (0, j)),
        out_shape=jax.ShapeDtypeStruct((s, GDN_QKV), F32),
        compiler_params=_cparams(("parallel",)),
    )(proj, conv_w)


def _gdn_pre_bwd(proj, conv_w, dqkv, dproj, *, name, comm=None):
    s = proj.shape[0]
    nblk = GDN_QKV // 128
    pad = GDN_CONV // 2

    def body(*refs):
        j = pl.program_id(0)
        (x_ref, w_ref, df_ref, dbk_ref, _), (dx_ref, dw_ref), _ = _comm_hooks(
            comm, refs, 5, 2, 0, j == 0, j == nblk // 2, j == nblk - 1)
        x = x_ref[...]
        rows = lax.broadcasted_iota(jnp.int32, x.shape, 0)
        xs = [_shift_rows(x, t - pad, rows) for t in range(GDN_CONV)]
        c = jnp.zeros_like(x)
        for t in range(GDN_CONV):
            c = c + w_ref[pl.ds(t, 1), :] * xs[t]
        sg = _sigmoid(c)
        a = c * sg
        rinv = lax.rsqrt(jnp.sum(a * a, axis=-1, keepdims=True) + 1e-6)
        scale = jnp.where(j < GDN_HEADS, GDN_DK ** -0.5, 1.0)
        dy = df_ref[...] + dbk_ref[...]
        nh = a * rinv
        da_n = (rinv * scale) * (dy - nh * jnp.sum(dy * nh, axis=-1, keepdims=True))
        da = jnp.where(j >= 2 * GDN_HEADS, dy, da_n)
        dc = da * (sg * (1.0 + c * (1.0 - sg)))
        dx = jnp.zeros_like(x)
        for t in range(GDN_CONV):
            dx = dx + w_ref[pl.ds(t, 1), :] * _shift_rows(dc, pad - t, rows)
            dw_ref[pl.ds(t, 1), :] = jnp.sum(dc * xs[t], axis=0, keepdims=True)
        dx_ref[...] = dx.astype(dx_ref.dtype)

    col = pl.BlockSpec((s, 128), lambda j: (0, j))
    wsp = pl.BlockSpec((GDN_CONV, 128), lambda j: (0, j))
    c_in, c_out, c_shape, c_scr = _comm_specs(comm)
    res = pl.pallas_call(
        body, name=name, grid=(nblk,),
        in_specs=[col, wsp, col, col, pl.BlockSpec(memory_space=pl.ANY)] + c_in, out_specs=[col, wsp] + c_out,
        out_shape=[jax.ShapeDtypeStruct(dproj.shape, BF16), jax.ShapeDtypeStruct((GDN_CONV, GDN_QKV), F32)] + c_shape,
        input_output_aliases={4: 0},
        scratch_shapes=c_scr,
        compiler_params=_cparams(("arbitrary",) if comm else ("parallel",)),
    )(proj, conv_w, dqkv[0], dqkv[1], dproj, *(comm.arrays if comm else []))
    return res[0], res[1], res[2:]


def _softplus(x):
    return jnp.maximum(x, 0.0) + jnp.log(1.0 + jnp.exp(-jnp.abs(x)))


def _gdn_gate_fwd(a, b, a_log, dt_bias, *, name):
    s = a.shape[0]
    nh = 2 * GDN_HEADS

    def body(a_ref, b_ref, al_ref, dt_ref, g_ref, be_ref):
        g_ref[...] = -jnp.exp(al_ref[...]) * _softplus(a_ref[...] + dt_ref[...])
        be_ref[...] = _sigmoid(b_ref[...])

    return pl.pallas_call(
        body, name=name,
        out_shape=[jax.ShapeDtypeStruct((s, nh), F32), jax.ShapeDtypeStruct((s, nh), F32)],
        compiler_params=_cparams(),
    )(a, b, a_log.reshape(1, nh), dt_bias.reshape(1, nh))


def _gdn_gate_bwd(a, b, a_log, dt_bias, dg, dbeta, *, name):
    s = a.shape[0]
    nh = 2 * GDN_HEADS

    def body(a_ref, b_ref, al_ref, dt_ref, dg_ref, db_ref, da_ref, dbb_ref, dal_ref, ddt_ref):
        ea = jnp.exp(al_ref[...])
        z = a_ref[...] + dt_ref[...]
        dgv = dg_ref[...]
        dz = dgv * (-ea) * _sigmoid(z)
        dal_ref[...] = jnp.sum(dgv * (-ea) * _softplus(z), axis=0, keepdims=True)
        ddt_ref[...] = jnp.sum(dz, axis=0, keepdims=True)
        sb = _sigmoid(b_ref[...])
        da_ref[...] = dz
        dbb_ref[...] = db_ref[...] * sb * (1.0 - sb)

    return pl.pallas_call(
        body, name=name,
        out_shape=[jax.ShapeDtypeStruct((s, nh), F32), jax.ShapeDtypeStruct((s, nh), F32),
                   jax.ShapeDtypeStruct((1, nh), F32), jax.ShapeDtypeStruct((1, nh), F32)],
        compiler_params=_cparams(),
    )(a, b, a_log.reshape(1, nh), dt_bias.reshape(1, nh), dg, dbeta)


def _chunk_masks(d):
    c = GDN_CHUNK
    ii = lax.broadcasted_iota(jnp.int32, (c, c), 0)
    jj = lax.broadcasted_iota(jnp.int32, (c, c), 1)
    dif = (ii - jj) * (1 - 2 * d)
    mi = dif >= 0
    mit = dif <= 0
    ms = dif > 0
    eye = ii == jj
    bds = [(ii >> sh) == (jj >> sh) for sh in range(3, c.bit_length() - 1)]
    return dict(mi=mi, mit=mit, ms=ms, eye=eye, bds=bds,
                mif=mi.astype(F32), mitf=mit.astype(F32), eyef=eye.astype(F32))


class _V:
    def __init__(self, xs):
        self.xs = tuple(xs)

    def __add__(self, o):
        return _lift(lambda a, b: a + b)(self, o)

    def __radd__(self, o):
        return _lift(lambda a, b: b + a)(self, o)

    def __sub__(self, o):
        return _lift(lambda a, b: a - b)(self, o)

    def __rsub__(self, o):
        return _lift(lambda a, b: b - a)(self, o)

    def __mul__(self, o):
        return _lift(lambda a, b: a * b)(self, o)

    def __rmul__(self, o):
        return _lift(lambda a, b: b * a)(self, o)

    def __and__(self, o):
        return _lift(lambda a, b: a & b)(self, o)

    def __neg__(self):
        return _lift(lambda a: -a)(self)

    def __rtruediv__(self, o):
        return _lift(lambda a, b: b / a)(self, o)


def _lift(f):
    def g(*args, **kw):
        n = next(len(a.xs) for a in args if isinstance(a, _V))
        return _V(f(*[a.xs[i] if isinstance(a, _V) else a for a in args], **kw) for i in range(n))
    return g


_vwhere, _vsum, _vexp, _vnot = _lift(jnp.where), _lift(jnp.sum), _lift(jnp.exp), _lift(jnp.logical_not)
_vbdot, _vbdot_nt, _vbdot_tn = _lift(_bdot), _lift(_bdot_nt), _lift(_bdot_tn)
_vcat = _lift(lambda a, b: jnp.concatenate([a, b], axis=1))
_vlo = _lift(lambda a, n: a[:, :n])
_vhi = _lift(lambda a, n: a[:, n:])


def _both_masks(n):
    m = [_chunk_masks(d) for d in range(2)]
    mk = {key: _V([m[0][key]] * n + [m[1][key]] * n) for key in m[0] if key != "bds"}
    mk["bds"] = [_V([m[0]["bds"][i]] * n + [m[1]["bds"][i]] * n) for i in range(len(m[0]["bds"]))]
    return mk


def _tri_inv(a, mk):
    eyef = mk["eyef"]
    bds = mk["bds"]
    a8 = _vwhere(bds[0], a, 0.0)
    a2 = _vbdot(a8, a8)
    a4 = _vbdot(a2, a2)
    t = _vbdot(_vbdot(eyef - a8, eyef + a2), eyef + a4)
    for inner, outer in zip(bds, bds[1:] + [None]):
        off = _vnot(inner) if outer is None else (outer & _vnot(inner))
        low = _vwhere(off, a, 0.0)
        t = t - _vbdot(_vbdot(t, low), t)
    return t


def _chunk_prep(q, k, v, g_row, b_row, mk, tuw=None):
    dv = GDN_DK
    g_col = _vsum(mk["eyef"] * g_row, axis=1, keepdims=True)
    b_col = _vsum(mk["eyef"] * b_row, axis=1, keepdims=True)
    gc_col = _vsum(mk["mif"] * g_row, axis=1, keepdims=True)
    gc_row = _vsum(mk["mitf"] * g_col, axis=0, keepdims=True)
    gl = _vsum(g_row, axis=1, keepdims=True)
    decay = _vwhere(mk["mi"], _vexp(_vwhere(mk["mi"], gc_col - gc_row, 0.0)), 0.0)
    eg = _vexp(gc_col)
    e2 = _vexp(gl - gc_col)
    egl = _vexp(gl)
    kb = k * b_col
    pm = _vbdot_nt(kb, k)
    if tuw is None:
        t = _tri_inv(_vwhere(mk["ms"], pm * decay, 0.0), mk)
        sol = _vbdot(t, _vcat(v * b_col, kb * eg))
        u, w = _vlo(sol, dv), _vhi(sol, dv)
    else:
        t, u, w = tuw
    qm = _vbdot_nt(q, k)
    return dict(b_col=b_col, decay=decay, eg=eg, e2=e2, egl=egl, kb=kb, pm=pm, t=t, u=u, w=w,
                qm=qm, intra=qm * decay, qd=q * eg, kd=k * e2)


def _chunk_fwd_step(p, state):
    v_new = p["u"] - _vbdot(p["w"], state)
    o = _vbdot(p["qd"], state) + _vbdot(p["intra"], v_new)
    new_state = state * p["egl"] + _vbdot_tn(p["kd"], v_new)
    return o, new_state


def _chunk_bwd_step(q, k, v, p, mk, state, dso, do):
    dv_dim = GDN_DK
    v_new = p["u"] - _vbdot(p["w"], state)
    dvn = _vbdot_tn(p["intra"], do) + _vbdot(p["kd"], dso)
    dintra = _vbdot_nt(do, v_new)
    dqd = _vbdot_nt(do, state)
    ds = p["egl"] * dso + _vbdot_tn(p["qd"], do) - _vbdot_tn(p["w"], dvn)
    dkd = _vbdot_nt(v_new, dso)
    dgl = _vsum(_vsum(dso * state, axis=1, keepdims=True), axis=0, keepdims=True) * p["egl"]
    dw = -_vbdot_nt(dvn, state)
    drhs = _vbdot_tn(p["t"], _vcat(dvn, dw))
    dru, drw = _vlo(drhs, dv_dim), _vhi(drhs, dv_dim)
    da = -_vwhere(mk["ms"], _vbdot_nt(drhs, _vcat(p["u"], p["w"])), 0.0)
    b_col = p["b_col"]
    dv = dru * b_col
    dbeta = _vsum(dru * v, axis=1, keepdims=True)
    dkb = drw * p["eg"]
    deg = _vsum(drw * p["kb"], axis=1, keepdims=True)
    dp = da * p["decay"]
    ddecay = da * p["pm"]
    dkb = dkb + _vbdot(dp, k)
    dk = _vbdot_tn(dp, p["kb"])
    dqm = dintra * p["decay"]
    ddecay = ddecay + dintra * p["qm"]
    dq = _vbdot(dqm, k)
    dk = dk + _vbdot_tn(dqm, q)
    dd = ddecay * p["decay"]
    dgc_col = _vsum(dd, axis=1, keepdims=True)
    dgc_row = -_vsum(dd, axis=0, keepdims=True)
    dq = dq + dqd * p["eg"]
    deg = deg + _vsum(dqd * q, axis=1, keepdims=True)
    dk = dk + dkd * p["e2"]
    de2 = _vsum(dkd * k, axis=1, keepdims=True) * p["e2"]
    dgl = dgl + _vsum(de2, axis=0, keepdims=True)
    dgc_col = dgc_col - de2 + deg * p["eg"]
    dk = dk + dkb * b_col
    dbeta = dbeta + _vsum(dkb * k, axis=1, keepdims=True)
    dgc_col = dgc_col + _vsum(mk["eyef"] * dgc_row, axis=1, keepdims=True)
    dg_row = _vsum(mk["mif"] * dgc_col, axis=0, keepdims=True) + dgl
    dbeta_row = _vsum(mk["eyef"] * dbeta, axis=0, keepdims=True)
    return dq, dk, dv, dg_row, dbeta_row, ds


def _gdn_chunk_fwd(qkvn, g5, b5, *, name, comm=None):
    s = qkvn.shape[0]
    c = GDN_CHUNK
    nc = s // c
    h_, dk = GDN_HEADS, GDN_DK

    def body(*refs):
        n = pl.program_id(0)
        ins, outs, (st_scr,) = _comm_hooks(comm, refs, 6, 10, 1, n == 0, n == (3 * nc) // 4, n == nc - 1)
        x_refs, g_refs, b_refs = ins[0:2], ins[2:4], ins[4:6]
        o_refs, st_refs, t_refs, u_refs, w_refs = outs[0:2], outs[2:4], outs[4:6], outs[6:8], outs[8:10]

        @pl.when(n == 0)
        def _():
            st_scr[...] = jnp.zeros_like(st_scr)

        ch = [(d, h) for d in range(2) for h in range(h_)]
        mk = _both_masks(h_)
        q, k, v = (_V(x_refs[d][:, (t * h_ + h) * dk:(t * h_ + h + 1) * dk] for d, h in ch) for t in range(3))
        g, b = (_V(r[d][0, h, 0] for d, h in ch) for r in (g_refs, b_refs))
        state = _V(st_scr[d * h_ + h] for d, h in ch)
        p = _chunk_prep(q, k, v, g, b, mk)
        o, new_state = _chunk_fwd_step(p, state)
        for i, (d, h) in enumerate(ch):
            st_refs[d][h, 0] = state.xs[i]
            st_scr[d * h_ + h] = new_state.xs[i]
            o_refs[d][:, h * dk:(h + 1) * dk] = o.xs[i]
            t_refs[d][h, 0] = p["t"].xs[i].astype(BF16)
            u_refs[d][h, 0] = p["u"].xs[i]
            w_refs[d][h, 0] = p["w"].xs[i].astype(BF16)

    ce = (lambda n: n, lambda n: nc - 1 - n)
    xs = [pl.BlockSpec((c, 3 * h_ * dk), lambda n, d=d: (ce[d](n), 0)) for d in range(2)]
    gates = [pl.BlockSpec((1, h_, 1, 1, c), lambda n, d=d: (d, 0, ce[d](n), 0, 0)) for d in range(2)]
    os_ = [pl.BlockSpec((c, h_ * dk), lambda n, d=d: (ce[d](n), 0)) for d in range(2)]
    sts = [pl.BlockSpec((h_, 1, dk, dk), lambda n, d=d: (0, ce[d](n), 0, 0)) for d in range(2)]
    tcc = [pl.BlockSpec((h_, 1, c, c), lambda n, d=d: (0, ce[d](n), 0, 0)) for d in range(2)]
    tck = [pl.BlockSpec((h_, 1, c, dk), lambda n, d=d: (0, ce[d](n), 0, 0)) for d in range(2)]
    per_chunk = lambda last, dt: [jax.ShapeDtypeStruct((h_, nc, c, last), dt)] * 2
    c_in, c_out, c_shape, c_scr = _comm_specs(comm)
    res = pl.pallas_call(
        body, name=name, grid=(nc,),
        in_specs=xs + gates + gates + c_in,
        out_specs=os_ + sts + tcc + tck + tck + c_out,
        out_shape=[jax.ShapeDtypeStruct((s, h_ * dk), F32)] * 2 + [jax.ShapeDtypeStruct((h_, nc, dk, dk), F32)] * 2
        + per_chunk(c, BF16) + per_chunk(dk, F32) + per_chunk(dk, BF16) + c_shape,
        scratch_shapes=[pltpu.VMEM((2 * h_, dk, dk), F32)] + c_scr,
        compiler_params=_cparams(("arbitrary",)),
    )(qkvn, qkvn, g5, g5, b5, b5, *(comm.arrays if comm else []))
    return res[0:2], res[2:10], res[10:]


def _gdn_chunk_bwd(qkvn, g5, b5, states, do, *, name, comm=None):
    s = qkvn.shape[0]
    c = GDN_CHUNK
    nc = s // c
    h_, dk = GDN_HEADS, GDN_DK

    def body(*refs):
        i = pl.program_id(0)
        ins, outs, (ds_scr,) = _comm_hooks(comm, refs, 16, 6, 1, i == 0, i == nc // 2, i == nc - 1)
        x_refs, g_refs, b_refs, st_refs = ins[0:2], ins[2:4], ins[4:6], ins[6:8]
        t_refs, u_refs, w_refs, do_refs = ins[8:10], ins[10:12], ins[12:14], ins[14:16]
        dx_refs, dg_refs, db_refs = outs[0:2], outs[2:4], outs[4:6]

        @pl.when(i == 0)
        def _():
            ds_scr[...] = jnp.zeros_like(ds_scr)

        ch = [(d, h) for d in range(2) for h in range(h_)]
        mk = _both_masks(h_)
        q, k, v = (_V(x_refs[d][:, (t * h_ + h) * dk:(t * h_ + h + 1) * dk] for d, h in ch) for t in range(3))
        g, b = (_V(r[d][0, h, 0] for d, h in ch) for r in (g_refs, b_refs))
        state = _V(st_refs[d][h, 0] for d, h in ch)
        dso = _V(ds_scr[d * h_ + h] for d, h in ch)
        dov = _V(do_refs[d][:, h * dk:(h + 1) * dk] for d, h in ch)
        tuw = tuple(_V(r[d][h, 0] for d, h in ch) for r in (t_refs, u_refs, w_refs))
        res = _chunk_bwd_step(q, k, v, _chunk_prep(q, k, v, g, b, mk, tuw), mk, state, dso, dov)
        for (d, h), (dq, dkk, dvv, dg_r, db_r, ds) in zip(ch, zip(*[r.xs for r in res])):
            ds_scr[d * h_ + h] = ds
            dg_refs[d][h, 0] = dg_r
            db_refs[d][h, 0] = db_r
            for t, val in enumerate((dq, dkk, dvv)):
                dx_refs[d][:, (t * h_ + h) * dk:(t * h_ + h + 1) * dk] = val

    ce = (lambda i: nc - 1 - i, lambda i: i)
    both = lambda mk_spec: [mk_spec(d) for d in range(2)]
    xs = both(lambda d: pl.BlockSpec((c, 3 * h_ * dk), lambda i: (ce[d](i), 0)))
    gates = both(lambda d: pl.BlockSpec((1, h_, 1, 1, c), lambda i: (d, 0, ce[d](i), 0, 0)))
    sts = both(lambda d: pl.BlockSpec((h_, 1, dk, dk), lambda i: (0, ce[d](i), 0, 0)))
    tcc = both(lambda d: pl.BlockSpec((h_, 1, c, c), lambda i: (0, ce[d](i), 0, 0)))
    tck = both(lambda d: pl.BlockSpec((h_, 1, c, dk), lambda i: (0, ce[d](i), 0, 0)))
    dos = both(lambda d: pl.BlockSpec((c, h_ * dk), lambda i: (ce[d](i), 0)))
    gouts = both(lambda d: pl.BlockSpec((h_, 1, 1, c), lambda i: (0, ce[d](i), 0, 0)))
    c_in, c_out, c_shape, c_scr = _comm_specs(comm)
    res = pl.pallas_call(
        body, name=name, grid=(nc,),
        in_specs=xs + gates + gates + sts + tcc + tck + tck + dos + c_in,
        out_specs=xs + gouts + gouts + c_out,
        out_shape=[jax.ShapeDtypeStruct((s, 3 * h_ * dk), F32)] * 2
        + [jax.ShapeDtypeStruct((h_, nc, 1, c), F32)] * 4 + c_shape,
        scratch_shapes=[pltpu.VMEM((2 * h_, dk, dk), F32)] + c_scr,
        compiler_params=_cparams(("arbitrary",)),
    )(qkvn, qkvn, g5, g5, b5, b5, *states, do, do, *(comm.arrays if comm else []))
    return res[0:2], jnp.stack(res[2:4]), jnp.stack(res[4:6]), res[6:]


def _gdn_post_fwd(o, z, norm_w, *, name):
    s = o[0].shape[0]
    h_, dk = GDN_HEADS, GDN_DK

    def body(of_ref, ob_ref, z_ref, w_ref, a_ref):
        ov = of_ref[...] + ob_ref[...]
        zv = z_ref[...]
        r = lax.rsqrt(jnp.mean(ov * ov, axis=-1, keepdims=True) + RMS_EPS)
        a_ref[...] = (ov * r * w_ref[...] * (zv * _sigmoid(zv))).astype(a_ref.dtype)

    col = pl.BlockSpec((s, dk), lambda h: (0, h))
    return pl.pallas_call(
        body, name=name, grid=(h_,),
        in_specs=[col, col, pl.BlockSpec((s, dk), lambda h: (0, 3 * h_ + h)), pl.BlockSpec((1, dk), lambda h: (0, 0))],
        out_specs=col,
        out_shape=jax.ShapeDtypeStruct((s, h_ * dk), BF16),
        compiler_params=_cparams(("parallel",)),
    )(o[0], o[1], z, norm_w.reshape(1, dk))


def _gdn_post_bwd(o, z, norm_w, dact, *, name):
    s = o[0].shape[0]
    h_, dk = GDN_HEADS, GDN_DK

    def body(of_ref, ob_ref, z_ref, w_ref, da_ref, do_ref, dz_ref, dw_ref):
        h = pl.program_id(0)
        ov = of_ref[...] + ob_ref[...]
        zv = z_ref[...]
        wv = w_ref[...]
        dav = da_ref[...]
        r = lax.rsqrt(jnp.mean(ov * ov, axis=-1, keepdims=True) + RMS_EPS)
        nrm = ov * r
        sg = _sigmoid(zv)
        sz = zv * sg
        dn = dav * wv * sz
        do_ref[...] = r * (dn - nrm * jnp.mean(dn * nrm, axis=-1, keepdims=True))
        dz_ref[...] = (dav * nrm * wv * (sg * (1.0 + zv * (1.0 - sg)))).astype(dz_ref.dtype)
        part = jnp.sum(dav * nrm * sz, axis=0, keepdims=True)

        @pl.when(h == 0)
        def _():
            dw_ref[...] = part

        @pl.when(h > 0)
        def _():
            dw_ref[...] += part

    col = pl.BlockSpec((s, dk), lambda h: (0, h))
    vec = pl.BlockSpec((1, dk), lambda h: (0, 0))
    return pl.pallas_call(
        body, name=name, grid=(h_,),
        in_specs=[col, col, pl.BlockSpec((s, dk), lambda h: (0, 3 * h_ + h)), vec, col],
        out_specs=[col, pl.BlockSpec((s, dk), lambda h: (0, 3 * h_ + h)), vec],
        out_shape=[jax.ShapeDtypeStruct((s, h_ * dk), F32), jax.ShapeDtypeStruct((s, GDN_MAIN), BF16),
                   jax.ShapeDtypeStruct((1, dk), F32)],
        compiler_params=_cparams(("arbitrary",)),
    )(o[0], o[1], z, norm_w.reshape(1, dk), dact)


def _rel_bucket(rel):
    nb = REL_BUCKETS // 2
    max_exact = nb // 2
    ret = jnp.where(rel > 0, nb, 0)
    n = jnp.abs(rel)
    nf = jnp.maximum(n, 1).astype(F32)
    large = max_exact + (jnp.log(nf / max_exact) / math.log(REL_MAX_DIST / max_exact)
                         * (nb - max_exact)).astype(jnp.int32)
    large = jnp.minimum(large, nb - 1)
    return ret + jnp.where(n < max_exact, n, large)


def _bucket_onehot():
    half = DSWA_HALF
    outs = []
    for dil in DSWA_DILS:
        rel = (jnp.arange(3 * half)[None, :] - half - jnp.arange(half)[:, None]) * dil
        outs.append(jax.nn.one_hot(_rel_bucket(rel).reshape(-1), REL_BUCKETS, dtype=F32, axis=0))
    return jnp.stack(outs)


def _head_group_select(vals):
    rows = lax.broadcasted_iota(jnp.int32, vals[0].shape, 0)
    return jnp.where(rows < DSWA_HG, vals[0], jnp.where(rows < 2 * DSWA_HG, vals[1], vals[2]))


def _dswa_bias(table_t, onehot, *, name):
    p = onehot.shape[-1]

    def body(t_ref, oh_ref, b_ref):
        b_ref[...] = _head_group_select([_hdot(t_ref[...], oh_ref[g]) for g in range(3)])

    return pl.pallas_call(body, name=name, out_shape=jax.ShapeDtypeStruct((DSWA_HEADS, p), F32),
                          compiler_params=_cparams())(table_t, onehot)


def _dswa_dtable(dbias, onehot, *, name):
    def body(d_ref, oh_ref, t_ref):
        t_ref[...] = _head_group_select([_hdot_nt(d_ref[...], oh_ref[g]) for g in range(3)])

    return pl.pallas_call(body, name=name, out_shape=jax.ShapeDtypeStruct((DSWA_HEADS, REL_BUCKETS), F32),
                          compiler_params=_cparams())(dbias, onehot)


def _rows(start, dil):
    if dil == 1:
        return pl.ds(pl.multiple_of(start, DSWA_HALF), DSWA_HALF)
    return pl.ds(start, DSWA_HALF, stride=dil)


def _attn_blocks(it, s, dil):
    half = DSWA_HALF
    nbs = s // half // dil
    ii = lax.broadcasted_iota(jnp.int32, (half, 3 * half), 0)
    jj = lax.broadcasted_iota(jnp.int32, (half, 3 * half), 1)
    band = jnp.abs(jj - half - ii) <= half
    out = []
    for u in range(DSWA_UNROLL):
        blk = it * DSWA_UNROLL + u
        r, b = blk // nbs, blk % nbs
        own = r + dil * half * b
        prev = own - jnp.where(b > 0, dil * half, 0)
        nxt = own + jnp.where(b < nbs - 1, dil * half, 0)
        ok = band & ((jj >= half) | (b > 0)) & ((jj < 2 * half) | (b < nbs - 1))
        out.append(((prev, own, nxt), ok))
    return out


def _attn_chains(q_ref, k_ref, v_ref, blocks, dil):
    lane = lax.broadcasted_iota(jnp.int32, (DSWA_HALF, 2 * DSWA_E), 1)
    qm, kw, vw, valid, hmask = [], [], [], [], []
    for (prev, own, nxt), ok in blocks:
        q = q_ref[_rows(own, dil), :].astype(BF16)
        k = jnp.concatenate([k_ref[_rows(st, dil), :] for st in (prev, own, nxt)], axis=0).astype(BF16)
        v = jnp.concatenate([v_ref[_rows(st, dil), :] for st in (prev, own, nxt)], axis=0).astype(BF16)
        for hd in range(2):
            mine = (lane < DSWA_E) if hd == 0 else (lane >= DSWA_E)
            qm.append(jnp.where(mine, q, jnp.zeros_like(q)))
            kw.append(k)
            vw.append(v)
            valid.append(ok)
            hmask.append(mine)
    return _V(qm), _V(kw), _V(vw), _V(valid), _V(hmask)


def _per_group(pr, fn):
    for gi, dil in enumerate(DSWA_DILS):
        pl.when(pr // DSWA_PG == gi)(functools.partial(fn, dil))


_vmax, _vlog = _lift(jnp.max), _lift(jnp.log)


def _dswa_attn_fwd(qkv, bias, *, name, comm=None):
    s = qkv.shape[0]
    half, e = DSWA_HALF, DSWA_E
    npair = DSWA_HEADS // 2

    def body(*refs):
        pr = pl.program_id(0)
        (q_ref, k_ref, v_ref, bias_ref), (o_ref, lse_ref), _ = _comm_hooks(
            comm, refs, 4, 2, 0, pr == 0, pr == (3 * npair) // 4, pr == npair - 1)
        bias_v = _V([bias_ref[0], bias_ref[1]] * DSWA_UNROLL)

        def run(dil):
            def step(it, carry):
                blocks = _attn_blocks(it, s, dil)
                qm, kw, vw, valid, hmask = _attn_chains(q_ref, k_ref, v_ref, blocks, dil)
                sc = _vwhere(valid, _vbdot_nt(qm, kw) * (e ** -0.5) + bias_v, NEG_INF)
                m = _vmax(sc, axis=-1, keepdims=True)
                p = _vexp(sc - m)
                l = _vsum(p, axis=-1, keepdims=True)
                o = _vbdot(p * (1.0 / l), vw)
                lse = m + _vlog(l)
                for u, ((_, own, _), _) in enumerate(blocks):
                    is_a = hmask.xs[2 * u]
                    o_ref[_rows(own, dil), :] = jnp.where(is_a, o.xs[2 * u], o.xs[2 * u + 1])
                    lse_ref[_rows(own, dil), :] = jnp.where(is_a, lse.xs[2 * u], lse.xs[2 * u + 1])
                return carry

            lax.fori_loop(0, s // half // DSWA_UNROLL, step, 0)

        _per_group(pr, run)

    col = lambda t: pl.BlockSpec((s, 2 * e), lambda p: (0, t * npair + p))
    pair = pl.BlockSpec((s, 2 * e), lambda p: (0, p))
    c_in, c_out, c_shape, c_scr = _comm_specs(comm)
    res = pl.pallas_call(
        body, name=name, grid=(npair,),
        in_specs=[col(0), col(1), col(2), pl.BlockSpec((2, half, 3 * half), lambda p: (p, 0, 0))] + c_in,
        out_specs=[pair, pair] + c_out,
        out_shape=[jax.ShapeDtypeStruct((s, npair * 2 * e), F32)] * 2 + c_shape,
        scratch_shapes=c_scr,
        compiler_params=_cparams(("arbitrary",)),
    )(qkv, qkv, qkv, bias, *(comm.arrays if comm else []))
    return res[0], res[1], res[2:]


def _dswa_attn_bwd(qkv, bias, lse, do, corr, *, name, comm=None):
    s = qkv.shape[0]
    half, e = DSWA_HALF, DSWA_E
    npair = DSWA_HEADS // 2
    w = 2 * e

    def body(*refs):
        pr = pl.program_id(0)
        (q_ref, k_ref, v_ref, bias_ref, lse_ref, do_ref, corr_ref), (dq_ref, dk_ref, dv_ref, db_ref), _ = _comm_hooks(
            comm, refs, 7, 4, 0, pr == 0, pr == npair // 2, pr == npair - 1)
        bias_v = _V([bias_ref[0], bias_ref[1]] * DSWA_UNROLL)
        dk_ref[...] = jnp.zeros_like(dk_ref)
        dv_ref[...] = jnp.zeros_like(dv_ref)

        def run(dil):
            def step(it, dbias):
                blocks = _attn_blocks(it, s, dil)
                qm, kw, vw, valid, hmask = _attn_chains(q_ref, k_ref, v_ref, blocks, dil)
                hd = [0, 1] * DSWA_UNROLL
                rows = [_rows(own, dil) for (_, own, _), _ in blocks for _ in range(2)]
                lse_c = _V(lse_ref[rw, :][:, h * e:h * e + 1] for rw, h in zip(rows, hd))
                corr_c = _V(corr_ref[rw, :][:, h * e:h * e + 1] for rw, h in zip(rows, hd))
                dov = _vwhere(hmask, _V(do_ref[rw, :] for rw in rows), 0.0)
                sc = _vbdot_nt(qm, kw) * (e ** -0.5) + bias_v
                p = _vwhere(valid, _vexp(_vwhere(valid, sc, 0.0) - lse_c), 0.0)
                dsc = p * (_vbdot_nt(dov, vw) + corr_c)
                dq = _vbdot(dsc, kw) * (e ** -0.5)
                dkc = _vbdot_tn(dsc, qm) * (e ** -0.5)
                dvc = _vbdot_tn(p, dov)
                for u, (starts, _) in enumerate(blocks):
                    dq_ref[_rows(starts[1], dil), :] = jnp.where(hmask.xs[2 * u], dq.xs[2 * u], dq.xs[2 * u + 1])
                    dk_u = dkc.xs[2 * u] + dkc.xs[2 * u + 1]
                    dv_u = dvc.xs[2 * u] + dvc.xs[2 * u + 1]
                    for t, st in enumerate(starts):
                        dk_ref[_rows(st, dil), :] += dk_u[t * half:(t + 1) * half]
                        dv_ref[_rows(st, dil), :] += dv_u[t * half:(t + 1) * half]
                da, db = dbias
                for u in range(DSWA_UNROLL):
                    da, db = da + dsc.xs[2 * u], db + dsc.xs[2 * u + 1]
                return da, db

            zero = jnp.zeros((half, 3 * half), F32)
            da, db = lax.fori_loop(0, s // half // DSWA_UNROLL, step, (zero, zero))
            db_ref[0] = da
            db_ref[1] = db

        _per_group(pr, run)

    col = lambda t: pl.BlockSpec((s, w), lambda p: (0, t * npair + p))
    ps = pl.BlockSpec((s, w), lambda p: (0, p))
    bs = pl.BlockSpec((2, half, 3 * half), lambda p: (p, 0, 0))
    c_in, c_out, c_shape, c_scr = _comm_specs(comm)
    res = pl.pallas_call(
        body, name=name, grid=(npair,),
        in_specs=[col(0), col(1), col(2), bs, ps, ps, ps] + c_in,
        out_specs=[ps, ps, ps, bs] + c_out,
        out_shape=[jax.ShapeDtypeStruct((s, npair * w), F32)] * 3
        + [jax.ShapeDtypeStruct((DSWA_HEADS, half, 3 * half), F32)] + c_shape,
        scratch_shapes=c_scr,
        compiler_params=_cparams(("arbitrary",)),
    )(qkv, qkv, qkv, bias, lse, do, corr, *(comm.arrays if comm else []))
    return res[0], res[1], res[2], res[3], res[4:]


def _pair_cols(g, j):
    w = 2 * DSWA_E
    return slice((g * DSWA_PG + j) * w, (g * DSWA_PG + j + 1) * w)


def _group_weights(l_ref, j):
    ls = [l_ref[:, _pair_cols(g, j)] for g in range(3)]
    m = jnp.maximum(jnp.maximum(ls[0], ls[1]), ls[2])
    es = [jnp.exp(x - m) for x in ls]
    inv = 1.0 / (es[0] + es[1] + es[2])
    return [x * inv for x in es]


def _dswa_combine_fwd(o, lse, *, name):
    s, wd = o.shape
    tr = _pick(s, (512, 256, 128))

    def body(o_ref, l_ref, c_ref):
        for j in range(DSWA_PG):
            al = _group_weights(l_ref, j)
            for g in range(3):
                c_ref[:, _pair_cols(g, j)] = (o_ref[:, _pair_cols(g, j)] * al[g]).astype(c_ref.dtype)

    row = pl.BlockSpec((tr, wd), lambda i: (i, 0))
    return pl.pallas_call(
        body, name=name, grid=(s // tr,),
        in_specs=[row, row], out_specs=row,
        out_shape=jax.ShapeDtypeStruct(o.shape, BF16),
        compiler_params=_cparams(("parallel",)),
    )(o, lse)


def _dswa_combine_bwd(o, lse, dc, *, name):
    s, wd = o.shape
    tr = _pick(s, (512, 256, 128))

    def body(o_ref, l_ref, dc_ref, do_ref, corr_ref):
        lane = lax.broadcasted_iota(jnp.int32, (tr, 2 * DSWA_E), 1)
        is_a = lane < DSWA_E
        for j in range(DSWA_PG):
            al = _group_weights(l_ref, j)
            tot = jnp.zeros((tr, 2 * DSWA_E), F32)
            for g in range(3):
                cols = _pair_cols(g, j)
                dcv = dc_ref[:, cols]
                do_ref[:, cols] = dcv * al[g]
                prod = dcv * o_ref[:, cols]
                dal = jnp.where(is_a, jnp.sum(jnp.where(is_a, prod, 0.0), axis=-1, keepdims=True),
                                jnp.sum(jnp.where(is_a, 0.0, prod), axis=-1, keepdims=True))
                tot = tot + al[g] * dal
            for g in range(3):
                corr_ref[:, _pair_cols(g, j)] = -al[g] * tot

    row = pl.BlockSpec((tr, wd), lambda i: (i, 0))
    return pl.pallas_call(
        body, name=name, grid=(s // tr,),
        in_specs=[row, row, row], out_specs=[row, row],
        out_shape=[jax.ShapeDtypeStruct(o.shape, F32)] * 2,
        compiler_params=_cparams(("parallel",)),
    )(o, lse, dc)


class _Comm:
    def __init__(self, mode, arrays, kinds=None):
        self.mode, self.arrays, self.kinds = mode, list(arrays), kinds
        self.n = len(self.arrays)

    def out_shapes(self):
        if self.mode == "exchange":
            return [jax.ShapeDtypeStruct(x.shape, x.dtype) for x in self.arrays]
        shapes = []
        for x, kd in zip(self.arrays, self.kinds):
            shp = list(x.shape)
            if kd == "stack":
                shp = [N_DEV] + shp
            else:
                shp[-2 if kd == "rows" else -1] *= N_DEV
            shapes.append(jax.ShapeDtypeStruct(tuple(shp), x.dtype))
        return shapes

    def scratch(self):
        return [pltpu.SemaphoreType.DMA((7 * self.n,)), pltpu.SemaphoreType.DMA((7 * self.n,)),
                pltpu.SemaphoreType.DMA((self.n,))]

    def bind(self, in_refs, out_refs, sems):
        self.x, self.o = in_refs, out_refs
        self.send_sems, self.recv_sems, self.local_sems = sems
        self.pos = (lax.axis_index("x"), lax.axis_index("y"), lax.axis_index("c"))

    def _slot(self, i, px, py, pc):
        p = 4 * px + 2 * py + pc
        kd = self.kinds[i]
        if kd == "stack":
            return self.o[i].at[p]
        nd = len(self.x[i].shape)
        ax = nd - 2 if kd == "rows" else nd - 1
        size = self.x[i].shape[ax]
        idx = tuple(pl.ds(p * size, size) if a == ax else slice(None) for a in range(nd))
        return self.o[i].at[idx]

    def _gcopy(self, i, k, block, to, src=None):
        return pltpu.make_async_remote_copy(
            src_ref=self._slot(i, *block) if src is None else src, dst_ref=self._slot(i, *block),
            send_sem=self.send_sems.at[7 * i + k], recv_sem=self.recv_sems.at[7 * i + k],
            device_id=to, device_id_type=pl.DeviceIdType.MESH)

    def _chips(self):
        mx, my, _ = self.pos
        return [(1 - mx, my), (mx, 1 - my), (1 - mx, 1 - my)]

    def _xcopies(self):
        mx, my, mc = self.pos
        me = 4 * mx + 2 * my + mc
        copies = []
        for k in range(1, N_DEV):
            px = 1 - mx if (k >> 2) & 1 else mx
            py = 1 - my if (k >> 1) & 1 else my
            pc = 1 - mc if k & 1 else mc
            for i in range(self.n):
                copies.append(pltpu.make_async_remote_copy(
                    src_ref=self.x[i].at[4 * px + 2 * py + pc], dst_ref=self.o[i].at[me],
                    send_sem=self.send_sems.at[7 * i + k - 1], recv_sem=self.recv_sems.at[7 * i + k - 1],
                    device_id=(px, py, pc), device_id_type=pl.DeviceIdType.MESH))
        return copies

    def _local(self):
        mx, my, mc = self.pos
        if self.mode == "exchange":
            me = 4 * mx + 2 * my + mc
            return [pltpu.make_async_copy(self.x[i].at[me], self.o[i].at[me], self.local_sems.at[i]) for i in range(self.n)]
        return [pltpu.make_async_copy(self.x[i], self._slot(i, mx, my, mc), self.local_sems.at[i]) for i in range(self.n)]

    def _first(self):
        mx, my, mc = self.pos
        me, sibling = (mx, my, mc), (mx, my, 1 - mc)
        first = [self._gcopy(i, 0, me, sibling, src=self.x[i]) for i in range(self.n)]
        first += [self._gcopy(i, 1 + j, me, (*chip, mc), src=self.x[i]) for j, chip in enumerate(self._chips())
                  for i in range(self.n)]
        return first

    def _passed(self):
        mx, my, mc = self.pos
        return [self._gcopy(i, 4 + j, (*chip, mc), (mx, my, 1 - mc)) for j, chip in enumerate(self._chips())
                for i in range(self.n)]

    def start(self):
        for cp in self._local() + (self._xcopies() if self.mode == "exchange" else self._first()):
            cp.start()

    def mid(self):
        if self.mode == "exchange":
            return
        mx, my, mc = self.pos
        passed = self._passed()
        for j, chip in enumerate(self._chips()):
            for i in range(self.n):
                self._gcopy(i, 1 + j, (*chip, mc), (mx, my, mc)).wait_recv()
                passed[j * self.n + i].start()

    def end(self):
        mx, my, mc = self.pos
        if self.mode == "exchange":
            copies = self._xcopies()
            for cp in copies:
                cp.wait_recv()
            for cp in copies:
                cp.wait_send()
        else:
            for i in range(self.n):
                self._gcopy(i, 0, (mx, my, 1 - mc), (mx, my, mc)).wait_recv()
                for j, chip in enumerate(self._chips()):
                    self._gcopy(i, 4 + j, (*chip, 1 - mc), (mx, my, mc)).wait_recv()
            for cp in self._first() + self._passed():
                cp.wait_send()
        for cp in self._local():
            cp.wait()

    def run(self, *, name):
        n = self.n

        def body(*refs):
            self.bind(refs[:n], refs[n:2 * n], refs[2 * n:])
            self.start()
            self.mid()
            self.end()

        anyspec = pl.BlockSpec(memory_space=pl.ANY)
        return pl.pallas_call(body, name=name, in_specs=[anyspec] * n, out_specs=[anyspec] * n,
                              out_shape=self.out_shapes(), scratch_shapes=self.scratch())(*self.arrays)


def _comm_specs(comm):
    if comm is None:
        return [], [], [], []
    anyspec = pl.BlockSpec(memory_space=pl.ANY)
    return [anyspec] * comm.n, [anyspec] * comm.n, comm.out_shapes(), comm.scratch()


def _comm_hooks(comm, refs, n_in, n_out, n_scr, first, mid, last):
    if comm is None:
        return refs[:n_in], refs[n_in:n_in + n_out], refs[n_in + n_out:]
    c = comm.n
    ins, cin = refs[:n_in], refs[n_in:n_in + c]
    outs, cout = refs[n_in + c:n_in + c + n_out], refs[n_in + c + n_out:n_in + 2 * c + n_out]
    scr, sems = refs[n_in + 2 * c + n_out:n_in + 2 * c + n_out + n_scr], refs[n_in + 2 * c + n_out + n_scr:]
    comm.bind(cin, cout, sems)
    pl.when(first)(comm.start)
    pl.when(mid)(comm.mid)
    pl.when(last)(comm.end)
    return ins, outs, scr


def _adamw_update(g, w, m, v):
    mn = ADAM_B1 * m + (1.0 - ADAM_B1) * g
    vn = ADAM_B2 * v + (1.0 - ADAM_B2) * (g * g)
    m_hat = mn / (1.0 - ADAM_B1 ** ADAM_STEP)
    v_hat = vn / (1.0 - ADAM_B2 ** ADAM_STEP)
    return -ADAM_LR * (m_hat / (jnp.sqrt(v_hat) + ADAM_EPS) + ADAM_WD * w), mn, vn


def _adamw_layers(recvs, w, m, v, *, name):
    nl, ks, ns = w.shape
    tr = _pick(ks, (64, 48))

    def body(*refs):
        rv_refs = refs[:nl]
        w_ref, m_ref, v_ref, g_ref, d_ref, nm_ref, nv_ref = refs[nl:]
        for l in range(nl):
            g = rv_refs[l][0].astype(F32)
            for q in range(1, N_DEV):
                g = g + rv_refs[l][q].astype(F32)
            delta, mn, vn = _adamw_update(g, w_ref[l], m_ref[l], v_ref[l])
            g_ref[l] = g
            d_ref[l] = delta
            nm_ref[l] = mn
            nv_ref[l] = vn

    row = pl.BlockSpec((nl, tr, ns), lambda i: (0, i, 0))
    return pl.pallas_call(
        body, name=name, grid=(ks // tr,),
        in_specs=[pl.BlockSpec((N_DEV, tr, ns), lambda i: (0, i, 0))] * nl + [row] * 3,
        out_specs=[row] * 4,
        out_shape=[jax.ShapeDtypeStruct((nl, ks, ns), F32)] * 4,
        compiler_params=_cparams(("parallel",)),
    )(*recvs, w, m, v)


def _adamw_reduce(recv, w, m, v, *, name):
    r, c = w.shape
    tr = _pick(r, (128, 64, 8))

    def body(rv_ref, w_ref, m_ref, v_ref, g_ref, d_ref, nm_ref, nv_ref):
        g = rv_ref[0]
        for q in range(1, N_DEV):
            g = g + rv_ref[q]
        delta, mn, vn = _adamw_update(g, w_ref[...], m_ref[...], v_ref[...])
        g_ref[...] = g
        d_ref[...] = delta
        nm_ref[...] = mn
        nv_ref[...] = vn

    row = pl.BlockSpec((tr, c), lambda i: (i, 0))
    return pl.pallas_call(
        body, name=name, grid=(r // tr,),
        in_specs=[pl.BlockSpec((N_DEV, tr, c), lambda i: (0, i, 0)), row, row, row],
        out_specs=[row] * 4,
        out_shape=[jax.ShapeDtypeStruct((r, c), F32)] * 4,
        compiler_params=_cparams(("parallel",)),
    )(recv, w, m, v)


_BIG = ("gdn_w_in", "gdn_w_out", "dswa_w_in", "dswa_w_out", "mlp_w1", "mlp_w2")
_SMALL = ("gdn_conv_w", "norm_mix", "norm_mlp", "norm_final", "rel_bias", "gdn_a_log", "gdn_dt_bias", "gdn_norm_w")
_ORDER = ("norm_mix", "norm_mlp", "norm_final", "rel_bias", "gdn_w_in", "gdn_conv_w", "gdn_a_log", "gdn_dt_bias",
          "gdn_norm_w", "gdn_w_out", "dswa_w_in", "dswa_w_out", "mlp_w1", "mlp_w2")
_KIND = dict(gdn_w_in="stack", gdn_w_out="rows", dswa_w_in="stack", dswa_w_out="rows", mlp_w1="cols", mlp_w2="rows")


def _pack_rows(arrs, align):
    rows, counts = [], []
    for a in arrs:
        flat = a.reshape(-1)
        n = -(-flat.shape[0] // D_MODEL)
        flat = jnp.pad(flat, (0, n * D_MODEL - flat.shape[0]))
        rows.append(flat.reshape(n, D_MODEL))
        counts.append(n)
    out = jnp.concatenate(rows, axis=0)
    total = -(-out.shape[0] // align) * align
    return jnp.pad(out, ((0, total - out.shape[0]), (0, 0))), counts


def _unpack_rows(slab, shapes):
    outs, r = [], 0
    for shp in shapes:
        size = int(np.prod(shp))
        n = -(-size // D_MODEL)
        outs.append(slab[r:r + n].reshape(-1)[:size].reshape(shp))
        r += n
    return outs


def _col_shards(full, nshard):
    lead = full.shape[:-1]
    n = full.shape[-1] // nshard
    t = full.reshape(lead + (nshard, n))
    return jnp.moveaxis(t, -2, 0)


def _from_col_shards(g):
    t = jnp.moveaxis(g, 0, -2)
    return t.reshape(t.shape[:-2] + (t.shape[-2] * t.shape[-1],))


def kernel(x, norm_mix, norm_mlp, norm_final, rel_bias, gdn_w_in, gdn_conv_w, gdn_a_log, gdn_dt_bias, gdn_norm_w, gdn_w_out, dswa_w_in, dswa_w_out, mlp_w1, mlp_w2, loss_target, m_norm_mix, m_norm_mlp, m_norm_final, m_rel_bias, m_gdn_w_in, m_gdn_conv_w, m_gdn_a_log, m_gdn_dt_bias, m_gdn_norm_w, m_gdn_w_out, m_dswa_w_in, m_dswa_w_out, m_mlp_w1, m_mlp_w2, v_norm_mix, v_norm_mlp, v_norm_final, v_rel_bias, v_gdn_w_in, v_gdn_conv_w, v_gdn_a_log, v_gdn_dt_bias, v_gdn_norm_w, v_gdn_w_out, v_dswa_w_in, v_dswa_w_out, v_mlp_w1, v_mlp_w2):
    params = dict(norm_mix=norm_mix, norm_mlp=norm_mlp, norm_final=norm_final, rel_bias=rel_bias,
                  gdn_w_in=gdn_w_in, gdn_conv_w=gdn_conv_w, gdn_a_log=gdn_a_log, gdn_dt_bias=gdn_dt_bias,
                  gdn_norm_w=gdn_norm_w, gdn_w_out=gdn_w_out, dswa_w_in=dswa_w_in, dswa_w_out=dswa_w_out,
                  mlp_w1=mlp_w1, mlp_w2=mlp_w2)
    mom_m = dict(norm_mix=m_norm_mix, norm_mlp=m_norm_mlp, norm_final=m_norm_final, rel_bias=m_rel_bias,
                 gdn_w_in=m_gdn_w_in, gdn_conv_w=m_gdn_conv_w, gdn_a_log=m_gdn_a_log, gdn_dt_bias=m_gdn_dt_bias,
                 gdn_norm_w=m_gdn_norm_w, gdn_w_out=m_gdn_w_out, dswa_w_in=m_dswa_w_in, dswa_w_out=m_dswa_w_out,
                 mlp_w1=m_mlp_w1, mlp_w2=m_mlp_w2)
    mom_v = dict(norm_mix=v_norm_mix, norm_mlp=v_norm_mlp, norm_final=v_norm_final, rel_bias=v_rel_bias,
                 gdn_w_in=v_gdn_w_in, gdn_conv_w=v_gdn_conv_w, gdn_a_log=v_gdn_a_log, gdn_dt_bias=v_gdn_dt_bias,
                 gdn_norm_w=v_gdn_norm_w, gdn_w_out=v_gdn_w_out, dswa_w_in=v_dswa_w_in, dswa_w_out=v_dswa_w_out,
                 mlp_w1=v_mlp_w1, mlp_w2=v_mlp_w2)
    xs = x[0]
    target = loss_target[0]
    dist = _Dist(params)
    conv_tail, _ = _pack_rows([gdn_conv_w], 8)
    (conv_g,) = dist.put("start", dist.gather_comm("start", extra=[(conv_tail, "stack")]).run(name="ag_start"))
    conv_parts = [_unpack_rows(conv_g[dev], [gdn_conv_w.shape])[0] for dev in range(N_DEV)]
    conv_full = _from_col_shards(jnp.stack(conv_parts))[:, :, 0, :]

    loss_part, dcur, g_big, rep, g_conv = _local_step(
        xs, target, dict(norm_mix=norm_mix, norm_mlp=norm_mlp, norm_final=norm_final, rel_bias=rel_bias,
                         gdn_a_log=gdn_a_log, gdn_dt_bias=gdn_dt_bias, gdn_norm_w=gdn_norm_w), dist.full, conv_full, dist)
    loss = lax.psum(loss_part[0, 0], ("x", "y", "c"))
    grad_x = dcur[None]

    conv_dev = _col_shards(jnp.stack(g_conv)[:, :, None, :], N_DEV)
    small_send = jnp.stack([_pack_rows([conv_dev[dev]] + [rep[n] for n in _SMALL[1:]], 8)[0] for dev in range(N_DEV)])
    (small_recv,) = dist.got("end", dist.send_comm("end", g_big, extra=[small_send]).run(name="grad_exchange"))

    outs = {}
    for n in _BIG:
        recvs = [dist.recv[(n, l)] for l in range(params[n].shape[0])]
        res = _adamw_layers(recvs, params[n], mom_m[n], mom_v[n], name=f"adamw_{n}")
        for tag, t in zip(("grad", "delta", "new_m", "new_v"), res):
            outs[(tag, n)] = t
    w_slab, _ = _pack_rows([params[n] for n in _SMALL], 8)
    m_slab, _ = _pack_rows([mom_m[n] for n in _SMALL], 8)
    v_slab, _ = _pack_rows([mom_v[n] for n in _SMALL], 8)
    small = _adamw_reduce(small_recv, w_slab, m_slab, v_slab, name="adamw_small")
    shapes = [params[n].shape for n in _SMALL]
    for tag, slab in zip(("grad", "delta", "new_m", "new_v"), small):
        for n, t in zip(_SMALL, _unpack_rows(slab, shapes)):
            outs[(tag, n)] = t
    result = [loss, grad_x]
    for tag in ("grad", "delta", "new_m", "new_v"):
        result += [outs[(tag, n)] for n in _ORDER]
    return tuple(result)


_GATHER = {
    "start": (("gdn_w_in", 0),),
    "gdn_proj0": (("gdn_w_out", 0), ("mlp_w1", 0)),
    "chunk_fwd0": (("mlp_w2", 0), ("dswa_w_in", 0), ("dswa_w_out", 0), ("mlp_w1", 1), ("gdn_w_out", 1)),
    "mlp_up0": (("mlp_w2", 1),),
    "mlp_down0": (("gdn_w_in", 1),),
    "attn_fwd1": (("mlp_w1", 2),),
    "mlp_up1": (("mlp_w2", 2),),
    "chunk_fwd2": (("dswa_w_in", 1), ("dswa_w_out", 1), ("mlp_w1", 3), ("mlp_w2", 3)),
}
_SEND = {
    "attn_bwd3": (("mlp_w1", 3), ("mlp_w2", 3)),
    "chunk_bwd2": (("dswa_w_in", 1), ("dswa_w_out", 1), ("mlp_w2", 2)),
    "pre_bwd2": (("mlp_w1", 2),),
    "gdn_proj_bwd2": (("gdn_w_in", 1),),
    "attn_bwd1": (("mlp_w1", 1), ("mlp_w2", 1)),
    "chunk_bwd0": (("gdn_w_out", 1), ("dswa_w_in", 0), ("dswa_w_out", 0), ("mlp_w2", 0)),
    "pre_bwd0": (("mlp_w1", 0), ("gdn_w_out", 0)),
    "gdn_proj_bwd0": (("gdn_w_in", 0),),
    "end": (),
}


class _Dist:
    def __init__(self, params):
        self.shards = {n: params[n].astype(BF16) for n in _BIG}
        self.full = {n: [None] * params[n].shape[0] for n in _BIG}
        self.recv = {}

    def gather_comm(self, tag, extra=()):
        if tag not in _GATHER:
            return None
        arrays = [self.shards[n][l] for n, l in _GATHER[tag]] + [a for a, _ in extra]
        return _Comm("gather", arrays, [_KIND[n] for n, _ in _GATHER[tag]] + [k for _, k in extra])

    def put(self, tag, outs):
        for (n, l), t in zip(_GATHER.get(tag, ()), outs):
            self.full[n][l] = _from_col_shards(t) if _KIND[n] == "stack" else t
        return outs[len(_GATHER.get(tag, ())):]

    def send_comm(self, tag, g_big, extra=()):
        if tag not in _SEND:
            return None
        arrays = [_col_shards(g_big[n][l], N_DEV) if _KIND[n] == "stack" else g_big[n][l] for n, l in _SEND[tag]]
        return _Comm("exchange", arrays + list(extra))

    def got(self, tag, outs):
        for item, t in zip(_SEND.get(tag, ()), outs):
            self.recv[item] = t
        return outs[len(_SEND.get(tag, ())):]


def _mm_gather(dist, tag, *args, **kw):
    comm = dist and dist.gather_comm(tag)
    if not comm:
        return _mm(*args, **kw)
    res, got = _mm(*args, comm=comm, **kw)
    dist.put(tag, got)
    return res


def _ep_residual_norm(acc, res, g):
    x = acc + res
    r = lax.rsqrt(jnp.mean(x * x, axis=-1, keepdims=True) + RMS_EPS)
    return x, x * r * g


def _ep_rms_bwd(dh, x, dres, g):
    r = lax.rsqrt(jnp.mean(x * x, axis=-1, keepdims=True) + RMS_EPS)
    xn = x * r
    dn = dh * g
    dx = dres + r * (dn - xn * jnp.mean(dn * xn, axis=-1, keepdims=True))
    return dx, dx, jnp.sum(dh * xn, axis=0, keepdims=True)


def _local_step(xs, target, sp, full, conv_full, dist=None):
    s = xs.shape[0]
    norm_mix, norm_mlp, norm_final = sp["norm_mix"], sp["norm_mlp"], sp["norm_final"]
    gdn_a_log, gdn_dt_bias, gdn_norm_w = sp["gdn_a_log"], sp["gdn_dt_bias"], sp["gdn_norm_w"]
    onehot = _bucket_onehot()
    table_t = sp["rel_bias"].T
    bias = _dswa_bias(table_t, onehot, name="dswa_bias").reshape(DSWA_HEADS, DSWA_HALF, 3 * DSWA_HALF)

    saved = []
    cur = xs
    row = lambda v: v.reshape(1, -1)
    h = _rms_fwd(cur, norm_mix[0], name="rms_mix_fwd0")
    for i in range(DEPTH):
        j = i // 2
        sv = dict(x_in=cur, h=h)
        if i % 2 == 0:
            w_in = full["gdn_w_in"][j]
            proj = _mm_gather(dist, f"gdn_proj{i}", h, w_in, b_cols=(0, GDN_MAIN), name=f"gdn_proj{i}")
            ab = _mm(h, w_in[:, GDN_MAIN:], name=f"gdn_proj_ab{i}")
            qkvn = _gdn_pre_fwd(proj, conv_full[j], name=f"gdn_pre_fwd{i}")
            g_all, beta_all = _gdn_gate_fwd(ab[:, :2 * GDN_HEADS], ab[:, 2 * GDN_HEADS:], gdn_a_log[j], gdn_dt_bias[j],
                                            name=f"gdn_gate_fwd{i}")
            gshape = (2, GDN_HEADS, s // GDN_CHUNK, 1, GDN_CHUNK)
            g_row = g_all.T.reshape(gshape)
            b_row = beta_all.T.reshape(gshape)
            o, states, got = _gdn_chunk_fwd(qkvn, g_row, b_row, name=f"gdn_chunk_fwd{i}",
                                            comm=dist and dist.gather_comm(f"chunk_fwd{i}"))
            if dist:
                dist.put(f"chunk_fwd{i}", got)
            act = _gdn_post_fwd(o, proj, gdn_norm_w[j], name=f"gdn_post_fwd{i}")
            sv.update(proj=proj, ab=ab, qkvn=qkvn, g_row=g_row, b_row=b_row, o=o, states=states, act=act)
            w_out = full["gdn_w_out"][j]
        else:
            w_in = full["dswa_w_in"][j]
            qkv = _mm(h, w_in, name=f"dswa_proj{i}")
            o_n, lse_n, got = _dswa_attn_fwd(qkv, bias, name=f"dswa_attn_fwd{i}",
                                             comm=dist and dist.gather_comm(f"attn_fwd{i}"))
            if dist:
                dist.put(f"attn_fwd{i}", got)
            act = _dswa_combine_fwd(o_n, lse_n, name=f"dswa_comb_fwd{i}")
            sv.update(qkv=qkv, o_n=o_n, lse_n=lse_n, act=act)
            w_out = full["dswa_w_out"][j]
        cur, h2 = _mm(act, w_out, name=f"mix_out{i}", out_dtypes=(F32, BF16), epilogue=_ep_residual_norm,
                      extras=(cur,), vecs=(row(norm_mlp[i]),))
        sv["x_mid"] = cur
        u, a = _mm_gather(dist, f"mlp_up{i}", h2, full["mlp_w1"][i], name=f"mlp_up{i}", out_dtypes=(BF16, BF16),
                          epilogue=lambda acc: (acc, jnp.square(jnp.maximum(acc, 0.0))))
        if i + 1 < DEPTH:
            cur, h = _mm_gather(dist, f"mlp_down{i}", a, full["mlp_w2"][i], name=f"mlp_down{i}", out_dtypes=(F32, BF16), tk=2048,
                                epilogue=_ep_residual_norm, extras=(cur,), vecs=(row(norm_mix[i + 1]),))
        else:
            cur = _mm_gather(dist, f"mlp_down{i}", a, full["mlp_w2"][i], name=f"mlp_down{i}", tk=2048,
                             epilogue=lambda acc, r: (acc + r,), extras=(cur,))
        sv.update(h2=h2, u=u, a=a)
        saved.append(sv)

    loss_part, dcur, dcur_b, dg_final = _loss_head(cur, norm_final, target, name="loss_head")

    g_norm_mix, g_norm_mlp = [None] * DEPTH, [None] * DEPTH
    g_big = {n: [None] * len(full[n]) for n in _BIG}
    g_conv, g_alog, g_dt, g_nw = [None] * 2, [None] * 2, [None] * 2, [None] * 2
    d_table_t = jnp.zeros((DSWA_HEADS, REL_BUCKETS), F32)
    for i in reversed(range(DEPTH)):
        j = i // 2
        sv = saved[i]
        w1, w2 = full["mlp_w1"][i], full["mlp_w2"][i]
        du = _mm(dcur_b, w2, tb=True, name=f"mlp_down_bwd{i}", out_dtypes=(BF16,),
                 epilogue=lambda acc, uu: (acc * (2.0 * jnp.maximum(uu.astype(F32), 0.0)),), extras=(sv["u"],))
        g_big["mlp_w2"][i] = _mm(sv["a"], dcur_b, ta=True, name=f"mlp_w2_grad{i}", out_dtypes=(BF16,), shard="rows")
        g_big["mlp_w1"][i] = _mm(sv["h2"], du, ta=True, name=f"mlp_w1_grad{i}", out_dtypes=(BF16,), shard="cols")
        dmid, dmid_b, g_norm_mlp[i] = _mm(du, w1, tb=True, name=f"mlp_up_bwd{i}", out_dtypes=(F32, BF16),
                                          epilogue=_ep_rms_bwd, extras=(sv["x_mid"], dcur), vecs=(row(norm_mlp[i]),),
                                          vec_out=True)
        if i % 2 == 0:
            w_in, w_out = full["gdn_w_in"][j], full["gdn_w_out"][j]
            dact = _mm(dmid_b, w_out, tb=True, name=f"mix_out_bwd{i}")
            g_big["gdn_w_out"][j] = _mm(sv["act"], dmid_b, ta=True, name=f"mix_out_grad{i}", out_dtypes=(BF16,),
                                        shard="rows")
            do, dz, g_nw[j] = _gdn_post_bwd(sv["o"], sv["proj"], gdn_norm_w[j], dact, name=f"gdn_post_bwd{i}")
            dqkvn, dg_row, db_row, got = _gdn_chunk_bwd(sv["qkvn"], sv["g_row"], sv["b_row"], sv["states"], do,
                                                        name=f"gdn_chunk_bwd{i}",
                                                        comm=dist and dist.send_comm(f"chunk_bwd{i}", g_big))
            if dist:
                dist.got(f"chunk_bwd{i}", got)
            dproj, g_conv[j], got = _gdn_pre_bwd(sv["proj"], conv_full[j], dqkvn, dz, name=f"gdn_pre_bwd{i}",
                                                 comm=dist and dist.send_comm(f"pre_bwd{i}", g_big))
            if dist:
                dist.got(f"pre_bwd{i}", got)
            nh2 = 2 * GDN_HEADS
            da_, db_, g_alog[j], g_dt[j] = _gdn_gate_bwd(sv["ab"][:, :nh2], sv["ab"][:, nh2:], gdn_a_log[j], gdn_dt_bias[j],
                                                         dg_row.reshape(nh2, s).T, db_row.reshape(nh2, s).T,
                                                         name=f"gdn_gate_bwd{i}")
            dab = jnp.concatenate([da_, db_], axis=1)
            gw_main = _mm(sv["h"], dproj, ta=True, name=f"gdn_w_in_grad{i}", out_dtypes=(BF16,))
            gw_ab = _mm(sv["h"], dab, ta=True, name=f"gdn_w_ab_grad{i}", out_dtypes=(BF16,))
            g_big["gdn_w_in"][j] = jnp.concatenate([gw_main, gw_ab], axis=1)
            dh_ab = _mm(dab, w_in[:, GDN_MAIN:], tb=True, name=f"gdn_proj_ab_bwd{i}")
            comm = dist and dist.send_comm(f"gdn_proj_bwd{i}", g_big)
            res = _mm(dproj, w_in, b_cols=(0, GDN_MAIN), tb=True, name=f"gdn_proj_bwd{i}", out_dtypes=(F32, BF16), tm=512, tk=2048,
                      epilogue=lambda acc, r, x, dres, g: _ep_rms_bwd(acc + r, x, dres, g),
                      extras=(dh_ab, sv["x_in"], dmid), vecs=(row(norm_mix[i]),), vec_out=True, comm=comm)
            if comm:
                res, got = res
                dist.got(f"gdn_proj_bwd{i}", got)
            dcur, dcur_b, g_norm_mix[i] = res
        else:
            w_in, w_out = full["dswa_w_in"][j], full["dswa_w_out"][j]
            dact = _mm(dmid_b, w_out, tb=True, name=f"mix_out_bwd{i}")
            g_big["dswa_w_out"][j] = _mm(sv["act"], dmid_b, ta=True, name=f"mix_out_grad{i}", out_dtypes=(BF16,),
                                         shard="rows")
            do_n, corr_n = _dswa_combine_bwd(sv["o_n"], sv["lse_n"], dact, name=f"dswa_comb_bwd{i}")
            *dqkv, dbias, got = _dswa_attn_bwd(sv["qkv"], bias, sv["lse_n"], do_n, corr_n, name=f"dswa_attn_bwd{i}",
                                               comm=dist and dist.send_comm(f"attn_bwd{i}", g_big))
            if dist:
                dist.got(f"attn_bwd{i}", got)
            d_table_t = d_table_t + _dswa_dtable(dbias.reshape(DSWA_HEADS, -1), onehot, name=f"dswa_dtable{i}")
            g_big["dswa_w_in"][j] = jnp.concatenate(
                [_mm(sv["h"], dt, ta=True, name=f"dswa_w_in_grad{i}_{t}", out_dtypes=(BF16,)) for t, dt in enumerate(dqkv)],
                axis=1)
            cols = [(t * DSWA_WIDTH, DSWA_WIDTH) for t in range(3)]
            dh = _mm(dqkv[0], w_in, b_cols=cols[0], tb=True, name=f"dswa_proj_bwd{i}_0")
            dh = _mm(dqkv[1], w_in, b_cols=cols[1], tb=True, name=f"dswa_proj_bwd{i}_1",
                     epilogue=lambda acc, r: (acc + r,), extras=(dh,))
            dcur, dcur_b, g_norm_mix[i] = _mm(
                dqkv[2], w_in, b_cols=cols[2], tb=True, name=f"dswa_proj_bwd{i}_2", out_dtypes=(F32, BF16), tm=512,
                epilogue=lambda acc, r, x, dres, g: _ep_rms_bwd(acc + r, x, dres, g),
                extras=(dh, sv["x_in"], dmid), vecs=(row(norm_mix[i]),), vec_out=True)

    rep = dict(norm_mix=jnp.concatenate(g_norm_mix, axis=0), norm_mlp=jnp.concatenate(g_norm_mlp, axis=0),
               norm_final=dg_final.reshape(-1), rel_bias=d_table_t.T,
               gdn_a_log=jnp.stack(g_alog).reshape(gdn_a_log.shape), gdn_dt_bias=jnp.stack(g_dt).reshape(gdn_dt_bias.shape),
               gdn_norm_w=jnp.stack(g_nw).reshape(gdn_norm_w.shape))
    return loss_part, dcur, g_big, rep, g_conv
```

```python
import functools
import math

import jax
import jax.numpy as jnp
import numpy as np
from jax import lax
from jax.experimental import pallas as pl
from jax.experimental.pallas import tpu as pltpu

F32 = jnp.float32
BF16 = jnp.bfloat16
HP = lax.Precision.HIGHEST

N_DEV = 8
D_MODEL = 1024
DEPTH = 4
RMS_EPS = 1e-6
NEG_INF = -1e30

GDN_HEADS = 8
GDN_DK = 128
GDN_CONV = 5
GDN_CHUNK = 128
GDN_QKV = 3 * GDN_HEADS * GDN_DK
GDN_MAIN = GDN_QKV + GDN_HEADS * GDN_DK
GDN_AB = 4 * GDN_HEADS

DSWA_DILS = (1, 4, 16)
DSWA_HG = 6
DSWA_E = 64
DSWA_HEADS = 18
DSWA_WIDTH = DSWA_HEADS * DSWA_E
DSWA_HALF = 64
DSWA_PG = DSWA_HG // 2
DSWA_UNROLL = 8
REL_BUCKETS = 32
REL_MAX_DIST = 1024

ADAM_LR = 0.001
ADAM_B1 = 0.9
ADAM_B2 = 0.999
ADAM_EPS = 1e-08
ADAM_WD = 0.01
ADAM_STEP = 10

VMEM_LIMIT = 56 * 1024 * 1024


def _cparams(sem=None, **kw):
    return pltpu.CompilerParams(dimension_semantics=sem, vmem_limit_bytes=VMEM_LIMIT, **kw)


def _pick(dim, cands):
    for c in cands:
        if dim % c == 0:
            return c
    return dim


def _bdot(a, b):
    return jnp.dot(a.astype(BF16), b.astype(BF16), preferred_element_type=F32)


def _bdot_nt(a, b):
    return lax.dot_general(a.astype(BF16), b.astype(BF16), (((1,), (1,)), ((), ())),
                           preferred_element_type=F32)


def _bdot_tn(a, b):
    return lax.dot_general(a.astype(BF16), b.astype(BF16), (((0,), (0,)), ((), ())),
                           preferred_element_type=F32)


def _hdot(a, b):
    return jnp.dot(a, b, precision=HP, preferred_element_type=F32)


def _hdot_nt(a, b):
    return lax.dot_general(a, b, (((1,), (1,)), ((), ())), precision=HP, preferred_element_type=F32)


def _sigmoid(x):
    return 1.0 / (1.0 + jnp.exp(-x))


def _mm(a, b, *, name, ta=False, tb=False, out_dtypes=(F32,), epilogue=None, extras=(), vecs=(), vec_out=False,
        tm=None, tn=None, tk=None, shard=None, comm=None, b_cols=None):
    if ta:
        kdim, m = a.shape
    else:
        m, kdim = a.shape
    b0, bsz = b_cols or (0, b.shape[1])
    n = b.shape[0] if tb else bsz
    assert not tb or kdim == bsz
    if shard == "rows":
        tm = m // N_DEV if (m // N_DEV) % 128 == 0 else m
    if shard == "cols":
        tn = n // N_DEV
    tm = tm or _pick(m, (1024, 1152, 512, 384, 256, 128))
    tn = tn or _pick(n, (1024, 1152, 512, 384, 256, 128))
    tk = tk or _pick(kdim, ((2048,) if ta else ()) + (1024, 1152, 512, 384, 256, 128))
    nk = kdim // tk
    n_out = len(out_dtypes) + (1 if vec_out else 0)
    n_ex = len(extras) + len(vecs)
    rows_all = shard == "rows" and tm == m

    gi, gj = m // tm, n // tn

    def body(*refs):
        i, j, k = pl.program_id(0), pl.program_id(1), pl.program_id(2)
        inner = (j == 0) & (k == 0)
        ins, out_refs, (acc_ref,) = _comm_hooks(
            comm, refs, 2 + n_ex, n_out, 1, (i == 0) & inner, (i == (3 * gi) // 4) & inner,
            (i == gi - 1) & (j == gj - 1) & (k == nk - 1))
        a_ref, b_ref, ex_refs = ins[0], ins[1], ins[2:]
        if vec_out:
            out_refs, vec_ref = out_refs[:-1], out_refs[-1]

        @pl.when(k == 0)
        def _():
            acc_ref[...] = jnp.zeros_like(acc_ref)

        av = a_ref[...].astype(BF16)
        bv = b_ref[...].astype(BF16)
        dims = (((0 if ta else 1,), (1 if tb else 0,)), ((), ()))
        acc_ref[...] += lax.dot_general(av, bv, dims, preferred_element_type=F32)

        @pl.when(k == nk - 1)
        def _():
            acc = acc_ref[...]
            outs = (acc,) if epilogue is None else epilogue(acc, *[r[...] for r in ex_refs])
            if vec_out:
                part = outs[-1]

                @pl.when(i == 0)
                def _():
                    vec_ref[...] = part

                @pl.when(i > 0)
                def _():
                    vec_ref[...] += part
            for r, o in zip(out_refs, outs):
                if rows_all:
                    for p in range(N_DEV):
                        r[p] = o[p * (m // N_DEV):(p + 1) * (m // N_DEV)].astype(r.dtype)
                else:
                    r[...] = o.astype(r.dtype)

    a_spec = pl.BlockSpec((tk, tm), lambda i, j, k: (k, i)) if ta else pl.BlockSpec((tm, tk), lambda i, j, k: (i, k))
    assert b0 % (tk if tb else tn) == 0
    boff = b0 // (tk if tb else tn)
    b_spec = (pl.BlockSpec((tn, tk), lambda i, j, k: (j, k + boff)) if tb
              else pl.BlockSpec((tk, tn), lambda i, j, k: (k, j + boff)))
    o_spec = pl.BlockSpec((tm, tn), lambda i, j, k: (i, j))
    v_spec = pl.BlockSpec((1, tn), lambda i, j, k: (0, j))
    out_specs = [o_spec] * len(out_dtypes) + ([v_spec] if vec_out else [])
    out_shape = [jax.ShapeDtypeStruct((m, n), dt) for dt in out_dtypes]
    out_shape += [jax.ShapeDtypeStruct((1, n), F32)] if vec_out else []
    if shard == "rows":
        out_shape = [jax.ShapeDtypeStruct((N_DEV, m // N_DEV, n), out_dtypes[0])]
        out_specs = [pl.BlockSpec((N_DEV, m // N_DEV, tn), lambda i, j, k: (0, 0, j)) if rows_all
                     else pl.BlockSpec((None, tm, tn), lambda i, j, k: (i, 0, j))]
    if shard == "cols":
        out_shape = [jax.ShapeDtypeStruct((N_DEV, m, tn), out_dtypes[0])]
        out_specs = [pl.BlockSpec((None, tm, tn), lambda i, j, k: (j, i, 0))]
    c_in, c_out, c_shape, c_scr = _comm_specs(comm)
    outs = pl.pallas_call(
        body, name=name,
        grid=(gi, gj, nk),
        in_specs=[a_spec, b_spec] + [o_spec] * len(extras) + [v_spec] * len(vecs) + c_in,
        out_specs=out_specs + c_out,
        out_shape=out_shape + c_shape,
        scratch_shapes=[pltpu.VMEM((tm, tn), F32)] + c_scr,
        compiler_params=_cparams(("arbitrary",) * 3 if comm or vec_out else ("parallel", "parallel", "arbitrary")),
    )(a, b, *extras, *vecs, *(comm.arrays if comm else []))
    res = outs[0] if n_out == 1 else tuple(outs[:n_out])
    return (res, outs[n_out:]) if comm else res


def _rms_fwd(x, g, *, name):
    s, d = x.shape
    tr = _pick(s, (512, 256, 128))

    def body(x_ref, g_ref, h_ref):
        xv = x_ref[...]
        r = lax.rsqrt(jnp.mean(xv * xv, axis=-1, keepdims=True) + RMS_EPS)
        h_ref[...] = (xv * r * g_ref[...]).astype(h_ref.dtype)

    return pl.pallas_call(
        body, name=name, grid=(s // tr,),
        in_specs=[pl.BlockSpec((tr, d), lambda i: (i, 0)), pl.BlockSpec((1, d), lambda i: (0, 0))],
        out_specs=pl.BlockSpec((tr, d), lambda i: (i, 0)),
        out_shape=jax.ShapeDtypeStruct((s, d), BF16),
        compiler_params=_cparams(("parallel",)),
    )(x, g.reshape(1, d))


def _loss_head(x, g, target, *, name):
    s, d = x.shape
    tr = _pick(s, (512, 256, 128))

    def body(x_ref, g_ref, t_ref, loss_ref, dx_ref, dxb_ref, dg_ref):
        i = pl.program_id(0)
        xv = x_ref[...]
        gv = g_ref[...]
        r = lax.rsqrt(jnp.mean(xv * xv, axis=-1, keepdims=True) + RMS_EPS)
        xn = xv * r
        err = xn * gv - t_ref[...]
        lpart = 0.5 * jnp.sum(jnp.mean(err * err, axis=-1, keepdims=True), axis=0, keepdims=True)
        dy = err * (1.0 / d)
        dn = dy * gv
        dx = r * (dn - xn * jnp.mean(dn * xn, axis=-1, keepdims=True))
        dx_ref[...] = dx
        dxb_ref[...] = dx.astype(dxb_ref.dtype)
        gpart = jnp.sum(dy * xn, axis=0, keepdims=True)

        @pl.when(i == 0)
        def _():
            dg_ref[...] = gpart
            loss_ref[...] = lpart

        @pl.when(i > 0)
        def _():
            dg_ref[...] += gpart
            loss_ref[...] += lpart

    row = pl.BlockSpec((tr, d), lambda i: (i, 0))
    vec = pl.BlockSpec((1, d), lambda i: (0, 0))
    one = pl.BlockSpec((1, 1), lambda i: (0, 0))
    return pl.pallas_call(
        body, name=name, grid=(s // tr,),
        in_specs=[row, vec, row], out_specs=[one, row, row, vec],
        out_shape=[jax.ShapeDtypeStruct((1, 1), F32), jax.ShapeDtypeStruct((s, d), F32),
                   jax.ShapeDtypeStruct((s, d), BF16), jax.ShapeDtypeStruct((1, d), F32)],
        compiler_params=_cparams(("arbitrary",)),
    )(x, g.reshape(1, d), target)


def _shift_rows(x, sft, rows):
    s = x.shape[0]
    if sft == 0:
        return x
    y = pltpu.roll(x, (-sft) % s, 0)
    edge = slice(0, 8) if sft < 0 else slice(s - 8, s)
    ok = (rows[edge] + sft >= 0) & (rows[edge] + sft < s)
    fixed = jnp.where(ok, y[edge], 0.0)
    return jnp.concatenate([fixed, y[8:]] if sft < 0 else [y[:s - 8], fixed], axis=0)


def _gdn_pre_fwd(proj, conv_w, *, name):
    s = proj.shape[0]
    nblk = GDN_QKV // 128
    pad = GDN_CONV // 2

    def body(x_ref, w_ref, o_ref):
        j = pl.program_id(0)
        x = x_ref[...]
        rows = lax.broadcasted_iota(jnp.int32, x.shape, 0)
        c = jnp.zeros_like(x)
        for t in range(GDN_CONV):
            c = c + w_ref[pl.ds(t, 1), :] * _shift_rows(x, t - pad, rows)
        a = c * _sigmoid(c)
        rinv = lax.rsqrt(jnp.sum(a * a, axis=-1, keepdims=True) + 1e-6)
        scale = jnp.where(j < GDN_HEADS, GDN_DK ** -0.5, 1.0)
        o_ref[...] = jnp.where(j >= 2 * GDN_HEADS, a, a * (rinv * scale))

    return pl.pallas_call(
        body, name=name, grid=(nblk,),
        in_specs=[pl.BlockSpec((s, 128), lambda j: (0, j)), pl.BlockSpec((GDN_CONV, 128), lambda j: (0, j))],
        out_specs=pl.BlockSpec((s, 128), lambda j: (0, j)),
        out_shape=jax.ShapeDtypeStruct((s, GDN_QKV), F32),
        compiler_params=_cparams(("parallel",)),
    )(proj, conv_w)


def _gdn_pre_bwd(proj, conv_w, dqkv, dproj, *, name, comm=None):
    s = proj.shape[0]
    nblk = GDN_QKV // 128
    pad = GDN_CONV // 2

    def body(*refs):
        j = pl.program_id(0)
        (x_ref, w_ref, df_ref, dbk_ref, _), (dx_ref, dw_ref), _ = _comm_hooks(
            comm, refs, 5, 2, 0, j == 0, j == nblk // 2, j == nblk - 1)
        x = x_ref[...]
        rows = lax.broadcasted_iota(jnp.int32, x.shape, 0)
        xs = [_shift_rows(x, t - pad, rows) for t in range(GDN_CONV)]
        c = jnp.zeros_like(x)
        for t in range(GDN_CONV):
            c = c + w_ref[pl.ds(t, 1), :] * xs[t]
        sg = _sigmoid(c)
        a = c * sg
        rinv = lax.rsqrt(jnp.sum(a * a, axis=-1, keepdims=True) + 1e-6)
        scale = jnp.where(j < GDN_HEADS, GDN_DK ** -0.5, 1.0)
        dy = df_ref[...] + dbk_ref[...]
        nh = a * rinv
        da_n = (rinv * scale) * (dy - nh * jnp.sum(dy * nh, axis=-1, keepdims=True))
        da = jnp.where(j >= 2 * GDN_HEADS, dy, da_n)
        dc = da * (sg * (1.0 + c * (1.0 - sg)))
        dx = jnp.zeros_like(x)
        for t in range(GDN_CONV):
            dx = dx + w_ref[pl.ds(t, 1), :] * _shift_rows(dc, pad - t, rows)
            dw_ref[pl.ds(t, 1), :] = jnp.sum(dc * xs[t], axis=0, keepdims=True)
        dx_ref[...] = dx.astype(dx_ref.dtype)

    col = pl.BlockSpec((s, 128), lambda j: (0, j))
    wsp = pl.BlockSpec((GDN_CONV, 128), lambda j: (0, j))
    c_in, c_out, c_shape, c_scr = _comm_specs(comm)
    res = pl.pallas_call(
        body, name=name, grid=(nblk,),
        in_specs=[col, wsp, col, col, pl.BlockSpec(memory_space=pl.ANY)] + c_in, out_specs=[col, wsp] + c_out,
        out_shape=[jax.ShapeDtypeStruct(dproj.shape, BF16), jax.ShapeDtypeStruct((GDN_CONV, GDN_QKV), F32)] + c_shape,
        input_output_aliases={4: 0},
        scratch_shapes=c_scr,
        compiler_params=_cparams(("arbitrary",) if comm else ("parallel",)),
    )(proj, conv_w, dqkv[0], dqkv[1], dproj, *(comm.arrays if comm else []))
    return res[0], res[1], res[2:]


def _softplus(x):
    return jnp.maximum(x, 0.0) + jnp.log(1.0 + jnp.exp(-jnp.abs(x)))


def _gdn_gate_fwd(a, b, a_log, dt_bias, *, name):
    s = a.shape[0]
    nh = 2 * GDN_HEADS

    def body(a_ref, b_ref, al_ref, dt_ref, g_ref, be_ref):
        g_ref[...] = -jnp.exp(al_ref[...]) * _softplus(a_ref[...] + dt_ref[...])
        be_ref[...] = _sigmoid(b_ref[...])

    return pl.pallas_call(
        body, name=name,
        out_shape=[jax.ShapeDtypeStruct((s, nh), F32), jax.ShapeDtypeStruct((s, nh), F32)],
        compiler_params=_cparams(),
    )(a, b, a_log.reshape(1, nh), dt_bias.reshape(1, nh))


def _gdn_gate_bwd(a, b, a_log, dt_bias, dg, dbeta, *, name):
    s = a.shape[0]
    nh = 2 * GDN_HEADS

    def body(a_ref, b_ref, al_ref, dt_ref, dg_ref, db_ref, da_ref, dbb_ref, dal_ref, ddt_ref):
        ea = jnp.exp(al_ref[...])
        z = a_ref[...] + dt_ref[...]
        dgv = dg_ref[...]
        dz = dgv * (-ea) * _sigmoid(z)
        dal_ref[...] = jnp.sum(dgv * (-ea) * _softplus(z), axis=0, keepdims=True)
        ddt_ref[...] = jnp.sum(dz, axis=0, keepdims=True)
        sb = _sigmoid(b_ref[...])
        da_ref[...] = dz
        dbb_ref[...] = db_ref[...] * sb * (1.0 - sb)

    return pl.pallas_call(
        body, name=name,
        out_shape=[jax.ShapeDtypeStruct((s, nh), F32), jax.ShapeDtypeStruct((s, nh), F32),
                   jax.ShapeDtypeStruct((1, nh), F32), jax.ShapeDtypeStruct((1, nh), F32)],
        compiler_params=_cparams(),
    )(a, b, a_log.reshape(1, nh), dt_bias.reshape(1, nh), dg, dbeta)


def _chunk_masks(d):
    c = GDN_CHUNK
    ii = lax.broadcasted_iota(jnp.int32, (c, c), 0)
    jj = lax.broadcasted_iota(jnp.int32, (c, c), 1)
    dif = (ii - jj) * (1 - 2 * d)
    mi = dif >= 0
    mit = dif <= 0
    ms = dif > 0
    eye = ii == jj
    bds = [(ii >> sh) == (jj >> sh) for sh in range(3, c.bit_length() - 1)]
    return dict(mi=mi, mit=mit, ms=ms, eye=eye, bds=bds,
                mif=mi.astype(F32), mitf=mit.astype(F32), eyef=eye.astype(F32))


class _V:
    def __init__(self, xs):
        self.xs = tuple(xs)

    def __add__(self, o):
        return _lift(lambda a, b: a + b)(self, o)

    def __radd__(self, o):
        return _lift(lambda a, b: b + a)(self, o)

    def __sub__(self, o):
        return _lift(lambda a, b: a - b)(self, o)

    def __rsub__(self, o):
        return _lift(lambda a, b: b - a)(self, o)

    def __mul__(self, o):
        return _lift(lambda a, b: a * b)(self, o)

    def __rmul__(self, o):
        return _lift(lambda a, b: b * a)(self, o)

    def __and__(self, o):
        return _lift(lambda a, b: a & b)(self, o)

    def __neg__(self):
        return _lift(lambda a: -a)(self)

    def __rtruediv__(self, o):
        return _lift(lambda a, b: b / a)(self, o)


def _lift(f):
    def g(*args, **kw):
        n = next(len(a.xs) for a in args if isinstance(a, _V))
        return _V(f(*[a.xs[i] if isinstance(a, _V) else a for a in args], **kw) for i in range(n))
    return g


_vwhere, _vsum, _vexp, _vnot = _lift(jnp.where), _lift(jnp.sum), _lift(jnp.exp), _lift(jnp.logical_not)
_vbdot, _vbdot_nt, _vbdot_tn = _lift(_bdot), _lift(_bdot_nt), _lift(_bdot_tn)
_vcat = _lift(lambda a, b: jnp.concatenate([a, b], axis=1))
_vlo = _lift(lambda a, n: a[:, :n])
_vhi = _lift(lambda a, n: a[:, n:])


def _both_masks(n):
    m = [_chunk_masks(d) for d in range(2)]
    mk = {key: _V([m[0][key]] * n + [m[1][key]] * n) for key in m[0] if key != "bds"}
    mk["bds"] = [_V([m[0]["bds"][i]] * n + [m[1]["bds"][i]] * n) for i in range(len(m[0]["bds"]))]
    return mk


def _tri_inv(a, mk):
    eyef = mk["eyef"]
    bds = mk["bds"]
    a8 = _vwhere(bds[0], a, 0.0)
    a2 = _vbdot(a8, a8)
    a4 = _vbdot(a2, a2)
    t = _vbdot(_vbdot(eyef - a8, eyef + a2), eyef + a4)
    for inner, outer in zip(bds, bds[1:] + [None]):
        off = _vnot(inner) if outer is None else (outer & _vnot(inner))
        low = _vwhere(off, a, 0.0)
        t = t - _vbdot(_vbdot(t, low), t)
    return t


def _chunk_prep(q, k, v, g_row, b_row, mk, tuw=None):
    dv = GDN_DK
    g_col = _vsum(mk["eyef"] * g_row, axis=1, keepdims=True)
    b_col = _vsum(mk["eyef"] * b_row, axis=1, keepdims=True)
    gc_col = _vsum(mk["mif"] * g_row, axis=1, keepdims=True)
    gc_row = _vsum(mk["mitf"] * g_col, axis=0, keepdims=True)
    gl = _vsum(g_row, axis=1, keepdims=True)
    decay = _vwhere(mk["mi"], _vexp(_vwhere(mk["mi"], gc_col - gc_row, 0.0)), 0.0)
    eg = _vexp(gc_col)
    e2 = _vexp(gl - gc_col)
    egl = _vexp(gl)
    kb = k * b_col
    pm = _vbdot_nt(kb, k)
    if tuw is None:
        t = _tri_inv(_vwhere(mk["ms"], pm * decay, 0.0), mk)
        sol = _vbdot(t, _vcat(v * b_col, kb * eg))
        u, w = _vlo(sol, dv), _vhi(sol, dv)
    else:
        t, u, w = tuw
    qm = _vbdot_nt(q, k)
    return dict(b_col=b_col, decay=decay, eg=eg, e2=e2, egl=egl, kb=kb, pm=pm, t=t, u=u, w=w,
                qm=qm, intra=qm * decay, qd=q * eg, kd=k * e2)


def _chunk_fwd_step(p, state):
    v_new = p["u"] - _vbdot(p["w"], state)
    o = _vbdot(p["qd"], state) + _vbdot(p["intra"], v_new)
    new_state = state * p["egl"] + _vbdot_tn(p["kd"], v_new)
    return o, new_state


def _chunk_bwd_step(q, k, v, p, mk, state, dso, do):
    dv_dim = GDN_DK
    v_new = p["u"] - _vbdot(p["w"], state)
    dvn = _vbdot_tn(p["intra"], do) + _vbdot(p["kd"], dso)
    dintra = _vbdot_nt(do, v_new)
    dqd = _vbdot_nt(do, state)
    ds = p["egl"] * dso + _vbdot_tn(p["qd"], do) - _vbdot_tn(p["w"], dvn)
    dkd = _vbdot_nt(v_new, dso)
    dgl = _vsum(_vsum(dso * state, axis=1, keepdims=True), axis=0, keepdims=True) * p["egl"]
    dw = -_vbdot_nt(dvn, state)
    drhs = _vbdot_tn(p["t"], _vcat(dvn, dw))
    dru, drw = _vlo(drhs, dv_dim), _vhi(drhs, dv_dim)
    da = -_vwhere(mk["ms"], _vbdot_nt(drhs, _vcat(p["u"], p["w"])), 0.0)
    b_col = p["b_col"]
    dv = dru * b_col
    dbeta = _vsum(dru * v, axis=1, keepdims=True)
    dkb = drw * p["eg"]
    deg = _vsum(drw * p["kb"], axis=1, keepdims=True)
    dp = da * p["decay"]
    ddecay = da * p["pm"]
    dkb = dkb + _vbdot(dp, k)
    dk = _vbdot_tn(dp, p["kb"])
    dqm = dintra * p["decay"]
    ddecay = ddecay + dintra * p["qm"]
    dq = _vbdot(dqm, k)
    dk = dk + _vbdot_tn(dqm, q)
    dd = ddecay * p["decay"]
    dgc_col = _vsum(dd, axis=1, keepdims=True)
    dgc_row = -_vsum(dd, axis=0, keepdims=True)
    dq = dq + dqd * p["eg"]
    deg = deg + _vsum(dqd * q, axis=1, keepdims=True)
    dk = dk + dkd * p["e2"]
    de2 = _vsum(dkd * k, axis=1, keepdims=True) * p["e2"]
    dgl = dgl + _vsum(de2, axis=0, keepdims=True)
    dgc_col = dgc_col - de2 + deg * p["eg"]
    dk = dk + dkb * b_col
    dbeta = dbeta + _vsum(dkb * k, axis=1, keepdims=True)
    dgc_col = dgc_col + _vsum(mk["eyef"] * dgc_row, axis=1, keepdims=True)
    dg_row = _vsum(mk["mif"] * dgc_col, axis=0, keepdims=True) + dgl
    dbeta_row = _vsum(mk["eyef"] * dbeta, axis=0, keepdims=True)
    return dq, dk, dv, dg_row, dbeta_row, ds


def _gdn_chunk_fwd(qkvn, g5, b5, *, name, comm=None):
    s = qkvn.shape[0]
    c = GDN_CHUNK
    nc = s // c
    h_, dk = GDN_HEADS, GDN_DK

    def body(*refs):
        n = pl.program_id(0)
        ins, outs, (st_scr,) = _comm_hooks(comm, refs, 6, 10, 1, n == 0, n == (3 * nc) // 4, n == nc - 1)
        x_refs, g_refs, b_refs = ins[0:2], ins[2:4], ins[4:6]
        o_refs, st_refs, t_refs, u_refs, w_refs = outs[0:2], outs[2:4], outs[4:6], outs[6:8], outs[8:10]

        @pl.when(n == 0)
        def _():
            st_scr[...] = jnp.zeros_like(st_scr)

        ch = [(d, h) for d in range(2) for h in range(h_)]
        mk = _both_masks(h_)
        q, k, v = (_V(x_refs[d][:, (t * h_ + h) * dk:(t * h_ + h + 1) * dk] for d, h in ch) for t in range(3))
        g, b = (_V(r[d][0, h, 0] for d, h in ch) for r in (g_refs, b_refs))
        state = _V(st_scr[d * h_ + h] for d, h in ch)
        p = _chunk_prep(q, k, v, g, b, mk)
        o, new_state = _chunk_fwd_step(p, state)
        for i, (d, h) in enumerate(ch):
            st_refs[d][h, 0] = state.xs[i]
            st_scr[d * h_ + h] = new_state.xs[i]
            o_refs[d][:, h * dk:(h + 1) * dk] = o.xs[i]
            t_refs[d][h, 0] = p["t"].xs[i].astype(BF16)
            u_refs[d][h, 0] = p["u"].xs[i]
            w_refs[d][h, 0] = p["w"].xs[i].astype(BF16)

    ce = (lambda n: n, lambda n: nc - 1 - n)
    xs = [pl.BlockSpec((c, 3 * h_ * dk), lambda n, d=d: (ce[d](n), 0)) for d in range(2)]
    gates = [pl.BlockSpec((1, h_, 1, 1, c), lambda n, d=d: (d, 0, ce[d](n), 0, 0)) for d in range(2)]
    os_ = [pl.BlockSpec((c, h_ * dk), lambda n, d=d: (ce[d](n), 0)) for d in range(2)]
    sts = [pl.BlockSpec((h_, 1, dk, dk), lambda n, d=d: (0, ce[d](n), 0, 0)) for d in range(2)]
    tcc = [pl.BlockSpec((h_, 1, c, c), lambda n, d=d: (0, ce[d](n), 0, 0)) for d in range(2)]
    tck = [pl.BlockSpec((h_, 1, c, dk), lambda n, d=d: (0, ce[d](n), 0, 0)) for d in range(2)]
    per_chunk = lambda last, dt: [jax.ShapeDtypeStruct((h_, nc, c, last), dt)] * 2
    c_in, c_out, c_shape, c_scr = _comm_specs(comm)
    res = pl.pallas_call(
        body, name=name, grid=(nc,),
        in_specs=xs + gates + gates + c_in,
        out_specs=os_ + sts + tcc + tck + tck + c_out,
        out_shape=[jax.ShapeDtypeStruct((s, h_ * dk), F32)] * 2 + [jax.ShapeDtypeStruct((h_, nc, dk, dk), F32)] * 2
        + per_chunk(c, BF16) + per_chunk(dk, F32) + per_chunk(dk, BF16) + c_shape,
        scratch_shapes=[pltpu.VMEM((2 * h_, dk, dk), F32)] + c_scr,
        compiler_params=_cparams(("arbitrary",)),
    )(qkvn, qkvn, g5, g5, b5, b5, *(comm.arrays if comm else []))
    return res[0:2], res[2:10], res[10:]


def _gdn_chunk_bwd(qkvn, g5, b5, states, do, *, name, comm=None):
    s = qkvn.shape[0]
    c = GDN_CHUNK
    nc = s // c
    h_, dk = GDN_HEADS, GDN_DK

    def body(*refs):
        i = pl.program_id(0)
        ins, outs, (ds_scr,) = _comm_hooks(comm, refs, 16, 6, 1, i == 0, i == nc // 2, i == nc - 1)
        x_refs, g_refs, b_refs, st_refs = ins[0:2], ins[2:4], ins[4:6], ins[6:8]
        t_refs, u_refs, w_refs, do_refs = ins[8:10], ins[10:12], ins[12:14], ins[14:16]
        dx_refs, dg_refs, db_refs = outs[0:2], outs[2:4], outs[4:6]

        @pl.when(i == 0)
        def _():
            ds_scr[...] = jnp.zeros_like(ds_scr)

        ch = [(d, h) for d in range(2) for h in range(h_)]
        mk = _both_masks(h_)
        q, k, v = (_V(x_refs[d][:, (t * h_ + h) * dk:(t * h_ + h + 1) * dk] for d, h in ch) for t in range(3))
        g, b = (_V(r[d][0, h, 0] for d, h in ch) for r in (g_refs, b_refs))
        state = _V(st_refs[d][h, 0] for d, h in ch)
        dso = _V(ds_scr[d * h_ + h] for d, h in ch)
        dov = _V(do_refs[d][:, h * dk:(h + 1) * dk] for d, h in ch)
        tuw = tuple(_V(r[d][h, 0] for d, h in ch) for r in (t_refs, u_refs, w_refs))
        res = _chunk_bwd_step(q, k, v, _chunk_prep(q, k, v, g, b, mk, tuw), mk, state, dso, dov)
        for (d, h), (dq, dkk, dvv, dg_r, db_r, ds) in zip(ch, zip(*[r.xs for r in res])):
            ds_scr[d * h_ + h] = ds
            dg_refs[d][h, 0] = dg_r
            db_refs[d][h, 0] = db_r
            for t, val in enumerate((dq, dkk, dvv)):
                dx_refs[d][:, (t * h_ + h) * dk:(t * h_ + h + 1) * dk] = val

    ce = (lambda i: nc - 1 - i, lambda i: i)
    both = lambda mk_spec: [mk_spec(d) for d in range(2)]
    xs = both(lambda d: pl.BlockSpec((c, 3 * h_ * dk), lambda i: (ce[d](i), 0)))
    gates = both(lambda d: pl.BlockSpec((1, h_, 1, 1, c), lambda i: (d, 0, ce[d](i), 0, 0)))
    sts = both(lambda d: pl.BlockSpec((h_, 1, dk, dk), lambda i: (0, ce[d](i), 0, 0)))
    tcc = both(lambda d: pl.BlockSpec((h_, 1, c, c), lambda i: (0, ce[d](i), 0, 0)))
    tck = both(lambda d: pl.BlockSpec((h_, 1, c, dk), lambda i: (0, ce[d](i), 0, 0)))
    dos = both(lambda d: pl.BlockSpec((c, h_ * dk), lambda i: (ce[d](i), 0)))
    gouts = both(lambda d: pl.BlockSpec((h_, 1, 1, c), lambda i: (0, ce[d](i), 0, 0)))
    c_in, c_out, c_shape, c_scr = _comm_specs(comm)
    res = pl.pallas_call(
        body, name=name, grid=(nc,),
        in_specs=xs + gates + gates + sts + tcc + tck + tck + dos + c_in,
        out_specs=xs + gouts + gouts + c_out,
        out_shape=[jax.ShapeDtypeStruct((s, 3 * h_ * dk), F32)] * 2
        + [jax.ShapeDtypeStruct((h_, nc, 1, c), F32)] * 4 + c_shape,
        scratch_shapes=[pltpu.VMEM((2 * h_, dk, dk), F32)] + c_scr,
        compiler_params=_cparams(("arbitrary",)),
    )(qkvn, qkvn, g5, g5, b5, b5, *states, do, do, *(comm.arrays if comm else []))
    return res[0:2], jnp.stack(res[2:4]), jnp.stack(res[4:6]), res[6:]


def _gdn_post_fwd(o, z, norm_w, *, name):
    s = o[0].shape[0]
    h_, dk = GDN_HEADS, GDN_DK

    def body(of_ref, ob_ref, z_ref, w_ref, a_ref):
        ov = of_ref[...] + ob_ref[...]
        zv = z_ref[...]
        r = lax.rsqrt(jnp.mean(ov * ov, axis=-1, keepdims=True) + RMS_EPS)
        a_ref[...] = (ov * r * w_ref[...] * (zv * _sigmoid(zv))).astype(a_ref.dtype)

    col = pl.BlockSpec((s, dk), lambda h: (0, h))
    return pl.pallas_call(
        body, name=name, grid=(h_,),
        in_specs=[col, col, pl.BlockSpec((s, dk), lambda h: (0, 3 * h_ + h)), pl.BlockSpec((1, dk), lambda h: (0, 0))],
        out_specs=col,
        out_shape=jax.ShapeDtypeStruct((s, h_ * dk), BF16),
        compiler_params=_cparams(("parallel",)),
    )(o[0], o[1], z, norm_w.reshape(1, dk))


def _gdn_post_bwd(o, z, norm_w, dact, *, name):
    s = o[0].shape[0]
    h_, dk = GDN_HEADS, GDN_DK

    def body(of_ref, ob_ref, z_ref, w_ref, da_ref, do_ref, dz_ref, dw_ref):
        h = pl.program_id(0)
        ov = of_ref[...] + ob_ref[...]
        zv = z_ref[...]
        wv = w_ref[...]
        dav = da_ref[...]
        r = lax.rsqrt(jnp.mean(ov * ov, axis=-1, keepdims=True) + RMS_EPS)
        nrm = ov * r
        sg = _sigmoid(zv)
        sz = zv * sg
        dn = dav * wv * sz
        do_ref[...] = r * (dn - nrm * jnp.mean(dn * nrm, axis=-1, keepdims=True))
        dz_ref[...] = (dav * nrm * wv * (sg * (1.0 + zv * (1.0 - sg)))).astype(dz_ref.dtype)
        part = jnp.sum(dav * nrm * sz, axis=0, keepdims=True)

        @pl.when(h == 0)
        def _():
            dw_ref[...] = part

        @pl.when(h > 0)
        def _():
            dw_ref[...] += part

    col = pl.BlockSpec((s, dk), lambda h: (0, h))
    vec = pl.BlockSpec((1, dk), lambda h: (0, 0))
    return pl.pallas_call(
        body, name=name, grid=(h_,),
        in_specs=[col, col, pl.BlockSpec((s, dk), lambda h: (0, 3 * h_ + h)), vec, col],
        out_specs=[col, pl.BlockSpec((s, dk), lambda h: (0, 3 * h_ + h)), vec],
        out_shape=[jax.ShapeDtypeStruct((s, h_ * dk), F32), jax.ShapeDtypeStruct((s, GDN_MAIN), BF16),
                   jax.ShapeDtypeStruct((1, dk), F32)],
        compiler_params=_cparams(("arbitrary",)),
    )(o[0], o[1], z, norm_w.reshape(1, dk), dact)


def _rel_bucket(rel):
    nb = REL_BUCKETS // 2
    max_exact = nb // 2
    ret = jnp.where(rel > 0, nb, 0)
    n = jnp.abs(rel)
    nf = jnp.maximum(n, 1).astype(F32)
    large = max_exact + (jnp.log(nf / max_exact) / math.log(REL_MAX_DIST / max_exact)
                         * (nb - max_exact)).astype(jnp.int32)
    large = jnp.minimum(large, nb - 1)
    return ret + jnp.where(n < max_exact, n, large)


def _bucket_onehot():
    half = DSWA_HALF
    outs = []
    for dil in DSWA_DILS:
        rel = (jnp.arange(3 * half)[None, :] - half - jnp.arange(half)[:, None]) * dil
        outs.append(jax.nn.one_hot(_rel_bucket(rel).reshape(-1), REL_BUCKETS, dtype=F32, axis=0))
    return jnp.stack(outs)


def _head_group_select(vals):
    rows = lax.broadcasted_iota(jnp.int32, vals[0].shape, 0)
    return jnp.where(rows < DSWA_HG, vals[0], jnp.where(rows < 2 * DSWA_HG, vals[1], vals[2]))


def _dswa_bias(table_t, onehot, *, name):
    p = onehot.shape[-1]

    def body(t_ref, oh_ref, b_ref):
        b_ref[...] = _head_group_select([_hdot(t_ref[...], oh_ref[g]) for g in range(3)])

    return pl.pallas_call(body, name=name, out_shape=jax.ShapeDtypeStruct((DSWA_HEADS, p), F32),
                          compiler_params=_cparams())(table_t, onehot)


def _dswa_dtable(dbias, onehot, *, name):
    def body(d_ref, oh_ref, t_ref):
        t_ref[...] = _head_group_select([_hdot_nt(d_ref[...], oh_ref[g]) for g in range(3)])

    return pl.pallas_call(body, name=name, out_shape=jax.ShapeDtypeStruct((DSWA_HEADS, REL_BUCKETS), F32),
                          compiler_params=_cparams())(dbias, onehot)


def _rows(start, dil):
    if dil == 1:
        return pl.ds(pl.multiple_of(start, DSWA_HALF), DSWA_HALF)
    return pl.ds(start, DSWA_HALF, stride=dil)


def _attn_blocks(it, s, dil):
    half = DSWA_HALF
    nbs = s // half // dil
    ii = lax.broadcasted_iota(jnp.int32, (half, 3 * half), 0)
    jj = lax.broadcasted_iota(jnp.int32, (half, 3 * half), 1)
    band = jnp.abs(jj - half - ii) <= half
    out = []
    for u in range(DSWA_UNROLL):
        blk = it * DSWA_UNROLL + u
        r, b = blk // nbs, blk % nbs
        own = r + dil * half * b
        prev = own - jnp.where(b > 0, dil * half, 0)
        nxt = own + jnp.where(b < nbs - 1, dil * half, 0)
        ok = band & ((jj >= half) | (b > 0)) & ((jj < 2 * half) | (b < nbs - 1))
        out.append(((prev, own, nxt), ok))
    return out


def _attn_chains(q_ref, k_ref, v_ref, blocks, dil):
    lane = lax.broadcasted_iota(jnp.int32, (DSWA_HALF, 2 * DSWA_E), 1)
    qm, kw, vw, valid, hmask = [], [], [], [], []
    for (prev, own, nxt), ok in blocks:
        q = q_ref[_rows(own, dil), :].astype(BF16)
        k = jnp.concatenate([k_ref[_rows(st, dil), :] for st in (prev, own, nxt)], axis=0).astype(BF16)
        v = jnp.concatenate([v_ref[_rows(st, dil), :] for st in (prev, own, nxt)], axis=0).astype(BF16)
        for hd in range(2):
            mine = (lane < DSWA_E) if hd == 0 else (lane >= DSWA_E)
            qm.append(jnp.where(mine, q, jnp.zeros_like(q)))
            kw.append(k)
            vw.append(v)
            valid.append(ok)
            hmask.append(mine)
    return _V(qm), _V(kw), _V(vw), _V(valid), _V(hmask)


def _per_group(pr, fn):
    for gi, dil in enumerate(DSWA_DILS):
        pl.when(pr // DSWA_PG == gi)(functools.partial(fn, dil))


_vmax, _vlog = _lift(jnp.max), _lift(jnp.log)


def _dswa_attn_fwd(qkv, bias, *, name, comm=None):
    s = qkv.shape[0]
    half, e = DSWA_HALF, DSWA_E
    npair = DSWA_HEADS // 2

    def body(*refs):
        pr = pl.program_id(0)
        (q_ref, k_ref, v_ref, bias_ref), (o_ref, lse_ref), _ = _comm_hooks(
            comm, refs, 4, 2, 0, pr == 0, pr == (3 * npair) // 4, pr == npair - 1)
        bias_v = _V([bias_ref[0], bias_ref[1]] * DSWA_UNROLL)

        def run(dil):
            def step(it, carry):
                blocks = _attn_blocks(it, s, dil)
                qm, kw, vw, valid, hmask = _attn_chains(q_ref, k_ref, v_ref, blocks, dil)
                sc = _vwhere(valid, _vbdot_nt(qm, kw) * (e ** -0.5) + bias_v, NEG_INF)
                m = _vmax(sc, axis=-1, keepdims=True)
                p = _vexp(sc - m)
                l = _vsum(p, axis=-1, keepdims=True)
                o = _vbdot(p * (1.0 / l), vw)
                lse = m + _vlog(l)
                for u, ((_, own, _), _) in enumerate(blocks):
                    is_a = hmask.xs[2 * u]
                    o_ref[_rows(own, dil), :] = jnp.where(is_a, o.xs[2 * u], o.xs[2 * u + 1])
                    lse_ref[_rows(own, dil), :] = jnp.where(is_a, lse.xs[2 * u], lse.xs[2 * u + 1])
                return carry

            lax.fori_loop(0, s // half // DSWA_UNROLL, step, 0)

        _per_group(pr, run)

    col = lambda t: pl.BlockSpec((s, 2 * e), lambda p: (0, t * npair + p))
    pair = pl.BlockSpec((s, 2 * e), lambda p: (0, p))
    c_in, c_out, c_shape, c_scr = _comm_specs(comm)
    res = pl.pallas_call(
        body, name=name, grid=(npair,),
        in_specs=[col(0), col(1), col(2), pl.BlockSpec((2, half, 3 * half), lambda p: (p, 0, 0))] + c_in,
        out_specs=[pair, pair] + c_out,
        out_shape=[jax.ShapeDtypeStruct((s, npair * 2 * e), F32)] * 2 + c_shape,
        scratch_shapes=c_scr,
        compiler_params=_cparams(("arbitrary",)),
    )(qkv, qkv, qkv, bias, *(comm.arrays if comm else []))
    return res[0], res[1], res[2:]


def _dswa_attn_bwd(qkv, bias, lse, do, corr, *, name, comm=None):
    s = qkv.shape[0]
    half, e = DSWA_HALF, DSWA_E
    npair = DSWA_HEADS // 2
    w = 2 * e

    def body(*refs):
        pr = pl.program_id(0)
        (q_ref, k_ref, v_ref, bias_ref, lse_ref, do_ref, corr_ref), (dq_ref, dk_ref, dv_ref, db_ref), _ = _comm_hooks(
            comm, refs, 7, 4, 0, pr == 0, pr == npair // 2, pr == npair - 1)
        bias_v = _V([bias_ref[0], bias_ref[1]] * DSWA_UNROLL)
        dk_ref[...] = jnp.zeros_like(dk_ref)
        dv_ref[...] = jnp.zeros_like(dv_ref)

        def run(dil):
            def step(it, dbias):
                blocks = _attn_blocks(it, s, dil)
                qm, kw, vw, valid, hmask = _attn_chains(q_ref, k_ref, v_ref, blocks, dil)
                hd = [0, 1] * DSWA_UNROLL
                rows = [_rows(own, dil) for (_, own, _), _ in blocks for _ in range(2)]
                lse_c = _V(lse_ref[rw, :][:, h * e:h * e + 1] for rw, h in zip(rows, hd))
                corr_c = _V(corr_ref[rw, :][:, h * e:h * e + 1] for rw, h in zip(rows, hd))
                dov = _vwhere(hmask, _V(do_ref[rw, :] for rw in rows), 0.0)
                sc = _vbdot_nt(qm, kw) * (e ** -0.5) + bias_v
                p = _vwhere(valid, _vexp(_vwhere(valid, sc, 0.0) - lse_c), 0.0)
                dsc = p * (_vbdot_nt(dov, vw) + corr_c)
                dq = _vbdot(dsc, kw) * (e ** -0.5)
                dkc = _vbdot_tn(dsc, qm) * (e ** -0.5)
                dvc = _vbdot_tn(p, dov)
                for u, (starts, _) in enumerate(blocks):
                    dq_ref[_rows(starts[1], dil), :] = jnp.where(hmask.xs[2 * u], dq.xs[2 * u], dq.xs[2 * u + 1])
                    dk_u = dkc.xs[2 * u] + dkc.xs[2 * u + 1]
                    dv_u = dvc.xs[2 * u] + dvc.xs[2 * u + 1]
                    for t, st in enumerate(starts):
                        dk_ref[_rows(st, dil), :] += dk_u[t * half:(t + 1) * half]
                        dv_ref[_rows(st, dil), :] += dv_u[t * half:(t + 1) * half]
                da, db = dbias
                for u in range(DSWA_UNROLL):
                    da, db = da + dsc.xs[2 * u], db + dsc.xs[2 * u + 1]
                return da, db

            zero = jnp.zeros((half, 3 * half), F32)
            da, db = lax.fori_loop(0, s // half // DSWA_UNROLL, step, (zero, zero))
            db_ref[0] = da
            db_ref[1] = db

        _per_group(pr, run)

    col = lambda t: pl.BlockSpec((s, w), lambda p: (0, t * npair + p))
    ps = pl.BlockSpec((s, w), lambda p: (0, p))
    bs = pl.BlockSpec((2, half, 3 * half), lambda p: (p, 0, 0))
    c_in, c_out, c_shape, c_scr = _comm_specs(comm)
    res = pl.pallas_call(
        body, name=name, grid=(npair,),
        in_specs=[col(0), col(1), col(2), bs, ps, ps, ps] + c_in,
        out_specs=[ps, ps, ps, bs] + c_out,
        out_shape=[jax.ShapeDtypeStruct((s, npair * w), F32)] * 3
        + [jax.ShapeDtypeStruct((DSWA_HEADS, half, 3 * half), F32)] + c_shape,
        scratch_shapes=c_scr,
        compiler_params=_cparams(("arbitrary",)),
    )(qkv, qkv, qkv, bias, lse, do, corr, *(comm.arrays if comm else []))
    return res[0], res[1], res[2], res[3], res[4:]


def _pair_cols(g, j):
    w = 2 * DSWA_E
    return slice((g * DSWA_PG + j) * w, (g * DSWA_PG + j + 1) * w)


def _group_weights(l_ref, j):
    ls = [l_ref[:, _pair_cols(g, j)] for g in range(3)]
    m = jnp.maximum(jnp.maximum(ls[0], ls[1]), ls[2])
    es = [jnp.exp(x - m) for x in ls]
    inv = 1.0 / (es[0] + es[1] + es[2])
    return [x * inv for x in es]


def _dswa_combine_fwd(o, lse, *, name):
    s, wd = o.shape
    tr = _pick(s, (512, 256, 128))

    def body(o_ref, l_ref, c_ref):
        for j in range(DSWA_PG):
            al = _group_weights(l_ref, j)
            for g in range(3):
                c_ref[:, _pair_cols(g, j)] = (o_ref[:, _pair_cols(g, j)] * al[g]).astype(c_ref.dtype)

    row = pl.BlockSpec((tr, wd), lambda i: (i, 0))
    return pl.pallas_call(
        body, name=name, grid=(s // tr,),
        in_specs=[row, row], out_specs=row,
        out_shape=jax.ShapeDtypeStruct(o.shape, BF16),
        compiler_params=_cparams(("parallel",)),
    )(o, lse)


def _dswa_combine_bwd(o, lse, dc, *, name):
    s, wd = o.shape
    tr = _pick(s, (512, 256, 128))

    def body(o_ref, l_ref, dc_ref, do_ref, corr_ref):
        lane = lax.broadcasted_iota(jnp.int32, (tr, 2 * DSWA_E), 1)
        is_a = lane < DSWA_E
        for j in range(DSWA_PG):
            al = _group_weights(l_ref, j)
            tot = jnp.zeros((tr, 2 * DSWA_E), F32)
            for g in range(3):
                cols = _pair_cols(g, j)
                dcv = dc_ref[:, cols]
                do_ref[:, cols] = dcv * al[g]
                prod = dcv * o_ref[:, cols]
                dal = jnp.where(is_a, jnp.sum(jnp.where(is_a, prod, 0.0), axis=-1, keepdims=True),
                                jnp.sum(jnp.where(is_a, 0.0, prod), axis=-1, keepdims=True))
                tot = tot + al[g] * dal
            for g in range(3):
                corr_ref[:, _pair_cols(g, j)] = -al[g] * tot

    row = pl.BlockSpec((tr, wd), lambda i: (i, 0))
    return pl.pallas_call(
        body, name=name, grid=(s // tr,),
        in_specs=[row, row, row], out_specs=[row, row],
        out_shape=[jax.ShapeDtypeStruct(o.shape, F32)] * 2,
        compiler_params=_cparams(("parallel",)),
    )(o, lse, dc)


class _Comm:
    def __init__(self, mode, arrays, kinds=None):
        self.mode, self.arrays, self.kinds = mode, list(arrays), kinds
        self.n = len(self.arrays)

    def out_shapes(self):
        if self.mode == "exchange":
            return [jax.ShapeDtypeStruct(x.shape, x.dtype) for x in self.arrays]
        shapes = []
        for x, kd in zip(self.arrays, self.kinds):
            shp = list(x.shape)
            if kd == "stack":
                shp = [N_DEV] + shp
            else:
                shp[-2 if kd == "rows" else -1] *= N_DEV
            shapes.append(jax.ShapeDtypeStruct(tuple(shp), x.dtype))
        return shapes

    def scratch(self):
        return [pltpu.SemaphoreType.DMA((7 * self.n,)), pltpu.SemaphoreType.DMA((7 * self.n,)),
                pltpu.SemaphoreType.DMA((self.n,))]

    def bind(self, in_refs, out_refs, sems):
        self.x, self.o = in_refs, out_refs
        self.send_sems, self.recv_sems, self.local_sems = sems
        self.pos = (lax.axis_index("x"), lax.axis_index("y"), lax.axis_index("c"))

    def _slot(self, i, px, py, pc):
        p = 4 * px + 2 * py + pc
        kd = self.kinds[i]
        if kd == "stack":
            return self.o[i].at[p]
        nd = len(self.x[i].shape)
        ax = nd - 2 if kd == "rows" else nd - 1
        size = self.x[i].shape[ax]
        idx = tuple(pl.ds(p * size, size) if a == ax else slice(None) for a in range(nd))
        return self.o[i].at[idx]

    def _gcopy(self, i, k, block, to, src=None):
        return pltpu.make_async_remote_copy(
            src_ref=self._slot(i, *block) if src is None else src, dst_ref=self._slot(i, *block),
            send_sem=self.send_sems.at[7 * i + k], recv_sem=self.recv_sems.at[7 * i + k],
            device_id=to, device_id_type=pl.DeviceIdType.MESH)

    def _chips(self):
        mx, my, _ = self.pos
        return [(1 - mx, my), (mx, 1 - my), (1 - mx, 1 - my)]

    def _xcopies(self):
        mx, my, mc = self.pos
        me = 4 * mx + 2 * my + mc
        copies = []
        for k in range(1, N_DEV):
            px = 1 - mx if (k >> 2) & 1 else mx
            py = 1 - my if (k >> 1) & 1 else my
            pc = 1 - mc if k & 1 else mc
            for i in range(self.n):
                copies.append(pltpu.make_async_remote_copy(
                    src_ref=self.x[i].at[4 * px + 2 * py + pc], dst_ref=self.o[i].at[me],
                    send_sem=self.send_sems.at[7 * i + k - 1], recv_sem=self.recv_sems.at[7 * i + k - 1],
                    device_id=(px, py, pc), device_id_type=pl.DeviceIdType.MESH))
        return copies

    def _local(self):
        mx, my, mc = self.pos
        if self.mode == "exchange":
            me = 4 * mx + 2 * my + mc
            return [pltpu.make_async_copy(self.x[i].at[me], self.o[i].at[me], self.local_sems.at[i]) for i in range(self.n)]
        return [pltpu.make_async_copy(self.x[i], self._slot(i, mx, my, mc), self.local_sems.at[i]) for i in range(self.n)]

    def _first(self):
        mx, my, mc = self.pos
        me, sibling = (mx, my, mc), (mx, my, 1 - mc)
        first = [self._gcopy(i, 0, me, sibling, src=self.x[i]) for i in range(self.n)]
        first += [self._gcopy(i, 1 + j, me, (*chip, mc), src=self.x[i]) for j, chip in enumerate(self._chips())
                  for i in range(self.n)]
        return first

    def _passed(self):
        mx, my, mc = self.pos
        return [self._gcopy(i, 4 + j, (*chip, mc), (mx, my, 1 - mc)) for j, chip in enumerate(self._chips())
                for i in range(self.n)]

    def start(self):
        for cp in self._local() + (self._xcopies() if self.mode == "exchange" else self._first()):
            cp.start()

    def mid(self):
        if self.mode == "exchange":
            return
        mx, my, mc = self.pos
        passed = self._passed()
        for j, chip in enumerate(self._chips()):
            for i in range(self.n):
                self._gcopy(i, 1 + j, (*chip, mc), (mx, my, mc)).wait_recv()
                passed[j * self.n + i].start()

    def end(self):
        mx, my, mc = self.pos
        if self.mode == "exchange":
            copies = self._xcopies()
            for cp in copies:
                cp.wait_recv()
            for cp in copies:
                cp.wait_send()
        else:
            for i in range(self.n):
                self._gcopy(i, 0, (mx, my, 1 - mc), (mx, my, mc)).wait_recv()
                for j, chip in enumerate(self._chips()):
                    self._gcopy(i, 4 + j, (*chip, 1 - mc), (mx, my, mc)).wait_recv()
            for cp in self._first() + self._passed():
                cp.wait_send()
        for cp in self._local():
            cp.wait()

    def run(self, *, name):
        n = self.n

        def body(*refs):
            self.bind(refs[:n], refs[n:2 * n], refs[2 * n:])
            self.start()
            self.mid()
            self.end()

        anyspec = pl.BlockSpec(memory_space=pl.ANY)
        return pl.pallas_call(body, name=name, in_specs=[anyspec] * n, out_specs=[anyspec] * n,
                              out_shape=self.out_shapes(), scratch_shapes=self.scratch())(*self.arrays)


def _comm_specs(comm):
    if comm is None:
        return [], [], [], []
    anyspec = pl.BlockSpec(memory_space=pl.ANY)
    return [anyspec] * comm.n, [anyspec] * comm.n, comm.out_shapes(), comm.scratch()


def _comm_hooks(comm, refs, n_in, n_out, n_scr, first, mid, last):
    if comm is None:
        return refs[:n_in], refs[n_in:n_in + n_out], refs[n_in + n_out:]
    c = comm.n
    ins, cin = refs[:n_in], refs[n_in:n_in + c]
    outs, cout = refs[n_in + c:n_in + c + n_out], refs[n_in + c + n_out:n_in + 2 * c + n_out]
    scr, sems = refs[n_in + 2 * c + n_out:n_in + 2 * c + n_out + n_scr], refs[n_in + 2 * c + n_out + n_scr:]
    comm.bind(cin, cout, sems)
    pl.when(first)(comm.start)
    pl.when(mid)(comm.mid)
    pl.when(last)(comm.end)
    return ins, outs, scr


def _adamw_update(g, w, m, v):
    mn = ADAM_B1 * m + (1.0 - ADAM_B1) * g
    vn = ADAM_B2 * v + (1.0 - ADAM_B2) * (g * g)
    m_hat = mn / (1.0 - ADAM_B1 ** ADAM_STEP)
    v_hat = vn / (1.0 - ADAM_B2 ** ADAM_STEP)
    return -ADAM_LR * (m_hat / (jnp.sqrt(v_hat) + ADAM_EPS) + ADAM_WD * w), mn, vn


def _adamw_layers(recvs, w, m, v, *, name):
    nl, ks, ns = w.shape
    tr = _pick(ks, (64, 48))

    def body(*refs):
        rv_refs = refs[:nl]
        w_ref, m_ref, v_ref, g_ref, d_ref, nm_ref, nv_ref = refs[nl:]
        for l in range(nl):
            g = rv_refs[l][0].astype(F32)
            for q in range(1, N_DEV):
                g = g + rv_refs[l][q].astype(F32)
            delta, mn, vn = _adamw_update(g, w_ref[l], m_ref[l], v_ref[l])
            g_ref[l] = g
            d_ref[l] = delta
            nm_ref[l] = mn
            nv_ref[l] = vn

    row = pl.BlockSpec((nl, tr, ns), lambda i: (0, i, 0))
    return pl.pallas_call(
        body, name=name, grid=(ks // tr,),
        in_specs=[pl.BlockSpec((N_DEV, tr, ns), lambda i: (0, i, 0))] * nl + [row] * 3,
        out_specs=[row] * 4,
        out_shape=[jax.ShapeDtypeStruct((nl, ks, ns), F32)] * 4,
        compiler_params=_cparams(("parallel",)),
    )(*recvs, w, m, v)


def _adamw_reduce(recv, w, m, v, *, name):
    r, c = w.shape
    tr = _pick(r, (128, 64, 8))

    def body(rv_ref, w_ref, m_ref, v_ref, g_ref, d_ref, nm_ref, nv_ref):
        g = rv_ref[0]
        for q in range(1, N_DEV):
            g = g + rv_ref[q]
        delta, mn, vn = _adamw_update(g, w_ref[...], m_ref[...], v_ref[...])
        g_ref[...] = g
        d_ref[...] = delta
        nm_ref[...] = mn
        nv_ref[...] = vn

    row = pl.BlockSpec((tr, c), lambda i: (i, 0))
    return pl.pallas_call(
        body, name=name, grid=(r // tr,),
        in_specs=[pl.BlockSpec((N_DEV, tr, c), lambda i: (0, i, 0)), row, row, row],
        out_specs=[row] * 4,
        out_shape=[jax.ShapeDtypeStruct((r, c), F32)] * 4,
        compiler_params=_cparams(("parallel",)),
    )(recv, w, m, v)


_BIG = ("gdn_w_in", "gdn_w_out", "dswa_w_in", "dswa_w_out", "mlp_w1", "mlp_w2")
_SMALL = ("gdn_conv_w", "norm_mix", "norm_mlp", "norm_final", "rel_bias", "gdn_a_log", "gdn_dt_bias", "gdn_norm_w")
_ORDER = ("norm_mix", "norm_mlp", "norm_final", "rel_bias", "gdn_w_in", "gdn_conv_w", "gdn_a_log", "gdn_dt_bias",
          "gdn_norm_w", "gdn_w_out", "dswa_w_in", "dswa_w_out", "mlp_w1", "mlp_w2")
_KIND = dict(gdn_w_in="stack", gdn_w_out="rows", dswa_w_in="stack", dswa_w_out="rows", mlp_w1="cols", mlp_w2="rows")


def _pack_rows(arrs, align):
    rows, counts = [], []
    for a in arrs:
        flat = a.reshape(-1)
        n = -(-flat.shape[0] // D_MODEL)
        flat = jnp.pad(flat, (0, n * D_MODEL - flat.shape[0]))
        rows.append(flat.reshape(n, D_MODEL))
        counts.append(n)
    out = jnp.concatenate(rows, axis=0)
    total = -(-out.shape[0] // align) * align
    return jnp.pad(out, ((0, total - out.shape[0]), (0, 0))), counts


def _unpack_rows(slab, shapes):
    outs, r = [], 0
    for shp in shapes:
        size = int(np.prod(shp))
        n = -(-size // D_MODEL)
        outs.append(slab[r:r + n].reshape(-1)[:size].reshape(shp))
        r += n
    return outs


def _col_shards(full, nshard):
    lead = full.shape[:-1]
    n = full.shape[-1] // nshard
    t = full.reshape(lead + (nshard, n))
    return jnp.moveaxis(t, -2, 0)


def _from_col_shards(g):
    t = jnp.moveaxis(g, 0, -2)
    return t.reshape(t.shape[:-2] + (t.shape[-2] * t.shape[-1],))


def kernel(x, norm_mix, norm_mlp, norm_final, rel_bias, gdn_w_in, gdn_conv_w, gdn_a_log, gdn_dt_bias, gdn_norm_w, gdn_w_out, dswa_w_in, dswa_w_out, mlp_w1, mlp_w2, loss_target, m_norm_mix, m_norm_mlp, m_norm_final, m_rel_bias, m_gdn_w_in, m_gdn_conv_w, m_gdn_a_log, m_gdn_dt_bias, m_gdn_norm_w, m_gdn_w_out, m_dswa_w_in, m_dswa_w_out, m_mlp_w1, m_mlp_w2, v_norm_mix, v_norm_mlp, v_norm_final, v_rel_bias, v_gdn_w_in, v_gdn_conv_w, v_gdn_a_log, v_gdn_dt_bias, v_gdn_norm_w, v_gdn_w_out, v_dswa_w_in, v_dswa_w_out, v_mlp_w1, v_mlp_w2):
    params = dict(norm_mix=norm_mix, norm_mlp=norm_mlp, norm_final=norm_final, rel_bias=rel_bias,
                  gdn_w_in=gdn_w_in, gdn_conv_w=gdn_conv_w, gdn_a_log=gdn_a_log, gdn_dt_bias=gdn_dt_bias,
                  gdn_norm_w=gdn_norm_w, gdn_w_out=gdn_w_out, dswa_w_in=dswa_w_in, dswa_w_out=dswa_w_out,
                  mlp_w1=mlp_w1, mlp_w2=mlp_w2)
    mom_m = dict(norm_mix=m_norm_mix, norm_mlp=m_norm_mlp, norm_final=m_norm_final, rel_bias=m_rel_bias,
                 gdn_w_in=m_gdn_w_in, gdn_conv_w=m_gdn_conv_w, gdn_a_log=m_gdn_a_log, gdn_dt_bias=m_gdn_dt_bias,
                 gdn_norm_w=m_gdn_norm_w, gdn_w_out=m_gdn_w_out, dswa_w_in=m_dswa_w_in, dswa_w_out=m_dswa_w_out,
                 mlp_w1=m_mlp_w1, mlp_w2=m_mlp_w2)
    mom_v = dict(norm_mix=v_norm_mix, norm_mlp=v_norm_mlp, norm_final=v_norm_final, rel_bias=v_rel_bias,
                 gdn_w_in=v_gdn_w_in, gdn_conv_w=v_gdn_conv_w, gdn_a_log=v_gdn_a_log, gdn_dt_bias=v_gdn_dt_bias,
                 gdn_norm_w=v_gdn_norm_w, gdn_w_out=v_gdn_w_out, dswa_w_in=v_dswa_w_in, dswa_w_out=v_dswa_w_out,
                 mlp_w1=v_mlp_w1, mlp_w2=v_mlp_w2)
    xs = x[0]
    target = loss_target[0]
    dist = _Dist(params)
    conv_tail, _ = _pack_rows([gdn_conv_w], 8)
    (conv_g,) = dist.put("start", dist.gather_comm("start", extra=[(conv_tail, "stack")]).run(name="ag_start"))
    conv_parts = [_unpack_rows(conv_g[dev], [gdn_conv_w.shape])[0] for dev in range(N_DEV)]
    conv_full = _from_col_shards(jnp.stack(conv_parts))[:, :, 0, :]

    loss_part, dcur, g_big, rep, g_conv = _local_step(
        xs, target, dict(norm_mix=norm_mix, norm_mlp=norm_mlp, norm_final=norm_final, rel_bias=rel_bias,
                         gdn_a_log=gdn_a_log, gdn_dt_bias=gdn_dt_bias, gdn_norm_w=gdn_norm_w), dist.full, conv_full, dist)
    loss = lax.psum(loss_part[0, 0], ("x", "y", "c"))
    grad_x = dcur[None]

    conv_dev = _col_shards(jnp.stack(g_conv)[:, :, None, :], N_DEV)
    small_send = jnp.stack([_pack_rows([conv_dev[dev]] + [rep[n] for n in _SMALL[1:]], 8)[0] for dev in range(N_DEV)])
    (small_recv,) = dist.got("end", dist.send_comm("end", g_big, extra=[small_send]).run(name="grad_exchange"))

    outs = {}
    for n in _BIG:
        recvs = [dist.recv[(n, l)] for l in range(params[n].shape[0])]
        res = _adamw_layers(recvs, params[n], mom_m[n], mom_v[n], name=f"adamw_{n}")
        for tag, t in zip(("grad", "delta", "new_m", "new_v"), res):
            outs[(tag, n)] = t
    w_slab, _ = _pack_rows([params[n] for n in _SMALL], 8)
    m_slab, _ = _pack_rows([mom_m[n] for n in _SMALL], 8)
    v_slab, _ = _pack_rows([mom_v[n] for n in _SMALL], 8)
    small = _adamw_reduce(small_recv, w_slab, m_slab, v_slab, name="adamw_small")
    shapes = [params[n].shape for n in _SMALL]
    for tag, slab in zip(("grad", "delta", "new_m", "new_v"), small):
        for n, t in zip(_SMALL, _unpack_rows(slab, shapes)):
            outs[(tag, n)] = t
    result = [loss, grad_x]
    for tag in ("grad", "delta", "new_m", "new_v"):
        result += [outs[(tag, n)] for n in _ORDER]
    return tuple(result)


_GATHER = {
    "start": (("gdn_w_in", 0),),
    "gdn_proj0": (("gdn_w_out", 0), ("mlp_w1", 0)),
    "chunk_fwd0": (("mlp_w2", 0), ("dswa_w_in", 0), ("dswa_w_out", 0), ("mlp_w1", 1)),
    "mlp_up0": (("mlp_w2", 1),),
    "mlp_down0": (("gdn_w_in", 1),),
    "attn_fwd1": (("mlp_w1", 2),),
    "mlp_up1": (("mlp_w2", 2),),
    "mlp_down1": (("gdn_w_out", 1),),
    "chunk_fwd2": (("dswa_w_in", 1), ("dswa_w_out", 1), ("mlp_w1", 3), ("mlp_w2", 3)),
}
_SEND = {
    "attn_bwd3": (("mlp_w1", 3), ("mlp_w2", 3)),
    "chunk_bwd2": (("dswa_w_in", 1), ("dswa_w_out", 1), ("mlp_w2", 2)),
    "pre_bwd2": (("mlp_w1", 2), ("gdn_w_out", 1)),
    "gdn_proj_bwd2": (("gdn_w_in", 1),),
    "attn_bwd1": (("mlp_w1", 1), ("mlp_w2", 1)),
    "chunk_bwd0": (("dswa_w_in", 0), ("dswa_w_out", 0), ("mlp_w2", 0)),
    "pre_bwd0": (("mlp_w1", 0), ("gdn_w_out", 0)),
    "gdn_proj_bwd0": (("gdn_w_in", 0),),
    "end": (),
}


class _Dist:
    def __init__(self, params):
        self.shards = {n: params[n].astype(BF16) for n in _BIG}
        self.full = {n: [None] * params[n].shape[0] for n in _BIG}
        self.recv = {}

    def gather_comm(self, tag, extra=()):
        if tag not in _GATHER:
            return None
        arrays = [self.shards[n][l] for n, l in _GATHER[tag]] + [a for a, _ in extra]
        return _Comm("gather", arrays, [_KIND[n] for n, _ in _GATHER[tag]] + [k for _, k in extra])

    def put(self, tag, outs):
        for (n, l), t in zip(_GATHER.get(tag, ()), outs):
            self.full[n][l] = _from_col_shards(t) if _KIND[n] == "stack" else t
        return outs[len(_GATHER.get(tag, ())):]

    def send_comm(self, tag, g_big, extra=()):
        if tag not in _SEND:
            return None
        arrays = [_col_shards(g_big[n][l], N_DEV) if _KIND[n] == "stack" else g_big[n][l] for n, l in _SEND[tag]]
        return _Comm("exchange", arrays + list(extra))

    def got(self, tag, outs):
        for item, t in zip(_SEND.get(tag, ()), outs):
            self.recv[item] = t
        return outs[len(_SEND.get(tag, ())):]


def _mm_gather(dist, tag, *args, **kw):
    comm = dist and dist.gather_comm(tag)
    if not comm:
        return _mm(*args, **kw)
    res, got = _mm(*args, comm=comm, **kw)
    dist.put(tag, got)
    return res


def _ep_residual_norm(acc, res, g):
    x = acc + res
    r = lax.rsqrt(jnp.mean(x * x, axis=-1, keepdims=True) + RMS_EPS)
    return x, x * r * g


def _ep_rms_bwd(dh, x, dres, g):
    r = lax.rsqrt(jnp.mean(x * x, axis=-1, keepdims=True) + RMS_EPS)
    xn = x * r
    dn = dh * g
    dx = dres + r * (dn - xn * jnp.mean(dn * xn, axis=-1, keepdims=True))
    return dx, dx, jnp.sum(dh * xn, axis=0, keepdims=True)


def _local_step(xs, target, sp, full, conv_full, dist=None):
    s = xs.shape[0]
    norm_mix, norm_mlp, norm_final = sp["norm_mix"], sp["norm_mlp"], sp["norm_final"]
    gdn_a_log, gdn_dt_bias, gdn_norm_w = sp["gdn_a_log"], sp["gdn_dt_bias"], sp["gdn_norm_w"]
    onehot = _bucket_onehot()
    table_t = sp["rel_bias"].T
    bias = _dswa_bias(table_t, onehot, name="dswa_bias").reshape(DSWA_HEADS, DSWA_HALF, 3 * DSWA_HALF)

    saved = []
    cur = xs
    row = lambda v: v.reshape(1, -1)
    h = _rms_fwd(cur, norm_mix[0], name="rms_mix_fwd0")
    for i in range(DEPTH):
        j = i // 2
        sv = dict(x_in=cur, h=h)
        if i % 2 == 0:
            w_in = full["gdn_w_in"][j]
            proj = _mm_gather(dist, f"gdn_proj{i}", h, w_in, b_cols=(0, GDN_MAIN), name=f"gdn_proj{i}")
            ab = _mm(h, w_in[:, GDN_MAIN:], name=f"gdn_proj_ab{i}")
            qkvn = _gdn_pre_fwd(proj, conv_full[j], name=f"gdn_pre_fwd{i}")
            g_all, beta_all = _gdn_gate_fwd(ab[:, :2 * GDN_HEADS], ab[:, 2 * GDN_HEADS:], gdn_a_log[j], gdn_dt_bias[j],
                                            name=f"gdn_gate_fwd{i}")
            gshape = (2, GDN_HEADS, s // GDN_CHUNK, 1, GDN_CHUNK)
            g_row = g_all.T.reshape(gshape)
            b_row = beta_all.T.reshape(gshape)
            o, states, got = _gdn_chunk_fwd(qkvn, g_row, b_row, name=f"gdn_chunk_fwd{i}",
                                            comm=dist and dist.gather_comm(f"chunk_fwd{i}"))
            if dist:
                dist.put(f"chunk_fwd{i}", got)
            act = _gdn_post_fwd(o, proj, gdn_norm_w[j], name=f"gdn_post_fwd{i}")
            sv.update(proj=proj, ab=ab, qkvn=qkvn, g_row=g_row, b_row=b_row, o=o, states=states, act=act)
            w_out = full["gdn_w_out"][j]
        else:
            w_in = full["dswa_w_in"][j]
            qkv = _mm(h, w_in, name=f"dswa_proj{i}")
            o_n, lse_n, got = _dswa_attn_fwd(qkv, bias, name=f"dswa_attn_fwd{i}",
                                             comm=dist and dist.gather_comm(f"attn_fwd{i}"))
            if dist:
                dist.put(f"attn_fwd{i}", got)
            act = _dswa_combine_fwd(o_n, lse_n, name=f"dswa_comb_fwd{i}")
            sv.update(qkv=qkv, o_n=o_n, lse_n=lse_n, act=act)
            w_out = full["dswa_w_out"][j]
        cur, h2 = _mm(act, w_out, name=f"mix_out{i}", out_dtypes=(F32, BF16), epilogue=_ep_residual_norm,
                      extras=(cur,), vecs=(row(norm_mlp[i]),))
        sv["x_mid"] = cur
        u, a = _mm_gather(dist, f"mlp_up{i}", h2, full["mlp_w1"][i], name=f"mlp_up{i}", out_dtypes=(BF16, BF16),
                          epilogue=lambda acc: (acc, jnp.square(jnp.maximum(acc, 0.0))))
        if i + 1 < DEPTH:
            cur, h = _mm_gather(dist, f"mlp_down{i}", a, full["mlp_w2"][i], name=f"mlp_down{i}", out_dtypes=(F32, BF16), tk=2048,
                                epilogue=_ep_residual_norm, extras=(cur,), vecs=(row(norm_mix[i + 1]),))
        else:
            cur = _mm_gather(dist, f"mlp_down{i}", a, full["mlp_w2"][i], name=f"mlp_down{i}", tk=2048,
                             epilogue=lambda acc, r: (acc + r,), extras=(cur,))
        sv.update(h2=h2, u=u, a=a)
        saved.append(sv)

    loss_part, dcur, dcur_b, dg_final = _loss_head(cur, norm_final, target, name="loss_head")

    g_norm_mix, g_norm_mlp = [None] * DEPTH, [None] * DEPTH
    g_big = {n: [None] * len(full[n]) for n in _BIG}
    g_conv, g_alog, g_dt, g_nw = [None] * 2, [None] * 2, [None] * 2, [None] * 2
    d_table_t = jnp.zeros((DSWA_HEADS, REL_BUCKETS), F32)
    for i in reversed(range(DEPTH)):
        j = i // 2
        sv = saved[i]
        w1, w2 = full["mlp_w1"][i], full["mlp_w2"][i]
        du = _mm(dcur_b, w2, tb=True, name=f"mlp_down_bwd{i}", out_dtypes=(BF16,),
                 epilogue=lambda acc, uu: (acc * (2.0 * jnp.maximum(uu.astype(F32), 0.0)),), extras=(sv["u"],))
        g_big["mlp_w2"][i] = _mm(sv["a"], dcur_b, ta=True, name=f"mlp_w2_grad{i}", out_dtypes=(BF16,), shard="rows")
        g_big["mlp_w1"][i] = _mm(sv["h2"], du, ta=True, name=f"mlp_w1_grad{i}", out_dtypes=(BF16,), shard="cols")
        dmid, dmid_b, g_norm_mlp[i] = _mm(du, w1, tb=True, name=f"mlp_up_bwd{i}", out_dtypes=(F32, BF16),
                                          epilogue=_ep_rms_bwd, extras=(sv["x_mid"], dcur), vecs=(row(norm_mlp[i]),),
                                          vec_out=True)
        if i % 2 == 0:
            w_in, w_out = full["gdn_w_in"][j], full["gdn_w_out"][j]
            dact = _mm(dmid_b, w_out, tb=True, name=f"mix_out_bwd{i}")
            g_big["gdn_w_out"][j] = _mm(sv["act"], dmid_b, ta=True, name=f"mix_out_grad{i}", out_dtypes=(BF16,),
                                        shard="rows")
            do, dz, g_nw[j] = _gdn_post_bwd(sv["o"], sv["proj"], gdn_norm_w[j], dact, name=f"gdn_post_bwd{i}")
            dqkvn, dg_row, db_row, got = _gdn_chunk_bwd(sv["qkvn"], sv["g_row"], sv["b_row"], sv["states"], do,
                                                        name=f"gdn_chunk_bwd{i}",
                                                        comm=dist and dist.send_comm(f"chunk_bwd{i}", g_big))
            if dist:
                dist.got(f"chunk_bwd{i}", got)
            dproj, g_conv[j], got = _gdn_pre_bwd(sv["proj"], conv_full[j], dqkvn, dz, name=f"gdn_pre_bwd{i}",
                                                 comm=dist and dist.send_comm(f"pre_bwd{i}", g_big))
            if dist:
                dist.got(f"pre_bwd{i}", got)
            nh2 = 2 * GDN_HEADS
            da_, db_, g_alog[j], g_dt[j] = _gdn_gate_bwd(sv["ab"][:, :nh2], sv["ab"][:, nh2:], gdn_a_log[j], gdn_dt_bias[j],
                                                         dg_row.reshape(nh2, s).T, db_row.reshape(nh2, s).T,
                                                         name=f"gdn_gate_bwd{i}")
            dab = jnp.concatenate([da_, db_], axis=1)
            gw_main = _mm(sv["h"], dproj, ta=True, name=f"gdn_w_in_grad{i}", out_dtypes=(BF16,))
            gw_ab = _mm(sv["h"], dab, ta=True, name=f"gdn_w_ab_grad{i}", out_dtypes=(BF16,))
            g_big["gdn_w_in"][j] = jnp.concatenate([gw_main, gw_ab], axis=1)
            dh_ab = _mm(dab, w_in[:, GDN_MAIN:], tb=True, name=f"gdn_proj_ab_bwd{i}")
            comm = dist and dist.send_comm(f"gdn_proj_bwd{i}", g_big)
            res = _mm(dproj, w_in, b_cols=(0, GDN_MAIN), tb=True, name=f"gdn_proj_bwd{i}", out_dtypes=(F32, BF16), tm=512, tk=2048,
                      epilogue=lambda acc, r, x, dres, g: _ep_rms_bwd(acc + r, x, dres, g),
                      extras=(dh_ab, sv["x_in"], dmid), vecs=(row(norm_mix[i]),), vec_out=True, comm=comm)
            if comm:
                res, got = res
                dist.got(f"gdn_proj_bwd{i}", got)
            dcur, dcur_b, g_norm_mix[i] = res
        else:
            w_in, w_out = full["dswa_w_in"][j], full["dswa_w_out"][j]
            dact = _mm(dmid_b, w_out, tb=True, name=f"mix_out_bwd{i}")
            g_big["dswa_w_out"][j] = _mm(sv["act"], dmid_b, ta=True, name=f"mix_out_grad{i}", out_dtypes=(BF16,),
                                         shard="rows")
            do_n, corr_n = _dswa_combine_bwd(sv["o_n"], sv["lse_n"], dact, name=f"dswa_comb_bwd{i}")
            *dqkv, dbias, got = _dswa_attn_bwd(sv["qkv"], bias, sv["lse_n"], do_n, corr_n, name=f"dswa_attn_bwd{i}",
                                               comm=dist and dist.send_comm(f"attn_bwd{i}", g_big))
            if dist:
                dist.got(f"attn_bwd{i}", got)
            d_table_t = d_table_t + _dswa_dtable(dbias.reshape(DSWA_HEADS, -1), onehot, name=f"dswa_dtable{i}")
            g_big["dswa_w_in"][j] = jnp.concatenate(
                [_mm(sv["h"], dt, ta=True, name=f"dswa_w_in_grad{i}_{t}", out_dtypes=(BF16,)) for t, dt in enumerate(dqkv)],
                axis=1)
            cols = [(t * DSWA_WIDTH, DSWA_WIDTH) for t in range(3)]
            dh = _mm(dqkv[0], w_in, b_cols=cols[0], tb=True, name=f"dswa_proj_bwd{i}_0")
            dh = _mm(dqkv[1], w_in, b_cols=cols[1], tb=True, name=f"dswa_proj_bwd{i}_1",
                     epilogue=lambda acc, r: (acc + r,), extras=(dh,))
            dcur, dcur_b, g_norm_mix[i] = _mm(
                dqkv[2], w_in, b_cols=cols[2], tb=True, name=f"dswa_proj_bwd{i}_2", out_dtypes=(F32, BF16), tm=512,
                epilogue=lambda acc, r, x, dres, g: _ep_rms_bwd(acc + r, x, dres, g),
                extras=(dh, sv["x_in"], dmid), vecs=(row(norm_mix[i]),), vec_out=True)

    rep = dict(norm_mix=jnp.concatenate(g_norm_mix, axis=0), norm_mlp=jnp.concatenate(g_norm_mlp, axis=0),
               norm_final=dg_final.reshape(-1), rel_bias=d_table_t.T,
               gdn_a_log=jnp.stack(g_alog).reshape(gdn_a_log.shape), gdn_dt_bias=jnp.stack(g_dt).reshape(gdn_dt_bias.shape),
               gdn_norm_w=jnp.stack(g_nw).reshape(gdn_norm_w.shape))
    return loss_part, dcur, g_big, rep, g_conv
```

```python
import functools
import math

import jax
import jax.numpy as jnp
import numpy as np
from jax import lax
from jax.experimental import pallas as pl
from jax.experimental.pallas import tpu as pltpu

F32 = jnp.float32
BF16 = jnp.bfloat16
HP = lax.Precision.HIGHEST

N_DEV = 8
D_MODEL = 1024
DEPTH = 4
RMS_EPS = 1e-6
NEG_INF = -1e30

GDN_HEADS = 8
GDN_DK = 128
GDN_CONV = 5
GDN_CHUNK = 128
GDN_QKV = 3 * GDN_HEADS * GDN_DK
GDN_MAIN = GDN_QKV + GDN_HEADS * GDN_DK
GDN_AB = 4 * GDN_HEADS

DSWA_DILS = (1, 4, 16)
DSWA_HG = 6
DSWA_E = 64
DSWA_HEADS = 18
DSWA_WIDTH = DSWA_HEADS * DSWA_E
DSWA_HALF = 64
DSWA_PG = DSWA_HG // 2
DSWA_UNROLL = 8
REL_BUCKETS = 32
REL_MAX_DIST = 1024

ADAM_LR = 0.001
ADAM_B1 = 0.9
ADAM_B2 = 0.999
ADAM_EPS = 1e-08
ADAM_WD = 0.01
ADAM_STEP = 10

VMEM_LIMIT = 56 * 1024 * 1024


def _cparams(sem=None, **kw):
    return pltpu.CompilerParams(dimension_semantics=sem, vmem_limit_bytes=VMEM_LIMIT, **kw)


def _pick(dim, cands):
    for c in cands:
        if dim % c == 0:
            return c
    return dim


def _bdot(a, b):
    return jnp.dot(a.astype(BF16), b.astype(BF16), preferred_element_type=F32)


def _bdot_nt(a, b):
    return lax.dot_general(a.astype(BF16), b.astype(BF16), (((1,), (1,)), ((), ())),
                           preferred_element_type=F32)


def _bdot_tn(a, b):
    return lax.dot_general(a.astype(BF16), b.astype(BF16), (((0,), (0,)), ((), ())),
                           preferred_element_type=F32)


def _hdot(a, b):
    return jnp.dot(a, b, precision=HP, preferred_element_type=F32)


def _hdot_nt(a, b):
    return lax.dot_general(a, b, (((1,), (1,)), ((), ())), precision=HP, preferred_element_type=F32)


def _sigmoid(x):
    return 1.0 / (1.0 + jnp.exp(-x))


def _mm(a, b, *, name, ta=False, tb=False, out_dtypes=(F32,), epilogue=None, extras=(), vecs=(), vec_out=False,
        tm=None, tn=None, tk=None, shard=None, comm=None, b_cols=None):
    if ta:
        kdim, m = a.shape
    else:
        m, kdim = a.shape
    b0, bsz = b_cols or (0, b.shape[1])
    n = b.shape[0] if tb else bsz
    assert not tb or kdim == bsz
    if shard == "rows":
        tm = m // N_DEV if (m // N_DEV) % 128 == 0 else m
    if shard == "cols":
        tn = n // N_DEV
    tm = tm or _pick(m, (1024, 1152, 512, 384, 256, 128))
    tn = tn or _pick(n, (1024, 1152, 512, 384, 256, 128))
    tk = tk or _pick(kdim, ((2048,) if ta else ()) + (1024, 1152, 512, 384, 256, 128))
    nk = kdim // tk
    n_out = len(out_dtypes) + (1 if vec_out else 0)
    n_ex = len(extras) + len(vecs)
    rows_all = shard == "rows" and tm == m

    gi, gj = m // tm, n // tn

    def body(*refs):
        i, j, k = pl.program_id(0), pl.program_id(1), pl.program_id(2)
        inner = (j == 0) & (k == 0)
        ins, out_refs, (acc_ref,) = _comm_hooks(
            comm, refs, 2 + n_ex, n_out, 1, (i == 0) & inner, (i == (3 * gi) // 4) & inner,
            (i == gi - 1) & (j == gj - 1) & (k == nk - 1))
        a_ref, b_ref, ex_refs = ins[0], ins[1], ins[2:]
        if vec_out:
            out_refs, vec_ref = out_refs[:-1], out_refs[-1]

        @pl.when(k == 0)
        def _():
            acc_ref[...] = jnp.zeros_like(acc_ref)

        av = a_ref[...].astype(BF16)
        bv = b_ref[...].astype(BF16)
        dims = (((0 if ta else 1,), (1 if tb else 0,)), ((), ()))
        acc_ref[...] += lax.dot_general(av, bv, dims, preferred_element_type=F32)

        @pl.when(k == nk - 1)
        def _():
            acc = acc_ref[...]
            outs = (acc,) if epilogue is None else epilogue(acc, *[r[...] for r in ex_refs])
            if vec_out:
                part = outs[-1]

                @pl.when(i == 0)
                def _():
                    vec_ref[...] = part

                @pl.when(i > 0)
                def _():
                    vec_ref[...] += part
            for r, o in zip(out_refs, outs):
                if rows_all:
                    for p in range(N_DEV):
                        r[p] = o[p * (m // N_DEV):(p + 1) * (m // N_DEV)].astype(r.dtype)
                else:
                    r[...] = o.astype(r.dtype)

    a_spec = pl.BlockSpec((tk, tm), lambda i, j, k: (k, i)) if ta else pl.BlockSpec((tm, tk), lambda i, j, k: (i, k))
    assert b0 % (tk if tb else tn) == 0
    boff = b0 // (tk if tb else tn)
    b_spec = (pl.BlockSpec((tn, tk), lambda i, j, k: (j, k + boff)) if tb
              else pl.BlockSpec((tk, tn), lambda i, j, k: (k, j + boff)))
    o_spec = pl.BlockSpec((tm, tn), lambda i, j, k: (i, j))
    v_spec = pl.BlockSpec((1, tn), lambda i, j, k: (0, j))
    out_specs = [o_spec] * len(out_dtypes) + ([v_spec] if vec_out else [])
    out_shape = [jax.ShapeDtypeStruct((m, n), dt) for dt in out_dtypes]
    out_shape += [jax.ShapeDtypeStruct((1, n), F32)] if vec_out else []
    if shard == "rows":
        out_shape = [jax.ShapeDtypeStruct((N_DEV, m // N_DEV, n), out_dtypes[0])]
        out_specs = [pl.BlockSpec((N_DEV, m // N_DEV, tn), lambda i, j, k: (0, 0, j)) if rows_all
                     else pl.BlockSpec((None, tm, tn), lambda i, j, k: (i, 0, j))]
    if shard == "cols":
        out_shape = [jax.ShapeDtypeStruct((N_DEV, m, tn), out_dtypes[0])]
        out_specs = [pl.BlockSpec((None, tm, tn), lambda i, j, k: (j, i, 0))]
    c_in, c_out, c_shape, c_scr = _comm_specs(comm)
    outs = pl.pallas_call(
        body, name=name,
        grid=(gi, gj, nk),
        in_specs=[a_spec, b_spec] + [o_spec] * len(extras) + [v_spec] * len(vecs) + c_in,
        out_specs=out_specs + c_out,
        out_shape=out_shape + c_shape,
        scratch_shapes=[pltpu.VMEM((tm, tn), F32)] + c_scr,
        compiler_params=_cparams(("arbitrary",) * 3 if comm or vec_out else ("parallel", "parallel", "arbitrary")),
    )(a, b, *extras, *vecs, *(comm.arrays if comm else []))
    res = outs[0] if n_out == 1 else tuple(outs[:n_out])
    return (res, outs[n_out:]) if comm else res


def _rms_fwd(x, g, *, name):
    s, d = x.shape
    tr = _pick(s, (512, 256, 128))

    def body(x_ref, g_ref, h_ref):
        xv = x_ref[...]
        r = lax.rsqrt(jnp.mean(xv * xv, axis=-1, keepdims=True) + RMS_EPS)
        h_ref[...] = (xv * r * g_ref[...]).astype(h_ref.dtype)

    return pl.pallas_call(
        body, name=name, grid=(s // tr,),
        in_specs=[pl.BlockSpec((tr, d), lambda i: (i, 0)), pl.BlockSpec((1, d), lambda i: (0, 0))],
        out_specs=pl.BlockSpec((tr, d), lambda i: (i, 0)),
        out_shape=jax.ShapeDtypeStruct((s, d), BF16),
        compiler_params=_cparams(("parallel",)),
    )(x, g.reshape(1, d))


def _loss_head(x, g, target, *, name):
    s, d = x.shape
    tr = _pick(s, (512, 256, 128))

    def body(x_ref, g_ref, t_ref, loss_ref, dx_ref, dxb_ref, dg_ref):
        i = pl.program_id(0)
        xv = x_ref[...]
        gv = g_ref[...]
        r = lax.rsqrt(jnp.mean(xv * xv, axis=-1, keepdims=True) + RMS_EPS)
        xn = xv * r
        err = xn * gv - t_ref[...]
        lpart = 0.5 * jnp.sum(jnp.mean(err * err, axis=-1, keepdims=True), axis=0, keepdims=True)
        dy = err * (1.0 / d)
        dn = dy * gv
        dx = r * (dn - xn * jnp.mean(dn * xn, axis=-1, keepdims=True))
        dx_ref[...] = dx
        dxb_ref[...] = dx.astype(dxb_ref.dtype)
        gpart = jnp.sum(dy * xn, axis=0, keepdims=True)

        @pl.when(i == 0)
        def _():
            dg_ref[...] = gpart
            loss_ref[...] = lpart

        @pl.when(i > 0)
        def _():
            dg_ref[...] += gpart
            loss_ref[...] += lpart

    row = pl.BlockSpec((tr, d), lambda i: (i, 0))
    vec = pl.BlockSpec((1, d), lambda i: (0, 0))
    one = pl.BlockSpec((1, 1), lambda i: (0, 0))
    return pl.pallas_call(
        body, name=name, grid=(s // tr,),
        in_specs=[row, vec, row], out_specs=[one, row, row, vec],
        out_shape=[jax.ShapeDtypeStruct((1, 1), F32), jax.ShapeDtypeStruct((s, d), F32),
                   jax.ShapeDtypeStruct((s, d), BF16), jax.ShapeDtypeStruct((1, d), F32)],
        compiler_params=_cparams(("arbitrary",)),
    )(x, g.reshape(1, d), target)


def _shift_rows(x, sft, rows):
    s = x.shape[0]
    if sft == 0:
        return x
    y = pltpu.roll(x, (-sft) % s, 0)
    edge = slice(0, 8) if sft < 0 else slice(s - 8, s)
    ok = (rows[edge] + sft >= 0) & (rows[edge] + sft < s)
    fixed = jnp.where(ok, y[edge], 0.0)
    return jnp.concatenate([fixed, y[8:]] if sft < 0 else [y[:s - 8], fixed], axis=0)


def _gdn_pre_fwd(proj, conv_w, *, name):
    s = proj.shape[0]
    nblk = GDN_QKV // 128
    pad = GDN_CONV // 2

    def body(x_ref, w_ref, o_ref):
        j = pl.program_id(0)
        x = x_ref[...]
        rows = lax.broadcasted_iota(jnp.int32, x.shape, 0)
        c = jnp.zeros_like(x)
        for t in range(GDN_CONV):
            c = c + w_ref[pl.ds(t, 1), :] * _shift_rows(x, t - pad, rows)
        a = c * _sigmoid(c)
        rinv = lax.rsqrt(jnp.sum(a * a, axis=-1, keepdims=True) + 1e-6)
        scale = jnp.where(j < GDN_HEADS, GDN_DK ** -0.5, 1.0)
        o_ref[...] = jnp.where(j >= 2 * GDN_HEADS, a, a * (rinv * scale))

    return pl.pallas_call(
        body, name=name, grid=(nblk,),
        in_specs=[pl.BlockSpec((s, 128), lambda j: (0, j)), pl.BlockSpec((GDN_CONV, 128), lambda j: (0, j))],
        out_specs=pl.BlockSpec((s, 128), lambda j: (0, j)),
        out_shape=jax.ShapeDtypeStruct((s, GDN_QKV), F32),
        compiler_params=_cparams(("parallel",)),
    )(proj, conv_w)


def _gdn_pre_bwd(proj, conv_w, dqkv, dproj, *, name, comm=None):
    s = proj.shape[0]
    nblk = GDN_QKV // 128
    pad = GDN_CONV // 2

    def body(*refs):
        j = pl.program_id(0)
        (x_ref, w_ref, df_ref, dbk_ref, _), (dx_ref, dw_ref), _ = _comm_hooks(
            comm, refs, 5, 2, 0, j == 0, j == nblk // 2, j == nblk - 1)
        x = x_ref[...]
        rows = lax.broadcasted_iota(jnp.int32, x.shape, 0)
        xs = [_shift_rows(x, t - pad, rows) for t in range(GDN_CONV)]
        c = jnp.zeros_like(x)
        for t in range(GDN_CONV):
            c = c + w_ref[pl.ds(t, 1), :] * xs[t]
        sg = _sigmoid(c)
        a = c * sg
        rinv = lax.rsqrt(jnp.sum(a * a, axis=-1, keepdims=True) + 1e-6)
        scale = jnp.where(j < GDN_HEADS, GDN_DK ** -0.5, 1.0)
        dy = df_ref[...] + dbk_ref[...]
        nh = a * rinv
        da_n = (rinv * scale) * (dy - nh * jnp.sum(dy * nh, axis=-1, keepdims=True))
        da = jnp.where(j >= 2 * GDN_HEADS, dy, da_n)
        dc = da * (sg * (1.0 + c * (1.0 - sg)))
        dx = jnp.zeros_like(x)
        for t in range(GDN_CONV):
            dx = dx + w_ref[pl.ds(t, 1), :] * _shift_rows(dc, pad - t, rows)
            dw_ref[pl.ds(t, 1), :] = jnp.sum(dc * xs[t], axis=0, keepdims=True)
        dx_ref[...] = dx.astype(dx_ref.dtype)

    col = pl.BlockSpec((s, 128), lambda j: (0, j))
    wsp = pl.BlockSpec((GDN_CONV, 128), lambda j: (0, j))
    c_in, c_out, c_shape, c_scr = _comm_specs(comm)
    res = pl.pallas_call(
        body, name=name, grid=(nblk,),
        in_specs=[col, wsp, col, col, pl.BlockSpec(memory_space=pl.ANY)] + c_in, out_specs=[col, wsp] + c_out,
        out_shape=[jax.ShapeDtypeStruct(dproj.shape, BF16), jax.ShapeDtypeStruct((GDN_CONV, GDN_QKV), F32)] + c_shape,
        input_output_aliases={4: 0},
        scratch_shapes=c_scr,
        compiler_params=_cparams(("arbitrary",) if comm else ("parallel",)),
    )(proj, conv_w, dqkv[0], dqkv[1], dproj, *(comm.arrays if comm else []))
    return res[0], res[1], res[2:]


def _softplus(x):
    return jnp.maximum(x, 0.0) + jnp.log(1.0 + jnp.exp(-jnp.abs(x)))


def _gdn_gate_fwd(a, b, a_log, dt_bias, *, name):
    s = a.shape[0]
    nh = 2 * GDN_HEADS

    def body(a_ref, b_ref, al_ref, dt_ref, g_ref, be_ref):
        g_ref[...] = -jnp.exp(al_ref[...]) * _softplus(a_ref[...] + dt_ref[...])
        be_ref[...] = _sigmoid(b_ref[...])

    return pl.pallas_call(
        body, name=name,
        out_shape=[jax.ShapeDtypeStruct((s, nh), F32), jax.ShapeDtypeStruct((s, nh), F32)],
        compiler_params=_cparams(),
    )(a, b, a_log.reshape(1, nh), dt_bias.reshape(1, nh))


def _gdn_gate_bwd(a, b, a_log, dt_bias, dg, dbeta, *, name):
    s = a.shape[0]
    nh = 2 * GDN_HEADS

    def body(a_ref, b_ref, al_ref, dt_ref, dg_ref, db_ref, da_ref, dbb_ref, dal_ref, ddt_ref):
        ea = jnp.exp(al_ref[...])
        z = a_ref[...] + dt_ref[...]
        dgv = dg_ref[...]
        dz = dgv * (-ea) * _sigmoid(z)
        dal_ref[...] = jnp.sum(dgv * (-ea) * _softplus(z), axis=0, keepdims=True)
        ddt_ref[...] = jnp.sum(dz, axis=0, keepdims=True)
        sb = _sigmoid(b_ref[...])
        da_ref[...] = dz
        dbb_ref[...] = db_ref[...] * sb * (1.0 - sb)

    return pl.pallas_call(
        body, name=name,
        out_shape=[jax.ShapeDtypeStruct((s, nh), F32), jax.ShapeDtypeStruct((s, nh), F32),
                   jax.ShapeDtypeStruct((1, nh), F32), jax.ShapeDtypeStruct((1, nh), F32)],
        compiler_params=_cparams(),
    )(a, b, a_log.reshape(1, nh), dt_bias.reshape(1, nh), dg, dbeta)


def _chunk_masks(d):
    c = GDN_CHUNK
    ii = lax.broadcasted_iota(jnp.int32, (c, c), 0)
    jj = lax.broadcasted_iota(jnp.int32, (c, c), 1)
    dif = (ii - jj) * (1 - 2 * d)
    mi = dif >= 0
    mit = dif <= 0
    ms = dif > 0
    eye = ii == jj
    bds = [(ii >> sh) == (jj >> sh) for sh in range(3, c.bit_length() - 1)]
    return dict(mi=mi, mit=mit, ms=ms, eye=eye, bds=bds,
                mif=mi.astype(F32), mitf=mit.astype(F32), eyef=eye.astype(F32))


class _V:
    def __init__(self, xs):
        self.xs = tuple(xs)

    def __add__(self, o):
        return _lift(lambda a, b: a + b)(self, o)

    def __radd__(self, o):
        return _lift(lambda a, b: b + a)(self, o)

    def __sub__(self, o):
        return _lift(lambda a, b: a - b)(self, o)

    def __rsub__(self, o):
        return _lift(lambda a, b: b - a)(self, o)

    def __mul__(self, o):
        return _lift(lambda a, b: a * b)(self, o)

    def __rmul__(self, o):
        return _lift(lambda a, b: b * a)(self, o)

    def __and__(self, o):
        return _lift(lambda a, b: a & b)(self, o)

    def __neg__(self):
        return _lift(lambda a: -a)(self)

    def __rtruediv__(self, o):
        return _lift(lambda a, b: b / a)(self, o)


def _lift(f):
    def g(*args, **kw):
        n = next(len(a.xs) for a in args if isinstance(a, _V))
        return _V(f(*[a.xs[i] if isinstance(a, _V) else a for a in args], **kw) for i in range(n))
    return g


_vwhere, _vsum, _vexp, _vnot = _lift(jnp.where), _lift(jnp.sum), _lift(jnp.exp), _lift(jnp.logical_not)
_vbdot, _vbdot_nt, _vbdot_tn = _lift(_bdot), _lift(_bdot_nt), _lift(_bdot_tn)
_vcat = _lift(lambda a, b: jnp.concatenate([a, b], axis=1))
_vlo = _lift(lambda a, n: a[:, :n])
_vhi = _lift(lambda a, n: a[:, n:])


def _both_masks(n):
    m = [_chunk_masks(d) for d in range(2)]
    mk = {key: _V([m[0][key]] * n + [m[1][key]] * n) for key in m[0] if key != "bds"}
    mk["bds"] = [_V([m[0]["bds"][i]] * n + [m[1]["bds"][i]] * n) for i in range(len(m[0]["bds"]))]
    return mk


def _tri_inv(a, mk):
    eyef = mk["eyef"]
    bds = mk["bds"]
    a8 = _vwhere(bds[0], a, 0.0)
    a2 = _vbdot(a8, a8)
    a4 = _vbdot(a2, a2)
    t = _vbdot(_vbdot(eyef - a8, eyef + a2), eyef + a4)
    for inner, outer in zip(bds, bds[1:] + [None]):
        off = _vnot(inner) if outer is None else (outer & _vnot(inner))
        low = _vwhere(off, a, 0.0)
        t = t - _vbdot(_vbdot(t, low), t)
    return t


def _chunk_prep(q, k, v, g_row, b_row, mk, tuw=None):
    dv = GDN_DK
    g_col = _vsum(mk["eyef"] * g_row, axis=1, keepdims=True)
    b_col = _vsum(mk["eyef"] * b_row, axis=1, keepdims=True)
    gc_col = _vsum(mk["mif"] * g_row, axis=1, keepdims=True)
    gc_row = _vsum(mk["mitf"] * g_col, axis=0, keepdims=True)
    gl = _vsum(g_row, axis=1, keepdims=True)
    decay = _vwhere(mk["mi"], _vexp(_vwhere(mk["mi"], gc_col - gc_row, 0.0)), 0.0)
    eg = _vexp(gc_col)
    e2 = _vexp(gl - gc_col)
    egl = _vexp(gl)
    kb = k * b_col
    pm = _vbdot_nt(kb, k)
    if tuw is None:
        t = _tri_inv(_vwhere(mk["ms"], pm * decay, 0.0), mk)
        sol = _vbdot(t, _vcat(v * b_col, kb * eg))
        u, w = _vlo(sol, dv), _vhi(sol, dv)
    else:
        t, u, w = tuw
    qm = _vbdot_nt(q, k)
    return dict(b_col=b_col, decay=decay, eg=eg, e2=e2, egl=egl, kb=kb, pm=pm, t=t, u=u, w=w,
                qm=qm, intra=qm * decay, qd=q * eg, kd=k * e2)


def _chunk_fwd_step(p, state):
    v_new = p["u"] - _vbdot(p["w"], state)
    o = _vbdot(p["qd"], state) + _vbdot(p["intra"], v_new)
    new_state = state * p["egl"] + _vbdot_tn(p["kd"], v_new)
    return o, new_state


def _chunk_bwd_step(q, k, v, p, mk, state, dso, do):
    dv_dim = GDN_DK
    v_new = p["u"] - _vbdot(p["w"], state)
    dvn = _vbdot_tn(p["intra"], do) + _vbdot(p["kd"], dso)
    dintra = _vbdot_nt(do, v_new)
    dqd = _vbdot_nt(do, state)
    ds = p["egl"] * dso + _vbdot_tn(p["qd"], do) - _vbdot_tn(p["w"], dvn)
    dkd = _vbdot_nt(v_new, dso)
    dgl = _vsum(_vsum(dso * state, axis=1, keepdims=True), axis=0, keepdims=True) * p["egl"]
    dw = -_vbdot_nt(dvn, state)
    drhs = _vbdot_tn(p["t"], _vcat(dvn, dw))
    dru, drw = _vlo(drhs, dv_dim), _vhi(drhs, dv_dim)
    da = -_vwhere(mk["ms"], _vbdot_nt(drhs, _vcat(p["u"], p["w"])), 0.0)
    b_col = p["b_col"]
    dv = dru * b_col
    dbeta = _vsum(dru * v, axis=1, keepdims=True)
    dkb = drw * p["eg"]
    deg = _vsum(drw * p["kb"], axis=1, keepdims=True)
    dp = da * p["decay"]
    ddecay = da * p["pm"]
    dkb = dkb + _vbdot(dp, k)
    dk = _vbdot_tn(dp, p["kb"])
    dqm = dintra * p["decay"]
    ddecay = ddecay + dintra * p["qm"]
    dq = _vbdot(dqm, k)
    dk = dk + _vbdot_tn(dqm, q)
    dd = ddecay * p["decay"]
    dgc_col = _vsum(dd, axis=1, keepdims=True)
    dgc_row = -_vsum(dd, axis=0, keepdims=True)
    dq = dq + dqd * p["eg"]
    deg = deg + _vsum(dqd * q, axis=1, keepdims=True)
    dk = dk + dkd * p["e2"]
    de2 = _vsum(dkd * k, axis=1, keepdims=True) * p["e2"]
    dgl = dgl + _vsum(de2, axis=0, keepdims=True)
    dgc_col = dgc_col - de2 + deg * p["eg"]
    dk = dk + dkb * b_col
    dbeta = dbeta + _vsum(dkb * k, axis=1, keepdims=True)
    dgc_col = dgc_col + _vsum(mk["eyef"] * dgc_row, axis=1, keepdims=True)
    dg_row = _vsum(mk["mif"] * dgc_col, axis=0, keepdims=True) + dgl
    dbeta_row = _vsum(mk["eyef"] * dbeta, axis=0, keepdims=True)
    return dq, dk, dv, dg_row, dbeta_row, ds


def _gdn_chunk_fwd(qkvn, g5, b5, *, name, comm=None):
    s = qkvn.shape[0]
    c = GDN_CHUNK
    nc = s // c
    h_, dk = GDN_HEADS, GDN_DK

    def body(*refs):
        n = pl.program_id(0)
        ins, outs, (st_scr,) = _comm_hooks(comm, refs, 6, 10, 1, n == 0, n == (3 * nc) // 4, n == nc - 1)
        x_refs, g_refs, b_refs = ins[0:2], ins[2:4], ins[4:6]
        o_refs, st_refs, t_refs, u_refs, w_refs = outs[0:2], outs[2:4], outs[4:6], outs[6:8], outs[8:10]

        @pl.when(n == 0)
        def _():
            st_scr[...] = jnp.zeros_like(st_scr)

        ch = [(d, h) for d in range(2) for h in range(h_)]
        mk = _both_masks(h_)
        q, k, v = (_V(x_refs[d][:, (t * h_ + h) * dk:(t * h_ + h + 1) * dk] for d, h in ch) for t in range(3))
        g, b = (_V(r[d][0, h, 0] for d, h in ch) for r in (g_refs, b_refs))
        state = _V(st_scr[d * h_ + h] for d, h in ch)
        p = _chunk_prep(q, k, v, g, b, mk)
        o, new_state = _chunk_fwd_step(p, state)
        for i, (d, h) in enumerate(ch):
            st_refs[d][h, 0] = state.xs[i]
            st_scr[d * h_ + h] = new_state.xs[i]
            o_refs[d][:, h * dk:(h + 1) * dk] = o.xs[i]
            t_refs[d][h, 0] = p["t"].xs[i].astype(BF16)
            u_refs[d][h, 0] = p["u"].xs[i]
            w_refs[d][h, 0] = p["w"].xs[i].astype(BF16)

    ce = (lambda n: n, lambda n: nc - 1 - n)
    xs = [pl.BlockSpec((c, 3 * h_ * dk), lambda n, d=d: (ce[d](n), 0)) for d in range(2)]
    gates = [pl.BlockSpec((1, h_, 1, 1, c), lambda n, d=d: (d, 0, ce[d](n), 0, 0)) for d in range(2)]
    os_ = [pl.BlockSpec((c, h_ * dk), lambda n, d=d: (ce[d](n), 0)) for d in range(2)]
    sts = [pl.BlockSpec((h_, 1, dk, dk), lambda n, d=d: (0, ce[d](n), 0, 0)) for d in range(2)]
    tcc = [pl.BlockSpec((h_, 1, c, c), lambda n, d=d: (0, ce[d](n), 0, 0)) for d in range(2)]
    tck = [pl.BlockSpec((h_, 1, c, dk), lambda n, d=d: (0, ce[d](n), 0, 0)) for d in range(2)]
    per_chunk = lambda last, dt: [jax.ShapeDtypeStruct((h_, nc, c, last), dt)] * 2
    c_in, c_out, c_shape, c_scr = _comm_specs(comm)
    res = pl.pallas_call(
        body, name=name, grid=(nc,),
        in_specs=xs + gates + gates + c_in,
        out_specs=os_ + sts + tcc + tck + tck + c_out,
        out_shape=[jax.ShapeDtypeStruct((s, h_ * dk), F32)] * 2 + [jax.ShapeDtypeStruct((h_, nc, dk, dk), F32)] * 2
        + per_chunk(c, BF16) + per_chunk(dk, F32) + per_chunk(dk, BF16) + c_shape,
        scratch_shapes=[pltpu.VMEM((2 * h_, dk, dk), F32)] + c_scr,
        compiler_params=_cparams(("arbitrary",)),
    )(qkvn, qkvn, g5, g5, b5, b5, *(comm.arrays if comm else []))
    return res[0:2], res[2:10], res[10:]


def _gdn_chunk_bwd(qkvn, g5, b5, states, do, *, name, comm=None):
    s = qkvn.shape[0]
    c = GDN_CHUNK
    nc = s // c
    h_, dk = GDN_HEADS, GDN_DK

    def body(*refs):
        i = pl.program_id(0)
        ins, outs, (ds_scr,) = _comm_hooks(comm, refs, 16, 6, 1, i == 0, i == nc // 2, i == nc - 1)
        x_refs, g_refs, b_refs, st_refs = ins[0:2], ins[2:4], ins[4:6], ins[6:8]
        t_refs, u_refs, w_refs, do_refs = ins[8:10], ins[10:12], ins[12:14], ins[14:16]
        dx_refs, dg_refs, db_refs = outs[0:2], outs[2:4], outs[4:6]

        @pl.when(i == 0)
        def _():
            ds_scr[...] = jnp.zeros_like(ds_scr)

        ch = [(d, h) for d in range(2) for h in range(h_)]
        mk = _both_masks(h_)
        q, k, v = (_V(x_refs[d][:, (t * h_ + h) * dk:(t * h_ + h + 1) * dk] for d, h in ch) for t in range(3))
        g, b = (_V(r[d][0, h, 0] for d, h in ch) for r in (g_refs, b_refs))
        state = _V(st_refs[d][h, 0] for d, h in ch)
        dso = _V(ds_scr[d * h_ + h] for d, h in ch)
        dov = _V(do_refs[d][:, h * dk:(h + 1) * dk] for d, h in ch)
        tuw = tuple(_V(r[d][h, 0] for d, h in ch) for r in (t_refs, u_refs, w_refs))
        res = _chunk_bwd_step(q, k, v, _chunk_prep(q, k, v, g, b, mk, tuw), mk, state, dso, dov)
        for (d, h), (dq, dkk, dvv, dg_r, db_r, ds) in zip(ch, zip(*[r.xs for r in res])):
            ds_scr[d * h_ + h] = ds
            dg_refs[d][h, 0] = dg_r
            db_refs[d][h, 0] = db_r
            for t, val in enumerate((dq, dkk, dvv)):
                dx_refs[d][:, (t * h_ + h) * dk:(t * h_ + h + 1) * dk] = val

    ce = (lambda i: nc - 1 - i, lambda i: i)
    both = lambda mk_spec: [mk_spec(d) for d in range(2)]
    xs = both(lambda d: pl.BlockSpec((c, 3 * h_ * dk), lambda i: (ce[d](i), 0)))
    gates = both(lambda d: pl.BlockSpec((1, h_, 1, 1, c), lambda i: (d, 0, ce[d](i), 0, 0)))
    sts = both(lambda d: pl.BlockSpec((h_, 1, dk, dk), lambda i: (0, ce[d](i), 0, 0)))
    tcc = both(lambda d: pl.BlockSpec((h_, 1, c, c), lambda i: (0, ce[d](i), 0, 0)))
    tck = both(lambda d: pl.BlockSpec((h_, 1, c, dk), lambda i: (0, ce[d](i), 0, 0)))
    dos = both(lambda d: pl.BlockSpec((c, h_ * dk), lambda i: (ce[d](i), 0)))
    gouts = both(lambda d: pl.BlockSpec((h_, 1, 1, c), lambda i: (0, ce[d](i), 0, 0)))
    c_in, c_out, c_shape, c_scr = _comm_specs(comm)
    res = pl.pallas_call(
        body, name=name, grid=(nc,),
        in_specs=xs + gates + gates + sts + tcc + tck + tck + dos + c_in,
        out_specs=xs + gouts + gouts + c_out,
        out_shape=[jax.ShapeDtypeStruct((s, 3 * h_ * dk), F32)] * 2
        + [jax.ShapeDtypeStruct((h_, nc, 1, c), F32)] * 4 + c_shape,
        scratch_shapes=[pltpu.VMEM((2 * h_, dk, dk), F32)] + c_scr,
        compiler_params=_cparams(("arbitrary",)),
    )(qkvn, qkvn, g5, g5, b5, b5, *states, do, do, *(comm.arrays if comm else []))
    return res[0:2], jnp.stack(res[2:4]), jnp.stack(res[4:6]), res[6:]


def _gdn_post_fwd(o, z, norm_w, *, name):
    s = o[0].shape[0]
    h_, dk = GDN_HEADS, GDN_DK

    def body(of_ref, ob_ref, z_ref, w_ref, a_ref):
        ov = of_ref[...] + ob_ref[...]
        zv = z_ref[...]
        r = lax.rsqrt(jnp.mean(ov * ov, axis=-1, keepdims=True) + RMS_EPS)
        a_ref[...] = (ov * r * w_ref[...] * (zv * _sigmoid(zv))).astype(a_ref.dtype)

    col = pl.BlockSpec((s, dk), lambda h: (0, h))
    return pl.pallas_call(
        body, name=name, grid=(h_,),
        in_specs=[col, col, pl.BlockSpec((s, dk), lambda h: (0, 3 * h_ + h)), pl.BlockSpec((1, dk), lambda h: (0, 0))],
        out_specs=col,
        out_shape=jax.ShapeDtypeStruct((s, h_ * dk), BF16),
        compiler_params=_cparams(("parallel",)),
    )(o[0], o[1], z, norm_w.reshape(1, dk))


def _gdn_post_bwd(o, z, norm_w, dact, *, name):
    s = o[0].shape[0]
    h_, dk = GDN_HEADS, GDN_DK

    def body(of_ref, ob_ref, z_ref, w_ref, da_ref, do_ref, dz_ref, dw_ref):
        h = pl.program_id(0)
        ov = of_ref[...] + ob_ref[...]
        zv = z_ref[...]
        wv = w_ref[...]
        dav = da_ref[...]
        r = lax.rsqrt(jnp.mean(ov * ov, axis=-1, keepdims=True) + RMS_EPS)
        nrm = ov * r
        sg = _sigmoid(zv)
        sz = zv * sg
        dn = dav * wv * sz
        do_ref[...] = r * (dn - nrm * jnp.mean(dn * nrm, axis=-1, keepdims=True))
        dz_ref[...] = (dav * nrm * wv * (sg * (1.0 + zv * (1.0 - sg)))).astype(dz_ref.dtype)
        part = jnp.sum(dav * nrm * sz, axis=0, keepdims=True)

        @pl.when(h == 0)
        def _():
            dw_ref[...] = part

        @pl.when(h > 0)
        def _():
            dw_ref[...] += part

    col = pl.BlockSpec((s, dk), lambda h: (0, h))
    vec = pl.BlockSpec((1, dk), lambda h: (0, 0))
    return pl.pallas_call(
        body, name=name, grid=(h_,),
        in_specs=[col, col, pl.BlockSpec((s, dk), lambda h: (0, 3 * h_ + h)), vec, col],
        out_specs=[col, pl.BlockSpec((s, dk), lambda h: (0, 3 * h_ + h)), vec],
        out_shape=[jax.ShapeDtypeStruct((s, h_ * dk), F32), jax.ShapeDtypeStruct((s, GDN_MAIN), BF16),
                   jax.ShapeDtypeStruct((1, dk), F32)],
        compiler_params=_cparams(("arbitrary",)),
    )(o[0], o[1], z, norm_w.reshape(1, dk), dact)


def _rel_bucket(rel):
    nb = REL_BUCKETS // 2
    max_exact = nb // 2
    ret = jnp.where(rel > 0, nb, 0)
    n = jnp.abs(rel)
    nf = jnp.maximum(n, 1).astype(F32)
    large = max_exact + (jnp.log(nf / max_exact) / math.log(REL_MAX_DIST / max_exact)
                         * (nb - max_exact)).astype(jnp.int32)
    large = jnp.minimum(large, nb - 1)
    return ret + jnp.where(n < max_exact, n, large)


def _bucket_onehot():
    half = DSWA_HALF
    outs = []
    for dil in DSWA_DILS:
        rel = (jnp.arange(3 * half)[None, :] - half - jnp.arange(half)[:, None]) * dil
        outs.append(jax.nn.one_hot(_rel_bucket(rel).reshape(-1), REL_BUCKETS, dtype=F32, axis=0))
    return jnp.stack(outs)


def _head_group_select(vals):
    rows = lax.broadcasted_iota(jnp.int32, vals[0].shape, 0)
    return jnp.where(rows < DSWA_HG, vals[0], jnp.where(rows < 2 * DSWA_HG, vals[1], vals[2]))


def _dswa_bias(table_t, onehot, *, name):
    p = onehot.shape[-1]

    def body(t_ref, oh_ref, b_ref):
        b_ref[...] = _head_group_select([_hdot(t_ref[...], oh_ref[g]) for g in range(3)])

    return pl.pallas_call(body, name=name, out_shape=jax.ShapeDtypeStruct((DSWA_HEADS, p), F32),
                          compiler_params=_cparams())(table_t, onehot)


def _dswa_dtable(dbias, onehot, *, name):
    def body(d_ref, oh_ref, t_ref):
        t_ref[...] = _head_group_select([_hdot_nt(d_ref[...], oh_ref[g]) for g in range(3)])

    return pl.pallas_call(body, name=name, out_shape=jax.ShapeDtypeStruct((DSWA_HEADS, REL_BUCKETS), F32),
                          compiler_params=_cparams())(dbias, onehot)


def _rows(start, dil):
    if dil == 1:
        return pl.ds(pl.multiple_of(start, DSWA_HALF), DSWA_HALF)
    return pl.ds(start, DSWA_HALF, stride=dil)


def _attn_blocks(it, s, dil):
    half = DSWA_HALF
    nbs = s // half // dil
    ii = lax.broadcasted_iota(jnp.int32, (half, 3 * half), 0)
    jj = lax.broadcasted_iota(jnp.int32, (half, 3 * half), 1)
    band = jnp.abs(jj - half - ii) <= half
    out = []
    for u in range(DSWA_UNROLL):
        blk = it * DSWA_UNROLL + u
        r, b = blk // nbs, blk % nbs
        own = r + dil * half * b
        prev = own - jnp.where(b > 0, dil * half, 0)
        nxt = own + jnp.where(b < nbs - 1, dil * half, 0)
        ok = band & ((jj >= half) | (b > 0)) & ((jj < 2 * half) | (b < nbs - 1))
        out.append(((prev, own, nxt), ok))
    return out


def _attn_chains(q_ref, k_ref, v_ref, blocks, dil):
    lane = lax.broadcasted_iota(jnp.int32, (DSWA_HALF, 2 * DSWA_E), 1)
    qm, kw, vw, valid, hmask = [], [], [], [], []
    for (prev, own, nxt), ok in blocks:
        q = q_ref[_rows(own, dil), :].astype(BF16)
        k = jnp.concatenate([k_ref[_rows(st, dil), :] for st in (prev, own, nxt)], axis=0).astype(BF16)
        v = jnp.concatenate([v_ref[_rows(st, dil), :] for st in (prev, own, nxt)], axis=0).astype(BF16)
        for hd in range(2):
            mine = (lane < DSWA_E) if hd == 0 else (lane >= DSWA_E)
            qm.append(jnp.where(mine, q, jnp.zeros_like(q)))
            kw.append(k)
            vw.append(v)
            valid.append(ok)
            hmask.append(mine)
    return _V(qm), _V(kw), _V(vw), _V(valid), _V(hmask)


def _per_group(pr, fn):
    for gi, dil in enumerate(DSWA_DILS):
        pl.when(pr // DSWA_PG == gi)(functools.partial(fn, dil))


_vmax, _vlog = _lift(jnp.max), _lift(jnp.log)


def _dswa_attn_fwd(qkv, bias, *, name, comm=None):
    s = qkv.shape[0]
    half, e = DSWA_HALF, DSWA_E
    npair = DSWA_HEADS // 2

    def body(*refs):
        pr = pl.program_id(0)
        (q_ref, k_ref, v_ref, bias_ref), (o_ref, lse_ref), _ = _comm_hooks(
            comm, refs, 4, 2, 0, pr == 0, pr == (3 * npair) // 4, pr == npair - 1)
        bias_v = _V([bias_ref[0], bias_ref[1]] * DSWA_UNROLL)

        def run(dil):
            def step(it, carry):
                blocks = _attn_blocks(it, s, dil)
                qm, kw, vw, valid, hmask = _attn_chains(q_ref, k_ref, v_ref, blocks, dil)
                sc = _vwhere(valid, _vbdot_nt(qm, kw) * (e ** -0.5) + bias_v, NEG_INF)
                m = _vmax(sc, axis=-1, keepdims=True)
                p = _vexp(sc - m)
                l = _vsum(p, axis=-1, keepdims=True)
                o = _vbdot(p * (1.0 / l), vw)
                lse = m + _vlog(l)
                for u, ((_, own, _), _) in enumerate(blocks):
                    is_a = hmask.xs[2 * u]
                    o_ref[_rows(own, dil), :] = jnp.where(is_a, o.xs[2 * u], o.xs[2 * u + 1])
                    lse_ref[_rows(own, dil), :] = jnp.where(is_a, lse.xs[2 * u], lse.xs[2 * u + 1])
                return carry

            lax.fori_loop(0, s // half // DSWA_UNROLL, step, 0)

        _per_group(pr, run)

    col = lambda t: pl.BlockSpec((s, 2 * e), lambda p: (0, t * npair + p))
    pair = pl.BlockSpec((s, 2 * e), lambda p: (0, p))
    c_in, c_out, c_shape, c_scr = _comm_specs(comm)
    res = pl.pallas_call(
        body, name=name, grid=(npair,),
        in_specs=[col(0), col(1), col(2), pl.BlockSpec((2, half, 3 * half), lambda p: (p, 0, 0))] + c_in,
        out_specs=[pair, pair] + c_out,
        out_shape=[jax.ShapeDtypeStruct((s, npair * 2 * e), F32)] * 2 + c_shape,
        scratch_shapes=c_scr,
        compiler_params=_cparams(("arbitrary",)),
    )(qkv, qkv, qkv, bias, *(comm.arrays if comm else []))
    return res[0], res[1], res[2:]


def _dswa_attn_bwd(qkv, bias, lse, do, corr, *, name, comm=None):
    s = qkv.shape[0]
    half, e = DSWA_HALF, DSWA_E
    npair = DSWA_HEADS // 2
    w = 2 * e

    def body(*refs):
        pr = pl.program_id(0)
        (q_ref, k_ref, v_ref, bias_ref, lse_ref, do_ref, corr_ref), (dq_ref, dk_ref, dv_ref, db_ref), _ = _comm_hooks(
            comm, refs, 7, 4, 0, pr == 0, pr == npair // 2, pr == npair - 1)
        bias_v = _V([bias_ref[0], bias_ref[1]] * DSWA_UNROLL)
        dk_ref[...] = jnp.zeros_like(dk_ref)
        dv_ref[...] = jnp.zeros_like(dv_ref)

        def run(dil):
            def step(it, dbias):
                blocks = _attn_blocks(it, s, dil)
                qm, kw, vw, valid, hmask = _attn_chains(q_ref, k_ref, v_ref, blocks, dil)
                hd = [0, 1] * DSWA_UNROLL
                rows = [_rows(own, dil) for (_, own, _), _ in blocks for _ in range(2)]
                lse_c = _V(lse_ref[rw, :][:, h * e:h * e + 1] for rw, h in zip(rows, hd))
                corr_c = _V(corr_ref[rw, :][:, h * e:h * e + 1] for rw, h in zip(rows, hd))
                dov = _vwhere(hmask, _V(do_ref[rw, :] for rw in rows), 0.0)
                sc = _vbdot_nt(qm, kw) * (e ** -0.5) + bias_v
                p = _vwhere(valid, _vexp(_vwhere(valid, sc, 0.0) - lse_c), 0.0)
                dsc = p * (_vbdot_nt(dov, vw) + corr_c)
                dq = _vbdot(dsc, kw) * (e ** -0.5)
                dkc = _vbdot_tn(dsc, qm) * (e ** -0.5)
                dvc = _vbdot_tn(p, dov)
                for u, (starts, _) in enumerate(blocks):
                    dq_ref[_rows(starts[1], dil), :] = jnp.where(hmask.xs[2 * u], dq.xs[2 * u], dq.xs[2 * u + 1])
                    dk_u = dkc.xs[2 * u] + dkc.xs[2 * u + 1]
                    dv_u = dvc.xs[2 * u] + dvc.xs[2 * u + 1]
                    for t, st in enumerate(starts):
                        dk_ref[_rows(st, dil), :] += dk_u[t * half:(t + 1) * half]
                        dv_ref[_rows(st, dil), :] += dv_u[t * half:(t + 1) * half]
                da, db = dbias
                for u in range(DSWA_UNROLL):
                    da, db = da + dsc.xs[2 * u], db + dsc.xs[2 * u + 1]
                return da, db

            zero = jnp.zeros((half, 3 * half), F32)
            da, db = lax.fori_loop(0, s // half // DSWA_UNROLL, step, (zero, zero))
            db_ref[0] = da
            db_ref[1] = db

        _per_group(pr, run)

    col = lambda t: pl.BlockSpec((s, w), lambda p: (0, t * npair + p))
    ps = pl.BlockSpec((s, w), lambda p: (0, p))
    bs = pl.BlockSpec((2, half, 3 * half), lambda p: (p, 0, 0))
    c_in, c_out, c_shape, c_scr = _comm_specs(comm)
    res = pl.pallas_call(
        body, name=name, grid=(npair,),
        in_specs=[col(0), col(1), col(2), bs, ps, ps, ps] + c_in,
        out_specs=[ps, ps, ps, bs] + c_out,
        out_shape=[jax.ShapeDtypeStruct((s, npair * w), F32)] * 3
        + [jax.ShapeDtypeStruct((DSWA_HEADS, half, 3 * half), F32)] + c_shape,
        scratch_shapes=c_scr,
        compiler_params=_cparams(("arbitrary",)),
    )(qkv, qkv, qkv, bias, lse, do, corr, *(comm.arrays if comm else []))
    return res[0], res[1], res[2], res[3], res[4:]


def _pair_cols(g, j):
    w = 2 * DSWA_E
    return slice((g * DSWA_PG + j) * w, (g * DSWA_PG + j + 1) * w)


def _group_weights(l_ref, j):
    ls = [l_ref[:, _pair_cols(g, j)] for g in range(3)]
    m = jnp.maximum(jnp.maximum(ls[0], ls[1]), ls[2])
    es = [jnp.exp(x - m) for x in ls]
    inv = 1.0 / (es[0] + es[1] + es[2])
    return [x * inv for x in es]


def _dswa_combine_fwd(o, lse, *, name):
    s, wd = o.shape
    tr = _pick(s, (512, 256, 128))

    def body(o_ref, l_ref, c_ref):
        for j in range(DSWA_PG):
            al = _group_weights(l_ref, j)
            for g in range(3):
                c_ref[:, _pair_cols(g, j)] = (o_ref[:, _pair_cols(g, j)] * al[g]).astype(c_ref.dtype)

    row = pl.BlockSpec((tr, wd), lambda i: (i, 0))
    return pl.pallas_call(
        body, name=name, grid=(s // tr,),
        in_specs=[row, row], out_specs=row,
        out_shape=jax.ShapeDtypeStruct(o.shape, BF16),
        compiler_params=_cparams(("parallel",)),
    )(o, lse)


def _dswa_combine_bwd(o, lse, dc, *, name):
    s, wd = o.shape
    tr = _pick(s, (512, 256, 128))

    def body(o_ref, l_ref, dc_ref, do_ref, corr_ref):
        lane = lax.broadcasted_iota(jnp.int32, (tr, 2 * DSWA_E), 1)
        is_a = lane < DSWA_E
        for j in range(DSWA_PG):
            al = _group_weights(l_ref, j)
            tot = jnp.zeros((tr, 2 * DSWA_E), F32)
            for g in range(3):
                cols = _pair_cols(g, j)
                dcv = dc_ref[:, cols]
                do_ref[:, cols] = dcv * al[g]
                prod = dcv * o_ref[:, cols]
                dal = jnp.where(is_a, jnp.sum(jnp.where(is_a, prod, 0.0), axis=-1, keepdims=True),
                                jnp.sum(jnp.where(is_a, 0.0, prod), axis=-1, keepdims=True))
                tot = tot + al[g] * dal
            for g in range(3):
                corr_ref[:, _pair_cols(g, j)] = -al[g] * tot

    row = pl.BlockSpec((tr, wd), lambda i: (i, 0))
    return pl.pallas_call(
        body, name=name, grid=(s // tr,),
        in_specs=[row, row, row], out_specs=[row, row],
        out_shape=[jax.ShapeDtypeStruct(o.shape, F32)] * 2,
        compiler_params=_cparams(("parallel",)),
    )(o, lse, dc)


class _Comm:
    def __init__(self, mode, arrays, kinds=None):
        self.mode, self.arrays, self.kinds = mode, list(arrays), kinds
        self.n = len(self.arrays)

    def out_shapes(self):
        if self.mode == "exchange":
            return [jax.ShapeDtypeStruct(x.shape, x.dtype) for x in self.arrays]
        shapes = []
        for x, kd in zip(self.arrays, self.kinds):
            shp = list(x.shape)
            if kd == "stack":
                shp = [N_DEV] + shp
            else:
                shp[-2 if kd == "rows" else -1] *= N_DEV
            shapes.append(jax.ShapeDtypeStruct(tuple(shp), x.dtype))
        return shapes

    def scratch(self):
        return [pltpu.SemaphoreType.DMA((7 * self.n,)), pltpu.SemaphoreType.DMA((7 * self.n,)),
                pltpu.SemaphoreType.DMA((self.n,))]

    def bind(self, in_refs, out_refs, sems):
        self.x, self.o = in_refs, out_refs
        self.send_sems, self.recv_sems, self.local_sems = sems
        self.pos = (lax.axis_index("x"), lax.axis_index("y"), lax.axis_index("c"))

    def _slot(self, i, px, py, pc):
        p = 4 * px + 2 * py + pc
        kd = self.kinds[i]
        if kd == "stack":
            return self.o[i].at[p]
        nd = len(self.x[i].shape)
        ax = nd - 2 if kd == "rows" else nd - 1
        size = self.x[i].shape[ax]
        idx = tuple(pl.ds(p * size, size) if a == ax else slice(None) for a in range(nd))
        return self.o[i].at[idx]

    def _gcopy(self, i, k, block, to, src=None):
        return pltpu.make_async_remote_copy(
            src_ref=self._slot(i, *block) if src is None else src, dst_ref=self._slot(i, *block),
            send_sem=self.send_sems.at[7 * i + k], recv_sem=self.recv_sems.at[7 * i + k],
            device_id=to, device_id_type=pl.DeviceIdType.MESH)

    def _chips(self):
        mx, my, _ = self.pos
        return [(1 - mx, my), (mx, 1 - my), (1 - mx, 1 - my)]

    def _xcopies(self):
        mx, my, mc = self.pos
        me = 4 * mx + 2 * my + mc
        copies = []
        for k in range(1, N_DEV):
            px = 1 - mx if (k >> 2) & 1 else mx
            py = 1 - my if (k >> 1) & 1 else my
            pc = 1 - mc if k & 1 else mc
            for i in range(self.n):
                copies.append(pltpu.make_async_remote_copy(
                    src_ref=self.x[i].at[4 * px + 2 * py + pc], dst_ref=self.o[i].at[me],
                    send_sem=self.send_sems.at[7 * i + k - 1], recv_sem=self.recv_sems.at[7 * i + k - 1],
                    device_id=(px, py, pc), device_id_type=pl.DeviceIdType.MESH))
        return copies

    def _local(self):
        mx, my, mc = self.pos
        if self.mode == "exchange":
            me = 4 * mx + 2 * my + mc
            return [pltpu.make_async_copy(self.x[i].at[me], self.o[i].at[me], self.local_sems.at[i]) for i in range(self.n)]
        return [pltpu.make_async_copy(self.x[i], self._slot(i, mx, my, mc), self.local_sems.at[i]) for i in range(self.n)]

    def _first(self):
        mx, my, mc = self.pos
        me, sibling = (mx, my, mc), (mx, my, 1 - mc)
        first = [self._gcopy(i, 0, me, sibling, src=self.x[i]) for i in range(self.n)]
        first += [self._gcopy(i, 1 + j, me, (*chip, mc), src=self.x[i]) for j, chip in enumerate(self._chips())
                  for i in range(self.n)]
        return first

    def _passed(self):
        mx, my, mc = self.pos
        return [self._gcopy(i, 4 + j, (*chip, mc), (mx, my, 1 - mc)) for j, chip in enumerate(self._chips())
                for i in range(self.n)]

    def start(self):
        for cp in self._local() + (self._xcopies() if self.mode == "exchange" else self._first()):
            cp.start()

    def mid(self):
        if self.mode == "exchange":
            return
        mx, my, mc = self.pos
        passed = self._passed()
        for j, chip in enumerate(self._chips()):
            for i in range(self.n):
                self._gcopy(i, 1 + j, (*chip, mc), (mx, my, mc)).wait_recv()
                passed[j * self.n + i].start()

    def end(self):
        mx, my, mc = self.pos
        if self.mode == "exchange":
            copies = self._xcopies()
            for cp in copies:
                cp.wait_recv()
            for cp in copies:
                cp.wait_send()
        else:
            for i in range(self.n):
                self._gcopy(i, 0, (mx, my, 1 - mc), (mx, my, mc)).wait_recv()
                for j, chip in enumerate(self._chips()):
                    self._gcopy(i, 4 + j, (*chip, 1 - mc), (mx, my, mc)).wait_recv()
            for cp in self._first() + self._passed():
                cp.wait_send()
        for cp in self._local():
            cp.wait()

    def run(self, *, name):
        n = self.n

        def body(*refs):
            self.bind(refs[:n], refs[n:2 * n], refs[2 * n:])
            self.start()
            self.mid()
            self.end()

        anyspec = pl.BlockSpec(memory_space=pl.ANY)
        return pl.pallas_call(body, name=name, in_specs=[anyspec] * n, out_specs=[anyspec] * n,
                              out_shape=self.out_shapes(), scratch_shapes=self.scratch())(*self.arrays)


def _comm_specs(comm):
    if comm is None:
        return [], [], [], []
    anyspec = pl.BlockSpec(memory_space=pl.ANY)
    return [anyspec] * comm.n, [anyspec] * comm.n, comm.out_shapes(), comm.scratch()


def _comm_hooks(comm, refs, n_in, n_out, n_scr, first, mid, last):
    if comm is None:
        return refs[:n_in], refs[n_in:n_in + n_out], refs[n_in + n_out:]
    c = comm.n
    ins, cin = refs[:n_in], refs[n_in:n_in + c]
    outs, cout = refs[n_in + c:n_in + c + n_out], refs[n_in + c + n_out:n_in + 2 * c + n_out]
    scr, sems = refs[n_in + 2 * c + n_out:n_in + 2 * c + n_out + n_scr], refs[n_in + 2 * c + n_out + n_scr:]
    comm.bind(cin, cout, sems)
    pl.when(first)(comm.start)
    pl.when(mid)(comm.mid)
    pl.when(last)(comm.end)
    return ins, outs, scr


def _adamw_update(g, w, m, v):
    mn = ADAM_B1 * m + (1.0 - ADAM_B1) * g
    vn = ADAM_B2 * v + (1.0 - ADAM_B2) * (g * g)
    m_hat = mn / (1.0 - ADAM_B1 ** ADAM_STEP)
    v_hat = vn / (1.0 - ADAM_B2 ** ADAM_STEP)
    return -ADAM_LR * (m_hat / (jnp.sqrt(v_hat) + ADAM_EPS) + ADAM_WD * w), mn, vn


def _adamw_layers(recvs, w, m, v, *, name):
    nl, ks, ns = w.shape
    tr = _pick(ks, (64, 48))

    def body(*refs):
        rv_refs = refs[:nl]
        w_ref, m_ref, v_ref, g_ref, d_ref, nm_ref, nv_ref = refs[nl:]
        for l in range(nl):
            g = rv_refs[l][0].astype(F32)
            for q in range(1, N_DEV):
                g = g + rv_refs[l][q].astype(F32)
            delta, mn, vn = _adamw_update(g, w_ref[l], m_ref[l], v_ref[l])
            g_ref[l] = g
            d_ref[l] = delta
            nm_ref[l] = mn
            nv_ref[l] = vn

    row = pl.BlockSpec((nl, tr, ns), lambda i: (0, i, 0))
    return pl.pallas_call(
        body, name=name, grid=(ks // tr,),
        in_specs=[pl.BlockSpec((N_DEV, tr, ns), lambda i: (0, i, 0))] * nl + [row] * 3,
        out_specs=[row] * 4,
        out_shape=[jax.ShapeDtypeStruct((nl, ks, ns), F32)] * 4,
        compiler_params=_cparams(("parallel",)),
    )(*recvs, w, m, v)


def _adamw_reduce(recv, w, m, v, *, name):
    r, c = w.shape
    tr = _pick(r, (128, 64, 8))

    def body(rv_ref, w_ref, m_ref, v_ref, g_ref, d_ref, nm_ref, nv_ref):
        g = rv_ref[0]
        for q in range(1, N_DEV):
            g = g + rv_ref[q]
        delta, mn, vn = _adamw_update(g, w_ref[...], m_ref[...], v_ref[...])
        g_ref[...] = g
        d_ref[...] = delta
        nm_ref[...] = mn
        nv_ref[...] = vn

    row = pl.BlockSpec((tr, c), lambda i: (i, 0))
    return pl.pallas_call(
        body, name=name, grid=(r // tr,),
        in_specs=[pl.BlockSpec((N_DEV, tr, c), lambda i: (0, i, 0)), row, row, row],
        out_specs=[row] * 4,
        out_shape=[jax.ShapeDtypeStruct((r, c), F32)] * 4,
        compiler_params=_cparams(("parallel",)),
    )(recv, w, m, v)


_BIG = ("gdn_w_in", "gdn_w_out", "dswa_w_in", "dswa_w_out", "mlp_w1", "mlp_w2")
_SMALL = ("gdn_conv_w", "norm_mix", "norm_mlp", "norm_final", "rel_bias", "gdn_a_log", "gdn_dt_bias", "gdn_norm_w")
_ORDER = ("norm_mix", "norm_mlp", "norm_final", "rel_bias", "gdn_w_in", "gdn_conv_w", "gdn_a_log", "gdn_dt_bias",
          "gdn_norm_w", "gdn_w_out", "dswa_w_in", "dswa_w_out", "mlp_w1", "mlp_w2")
_KIND = dict(gdn_w_in="stack", gdn_w_out="rows", dswa_w_in="stack", dswa_w_out="rows", mlp_w1="cols", mlp_w2="rows")


def _pack_rows(arrs, align):
    rows, counts = [], []
    for a in arrs:
        flat = a.reshape(-1)
        n = -(-flat.shape[0] // D_MODEL)
        flat = jnp.pad(flat, (0, n * D_MODEL - flat.shape[0]))
        rows.append(flat.reshape(n, D_MODEL))
        counts.append(n)
    out = jnp.concatenate(rows, axis=0)
    total = -(-out.shape[0] // align) * align
    return jnp.pad(out, ((0, total - out.shape[0]), (0, 0))), counts


def _unpack_rows(slab, shapes):
    outs, r = [], 0
    for shp in shapes:
        size = int(np.prod(shp))
        n = -(-size // D_MODEL)
        outs.append(slab[r:r + n].reshape(-1)[:size].reshape(shp))
        r += n
    return outs


def _col_shards(full, nshard):
    lead = full.shape[:-1]
    n = full.shape[-1] // nshard
    t = full.reshape(lead + (nshard, n))
    return jnp.moveaxis(t, -2, 0)


def _from_col_shards(g):
    t = jnp.moveaxis(g, 0, -2)
    return t.reshape(t.shape[:-2] + (t.shape[-2] * t.shape[-1],))


def kernel(x, norm_mix, norm_mlp, norm_final, rel_bias, gdn_w_in, gdn_conv_w, gdn_a_log, gdn_dt_bias, gdn_norm_w, gdn_w_out, dswa_w_in, dswa_w_out, mlp_w1, mlp_w2, loss_target, m_norm_mix, m_norm_mlp, m_norm_final, m_rel_bias, m_gdn_w_in, m_gdn_conv_w, m_gdn_a_log, m_gdn_dt_bias, m_gdn_norm_w, m_gdn_w_out, m_dswa_w_in, m_dswa_w_out, m_mlp_w1, m_mlp_w2, v_norm_mix, v_norm_mlp, v_norm_final, v_rel_bias, v_gdn_w_in, v_gdn_conv_w, v_gdn_a_log, v_gdn_dt_bias, v_gdn_norm_w, v_gdn_w_out, v_dswa_w_in, v_dswa_w_out, v_mlp_w1, v_mlp_w2):
    params = dict(norm_mix=norm_mix, norm_mlp=norm_mlp, norm_final=norm_final, rel_bias=rel_bias,
                  gdn_w_in=gdn_w_in, gdn_conv_w=gdn_conv_w, gdn_a_log=gdn_a_log, gdn_dt_bias=gdn_dt_bias,
                  gdn_norm_w=gdn_norm_w, gdn_w_out=gdn_w_out, dswa_w_in=dswa_w_in, dswa_w_out=dswa_w_out,
                  mlp_w1=mlp_w1, mlp_w2=mlp_w2)
    mom_m = dict(norm_mix=m_norm_mix, norm_mlp=m_norm_mlp, norm_final=m_norm_final, rel_bias=m_rel_bias,
                 gdn_w_in=m_gdn_w_in, gdn_conv_w=m_gdn_conv_w, gdn_a_log=m_gdn_a_log, gdn_dt_bias=m_gdn_dt_bias,
                 gdn_norm_w=m_gdn_norm_w, gdn_w_out=m_gdn_w_out, dswa_w_in=m_dswa_w_in, dswa_w_out=m_dswa_w_out,
                 mlp_w1=m_mlp_w1, mlp_w2=m_mlp_w2)
    mom_v = dict(norm_mix=v_norm_mix, norm_mlp=v_norm_mlp, norm_final=v_norm_final, rel_bias=v_rel_bias,
                 gdn_w_in=v_gdn_w_in, gdn_conv_w=v_gdn_conv_w, gdn_a_log=v_gdn_a_log, gdn_dt_bias=v_gdn_dt_bias,
                 gdn_norm_w=v_gdn_norm_w, gdn_w_out=v_gdn_w_out, dswa_w_in=v_dswa_w_in, dswa_w_out=v_dswa_w_out,
                 mlp_w1=v_mlp_w1, mlp_w2=v_mlp_w2)
    xs = x[0]
    target = loss_target[0]
    dist = _Dist(params)
    conv_tail, _ = _pack_rows([gdn_conv_w], 8)
    (conv_g,) = dist.put("start", dist.gather_comm("start", extra=[(conv_tail, "stack")]).run(name="ag_start"))
    conv_parts = [_unpack_rows(conv_g[dev], [gdn_conv_w.shape])[0] for dev in range(N_DEV)]
    conv_full = _from_col_shards(jnp.stack(conv_parts))[:, :, 0, :]

    loss_part, dcur, g_big, rep, g_conv = _local_step(
        xs, target, dict(norm_mix=norm_mix, norm_mlp=norm_mlp, norm_final=norm_final, rel_bias=rel_bias,
                         gdn_a_log=gdn_a_log, gdn_dt_bias=gdn_dt_bias, gdn_norm_w=gdn_norm_w), dist.full, conv_full, dist)
    loss = lax.psum(loss_part[0, 0], ("x", "y", "c"))
    grad_x = dcur[None]

    conv_dev = _col_shards(jnp.stack(g_conv)[:, :, None, :], N_DEV)
    small_send = jnp.stack([_pack_rows([conv_dev[dev]] + [rep[n] for n in _SMALL[1:]], 8)[0] for dev in range(N_DEV)])
    (small_recv,) = dist.got("end", dist.send_comm("end", g_big, extra=[small_send]).run(name="grad_exchange"))

    outs = {}
    for n in _BIG:
        recvs = [dist.recv[(n, l)] for l in range(params[n].shape[0])]
        res = _adamw_layers(recvs, params[n], mom_m[n], mom_v[n], name=f"adamw_{n}")
        for tag, t in zip(("grad", "delta", "new_m", "new_v"), res):
            outs[(tag, n)] = t
    w_slab, _ = _pack_rows([params[n] for n in _SMALL], 8)
    m_slab, _ = _pack_rows([mom_m[n] for n in _SMALL], 8)
    v_slab, _ = _pack_rows([mom_v[n] for n in _SMALL], 8)
    small = _adamw_reduce(small_recv, w_slab, m_slab, v_slab, name="adamw_small")
    shapes = [params[n].shape for n in _SMALL]
    for tag, slab in zip(("grad", "delta", "new_m", "new_v"), small):
        for n, t in zip(_SMALL, _unpack_rows(slab, shapes)):
            outs[(tag, n)] = t
    result = [loss, grad_x]
    for tag in ("grad", "delta", "new_m", "new_v"):
        result += [outs[(tag, n)] for n in _ORDER]
    return tuple(result)


_GATHER = {
    "start": (("gdn_w_in", 0),),
    "gdn_proj0": (("gdn_w_out", 0),),
    "chunk_fwd0": (("mlp_w1", 0), ("mlp_w2", 0), ("dswa_w_in", 0), ("dswa_w_out", 0)),
    "mlp_up0": (("mlp_w2", 1),),
    "mlp_down0": (("gdn_w_in", 1),),
    "attn_fwd1": (("mlp_w1", 1), ("mlp_w1", 2)),
    "mlp_up1": (("mlp_w2", 2),),
    "mlp_down1": (("gdn_w_out", 1),),
    "chunk_fwd2": (("dswa_w_in", 1), ("dswa_w_out", 1), ("mlp_w1", 3), ("mlp_w2", 3)),
}
_SEND = {
    "attn_bwd3": (("mlp_w1", 3), ("mlp_w2", 3)),
    "chunk_bwd2": (("dswa_w_in", 1), ("dswa_w_out", 1), ("mlp_w2", 2)),
    "pre_bwd2": (("mlp_w1", 2), ("gdn_w_out", 1)),
    "gdn_proj_bwd2": (("gdn_w_in", 1),),
    "attn_bwd1": (("mlp_w1", 1), ("mlp_w2", 1)),
    "chunk_bwd0": (("dswa_w_in", 0), ("dswa_w_out", 0), ("mlp_w2", 0)),
    "pre_bwd0": (("mlp_w1", 0), ("gdn_w_out", 0)),
    "gdn_proj_bwd0": (("gdn_w_in", 0),),
    "end": (),
}


class _Dist:
    def __init__(self, params):
        self.shards = {n: params[n].astype(BF16) for n in _BIG}
        self.full = {n: [None] * params[n].shape[0] for n in _BIG}
        self.recv = {}

    def gather_comm(self, tag, extra=()):
        if tag not in _GATHER:
            return None
        arrays = [self.shards[n][l] for n, l in _GATHER[tag]] + [a for a, _ in extra]
        return _Comm("gather", arrays, [_KIND[n] for n, _ in _GATHER[tag]] + [k for _, k in extra])

    def put(self, tag, outs):
        for (n, l), t in zip(_GATHER.get(tag, ()), outs):
            self.full[n][l] = _from_col_shards(t) if _KIND[n] == "stack" else t
        return outs[len(_GATHER.get(tag, ())):]

    def send_comm(self, tag, g_big, extra=()):
        if tag not in _SEND:
            return None
        arrays = [_col_shards(g_big[n][l], N_DEV) if _KIND[n] == "stack" else g_big[n][l] for n, l in _SEND[tag]]
        return _Comm("exchange", arrays + list(extra))

    def got(self, tag, outs):
        for item, t in zip(_SEND.get(tag, ()), outs):
            self.recv[item] = t
        return outs[len(_SEND.get(tag, ())):]


def _mm_gather(dist, tag, *args, **kw):
    comm = dist and dist.gather_comm(tag)
    if not comm:
        return _mm(*args, **kw)
    res, got = _mm(*args, comm=comm, **kw)
    dist.put(tag, got)
    return res


def _ep_residual_norm(acc, res, g):
    x = acc + res
    r = lax.rsqrt(jnp.mean(x * x, axis=-1, keepdims=True) + RMS_EPS)
    return x, x * r * g


def _ep_rms_bwd(dh, x, dres, g):
    r = lax.rsqrt(jnp.mean(x * x, axis=-1, keepdims=True) + RMS_EPS)
    xn = x * r
    dn = dh * g
    dx = dres + r * (dn - xn * jnp.mean(dn * xn, axis=-1, keepdims=True))
    return dx, dx, jnp.sum(dh * xn, axis=0, keepdims=True)


def _local_step(xs, target, sp, full, conv_full, dist=None):
    s = xs.shape[0]
    norm_mix, norm_mlp, norm_final = sp["norm_mix"], sp["norm_mlp"], sp["norm_final"]
    gdn_a_log, gdn_dt_bias, gdn_norm_w = sp["gdn_a_log"], sp["gdn_dt_bias"], sp["gdn_norm_w"]
    onehot = _bucket_onehot()
    table_t = sp["rel_bias"].T
    bias = _dswa_bias(table_t, onehot, name="dswa_bias").reshape(DSWA_HEADS, DSWA_HALF, 3 * DSWA_HALF)

    saved = []
    cur = xs
    row = lambda v: v.reshape(1, -1)
    h = _rms_fwd(cur, norm_mix[0], name="rms_mix_fwd0")
    for i in range(DEPTH):
        j = i // 2
        sv = dict(x_in=cur, h=h)
        if i % 2 == 0:
            w_in = full["gdn_w_in"][j]
            proj = _mm_gather(dist, f"gdn_proj{i}", h, w_in, b_cols=(0, GDN_MAIN), name=f"gdn_proj{i}")
            ab = _mm(h, w_in[:, GDN_MAIN:], name=f"gdn_proj_ab{i}")
            qkvn = _gdn_pre_fwd(proj, conv_full[j], name=f"gdn_pre_fwd{i}")
            g_all, beta_all = _gdn_gate_fwd(ab[:, :2 * GDN_HEADS], ab[:, 2 * GDN_HEADS:], gdn_a_log[j], gdn_dt_bias[j],
                                            name=f"gdn_gate_fwd{i}")
            gshape = (2, GDN_HEADS, s // GDN_CHUNK, 1, GDN_CHUNK)
            g_row = g_all.T.reshape(gshape)
            b_row = beta_all.T.reshape(gshape)
            o, states, got = _gdn_chunk_fwd(qkvn, g_row, b_row, name=f"gdn_chunk_fwd{i}",
                                            comm=dist and dist.gather_comm(f"chunk_fwd{i}"))
            if dist:
                dist.put(f"chunk_fwd{i}", got)
            act = _gdn_post_fwd(o, proj, gdn_norm_w[j], name=f"gdn_post_fwd{i}")
            sv.update(proj=proj, ab=ab, qkvn=qkvn, g_row=g_row, b_row=b_row, o=o, states=states, act=act)
            w_out = full["gdn_w_out"][j]
        else:
            w_in = full["dswa_w_in"][j]
            qkv = _mm(h, w_in, name=f"dswa_proj{i}")
            o_n, lse_n, got = _dswa_attn_fwd(qkv, bias, name=f"dswa_attn_fwd{i}",
                                             comm=dist and dist.gather_comm(f"attn_fwd{i}"))
            if dist:
                dist.put(f"attn_fwd{i}", got)
            act = _dswa_combine_fwd(o_n, lse_n, name=f"dswa_comb_fwd{i}")
            sv.update(qkv=qkv, o_n=o_n, lse_n=lse_n, act=act)
            w_out = full["dswa_w_out"][j]
        cur, h2 = _mm(act, w_out, name=f"mix_out{i}", out_dtypes=(F32, BF16), epilogue=_ep_residual_norm,
                      extras=(cur,), vecs=(row(norm_mlp[i]),))
        sv["x_mid"] = cur
        u, a = _mm_gather(dist, f"mlp_up{i}", h2, full["mlp_w1"][i], name=f"mlp_up{i}", out_dtypes=(BF16, BF16),
                          epilogue=lambda acc: (acc, jnp.square(jnp.maximum(acc, 0.0))))
        if i + 1 < DEPTH:
            cur, h = _mm_gather(dist, f"mlp_down{i}", a, full["mlp_w2"][i], name=f"mlp_down{i}", out_dtypes=(F32, BF16), tk=2048,
                                epilogue=_ep_residual_norm, extras=(cur,), vecs=(row(norm_mix[i + 1]),))
        else:
            cur = _mm_gather(dist, f"mlp_down{i}", a, full["mlp_w2"][i], name=f"mlp_down{i}", tk=2048,
                             epilogue=lambda acc, r: (acc + r,), extras=(cur,))
        sv.update(h2=h2, u=u, a=a)
        saved.append(sv)

    loss_part, dcur, dcur_b, dg_final = _loss_head(cur, norm_final, target, name="loss_head")

    g_norm_mix, g_norm_mlp = [None] * DEPTH, [None] * DEPTH
    g_big = {n: [None] * len(full[n]) for n in _BIG}
    g_conv, g_alog, g_dt, g_nw = [None] * 2, [None] * 2, [None] * 2, [None] * 2
    d_table_t = jnp.zeros((DSWA_HEADS, REL_BUCKETS), F32)
    for i in reversed(range(DEPTH)):
        j = i // 2
        sv = saved[i]
        w1, w2 = full["mlp_w1"][i], full["mlp_w2"][i]
        du = _mm(dcur_b, w2, tb=True, name=f"mlp_down_bwd{i}", out_dtypes=(BF16,),
                 epilogue=lambda acc, uu: (acc * (2.0 * jnp.maximum(uu.astype(F32), 0.0)),), extras=(sv["u"],))
        g_big["mlp_w2"][i] = _mm(sv["a"], dcur_b, ta=True, name=f"mlp_w2_grad{i}", out_dtypes=(BF16,), shard="rows")
        g_big["mlp_w1"][i] = _mm(sv["h2"], du, ta=True, name=f"mlp_w1_grad{i}", out_dtypes=(BF16,), shard="cols")
        dmid, dmid_b, g_norm_mlp[i] = _mm(du, w1, tb=True, name=f"mlp_up_bwd{i}", out_dtypes=(F32, BF16),
                                          epilogue=_ep_rms_bwd, extras=(sv["x_mid"], dcur), vecs=(row(norm_mlp[i]),),
                                          vec_out=True)
        if i % 2 == 0:
            w_in, w_out = full["gdn_w_in"][j], full["gdn_w_out"][j]
            dact = _mm(dmid_b, w_out, tb=True, name=f"mix_out_bwd{i}")
            g_big["gdn_w_out"][j] = _mm(sv["act"], dmid_b, ta=True, name=f"mix_out_grad{i}", out_dtypes=(BF16,),
                                        shard="rows")
            do, dz, g_nw[j] = _gdn_post_bwd(sv["o"], sv["proj"], gdn_norm_w[j], dact, name=f"gdn_post_bwd{i}")
            dqkvn, dg_row, db_row, got = _gdn_chunk_bwd(sv["qkvn"], sv["g_row"], sv["b_row"], sv["states"], do,
                                                        name=f"gdn_chunk_bwd{i}",
                                                        comm=dist and dist.send_comm(f"chunk_bwd{i}", g_big))
            if dist:
                dist.got(f"chunk_bwd{i}", got)
            dproj, g_conv[j], got = _gdn_pre_bwd(sv["proj"], conv_full[j], dqkvn, dz, name=f"gdn_pre_bwd{i}",
                                                 comm=dist and dist.send_comm(f"pre_bwd{i}", g_big))
            if dist:
                dist.got(f"pre_bwd{i}", got)
            nh2 = 2 * GDN_HEADS
            da_, db_, g_alog[j], g_dt[j] = _gdn_gate_bwd(sv["ab"][:, :nh2], sv["ab"][:, nh2:], gdn_a_log[j], gdn_dt_bias[j],
                                                         dg_row.reshape(nh2, s).T, db_row.reshape(nh2, s).T,
                                                         name=f"gdn_gate_bwd{i}")
            dab = jnp.concatenate([da_, db_], axis=1)
            gw_main = _mm(sv["h"], dproj, ta=True, name=f"gdn_w_in_grad{i}", out_dtypes=(BF16,))
            gw_ab = _mm(sv["h"], dab, ta=True, name=f"gdn_w_ab_grad{i}", out_dtypes=(BF16,))
            g_big["gdn_w_in"][j] = jnp.concatenate([gw_main, gw_ab], axis=1)
            dh_ab = _mm(dab, w_in[:, GDN_MAIN:], tb=True, name=f"gdn_proj_ab_bwd{i}")
            comm = dist and dist.send_comm(f"gdn_proj_bwd{i}", g_big)
            res = _mm(dproj, w_in, b_cols=(0, GDN_MAIN), tb=True, name=f"gdn_proj_bwd{i}", out_dtypes=(F32, BF16), tm=512, tk=2048,
                      epilogue=lambda acc, r, x, dres, g: _ep_rms_bwd(acc + r, x, dres, g),
                      extras=(dh_ab, sv["x_in"], dmid), vecs=(row(norm_mix[i]),), vec_out=True, comm=comm)
            if comm:
                res, got = res
                dist.got(f"gdn_proj_bwd{i}", got)
            dcur, dcur_b, g_norm_mix[i] = res
        else:
            w_in, w_out = full["dswa_w_in"][j], full["dswa_w_out"][j]
            dact = _mm(dmid_b, w_out, tb=True, name=f"mix_out_bwd{i}")
            g_big["dswa_w_out"][j] = _mm(sv["act"], dmid_b, ta=True, name=f"mix_out_grad{i}", out_dtypes=(BF16,),
                                         shard="rows")
            do_n, corr_n = _dswa_combine_bwd(sv["o_n"], sv["lse_n"], dact, name=f"dswa_comb_bwd{i}")
            *dqkv, dbias, got = _dswa_attn_bwd(sv["qkv"], bias, sv["lse_n"], do_n, corr_n, name=f"dswa_attn_bwd{i}",
                                               comm=dist and dist.send_comm(f"attn_bwd{i}", g_big))
            if dist:
                dist.got(f"attn_bwd{i}", got)
            d_table_t = d_table_t + _dswa_dtable(dbias.reshape(DSWA_HEADS, -1), onehot, name=f"dswa_dtable{i}")
            g_big["dswa_w_in"][j] = jnp.concatenate(
                [_mm(sv["h"], dt, ta=True, name=f"dswa_w_in_grad{i}_{t}", out_dtypes=(BF16,)) for t, dt in enumerate(dqkv)],
                axis=1)
            cols = [(t * DSWA_WIDTH, DSWA_WIDTH) for t in range(3)]
            dh = _mm(dqkv[0], w_in, b_cols=cols[0], tb=True, name=f"dswa_proj_bwd{i}_0")
            dh = _mm(dqkv[1], w_in, b_cols=cols[1], tb=True, name=f"dswa_proj_bwd{i}_1",
                     epilogue=lambda acc, r: (acc + r,), extras=(dh,))
            dcur, dcur_b, g_norm_mix[i] = _mm(
                dqkv[2], w_in, b_cols=cols[2], tb=True, name=f"dswa_proj_bwd{i}_2", out_dtypes=(F32, BF16), tm=512,
                epilogue=lambda acc, r, x, dres, g: _ep_rms_bwd(acc + r, x, dres, g),
                extras=(dh, sv["x_in"], dmid), vecs=(row(norm_mix[i]),), vec_out=True)

    rep = dict(norm_mix=jnp.concatenate(g_norm_mix, axis=0), norm_mlp=jnp.concatenate(g_norm_mlp, axis=0),
               norm_final=dg_final.reshape(-1), rel_bias=d_table_t.T,
               gdn_a_log=jnp.stack(g_alog).reshape(gdn_a_log.shape), gdn_dt_bias=jnp.stack(g_dt).reshape(gdn_dt_bias.shape),
               gdn_norm_w=jnp.stack(g_nw).reshape(gdn_norm_w.shape))
    return loss_part, dcur, g_big, rep, g_conv
```

```python
import functools
import math

import jax
import jax.numpy as jnp
import numpy as np
from jax import lax
from jax.experimental import pallas as pl
from jax.experimental.pallas import tpu as pltpu

F32 = jnp.float32
BF16 = jnp.bfloat16
HP = lax.Precision.HIGHEST

N_DEV = 8
D_MODEL = 1024
DEPTH = 4
RMS_EPS = 1e-6
NEG_INF = -1e30

GDN_HEADS = 8
GDN_DK = 128
GDN_CONV = 5
GDN_CHUNK = 128
GDN_QKV = 3 * GDN_HEADS * GDN_DK
GDN_MAIN = GDN_QKV + GDN_HEADS * GDN_DK
GDN_AB = 4 * GDN_HEADS

DSWA_DILS = (1, 4, 16)
DSWA_HG = 6
DSWA_E = 64
DSWA_HEADS = 18
DSWA_WIDTH = DSWA_HEADS * DSWA_E
DSWA_HALF = 64
DSWA_PG = DSWA_HG // 2
DSWA_UNROLL = 8
REL_BUCKETS = 32
REL_MAX_DIST = 1024

ADAM_LR = 0.001
ADAM_B1 = 0.9
ADAM_B2 = 0.999
ADAM_EPS = 1e-08
ADAM_WD = 0.01
ADAM_STEP = 10

VMEM_LIMIT = 56 * 1024 * 1024


def _cparams(sem=None, **kw):
    return pltpu.CompilerParams(dimension_semantics=sem, vmem_limit_bytes=VMEM_LIMIT, **kw)


def _pick(dim, cands):
    for c in cands:
        if dim % c == 0:
            return c
    return dim


def _bdot(a, b):
    return jnp.dot(a.astype(BF16), b.astype(BF16), preferred_element_type=F32)


def _bdot_nt(a, b):
    return lax.dot_general(a.astype(BF16), b.astype(BF16), (((1,), (1,)), ((), ())),
                           preferred_element_type=F32)


def _bdot_tn(a, b):
    return lax.dot_general(a.astype(BF16), b.astype(BF16), (((0,), (0,)), ((), ())),
                           preferred_element_type=F32)


def _hdot(a, b):
    return jnp.dot(a, b, precision=HP, preferred_element_type=F32)


def _hdot_nt(a, b):
    return lax.dot_general(a, b, (((1,), (1,)), ((), ())), precision=HP, preferred_element_type=F32)


def _sigmoid(x):
    return 1.0 / (1.0 + jnp.exp(-x))


def _mm(a, b, *, name, ta=False, tb=False, out_dtypes=(F32,), epilogue=None, extras=(), vecs=(), vec_out=False,
        tm=None, tn=None, tk=None, shard=None, comm=None, b_cols=None):
    if ta:
        kdim, m = a.shape
    else:
        m, kdim = a.shape
    b0, bsz = b_cols or (0, b.shape[1])
    n = b.shape[0] if tb else bsz
    assert not tb or kdim == bsz
    if shard == "rows":
        tm = m // N_DEV if (m // N_DEV) % 128 == 0 else m
    if shard == "cols":
        tn = n // N_DEV
    tm = tm or _pick(m, (1024, 1152, 512, 384, 256, 128))
    tn = tn or _pick(n, (1024, 1152, 512, 384, 256, 128))
    tk = tk or _pick(kdim, ((2048,) if ta else ()) + (1024, 1152, 512, 384, 256, 128))
    nk = kdim // tk
    n_out = len(out_dtypes) + (1 if vec_out else 0)
    n_ex = len(extras) + len(vecs)
    rows_all = shard == "rows" and tm == m

    gi, gj = m // tm, n // tn

    def body(*refs):
        i, j, k = pl.program_id(0), pl.program_id(1), pl.program_id(2)
        inner = (j == 0) & (k == 0)
        ins, out_refs, (acc_ref,) = _comm_hooks(
            comm, refs, 2 + n_ex, n_out, 1, (i == 0) & inner, (i == (3 * gi) // 4) & inner,
            (i == gi - 1) & (j == gj - 1) & (k == nk - 1))
        a_ref, b_ref, ex_refs = ins[0], ins[1], ins[2:]
        if vec_out:
            out_refs, vec_ref = out_refs[:-1], out_refs[-1]

        @pl.when(k == 0)
        def _():
            acc_ref[...] = jnp.zeros_like(acc_ref)

        av = a_ref[...].astype(BF16)
        bv = b_ref[...].astype(BF16)
        dims = (((0 if ta else 1,), (1 if tb else 0,)), ((), ()))
        acc_ref[...] += lax.dot_general(av, bv, dims, preferred_element_type=F32)

        @pl.when(k == nk - 1)
        def _():
            acc = acc_ref[...]
            outs = (acc,) if epilogue is None else epilogue(acc, *[r[...] for r in ex_refs])
            if vec_out:
                part = outs[-1]

                @pl.when(i == 0)
                def _():
                    vec_ref[...] = part

                @pl.when(i > 0)
                def _():
                    vec_ref[...] += part
            for r, o in zip(out_refs, outs):
                if rows_all:
                    for p in range(N_DEV):
                        r[p] = o[p * (m // N_DEV):(p + 1) * (m // N_DEV)].astype(r.dtype)
                else:
                    r[...] = o.astype(r.dtype)

    a_spec = pl.BlockSpec((tk, tm), lambda i, j, k: (k, i)) if ta else pl.BlockSpec((tm, tk), lambda i, j, k: (i, k))
    assert b0 % (tk if tb else tn) == 0
    boff = b0 // (tk if tb else tn)
    b_spec = (pl.BlockSpec((tn, tk), lambda i, j, k: (j, k + boff)) if tb
              else pl.BlockSpec((tk, tn), lambda i, j, k: (k, j + boff)))
    o_spec = pl.BlockSpec((tm, tn), lambda i, j, k: (i, j))
    v_spec = pl.BlockSpec((1, tn), lambda i, j, k: (0, j))
    out_specs = [o_spec] * len(out_dtypes) + ([v_spec] if vec_out else [])
    out_shape = [jax.ShapeDtypeStruct((m, n), dt) for dt in out_dtypes]
    out_shape += [jax.ShapeDtypeStruct((1, n), F32)] if vec_out else []
    if shard == "rows":
        out_shape = [jax.ShapeDtypeStruct((N_DEV, m // N_DEV, n), out_dtypes[0])]
        out_specs = [pl.BlockSpec((N_DEV, m // N_DEV, tn), lambda i, j, k: (0, 0, j)) if rows_all
                     else pl.BlockSpec((None, tm, tn), lambda i, j, k: (i, 0, j))]
    if shard == "cols":
        out_shape = [jax.ShapeDtypeStruct((N_DEV, m, tn), out_dtypes[0])]
        out_specs = [pl.BlockSpec((None, tm, tn), lambda i, j, k: (j, i, 0))]
    c_in, c_out, c_shape, c_scr = _comm_specs(comm)
    outs = pl.pallas_call(
        body, name=name,
        grid=(gi, gj, nk),
        in_specs=[a_spec, b_spec] + [o_spec] * len(extras) + [v_spec] * len(vecs) + c_in,
        out_specs=out_specs + c_out,
        out_shape=out_shape + c_shape,
        scratch_shapes=[pltpu.VMEM((tm, tn), F32)] + c_scr,
        compiler_params=_cparams(("arbitrary",) * 3 if comm or vec_out else ("parallel", "parallel", "arbitrary")),
    )(a, b, *extras, *vecs, *(comm.arrays if comm else []))
    res = outs[0] if n_out == 1 else tuple(outs[:n_out])
    return (res, outs[n_out:]) if comm else res


def _rms_fwd(x, g, *, name):
    s, d = x.shape
    tr = _pick(s, (512, 256, 128))

    def body(x_ref, g_ref, h_ref):
        xv = x_ref[...]
        r = lax.rsqrt(jnp.mean(xv * xv, axis=-1, keepdims=True) + RMS_EPS)
        h_ref[...] = (xv * r * g_ref[...]).astype(h_ref.dtype)

    return pl.pallas_call(
        body, name=name, grid=(s // tr,),
        in_specs=[pl.BlockSpec((tr, d), lambda i: (i, 0)), pl.BlockSpec((1, d), lambda i: (0, 0))],
        out_specs=pl.BlockSpec((tr, d), lambda i: (i, 0)),
        out_shape=jax.ShapeDtypeStruct((s, d), BF16),
        compiler_params=_cparams(("parallel",)),
    )(x, g.reshape(1, d))


def _loss_head(x, g, target, *, name):
    s, d = x.shape
    tr = _pick(s, (512, 256, 128))

    def body(x_ref, g_ref, t_ref, loss_ref, dx_ref, dxb_ref, dg_ref):
        i = pl.program_id(0)
        xv = x_ref[...]
        gv = g_ref[...]
        r = lax.rsqrt(jnp.mean(xv * xv, axis=-1, keepdims=True) + RMS_EPS)
        xn = xv * r
        err = xn * gv - t_ref[...]
        lpart = 0.5 * jnp.sum(jnp.mean(err * err, axis=-1, keepdims=True), axis=0, keepdims=True)
        dy = err * (1.0 / d)
        dn = dy * gv
        dx = r * (dn - xn * jnp.mean(dn * xn, axis=-1, keepdims=True))
        dx_ref[...] = dx
        dxb_ref[...] = dx.astype(dxb_ref.dtype)
        gpart = jnp.sum(dy * xn, axis=0, keepdims=True)

        @pl.when(i == 0)
        def _():
            dg_ref[...] = gpart
            loss_ref[...] = lpart

        @pl.when(i > 0)
        def _():
            dg_ref[...] += gpart
            loss_ref[...] += lpart

    row = pl.BlockSpec((tr, d), lambda i: (i, 0))
    vec = pl.BlockSpec((1, d), lambda i: (0, 0))
    one = pl.BlockSpec((1, 1), lambda i: (0, 0))
    return pl.pallas_call(
        body, name=name, grid=(s // tr,),
        in_specs=[row, vec, row], out_specs=[one, row, row, vec],
        out_shape=[jax.ShapeDtypeStruct((1, 1), F32), jax.ShapeDtypeStruct((s, d), F32),
                   jax.ShapeDtypeStruct((s, d), BF16), jax.ShapeDtypeStruct((1, d), F32)],
        compiler_params=_cparams(("arbitrary",)),
    )(x, g.reshape(1, d), target)


def _shift_rows(x, sft, rows):
    s = x.shape[0]
    if sft == 0:
        return x
    y = pltpu.roll(x, (-sft) % s, 0)
    edge = slice(0, 8) if sft < 0 else slice(s - 8, s)
    ok = (rows[edge] + sft >= 0) & (rows[edge] + sft < s)
    fixed = jnp.where(ok, y[edge], 0.0)
    return jnp.concatenate([fixed, y[8:]] if sft < 0 else [y[:s - 8], fixed], axis=0)


def _gdn_pre_fwd(proj, conv_w, *, name):
    s = proj.shape[0]
    nblk = GDN_QKV // 128
    pad = GDN_CONV // 2

    def body(x_ref, w_ref, o_ref):
        j = pl.program_id(0)
        x = x_ref[...]
        rows = lax.broadcasted_iota(jnp.int32, x.shape, 0)
        c = jnp.zeros_like(x)
        for t in range(GDN_CONV):
            c = c + w_ref[pl.ds(t, 1), :] * _shift_rows(x, t - pad, rows)
        a = c * _sigmoid(c)
        rinv = lax.rsqrt(jnp.sum(a * a, axis=-1, keepdims=True) + 1e-6)
        scale = jnp.where(j < GDN_HEADS, GDN_DK ** -0.5, 1.0)
        o_ref[...] = jnp.where(j >= 2 * GDN_HEADS, a, a * (rinv * scale))

    return pl.pallas_call(
        body, name=name, grid=(nblk,),
        in_specs=[pl.BlockSpec((s, 128), lambda j: (0, j)), pl.BlockSpec((GDN_CONV, 128), lambda j: (0, j))],
        out_specs=pl.BlockSpec((s, 128), lambda j: (0, j)),
        out_shape=jax.ShapeDtypeStruct((s, GDN_QKV), F32),
        compiler_params=_cparams(("parallel",)),
    )(proj, conv_w)


def _gdn_pre_bwd(proj, conv_w, dqkv, dproj, *, name, comm=None):
    s = proj.shape[0]
    nblk = GDN_QKV // 128
    pad = GDN_CONV // 2

    def body(*refs):
        j = pl.program_id(0)
        (x_ref, w_ref, df_ref, dbk_ref, _), (dx_ref, dw_ref), _ = _comm_hooks(
            comm, refs, 5, 2, 0, j == 0, j == nblk // 2, j == nblk - 1)
        x = x_ref[...]
        rows = lax.broadcasted_iota(jnp.int32, x.shape, 0)
        xs = [_shift_rows(x, t - pad, rows) for t in range(GDN_CONV)]
        c = jnp.zeros_like(x)
        for t in range(GDN_CONV):
            c = c + w_ref[pl.ds(t, 1), :] * xs[t]
        sg = _sigmoid(c)
        a = c * sg
        rinv = lax.rsqrt(jnp.sum(a * a, axis=-1, keepdims=True) + 1e-6)
        scale = jnp.where(j < GDN_HEADS, GDN_DK ** -0.5, 1.0)
        dy = df_ref[...] + dbk_ref[...]
        nh = a * rinv
        da_n = (rinv * scale) * (dy - nh * jnp.sum(dy * nh, axis=-1, keepdims=True))
        da = jnp.where(j >= 2 * GDN_HEADS, dy, da_n)
        dc = da * (sg * (1.0 + c * (1.0 - sg)))
        dx = jnp.zeros_like(x)
        for t in range(GDN_CONV):
            dx = dx + w_ref[pl.ds(t, 1), :] * _shift_rows(dc, pad - t, rows)
            dw_ref[pl.ds(t, 1), :] = jnp.sum(dc * xs[t], axis=0, keepdims=True)
        dx_ref[...] = dx.astype(dx_ref.dtype)

    col = pl.BlockSpec((s, 128), lambda j: (0, j))
    wsp = pl.BlockSpec((GDN_CONV, 128), lambda j: (0, j))
    c_in, c_out, c_shape, c_scr = _comm_specs(comm)
    res = pl.pallas_call(
        body, name=name, grid=(nblk,),
        in_specs=[col, wsp, col, col, pl.BlockSpec(memory_space=pl.ANY)] + c_in, out_specs=[col, wsp] + c_out,
        out_shape=[jax.ShapeDtypeStruct(dproj.shape, BF16), jax.ShapeDtypeStruct((GDN_CONV, GDN_QKV), F32)] + c_shape,
        input_output_aliases={4: 0},
        scratch_shapes=c_scr,
        compiler_params=_cparams(("arbitrary",) if comm else ("parallel",)),
    )(proj, conv_w, dqkv[0], dqkv[1], dproj, *(comm.arrays if comm else []))
    return res[0], res[1], res[2:]


def _softplus(x):
    return jnp.maximum(x, 0.0) + jnp.log(1.0 + jnp.exp(-jnp.abs(x)))


def _gdn_gate_fwd(a, b, a_log, dt_bias, *, name):
    s = a.shape[0]
    nh = 2 * GDN_HEADS

    def body(a_ref, b_ref, al_ref, dt_ref, g_ref, be_ref):
        g_ref[...] = -jnp.exp(al_ref[...]) * _softplus(a_ref[...] + dt_ref[...])
        be_ref[...] = _sigmoid(b_ref[...])

    return pl.pallas_call(
        body, name=name,
        out_shape=[jax.ShapeDtypeStruct((s, nh), F32), jax.ShapeDtypeStruct((s, nh), F32)],
        compiler_params=_cparams(),
    )(a, b, a_log.reshape(1, nh), dt_bias.reshape(1, nh))


def _gdn_gate_bwd(a, b, a_log, dt_bias, dg, dbeta, *, name):
    s = a.shape[0]
    nh = 2 * GDN_HEADS

    def body(a_ref, b_ref, al_ref, dt_ref, dg_ref, db_ref, da_ref, dbb_ref, dal_ref, ddt_ref):
        ea = jnp.exp(al_ref[...])
        z = a_ref[...] + dt_ref[...]
        dgv = dg_ref[...]
        dz = dgv * (-ea) * _sigmoid(z)
        dal_ref[...] = jnp.sum(dgv * (-ea) * _softplus(z), axis=0, keepdims=True)
        ddt_ref[...] = jnp.sum(dz, axis=0, keepdims=True)
        sb = _sigmoid(b_ref[...])
        da_ref[...] = dz
        dbb_ref[...] = db_ref[...] * sb * (1.0 - sb)

    return pl.pallas_call(
        body, name=name,
        out_shape=[jax.ShapeDtypeStruct((s, nh), F32), jax.ShapeDtypeStruct((s, nh), F32),
                   jax.ShapeDtypeStruct((1, nh), F32), jax.ShapeDtypeStruct((1, nh), F32)],
        compiler_params=_cparams(),
    )(a, b, a_log.reshape(1, nh), dt_bias.reshape(1, nh), dg, dbeta)


def _chunk_masks(d):
    c = GDN_CHUNK
    ii = lax.broadcasted_iota(jnp.int32, (c, c), 0)
    jj = lax.broadcasted_iota(jnp.int32, (c, c), 1)
    dif = (ii - jj) * (1 - 2 * d)
    mi = dif >= 0
    mit = dif <= 0
    ms = dif > 0
    eye = ii == jj
    bds = [(ii >> sh) == (jj >> sh) for sh in range(3, c.bit_length() - 1)]
    return dict(mi=mi, mit=mit, ms=ms, eye=eye, bds=bds,
                mif=mi.astype(F32), mitf=mit.astype(F32), eyef=eye.astype(F32))


class _V:
    def __init__(self, xs):
        self.xs = tuple(xs)

    def __add__(self, o):
        return _lift(lambda a, b: a + b)(self, o)

    def __radd__(self, o):
        return _lift(lambda a, b: b + a)(self, o)

    def __sub__(self, o):
        return _lift(lambda a, b: a - b)(self, o)

    def __rsub__(self, o):
        return _lift(lambda a, b: b - a)(self, o)

    def __mul__(self, o):
        return _lift(lambda a, b: a * b)(self, o)

    def __rmul__(self, o):
        return _lift(lambda a, b: b * a)(self, o)

    def __and__(self, o):
        return _lift(lambda a, b: a & b)(self, o)

    def __neg__(self):
        return _lift(lambda a: -a)(self)

    def __rtruediv__(self, o):
        return _lift(lambda a, b: b / a)(self, o)


def _lift(f):
    def g(*args, **kw):
        n = next(len(a.xs) for a in args if isinstance(a, _V))
        return _V(f(*[a.xs[i] if isinstance(a, _V) else a for a in args], **kw) for i in range(n))
    return g


_vwhere, _vsum, _vexp, _vnot = _lift(jnp.where), _lift(jnp.sum), _lift(jnp.exp), _lift(jnp.logical_not)
_vbdot, _vbdot_nt, _vbdot_tn = _lift(_bdot), _lift(_bdot_nt), _lift(_bdot_tn)
_vcat = _lift(lambda a, b: jnp.concatenate([a, b], axis=1))
_vlo = _lift(lambda a, n: a[:, :n])
_vhi = _lift(lambda a, n: a[:, n:])


def _both_masks(n):
    m = [_chunk_masks(d) for d in range(2)]
    mk = {key: _V([m[0][key]] * n + [m[1][key]] * n) for key in m[0] if key != "bds"}
    mk["bds"] = [_V([m[0]["bds"][i]] * n + [m[1]["bds"][i]] * n) for i in range(len(m[0]["bds"]))]
    return mk


def _tri_inv(a, mk):
    eyef = mk["eyef"]
    bds = mk["bds"]
    a8 = _vwhere(bds[0], a, 0.0)
    a2 = _vbdot(a8, a8)
    a4 = _vbdot(a2, a2)
    t = _vbdot(_vbdot(eyef - a8, eyef + a2), eyef + a4)
    for inner, outer in zip(bds, bds[1:] + [None]):
        off = _vnot(inner) if outer is None else (outer & _vnot(inner))
        low = _vwhere(off, a, 0.0)
        t = t - _vbdot(_vbdot(t, low), t)
    return t


def _chunk_prep(q, k, v, g_row, b_row, mk, tuw=None):
    dv = GDN_DK
    g_col = _vsum(mk["eyef"] * g_row, axis=1, keepdims=True)
    b_col = _vsum(mk["eyef"] * b_row, axis=1, keepdims=True)
    gc_col = _vsum(mk["mif"] * g_row, axis=1, keepdims=True)
    gc_row = _vsum(mk["mitf"] * g_col, axis=0, keepdims=True)
    gl = _vsum(g_row, axis=1, keepdims=True)
    decay = _vwhere(mk["mi"], _vexp(_vwhere(mk["mi"], gc_col - gc_row, 0.0)), 0.0)
    eg = _vexp(gc_col)
    e2 = _vexp(gl - gc_col)
    egl = _vexp(gl)
    kb = k * b_col
    pm = _vbdot_nt(kb, k)
    if tuw is None:
        t = _tri_inv(_vwhere(mk["ms"], pm * decay, 0.0), mk)
        sol = _vbdot(t, _vcat(v * b_col, kb * eg))
        u, w = _vlo(sol, dv), _vhi(sol, dv)
    else:
        t, u, w = tuw
    qm = _vbdot_nt(q, k)
    return dict(b_col=b_col, decay=decay, eg=eg, e2=e2, egl=egl, kb=kb, pm=pm, t=t, u=u, w=w,
                qm=qm, intra=qm * decay, qd=q * eg, kd=k * e2)


def _chunk_fwd_step(p, state):
    v_new = p["u"] - _vbdot(p["w"], state)
    o = _vbdot(p["qd"], state) + _vbdot(p["intra"], v_new)
    new_state = state * p["egl"] + _vbdot_tn(p["kd"], v_new)
    return o, new_state


def _chunk_bwd_step(q, k, v, p, mk, state, dso, do):
    dv_dim = GDN_DK
    v_new = p["u"] - _vbdot(p["w"], state)
    dvn = _vbdot_tn(p["intra"], do) + _vbdot(p["kd"], dso)
    dintra = _vbdot_nt(do, v_new)
    dqd = _vbdot_nt(do, state)
    ds = p["egl"] * dso + _vbdot_tn(p["qd"], do) - _vbdot_tn(p["w"], dvn)
    dkd = _vbdot_nt(v_new, dso)
    dgl = _vsum(_vsum(dso * state, axis=1, keepdims=True), axis=0, keepdims=True) * p["egl"]
    dw = -_vbdot_nt(dvn, state)
    drhs = _vbdot_tn(p["t"], _vcat(dvn, dw))
    dru, drw = _vlo(drhs, dv_dim), _vhi(drhs, dv_dim)
    da = -_vwhere(mk["ms"], _vbdot_nt(drhs, _vcat(p["u"], p["w"])), 0.0)
    b_col = p["b_col"]
    dv = dru * b_col
    dbeta = _vsum(dru * v, axis=1, keepdims=True)
    dkb = drw * p["eg"]
    deg = _vsum(drw * p["kb"], axis=1, keepdims=True)
    dp = da * p["decay"]
    ddecay = da * p["pm"]
    dkb = dkb + _vbdot(dp, k)
    dk = _vbdot_tn(dp, p["kb"])
    dqm = dintra * p["decay"]
    ddecay = ddecay + dintra * p["qm"]
    dq = _vbdot(dqm, k)
    dk = dk + _vbdot_tn(dqm, q)
    dd = ddecay * p["decay"]
    dgc_col = _vsum(dd, axis=1, keepdims=True)
    dgc_row = -_vsum(dd, axis=0, keepdims=True)
    dq = dq + dqd * p["eg"]
    deg = deg + _vsum(dqd * q, axis=1, keepdims=True)
    dk = dk + dkd * p["e2"]
    de2 = _vsum(dkd * k, axis=1, keepdims=True) * p["e2"]
    dgl = dgl + _vsum(de2, axis=0, keepdims=True)
    dgc_col = dgc_col - de2 + deg * p["eg"]
    dk = dk + dkb * b_col
    dbeta = dbeta + _vsum(dkb * k, axis=1, keepdims=True)
    dgc_col = dgc_col + _vsum(mk["eyef"] * dgc_row, axis=1, keepdims=True)
    dg_row = _vsum(mk["mif"] * dgc_col, axis=0, keepdims=True) + dgl
    dbeta_row = _vsum(mk["eyef"] * dbeta, axis=0, keepdims=True)
    return dq, dk, dv, dg_row, dbeta_row, ds


def _gdn_chunk_fwd(qkvn, g5, b5, *, name, comm=None):
    s = qkvn.shape[0]
    c = GDN_CHUNK
    nc = s // c
    h_, dk = GDN_HEADS, GDN_DK

    def body(*refs):
        n = pl.program_id(0)
        ins, outs, (st_scr,) = _comm_hooks(comm, refs, 6, 10, 1, n == 0, n == (3 * nc) // 4, n == nc - 1)
        x_refs, g_refs, b_refs = ins[0:2], ins[2:4], ins[4:6]
        o_refs, st_refs, t_refs, u_refs, w_refs = outs[0:2], outs[2:4], outs[4:6], outs[6:8], outs[8:10]

        @pl.when(n == 0)
        def _():
            st_scr[...] = jnp.zeros_like(st_scr)

        ch = [(d, h) for d in range(2) for h in range(h_)]
        mk = _both_masks(h_)
        q, k, v = (_V(x_refs[d][:, (t * h_ + h) * dk:(t * h_ + h + 1) * dk] for d, h in ch) for t in range(3))
        g, b = (_V(r[d][0, h, 0] for d, h in ch) for r in (g_refs, b_refs))
        state = _V(st_scr[d * h_ + h] for d, h in ch)
        p = _chunk_prep(q, k, v, g, b, mk)
        o, new_state = _chunk_fwd_step(p, state)
        for i, (d, h) in enumerate(ch):
            st_refs[d][h, 0] = state.xs[i]
            st_scr[d * h_ + h] = new_state.xs[i]
            o_refs[d][:, h * dk:(h + 1) * dk] = o.xs[i]
            t_refs[d][h, 0] = p["t"].xs[i].astype(BF16)
            u_refs[d][h, 0] = p["u"].xs[i]
            w_refs[d][h, 0] = p["w"].xs[i].astype(BF16)

    ce = (lambda n: n, lambda n: nc - 1 - n)
    xs = [pl.BlockSpec((c, 3 * h_ * dk), lambda n, d=d: (ce[d](n), 0)) for d in range(2)]
    gates = [pl.BlockSpec((1, h_, 1, 1, c), lambda n, d=d: (d, 0, ce[d](n), 0, 0)) for d in range(2)]
    os_ = [pl.BlockSpec((c, h_ * dk), lambda n, d=d: (ce[d](n), 0)) for d in range(2)]
    sts = [pl.BlockSpec((h_, 1, dk, dk), lambda n, d=d: (0, ce[d](n), 0, 0)) for d in range(2)]
    tcc = [pl.BlockSpec((h_, 1, c, c), lambda n, d=d: (0, ce[d](n), 0, 0)) for d in range(2)]
    tck = [pl.BlockSpec((h_, 1, c, dk), lambda n, d=d: (0, ce[d](n), 0, 0)) for d in range(2)]
    per_chunk = lambda last, dt: [jax.ShapeDtypeStruct((h_, nc, c, last), dt)] * 2
    c_in, c_out, c_shape, c_scr = _comm_specs(comm)
    res = pl.pallas_call(
        body, name=name, grid=(nc,),
        in_specs=xs + gates + gates + c_in,
        out_specs=os_ + sts + tcc + tck + tck + c_out,
        out_shape=[jax.ShapeDtypeStruct((s, h_ * dk), F32)] * 2 + [jax.ShapeDtypeStruct((h_, nc, dk, dk), F32)] * 2
        + per_chunk(c, BF16) + per_chunk(dk, F32) + per_chunk(dk, BF16) + c_shape,
        scratch_shapes=[pltpu.VMEM((2 * h_, dk, dk), F32)] + c_scr,
        compiler_params=_cparams(("arbitrary",)),
    )(qkvn, qkvn, g5, g5, b5, b5, *(comm.arrays if comm else []))
    return res[0:2], res[2:10], res[10:]


def _gdn_chunk_bwd(qkvn, g5, b5, states, do, *, name, comm=None):
    s = qkvn.shape[0]
    c = GDN_CHUNK
    nc = s // c
    h_, dk = GDN_HEADS, GDN_DK

    def body(*refs):
        i = pl.program_id(0)
        ins, outs, (ds_scr,) = _comm_hooks(comm, refs, 16, 6, 1, i == 0, i == nc // 2, i == nc - 1)
        x_refs, g_refs, b_refs, st_refs = ins[0:2], ins[2:4], ins[4:6], ins[6:8]
        t_refs, u_refs, w_refs, do_refs = ins[8:10], ins[10:12], ins[12:14], ins[14:16]
        dx_refs, dg_refs, db_refs = outs[0:2], outs[2:4], outs[4:6]

        @pl.when(i == 0)
        def _():
            ds_scr[...] = jnp.zeros_like(ds_scr)

        ch = [(d, h) for d in range(2) for h in range(h_)]
        mk = _both_masks(h_)
        q, k, v = (_V(x_refs[d][:, (t * h_ + h) * dk:(t * h_ + h + 1) * dk] for d, h in ch) for t in range(3))
        g, b = (_V(r[d][0, h, 0] for d, h in ch) for r in (g_refs, b_refs))
        state = _V(st_refs[d][h, 0] for d, h in ch)
        dso = _V(ds_scr[d * h_ + h] for d, h in ch)
        dov = _V(do_refs[d][:, h * dk:(h + 1) * dk] for d, h in ch)
        tuw = tuple(_V(r[d][h, 0] for d, h in ch) for r in (t_refs, u_refs, w_refs))
        res = _chunk_bwd_step(q, k, v, _chunk_prep(q, k, v, g, b, mk, tuw), mk, state, dso, dov)
        for (d, h), (dq, dkk, dvv, dg_r, db_r, ds) in zip(ch, zip(*[r.xs for r in res])):
            ds_scr[d * h_ + h] = ds
            dg_refs[d][h, 0] = dg_r
            db_refs[d][h, 0] = db_r
            for t, val in enumerate((dq, dkk, dvv)):
                dx_refs[d][:, (t * h_ + h) * dk:(t * h_ + h + 1) * dk] = val

    ce = (lambda i: nc - 1 - i, lambda i: i)
    both = lambda mk_spec: [mk_spec(d) for d in range(2)]
    xs = both(lambda d: pl.BlockSpec((c, 3 * h_ * dk), lambda i: (ce[d](i), 0)))
    gates = both(lambda d: pl.BlockSpec((1, h_, 1, 1, c), lambda i: (d, 0, ce[d](i), 0, 0)))
    sts = both(lambda d: pl.BlockSpec((h_, 1, dk, dk), lambda i: (0, ce[d](i), 0, 0)))
    tcc = both(lambda d: pl.BlockSpec((h_, 1, c, c), lambda i: (0, ce[d](i), 0, 0)))
    tck = both(lambda d: pl.BlockSpec((h_, 1, c, dk), lambda i: (0, ce[d](i), 0, 0)))
    dos = both(lambda d: pl.BlockSpec((c, h_ * dk), lambda i: (ce[d](i), 0)))
    gouts = both(lambda d: pl.BlockSpec((h_, 1, 1, c), lambda i: (0, ce[d](i), 0, 0)))
    c_in, c_out, c_shape, c_scr = _comm_specs(comm)
    res = pl.pallas_call(
        body, name=name, grid=(nc,),
        in_specs=xs + gates + gates + sts + tcc + tck + tck + dos + c_in,
        out_specs=xs + gouts + gouts + c_out,
        out_shape=[jax.ShapeDtypeStruct((s, 3 * h_ * dk), F32)] * 2
        + [jax.ShapeDtypeStruct((h_, nc, 1, c), F32)] * 4 + c_shape,
        scratch_shapes=[pltpu.VMEM((2 * h_, dk, dk), F32)] + c_scr,
        compiler_params=_cparams(("arbitrary",)),
    )(qkvn, qkvn, g5, g5, b5, b5, *states, do, do, *(comm.arrays if comm else []))
    return res[0:2], jnp.stack(res[2:4]), jnp.stack(res[4:6]), res[6:]


def _gdn_post_fwd(o, z, norm_w, *, name):
    s = o[0].shape[0]
    h_, dk = GDN_HEADS, GDN_DK

    def body(of_ref, ob_ref, z_ref, w_ref, a_ref):
        ov = of_ref[...] + ob_ref[...]
        zv = z_ref[...]
        r = lax.rsqrt(jnp.mean(ov * ov, axis=-1, keepdims=True) + RMS_EPS)
        a_ref[...] = (ov * r * w_ref[...] * (zv * _sigmoid(zv))).astype(a_ref.dtype)

    col = pl.BlockSpec((s, dk), lambda h: (0, h))
    return pl.pallas_call(
        body, name=name, grid=(h_,),
        in_specs=[col, col, pl.BlockSpec((s, dk), lambda h: (0, 3 * h_ + h)), pl.BlockSpec((1, dk), lambda h: (0, 0))],
        out_specs=col,
        out_shape=jax.ShapeDtypeStruct((s, h_ * dk), BF16),
        compiler_params=_cparams(("parallel",)),
    )(o[0], o[1], z, norm_w.reshape(1, dk))


def _gdn_post_bwd(o, z, norm_w, dact, *, name):
    s = o[0].shape[0]
    h_, dk = GDN_HEADS, GDN_DK

    def body(of_ref, ob_ref, z_ref, w_ref, da_ref, do_ref, dz_ref, dw_ref):
        h = pl.program_id(0)
        ov = of_ref[...] + ob_ref[...]
        zv = z_ref[...]
        wv = w_ref[...]
        dav = da_ref[...]
        r = lax.rsqrt(jnp.mean(ov * ov, axis=-1, keepdims=True) + RMS_EPS)
        nrm = ov * r
        sg = _sigmoid(zv)
        sz = zv * sg
        dn = dav * wv * sz
        do_ref[...] = r * (dn - nrm * jnp.mean(dn * nrm, axis=-1, keepdims=True))
        dz_ref[...] = (dav * nrm * wv * (sg * (1.0 + zv * (1.0 - sg)))).astype(dz_ref.dtype)
        part = jnp.sum(dav * nrm * sz, axis=0, keepdims=True)

        @pl.when(h == 0)
        def _():
            dw_ref[...] = part

        @pl.when(h > 0)
        def _():
            dw_ref[...] += part

    col = pl.BlockSpec((s, dk), lambda h: (0, h))
    vec = pl.BlockSpec((1, dk), lambda h: (0, 0))
    return pl.pallas_call(
        body, name=name, grid=(h_,),
        in_specs=[col, col, pl.BlockSpec((s, dk), lambda h: (0, 3 * h_ + h)), vec, col],
        out_specs=[col, pl.BlockSpec((s, dk), lambda h: (0, 3 * h_ + h)), vec],
        out_shape=[jax.ShapeDtypeStruct((s, h_ * dk), F32), jax.ShapeDtypeStruct((s, GDN_MAIN), BF16),
                   jax.ShapeDtypeStruct((1, dk), F32)],
        compiler_params=_cparams(("arbitrary",)),
    )(o[0], o[1], z, norm_w.reshape(1, dk), dact)


def _rel_bucket(rel):
    nb = REL_BUCKETS // 2
    max_exact = nb // 2
    ret = jnp.where(rel > 0, nb, 0)
    n = jnp.abs(rel)
    nf = jnp.maximum(n, 1).astype(F32)
    large = max_exact + (jnp.log(nf / max_exact) / math.log(REL_MAX_DIST / max_exact)
                         * (nb - max_exact)).astype(jnp.int32)
    large = jnp.minimum(large, nb - 1)
    return ret + jnp.where(n < max_exact, n, large)


def _bucket_onehot():
    half = DSWA_HALF
    outs = []
    for dil in DSWA_DILS:
        rel = (jnp.arange(3 * half)[None, :] - half - jnp.arange(half)[:, None]) * dil
        outs.append(jax.nn.one_hot(_rel_bucket(rel).reshape(-1), REL_BUCKETS, dtype=F32, axis=0))
    return jnp.stack(outs)


def _head_group_select(vals):
    rows = lax.broadcasted_iota(jnp.int32, vals[0].shape, 0)
    return jnp.where(rows < DSWA_HG, vals[0], jnp.where(rows < 2 * DSWA_HG, vals[1], vals[2]))


def _dswa_bias(table_t, onehot, *, name):
    p = onehot.shape[-1]

    def body(t_ref, oh_ref, b_ref):
        b_ref[...] = _head_group_select([_hdot(t_ref[...], oh_ref[g]) for g in range(3)])

    return pl.pallas_call(body, name=name, out_shape=jax.ShapeDtypeStruct((DSWA_HEADS, p), F32),
                          compiler_params=_cparams())(table_t, onehot)


def _dswa_dtable(dbias, onehot, *, name):
    def body(d_ref, oh_ref, t_ref):
        t_ref[...] = _head_group_select([_hdot_nt(d_ref[...], oh_ref[g]) for g in range(3)])

    return pl.pallas_call(body, name=name, out_shape=jax.ShapeDtypeStruct((DSWA_HEADS, REL_BUCKETS), F32),
                          compiler_params=_cparams())(dbias, onehot)


def _rows(start, dil):
    if dil == 1:
        return pl.ds(pl.multiple_of(start, DSWA_HALF), DSWA_HALF)
    return pl.ds(start, DSWA_HALF, stride=dil)


def _attn_blocks(it, s, dil):
    half = DSWA_HALF
    nbs = s // half // dil
    ii = lax.broadcasted_iota(jnp.int32, (half, 3 * half), 0)
    jj = lax.broadcasted_iota(jnp.int32, (half, 3 * half), 1)
    band = jnp.abs(jj - half - ii) <= half
    out = []
    for u in range(DSWA_UNROLL):
        blk = it * DSWA_UNROLL + u
        r, b = blk // nbs, blk % nbs
        own = r + dil * half * b
        prev = own - jnp.where(b > 0, dil * half, 0)
        nxt = own + jnp.where(b < nbs - 1, dil * half, 0)
        ok = band & ((jj >= half) | (b > 0)) & ((jj < 2 * half) | (b < nbs - 1))
        out.append(((prev, own, nxt), ok))
    return out


def _attn_chains(q_ref, k_ref, v_ref, blocks, dil):
    lane = lax.broadcasted_iota(jnp.int32, (DSWA_HALF, 2 * DSWA_E), 1)
    qm, kw, vw, valid, hmask = [], [], [], [], []
    for (prev, own, nxt), ok in blocks:
        q = q_ref[_rows(own, dil), :].astype(BF16)
        k = jnp.concatenate([k_ref[_rows(st, dil), :] for st in (prev, own, nxt)], axis=0).astype(BF16)
        v = jnp.concatenate([v_ref[_rows(st, dil), :] for st in (prev, own, nxt)], axis=0).astype(BF16)
        for hd in range(2):
            mine = (lane < DSWA_E) if hd == 0 else (lane >= DSWA_E)
            qm.append(jnp.where(mine, q, jnp.zeros_like(q)))
            kw.append(k)
            vw.append(v)
            valid.append(ok)
            hmask.append(mine)
    return _V(qm), _V(kw), _V(vw), _V(valid), _V(hmask)


def _per_group(pr, fn):
    for gi, dil in enumerate(DSWA_DILS):
        pl.when(pr // DSWA_PG == gi)(functools.partial(fn, dil))


_vmax, _vlog = _lift(jnp.max), _lift(jnp.log)


def _dswa_attn_fwd(qkv, bias, *, name, comm=None):
    s = qkv.shape[0]
    half, e = DSWA_HALF, DSWA_E
    npair = DSWA_HEADS // 2

    def body(*refs):
        pr = pl.program_id(0)
        (q_ref, k_ref, v_ref, bias_ref), (o_ref, lse_ref), _ = _comm_hooks(
            comm, refs, 4, 2, 0, pr == 0, pr == (3 * npair) // 4, pr == npair - 1)
        bias_v = _V([bias_ref[0], bias_ref[1]] * DSWA_UNROLL)

        def run(dil):
            def step(it, carry):
                blocks = _attn_blocks(it, s, dil)
                qm, kw, vw, valid, hmask = _attn_chains(q_ref, k_ref, v_ref, blocks, dil)
                sc = _vwhere(valid, _vbdot_nt(qm, kw) * (e ** -0.5) + bias_v, NEG_INF)
                m = _vmax(sc, axis=-1, keepdims=True)
                p = _vexp(sc - m)
                l = _vsum(p, axis=-1, keepdims=True)
                o = _vbdot(p * (1.0 / l), vw)
                lse = m + _vlog(l)
                for u, ((_, own, _), _) in enumerate(blocks):
                    is_a = hmask.xs[2 * u]
                    o_ref[_rows(own, dil), :] = jnp.where(is_a, o.xs[2 * u], o.xs[2 * u + 1])
                    lse_ref[_rows(own, dil), :] = jnp.where(is_a, lse.xs[2 * u], lse.xs[2 * u + 1])
                return carry

            lax.fori_loop(0, s // half // DSWA_UNROLL, step, 0)

        _per_group(pr, run)

    col = lambda t: pl.BlockSpec((s, 2 * e), lambda p: (0, t * npair + p))
    pair = pl.BlockSpec((s, 2 * e), lambda p: (0, p))
    c_in, c_out, c_shape, c_scr = _comm_specs(comm)
    res = pl.pallas_call(
        body, name=name, grid=(npair,),
        in_specs=[col(0), col(1), col(2), pl.BlockSpec((2, half, 3 * half), lambda p: (p, 0, 0))] + c_in,
        out_specs=[pair, pair] + c_out,
        out_shape=[jax.ShapeDtypeStruct((s, npair * 2 * e), F32)] * 2 + c_shape,
        scratch_shapes=c_scr,
        compiler_params=_cparams(("arbitrary",)),
    )(qkv, qkv, qkv, bias, *(comm.arrays if comm else []))
    return res[0], res[1], res[2:]


def _dswa_attn_bwd(qkv, bias, lse, do, corr, *, name, comm=None):
    s = qkv.shape[0]
    half, e = DSWA_HALF, DSWA_E
    npair = DSWA_HEADS // 2
    w = 2 * e

    def body(*refs):
        pr = pl.program_id(0)
        (q_ref, k_ref, v_ref, bias_ref, lse_ref, do_ref, corr_ref), (dq_ref, dk_ref, dv_ref, db_ref), _ = _comm_hooks(
            comm, refs, 7, 4, 0, pr == 0, pr == npair // 2, pr == npair - 1)
        bias_v = _V([bias_ref[0], bias_ref[1]] * DSWA_UNROLL)
        dk_ref[...] = jnp.zeros_like(dk_ref)
        dv_ref[...] = jnp.zeros_like(dv_ref)

        def run(dil):
            def step(it, dbias):
                blocks = _attn_blocks(it, s, dil)
                qm, kw, vw, valid, hmask = _attn_chains(q_ref, k_ref, v_ref, blocks, dil)
                hd = [0, 1] * DSWA_UNROLL
                rows = [_rows(own, dil) for (_, own, _), _ in blocks for _ in range(2)]
                lse_c = _V(lse_ref[rw, :][:, h * e:h * e + 1] for rw, h in zip(rows, hd))
                corr_c = _V(corr_ref[rw, :][:, h * e:h * e + 1] for rw, h in zip(rows, hd))
                dov = _vwhere(hmask, _V(do_ref[rw, :] for rw in rows), 0.0)
                sc = _vbdot_nt(qm, kw) * (e ** -0.5) + bias_v
                p = _vwhere(valid, _vexp(_vwhere(valid, sc, 0.0) - lse_c), 0.0)
                dsc = p * (_vbdot_nt(dov, vw) + corr_c)
                dq = _vbdot(dsc, kw) * (e ** -0.5)
                dkc = _vbdot_tn(dsc, qm) * (e ** -0.5)
                dvc = _vbdot_tn(p, dov)
                for u, (starts, _) in enumerate(blocks):
                    dq_ref[_rows(starts[1], dil), :] = jnp.where(hmask.xs[2 * u], dq.xs[2 * u], dq.xs[2 * u + 1])
                    dk_u = dkc.xs[2 * u] + dkc.xs[2 * u + 1]
                    dv_u = dvc.xs[2 * u] + dvc.xs[2 * u + 1]
                    for t, st in enumerate(starts):
                        dk_ref[_rows(st, dil), :] += dk_u[t * half:(t + 1) * half]
                        dv_ref[_rows(st, dil), :] += dv_u[t * half:(t + 1) * half]
                da, db = dbias
                for u in range(DSWA_UNROLL):
                    da, db = da + dsc.xs[2 * u], db + dsc.xs[2 * u + 1]
                return da, db

            zero = jnp.zeros((half, 3 * half), F32)
            da, db = lax.fori_loop(0, s // half // DSWA_UNROLL, step, (zero, zero))
            db_ref[0] = da
            db_ref[1] = db

        _per_group(pr, run)

    col = lambda t: pl.BlockSpec((s, w), lambda p: (0, t * npair + p))
    ps = pl.BlockSpec((s, w), lambda p: (0, p))
    bs = pl.BlockSpec((2, half, 3 * half), lambda p: (p, 0, 0))
    c_in, c_out, c_shape, c_scr = _comm_specs(comm)
    res = pl.pallas_call(
        body, name=name, grid=(npair,),
        in_specs=[col(0), col(1), col(2), bs, ps, ps, ps] + c_in,
        out_specs=[ps, ps, ps, bs] + c_out,
        out_shape=[jax.ShapeDtypeStruct((s, npair * w), F32)] * 3
        + [jax.ShapeDtypeStruct((DSWA_HEADS, half, 3 * half), F32)] + c_shape,
        scratch_shapes=c_scr,
        compiler_params=_cparams(("arbitrary",)),
    )(qkv, qkv, qkv, bias, lse, do, corr, *(comm.arrays if comm else []))
    return res[0], res[1], res[2], res[3], res[4:]


def _pair_cols(g, j):
    w = 2 * DSWA_E
    return slice((g * DSWA_PG + j) * w, (g * DSWA_PG + j + 1) * w)


def _group_weights(l_ref, j):
    ls = [l_ref[:, _pair_cols(g, j)] for g in range(3)]
    m = jnp.maximum(jnp.maximum(ls[0], ls[1]), ls[2])
    es = [jnp.exp(x - m) for x in ls]
    inv = 1.0 / (es[0] + es[1] + es[2])
    return [x * inv for x in es]


def _dswa_combine_fwd(o, lse, *, name):
    s, wd = o.shape
    tr = _pick(s, (512, 256, 128))

    def body(o_ref, l_ref, c_ref):
        for j in range(DSWA_PG):
            al = _group_weights(l_ref, j)
            for g in range(3):
                c_ref[:, _pair_cols(g, j)] = (o_ref[:, _pair_cols(g, j)] * al[g]).astype(c_ref.dtype)

    row = pl.BlockSpec((tr, wd), lambda i: (i, 0))
    return pl.pallas_call(
        body, name=name, grid=(s // tr,),
        in_specs=[row, row], out_specs=row,
        out_shape=jax.ShapeDtypeStruct(o.shape, BF16),
        compiler_params=_cparams(("parallel",)),
    )(o, lse)


def _dswa_combine_bwd(o, lse, dc, *, name):
    s, wd = o.shape
    tr = _pick(s, (512, 256, 128))

    def body(o_ref, l_ref, dc_ref, do_ref, corr_ref):
        lane = lax.broadcasted_iota(jnp.int32, (tr, 2 * DSWA_E), 1)
        is_a = lane < DSWA_E
        for j in range(DSWA_PG):
            al = _group_weights(l_ref, j)
            tot = jnp.zeros((tr, 2 * DSWA_E), F32)
            for g in range(3):
                cols = _pair_cols(g, j)
                dcv = dc_ref[:, cols]
                do_ref[:, cols] = dcv * al[g]
                prod = dcv * o_ref[:, cols]
                dal = jnp.where(is_a, jnp.sum(jnp.where(is_a, prod, 0.0), axis=-1, keepdims=True),
                                jnp.sum(jnp.where(is_a, 0.0, prod), axis=-1, keepdims=True))
                tot = tot + al[g] * dal
            for g in range(3):
                corr_ref[:, _pair_cols(g, j)] = -al[g] * tot

    row = pl.BlockSpec((tr, wd), lambda i: (i, 0))
    return pl.pallas_call(
        body, name=name, grid=(s // tr,),
        in_specs=[row, row, row], out_specs=[row, row],
        out_shape=[jax.ShapeDtypeStruct(o.shape, F32)] * 2,
        compiler_params=_cparams(("parallel",)),
    )(o, lse, dc)


class _Comm:
    def __init__(self, mode, arrays, kinds=None):
        self.mode, self.arrays, self.kinds = mode, list(arrays), kinds
        self.n = len(self.arrays)

    def out_shapes(self):
        if self.mode == "exchange":
            return [jax.ShapeDtypeStruct(x.shape, x.dtype) for x in self.arrays]
        shapes = []
        for x, kd in zip(self.arrays, self.kinds):
            shp = list(x.shape)
            if kd == "stack":
                shp = [N_DEV] + shp
            else:
                shp[-2 if kd == "rows" else -1] *= N_DEV
            shapes.append(jax.ShapeDtypeStruct(tuple(shp), x.dtype))
        return shapes

    def scratch(self):
        return [pltpu.SemaphoreType.DMA((7 * self.n,)), pltpu.SemaphoreType.DMA((7 * self.n,)),
                pltpu.SemaphoreType.DMA((self.n,))]

    def bind(self, in_refs, out_refs, sems):
        self.x, self.o = in_refs, out_refs
        self.send_sems, self.recv_sems, self.local_sems = sems
        self.pos = (lax.axis_index("x"), lax.axis_index("y"), lax.axis_index("c"))

    def _slot(self, i, px, py, pc):
        p = 4 * px + 2 * py + pc
        kd = self.kinds[i]
        if kd == "stack":
            return self.o[i].at[p]
        nd = len(self.x[i].shape)
        ax = nd - 2 if kd == "rows" else nd - 1
        size = self.x[i].shape[ax]
        idx = tuple(pl.ds(p * size, size) if a == ax else slice(None) for a in range(nd))
        return self.o[i].at[idx]

    def _gcopy(self, i, k, block, to, src=None):
        return pltpu.make_async_remote_copy(
            src_ref=self._slot(i, *block) if src is None else src, dst_ref=self._slot(i, *block),
            send_sem=self.send_sems.at[7 * i + k], recv_sem=self.recv_sems.at[7 * i + k],
            device_id=to, device_id_type=pl.DeviceIdType.MESH)

    def _chips(self):
        mx, my, _ = self.pos
        return [(1 - mx, my), (mx, 1 - my), (1 - mx, 1 - my)]

    def _xcopies(self):
        mx, my, mc = self.pos
        me = 4 * mx + 2 * my + mc
        copies = []
        for k in range(1, N_DEV):
            px = 1 - mx if (k >> 2) & 1 else mx
            py = 1 - my if (k >> 1) & 1 else my
            pc = 1 - mc if k & 1 else mc
            for i in range(self.n):
                copies.append(pltpu.make_async_remote_copy(
                    src_ref=self.x[i].at[4 * px + 2 * py + pc], dst_ref=self.o[i].at[me],
                    send_sem=self.send_sems.at[7 * i + k - 1], recv_sem=self.recv_sems.at[7 * i + k - 1],
                    device_id=(px, py, pc), device_id_type=pl.DeviceIdType.MESH))
        return copies

    def _local(self):
        mx, my, mc = self.pos
        if self.mode == "exchange":
            me = 4 * mx + 2 * my + mc
            return [pltpu.make_async_copy(self.x[i].at[me], self.o[i].at[me], self.local_sems.at[i]) for i in range(self.n)]
        return [pltpu.make_async_copy(self.x[i], self._slot(i, mx, my, mc), self.local_sems.at[i]) for i in range(self.n)]

    def _first(self):
        mx, my, mc = self.pos
        me, sibling = (mx, my, mc), (mx, my, 1 - mc)
        first = [self._gcopy(i, 0, me, sibling, src=self.x[i]) for i in range(self.n)]
        first += [self._gcopy(i, 1 + j, me, (*chip, mc), src=self.x[i]) for j, chip in enumerate(self._chips())
                  for i in range(self.n)]
        return first

    def _passed(self):
        mx, my, mc = self.pos
        return [self._gcopy(i, 4 + j, (*chip, mc), (mx, my, 1 - mc)) for j, chip in enumerate(self._chips())
                for i in range(self.n)]

    def start(self):
        for cp in self._local() + (self._xcopies() if self.mode == "exchange" else self._first()):
            cp.start()

    def mid(self):
        if self.mode == "exchange":
            return
        mx, my, mc = self.pos
        passed = self._passed()
        for j, chip in enumerate(self._chips()):
            for i in range(self.n):
                self._gcopy(i, 1 + j, (*chip, mc), (mx, my, mc)).wait_recv()
                passed[j * self.n + i].start()

    def end(self):
        mx, my, mc = self.pos
        if self.mode == "exchange":
            copies = self._xcopies()
            for cp in copies:
                cp.wait_recv()
            for cp in copies:
                cp.wait_send()
        else:
            for i in range(self.n):
                self._gcopy(i, 0, (mx, my, 1 - mc), (mx, my, mc)).wait_recv()
                for j, chip in enumerate(self._chips()):
                    self._gcopy(i, 4 + j, (*chip, 1 - mc), (mx, my, mc)).wait_recv()
            for cp in self._first() + self._passed():
                cp.wait_send()
        for cp in self._local():
            cp.wait()

    def run(self, *, name):
        n = self.n

        def body(*refs):
            self.bind(refs[:n], refs[n:2 * n], refs[2 * n:])
            self.start()
            self.mid()
            self.end()

        anyspec = pl.BlockSpec(memory_space=pl.ANY)
        return pl.pallas_call(body, name=name, in_specs=[anyspec] * n, out_specs=[anyspec] * n,
                              out_shape=self.out_shapes(), scratch_shapes=self.scratch())(*self.arrays)


def _comm_specs(comm):
    if comm is None:
        return [], [], [], []
    anyspec = pl.BlockSpec(memory_space=pl.ANY)
    return [anyspec] * comm.n, [anyspec] * comm.n, comm.out_shapes(), comm.scratch()


def _comm_hooks(comm, refs, n_in, n_out, n_scr, first, mid, last):
    if comm is None:
        return refs[:n_in], refs[n_in:n_in + n_out], refs[n_in + n_out:]
    c = comm.n
    ins, cin = refs[:n_in], refs[n_in:n_in + c]
    outs, cout = refs[n_in + c:n_in + c + n_out], refs[n_in + c + n_out:n_in + 2 * c + n_out]
    scr, sems = refs[n_in + 2 * c + n_out:n_in + 2 * c + n_out + n_scr], refs[n_in + 2 * c + n_out + n_scr:]
    comm.bind(cin, cout, sems)
    pl.when(first)(comm.start)
    pl.when(mid)(comm.mid)
    pl.when(last)(comm.end)
    return ins, outs, scr


def _adamw_update(g, w, m, v):
    mn = ADAM_B1 * m + (1.0 - ADAM_B1) * g
    vn = ADAM_B2 * v + (1.0 - ADAM_B2) * (g * g)
    m_hat = mn / (1.0 - ADAM_B1 ** ADAM_STEP)
    v_hat = vn / (1.0 - ADAM_B2 ** ADAM_STEP)
    return -ADAM_LR * (m_hat / (jnp.sqrt(v_hat) + ADAM_EPS) + ADAM_WD * w), mn, vn


def _adamw_layers(recvs, w, m, v, *, name):
    nl, ks, ns = w.shape
    tr = _pick(ks, (128, 64, 48))

    def body(*refs):
        rv_refs = refs[:nl]
        w_ref, m_ref, v_ref, g_ref, d_ref, nm_ref, nv_ref = refs[nl:]
        for l in range(nl):
            g = rv_refs[l][0].astype(F32)
            for q in range(1, N_DEV):
                g = g + rv_refs[l][q].astype(F32)
            delta, mn, vn = _adamw_update(g, w_ref[l], m_ref[l], v_ref[l])
            g_ref[l] = g
            d_ref[l] = delta
            nm_ref[l] = mn
            nv_ref[l] = vn

    row = pl.BlockSpec((nl, tr, ns), lambda i: (0, i, 0))
    return pl.pallas_call(
        body, name=name, grid=(ks // tr,),
        in_specs=[pl.BlockSpec((N_DEV, tr, ns), lambda i: (0, i, 0))] * nl + [row] * 3,
        out_specs=[row] * 4,
        out_shape=[jax.ShapeDtypeStruct((nl, ks, ns), F32)] * 4,
        compiler_params=_cparams(("parallel",)),
    )(*recvs, w, m, v)


def _adamw_reduce(recv, w, m, v, *, name):
    r, c = w.shape
    tr = _pick(r, (128, 64, 8))

    def body(rv_ref, w_ref, m_ref, v_ref, g_ref, d_ref, nm_ref, nv_ref):
        g = rv_ref[0]
        for q in range(1, N_DEV):
            g = g + rv_ref[q]
        delta, mn, vn = _adamw_update(g, w_ref[...], m_ref[...], v_ref[...])
        g_ref[...] = g
        d_ref[...] = delta
        nm_ref[...] = mn
        nv_ref[...] = vn

    row = pl.BlockSpec((tr, c), lambda i: (i, 0))
    return pl.pallas_call(
        body, name=name, grid=(r // tr,),
        in_specs=[pl.BlockSpec((N_DEV, tr, c), lambda i: (0, i, 0)), row, row, row],
        out_specs=[row] * 4,
        out_shape=[jax.ShapeDtypeStruct((r, c), F32)] * 4,
        compiler_params=_cparams(("parallel",)),
    )(recv, w, m, v)


_BIG = ("gdn_w_in", "gdn_w_out", "dswa_w_in", "dswa_w_out", "mlp_w1", "mlp_w2")
_SMALL = ("gdn_conv_w", "norm_mix", "norm_mlp", "norm_final", "rel_bias", "gdn_a_log", "gdn_dt_bias", "gdn_norm_w")
_ORDER = ("norm_mix", "norm_mlp", "norm_final", "rel_bias", "gdn_w_in", "gdn_conv_w", "gdn_a_log", "gdn_dt_bias",
          "gdn_norm_w", "gdn_w_out", "dswa_w_in", "dswa_w_out", "mlp_w1", "mlp_w2")
_KIND = dict(gdn_w_in="stack", gdn_w_out="rows", dswa_w_in="stack", dswa_w_out="rows", mlp_w1="cols", mlp_w2="rows")


def _pack_rows(arrs, align):
    rows, counts = [], []
    for a in arrs:
        flat = a.reshape(-1)
        n = -(-flat.shape[0] // D_MODEL)
        flat = jnp.pad(flat, (0, n * D_MODEL - flat.shape[0]))
        rows.append(flat.reshape(n, D_MODEL))
        counts.append(n)
    out = jnp.concatenate(rows, axis=0)
    total = -(-out.shape[0] // align) * align
    return jnp.pad(out, ((0, total - out.shape[0]), (0, 0))), counts


def _unpack_rows(slab, shapes):
    outs, r = [], 0
    for shp in shapes:
        size = int(np.prod(shp))
        n = -(-size // D_MODEL)
        outs.append(slab[r:r + n].reshape(-1)[:size].reshape(shp))
        r += n
    return outs


def _col_shards(full, nshard):
    lead = full.shape[:-1]
    n = full.shape[-1] // nshard
    t = full.reshape(lead + (nshard, n))
    return jnp.moveaxis(t, -2, 0)


def _from_col_shards(g):
    t = jnp.moveaxis(g, 0, -2)
    return t.reshape(t.shape[:-2] + (t.shape[-2] * t.shape[-1],))


def kernel(x, norm_mix, norm_mlp, norm_final, rel_bias, gdn_w_in, gdn_conv_w, gdn_a_log, gdn_dt_bias, gdn_norm_w, gdn_w_out, dswa_w_in, dswa_w_out, mlp_w1, mlp_w2, loss_target, m_norm_mix, m_norm_mlp, m_norm_final, m_rel_bias, m_gdn_w_in, m_gdn_conv_w, m_gdn_a_log, m_gdn_dt_bias, m_gdn_norm_w, m_gdn_w_out, m_dswa_w_in, m_dswa_w_out, m_mlp_w1, m_mlp_w2, v_norm_mix, v_norm_mlp, v_norm_final, v_rel_bias, v_gdn_w_in, v_gdn_conv_w, v_gdn_a_log, v_gdn_dt_bias, v_gdn_norm_w, v_gdn_w_out, v_dswa_w_in, v_dswa_w_out, v_mlp_w1, v_mlp_w2):
    params = dict(norm_mix=norm_mix, norm_mlp=norm_mlp, norm_final=norm_final, rel_bias=rel_bias,
                  gdn_w_in=gdn_w_in, gdn_conv_w=gdn_conv_w, gdn_a_log=gdn_a_log, gdn_dt_bias=gdn_dt_bias,
                  gdn_norm_w=gdn_norm_w, gdn_w_out=gdn_w_out, dswa_w_in=dswa_w_in, dswa_w_out=dswa_w_out,
                  mlp_w1=mlp_w1, mlp_w2=mlp_w2)
    mom_m = dict(norm_mix=m_norm_mix, norm_mlp=m_norm_mlp, norm_final=m_norm_final, rel_bias=m_rel_bias,
                 gdn_w_in=m_gdn_w_in, gdn_conv_w=m_gdn_conv_w, gdn_a_log=m_gdn_a_log, gdn_dt_bias=m_gdn_dt_bias,
                 gdn_norm_w=m_gdn_norm_w, gdn_w_out=m_gdn_w_out, dswa_w_in=m_dswa_w_in, dswa_w_out=m_dswa_w_out,
                 mlp_w1=m_mlp_w1, mlp_w2=m_mlp_w2)
    mom_v = dict(norm_mix=v_norm_mix, norm_mlp=v_norm_mlp, norm_final=v_norm_final, rel_bias=v_rel_bias,
                 gdn_w_in=v_gdn_w_in, gdn_conv_w=v_gdn_conv_w, gdn_a_log=v_gdn_a_log, gdn_dt_bias=v_gdn_dt_bias,
                 gdn_norm_w=v_gdn_norm_w, gdn_w_out=v_gdn_w_out, dswa_w_in=v_dswa_w_in, dswa_w_out=v_dswa_w_out,
                 mlp_w1=v_mlp_w1, mlp_w2=v_mlp_w2)
    xs = x[0]
    target = loss_target[0]
    dist = _Dist(params)
    conv_tail, _ = _pack_rows([gdn_conv_w], 8)
    (conv_g,) = dist.put("start", dist.gather_comm("start", extra=[(conv_tail, "stack")]).run(name="ag_start"))
    conv_parts = [_unpack_rows(conv_g[dev], [gdn_conv_w.shape])[0] for dev in range(N_DEV)]
    conv_full = _from_col_shards(jnp.stack(conv_parts))[:, :, 0, :]

    loss_part, dcur, g_big, rep, g_conv = _local_step(
        xs, target, dict(norm_mix=norm_mix, norm_mlp=norm_mlp, norm_final=norm_final, rel_bias=rel_bias,
                         gdn_a_log=gdn_a_log, gdn_dt_bias=gdn_dt_bias, gdn_norm_w=gdn_norm_w), dist.full, conv_full, dist)
    loss = lax.psum(loss_part[0, 0], ("x", "y", "c"))
    grad_x = dcur[None]

    conv_dev = _col_shards(jnp.stack(g_conv)[:, :, None, :], N_DEV)
    small_send = jnp.stack([_pack_rows([conv_dev[dev]] + [rep[n] for n in _SMALL[1:]], 8)[0] for dev in range(N_DEV)])
    (small_recv,) = dist.got("end", dist.send_comm("end", g_big, extra=[small_send]).run(name="grad_exchange"))

    outs = {}
    for n in _BIG:
        recvs = [dist.recv[(n, l)] for l in range(params[n].shape[0])]
        res = _adamw_layers(recvs, params[n], mom_m[n], mom_v[n], name=f"adamw_{n}")
        for tag, t in zip(("grad", "delta", "new_m", "new_v"), res):
            outs[(tag, n)] = t
    w_slab, _ = _pack_rows([params[n] for n in _SMALL], 8)
    m_slab, _ = _pack_rows([mom_m[n] for n in _SMALL], 8)
    v_slab, _ = _pack_rows([mom_v[n] for n in _SMALL], 8)
    small = _adamw_reduce(small_recv, w_slab, m_slab, v_slab, name="adamw_small")
    shapes = [params[n].shape for n in _SMALL]
    for tag, slab in zip(("grad", "delta", "new_m", "new_v"), small):
        for n, t in zip(_SMALL, _unpack_rows(slab, shapes)):
            outs[(tag, n)] = t
    result = [loss, grad_x]
    for tag in ("grad", "delta", "new_m", "new_v"):
        result += [outs[(tag, n)] for n in _ORDER]
    return tuple(result)


_GATHER = {
    "start": (("gdn_w_in", 0),),
    "gdn_proj0": (("gdn_w_out", 0), ("mlp_w1", 0)),
    "chunk_fwd0": (("mlp_w2", 0), ("dswa_w_in", 0), ("dswa_w_out", 0), ("mlp_w1", 1)),
    "mlp_up0": (("mlp_w2", 1),),
    "mlp_down0": (("gdn_w_in", 1),),
    "attn_fwd1": (("mlp_w1", 2),),
    "mlp_up1": (("mlp_w2", 2),),
    "mlp_down1": (("gdn_w_out", 1),),
    "chunk_fwd2": (("dswa_w_in", 1), ("dswa_w_out", 1), ("mlp_w1", 3), ("mlp_w2", 3)),
}
_SEND = {
    "attn_bwd3": (("mlp_w1", 3), ("mlp_w2", 3)),
    "chunk_bwd2": (("dswa_w_in", 1), ("dswa_w_out", 1), ("mlp_w2", 2)),
    "pre_bwd2": (("mlp_w1", 2), ("gdn_w_out", 1)),
    "gdn_proj_bwd2": (("gdn_w_in", 1),),
    "attn_bwd1": (("mlp_w1", 1), ("mlp_w2", 1)),
    "chunk_bwd0": (("dswa_w_in", 0), ("dswa_w_out", 0), ("mlp_w2", 0)),
    "pre_bwd0": (("mlp_w1", 0), ("gdn_w_out", 0)),
    "gdn_proj_bwd0": (("gdn_w_in", 0),),
    "end": (),
}


class _Dist:
    def __init__(self, params):
        self.shards = {n: params[n].astype(BF16) for n in _BIG}
        self.full = {n: [None] * params[n].shape[0] for n in _BIG}
        self.recv = {}

    def gather_comm(self, tag, extra=()):
        if tag not in _GATHER:
            return None
        arrays = [self.shards[n][l] for n, l in _GATHER[tag]] + [a for a, _ in extra]
        return _Comm("gather", arrays, [_KIND[n] for n, _ in _GATHER[tag]] + [k for _, k in extra])

    def put(self, tag, outs):
        for (n, l), t in zip(_GATHER.get(tag, ()), outs):
            self.full[n][l] = _from_col_shards(t) if _KIND[n] == "stack" else t
        return outs[len(_GATHER.get(tag, ())):]

    def send_comm(self, tag, g_big, extra=()):
        if tag not in _SEND:
            return None
        arrays = [_col_shards(g_big[n][l], N_DEV) if _KIND[n] == "stack" else g_big[n][l] for n, l in _SEND[tag]]
        return _Comm("exchange", arrays + list(extra))

    def got(self, tag, outs):
        for item, t in zip(_SEND.get(tag, ()), outs):
            self.recv[item] = t
        return outs[len(_SEND.get(tag, ())):]


def _mm_gather(dist, tag, *args, **kw):
    comm = dist and dist.gather_comm(tag)
    if not comm:
        return _mm(*args, **kw)
    res, got = _mm(*args, comm=comm, **kw)
    dist.put(tag, got)
    return res


def _ep_residual_norm(acc, res, g):
    x = acc + res
    r = lax.rsqrt(jnp.mean(x * x, axis=-1, keepdims=True) + RMS_EPS)
    return x, x * r * g


def _ep_rms_bwd(dh, x, dres, g):
    r = lax.rsqrt(jnp.mean(x * x, axis=-1, keepdims=True) + RMS_EPS)
    xn = x * r
    dn = dh * g
    dx = dres + r * (dn - xn * jnp.mean(dn * xn, axis=-1, keepdims=True))
    return dx, dx, jnp.sum(dh * xn, axis=0, keepdims=True)


def _local_step(xs, target, sp, full, conv_full, dist=None):
    s = xs.shape[0]
    norm_mix, norm_mlp, norm_final = sp["norm_mix"], sp["norm_mlp"], sp["norm_final"]
    gdn_a_log, gdn_dt_bias, gdn_norm_w = sp["gdn_a_log"], sp["gdn_dt_bias"], sp["gdn_norm_w"]
    onehot = _bucket_onehot()
    table_t = sp["rel_bias"].T
    bias = _dswa_bias(table_t, onehot, name="dswa_bias").reshape(DSWA_HEADS, DSWA_HALF, 3 * DSWA_HALF)

    saved = []
    cur = xs
    row = lambda v: v.reshape(1, -1)
    h = _rms_fwd(cur, norm_mix[0], name="rms_mix_fwd0")
    for i in range(DEPTH):
        j = i // 2
        sv = dict(x_in=cur, h=h)
        if i % 2 == 0:
            w_in = full["gdn_w_in"][j]
            proj = _mm_gather(dist, f"gdn_proj{i}", h, w_in, b_cols=(0, GDN_MAIN), name=f"gdn_proj{i}")
            ab = _mm(h, w_in[:, GDN_MAIN:], name=f"gdn_proj_ab{i}")
            qkvn = _gdn_pre_fwd(proj, conv_full[j], name=f"gdn_pre_fwd{i}")
            g_all, beta_all = _gdn_gate_fwd(ab[:, :2 * GDN_HEADS], ab[:, 2 * GDN_HEADS:], gdn_a_log[j], gdn_dt_bias[j],
                                            name=f"gdn_gate_fwd{i}")
            gshape = (2, GDN_HEADS, s // GDN_CHUNK, 1, GDN_CHUNK)
            g_row = g_all.T.reshape(gshape)
            b_row = beta_all.T.reshape(gshape)
            o, states, got = _gdn_chunk_fwd(qkvn, g_row, b_row, name=f"gdn_chunk_fwd{i}",
                                            comm=dist and dist.gather_comm(f"chunk_fwd{i}"))
            if dist:
                dist.put(f"chunk_fwd{i}", got)
            act = _gdn_post_fwd(o, proj, gdn_norm_w[j], name=f"gdn_post_fwd{i}")
            sv.update(proj=proj, ab=ab, qkvn=qkvn, g_row=g_row, b_row=b_row, o=o, states=states, act=act)
            w_out = full["gdn_w_out"][j]
        else:
            w_in = full["dswa_w_in"][j]
            qkv = _mm(h, w_in, name=f"dswa_proj{i}")
            o_n, lse_n, got = _dswa_attn_fwd(qkv, bias, name=f"dswa_attn_fwd{i}",
                                             comm=dist and dist.gather_comm(f"attn_fwd{i}"))
            if dist:
                dist.put(f"attn_fwd{i}", got)
            act = _dswa_combine_fwd(o_n, lse_n, name=f"dswa_comb_fwd{i}")
            sv.update(qkv=qkv, o_n=o_n, lse_n=lse_n, act=act)
            w_out = full["dswa_w_out"][j]
        cur, h2 = _mm(act, w_out, name=f"mix_out{i}", out_dtypes=(F32, BF16), epilogue=_ep_residual_norm,
                      extras=(cur,), vecs=(row(norm_mlp[i]),))
        sv["x_mid"] = cur
        u, a = _mm_gather(dist, f"mlp_up{i}", h2, full["mlp_w1"][i], name=f"mlp_up{i}", out_dtypes=(BF16, BF16),
                          epilogue=lambda acc: (acc, jnp.square(jnp.maximum(acc, 0.0))))
        if i + 1 < DEPTH:
            cur, h = _mm_gather(dist, f"mlp_down{i}", a, full["mlp_w2"][i], name=f"mlp_down{i}", out_dtypes=(F32, BF16), tk=2048,
                                epilogue=_ep_residual_norm, extras=(cur,), vecs=(row(norm_mix[i + 1]),))
        else:
            cur = _mm_gather(dist, f"mlp_down{i}", a, full["mlp_w2"][i], name=f"mlp_down{i}", tk=2048,
                             epilogue=lambda acc, r: (acc + r,), extras=(cur,))
        sv.update(h2=h2, u=u, a=a)
        saved.append(sv)

    loss_part, dcur, dcur_b, dg_final = _loss_head(cur, norm_final, target, name="loss_head")

    g_norm_mix, g_norm_mlp = [None] * DEPTH, [None] * DEPTH
    g_big = {n: [None] * len(full[n]) for n in _BIG}
    g_conv, g_alog, g_dt, g_nw = [None] * 2, [None] * 2, [None] * 2, [None] * 2
    d_table_t = jnp.zeros((DSWA_HEADS, REL_BUCKETS), F32)
    for i in reversed(range(DEPTH)):
        j = i // 2
        sv = saved[i]
        w1, w2 = full["mlp_w1"][i], full["mlp_w2"][i]
        du = _mm(dcur_b, w2, tb=True, name=f"mlp_down_bwd{i}", out_dtypes=(BF16,),
                 epilogue=lambda acc, uu: (acc * (2.0 * jnp.maximum(uu.astype(F32), 0.0)),), extras=(sv["u"],))
        g_big["mlp_w2"][i] = _mm(sv["a"], dcur_b, ta=True, name=f"mlp_w2_grad{i}", out_dtypes=(BF16,), shard="rows")
        g_big["mlp_w1"][i] = _mm(sv["h2"], du, ta=True, name=f"mlp_w1_grad{i}", out_dtypes=(BF16,), shard="cols")
        dmid, dmid_b, g_norm_mlp[i] = _mm(du, w1, tb=True, name=f"mlp_up_bwd{i}", out_dtypes=(F32, BF16),
                                          epilogue=_ep_rms_bwd, extras=(sv["x_mid"], dcur), vecs=(row(norm_mlp[i]),),
                                          vec_out=True)
        if i % 2 == 0:
            w_in, w_out = full["gdn_w_in"][j], full["gdn_w_out"][j]
            dact = _mm(dmid_b, w_out, tb=True, name=f"mix_out_bwd{i}")
            g_big["gdn_w_out"][j] = _mm(sv["act"], dmid_b, ta=True, name=f"mix_out_grad{i}", out_dtypes=(BF16,),
                                        shard="rows")
            do, dz, g_nw[j] = _gdn_post_bwd(sv["o"], sv["proj"], gdn_norm_w[j], dact, name=f"gdn_post_bwd{i}")
            dqkvn, dg_row, db_row, got = _gdn_chunk_bwd(sv["qkvn"], sv["g_row"], sv["b_row"], sv["states"], do,
                                                        name=f"gdn_chunk_bwd{i}",
                                                        comm=dist and dist.send_comm(f"chunk_bwd{i}", g_big))
            if dist:
                dist.got(f"chunk_bwd{i}", got)
            dproj, g_conv[j], got = _gdn_pre_bwd(sv["proj"], conv_full[j], dqkvn, dz, name=f"gdn_pre_bwd{i}",
                                                 comm=dist and dist.send_comm(f"pre_bwd{i}", g_big))
            if dist:
                dist.got(f"pre_bwd{i}", got)
            nh2 = 2 * GDN_HEADS
            da_, db_, g_alog[j], g_dt[j] = _gdn_gate_bwd(sv["ab"][:, :nh2], sv["ab"][:, nh2:], gdn_a_log[j], gdn_dt_bias[j],
                                                         dg_row.reshape(nh2, s).T, db_row.reshape(nh2, s).T,
                                                         name=f"gdn_gate_bwd{i}")
            dab = jnp.concatenate([da_, db_], axis=1)
            gw_main = _mm(sv["h"], dproj, ta=True, name=f"gdn_w_in_grad{i}", out_dtypes=(BF16,))
            gw_ab = _mm(sv["h"], dab, ta=True, name=f"gdn_w_ab_grad{i}", out_dtypes=(BF16,))
            g_big["gdn_w_in"][j] = jnp.concatenate([gw_main, gw_ab], axis=1)
            dh_ab = _mm(dab, w_in[:, GDN_MAIN:], tb=True, name=f"gdn_proj_ab_bwd{i}")
            comm = dist and dist.send_comm(f"gdn_proj_bwd{i}", g_big)
            res = _mm(dproj, w_in, b_cols=(0, GDN_MAIN), tb=True, name=f"gdn_proj_bwd{i}", out_dtypes=(F32, BF16), tm=512, tk=2048,
                      epilogue=lambda acc, r, x, dres, g: _ep_rms_bwd(acc + r, x, dres, g),
                      extras=(dh_ab, sv["x_in"], dmid), vecs=(row(norm_mix[i]),), vec_out=True, comm=comm)
            if comm:
                res, got = res
                dist.got(f"gdn_proj_bwd{i}", got)
            dcur, dcur_b, g_norm_mix[i] = res
        else:
            w_in, w_out = full["dswa_w_in"][j], full["dswa_w_out"][j]
            dact = _mm(dmid_b, w_out, tb=True, name=f"mix_out_bwd{i}")
            g_big["dswa_w_out"][j] = _mm(sv["act"], dmid_b, ta=True, name=f"mix_out_grad{i}", out_dtypes=(BF16,),
                                         shard="rows")
            do_n, corr_n = _dswa_combine_bwd(sv["o_n"], sv["lse_n"], dact, name=f"dswa_comb_bwd{i}")
            *dqkv, dbias, got = _dswa_attn_bwd(sv["qkv"], bias, sv["lse_n"], do_n, corr_n, name=f"dswa_attn_bwd{i}",
                                               comm=dist and dist.send_comm(f"attn_bwd{i}", g_big))
            if dist:
                dist.got(f"attn_bwd{i}", got)
            d_table_t = d_table_t + _dswa_dtable(dbias.reshape(DSWA_HEADS, -1), onehot, name=f"dswa_dtable{i}")
            g_big["dswa_w_in"][j] = jnp.concatenate(
                [_mm(sv["h"], dt, ta=True, name=f"dswa_w_in_grad{i}_{t}", out_dtypes=(BF16,)) for t, dt in enumerate(dqkv)],
                axis=1)
            cols = [(t * DSWA_WIDTH, DSWA_WIDTH) for t in range(3)]
            dh = _mm(dqkv[0], w_in, b_cols=cols[0], tb=True, name=f"dswa_proj_bwd{i}_0")
            dh = _mm(dqkv[1], w_in, b_cols=cols[1], tb=True, name=f"dswa_proj_bwd{i}_1",
                     epilogue=lambda acc, r: (acc + r,), extras=(dh,))
            dcur, dcur_b, g_norm_mix[i] = _mm(
                dqkv[2], w_in, b_cols=cols[2], tb=True, name=f"dswa_proj_bwd{i}_2", out_dtypes=(F32, BF16), tm=512,
                epilogue=lambda acc, r, x, dres, g: _ep_rms_bwd(acc + r, x, dres, g),
                extras=(dh, sv["x_in"], dmid), vecs=(row(norm_mix[i]),), vec_out=True)

    rep = dict(norm_mix=jnp.concatenate(g_norm_mix, axis=0), norm_mlp=jnp.concatenate(g_norm_mlp, axis=0),
               norm_final=dg_final.reshape(-1), rel_bias=d_table_t.T,
               gdn_a_log=jnp.stack(g_alog).reshape(gdn_a_log.shape), gdn_dt_bias=jnp.stack(g_dt).reshape(gdn_dt_bias.shape),
               gdn_norm_w=jnp.stack(g_nw).reshape(gdn_norm_w.shape))
    return loss_part, dcur, g_big, rep, g_conv
```
